```python
import jax
import jax.numpy as jnp
from jax import lax
import numpy as np

D_MODEL = 1024
BATCH = 8
SEQ = 4096
DEPTH = 4

D_A = D_MODEL // 2
A_HEAD_DIM = 128
A_HEADS = D_A // A_HEAD_DIM
A_CHUNK = 64
D_B = D_MODEL // 4
B_BLOCKS = 4
B_BLOCK_DIM = D_B // B_BLOCKS
CONV_WIDTH = 4
LRU_C = 8.0
D_C = D_MODEL // 4
C_GROUPS = 4
C_GROUP_DIM = D_C // C_GROUPS
C_CHUNK = 128

D_MIX = D_A + D_B + D_C
SPLIT_SIZES = (D_A, D_A, D_A, D_A, D_B, D_B, D_C, D_C)
D_IN = sum(SPLIT_SIZES)
D_FF = ((8 * D_MODEL // 3 + 127) // 128) * 128
EPS = 1e-6

kernel_name = 'hybrid_hgrn2_rglru_sgu_macaron'


def rmsnorm(x, gain):
    x32 = x.astype(jnp.float32)
    y = x32 * lax.rsqrt(jnp.mean(x32 * x32, axis=-1, keepdims=True) + EPS)
    return (y * gain.astype(jnp.float32)).astype(x.dtype)


def group_rmsnorm(x, gain, n_groups):
    shp = x.shape
    x32 = x.astype(jnp.float32).reshape(shp[:-1] + (n_groups, shp[-1] // n_groups))
    y = x32 * lax.rsqrt(jnp.mean(x32 * x32, axis=-1, keepdims=True) + EPS)
    return y.reshape(shp) * gain.astype(jnp.float32)


def swiglu(x, w_gate, w_up, w_down):
    return (jax.nn.silu(x @ w_gate) * (x @ w_up)) @ w_down


def hgrn2(q, f_logit, i, g, lower_bound, norm_gain):
    bsz, seq, _ = q.shape
    n_chunks = seq // A_CHUNK
    f32 = jnp.float32
    q = jax.nn.silu(q.astype(f32))
    lb = lower_bound.astype(f32)
    forget = lb + (1.0 - lb) * jax.nn.sigmoid(f_logit.astype(f32))
    k = 1.0 - forget
    log_f = jnp.log(forget)

    def to_chunks(t):
        return t.reshape(bsz, n_chunks, A_CHUNK, A_HEADS, A_HEAD_DIM).transpose(1, 0, 3, 2, 4)

    qc, kc, vc = to_chunks(q), to_chunks(k), to_chunks(i.astype(f32))
    bc = jnp.cumsum(to_chunks(log_f), axis=3)
    causal = jnp.tril(jnp.ones((A_CHUNK, A_CHUNK), bool))[:, :, None]

    def chunk_step(state, inp):
        q_t, k_t, v_t, b_t = inp
        diff = jnp.where(causal, b_t[:, :, :, None, :] - b_t[:, :, None, :, :], -jnp.inf)
        scores = jnp.einsum('bhtk,bhtsk,bhsk->bhts', q_t, jnp.exp(diff), k_t)
        out = (jnp.einsum('bhts,bhsv->bhtv', scores, v_t)
               + jnp.einsum('bhtk,bhkv->bhtv', q_t * jnp.exp(b_t), state))
        b_end = b_t[:, :, -1:, :]
        state = (jnp.exp(b_end[:, :, 0, :, None]) * state
                 + jnp.einsum('bhsk,bhsv->bhkv', k_t * jnp.exp(b_end - b_t), v_t))
        return state, out

    state0 = jnp.zeros((bsz, A_HEADS, A_HEAD_DIM, A_HEAD_DIM), f32)
    _, o = lax.scan(chunk_step, state0, (qc, kc, vc, bc))
    o = o.transpose(1, 0, 3, 2, 4).reshape(bsz, seq, D_A)
    o = group_rmsnorm(o, norm_gain, A_HEADS) * jax.nn.silu(g.astype(f32))
    return o.astype(g.dtype)


def rglru(xb, gate, conv_w, conv_b, w_a, b_a, w_x, b_x, lam, norm_gain):
    bsz, seq, _ = xb.shape
    f32 = jnp.float32
    xp = jnp.pad(xb, ((0, 0), (CONV_WIDTH - 1, 0), (0, 0)))
    xc = conv_b + xp[:, 0:seq] * conv_w[0]
    for tap in range(1, CONV_WIDTH):
        xc = xc + xp[:, tap:tap + seq] * conv_w[tap]
    xh = xc.reshape(bsz, seq, B_BLOCKS, B_BLOCK_DIM)
    r = jax.nn.sigmoid((jnp.einsum('blhi,hij->blhj', xh, w_a) + b_a).astype(f32)).reshape(bsz, seq, D_B)
    gate_in = jax.nn.sigmoid((jnp.einsum('blhi,hij->blhj', xh, w_x) + b_x).astype(f32)).reshape(bsz, seq, D_B)
    log_a = -LRU_C * r * jax.nn.softplus(-lam.astype(f32))
    a = jnp.exp(log_a)
    mult = jnp.sqrt(-jnp.expm1(2.0 * log_a))
    u = mult * gate_in * xc.astype(f32)

    def combine(left, right):
        return right[0] * left[0], right[0] * left[1] + right[1]

    _, h = lax.associative_scan(combine, (a, u), axis=1)
    y = h * jax.nn.gelu(gate.astype(f32))
    return group_rmsnorm(y, norm_gain, B_BLOCKS).astype(xb.dtype)


def chunked_sgu(u_in, v_in, w_s, b_s, norm_gain):
    bsz, seq, _ = u_in.shape
    n_chunks = seq // C_CHUNK
    f32 = jnp.float32
    u = jax.nn.gelu(u_in.astype(f32))
    v = jax.nn.gelu(v_in.astype(f32)).reshape(bsz, n_chunks, C_CHUNK, C_GROUPS, C_GROUP_DIM)
    mu = jnp.mean(v, axis=-1, keepdims=True)
    var = jnp.mean(jnp.square(v - mu), axis=-1, keepdims=True)
    v = (v - mu) * lax.rsqrt(var + EPS)
    w = w_s.astype(f32) * jnp.tril(jnp.ones((C_CHUNK, C_CHUNK), f32))
    z = jnp.einsum('gts,bnsgc->bntgc', w, v) + b_s.astype(f32).T[:, :, None]
    y = u * z.reshape(bsz, seq, D_C)
    return group_rmsnorm(y, norm_gain, C_GROUPS).astype(u_in.dtype)


def _fwd_setup_inputs(seed: int = 0) -> dict:
    key = jax.random.key(seed)
    ks = jax.random.split(key, 32)
    f32 = jnp.float32

    def nrm(k, shape, scale):
        return jax.random.normal(k, shape, f32) * scale

    def gain(k, shape):
        return 1.0 + 0.05 * jax.random.normal(k, shape, f32)

    a0 = jax.random.uniform(ks[15], (DEPTH, D_B), f32, 0.9, 0.999)
    lam = jnp.log(a0) - jnp.log1p(-a0)
    return {
        'x': jax.random.normal(ks[0], (BATCH, SEQ, D_MODEL), f32),
        'ffn1_norm': gain(ks[1], (DEPTH, D_MODEL)),
        'ffn1_wg': nrm(ks[2], (DEPTH, D_MODEL, D_FF), D_MODEL ** -0.5),
        'ffn1_wu': nrm(ks[3], (DEPTH, D_MODEL, D_FF), D_MODEL ** -0.5),
        'ffn1_wd': nrm(ks[4], (DEPTH, D_FF, D_MODEL), D_FF ** -0.5),
        'mix_norm': gain(ks[5], (DEPTH, D_MODEL)),
        'w_in': nrm(ks[6], (DEPTH, D_MODEL, D_IN), D_MODEL ** -0.5),
        'hgrn_lb_logits': nrm(ks[7], (DEPTH, D_A), 0.5),
        'hgrn_norm': gain(ks[8], (DEPTH, D_A)),
        'conv_w': nrm(ks[9], (DEPTH, CONV_WIDTH, D_B), CONV_WIDTH ** -0.5),
        'conv_b': nrm(ks[10], (DEPTH, D_B), 0.02),
        'lru_wa': nrm(ks[11], (DEPTH, B_BLOCKS, B_BLOCK_DIM, B_BLOCK_DIM), B_BLOCK_DIM ** -0.5),
        'lru_ba': nrm(ks[12], (DEPTH, B_BLOCKS, B_BLOCK_DIM), 0.02),
        'lru_wx': nrm(ks[13], (DEPTH, B_BLOCKS, B_BLOCK_DIM, B_BLOCK_DIM), B_BLOCK_DIM ** -0.5),
        'lru_bx': nrm(ks[14], (DEPTH, B_BLOCKS, B_BLOCK_DIM), 0.02),
        'lru_lambda': lam,
        'lru_norm': gain(ks[16], (DEPTH, D_B)),
        'sgu_w': nrm(ks[17], (DEPTH, C_GROUPS, C_CHUNK, C_CHUNK), C_CHUNK ** -0.5),
        'sgu_b': gain(ks[18], (DEPTH, C_GROUPS, C_CHUNK)),
        'sgu_norm': gain(ks[19], (DEPTH, D_C)),
        'w_out': nrm(ks[20], (DEPTH, D_MIX, D_MODEL), D_MIX ** -0.5),
        'ffn2_norm': gain(ks[21], (DEPTH, D_MODEL)),
        'ffn2_wg': nrm(ks[22], (DEPTH, D_MODEL, D_FF), D_MODEL ** -0.5),
        'ffn2_wu': nrm(ks[23], (DEPTH, D_MODEL, D_FF), D_MODEL ** -0.5),
        'ffn2_wd': nrm(ks[24], (DEPTH, D_FF, D_MODEL), D_FF ** -0.5),
        'final_norm': gain(ks[25], (D_MODEL,)),
    }


def _fwd_reference(x, ffn1_norm, ffn1_wg, ffn1_wu, ffn1_wd, mix_norm, w_in, hgrn_lb_logits, hgrn_norm,
              conv_w, conv_b, lru_wa, lru_ba, lru_wx, lru_bx, lru_lambda, lru_norm,
              sgu_w, sgu_b, sgu_norm, w_out, ffn2_norm, ffn2_wg, ffn2_wu, ffn2_wd, final_norm):
    lb_soft = jax.nn.softmax(hgrn_lb_logits.astype(jnp.float32), axis=0)
    lower_bounds = jnp.cumsum(lb_soft, axis=0) - lb_soft[0]
    split_points = [int(p) for p in np.cumsum(SPLIT_SIZES)[:-1]]
    h = x
    for layer in range(DEPTH):
        h = h + 0.5 * swiglu(rmsnorm(h, ffn1_norm[layer]), ffn1_wg[layer], ffn1_wu[layer], ffn1_wd[layer])
        z = rmsnorm(h, mix_norm[layer]) @ w_in[layer]
        q, f_logit, i, g, xb, gate, u, v = jnp.split(z, split_points, axis=-1)
        out_a = hgrn2(q, f_logit, i, g, lower_bounds[layer], hgrn_norm[layer])
        out_b = rglru(xb, gate, conv_w[layer], conv_b[layer], lru_wa[layer], lru_ba[layer],
                      lru_wx[layer], lru_bx[layer], lru_lambda[layer], lru_norm[layer])
        out_c = chunked_sgu(u, v, sgu_w[layer], sgu_b[layer], sgu_norm[layer])
        h = h + jnp.concatenate([out_a, out_b, out_c], axis=-1) @ w_out[layer]
        h = h + 0.5 * swiglu(rmsnorm(h, ffn2_norm[layer]), ffn2_wg[layer], ffn2_wu[layer], ffn2_wd[layer])
    return rmsnorm(h, final_norm)


import jax as _jax
import jax.numpy as _jnp

TWIN_FORMAT = 'train_step'
FWD_PARAMS = ['x', 'ffn1_norm', 'ffn1_wg', 'ffn1_wu', 'ffn1_wd', 'mix_norm', 'w_in', 'hgrn_lb_logits', 'hgrn_norm', 'conv_w', 'conv_b', 'lru_wa', 'lru_ba', 'lru_wx', 'lru_bx', 'lru_lambda', 'lru_norm', 'sgu_w', 'sgu_b', 'sgu_norm', 'w_out', 'ffn2_norm', 'ffn2_wg', 'ffn2_wu', 'ffn2_wd', 'final_norm']
TWIN_WEIGHTS = ['ffn1_norm', 'ffn1_wg', 'ffn1_wu', 'ffn1_wd', 'mix_norm', 'w_in', 'hgrn_lb_logits', 'hgrn_norm', 'conv_w', 'conv_b', 'lru_wa', 'lru_ba', 'lru_wx', 'lru_bx', 'lru_lambda', 'lru_norm', 'sgu_w', 'sgu_b', 'sgu_norm', 'w_out', 'ffn2_norm', 'ffn2_wg', 'ffn2_wu', 'ffn2_wd', 'final_norm']
TWIN_DIFF_INPUT = 'x'
TWIN_INPUTS = ['x', 'ffn1_norm', 'ffn1_wg', 'ffn1_wu', 'ffn1_wd', 'mix_norm', 'w_in', 'hgrn_lb_logits', 'hgrn_norm', 'conv_w', 'conv_b', 'lru_wa', 'lru_ba', 'lru_wx', 'lru_bx', 'lru_lambda', 'lru_norm', 'sgu_w', 'sgu_b', 'sgu_norm', 'w_out', 'ffn2_norm', 'ffn2_wg', 'ffn2_wu', 'ffn2_wd', 'final_norm', 'loss_target', 'm_ffn1_norm', 'm_ffn1_wg', 'm_ffn1_wu', 'm_ffn1_wd', 'm_mix_norm', 'm_w_in', 'm_hgrn_lb_logits', 'm_hgrn_norm', 'm_conv_w', 'm_conv_b', 'm_lru_wa', 'm_lru_ba', 'm_lru_wx', 'm_lru_bx', 'm_lru_lambda', 'm_lru_norm', 'm_sgu_w', 'm_sgu_b', 'm_sgu_norm', 'm_w_out', 'm_ffn2_norm', 'm_ffn2_wg', 'm_ffn2_wu', 'm_ffn2_wd', 'm_final_norm', 'v_ffn1_norm', 'v_ffn1_wg', 'v_ffn1_wu', 'v_ffn1_wd', 'v_mix_norm', 'v_w_in', 'v_hgrn_lb_logits', 'v_hgrn_norm', 'v_conv_w', 'v_conv_b', 'v_lru_wa', 'v_lru_ba', 'v_lru_wx', 'v_lru_bx', 'v_lru_lambda', 'v_lru_norm', 'v_sgu_w', 'v_sgu_b', 'v_sgu_norm', 'v_w_out', 'v_ffn2_norm', 'v_ffn2_wg', 'v_ffn2_wu', 'v_ffn2_wd', 'v_final_norm']
TWIN_OUTPUTS = ['loss', 'grad_x', 'grad_ffn1_norm', 'grad_ffn1_wg', 'grad_ffn1_wu', 'grad_ffn1_wd', 'grad_mix_norm', 'grad_w_in', 'grad_hgrn_lb_logits', 'grad_hgrn_norm', 'grad_conv_w', 'grad_conv_b', 'grad_lru_wa', 'grad_lru_ba', 'grad_lru_wx', 'grad_lru_bx', 'grad_lru_lambda', 'grad_lru_norm', 'grad_sgu_w', 'grad_sgu_b', 'grad_sgu_norm', 'grad_w_out', 'grad_ffn2_norm', 'grad_ffn2_wg', 'grad_ffn2_wu', 'grad_ffn2_wd', 'grad_final_norm', 'delta_ffn1_norm', 'delta_ffn1_wg', 'delta_ffn1_wu', 'delta_ffn1_wd', 'delta_mix_norm', 'delta_w_in', 'delta_hgrn_lb_logits', 'delta_hgrn_norm', 'delta_conv_w', 'delta_conv_b', 'delta_lru_wa', 'delta_lru_ba', 'delta_lru_wx', 'delta_lru_bx', 'delta_lru_lambda', 'delta_lru_norm', 'delta_sgu_w', 'delta_sgu_b', 'delta_sgu_norm', 'delta_w_out', 'delta_ffn2_norm', 'delta_ffn2_wg', 'delta_ffn2_wu', 'delta_ffn2_wd', 'delta_final_norm', 'new_m_ffn1_norm', 'new_m_ffn1_wg', 'new_m_ffn1_wu', 'new_m_ffn1_wd', 'new_m_mix_norm', 'new_m_w_in', 'new_m_hgrn_lb_logits', 'new_m_hgrn_norm', 'new_m_conv_w', 'new_m_conv_b', 'new_m_lru_wa', 'new_m_lru_ba', 'new_m_lru_wx', 'new_m_lru_bx', 'new_m_lru_lambda', 'new_m_lru_norm', 'new_m_sgu_w', 'new_m_sgu_b', 'new_m_sgu_norm', 'new_m_w_out', 'new_m_ffn2_norm', 'new_m_ffn2_wg', 'new_m_ffn2_wu', 'new_m_ffn2_wd', 'new_m_final_norm', 'new_v_ffn1_norm', 'new_v_ffn1_wg', 'new_v_ffn1_wu', 'new_v_ffn1_wd', 'new_v_mix_norm', 'new_v_w_in', 'new_v_hgrn_lb_logits', 'new_v_hgrn_norm', 'new_v_conv_w', 'new_v_conv_b', 'new_v_lru_wa', 'new_v_lru_ba', 'new_v_lru_wx', 'new_v_lru_bx', 'new_v_lru_lambda', 'new_v_lru_norm', 'new_v_sgu_w', 'new_v_sgu_b', 'new_v_sgu_norm', 'new_v_w_out', 'new_v_ffn2_norm', 'new_v_ffn2_wg', 'new_v_ffn2_wu', 'new_v_ffn2_wd', 'new_v_final_norm']
TWIN_LEAF_KINDS = {'loss': 'loss', 'grad_x': 'grad_x', 'grad_ffn1_norm': 'grad_w', 'grad_ffn1_wg': 'grad_w', 'grad_ffn1_wu': 'grad_w', 'grad_ffn1_wd': 'grad_w', 'grad_mix_norm': 'grad_w', 'grad_w_in': 'grad_w', 'grad_hgrn_lb_logits': 'grad_w', 'grad_hgrn_norm': 'grad_w', 'grad_conv_w': 'grad_w', 'grad_conv_b': 'grad_w', 'grad_lru_wa': 'grad_w', 'grad_lru_ba': 'grad_w', 'grad_lru_wx': 'grad_w', 'grad_lru_bx': 'grad_w', 'grad_lru_lambda': 'grad_w', 'grad_lru_norm': 'grad_w', 'grad_sgu_w': 'grad_w', 'grad_sgu_b': 'grad_w', 'grad_sgu_norm': 'grad_w', 'grad_w_out': 'grad_w', 'grad_ffn2_norm': 'grad_w', 'grad_ffn2_wg': 'grad_w', 'grad_ffn2_wu': 'grad_w', 'grad_ffn2_wd': 'grad_w', 'grad_final_norm': 'grad_w', 'delta_ffn1_norm': 'delta_w', 'delta_ffn1_wg': 'delta_w', 'delta_ffn1_wu': 'delta_w', 'delta_ffn1_wd': 'delta_w', 'delta_mix_norm': 'delta_w', 'delta_w_in': 'delta_w', 'delta_hgrn_lb_logits': 'delta_w', 'delta_hgrn_norm': 'delta_w', 'delta_conv_w': 'delta_w', 'delta_conv_b': 'delta_w', 'delta_lru_wa': 'delta_w', 'delta_lru_ba': 'delta_w', 'delta_lru_wx': 'delta_w', 'delta_lru_bx': 'delta_w', 'delta_lru_lambda': 'delta_w', 'delta_lru_norm': 'delta_w', 'delta_sgu_w': 'delta_w', 'delta_sgu_b': 'delta_w', 'delta_sgu_norm': 'delta_w', 'delta_w_out': 'delta_w', 'delta_ffn2_norm': 'delta_w', 'delta_ffn2_wg': 'delta_w', 'delta_ffn2_wu': 'delta_w', 'delta_ffn2_wd': 'delta_w', 'delta_final_norm': 'delta_w', 'new_m_ffn1_norm': 'new_m', 'new_m_ffn1_wg': 'new_m', 'new_m_ffn1_wu': 'new_m', 'new_m_ffn1_wd': 'new_m', 'new_m_mix_norm': 'new_m', 'new_m_w_in': 'new_m', 'new_m_hgrn_lb_logits': 'new_m', 'new_m_hgrn_norm': 'new_m', 'new_m_conv_w': 'new_m', 'new_m_conv_b': 'new_m', 'new_m_lru_wa': 'new_m', 'new_m_lru_ba': 'new_m', 'new_m_lru_wx': 'new_m', 'new_m_lru_bx': 'new_m', 'new_m_lru_lambda': 'new_m', 'new_m_lru_norm': 'new_m', 'new_m_sgu_w': 'new_m', 'new_m_sgu_b': 'new_m', 'new_m_sgu_norm': 'new_m', 'new_m_w_out': 'new_m', 'new_m_ffn2_norm': 'new_m', 'new_m_ffn2_wg': 'new_m', 'new_m_ffn2_wu': 'new_m', 'new_m_ffn2_wd': 'new_m', 'new_m_final_norm': 'new_m', 'new_v_ffn1_norm': 'new_v', 'new_v_ffn1_wg': 'new_v', 'new_v_ffn1_wu': 'new_v', 'new_v_ffn1_wd': 'new_v', 'new_v_mix_norm': 'new_v', 'new_v_w_in': 'new_v', 'new_v_hgrn_lb_logits': 'new_v', 'new_v_hgrn_norm': 'new_v', 'new_v_conv_w': 'new_v', 'new_v_conv_b': 'new_v', 'new_v_lru_wa': 'new_v', 'new_v_lru_ba': 'new_v', 'new_v_lru_wx': 'new_v', 'new_v_lru_bx': 'new_v', 'new_v_lru_lambda': 'new_v', 'new_v_lru_norm': 'new_v', 'new_v_sgu_w': 'new_v', 'new_v_sgu_b': 'new_v', 'new_v_sgu_norm': 'new_v', 'new_v_w_out': 'new_v', 'new_v_ffn2_norm': 'new_v', 'new_v_ffn2_wg': 'new_v', 'new_v_ffn2_wu': 'new_v', 'new_v_ffn2_wd': 'new_v', 'new_v_final_norm': 'new_v'}


def _forward(args):
    return _fwd_reference(*[args[k] for k in FWD_PARAMS])


def _output_shape():
    out = _jax.eval_shape(lambda: _forward(_fwd_setup_inputs(0)))
    return out.shape, out.dtype

N_MICROBATCH = 1
ADAM_LR = 0.001
ADAM_B1 = 0.9
ADAM_B2 = 0.999
ADAM_EPS = 1e-08
ADAM_WD = 0.01
ADAM_STEP = 10
PER_EXAMPLE_BATCH_AXIS = {'x': 0, 'loss_target': 0}
SHARED_INPUTS = []
_WEIGHT_DTYPES = {'ffn1_norm': _jnp.float32, 'ffn1_wg': _jnp.float32, 'ffn1_wu': _jnp.float32, 'ffn1_wd': _jnp.float32, 'mix_norm': _jnp.float32, 'w_in': _jnp.float32, 'hgrn_lb_logits': _jnp.float32, 'hgrn_norm': _jnp.float32, 'conv_w': _jnp.float32, 'conv_b': _jnp.float32, 'lru_wa': _jnp.float32, 'lru_ba': _jnp.float32, 'lru_wx': _jnp.float32, 'lru_bx': _jnp.float32, 'lru_lambda': _jnp.float32, 'lru_norm': _jnp.float32, 'sgu_w': _jnp.float32, 'sgu_b': _jnp.float32, 'sgu_norm': _jnp.float32, 'w_out': _jnp.float32, 'ffn2_norm': _jnp.float32, 'ffn2_wg': _jnp.float32, 'ffn2_wu': _jnp.float32, 'ffn2_wd': _jnp.float32, 'final_norm': _jnp.float32}
MOMENT_SCALE = {'ffn1_norm': 8.390695e-02, 'ffn1_wg': 3.545852e-02, 'ffn1_wu': 3.437242e-02, 'ffn1_wd': 5.708422e-02, 'mix_norm': 1.739715e-01, 'w_in': 9.464307e-02, 'hgrn_lb_logits': 4.838065e-03, 'hgrn_norm': 8.332300e-02, 'conv_w': 1.659748e-01, 'conv_b': 6.578611e-01, 'lru_wa': 3.733939e-02, 'lru_ba': 3.461807e-02, 'lru_wx': 6.422794e-02, 'lru_bx': 5.056663e-02, 'lru_lambda': 7.190334e-02, 'lru_norm': 1.402123e-01, 'sgu_w': 5.378348e-02, 'sgu_b': 4.508203e-02, 'sgu_norm': 1.578841e-01, 'w_out': 1.276790e-01, 'ffn2_norm': 5.697479e-02, 'ffn2_wg': 2.453626e-02, 'ffn2_wu': 2.398270e-02, 'ffn2_wd': 3.979369e-02, 'final_norm': 3.207763e+01}


def _to_microbatches(a, axis):
    t = _jnp.moveaxis(a, axis, 0)
    t = t.reshape((N_MICROBATCH, t.shape[0] // N_MICROBATCH) + t.shape[1:])
    return _jnp.moveaxis(t, 1, axis + 1)


def setup_inputs(seed: int = 0) -> dict:
    inp = _fwd_setup_inputs(seed)
    key = _jax.random.fold_in(_jax.random.key(seed), 7919)
    shape, _ = _output_shape()
    out = dict(inp)
    out["loss_target"] = _jax.random.normal(_jax.random.fold_in(key, 0), shape, _jnp.float32)
    for i, name in enumerate(TWIN_WEIGHTS):
        w = inp[name].astype(_jnp.float32)
        if MOMENT_SCALE is None:
            s = _jnp.sqrt(_jnp.mean(_jnp.square(w)) + 1e-30)
        else:
            s = MOMENT_SCALE[name]
        km, kv = _jax.random.split(_jax.random.fold_in(key, i + 1))
        out[name] = w
        out["m_" + name] = s * _jax.random.normal(km, w.shape, _jnp.float32)
        out["v_" + name] = (s * s) * _jax.random.uniform(kv, w.shape, _jnp.float32, 0.5, 1.5)
    if N_MICROBATCH > 1:
        for name, axis in PER_EXAMPLE_BATCH_AXIS.items():
            out[name] = _to_microbatches(out[name], axis)
    return {'x': out['x'], 'ffn1_norm': out['ffn1_norm'], 'ffn1_wg': out['ffn1_wg'], 'ffn1_wu': out['ffn1_wu'], 'ffn1_wd': out['ffn1_wd'], 'mix_norm': out['mix_norm'], 'w_in': out['w_in'], 'hgrn_lb_logits': out['hgrn_lb_logits'], 'hgrn_norm': out['hgrn_norm'], 'conv_w': out['conv_w'], 'conv_b': out['conv_b'], 'lru_wa': out['lru_wa'], 'lru_ba': out['lru_ba'], 'lru_wx': out['lru_wx'], 'lru_bx': out['lru_bx'], 'lru_lambda': out['lru_lambda'], 'lru_norm': out['lru_norm'], 'sgu_w': out['sgu_w'], 'sgu_b': out['sgu_b'], 'sgu_norm': out['sgu_norm'], 'w_out': out['w_out'], 'ffn2_norm': out['ffn2_norm'], 'ffn2_wg': out['ffn2_wg'], 'ffn2_wu': out['ffn2_wu'], 'ffn2_wd': out['ffn2_wd'], 'final_norm': out['final_norm'], 'loss_target': out['loss_target'], 'm_ffn1_norm': out['m_ffn1_norm'], 'm_ffn1_wg': out['m_ffn1_wg'], 'm_ffn1_wu': out['m_ffn1_wu'], 'm_ffn1_wd': out['m_ffn1_wd'], 'm_mix_norm': out['m_mix_norm'], 'm_w_in': out['m_w_in'], 'm_hgrn_lb_logits': out['m_hgrn_lb_logits'], 'm_hgrn_norm': out['m_hgrn_norm'], 'm_conv_w': out['m_conv_w'], 'm_conv_b': out['m_conv_b'], 'm_lru_wa': out['m_lru_wa'], 'm_lru_ba': out['m_lru_ba'], 'm_lru_wx': out['m_lru_wx'], 'm_lru_bx': out['m_lru_bx'], 'm_lru_lambda': out['m_lru_lambda'], 'm_lru_norm': out['m_lru_norm'], 'm_sgu_w': out['m_sgu_w'], 'm_sgu_b': out['m_sgu_b'], 'm_sgu_norm': out['m_sgu_norm'], 'm_w_out': out['m_w_out'], 'm_ffn2_norm': out['m_ffn2_norm'], 'm_ffn2_wg': out['m_ffn2_wg'], 'm_ffn2_wu': out['m_ffn2_wu'], 'm_ffn2_wd': out['m_ffn2_wd'], 'm_final_norm': out['m_final_norm'], 'v_ffn1_norm': out['v_ffn1_norm'], 'v_ffn1_wg': out['v_ffn1_wg'], 'v_ffn1_wu': out['v_ffn1_wu'], 'v_ffn1_wd': out['v_ffn1_wd'], 'v_mix_norm': out['v_mix_norm'], 'v_w_in': out['v_w_in'], 'v_hgrn_lb_logits': out['v_hgrn_lb_logits'], 'v_hgrn_norm': out['v_hgrn_norm'], 'v_conv_w': out['v_conv_w'], 'v_conv_b': out['v_conv_b'], 'v_lru_wa': out['v_lru_wa'], 'v_lru_ba': out['v_lru_ba'], 'v_lru_wx': out['v_lru_wx'], 'v_lru_bx': out['v_lru_bx'], 'v_lru_lambda': out['v_lru_lambda'], 'v_lru_norm': out['v_lru_norm'], 'v_sgu_w': out['v_sgu_w'], 'v_sgu_b': out['v_sgu_b'], 'v_sgu_norm': out['v_sgu_norm'], 'v_w_out': out['v_w_out'], 'v_ffn2_norm': out['v_ffn2_norm'], 'v_ffn2_wg': out['v_ffn2_wg'], 'v_ffn2_wu': out['v_ffn2_wu'], 'v_ffn2_wd': out['v_ffn2_wd'], 'v_final_norm': out['v_final_norm']}


def _loss(weights, diff, rest, loss_target):
    with _jax.named_scope("forward"):
        args = {**rest, TWIN_DIFF_INPUT: diff, **{k: w.astype(_WEIGHT_DTYPES[k]) for k, w in weights.items()}}
        y = _forward(args)
    with _jax.named_scope("loss_head"):
        err = _jnp.square(y.astype(_jnp.float32) - loss_target)
        return 0.5 * _jnp.sum(_jnp.mean(err, axis=-1)) if err.ndim else 0.5 * err


def _adamw(w, g, m, v):
    m = ADAM_B1 * m + (1.0 - ADAM_B1) * g
    v = ADAM_B2 * v + (1.0 - ADAM_B2) * _jnp.square(g)
    m_hat = m / (1.0 - ADAM_B1 ** ADAM_STEP)
    v_hat = v / (1.0 - ADAM_B2 ** ADAM_STEP)
    delta = -ADAM_LR * (m_hat / (_jnp.sqrt(v_hat) + ADAM_EPS) + ADAM_WD * w)
    return delta, m, v


def reference(x, ffn1_norm, ffn1_wg, ffn1_wu, ffn1_wd, mix_norm, w_in, hgrn_lb_logits, hgrn_norm, conv_w, conv_b, lru_wa, lru_ba, lru_wx, lru_bx, lru_lambda, lru_norm, sgu_w, sgu_b, sgu_norm, w_out, ffn2_norm, ffn2_wg, ffn2_wu, ffn2_wd, final_norm, loss_target, m_ffn1_norm, m_ffn1_wg, m_ffn1_wu, m_ffn1_wd, m_mix_norm, m_w_in, m_hgrn_lb_logits, m_hgrn_norm, m_conv_w, m_conv_b, m_lru_wa, m_lru_ba, m_lru_wx, m_lru_bx, m_lru_lambda, m_lru_norm, m_sgu_w, m_sgu_b, m_sgu_norm, m_w_out, m_ffn2_norm, m_ffn2_wg, m_ffn2_wu, m_ffn2_wd, m_final_norm, v_ffn1_norm, v_ffn1_wg, v_ffn1_wu, v_ffn1_wd, v_mix_norm, v_w_in, v_hgrn_lb_logits, v_hgrn_norm, v_conv_w, v_conv_b, v_lru_wa, v_lru_ba, v_lru_wx, v_lru_bx, v_lru_lambda, v_lru_norm, v_sgu_w, v_sgu_b, v_sgu_norm, v_w_out, v_ffn2_norm, v_ffn2_wg, v_ffn2_wu, v_ffn2_wd, v_final_norm):
    given = dict(x=x, ffn1_norm=ffn1_norm, ffn1_wg=ffn1_wg, ffn1_wu=ffn1_wu, ffn1_wd=ffn1_wd, mix_norm=mix_norm, w_in=w_in, hgrn_lb_logits=hgrn_lb_logits, hgrn_norm=hgrn_norm, conv_w=conv_w, conv_b=conv_b, lru_wa=lru_wa, lru_ba=lru_ba, lru_wx=lru_wx, lru_bx=lru_bx, lru_lambda=lru_lambda, lru_norm=lru_norm, sgu_w=sgu_w, sgu_b=sgu_b, sgu_norm=sgu_norm, w_out=w_out, ffn2_norm=ffn2_norm, ffn2_wg=ffn2_wg, ffn2_wu=ffn2_wu, ffn2_wd=ffn2_wd, final_norm=final_norm, loss_target=loss_target, m_ffn1_norm=m_ffn1_norm, m_ffn1_wg=m_ffn1_wg, m_ffn1_wu=m_ffn1_wu, m_ffn1_wd=m_ffn1_wd, m_mix_norm=m_mix_norm, m_w_in=m_w_in, m_hgrn_lb_logits=m_hgrn_lb_logits, m_hgrn_norm=m_hgrn_norm, m_conv_w=m_conv_w, m_conv_b=m_conv_b, m_lru_wa=m_lru_wa, m_lru_ba=m_lru_ba, m_lru_wx=m_lru_wx, m_lru_bx=m_lru_bx, m_lru_lambda=m_lru_lambda, m_lru_norm=m_lru_norm, m_sgu_w=m_sgu_w, m_sgu_b=m_sgu_b, m_sgu_norm=m_sgu_norm, m_w_out=m_w_out, m_ffn2_norm=m_ffn2_norm, m_ffn2_wg=m_ffn2_wg, m_ffn2_wu=m_ffn2_wu, m_ffn2_wd=m_ffn2_wd, m_final_norm=m_final_norm, v_ffn1_norm=v_ffn1_norm, v_ffn1_wg=v_ffn1_wg, v_ffn1_wu=v_ffn1_wu, v_ffn1_wd=v_ffn1_wd, v_mix_norm=v_mix_norm, v_w_in=v_w_in, v_hgrn_lb_logits=v_hgrn_lb_logits, v_hgrn_norm=v_hgrn_norm, v_conv_w=v_conv_w, v_conv_b=v_conv_b, v_lru_wa=v_lru_wa, v_lru_ba=v_lru_ba, v_lru_wx=v_lru_wx, v_lru_bx=v_lru_bx, v_lru_lambda=v_lru_lambda, v_lru_norm=v_lru_norm, v_sgu_w=v_sgu_w, v_sgu_b=v_sgu_b, v_sgu_norm=v_sgu_norm, v_w_out=v_w_out, v_ffn2_norm=v_ffn2_norm, v_ffn2_wg=v_ffn2_wg, v_ffn2_wu=v_ffn2_wu, v_ffn2_wd=v_ffn2_wd, v_final_norm=v_final_norm)
    weights = {n: given[n] for n in TWIN_WEIGHTS}
    shared = {n: given[n] for n in SHARED_INPUTS}
    per_example = {n: given[n] for n in ['x']}
    grad_fn = _jax.value_and_grad(_loss, argnums=(0, 1))

    def one_microbatch(ex, loss_target):
        ex = dict(ex)
        diff = ex.pop(TWIN_DIFF_INPUT)
        return grad_fn(weights, diff, {**shared, **ex}, loss_target)

    if N_MICROBATCH == 1:
        loss, (grad_w, grad_x) = one_microbatch(per_example, given["loss_target"])
    else:
        def body(carry, xs):
            loss_sum, grad_sum = carry
            l_k, (gw_k, gx_k) = one_microbatch(xs[0], xs[1])
            with _jax.named_scope("update"):
                return (loss_sum + l_k, _jax.tree.map(_jnp.add, grad_sum, gw_k)), gx_k

        init = (_jnp.zeros((), _jnp.float32), _jax.tree.map(_jnp.zeros_like, weights))
        (loss, grad_w), grad_x = _jax.lax.scan(body, init, (per_example, given["loss_target"]))
    with _jax.named_scope("update"):
        delta_w, new_m, new_v = {}, {}, {}
        for n in TWIN_WEIGHTS:
            delta_w[n], new_m[n], new_v[n] = _adamw(weights[n], grad_w[n], given["m_" + n], given["v_" + n])
    return (loss, grad_x, *[grad_w[n] for n in TWIN_WEIGHTS], *[delta_w[n] for n in TWIN_WEIGHTS],
            *[new_m[n] for n in TWIN_WEIGHTS], *[new_v[n] for n in TWIN_WEIGHTS])
```

```python
import functools

import jax
import jax.numpy as jnp
from jax import lax
from jax.experimental import pallas as pl
from jax.experimental.pallas import tpu as pltpu

F32 = jnp.float32
MXU = jnp.bfloat16
SAVE = jnp.bfloat16
WIRE = jnp.bfloat16

NDEV = 8
D = 1024
FF = 2816
FFS = FF // NDEV
FFP = 384
DIN = 3072
DINS = DIN // NDEV
DA, DB, DC = 512, 256, 256
HD = 128
NH = DA // HD
ACH = 64
CCH = 128
GRP = 64
EPS = 1e-6
LRU_C = 8.0
VMEM_LIMIT = 56 * 1024 * 1024
TM_F = 512
TM_B = 256
TB = 512

ADAM_LR, ADAM_B1, ADAM_B2, ADAM_EPS, ADAM_WD, ADAM_STEP = 0.001, 0.9, 0.999, 1e-08, 0.01, 10

MESH = pl.DeviceIdType.MESH


def _mm(a, b):
    return jnp.dot(a.astype(MXU), b.astype(MXU), preferred_element_type=F32)


def _mm_nt(a, b):
    return lax.dot_general(a.astype(MXU), b.astype(MXU), (((1,), (1,)), ((), ())), preferred_element_type=F32)


def _mm_tn(a, b):
    return lax.dot_general(a.astype(MXU), b.astype(MXU), (((0,), (0,)), ((), ())), preferred_element_type=F32)


def _split3(x):
    x1 = x.astype(MXU)
    r1 = x - x1.astype(F32)
    x2 = r1.astype(MXU)
    r2 = r1 - x2.astype(F32)
    return x1, x2, r2.astype(MXU)


def _mm_exact_l(c, x):
    x1, x2, x3 = _split3(x)
    return _mm(c, x1) + _mm(c, x2) + _mm(c, x3)


def _mm_exact_r(x, c):
    x1, x2, x3 = _split3(x)
    return _mm(x1, c) + _mm(x2, c) + _mm(x3, c)


def _sigmoid(x):
    return 1.0 / (1.0 + jnp.exp(-x))


def _gelu(x):
    c, k = 0.7978845608028654, 0.044715
    th = jnp.tanh(c * (x + k * x * x * x))
    return 0.5 * x * (1.0 + th)


def _gelu_and_grad(x):
    c, k = 0.7978845608028654, 0.044715
    th = jnp.tanh(c * (x + k * x * x * x))
    g = 0.5 * x * (1.0 + th)
    dg = 0.5 * (1.0 + th) + 0.5 * x * (1.0 - th * th) * c * (1.0 + 3.0 * k * x * x)
    return g, dg


def _expm1(x):
    series = x * (1.0 + x * (0.5 + x * (1.0 / 6.0 + x * (1.0 / 24.0 + x * (1.0 / 120.0)))))
    return jnp.where(jnp.abs(x) < 0.05, series, jnp.exp(x) - 1.0)


def _iota(shape, dim):
    return lax.broadcasted_iota(jnp.int32, shape, dim)


def _tri(n, lower):
    r, c = _iota((n, n), 0), _iota((n, n), 1)
    return jnp.where((r >= c) if lower else (r <= c), 1.0, 0.0).astype(F32)


def _group_matrix(n, value):
    r, c = _iota((n, n), 0), _iota((n, n), 1)
    return jnp.where((r // GRP) == (c // GRP), value, 0.0).astype(F32)


def _row(x, k):
    r = _iota(x.shape, 0)
    return jnp.sum(jnp.where(r == k, x, 0.0), axis=0, keepdims=True)


def _rms_bwd(dxn, hh, gain):
    rstd = lax.rsqrt(jnp.mean(hh * hh, axis=-1, keepdims=True) + EPS)
    xhat = hh * rstd
    dxh = dxn * gain
    dh = rstd * (dxh - xhat * jnp.mean(dxh * xhat, axis=-1, keepdims=True))
    return dh, jnp.sum(dxn * xhat, axis=0, keepdims=True)


def _params(sem):
    return pltpu.CompilerParams(dimension_semantics=sem, vmem_limit_bytes=VMEM_LIMIT)


def _all_gather(arrs, name):
    n = len(arrs)

    def body(*refs):
        ins, outs = refs[:n], refs[n:2 * n]
        send_sems, recv_sems, local_sems = refs[2 * n:]
        x, y, c = lax.axis_index("x"), lax.axis_index("y"), lax.axis_index("c")
        me, sibling = (x, y, c), (x, y, 1 - c)
        chips = [(1 - x, y), (x, 1 - y), (1 - x, 1 - y)]

        def slot(px, py, pc):
            return 4 * px + 2 * py + pc

        def copy(a, k, block, to, src=None):
            dst = outs[a].at[slot(*block)]
            return pltpu.make_async_remote_copy(
                src_ref=dst if src is None else src, dst_ref=dst,
                send_sem=send_sems.at[a * 7 + k], recv_sem=recv_sems.at[a * 7 + k],
                device_id=to, device_id_type=MESH)

        started = []
        for a in range(n):
            mine = pltpu.make_async_copy(ins[a], outs[a].at[slot(*me)], local_sems.at[a])
            mine.start()
            started.append(mine)
        first = []
        for a in range(n):
            first.append(copy(a, 0, me, sibling, src=ins[a]))
            first += [copy(a, 1 + j, me, (*chip, c), src=ins[a]) for j, chip in enumerate(chips)]
        for cp in first:
            cp.start()
        passed = []
        for a in range(n):
            for j, chip in enumerate(chips):
                copy(a, 1 + j, (*chip, c), me).wait_recv()
                fwd = copy(a, 4 + j, (*chip, c), sibling)
                fwd.start()
                passed.append(fwd)
        for a in range(n):
            copy(a, 0, sibling, me).wait_recv()
            for j, chip in enumerate(chips):
                copy(a, 4 + j, (*chip, 1 - c), me).wait_recv()
        for cp in first + passed:
            cp.wait_send()
        for mine in started:
            mine.wait()

    hbm = pl.BlockSpec(memory_space=pl.ANY)
    return pl.pallas_call(
        body, name=name,
        out_shape=[jax.ShapeDtypeStruct((NDEV,) + a.shape, a.dtype) for a in arrs],
        in_specs=[hbm] * n, out_specs=[hbm] * n,
        scratch_shapes=[pltpu.SemaphoreType.DMA((7 * n,)), pltpu.SemaphoreType.DMA((7 * n,)),
                        pltpu.SemaphoreType.DMA((n,))],
    )(*arrs)


def _exchange_blocks(groups, name):
    ng, depth = len(groups), len(groups[0])
    n = ng * depth
    flat = [a for grp in groups for a in grp]

    def body(*refs):
        ins, outs = refs[:n], refs[n:n + ng]
        send_sems, recv_sems, local_sems = refs[n + ng:]
        x, y, c = lax.axis_index("x"), lax.axis_index("y"), lax.axis_index("c")
        my = 4 * x + 2 * y + c
        peers = [(x ^ ((k >> 2) & 1), y ^ ((k >> 1) & 1), c ^ (k & 1)) for k in range(1, NDEV)]

        def slot(p):
            return 4 * p[0] + 2 * p[1] + p[2]

        def copy(a, k, src_slot, dst_slot, peer):
            return pltpu.make_async_remote_copy(
                src_ref=ins[a].at[src_slot], dst_ref=outs[a // depth].at[a % depth, dst_slot],
                send_sem=send_sems.at[a * 7 + k], recv_sem=recv_sems.at[a * 7 + k],
                device_id=peer, device_id_type=MESH)

        local = []
        for a in range(n):
            cp = pltpu.make_async_copy(ins[a].at[my], outs[a // depth].at[a % depth, my], local_sems.at[a])
            cp.start()
            local.append(cp)
        sent = [copy(a, k, slot(peer), my, peer) for a in range(n) for k, peer in enumerate(peers)]
        for cp in sent:
            cp.start()
        for a in range(n):
            for k, peer in enumerate(peers):
                copy(a, k, my, slot(peer), peer).wait_recv()
        for cp in sent:
            cp.wait_send()
        for cp in local:
            cp.wait()

    hbm = pl.BlockSpec(memory_space=pl.ANY)
    return pl.pallas_call(
        body, name=name,
        out_shape=[jax.ShapeDtypeStruct((depth,) + grp[0].shape, grp[0].dtype) for grp in groups],
        in_specs=[hbm] * n, out_specs=[hbm] * ng,
        scratch_shapes=[pltpu.SemaphoreType.DMA((7 * n,)), pltpu.SemaphoreType.DMA((7 * n,)),
                        pltpu.SemaphoreType.DMA((n,))],
    )(*flat)


def _ffn_fwd(h, gain, wgu, wd, tm):
    t = h.shape[0]

    def body(h_ref, g_ref, wgu_ref, wd_ref, out_ref, xn_ref, ab_ref, acc_ref):
        j = pl.program_id(1)

        @pl.when(j == 0)
        def _():
            hh = h_ref[...]
            rstd = lax.rsqrt(jnp.mean(hh * hh, axis=-1, keepdims=True) + EPS)
            xn_ref[...] = (hh * rstd * g_ref[...]).astype(xn_ref.dtype)
            acc_ref[...] = jnp.zeros_like(acc_ref)

        ab = _mm(xn_ref[...], wgu_ref[...])
        ab_ref[...] = ab.astype(ab_ref.dtype)
        a, b = ab[:, :FFP], ab[:, FFP:]
        s = a * _sigmoid(a) * b
        acc_ref[...] += _mm(s, wd_ref[...])

        @pl.when(j == NDEV - 1)
        def _():
            out_ref[...] = h_ref[...] + 0.5 * acc_ref[...]

    return pl.pallas_call(
        body, name="ffn_fwd", grid=(t // tm, NDEV),
        in_specs=[pl.BlockSpec((tm, D), lambda i, j: (i, 0)),
                  pl.BlockSpec((1, D), lambda i, j: (0, 0)),
                  pl.BlockSpec((None, D, 2 * FFP), lambda i, j: (j, 0, 0)),
                  pl.BlockSpec((None, FFP, D), lambda i, j: (j, 0, 0))],
        out_specs=[pl.BlockSpec((tm, D), lambda i, j: (i, 0)),
                   pl.BlockSpec((tm, D), lambda i, j: (i, 0)),
                   pl.BlockSpec((tm, 2 * FFP), lambda i, j: (i, j))],
        out_shape=[jax.ShapeDtypeStruct((t, D), F32), jax.ShapeDtypeStruct((t, D), SAVE),
                   jax.ShapeDtypeStruct((t, NDEV * 2 * FFP), SAVE)],
        scratch_shapes=[pltpu.VMEM((tm, D), F32)],
        compiler_params=_params(("parallel", "arbitrary")),
    )(h, gain, wgu, wd)


def _ffn_bwd(dout, h, gain, xn, ab, wgu, wd, tm):
    t = h.shape[0]
    nt = t // tm
    last = NDEV - 1

    def body(dout_ref, h_ref, g_ref, xn_ref, ab_ref, wgu_ref, wd_ref,
             dh_ref, dwgu_ref, dwd_ref, dgain_ref, dxn_scr, agu_scr, awd_scr):
        j, i = pl.program_id(0), pl.program_id(1)

        @pl.when(i == 0)
        def _():
            agu_scr[...] = jnp.zeros_like(agu_scr)
            awd_scr[...] = jnp.zeros_like(awd_scr)

        @pl.when((i == 0) & (j == 0))
        def _():
            dgain_ref[...] = jnp.zeros_like(dgain_ref)

        dy = (0.5 * dout_ref[...]).astype(MXU)
        ds = _mm_nt(dy, wd_ref[...])
        ab_ = ab_ref[...].astype(F32)
        a, b = ab_[:, :FFP], ab_[:, FFP:]
        sg = _sigmoid(a)
        sa = a * sg
        db = ds * sa
        da = ds * b * (sg * (1.0 + a * (1.0 - sg)))
        s = sa * b
        dab = jnp.concatenate([da, db], axis=1).astype(MXU)
        part = _mm_nt(dab, wgu_ref[...])
        agu_scr[...] += _mm_tn(xn_ref[...], dab)
        awd_scr[...] += _mm_tn(s.astype(MXU), dy)
        rows = pl.ds(pl.multiple_of(i * tm, tm), tm)

        @pl.when(j == 0)
        def _():
            dxn_scr[rows, :] = part

        @pl.when((j > 0) & (j < last))
        def _():
            dxn_scr[rows, :] += part

        @pl.when(j == last)
        def _():
            dxn = dxn_scr[rows, :] + part
            dh, dg = _rms_bwd(dxn, h_ref[...], g_ref[...])
            dh_ref[...] = dout_ref[...] + dh
            dgain_ref[...] += dg

        @pl.when(i == nt - 1)
        def _():
            dwgu_ref[...] = agu_scr[...].astype(dwgu_ref.dtype)
            dwd_ref[...] = awd_scr[...].astype(dwd_ref.dtype)

    def tail(j, i):
        return (jnp.where(j == last, i, 0), 0)

    return pl.pallas_call(
        body, name="ffn_bwd", grid=(NDEV, nt),
        in_specs=[pl.BlockSpec((tm, D), lambda j, i: (i, 0)),
                  pl.BlockSpec((tm, D), tail),
                  pl.BlockSpec((1, D), lambda j, i: (0, 0)),
                  pl.BlockSpec((tm, D), lambda j, i: (i, 0)),
                  pl.BlockSpec((tm, 2 * FFP), lambda j, i: (i, j)),
                  pl.BlockSpec((None, D, 2 * FFP), lambda j, i: (j, 0, 0)),
                  pl.BlockSpec((None, FFP, D), lambda j, i: (j, 0, 0))],
        out_specs=[pl.BlockSpec((tm, D), tail),
                   pl.BlockSpec((None, D, 2 * FFP), lambda j, i: (j, 0, 0)),
                   pl.BlockSpec((None, FFP, D), lambda j, i: (j, 0, 0)),
                   pl.BlockSpec((1, D), lambda j, i: (0, 0))],
        out_shape=[jax.ShapeDtypeStruct((t, D), F32),
                   jax.ShapeDtypeStruct((NDEV, D, 2 * FFP), WIRE),
                   jax.ShapeDtypeStruct((NDEV, FFP, D), WIRE),
                   jax.ShapeDtypeStruct((1, D), F32)],
        scratch_shapes=[pltpu.VMEM((t, D), F32), pltpu.VMEM((D, 2 * FFP), F32), pltpu.VMEM((FFP, D), F32)],
        compiler_params=_params(("arbitrary", "arbitrary")),
    )(dout, h, gain, xn, ab, wgu, wd)


def _inproj_fwd(h, gain, win, tm):
    t = h.shape[0]

    def body(h_ref, g_ref, w_ref, z_ref, xn_ref):
        hh = h_ref[...]
        rstd = lax.rsqrt(jnp.mean(hh * hh, axis=-1, keepdims=True) + EPS)
        xn = (hh * rstd * g_ref[...]).astype(MXU)
        xn_ref[...] = xn.astype(xn_ref.dtype)
        for j in range(NDEV):
            z_ref[:, j * DINS:(j + 1) * DINS] = _mm(xn, w_ref[j])

    return pl.pallas_call(
        body, name="inproj_fwd", grid=(t // tm,),
        in_specs=[pl.BlockSpec((tm, D), lambda i: (i, 0)),
                  pl.BlockSpec((1, D), lambda i: (0, 0)),
                  pl.BlockSpec((NDEV, D, DINS), lambda i: (0, 0, 0))],
        out_specs=[pl.BlockSpec((tm, DIN), lambda i: (i, 0)),
                   pl.BlockSpec((tm, D), lambda i: (i, 0))],
        out_shape=[jax.ShapeDtypeStruct((t, DIN), F32), jax.ShapeDtypeStruct((t, D), SAVE)],
        compiler_params=_params(("parallel",)),
    )(h, gain, win)


def _inproj_bwd(dres, dz, h, gain, xn, win, tm):
    t = h.shape[0]
    nt = t // tm
    last = NDEV - 1

    def body(dres_ref, dz_ref, h_ref, g_ref, xn_ref, w_ref, dh_ref, dw_ref, dgain_ref, dxn_scr, acc_scr):
        j, i = pl.program_id(0), pl.program_id(1)

        @pl.when(i == 0)
        def _():
            acc_scr[...] = jnp.zeros_like(acc_scr)

        @pl.when((i == 0) & (j == 0))
        def _():
            dgain_ref[...] = jnp.zeros_like(dgain_ref)

        dzb = dz_ref[...]
        part = _mm_nt(dzb, w_ref[...])
        acc_scr[...] += _mm_tn(xn_ref[...], dzb)
        rows = pl.ds(pl.multiple_of(i * tm, tm), tm)

        @pl.when(j == 0)
        def _():
            dxn_scr[rows, :] = part

        @pl.when((j > 0) & (j < last))
        def _():
            dxn_scr[rows, :] += part

        @pl.when(j == last)
        def _():
            dxn = dxn_scr[rows, :] + part
            dh, dg = _rms_bwd(dxn, h_ref[...], g_ref[...])
            dh_ref[...] = dres_ref[...] + dh
            dgain_ref[...] += dg

        @pl.when(i == nt - 1)
        def _():
            dw_ref[...] = acc_scr[...].astype(dw_ref.dtype)

    def tail(j, i):
        return (jnp.where(j == last, i, 0), 0)

    return pl.pallas_call(
        body, name="inproj_bwd", grid=(NDEV, nt),
        in_specs=[pl.BlockSpec((tm, D), tail),
                  pl.BlockSpec((tm, DINS), lambda j, i: (i, j)),
                  pl.BlockSpec((tm, D), tail),
                  pl.BlockSpec((1, D), lambda j, i: (0, 0)),
                  pl.BlockSpec((tm, D), lambda j, i: (i, 0)),
                  pl.BlockSpec((None, D, DINS), lambda j, i: (j, 0, 0))],
        out_specs=[pl.BlockSpec((tm, D), tail),
                   pl.BlockSpec((None, D, DINS), lambda j, i: (j, 0, 0)),
                   pl.BlockSpec((1, D), lambda j, i: (0, 0))],
        out_shape=[jax.ShapeDtypeStruct((t, D), F32),
                   jax.ShapeDtypeStruct((NDEV, D, DINS), WIRE),
                   jax.ShapeDtypeStruct((1, D), F32)],
        scratch_shapes=[pltpu.VMEM((t, D), F32), pltpu.VMEM((D, DINS), F32)],
        compiler_params=_params(("arbitrary", "arbitrary")),
    )(dres, dz, h, gain, xn, win)


def _outproj_fwd(h, oa, ob, oc, wout, tm):
    t = h.shape[0]

    def body(h_ref, oa_ref, ob_ref, oc_ref, w_ref, out_ref):
        ym = jnp.concatenate([oa_ref[...], ob_ref[...], oc_ref[...]], axis=1)
        out_ref[...] = h_ref[...] + _mm(ym, w_ref[...])

    return pl.pallas_call(
        body, name="outproj_fwd", grid=(t // tm,),
        in_specs=[pl.BlockSpec((tm, D), lambda i: (i, 0)),
                  pl.BlockSpec((tm, DA), lambda i: (i, 0)),
                  pl.BlockSpec((tm, DB), lambda i: (i, 0)),
                  pl.BlockSpec((tm, DC), lambda i: (i, 0)),
                  pl.BlockSpec((D, D), lambda i: (0, 0))],
        out_specs=pl.BlockSpec((tm, D), lambda i: (i, 0)),
        out_shape=jax.ShapeDtypeStruct((t, D), F32),
        compiler_params=_params(("parallel",)),
    )(h, oa, ob, oc, wout)


def _outproj_bwd(dh, oa, ob, oc, wout, tm):
    t = dh.shape[0]
    nt = t // tm

    def body(dh_ref, oa_ref, ob_ref, oc_ref, w_ref, da_ref, db_ref, dc_ref, dw_ref, acc_scr):
        i = pl.program_id(0)

        @pl.when(i == 0)
        def _():
            acc_scr[...] = jnp.zeros_like(acc_scr)

        d16 = dh_ref[...].astype(MXU)
        dym = _mm_nt(d16, w_ref[...])
        da_ref[...] = dym[:, :DA]
        db_ref[...] = dym[:, DA:DA + DB]
        dc_ref[...] = dym[:, DA + DB:]
        ym = jnp.concatenate([oa_ref[...], ob_ref[...], oc_ref[...]], axis=1)
        acc_scr[...] += _mm_tn(ym, d16)

        @pl.when(i == nt - 1)
        def _():
            dw_ref[...] = acc_scr[...].astype(dw_ref.dtype)

    return pl.pallas_call(
        body, name="outproj_bwd", grid=(nt,),
        in_specs=[pl.BlockSpec((tm, D), lambda i: (i, 0)),
                  pl.BlockSpec((tm, DA), lambda i: (i, 0)),
                  pl.BlockSpec((tm, DB), lambda i: (i, 0)),
                  pl.BlockSpec((tm, DC), lambda i: (i, 0)),
                  pl.BlockSpec((D, D), lambda i: (0, 0))],
        out_specs=[pl.BlockSpec((tm, DA), lambda i: (i, 0)),
                   pl.BlockSpec((tm, DB), lambda i: (i, 0)),
                   pl.BlockSpec((tm, DC), lambda i: (i, 0)),
                   pl.BlockSpec((D, D), lambda i: (0, 0))],
        out_shape=[jax.ShapeDtypeStruct((t, DA), F32), jax.ShapeDtypeStruct((t, DB), F32),
                   jax.ShapeDtypeStruct((t, DC), F32), jax.ShapeDtypeStruct((D, D), WIRE)],
        scratch_shapes=[pltpu.VMEM((D, D), F32)],
        compiler_params=_params(("arbitrary",)),
    )(dh, oa, ob, oc, wout)


def _lower_bounds(logits):
    depth, n = logits.shape

    def body(l_ref, lb_ref, p_ref):
        rows = [l_ref[l:l + 1, :] for l in range(depth)]
        mx = functools.reduce(jnp.maximum, rows)
        ex = [jnp.exp(r - mx) for r in rows]
        den = functools.reduce(lambda u, v: u + v, ex)
        acc = jnp.zeros_like(den)
        for l in range(depth):
            p = ex[l] / den
            p_ref[l:l + 1, :] = p
            if l > 0:
                acc = acc + p
            lb_ref[l:l + 1, :] = acc

    return pl.pallas_call(
        body, name="lower_bounds",
        out_shape=[jax.ShapeDtypeStruct((depth, n), F32), jax.ShapeDtypeStruct((depth, n), F32)],
    )(logits)


def _lower_bounds_bwd(p, dlb):
    depth, n = p.shape

    def body(p_ref, d_ref, out_ref):
        ps = [p_ref[l:l + 1, :] for l in range(depth)]
        ds = [d_ref[l:l + 1, :] for l in range(depth)]
        dp = [jnp.zeros_like(ps[0]) for _ in range(depth)]
        run = jnp.zeros_like(ps[0])
        for l in range(depth - 1, 0, -1):
            run = run + ds[l]
            dp[l] = run
        dot = functools.reduce(lambda u, v: u + v, [ps[l] * dp[l] for l in range(depth)])
        for l in range(depth):
            out_ref[l:l + 1, :] = ps[l] * (dp[l] - dot)

    return pl.pallas_call(body, name="lower_bounds_bwd", out_shape=jax.ShapeDtypeStruct((depth, n), F32))(p, dlb)


def _hgrn_chunk(z_ref, lb_ref, hd):
    c0 = hd * HD
    q = z_ref[:, c0:c0 + HD]
    fl = z_ref[:, DA + c0:DA + c0 + HD]
    v = z_ref[:, 2 * DA + c0:2 * DA + c0 + HD]
    g = z_ref[:, 3 * DA + c0:3 * DA + c0 + HD]
    lb = lb_ref[:, c0:c0 + HD]
    sq = _sigmoid(q)
    qs = q * sq
    sg = _sigmoid(fl)
    f = lb + (1.0 - lb) * sg
    k = 1.0 - f
    lf = jnp.log(f)
    b = _mm_exact_l(_tri(ACH, True).astype(MXU), lf)
    bend = jnp.sum(lf, axis=0, keepdims=True)
    r = 0.5 * bend
    eq, ek, eb, ed = jnp.exp(b - r), jnp.exp(r - b), jnp.exp(b), jnp.exp(bend - b)
    qt, kt, qe, kd = qs * eq, k * ek, qs * eb, k * ed
    causal = _iota((ACH, ACH), 0) >= _iota((ACH, ACH), 1)
    att = jnp.where(causal, _mm_nt(qt, kt), 0.0)
    return dict(q=q, v=v, g=g, lb=lb, sq=sq, qs=qs, sg=sg, f=f, k=k, bend=bend, eq=eq, ek=ek, eb=eb, ed=ed,
                qt=qt, kt=kt, qe=qe, kd=kd, att=att, causal=causal)


def _hgrn_fwd(z, lb, gain):
    t = z.shape[0]
    nc = t // ACH

    def body(z_ref, lb_ref, g_ref, o_ref, oa_ref, st_ref, st_scr):
        @pl.when(pl.program_id(0) == 0)
        def _():
            st_scr[...] = jnp.zeros_like(st_scr)

        for hd in range(NH):
            c = _hgrn_chunk(z_ref, lb_ref, hd)
            st = st_scr[hd]
            st_ref[0, hd] = st
            o = _mm(c["att"], c["v"]) + _mm_nt(c["qe"], st)
            st_scr[hd] = st * jnp.exp(c["bend"]) + _mm_tn(c["v"], c["kd"])
            cols = slice(hd * HD, (hd + 1) * HD)
            o_ref[:, cols] = o
            rstd = lax.rsqrt(jnp.mean(o * o, axis=-1, keepdims=True) + EPS)
            gg = c["g"]
            oa_ref[:, cols] = (o * rstd * g_ref[:, cols] * (gg * _sigmoid(gg))).astype(oa_ref.dtype)

    return pl.pallas_call(
        body, name="hgrn_fwd", grid=(nc,),
        in_specs=[pl.BlockSpec((ACH, 4 * DA), lambda c: (c, 0)),
                  pl.BlockSpec((1, DA), lambda c: (0, 0)),
                  pl.BlockSpec((1, DA), lambda c: (0, 0))],
        out_specs=[pl.BlockSpec((ACH, DA), lambda c: (c, 0)),
                   pl.BlockSpec((ACH, DA), lambda c: (c, 0)),
                   pl.BlockSpec((1, NH, HD, HD), lambda c: (c, 0, 0, 0))],
        out_shape=[jax.ShapeDtypeStruct((t, DA), F32), jax.ShapeDtypeStruct((t, DA), SAVE),
                   jax.ShapeDtypeStruct((nc, NH, HD, HD), F32)],
        scratch_shapes=[pltpu.VMEM((NH, HD, HD), F32)],
        compiler_params=_params(("arbitrary",)),
    )(z, lb, gain)


def _hgrn_bwd(z, lb, gain, o, states, doa):
    t = z.shape[0]
    nc = t // ACH

    def body(z_ref, lb_ref, g_ref, o_ref, st_ref, doa_ref, dz_ref, dgain_ref, dlb_ref, dst_scr):
        @pl.when(pl.program_id(0) == 0)
        def _():
            dst_scr[...] = jnp.zeros_like(dst_scr)
            dgain_ref[...] = jnp.zeros_like(dgain_ref)
            dlb_ref[...] = jnp.zeros_like(dlb_ref)

        upper = _tri(ACH, False).astype(MXU)
        for hd in range(NH):
            c = _hgrn_chunk(z_ref, lb_ref, hd)
            cols = slice(hd * HD, (hd + 1) * HD)
            o = o_ref[:, cols]
            do_a = doa_ref[:, cols]
            gain = g_ref[:, cols]
            gg = c["g"]
            sgg = _sigmoid(gg)
            silu_g = gg * sgg
            rstd = lax.rsqrt(jnp.mean(o * o, axis=-1, keepdims=True) + EPS)
            n = o * rstd
            dn = do_a * gain * silu_g
            dg = do_a * n * gain * (sgg * (1.0 + gg * (1.0 - sgg)))
            dgain_ref[:, cols] += jnp.sum(do_a * silu_g * n, axis=0, keepdims=True)
            d_o = rstd * (dn - n * jnp.mean(dn * n, axis=-1, keepdims=True))

            st = st_ref[0, hd]
            dsp = dst_scr[hd]
            datt = jnp.where(c["causal"], _mm_nt(d_o, c["v"]), 0.0)
            dv = _mm_tn(c["att"], d_o) + _mm_nt(c["kd"], dsp)
            dqt = _mm(datt, c["kt"])
            dqe = _mm(d_o, st)
            dkt = _mm_tn(datt, c["qt"])
            dkd = _mm(c["v"], dsp)
            decay = jnp.exp(c["bend"])
            dst_scr[hd] = dsp * decay + _mm_tn(d_o, c["qe"])
            dbend = decay * jnp.sum(st * dsp, axis=0, keepdims=True) + jnp.sum(dkd * c["kd"], axis=0, keepdims=True)
            db = dqt * c["qt"] + dqe * c["qe"] - dkt * c["kt"] - dkd * c["kd"]
            dqs = dqt * c["eq"] + dqe * c["eb"]
            dk = dkt * c["ek"] + dkd * c["ed"]
            dlf = _mm_exact_l(upper, db) + dbend
            df = dlf / c["f"] - dk
            sg = c["sg"]
            dlb_ref[:, cols] += jnp.sum(df * (1.0 - sg), axis=0, keepdims=True)
            dfl = df * (1.0 - c["lb"]) * sg * (1.0 - sg)
            sq, q = c["sq"], c["q"]
            dq = dqs * (sq * (1.0 + q * (1.0 - sq)))
            c0 = hd * HD
            dz_ref[:, c0:c0 + HD] = dq.astype(dz_ref.dtype)
            dz_ref[:, DA + c0:DA + c0 + HD] = dfl.astype(dz_ref.dtype)
            dz_ref[:, 2 * DA + c0:2 * DA + c0 + HD] = dv.astype(dz_ref.dtype)
            dz_ref[:, 3 * DA + c0:3 * DA + c0 + HD] = dg.astype(dz_ref.dtype)

    rev = lambda c: (nc - 1 - c, 0)
    return pl.pallas_call(
        body, name="hgrn_bwd", grid=(nc,),
        in_specs=[pl.BlockSpec((ACH, 4 * DA), rev),
                  pl.BlockSpec((1, DA), lambda c: (0, 0)),
                  pl.BlockSpec((1, DA), lambda c: (0, 0)),
                  pl.BlockSpec((ACH, DA), rev),
                  pl.BlockSpec((1, NH, HD, HD), lambda c: (nc - 1 - c, 0, 0, 0)),
                  pl.BlockSpec((ACH, DA), rev)],
        out_specs=[pl.BlockSpec((ACH, 4 * DA), rev),
                   pl.BlockSpec((1, DA), lambda c: (0, 0)),
                   pl.BlockSpec((1, DA), lambda c: (0, 0))],
        out_shape=[jax.ShapeDtypeStruct((t, 4 * DA), SAVE), jax.ShapeDtypeStruct((1, DA), F32),
                   jax.ShapeDtypeStruct((1, DA), F32)],
        scratch_shapes=[pltpu.VMEM((NH, HD, HD), F32)],
        compiler_params=_params(("arbitrary",)),
    )(z, lb, gain, o, states, doa)


def _shift_down(prev8, x, k):
    cat = jnp.concatenate([prev8, x], axis=0)
    return pltpu.roll(cat, k, axis=0)[8:, :]


def _shift_up(x, next8, k):
    n = x.shape[0]
    cat = jnp.concatenate([x, next8], axis=0)
    return pltpu.roll(cat, n + 8 - k, axis=0)[:n, :]


def _lru_gates(x, prev8, cw_ref, vec_ref, wa_ref, wx_ref):
    xs = [x, _shift_down(prev8, x, 1), _shift_down(prev8, x, 2), _shift_down(prev8, x, 3)]
    xc = vec_ref[0:1, :] + cw_ref[3:4, :] * xs[0] + cw_ref[2:3, :] * xs[1] + cw_ref[1:2, :] * xs[2] + cw_ref[0:1, :] * xs[3]
    r = _sigmoid(_mm(xc, wa_ref[...]) + vec_ref[1:2, :])
    gi = _sigmoid(_mm(xc, wx_ref[...]) + vec_ref[2:3, :])
    lam = vec_ref[3:4, :]
    sp = jnp.maximum(-lam, 0.0) + jnp.log(1.0 + jnp.exp(-jnp.abs(lam)))
    la = -LRU_C * r * sp
    a = jnp.exp(la)
    mult = jnp.sqrt(-_expm1(2.0 * la))
    return xs, xc, r, gi, sp, a, mult


def _scan_down(a, u):
    n = a.shape[0]
    row = _iota(a.shape, 0)
    s = 1
    while s < n:
        keep = row >= s
        ash = jnp.where(keep, pltpu.roll(a, s, axis=0), 1.0)
        ush = jnp.where(keep, pltpu.roll(u, s, axis=0), 0.0)
        u = a * ush + u
        a = a * ash
        s *= 2
    return a, u


def _scan_up(a, u):
    n = a.shape[0]
    row = _iota(a.shape, 0)
    s = 1
    while s < n:
        keep = row < n - s
        ash = jnp.where(keep, pltpu.roll(a, n - s, axis=0), 1.0)
        ush = jnp.where(keep, pltpu.roll(u, n - s, axis=0), 0.0)
        u = a * ush + u
        a = a * ash
        s *= 2
    return a, u


def _lru_fwd(z, cw, vec, wa, wx, tb):
    t = z.shape[0]
    xcol, gcol = (4 * DA) // DB, (4 * DA) // DB + 1

    def body(x_ref, gate_ref, cw_ref, vec_ref, wa_ref, wx_ref, ob_ref, h_ref, xprev_scr, hc_scr):
        @pl.when(pl.program_id(0) == 0)
        def _():
            xprev_scr[...] = jnp.zeros_like(xprev_scr)
            hc_scr[...] = jnp.zeros_like(hc_scr)

        x = x_ref[...]
        _, xc, _, gi, _, a, mult = _lru_gates(x, xprev_scr[...], cw_ref, vec_ref, wa_ref, wx_ref)
        acum, hloc = _scan_down(a, mult * gi * xc)
        h = hloc + acum * hc_scr[0:1, :]
        h_ref[...] = h
        hc_scr[...] = jnp.broadcast_to(_row(h, tb - 1), hc_scr.shape)
        xprev_scr[...] = x[tb - 8:, :]
        y = h * _gelu(gate_ref[...])
        ms = _mm_exact_r(y * y, _group_matrix(DB, 1.0 / GRP).astype(MXU))
        ob_ref[...] = (y * lax.rsqrt(ms + EPS) * vec_ref[4:5, :]).astype(ob_ref.dtype)

    return pl.pallas_call(
        body, name="lru_fwd", grid=(t // tb,),
        in_specs=[pl.BlockSpec((tb, DB), lambda i: (i, xcol)),
                  pl.BlockSpec((tb, DB), lambda i: (i, gcol)),
                  pl.BlockSpec((8, DB), lambda i: (0, 0)),
                  pl.BlockSpec((8, DB), lambda i: (0, 0)),
                  pl.BlockSpec((DB, DB), lambda i: (0, 0)),
                  pl.BlockSpec((DB, DB), lambda i: (0, 0))],
        out_specs=[pl.BlockSpec((tb, DB), lambda i: (i, 0)),
                   pl.BlockSpec((tb, DB), lambda i: (i, 0))],
        out_shape=[jax.ShapeDtypeStruct((t, DB), SAVE), jax.ShapeDtypeStruct((t, DB), F32)],
        scratch_shapes=[pltpu.VMEM((8, DB), F32), pltpu.VMEM((8, DB), F32)],
        compiler_params=_params(("arbitrary",)),
    )(z, z, cw, vec, wa, wx)


def _lru_bwd(z, hseq, dob, cw, vec, wa, wx, tb):
    t = z.shape[0]
    nb = t // tb
    xcol, gcol = (4 * DA) // DB, (4 * DA) // DB + 1
    per = tb // 8

    def body(x_ref, xh_ref, gate_ref, h_ref, hh_ref, dob_ref, cw_ref, vec_ref, wa_ref, wx_ref,
             dz_ref, dcw_ref, dvec_ref, dwa_ref, dwx_ref, gc_scr, an_scr, dxc_scr):
        step = pl.program_id(0)
        blk = nb - 1 - step

        @pl.when(step == 0)
        def _():
            for ref in (gc_scr, an_scr, dxc_scr, dcw_ref, dvec_ref, dwa_ref, dwx_ref):
                ref[...] = jnp.zeros_like(ref)

        first = (blk > 0).astype(F32)
        x = x_ref[...]
        xs, xc, r, gi, sp, a, mult = _lru_gates(x, xh_ref[...] * first, cw_ref, vec_ref, wa_ref, wx_ref)
        h = h_ref[...]
        hprev = _shift_down(hh_ref[...] * first, h, 1)
        ge, dge = _gelu_and_grad(gate_ref[...])
        y = h * ge
        gmat = _group_matrix(DB, 1.0 / GRP).astype(MXU)
        rstd = lax.rsqrt(_mm_exact_r(y * y, gmat) + EPS)
        n = y * rstd
        d_ob = dob_ref[...]
        dn = d_ob * vec_ref[4:5, :]
        dvec_ref[4:5, :] += jnp.sum(d_ob * n, axis=0, keepdims=True)
        dy = rstd * (dn - n * _mm_exact_r(dn * n, gmat))
        dh = dy * ge
        dgate = dy * h * dge

        row = _iota(a.shape, 0)
        anext = jnp.where(row == tb - 1, an_scr[0:1, :], pltpu.roll(a, tb - 1, axis=0))
        acum, gloc = _scan_up(anext, dh)
        g = gloc + acum * gc_scr[0:1, :]
        gc_scr[...] = jnp.broadcast_to(_row(g, 0), gc_scr.shape)
        an_scr[...] = jnp.broadcast_to(_row(a, 0), an_scr.shape)

        da = g * hprev
        dmult = g * gi * xc
        dgi = g * mult * xc
        dxc = g * mult * gi
        dla = da * a - dmult * (a * a) / mult
        dr = dla * (-LRU_C * sp)
        dsp = jnp.sum(dla * (-LRU_C * r), axis=0, keepdims=True)
        lam = vec_ref[3:4, :]
        dvec_ref[3:4, :] += -dsp * _sigmoid(-lam)
        dpa = dr * r * (1.0 - r)
        dpx = dgi * gi * (1.0 - gi)
        dwa_ref[...] += _mm_tn(xc, dpa)
        dwx_ref[...] += _mm_tn(xc, dpx)
        dvec_ref[1:2, :] += jnp.sum(dpa, axis=0, keepdims=True)
        dvec_ref[2:3, :] += jnp.sum(dpx, axis=0, keepdims=True)
        dxc = dxc + _mm_nt(dpa, wa_ref[...]) + _mm_nt(dpx, wx_ref[...])
        dvec_ref[0:1, :] += jnp.sum(dxc, axis=0, keepdims=True)
        for tap in range(4):
            dcw_ref[tap:tap + 1, :] += jnp.sum(dxc * xs[3 - tap], axis=0, keepdims=True)
        nxt = dxc_scr[...]
        dx = (cw_ref[3:4, :] * dxc + cw_ref[2:3, :] * _shift_up(dxc, nxt, 1)
              + cw_ref[1:2, :] * _shift_up(dxc, nxt, 2) + cw_ref[0:1, :] * _shift_up(dxc, nxt, 3))
        dxc_scr[...] = dxc[:8, :]
        dz_ref[:, :DB] = dx.astype(dz_ref.dtype)
        dz_ref[:, DB:] = dgate.astype(dz_ref.dtype)

    def halo(col):
        return lambda s: (jnp.maximum((nb - 1 - s) * per - 1, 0), col)

    const = lambda s: (0, 0)
    return pl.pallas_call(
        body, name="lru_bwd", grid=(nb,),
        in_specs=[pl.BlockSpec((tb, DB), lambda s: (nb - 1 - s, xcol)),
                  pl.BlockSpec((8, DB), halo(xcol)),
                  pl.BlockSpec((tb, DB), lambda s: (nb - 1 - s, gcol)),
                  pl.BlockSpec((tb, DB), lambda s: (nb - 1 - s, 0)),
                  pl.BlockSpec((8, DB), halo(0)),
                  pl.BlockSpec((tb, DB), lambda s: (nb - 1 - s, 0)),
                  pl.BlockSpec((8, DB), const), pl.BlockSpec((8, DB), const),
                  pl.BlockSpec((DB, DB), const), pl.BlockSpec((DB, DB), const)],
        out_specs=[pl.BlockSpec((tb, 2 * DB), lambda s: (nb - 1 - s, 0)),
                   pl.BlockSpec((8, DB), const), pl.BlockSpec((8, DB), const),
                   pl.BlockSpec((DB, DB), const), pl.BlockSpec((DB, DB), const)],
        out_shape=[jax.ShapeDtypeStruct((t, 2 * DB), SAVE), jax.ShapeDtypeStruct((8, DB), F32),
                   jax.ShapeDtypeStruct((8, DB), F32), jax.ShapeDtypeStruct((DB, DB), F32),
                   jax.ShapeDtypeStruct((DB, DB), F32)],
        scratch_shapes=[pltpu.VMEM((8, DB), F32), pltpu.VMEM((8, DB), F32), pltpu.VMEM((8, DB), F32)],
        compiler_params=_params(("arbitrary",)),
    )(z, z, z, hseq, hseq, dob, cw, vec, wa, wx)


def _sgu_chunk(u_in, v_in, w_ref, bias, gmat):
    uu, duu = _gelu_and_grad(u_in)
    vv, dvv = _gelu_and_grad(v_in)
    mu = _mm_exact_r(vv, gmat)
    dlt = vv - mu
    rstd_v = lax.rsqrt(_mm_exact_r(dlt * dlt, gmat) + EPS)
    vn = dlt * rstd_v
    col = _iota((CCH, DC), 1) // GRP
    causal = _iota((CCH, CCH), 0) >= _iota((CCH, CCH), 1)
    ws = [jnp.where(causal, w_ref[g], 0.0) for g in range(DC // GRP)]
    zz = bias
    for g, w in enumerate(ws):
        zz = zz + jnp.where(col == g, _mm(w, vn), 0.0)
    return uu, duu, dvv, rstd_v, vn, zz, ws, col, causal


def _sgu_fwd(z, w, bias, gain, tb):
    t = z.shape[0]
    ucol, vcol = (4 * DA + 2 * DB) // DC, (4 * DA + 2 * DB) // DC + 1

    def body(u_ref, v_ref, w_ref, b_ref, g_ref, oc_ref):
        gmat = _group_matrix(DC, 1.0 / GRP).astype(MXU)
        for ch in range(tb // CCH):
            rows = slice(ch * CCH, (ch + 1) * CCH)
            uu, _, _, _, _, zz, _, _, _ = _sgu_chunk(u_ref[rows, :], v_ref[rows, :], w_ref, b_ref[...], gmat)
            y = uu * zz
            ms = _mm_exact_r(y * y, gmat)
            oc_ref[rows, :] = (y * lax.rsqrt(ms + EPS) * g_ref[...]).astype(oc_ref.dtype)

    const = lambda i: (0, 0)
    return pl.pallas_call(
        body, name="sgu_fwd", grid=(t // tb,),
        in_specs=[pl.BlockSpec((tb, DC), lambda i: (i, ucol)),
                  pl.BlockSpec((tb, DC), lambda i: (i, vcol)),
                  pl.BlockSpec((DC // GRP, CCH, CCH), lambda i: (0, 0, 0)),
                  pl.BlockSpec((CCH, DC), const), pl.BlockSpec((1, DC), const)],
        out_specs=pl.BlockSpec((tb, DC), lambda i: (i, 0)),
        out_shape=jax.ShapeDtypeStruct((t, DC), SAVE),
        compiler_params=_params(("parallel",)),
    )(z, z, w, bias, gain)


def _sgu_bwd(z, doc, w, bias, gain, tb):
    t = z.shape[0]
    nb = t // tb
    ucol, vcol = (4 * DA + 2 * DB) // DC, (4 * DA + 2 * DB) // DC + 1
    ng = DC // GRP

    def body(u_ref, v_ref, doc_ref, w_ref, b_ref, g_ref, dz_ref, dw_ref, dbias_ref, dgain_ref, dbsum_scr):
        i = pl.program_id(0)

        @pl.when(i == 0)
        def _():
            for ref in (dw_ref, dgain_ref, dbsum_scr):
                ref[...] = jnp.zeros_like(ref)

        gmat = _group_matrix(DC, 1.0 / GRP).astype(MXU)
        for ch in range(tb // CCH):
            rows = slice(ch * CCH, (ch + 1) * CCH)
            uu, duu, dvv, rstd_v, vn, zz, ws, col, causal = _sgu_chunk(
                u_ref[rows, :], v_ref[rows, :], w_ref, b_ref[...], gmat)
            y = uu * zz
            rstd = lax.rsqrt(_mm_exact_r(y * y, gmat) + EPS)
            n = y * rstd
            d_oc = doc_ref[rows, :]
            dn = d_oc * g_ref[...]
            dgain_ref[0:1, :] += jnp.sum(d_oc * n, axis=0, keepdims=True)
            dy = rstd * (dn - n * _mm_exact_r(dn * n, gmat))
            dzz = dy * uu
            dz_ref[rows, :DC] = (dy * zz * duu).astype(dz_ref.dtype)
            dbsum_scr[...] += dzz
            dvn = jnp.zeros_like(dzz)
            for g in range(ng):
                sel = col == g
                dvn = dvn + jnp.where(sel, _mm_tn(ws[g], dzz), 0.0)
                dw_ref[g] += jnp.where(causal, _mm_nt(jnp.where(sel, dzz, 0.0), vn), 0.0)
            dv = rstd_v * (dvn - _mm_exact_r(dvn, gmat) - vn * _mm_exact_r(dvn * vn, gmat))
            dz_ref[rows, DC:] = (dv * dvv).astype(dz_ref.dtype)

        @pl.when(i == nb - 1)
        def _():
            dbias_ref[...] = _mm_exact_r(dbsum_scr[...], _group_matrix(DC, 1.0).astype(MXU))

    const = lambda i: (0, 0)
    return pl.pallas_call(
        body, name="sgu_bwd", grid=(nb,),
        in_specs=[pl.BlockSpec((tb, DC), lambda i: (i, ucol)),
                  pl.BlockSpec((tb, DC), lambda i: (i, vcol)),
                  pl.BlockSpec((tb, DC), lambda i: (i, 0)),
                  pl.BlockSpec((ng, CCH, CCH), lambda i: (0, 0, 0)),
                  pl.BlockSpec((CCH, DC), const), pl.BlockSpec((1, DC), const)],
        out_specs=[pl.BlockSpec((tb, 2 * DC), lambda i: (i, 0)),
                   pl.BlockSpec((ng, CCH, CCH), lambda i: (0, 0, 0)),
                   pl.BlockSpec((CCH, DC), const), pl.BlockSpec((8, DC), const)],
        out_shape=[jax.ShapeDtypeStruct((t, 2 * DC), SAVE), jax.ShapeDtypeStruct((ng, CCH, CCH), F32),
                   jax.ShapeDtypeStruct((CCH, DC), F32), jax.ShapeDtypeStruct((8, DC), F32)],
        scratch_shapes=[pltpu.VMEM((CCH, DC), F32)],
        compiler_params=_params(("arbitrary",)),
    )(z, z, doc, w, bias, gain)


def _head(h, gain, target, tm):
    t = h.shape[0]

    def body(h_ref, g_ref, t_ref, dh_ref, loss_ref, dgain_ref):
        @pl.when(pl.program_id(0) == 0)
        def _():
            loss_ref[...] = jnp.zeros_like(loss_ref)
            dgain_ref[...] = jnp.zeros_like(dgain_ref)

        hh = h_ref[...]
        gain = g_ref[...]
        rstd = lax.rsqrt(jnp.mean(hh * hh, axis=-1, keepdims=True) + EPS)
        xhat = hh * rstd
        err = xhat * gain - t_ref[...]
        per_tok = jnp.mean(err * err, axis=-1, keepdims=True)
        loss_ref[...] += 0.5 * jnp.sum(per_tok, axis=0, keepdims=True)
        dy = err * (1.0 / D)
        dgain_ref[...] += jnp.sum(dy * xhat, axis=0, keepdims=True)
        dxh = dy * gain
        dh_ref[...] = rstd * (dxh - xhat * jnp.mean(dxh * xhat, axis=-1, keepdims=True))

    return pl.pallas_call(
        body, name="head", grid=(t // tm,),
        in_specs=[pl.BlockSpec((tm, D), lambda i: (i, 0)),
                  pl.BlockSpec((1, D), lambda i: (0, 0)),
                  pl.BlockSpec((tm, D), lambda i: (i, 0))],
        out_specs=[pl.BlockSpec((tm, D), lambda i: (i, 0)),
                   pl.BlockSpec((1, 128), lambda i: (0, 0)),
                   pl.BlockSpec((1, D), lambda i: (0, 0))],
        out_shape=[jax.ShapeDtypeStruct((t, D), F32), jax.ShapeDtypeStruct((1, 128), F32),
                   jax.ShapeDtypeStruct((1, D), F32)],
        compiler_params=_params(("arbitrary",)),
    )(h, gain, target)


def _adamw(w, g, m, v):
    m = ADAM_B1 * m + (1.0 - ADAM_B1) * g
    v = ADAM_B2 * v + (1.0 - ADAM_B2) * (g * g)
    m_hat = m / (1.0 - ADAM_B1 ** ADAM_STEP)
    v_hat = v / (1.0 - ADAM_B2 ** ADAM_STEP)
    delta = -ADAM_LR * (m_hat / (jnp.sqrt(v_hat) + ADAM_EPS) + ADAM_WD * w)
    return delta, m, v


def _adamw_big(recv, params, tr, name):
    depth, _, _, cols_p = recv.shape
    np_ = len(params)
    rows, colss = params[0][0].shape[1], [p[0].shape[2] for p in params]

    def body(*refs):
        r_ref = refs[0]
        ins, outs = refs[1:1 + 3 * np_], refs[1 + 3 * np_:]
        g = r_ref[0].astype(F32)
        for k in range(1, NDEV):
            g = g + r_ref[k].astype(F32)
        for p in range(np_):
            col0 = params[p][3]
            gp = g[:, col0:col0 + colss[p]]
            w_ref, m_ref, v_ref = ins[3 * p:3 * p + 3]
            delta, m, v = _adamw(w_ref[...], gp, m_ref[...], v_ref[...])
            g_out, d_out, m_out, v_out = outs[4 * p:4 * p + 4]
            g_out[...] = gp
            d_out[...] = delta
            m_out[...] = m
            v_out[...] = v

    in_specs = [pl.BlockSpec((None, NDEV, tr, cols_p), lambda l, i: (l, 0, i, 0))]
    out_specs, out_shape, args = [], [], [recv]
    for p in range(np_):
        spec = pl.BlockSpec((None, tr, colss[p]), lambda l, i: (l, i, 0))
        in_specs += [spec] * 3
        out_specs += [spec] * 4
        out_shape += [jax.ShapeDtypeStruct((depth, rows, colss[p]), F32)] * 4
        args += list(params[p][:3])
    res = pl.pallas_call(
        body, name=name, grid=(depth, rows // tr),
        in_specs=in_specs, out_specs=out_specs, out_shape=out_shape,
        compiler_params=_params(("parallel", "parallel")),
    )(*args)
    return [tuple(res[4 * p:4 * p + 4]) for p in range(np_)]


def _sum_devices(recv):
    _, r, _ = recv.shape

    def body(r_ref, out_ref):
        g = r_ref[0]
        for k in range(1, NDEV):
            g = g + r_ref[k]
        out_ref[...] = g

    return pl.pallas_call(body, name="sum_devices", out_shape=jax.ShapeDtypeStruct((r, 128), F32))(recv)


def _adamw_small(w, g, m, v):
    def body(w_ref, g_ref, m_ref, v_ref, d_out, m_out, v_out):
        delta, m_, v_ = _adamw(w_ref[...], g_ref[...], m_ref[...], v_ref[...])
        d_out[...] = delta
        m_out[...] = m_
        v_out[...] = v_

    return pl.pallas_call(body, name="adamw_small", out_shape=[jax.ShapeDtypeStruct(w.shape, F32)] * 3)(w, g, m, v)


def _pack(arrs):
    flat = jnp.concatenate([a.reshape(-1) for a in arrs])
    pad = (-flat.shape[0]) % 1024
    return jnp.pad(flat, (0, pad)).reshape(-1, 128)


def _unpack(buf, like):
    flat = buf.reshape(-1)
    out, off = [], 0
    for a in like:
        out.append(flat[off:off + a.size].reshape(a.shape))
        off += a.size
    return out


def _block_diag(w):
    nb, bd, _ = w.shape
    eye = jnp.eye(nb, dtype=w.dtype)
    return (eye[:, None, :, None] * w[:, :, None, :]).reshape(nb * bd, nb * bd)


def _diag_blocks(w):
    nb = w.shape[0] // GRP
    return jnp.stack([w[g * GRP:(g + 1) * GRP, g * GRP:(g + 1) * GRP] for g in range(nb)])


def _pad_cols(a, n):
    return jnp.pad(a, ((0, 0), (0, n - a.shape[1])))


SMALL = ['ffn1_norm', 'mix_norm', 'hgrn_lb_logits', 'hgrn_norm', 'conv_b', 'lru_wa', 'lru_ba', 'lru_wx', 'lru_bx',
         'lru_lambda', 'lru_norm', 'sgu_w', 'sgu_b', 'sgu_norm', 'ffn2_norm', 'final_norm']
NAMES = ['ffn1_norm', 'ffn1_wg', 'ffn1_wu', 'ffn1_wd', 'mix_norm', 'w_in', 'hgrn_lb_logits', 'hgrn_norm', 'conv_w',
         'conv_b', 'lru_wa', 'lru_ba', 'lru_wx', 'lru_bx', 'lru_lambda', 'lru_norm', 'sgu_w', 'sgu_b', 'sgu_norm',
         'w_out', 'ffn2_norm', 'ffn2_wg', 'ffn2_wu', 'ffn2_wd', 'final_norm']


def _step(x, target, w, m, v):
    depth = w['ffn1_wg'].shape[0]
    t = x.shape[1]
    h = x.reshape(t, D)
    target = target.reshape(t, D)
    tm_f, tm_b, tb = min(TM_F, t), min(TM_B, t), min(TB, t)
    my = 4 * lax.axis_index("x") + 2 * lax.axis_index("y") + lax.axis_index("c")

    cw_all = _all_gather([w['conv_w']], "gather_conv")[0]
    conv_w = jnp.moveaxis(cw_all, 0, 2).reshape(depth, 4, DB)
    lbs, lb_soft = _lower_bounds(w['hgrn_lb_logits'])

    def row(a):
        return a.reshape(1, -1)

    saved = []
    for l in range(depth):
        shards = []
        for f in ('ffn1', 'ffn2'):
            shards.append(jnp.concatenate([_pad_cols(w[f + '_wg'][l], FFP), _pad_cols(w[f + '_wu'][l], FFP)],
                                          axis=1).astype(WIRE))
            shards.append(jnp.pad(w[f + '_wd'][l], ((0, FFP - FFS), (0, 0))).astype(WIRE))
        shards += [w['w_in'][l].astype(WIRE), w['w_out'][l].astype(WIRE)]
        wgu1, wd1, wgu2, wd2, win, wout = _all_gather(shards, "gather_weights")
        wout = wout.reshape(D, D)

        s = dict(wgu1=wgu1, wd1=wd1, wgu2=wgu2, wd2=wd2, win=win, wout=wout, h0=h)
        h, s['xn1'], s['ab1'] = _ffn_fwd(h, row(w['ffn1_norm'][l]), wgu1, wd1, tm_f)
        s['h1'] = h
        z, s['xnm'] = _inproj_fwd(h, row(w['mix_norm'][l]), win, tm_f)
        s['z'] = z
        s['o'], oa, s['states'] = _hgrn_fwd(z, row(lbs[l]), row(w['hgrn_norm'][l]))
        s['cw'] = jnp.pad(conv_w[l], ((0, 4), (0, 0)))
        s['vec'] = jnp.concatenate([row(w['conv_b'][l]), row(w['lru_ba'][l]), row(w['lru_bx'][l]),
                                    row(w['lru_lambda'][l]), row(w['lru_norm'][l]), jnp.zeros((3, DB), F32)])
        s['wa'], s['wx'] = _block_diag(w['lru_wa'][l]), _block_diag(w['lru_wx'][l])
        ob, s['hseq'] = _lru_fwd(z, s['cw'], s['vec'], s['wa'], s['wx'], tb)
        s['bias'] = jnp.repeat(w['sgu_b'][l].T, GRP, axis=1)
        oc = _sgu_fwd(z, w['sgu_w'][l], s['bias'], row(w['sgu_norm'][l]), tb)
        s['oa'], s['ob'], s['oc'] = oa, ob, oc
        h = _outproj_fwd(h, oa, ob, oc, wout, tm_f)
        s['h2'] = h
        h, s['xn2'], s['ab2'] = _ffn_fwd(h, row(w['ffn2_norm'][l]), wgu2, wd2, tm_f)
        saved.append(s)

    dh, loss_part, g_final = _head(h, row(w['final_norm']), target, tm_f)
    loss = lax.psum(loss_part[0, 0], ("x", "y", "c"))

    big = {k: [None] * depth for k in ('wgu1', 'wd1', 'wgu2', 'wd2', 'win', 'wout')}
    small = {k: [None] * depth for k in SMALL if k != 'final_norm'}
    dconv = [None] * depth
    dlb = [None] * depth
    for l in reversed(range(depth)):
        s = saved[l]
        dh, big['wgu2'][l], big['wd2'][l], g = _ffn_bwd(dh, s['h2'], row(w['ffn2_norm'][l]), s['xn2'], s['ab2'],
                                                         s['wgu2'], s['wd2'], tm_b)
        small['ffn2_norm'][l] = g
        doa, dob, doc, dwout = _outproj_bwd(dh, s['oa'], s['ob'], s['oc'], s['wout'], tm_f)
        big['wout'][l] = dwout.reshape(NDEV, D // NDEV, D)
        dza, g_hn, dlb[l] = _hgrn_bwd(s['z'], row(lbs[l]), row(w['hgrn_norm'][l]), s['o'], s['states'], doa)
        small['hgrn_norm'][l] = g_hn
        dzb, dcw, dvec, dwa, dwx = _lru_bwd(s['z'], s['hseq'], dob, s['cw'], s['vec'], s['wa'], s['wx'], tb)
        dconv[l] = dcw[:4]
        small['conv_b'][l], small['lru_ba'][l], small['lru_bx'][l] = dvec[0], dvec[1].reshape(4, GRP), dvec[2].reshape(4, GRP)
        small['lru_lambda'][l], small['lru_norm'][l] = dvec[3], dvec[4]
        small['lru_wa'][l], small['lru_wx'][l] = _diag_blocks(dwa), _diag_blocks(dwx)
        dzc, dsw, dbias, dgc = _sgu_bwd(s['z'], doc, w['sgu_w'][l], s['bias'], row(w['sgu_norm'][l]), tb)
        small['sgu_w'][l], small['sgu_b'][l], small['sgu_norm'][l] = dsw, dbias[:, ::GRP].T, dgc[0]
        dz = jnp.concatenate([dza, dzb, dzc], axis=1)
        dh, big['win'][l], g = _inproj_bwd(dh, dz, s['h1'], row(w['mix_norm'][l]), s['xnm'], s['win'], tm_b)
        small['mix_norm'][l] = g
        dh, big['wgu1'][l], big['wd1'][l], g = _ffn_bwd(dh, s['h0'], row(w['ffn1_norm'][l]), s['xn1'], s['ab1'],
                                                         s['wgu1'], s['wd1'], tm_b)
        small['ffn1_norm'][l] = g
    grad_x = dh.reshape(1, t, D)
    small['hgrn_lb_logits'] = list(_lower_bounds_bwd(lb_soft, jnp.concatenate(dlb, axis=0)))

    kinds = ('wgu1', 'wd1', 'wgu2', 'wd2', 'win', 'wout')
    recv = dict(zip(kinds, _exchange_blocks([big[k] for k in kinds], "exchange_grads")))

    out = {}
    for f, ku, kd in (('ffn1', 'wgu1', 'wd1'), ('ffn2', 'wgu2', 'wd2')):
        res = _adamw_big(recv[ku], [(w[f + '_wg'], m[f + '_wg'], v[f + '_wg'], 0),
                                    (w[f + '_wu'], m[f + '_wu'], v[f + '_wu'], FFP)], 256, "adamw_wgu")
        out[f + '_wg'], out[f + '_wu'] = res
        out[f + '_wd'] = _adamw_big(recv[kd], [(w[f + '_wd'], m[f + '_wd'], v[f + '_wd'], 0)], FFS // 2, "adamw_wd")[0]
    out['w_in'] = _adamw_big(recv['win'], [(w['w_in'], m['w_in'], v['w_in'], 0)], 256, "adamw_win")[0]
    out['w_out'] = _adamw_big(recv['wout'], [(w['w_out'], m['w_out'], v['w_out'], 0)], D // NDEV, "adamw_wout")[0]

    parts = [jnp.stack([small[k][l].reshape(w[k].shape[1:]) for l in range(depth)]) for k in SMALL if k != 'final_norm']
    parts += [g_final.reshape(D), jnp.stack(dconv)]
    total = _sum_devices(_all_gather([_pack(parts)], "gather_small")[0])
    like = [w[k] for k in SMALL] + [jax.ShapeDtypeStruct((depth, 4, DB), F32)]
    grads = _unpack(total, like)
    gsmall = dict(zip(SMALL, grads[:-1]))
    gsmall['conv_w'] = lax.dynamic_slice_in_dim(grads[-1], my * (DB // NDEV), DB // NDEV, axis=2)
    keys = SMALL + ['conv_w']
    dl, mm, vv = _adamw_small(_pack([w[k] for k in keys]), _pack([gsmall[k] for k in keys]),
                              _pack([m[k] for k in keys]), _pack([v[k] for k in keys]))
    like = [w[k] for k in keys]
    for k, d_, m_, v_ in zip(keys, _unpack(dl, like), _unpack(mm, like), _unpack(vv, like)):
        out[k] = (gsmall[k], d_, m_, v_)

    return (loss, grad_x, *[out[k][0] for k in NAMES], *[out[k][1] for k in NAMES],
            *[out[k][2] for k in NAMES], *[out[k][3] for k in NAMES])


def kernel(x, ffn1_norm, ffn1_wg, ffn1_wu, ffn1_wd, mix_norm, w_in, hgrn_lb_logits, hgrn_norm, conv_w, conv_b, lru_wa, lru_ba, lru_wx, lru_bx, lru_lambda, lru_norm, sgu_w, sgu_b, sgu_norm, w_out, ffn2_norm, ffn2_wg, ffn2_wu, ffn2_wd, final_norm, loss_target, m_ffn1_norm, m_ffn1_wg, m_ffn1_wu, m_ffn1_wd, m_mix_norm, m_w_in, m_hgrn_lb_logits, m_hgrn_norm, m_conv_w, m_conv_b, m_lru_wa, m_lru_ba, m_lru_wx, m_lru_bx, m_lru_lambda, m_lru_norm, m_sgu_w, m_sgu_b, m_sgu_norm, m_w_out, m_ffn2_norm, m_ffn2_wg, m_ffn2_wu, m_ffn2_wd, m_final_norm, v_ffn1_norm, v_ffn1_wg, v_ffn1_wu, v_ffn1_wd, v_mix_norm, v_w_in, v_hgrn_lb_logits, v_hgrn_norm, v_conv_w, v_conv_b, v_lru_wa, v_lru_ba, v_lru_wx, v_lru_bx, v_lru_lambda, v_lru_norm, v_sgu_w, v_sgu_b, v_sgu_norm, v_w_out, v_ffn2_norm, v_ffn2_wg, v_ffn2_wu, v_ffn2_wd, v_final_norm):
    args = locals()
    w = {k: args[k] for k in NAMES}
    m = {k: args['m_' + k] for k in NAMES}
    v = {k: args['v_' + k] for k in NAMES}
    return _step(x, loss_target, w, m, v)
```

```python
import functools

import jax
import jax.numpy as jnp
from jax import lax
from jax.experimental import pallas as pl
from jax.experimental.pallas import tpu as pltpu

F32 = jnp.float32
MXU = jnp.bfloat16
SAVE = jnp.bfloat16
WIRE = jnp.bfloat16

NDEV = 8
D = 1024
FF = 2816
FFS = FF // NDEV
FFP = 384
DIN = 3072
DINS = DIN // NDEV
DA, DB, DC = 512, 256, 256
HD = 128
NH = DA // HD
ACH = 64
CCH = 128
GRP = 64
EPS = 1e-6
LRU_C = 8.0
VMEM_LIMIT = 56 * 1024 * 1024
TM_F = 512
TM_B = 256
TB = 512

ADAM_LR, ADAM_B1, ADAM_B2, ADAM_EPS, ADAM_WD, ADAM_STEP = 0.001, 0.9, 0.999, 1e-08, 0.01, 10

MESH = pl.DeviceIdType.MESH


def _mm(a, b):
    return jnp.dot(a.astype(MXU), b.astype(MXU), preferred_element_type=F32)


def _mm_nt(a, b):
    return lax.dot_general(a.astype(MXU), b.astype(MXU), (((1,), (1,)), ((), ())), preferred_element_type=F32)


def _mm_tn(a, b):
    return lax.dot_general(a.astype(MXU), b.astype(MXU), (((0,), (0,)), ((), ())), preferred_element_type=F32)


def _split3(x):
    x1 = x.astype(MXU)
    r1 = x - x1.astype(F32)
    x2 = r1.astype(MXU)
    r2 = r1 - x2.astype(F32)
    return x1, x2, r2.astype(MXU)


def _mm_exact_l(c, x):
    x1, x2, x3 = _split3(x)
    return _mm(c, x1) + _mm(c, x2) + _mm(c, x3)


def _mm_exact_r(x, c):
    x1, x2, x3 = _split3(x)
    return _mm(x1, c) + _mm(x2, c) + _mm(x3, c)


def _sigmoid(x):
    return 1.0 / (1.0 + jnp.exp(-x))


def _gelu(x):
    c, k = 0.7978845608028654, 0.044715
    th = jnp.tanh(c * (x + k * x * x * x))
    return 0.5 * x * (1.0 + th)


def _gelu_and_grad(x):
    c, k = 0.7978845608028654, 0.044715
    th = jnp.tanh(c * (x + k * x * x * x))
    g = 0.5 * x * (1.0 + th)
    dg = 0.5 * (1.0 + th) + 0.5 * x * (1.0 - th * th) * c * (1.0 + 3.0 * k * x * x)
    return g, dg


def _expm1(x):
    series = x * (1.0 + x * (0.5 + x * (1.0 / 6.0 + x * (1.0 / 24.0 + x * (1.0 / 120.0)))))
    return jnp.where(jnp.abs(x) < 0.05, series, jnp.exp(x) - 1.0)


def _iota(shape, dim):
    return lax.broadcasted_iota(jnp.int32, shape, dim)


def _tri(n, lower):
    r, c = _iota((n, n), 0), _iota((n, n), 1)
    return jnp.where((r >= c) if lower else (r <= c), 1.0, 0.0).astype(F32)


def _group_matrix(n, value):
    r, c = _iota((n, n), 0), _iota((n, n), 1)
    return jnp.where((r // GRP) == (c // GRP), value, 0.0).astype(F32)


def _row(x, k):
    r = _iota(x.shape, 0)
    return jnp.sum(jnp.where(r == k, x, 0.0), axis=0, keepdims=True)


def _rms_bwd(dxn, hh, gain):
    rstd = lax.rsqrt(jnp.mean(hh * hh, axis=-1, keepdims=True) + EPS)
    xhat = hh * rstd
    dxh = dxn * gain
    dh = rstd * (dxh - xhat * jnp.mean(dxh * xhat, axis=-1, keepdims=True))
    return dh, jnp.sum(dxn * xhat, axis=0, keepdims=True)


def _params(sem):
    return pltpu.CompilerParams(dimension_semantics=sem, vmem_limit_bytes=VMEM_LIMIT)


def _all_gather(arrs, name):
    n = len(arrs)

    def body(*refs):
        ins, outs = refs[:n], refs[n:2 * n]
        send_sems, recv_sems, local_sems = refs[2 * n:]
        x, y, c = lax.axis_index("x"), lax.axis_index("y"), lax.axis_index("c")
        me, sibling = (x, y, c), (x, y, 1 - c)
        chips = [(1 - x, y), (x, 1 - y), (1 - x, 1 - y)]

        def slot(px, py, pc):
            return 4 * px + 2 * py + pc

        def copy(a, k, block, to, src=None):
            dst = outs[a].at[slot(*block)]
            return pltpu.make_async_remote_copy(
                src_ref=dst if src is None else src, dst_ref=dst,
                send_sem=send_sems.at[a * 7 + k], recv_sem=recv_sems.at[a * 7 + k],
                device_id=to, device_id_type=MESH)

        started = []
        for a in range(n):
            mine = pltpu.make_async_copy(ins[a], outs[a].at[slot(*me)], local_sems.at[a])
            mine.start()
            started.append(mine)
        first = []
        for a in range(n):
            first.append(copy(a, 0, me, sibling, src=ins[a]))
            first += [copy(a, 1 + j, me, (*chip, c), src=ins[a]) for j, chip in enumerate(chips)]
        for cp in first:
            cp.start()
        passed = []
        for a in range(n):
            for j, chip in enumerate(chips):
                copy(a, 1 + j, (*chip, c), me).wait_recv()
                fwd = copy(a, 4 + j, (*chip, c), sibling)
                fwd.start()
                passed.append(fwd)
        for a in range(n):
            copy(a, 0, sibling, me).wait_recv()
            for j, chip in enumerate(chips):
                copy(a, 4 + j, (*chip, 1 - c), me).wait_recv()
        for cp in first + passed:
            cp.wait_send()
        for mine in started:
            mine.wait()

    hbm = pl.BlockSpec(memory_space=pl.ANY)
    return pl.pallas_call(
        body, name=name,
        out_shape=[jax.ShapeDtypeStruct((NDEV,) + a.shape, a.dtype) for a in arrs],
        in_specs=[hbm] * n, out_specs=[hbm] * n,
        scratch_shapes=[pltpu.SemaphoreType.DMA((7 * n,)), pltpu.SemaphoreType.DMA((7 * n,)),
                        pltpu.SemaphoreType.DMA((n,))],
    )(*arrs)


def _peers():
    x, y, c = lax.axis_index("x"), lax.axis_index("y"), lax.axis_index("c")
    peers = [(x ^ ((k >> 2) & 1), y ^ ((k >> 1) & 1), c ^ (k & 1)) for k in range(1, NDEV)]
    return (x, y, c), 4 * x + 2 * y + c, peers


_HBM = pl.BlockSpec(memory_space=pltpu.HBM)
_SEM = pl.BlockSpec(memory_space=pltpu.SEMAPHORE)
_EFFECT = pltpu.SideEffectType.DATAFLOW_SIDE_EFFECTING


def _transfer_start(arrs, gather, name, deps=()):
    n, nd = len(arrs), len(deps)
    shapes = [((NDEV,) + a.shape) if gather else a.shape for a in arrs]

    def body(*refs):
        ins, lands = refs[:n], refs[n:2 * n]
        send_sems, recv_sems, local_sems = refs[2 * n + nd:2 * n + nd + 3]
        token = refs[-1]
        _, my, peers = _peers()
        for a in range(n):
            own = ins[a] if gather else ins[a].at[my]
            pltpu.make_async_copy(own, lands[a].at[my], local_sems.at[a]).start()
        for a in range(n):
            for peer in peers:
                src = ins[a] if gather else ins[a].at[4 * peer[0] + 2 * peer[1] + peer[2]]
                pltpu.make_async_remote_copy(
                    src_ref=src, dst_ref=lands[a].at[my], send_sem=send_sems.at[a], recv_sem=recv_sems.at[a],
                    device_id=peer, device_id_type=MESH).start()
        token[...] = jnp.zeros_like(token)

    out_shape = [pltpu.SemaphoreType.DMA((n,))] * 3
    out_shape += [pltpu.HBM(a.shape, a.dtype) for a in arrs]
    out_shape += [pltpu.HBM(s, a.dtype) for s, a in zip(shapes, arrs)]
    out_shape += [jax.ShapeDtypeStruct((8, 128), F32)]
    operands = [pltpu.with_memory_space_constraint(a, pltpu.HBM) for a in arrs]
    operands += [pltpu.with_memory_space_constraint(lax.empty(s, a.dtype), pltpu.HBM) for s, a in zip(shapes, arrs)]
    res = pl.pallas_call(
        body, name=name, out_shape=out_shape,
        in_specs=[_HBM] * (2 * n) + [pl.BlockSpec(memory_space=pl.ANY)] * nd,
        out_specs=[_SEM] * 3 + [_HBM] * (2 * n) + [pl.BlockSpec(memory_space=pltpu.VMEM)],
        input_output_aliases={i: 3 + i for i in range(2 * n)},
        compiler_params=pltpu.CompilerParams(has_side_effects=_EFFECT),
    )(*operands, *deps)
    return dict(sems=res[:3], src=res[3:3 + n], lands=res[3 + n:3 + 2 * n], token=res[-1], n=n)


def _transfer_wait(handle, after, name):
    n = handle["n"]

    def body(*refs):
        srcs, lands = refs[:n], refs[n:2 * n]
        send_sems, recv_sems, local_sems = refs[2 * n:2 * n + 3]
        me, _, _ = _peers()
        for a in range(n):
            seven = lands[a].at[pl.ds(0, NDEV - 1)]
            both = pltpu.make_async_remote_copy(
                src_ref=seven, dst_ref=seven, send_sem=send_sems.at[a], recv_sem=recv_sems.at[a],
                device_id=me, device_id_type=MESH)
            both.wait_send()
            both.wait_recv()
            pltpu.make_async_copy(lands[a].at[0], lands[a].at[1], local_sems.at[a]).wait()

    src, lands = handle["src"], handle["lands"]
    res = pl.pallas_call(
        body, name=name,
        out_shape=[pltpu.HBM(a.shape, a.dtype) for a in list(src) + list(lands)],
        in_specs=[_HBM] * (2 * n) + [_SEM] * 3 + [pl.BlockSpec(memory_space=pl.ANY)],
        out_specs=[_HBM] * (2 * n),
        input_output_aliases={i: i for i in range(2 * n)},
        compiler_params=pltpu.CompilerParams(has_side_effects=_EFFECT),
    )(*src, *lands, *handle["sems"], after)
    return list(res[n:])


def _ffn_fwd(h, gain, wgu, wd, tm):
    t = h.shape[0]

    def body(h_ref, g_ref, wgu_ref, wd_ref, out_ref, xn_ref, ab_ref, acc_ref):
        j = pl.program_id(1)

        @pl.when(j == 0)
        def _():
            hh = h_ref[...]
            rstd = lax.rsqrt(jnp.mean(hh * hh, axis=-1, keepdims=True) + EPS)
            xn_ref[...] = (hh * rstd * g_ref[...]).astype(xn_ref.dtype)
            acc_ref[...] = jnp.zeros_like(acc_ref)

        ab = _mm(xn_ref[...], wgu_ref[...])
        ab_ref[...] = ab.astype(ab_ref.dtype)
        a, b = ab[:, :FFP], ab[:, FFP:]
        s = a * _sigmoid(a) * b
        acc_ref[...] += _mm(s, wd_ref[...])

        @pl.when(j == NDEV - 1)
        def _():
            out_ref[...] = h_ref[...] + 0.5 * acc_ref[...]

    return pl.pallas_call(
        body, name="ffn_fwd", grid=(t // tm, NDEV),
        in_specs=[pl.BlockSpec((tm, D), lambda i, j: (i, 0)),
                  pl.BlockSpec((1, D), lambda i, j: (0, 0)),
                  pl.BlockSpec((None, D, 2 * FFP), lambda i, j: (j, 0, 0)),
                  pl.BlockSpec((None, FFP, D), lambda i, j: (j, 0, 0))],
        out_specs=[pl.BlockSpec((tm, D), lambda i, j: (i, 0)),
                   pl.BlockSpec((tm, D), lambda i, j: (i, 0)),
                   pl.BlockSpec((tm, 2 * FFP), lambda i, j: (i, j))],
        out_shape=[jax.ShapeDtypeStruct((t, D), F32), jax.ShapeDtypeStruct((t, D), SAVE),
                   jax.ShapeDtypeStruct((t, NDEV * 2 * FFP), SAVE)],
        scratch_shapes=[pltpu.VMEM((tm, D), F32)],
        compiler_params=_params(("parallel", "arbitrary")),
    )(h, gain, wgu, wd)


def _ffn_bwd(dout, h, gain, xn, ab, wgu, wd, tm):
    t = h.shape[0]
    nt = t // tm
    last = NDEV - 1

    def body(dout_ref, h_ref, g_ref, xn_ref, ab_ref, wgu_ref, wd_ref,
             dh_ref, dwgu_ref, dwd_ref, dgain_ref, dxn_scr, agu_scr, awd_scr):
        j, i = pl.program_id(0), pl.program_id(1)

        @pl.when(i == 0)
        def _():
            agu_scr[...] = jnp.zeros_like(agu_scr)
            awd_scr[...] = jnp.zeros_like(awd_scr)

        @pl.when((i == 0) & (j == 0))
        def _():
            dgain_ref[...] = jnp.zeros_like(dgain_ref)

        dy = (0.5 * dout_ref[...]).astype(MXU)
        ds = _mm_nt(dy, wd_ref[...])
        ab_ = ab_ref[...].astype(F32)
        a, b = ab_[:, :FFP], ab_[:, FFP:]
        sg = _sigmoid(a)
        sa = a * sg
        db = ds * sa
        da = ds * b * (sg * (1.0 + a * (1.0 - sg)))
        s = sa * b
        dab = jnp.concatenate([da, db], axis=1).astype(MXU)
        part = _mm_nt(dab, wgu_ref[...])
        agu_scr[...] += _mm_tn(xn_ref[...], dab)
        awd_scr[...] += _mm_tn(s.astype(MXU), dy)
        rows = pl.ds(pl.multiple_of(i * tm, tm), tm)

        @pl.when(j == 0)
        def _():
            dxn_scr[rows, :] = part

        @pl.when((j > 0) & (j < last))
        def _():
            dxn_scr[rows, :] += part

        @pl.when(j == last)
        def _():
            dxn = dxn_scr[rows, :] + part
            dh, dg = _rms_bwd(dxn, h_ref[...], g_ref[...])
            dh_ref[...] = dout_ref[...] + dh
            dgain_ref[...] += dg

        @pl.when(i == nt - 1)
        def _():
            dwgu_ref[...] = agu_scr[...].astype(dwgu_ref.dtype)
            dwd_ref[...] = awd_scr[...].astype(dwd_ref.dtype)

    def tail(j, i):
        return (jnp.where(j == last, i, 0), 0)

    return pl.pallas_call(
        body, name="ffn_bwd", grid=(NDEV, nt),
        in_specs=[pl.BlockSpec((tm, D), lambda j, i: (i, 0)),
                  pl.BlockSpec((tm, D), tail),
                  pl.BlockSpec((1, D), lambda j, i: (0, 0)),
                  pl.BlockSpec((tm, D), lambda j, i: (i, 0)),
                  pl.BlockSpec((tm, 2 * FFP), lambda j, i: (i, j)),
                  pl.BlockSpec((None, D, 2 * FFP), lambda j, i: (j, 0, 0)),
                  pl.BlockSpec((None, FFP, D), lambda j, i: (j, 0, 0))],
        out_specs=[pl.BlockSpec((tm, D), tail),
                   pl.BlockSpec((None, D, 2 * FFP), lambda j, i: (j, 0, 0)),
                   pl.BlockSpec((None, FFP, D), lambda j, i: (j, 0, 0)),
                   pl.BlockSpec((1, D), lambda j, i: (0, 0))],
        out_shape=[jax.ShapeDtypeStruct((t, D), F32),
                   jax.ShapeDtypeStruct((NDEV, D, 2 * FFP), WIRE),
                   jax.ShapeDtypeStruct((NDEV, FFP, D), WIRE),
                   jax.ShapeDtypeStruct((1, D), F32)],
        scratch_shapes=[pltpu.VMEM((t, D), F32), pltpu.VMEM((D, 2 * FFP), F32), pltpu.VMEM((FFP, D), F32)],
        compiler_params=_params(("arbitrary", "arbitrary")),
    )(dout, h, gain, xn, ab, wgu, wd)


def _inproj_fwd(h, gain, win, tm):
    t = h.shape[0]

    def body(h_ref, g_ref, w_ref, z_ref, xn_ref):
        hh = h_ref[...]
        rstd = lax.rsqrt(jnp.mean(hh * hh, axis=-1, keepdims=True) + EPS)
        xn = (hh * rstd * g_ref[...]).astype(MXU)
        xn_ref[...] = xn.astype(xn_ref.dtype)
        for j in range(NDEV):
            z_ref[:, j * DINS:(j + 1) * DINS] = _mm(xn, w_ref[j])

    return pl.pallas_call(
        body, name="inproj_fwd", grid=(t // tm,),
        in_specs=[pl.BlockSpec((tm, D), lambda i: (i, 0)),
                  pl.BlockSpec((1, D), lambda i: (0, 0)),
                  pl.BlockSpec((NDEV, D, DINS), lambda i: (0, 0, 0))],
        out_specs=[pl.BlockSpec((tm, DIN), lambda i: (i, 0)),
                   pl.BlockSpec((tm, D), lambda i: (i, 0))],
        out_shape=[jax.ShapeDtypeStruct((t, DIN), F32), jax.ShapeDtypeStruct((t, D), SAVE)],
        compiler_params=_params(("parallel",)),
    )(h, gain, win)


def _inproj_bwd(dres, dz, h, gain, xn, win, tm):
    t = h.shape[0]
    nt = t // tm
    last = NDEV - 1

    def body(dres_ref, dz_ref, h_ref, g_ref, xn_ref, w_ref, dh_ref, dw_ref, dgain_ref, dxn_scr, acc_scr):
        j, i = pl.program_id(0), pl.program_id(1)

        @pl.when(i == 0)
        def _():
            acc_scr[...] = jnp.zeros_like(acc_scr)

        @pl.when((i == 0) & (j == 0))
        def _():
            dgain_ref[...] = jnp.zeros_like(dgain_ref)

        dzb = dz_ref[...]
        part = _mm_nt(dzb, w_ref[...])
        acc_scr[...] += _mm_tn(xn_ref[...], dzb)
        rows = pl.ds(pl.multiple_of(i * tm, tm), tm)

        @pl.when(j == 0)
        def _():
            dxn_scr[rows, :] = part

        @pl.when((j > 0) & (j < last))
        def _():
            dxn_scr[rows, :] += part

        @pl.when(j == last)
        def _():
            dxn = dxn_scr[rows, :] + part
            dh, dg = _rms_bwd(dxn, h_ref[...], g_ref[...])
            dh_ref[...] = dres_ref[...] + dh
            dgain_ref[...] += dg

        @pl.when(i == nt - 1)
        def _():
            dw_ref[...] = acc_scr[...].astype(dw_ref.dtype)

    def tail(j, i):
        return (jnp.where(j == last, i, 0), 0)

    return pl.pallas_call(
        body, name="inproj_bwd", grid=(NDEV, nt),
        in_specs=[pl.BlockSpec((tm, D), tail),
                  pl.BlockSpec((tm, DINS), lambda j, i: (i, j)),
                  pl.BlockSpec((tm, D), tail),
                  pl.BlockSpec((1, D), lambda j, i: (0, 0)),
                  pl.BlockSpec((tm, D), lambda j, i: (i, 0)),
                  pl.BlockSpec((None, D, DINS), lambda j, i: (j, 0, 0))],
        out_specs=[pl.BlockSpec((tm, D), tail),
                   pl.BlockSpec((None, D, DINS), lambda j, i: (j, 0, 0)),
                   pl.BlockSpec((1, D), lambda j, i: (0, 0))],
        out_shape=[jax.ShapeDtypeStruct((t, D), F32),
                   jax.ShapeDtypeStruct((NDEV, D, DINS), WIRE),
                   jax.ShapeDtypeStruct((1, D), F32)],
        scratch_shapes=[pltpu.VMEM((t, D), F32), pltpu.VMEM((D, DINS), F32)],
        compiler_params=_params(("arbitrary", "arbitrary")),
    )(dres, dz, h, gain, xn, win)


def _outproj_fwd(h, oa, ob, oc, wout, tm):
    t = h.shape[0]

    def body(h_ref, oa_ref, ob_ref, oc_ref, w_ref, out_ref):
        ym = jnp.concatenate([oa_ref[...], ob_ref[...], oc_ref[...]], axis=1)
        out_ref[...] = h_ref[...] + _mm(ym, w_ref[...])

    return pl.pallas_call(
        body, name="outproj_fwd", grid=(t // tm,),
        in_specs=[pl.BlockSpec((tm, D), lambda i: (i, 0)),
                  pl.BlockSpec((tm, DA), lambda i: (i, 0)),
                  pl.BlockSpec((tm, DB), lambda i: (i, 0)),
                  pl.BlockSpec((tm, DC), lambda i: (i, 0)),
                  pl.BlockSpec((D, D), lambda i: (0, 0))],
        out_specs=pl.BlockSpec((tm, D), lambda i: (i, 0)),
        out_shape=jax.ShapeDtypeStruct((t, D), F32),
        compiler_params=_params(("parallel",)),
    )(h, oa, ob, oc, wout)


def _outproj_bwd(dh, oa, ob, oc, wout, tm):
    t = dh.shape[0]
    nt = t // tm

    def body(dh_ref, oa_ref, ob_ref, oc_ref, w_ref, da_ref, db_ref, dc_ref, dw_ref, acc_scr):
        i = pl.program_id(0)

        @pl.when(i == 0)
        def _():
            acc_scr[...] = jnp.zeros_like(acc_scr)

        d16 = dh_ref[...].astype(MXU)
        dym = _mm_nt(d16, w_ref[...])
        da_ref[...] = dym[:, :DA]
        db_ref[...] = dym[:, DA:DA + DB]
        dc_ref[...] = dym[:, DA + DB:]
        ym = jnp.concatenate([oa_ref[...], ob_ref[...], oc_ref[...]], axis=1)
        acc_scr[...] += _mm_tn(ym, d16)

        @pl.when(i == nt - 1)
        def _():
            dw_ref[...] = acc_scr[...].astype(dw_ref.dtype)

    return pl.pallas_call(
        body, name="outproj_bwd", grid=(nt,),
        in_specs=[pl.BlockSpec((tm, D), lambda i: (i, 0)),
                  pl.BlockSpec((tm, DA), lambda i: (i, 0)),
                  pl.BlockSpec((tm, DB), lambda i: (i, 0)),
                  pl.BlockSpec((tm, DC), lambda i: (i, 0)),
                  pl.BlockSpec((D, D), lambda i: (0, 0))],
        out_specs=[pl.BlockSpec((tm, DA), lambda i: (i, 0)),
                   pl.BlockSpec((tm, DB), lambda i: (i, 0)),
                   pl.BlockSpec((tm, DC), lambda i: (i, 0)),
                   pl.BlockSpec((D, D), lambda i: (0, 0))],
        out_shape=[jax.ShapeDtypeStruct((t, DA), F32), jax.ShapeDtypeStruct((t, DB), F32),
                   jax.ShapeDtypeStruct((t, DC), F32), jax.ShapeDtypeStruct((D, D), WIRE)],
        scratch_shapes=[pltpu.VMEM((D, D), F32)],
        compiler_params=_params(("arbitrary",)),
    )(dh, oa, ob, oc, wout)


def _lower_bounds(logits):
    depth, n = logits.shape

    def body(l_ref, lb_ref, p_ref):
        rows = [l_ref[l:l + 1, :] for l in range(depth)]
        mx = functools.reduce(jnp.maximum, rows)
        ex = [jnp.exp(r - mx) for r in rows]
        den = functools.reduce(lambda u, v: u + v, ex)
        acc = jnp.zeros_like(den)
        for l in range(depth):
            p = ex[l] / den
            p_ref[l:l + 1, :] = p
            if l > 0:
                acc = acc + p
            lb_ref[l:l + 1, :] = acc

    return pl.pallas_call(
        body, name="lower_bounds",
        out_shape=[jax.ShapeDtypeStruct((depth, n), F32), jax.ShapeDtypeStruct((depth, n), F32)],
    )(logits)


def _lower_bounds_bwd(p, dlb):
    depth, n = p.shape

    def body(p_ref, d_ref, out_ref):
        ps = [p_ref[l:l + 1, :] for l in range(depth)]
        ds = [d_ref[l:l + 1, :] for l in range(depth)]
        dp = [jnp.zeros_like(ps[0]) for _ in range(depth)]
        run = jnp.zeros_like(ps[0])
        for l in range(depth - 1, 0, -1):
            run = run + ds[l]
            dp[l] = run
        dot = functools.reduce(lambda u, v: u + v, [ps[l] * dp[l] for l in range(depth)])
        for l in range(depth):
            out_ref[l:l + 1, :] = ps[l] * (dp[l] - dot)

    return pl.pallas_call(body, name="lower_bounds_bwd", out_shape=jax.ShapeDtypeStruct((depth, n), F32))(p, dlb)


def _hgrn_chunk(z_ref, lb_ref, hd):
    c0 = hd * HD
    q = z_ref[:, c0:c0 + HD]
    fl = z_ref[:, DA + c0:DA + c0 + HD]
    v = z_ref[:, 2 * DA + c0:2 * DA + c0 + HD]
    g = z_ref[:, 3 * DA + c0:3 * DA + c0 + HD]
    lb = lb_ref[:, c0:c0 + HD]
    sq = _sigmoid(q)
    qs = q * sq
    sg = _sigmoid(fl)
    f = lb + (1.0 - lb) * sg
    k = 1.0 - f
    lf = jnp.log(f)
    b = _mm_exact_l(_tri(ACH, True).astype(MXU), lf)
    bend = jnp.sum(lf, axis=0, keepdims=True)
    r = 0.5 * bend
    eq, ek, eb, ed = jnp.exp(b - r), jnp.exp(r - b), jnp.exp(b), jnp.exp(bend - b)
    qt, kt, qe, kd = qs * eq, k * ek, qs * eb, k * ed
    causal = _iota((ACH, ACH), 0) >= _iota((ACH, ACH), 1)
    att = jnp.where(causal, _mm_nt(qt, kt), 0.0)
    return dict(q=q, v=v, g=g, lb=lb, sq=sq, qs=qs, sg=sg, f=f, k=k, bend=bend, eq=eq, ek=ek, eb=eb, ed=ed,
                qt=qt, kt=kt, qe=qe, kd=kd, att=att, causal=causal)


def _hgrn_fwd(z, lb, gain):
    t = z.shape[0]
    nc = t // ACH

    def body(z_ref, lb_ref, g_ref, o_ref, oa_ref, st_ref, st_scr):
        @pl.when(pl.program_id(0) == 0)
        def _():
            st_scr[...] = jnp.zeros_like(st_scr)

        for hd in range(NH):
            c = _hgrn_chunk(z_ref, lb_ref, hd)
            st = st_scr[hd]
            st_ref[0, hd] = st
            o = _mm(c["att"], c["v"]) + _mm_nt(c["qe"], st)
            st_scr[hd] = st * jnp.exp(c["bend"]) + _mm_tn(c["v"], c["kd"])
            cols = slice(hd * HD, (hd + 1) * HD)
            o_ref[:, cols] = o
            rstd = lax.rsqrt(jnp.mean(o * o, axis=-1, keepdims=True) + EPS)
            gg = c["g"]
            oa_ref[:, cols] = (o * rstd * g_ref[:, cols] * (gg * _sigmoid(gg))).astype(oa_ref.dtype)

    return pl.pallas_call(
        body, name="hgrn_fwd", grid=(nc,),
        in_specs=[pl.BlockSpec((ACH, 4 * DA), lambda c: (c, 0)),
                  pl.BlockSpec((1, DA), lambda c: (0, 0)),
                  pl.BlockSpec((1, DA), lambda c: (0, 0))],
        out_specs=[pl.BlockSpec((ACH, DA), lambda c: (c, 0)),
                   pl.BlockSpec((ACH, DA), lambda c: (c, 0)),
                   pl.BlockSpec((1, NH, HD, HD), lambda c: (c, 0, 0, 0))],
        out_shape=[jax.ShapeDtypeStruct((t, DA), F32), jax.ShapeDtypeStruct((t, DA), SAVE),
                   jax.ShapeDtypeStruct((nc, NH, HD, HD), F32)],
        scratch_shapes=[pltpu.VMEM((NH, HD, HD), F32)],
        compiler_params=_params(("arbitrary",)),
    )(z, lb, gain)


def _hgrn_bwd(z, lb, gain, o, states, doa):
    t = z.shape[0]
    nc = t // ACH

    def body(z_ref, lb_ref, g_ref, o_ref, st_ref, doa_ref, dz_ref, dgain_ref, dlb_ref, dst_scr):
        @pl.when(pl.program_id(0) == 0)
        def _():
            dst_scr[...] = jnp.zeros_like(dst_scr)
            dgain_ref[...] = jnp.zeros_like(dgain_ref)
            dlb_ref[...] = jnp.zeros_like(dlb_ref)

        upper = _tri(ACH, False).astype(MXU)
        for hd in range(NH):
            c = _hgrn_chunk(z_ref, lb_ref, hd)
            cols = slice(hd * HD, (hd + 1) * HD)
            o = o_ref[:, cols]
            do_a = doa_ref[:, cols]
            gain = g_ref[:, cols]
            gg = c["g"]
            sgg = _sigmoid(gg)
            silu_g = gg * sgg
            rstd = lax.rsqrt(jnp.mean(o * o, axis=-1, keepdims=True) + EPS)
            n = o * rstd
            dn = do_a * gain * silu_g
            dg = do_a * n * gain * (sgg * (1.0 + gg * (1.0 - sgg)))
            dgain_ref[:, cols] += jnp.sum(do_a * silu_g * n, axis=0, keepdims=True)
            d_o = rstd * (dn - n * jnp.mean(dn * n, axis=-1, keepdims=True))

            st = st_ref[0, hd]
            dsp = dst_scr[hd]
            datt = jnp.where(c["causal"], _mm_nt(d_o, c["v"]), 0.0)
            dv = _mm_tn(c["att"], d_o) + _mm_nt(c["kd"], dsp)
            dqt = _mm(datt, c["kt"])
            dqe = _mm(d_o, st)
            dkt = _mm_tn(datt, c["qt"])
            dkd = _mm(c["v"], dsp)
            decay = jnp.exp(c["bend"])
            dst_scr[hd] = dsp * decay + _mm_tn(d_o, c["qe"])
            dbend = decay * jnp.sum(st * dsp, axis=0, keepdims=True) + jnp.sum(dkd * c["kd"], axis=0, keepdims=True)
            db = dqt * c["qt"] + dqe * c["qe"] - dkt * c["kt"] - dkd * c["kd"]
            dqs = dqt * c["eq"] + dqe * c["eb"]
            dk = dkt * c["ek"] + dkd * c["ed"]
            dlf = _mm_exact_l(upper, db) + dbend
            df = dlf / c["f"] - dk
            sg = c["sg"]
            dlb_ref[:, cols] += jnp.sum(df * (1.0 - sg), axis=0, keepdims=True)
            dfl = df * (1.0 - c["lb"]) * sg * (1.0 - sg)
            sq, q = c["sq"], c["q"]
            dq = dqs * (sq * (1.0 + q * (1.0 - sq)))
            c0 = hd * HD
            dz_ref[:, c0:c0 + HD] = dq.astype(dz_ref.dtype)
            dz_ref[:, DA + c0:DA + c0 + HD] = dfl.astype(dz_ref.dtype)
            dz_ref[:, 2 * DA + c0:2 * DA + c0 + HD] = dv.astype(dz_ref.dtype)
            dz_ref[:, 3 * DA + c0:3 * DA + c0 + HD] = dg.astype(dz_ref.dtype)

    rev = lambda c: (nc - 1 - c, 0)
    return pl.pallas_call(
        body, name="hgrn_bwd", grid=(nc,),
        in_specs=[pl.BlockSpec((ACH, 4 * DA), rev),
                  pl.BlockSpec((1, DA), lambda c: (0, 0)),
                  pl.BlockSpec((1, DA), lambda c: (0, 0)),
                  pl.BlockSpec((ACH, DA), rev),
                  pl.BlockSpec((1, NH, HD, HD), lambda c: (nc - 1 - c, 0, 0, 0)),
                  pl.BlockSpec((ACH, DA), rev)],
        out_specs=[pl.BlockSpec((ACH, 4 * DA), rev),
                   pl.BlockSpec((1, DA), lambda c: (0, 0)),
                   pl.BlockSpec((1, DA), lambda c: (0, 0))],
        out_shape=[jax.ShapeDtypeStruct((t, 4 * DA), SAVE), jax.ShapeDtypeStruct((1, DA), F32),
                   jax.ShapeDtypeStruct((1, DA), F32)],
        scratch_shapes=[pltpu.VMEM((NH, HD, HD), F32)],
        compiler_params=_params(("arbitrary",)),
    )(z, lb, gain, o, states, doa)


def _shift_down(prev8, x, k):
    cat = jnp.concatenate([prev8, x], axis=0)
    return pltpu.roll(cat, k, axis=0)[8:, :]


def _shift_up(x, next8, k):
    n = x.shape[0]
    cat = jnp.concatenate([x, next8], axis=0)
    return pltpu.roll(cat, n + 8 - k, axis=0)[:n, :]


def _lru_gates(x, prev8, cw_ref, vec_ref, wa_ref, wx_ref):
    xs = [x, _shift_down(prev8, x, 1), _shift_down(prev8, x, 2), _shift_down(prev8, x, 3)]
    xc = vec_ref[0:1, :] + cw_ref[3:4, :] * xs[0] + cw_ref[2:3, :] * xs[1] + cw_ref[1:2, :] * xs[2] + cw_ref[0:1, :] * xs[3]
    r = _sigmoid(_mm(xc, wa_ref[...]) + vec_ref[1:2, :])
    gi = _sigmoid(_mm(xc, wx_ref[...]) + vec_ref[2:3, :])
    lam = vec_ref[3:4, :]
    sp = jnp.maximum(-lam, 0.0) + jnp.log(1.0 + jnp.exp(-jnp.abs(lam)))
    la = -LRU_C * r * sp
    a = jnp.exp(la)
    mult = jnp.sqrt(-_expm1(2.0 * la))
    return xs, xc, r, gi, sp, a, mult


def _scan_down(a, u):
    n = a.shape[0]
    row = _iota(a.shape, 0)
    s = 1
    while s < n:
        keep = row >= s
        ash = jnp.where(keep, pltpu.roll(a, s, axis=0), 1.0)
        ush = jnp.where(keep, pltpu.roll(u, s, axis=0), 0.0)
        u = a * ush + u
        a = a * ash
        s *= 2
    return a, u


def _scan_up(a, u):
    n = a.shape[0]
    row = _iota(a.shape, 0)
    s = 1
    while s < n:
        keep = row < n - s
        ash = jnp.where(keep, pltpu.roll(a, n - s, axis=0), 1.0)
        ush = jnp.where(keep, pltpu.roll(u, n - s, axis=0), 0.0)
        u = a * ush + u
        a = a * ash
        s *= 2
    return a, u


def _lru_fwd(z, cw, vec, wa, wx, tb):
    t = z.shape[0]
    xcol, gcol = (4 * DA) // DB, (4 * DA) // DB + 1

    def body(x_ref, gate_ref, cw_ref, vec_ref, wa_ref, wx_ref, ob_ref, h_ref, xprev_scr, hc_scr):
        @pl.when(pl.program_id(0) == 0)
        def _():
            xprev_scr[...] = jnp.zeros_like(xprev_scr)
            hc_scr[...] = jnp.zeros_like(hc_scr)

        x = x_ref[...]
        _, xc, _, gi, _, a, mult = _lru_gates(x, xprev_scr[...], cw_ref, vec_ref, wa_ref, wx_ref)
        acum, hloc = _scan_down(a, mult * gi * xc)
        h = hloc + acum * hc_scr[0:1, :]
        h_ref[...] = h
        hc_scr[...] = jnp.broadcast_to(_row(h, tb - 1), hc_scr.shape)
        xprev_scr[...] = x[tb - 8:, :]
        y = h * _gelu(gate_ref[...])
        ms = _mm_exact_r(y * y, _group_matrix(DB, 1.0 / GRP).astype(MXU))
        ob_ref[...] = (y * lax.rsqrt(ms + EPS) * vec_ref[4:5, :]).astype(ob_ref.dtype)

    return pl.pallas_call(
        body, name="lru_fwd", grid=(t // tb,),
        in_specs=[pl.BlockSpec((tb, DB), lambda i: (i, xcol)),
                  pl.BlockSpec((tb, DB), lambda i: (i, gcol)),
                  pl.BlockSpec((8, DB), lambda i: (0, 0)),
                  pl.BlockSpec((8, DB), lambda i: (0, 0)),
                  pl.BlockSpec((DB, DB), lambda i: (0, 0)),
                  pl.BlockSpec((DB, DB), lambda i: (0, 0))],
        out_specs=[pl.BlockSpec((tb, DB), lambda i: (i, 0)),
                   pl.BlockSpec((tb, DB), lambda i: (i, 0))],
        out_shape=[jax.ShapeDtypeStruct((t, DB), SAVE), jax.ShapeDtypeStruct((t, DB), F32)],
        scratch_shapes=[pltpu.VMEM((8, DB), F32), pltpu.VMEM((8, DB), F32)],
        compiler_params=_params(("arbitrary",)),
    )(z, z, cw, vec, wa, wx)


def _lru_bwd(z, hseq, dob, cw, vec, wa, wx, tb):
    t = z.shape[0]
    nb = t // tb
    xcol, gcol = (4 * DA) // DB, (4 * DA) // DB + 1
    per = tb // 8

    def body(x_ref, xh_ref, gate_ref, h_ref, hh_ref, dob_ref, cw_ref, vec_ref, wa_ref, wx_ref,
             dz_ref, dcw_ref, dvec_ref, dwa_ref, dwx_ref, gc_scr, an_scr, dxc_scr):
        step = pl.program_id(0)
        blk = nb - 1 - step

        @pl.when(step == 0)
        def _():
            for ref in (gc_scr, an_scr, dxc_scr, dcw_ref, dvec_ref, dwa_ref, dwx_ref):
                ref[...] = jnp.zeros_like(ref)

        first = (blk > 0).astype(F32)
        x = x_ref[...]
        xs, xc, r, gi, sp, a, mult = _lru_gates(x, xh_ref[...] * first, cw_ref, vec_ref, wa_ref, wx_ref)
        h = h_ref[...]
        hprev = _shift_down(hh_ref[...] * first, h, 1)
        ge, dge = _gelu_and_grad(gate_ref[...])
        y = h * ge
        gmat = _group_matrix(DB, 1.0 / GRP).astype(MXU)
        rstd = lax.rsqrt(_mm_exact_r(y * y, gmat) + EPS)
        n = y * rstd
        d_ob = dob_ref[...]
        dn = d_ob * vec_ref[4:5, :]
        dvec_ref[4:5, :] += jnp.sum(d_ob * n, axis=0, keepdims=True)
        dy = rstd * (dn - n * _mm_exact_r(dn * n, gmat))
        dh = dy * ge
        dgate = dy * h * dge

        row = _iota(a.shape, 0)
        anext = jnp.where(row == tb - 1, an_scr[0:1, :], pltpu.roll(a, tb - 1, axis=0))
        acum, gloc = _scan_up(anext, dh)
        g = gloc + acum * gc_scr[0:1, :]
        gc_scr[...] = jnp.broadcast_to(_row(g, 0), gc_scr.shape)
        an_scr[...] = jnp.broadcast_to(_row(a, 0), an_scr.shape)

        da = g * hprev
        dmult = g * gi * xc
        dgi = g * mult * xc
        dxc = g * mult * gi
        dla = da * a - dmult * (a * a) / mult
        dr = dla * (-LRU_C * sp)
        dsp = jnp.sum(dla * (-LRU_C * r), axis=0, keepdims=True)
        lam = vec_ref[3:4, :]
        dvec_ref[3:4, :] += -dsp * _sigmoid(-lam)
        dpa = dr * r * (1.0 - r)
        dpx = dgi * gi * (1.0 - gi)
        dwa_ref[...] += _mm_tn(xc, dpa)
        dwx_ref[...] += _mm_tn(xc, dpx)
        dvec_ref[1:2, :] += jnp.sum(dpa, axis=0, keepdims=True)
        dvec_ref[2:3, :] += jnp.sum(dpx, axis=0, keepdims=True)
        dxc = dxc + _mm_nt(dpa, wa_ref[...]) + _mm_nt(dpx, wx_ref[...])
        dvec_ref[0:1, :] += jnp.sum(dxc, axis=0, keepdims=True)
        for tap in range(4):
            dcw_ref[tap:tap + 1, :] += jnp.sum(dxc * xs[3 - tap], axis=0, keepdims=True)
        nxt = dxc_scr[...]
        dx = (cw_ref[3:4, :] * dxc + cw_ref[2:3, :] * _shift_up(dxc, nxt, 1)
              + cw_ref[1:2, :] * _shift_up(dxc, nxt, 2) + cw_ref[0:1, :] * _shift_up(dxc, nxt, 3))
        dxc_scr[...] = dxc[:8, :]
        dz_ref[:, :DB] = dx.astype(dz_ref.dtype)
        dz_ref[:, DB:] = dgate.astype(dz_ref.dtype)

    def halo(col):
        return lambda s: (jnp.maximum((nb - 1 - s) * per - 1, 0), col)

    const = lambda s: (0, 0)
    return pl.pallas_call(
        body, name="lru_bwd", grid=(nb,),
        in_specs=[pl.BlockSpec((tb, DB), lambda s: (nb - 1 - s, xcol)),
                  pl.BlockSpec((8, DB), halo(xcol)),
                  pl.BlockSpec((tb, DB), lambda s: (nb - 1 - s, gcol)),
                  pl.BlockSpec((tb, DB), lambda s: (nb - 1 - s, 0)),
                  pl.BlockSpec((8, DB), halo(0)),
                  pl.BlockSpec((tb, DB), lambda s: (nb - 1 - s, 0)),
                  pl.BlockSpec((8, DB), const), pl.BlockSpec((8, DB), const),
                  pl.BlockSpec((DB, DB), const), pl.BlockSpec((DB, DB), const)],
        out_specs=[pl.BlockSpec((tb, 2 * DB), lambda s: (nb - 1 - s, 0)),
                   pl.BlockSpec((8, DB), const), pl.BlockSpec((8, DB), const),
                   pl.BlockSpec((DB, DB), const), pl.BlockSpec((DB, DB), const)],
        out_shape=[jax.ShapeDtypeStruct((t, 2 * DB), SAVE), jax.ShapeDtypeStruct((8, DB), F32),
                   jax.ShapeDtypeStruct((8, DB), F32), jax.ShapeDtypeStruct((DB, DB), F32),
                   jax.ShapeDtypeStruct((DB, DB), F32)],
        scratch_shapes=[pltpu.VMEM((8, DB), F32), pltpu.VMEM((8, DB), F32), pltpu.VMEM((8, DB), F32)],
        compiler_params=_params(("arbitrary",)),
    )(z, z, z, hseq, hseq, dob, cw, vec, wa, wx)


def _sgu_chunk(u_in, v_in, w_ref, bias, gmat):
    uu, duu = _gelu_and_grad(u_in)
    vv, dvv = _gelu_and_grad(v_in)
    mu = _mm_exact_r(vv, gmat)
    dlt = vv - mu
    rstd_v = lax.rsqrt(_mm_exact_r(dlt * dlt, gmat) + EPS)
    vn = dlt * rstd_v
    col = _iota((CCH, DC), 1) // GRP
    causal = _iota((CCH, CCH), 0) >= _iota((CCH, CCH), 1)
    ws = [jnp.where(causal, w_ref[g], 0.0) for g in range(DC // GRP)]
    zz = bias
    for g, w in enumerate(ws):
        zz = zz + jnp.where(col == g, _mm(w, vn), 0.0)
    return uu, duu, dvv, rstd_v, vn, zz, ws, col, causal


def _sgu_fwd(z, w, bias, gain, tb):
    t = z.shape[0]
    ucol, vcol = (4 * DA + 2 * DB) // DC, (4 * DA + 2 * DB) // DC + 1

    def body(u_ref, v_ref, w_ref, b_ref, g_ref, oc_ref):
        gmat = _group_matrix(DC, 1.0 / GRP).astype(MXU)
        for ch in range(tb // CCH):
            rows = slice(ch * CCH, (ch + 1) * CCH)
            uu, _, _, _, _, zz, _, _, _ = _sgu_chunk(u_ref[rows, :], v_ref[rows, :], w_ref, b_ref[...], gmat)
            y = uu * zz
            ms = _mm_exact_r(y * y, gmat)
            oc_ref[rows, :] = (y * lax.rsqrt(ms + EPS) * g_ref[...]).astype(oc_ref.dtype)

    const = lambda i: (0, 0)
    return pl.pallas_call(
        body, name="sgu_fwd", grid=(t // tb,),
        in_specs=[pl.BlockSpec((tb, DC), lambda i: (i, ucol)),
                  pl.BlockSpec((tb, DC), lambda i: (i, vcol)),
                  pl.BlockSpec((DC // GRP, CCH, CCH), lambda i: (0, 0, 0)),
                  pl.BlockSpec((CCH, DC), const), pl.BlockSpec((1, DC), const)],
        out_specs=pl.BlockSpec((tb, DC), lambda i: (i, 0)),
        out_shape=jax.ShapeDtypeStruct((t, DC), SAVE),
        compiler_params=_params(("parallel",)),
    )(z, z, w, bias, gain)


def _sgu_bwd(z, doc, w, bias, gain, tb):
    t = z.shape[0]
    nb = t // tb
    ucol, vcol = (4 * DA + 2 * DB) // DC, (4 * DA + 2 * DB) // DC + 1
    ng = DC // GRP

    def body(u_ref, v_ref, doc_ref, w_ref, b_ref, g_ref, dz_ref, dw_ref, dbias_ref, dgain_ref, dbsum_scr):
        i = pl.program_id(0)

        @pl.when(i == 0)
        def _():
            for ref in (dw_ref, dgain_ref, dbsum_scr):
                ref[...] = jnp.zeros_like(ref)

        gmat = _group_matrix(DC, 1.0 / GRP).astype(MXU)
        for ch in range(tb // CCH):
            rows = slice(ch * CCH, (ch + 1) * CCH)
            uu, duu, dvv, rstd_v, vn, zz, ws, col, causal = _sgu_chunk(
                u_ref[rows, :], v_ref[rows, :], w_ref, b_ref[...], gmat)
            y = uu * zz
            rstd = lax.rsqrt(_mm_exact_r(y * y, gmat) + EPS)
            n = y * rstd
            d_oc = doc_ref[rows, :]
            dn = d_oc * g_ref[...]
            dgain_ref[0:1, :] += jnp.sum(d_oc * n, axis=0, keepdims=True)
            dy = rstd * (dn - n * _mm_exact_r(dn * n, gmat))
            dzz = dy * uu
            dz_ref[rows, :DC] = (dy * zz * duu).astype(dz_ref.dtype)
            dbsum_scr[...] += dzz
            dvn = jnp.zeros_like(dzz)
            for g in range(ng):
                sel = col == g
                dvn = dvn + jnp.where(sel, _mm_tn(ws[g], dzz), 0.0)
                dw_ref[g] += jnp.where(causal, _mm_nt(jnp.where(sel, dzz, 0.0), vn), 0.0)
            dv = rstd_v * (dvn - _mm_exact_r(dvn, gmat) - vn * _mm_exact_r(dvn * vn, gmat))
            dz_ref[rows, DC:] = (dv * dvv).astype(dz_ref.dtype)

        @pl.when(i == nb - 1)
        def _():
            dbias_ref[...] = _mm_exact_r(dbsum_scr[...], _group_matrix(DC, 1.0).astype(MXU))

    const = lambda i: (0, 0)
    return pl.pallas_call(
        body, name="sgu_bwd", grid=(nb,),
        in_specs=[pl.BlockSpec((tb, DC), lambda i: (i, ucol)),
                  pl.BlockSpec((tb, DC), lambda i: (i, vcol)),
                  pl.BlockSpec((tb, DC), lambda i: (i, 0)),
                  pl.BlockSpec((ng, CCH, CCH), lambda i: (0, 0, 0)),
                  pl.BlockSpec((CCH, DC), const), pl.BlockSpec((1, DC), const)],
        out_specs=[pl.BlockSpec((tb, 2 * DC), lambda i: (i, 0)),
                   pl.BlockSpec((ng, CCH, CCH), lambda i: (0, 0, 0)),
                   pl.BlockSpec((CCH, DC), const), pl.BlockSpec((8, DC), const)],
        out_shape=[jax.ShapeDtypeStruct((t, 2 * DC), SAVE), jax.ShapeDtypeStruct((ng, CCH, CCH), F32),
                   jax.ShapeDtypeStruct((CCH, DC), F32), jax.ShapeDtypeStruct((8, DC), F32)],
        scratch_shapes=[pltpu.VMEM((CCH, DC), F32)],
        compiler_params=_params(("arbitrary",)),
    )(z, z, doc, w, bias, gain)


def _head(h, gain, target, tm):
    t = h.shape[0]

    def body(h_ref, g_ref, t_ref, dh_ref, loss_ref, dgain_ref):
        @pl.when(pl.program_id(0) == 0)
        def _():
            loss_ref[...] = jnp.zeros_like(loss_ref)
            dgain_ref[...] = jnp.zeros_like(dgain_ref)

        hh = h_ref[...]
        gain = g_ref[...]
        rstd = lax.rsqrt(jnp.mean(hh * hh, axis=-1, keepdims=True) + EPS)
        xhat = hh * rstd
        err = xhat * gain - t_ref[...]
        per_tok = jnp.mean(err * err, axis=-1, keepdims=True)
        loss_ref[...] += 0.5 * jnp.sum(per_tok, axis=0, keepdims=True)
        dy = err * (1.0 / D)
        dgain_ref[...] += jnp.sum(dy * xhat, axis=0, keepdims=True)
        dxh = dy * gain
        dh_ref[...] = rstd * (dxh - xhat * jnp.mean(dxh * xhat, axis=-1, keepdims=True))

    return pl.pallas_call(
        body, name="head", grid=(t // tm,),
        in_specs=[pl.BlockSpec((tm, D), lambda i: (i, 0)),
                  pl.BlockSpec((1, D), lambda i: (0, 0)),
                  pl.BlockSpec((tm, D), lambda i: (i, 0))],
        out_specs=[pl.BlockSpec((tm, D), lambda i: (i, 0)),
                   pl.BlockSpec((1, 128), lambda i: (0, 0)),
                   pl.BlockSpec((1, D), lambda i: (0, 0))],
        out_shape=[jax.ShapeDtypeStruct((t, D), F32), jax.ShapeDtypeStruct((1, 128), F32),
                   jax.ShapeDtypeStruct((1, D), F32)],
        compiler_params=_params(("arbitrary",)),
    )(h, gain, target)


def _adamw(w, g, m, v):
    m = ADAM_B1 * m + (1.0 - ADAM_B1) * g
    v = ADAM_B2 * v + (1.0 - ADAM_B2) * (g * g)
    m_hat = m / (1.0 - ADAM_B1 ** ADAM_STEP)
    v_hat = v / (1.0 - ADAM_B2 ** ADAM_STEP)
    delta = -ADAM_LR * (m_hat / (jnp.sqrt(v_hat) + ADAM_EPS) + ADAM_WD * w)
    return delta, m, v


def _adamw_big(recv, params, tr, name):
    depth = len(recv)
    cols_p = recv[0].shape[2]
    np_ = len(params)
    rows, colss = params[0][0].shape[1], [p[0].shape[2] for p in params]

    def body(*refs):
        r_refs = refs[:depth]
        ins, outs = refs[depth:depth + 3 * np_], refs[depth + 3 * np_:]
        for l in range(depth):
            g = r_refs[l][0].astype(F32)
            for k in range(1, NDEV):
                g = g + r_refs[l][k].astype(F32)
            for p in range(np_):
                col0 = params[p][3]
                gp = g[:, col0:col0 + colss[p]]
                w_ref, m_ref, v_ref = ins[3 * p:3 * p + 3]
                delta, m, v = _adamw(w_ref[l], gp, m_ref[l], v_ref[l])
                g_out, d_out, m_out, v_out = outs[4 * p:4 * p + 4]
                g_out[l] = gp
                d_out[l] = delta
                m_out[l] = m
                v_out[l] = v

    in_specs = [pl.BlockSpec((NDEV, tr, cols_p), lambda i: (0, i, 0))] * depth
    out_specs, out_shape, args = [], [], list(recv)
    for p in range(np_):
        spec = pl.BlockSpec((depth, tr, colss[p]), lambda i: (0, i, 0))
        in_specs += [spec] * 3
        out_specs += [spec] * 4
        out_shape += [jax.ShapeDtypeStruct((depth, rows, colss[p]), F32)] * 4
        args += list(params[p][:3])
    res = pl.pallas_call(
        body, name=name, grid=(rows // tr,),
        in_specs=in_specs, out_specs=out_specs, out_shape=out_shape,
        compiler_params=_params(("parallel",)),
    )(*args)
    return [tuple(res[4 * p:4 * p + 4]) for p in range(np_)]


def _sum_devices(recv):
    _, r, _ = recv.shape

    def body(r_ref, out_ref):
        g = r_ref[0]
        for k in range(1, NDEV):
            g = g + r_ref[k]
        out_ref[...] = g

    return pl.pallas_call(body, name="sum_devices", out_shape=jax.ShapeDtypeStruct((r, 128), F32))(recv)


def _adamw_small(w, g, m, v):
    def body(w_ref, g_ref, m_ref, v_ref, d_out, m_out, v_out):
        delta, m_, v_ = _adamw(w_ref[...], g_ref[...], m_ref[...], v_ref[...])
        d_out[...] = delta
        m_out[...] = m_
        v_out[...] = v_

    return pl.pallas_call(body, name="adamw_small", out_shape=[jax.ShapeDtypeStruct(w.shape, F32)] * 3)(w, g, m, v)


def _pack(arrs):
    flat = jnp.concatenate([a.reshape(-1) for a in arrs])
    pad = (-flat.shape[0]) % 1024
    return jnp.pad(flat, (0, pad)).reshape(-1, 128)


def _unpack(buf, like):
    flat = buf.reshape(-1)
    out, off = [], 0
    for a in like:
        out.append(flat[off:off + a.size].reshape(a.shape))
        off += a.size
    return out


def _block_diag(w):
    nb, bd, _ = w.shape
    eye = jnp.eye(nb, dtype=w.dtype)
    return (eye[:, None, :, None] * w[:, :, None, :]).reshape(nb * bd, nb * bd)


def _diag_blocks(w):
    nb = w.shape[0] // GRP
    return jnp.stack([w[g * GRP:(g + 1) * GRP, g * GRP:(g + 1) * GRP] for g in range(nb)])


def _pad_cols(a, n):
    return jnp.pad(a, ((0, 0), (0, n - a.shape[1])))


SMALL = ['ffn1_norm', 'mix_norm', 'hgrn_lb_logits', 'hgrn_norm', 'conv_b', 'lru_wa', 'lru_ba', 'lru_wx', 'lru_bx',
         'lru_lambda', 'lru_norm', 'sgu_w', 'sgu_b', 'sgu_norm', 'ffn2_norm', 'final_norm']
NAMES = ['ffn1_norm', 'ffn1_wg', 'ffn1_wu', 'ffn1_wd', 'mix_norm', 'w_in', 'hgrn_lb_logits', 'hgrn_norm', 'conv_w',
         'conv_b', 'lru_wa', 'lru_ba', 'lru_wx', 'lru_bx', 'lru_lambda', 'lru_norm', 'sgu_w', 'sgu_b', 'sgu_norm',
         'w_out', 'ffn2_norm', 'ffn2_wg', 'ffn2_wu', 'ffn2_wd', 'final_norm']


def _step(x, target, w, m, v):
    depth = w['ffn1_wg'].shape[0]
    t = x.shape[1]
    h = x.reshape(t, D)
    target = target.reshape(t, D)
    tm_f, tm_b, tb = min(TM_F, t), min(TM_B, t), min(TB, t)
    my = 4 * lax.axis_index("x") + 2 * lax.axis_index("y") + lax.axis_index("c")

    cw_all = _all_gather([w['conv_w']], "gather_conv")[0]
    conv_w = jnp.moveaxis(cw_all, 0, 2).reshape(depth, 4, DB)
    lbs, lb_soft = _lower_bounds(w['hgrn_lb_logits'])

    def row(a):
        return a.reshape(1, -1)

    def shards(l, unit):
        if unit == 1:
            return [w['w_in'][l].astype(WIRE), w['w_out'][l].astype(WIRE)]
        f = 'ffn1' if unit == 0 else 'ffn2'
        wgu = jnp.concatenate([_pad_cols(w[f + '_wg'][l], FFP), _pad_cols(w[f + '_wu'][l], FFP)], axis=1)
        return [wgu.astype(WIRE), jnp.pad(w[f + '_wd'][l], ((0, FFP - FFS), (0, 0))).astype(WIRE)]

    units = [(l, u) for l in range(depth) for u in range(3)]
    state = dict(pending=_transfer_start(shards(0, 0), True, "gather_start_0_0"), idx=0)

    def next_weights(after):
        l, u = units[state['idx']]
        lands = _transfer_wait(state['pending'], after, f"gather_wait_{l}_{u}")
        state['idx'] += 1
        tok = 0.0
        if state['idx'] < len(units):
            nl, nu = units[state['idx']]
            state['pending'] = _transfer_start(shards(nl, nu), True, f"gather_start_{nl}_{nu}", deps=(lands[-1],))
            tok = state['pending']['token'][0, 0]
        return lands, tok

    saved = []
    for l in range(depth):
        (wgu1, wd1), tok = next_weights(h)
        s = dict(wgu1=wgu1, wd1=wd1, h0=h)
        h, s['xn1'], s['ab1'] = _ffn_fwd(h, row(w['ffn1_norm'][l]) + tok, wgu1, wd1, tm_f)
        s['h1'] = h
        (win, wout), tok = next_weights(h)
        wout = wout.reshape(D, D)
        s['win'], s['wout'] = win, wout
        z, s['xnm'] = _inproj_fwd(h, row(w['mix_norm'][l]) + tok, win, tm_f)
        s['z'] = z
        s['o'], oa, s['states'] = _hgrn_fwd(z, row(lbs[l]), row(w['hgrn_norm'][l]))
        s['cw'] = jnp.pad(conv_w[l], ((0, 4), (0, 0)))
        s['vec'] = jnp.concatenate([row(w['conv_b'][l]), row(w['lru_ba'][l]), row(w['lru_bx'][l]),
                                    row(w['lru_lambda'][l]), row(w['lru_norm'][l]), jnp.zeros((3, DB), F32)])
        s['wa'], s['wx'] = _block_diag(w['lru_wa'][l]), _block_diag(w['lru_wx'][l])
        ob, s['hseq'] = _lru_fwd(z, s['cw'], s['vec'], s['wa'], s['wx'], tb)
        s['bias'] = jnp.repeat(w['sgu_b'][l].T, GRP, axis=1)
        oc = _sgu_fwd(z, w['sgu_w'][l], s['bias'], row(w['sgu_norm'][l]), tb)
        s['oa'], s['ob'], s['oc'] = oa, ob, oc
        h = _outproj_fwd(h, oa, ob, oc, wout, tm_f)
        s['h2'] = h
        (wgu2, wd2), tok = next_weights(h)
        s['wgu2'], s['wd2'] = wgu2, wd2
        h, s['xn2'], s['ab2'] = _ffn_fwd(h, row(w['ffn2_norm'][l]) + tok, wgu2, wd2, tm_f)
        saved.append(s)

    dh, loss_part, g_final = _head(h, row(w['final_norm']), target, tm_f)
    loss = lax.psum(loss_part[0, 0], ("x", "y", "c"))

    recv = {k: [None] * depth for k in ('wgu1', 'wd1', 'wgu2', 'wd2', 'win', 'wout')}
    flight = []

    def land(after):
        handle, kinds, l = flight.pop()
        for k, a in zip(kinds, _transfer_wait(handle, after, f"exchange_wait_{kinds[0]}_{l}")):
            recv[k][l] = a

    def exchange(arrs, kinds, l):
        handle = _transfer_start(arrs, False, f"exchange_start_{kinds[0]}_{l}")
        if flight:
            land(handle['token'])
        flight.append((handle, kinds, l))
        return handle['token'][0, 0]

    small = {k: [None] * depth for k in SMALL if k != 'final_norm'}
    dconv = [None] * depth
    dlb = [None] * depth
    tok = 0.0
    for l in reversed(range(depth)):
        s = saved[l]
        dh, dwgu, dwd, g = _ffn_bwd(dh, s['h2'], row(w['ffn2_norm'][l]) + tok, s['xn2'], s['ab2'],
                                    s['wgu2'], s['wd2'], tm_b)
        tok = exchange([dwgu, dwd], ('wgu2', 'wd2'), l)
        small['ffn2_norm'][l] = g
        doa, dob, doc, dwout = _outproj_bwd(dh, s['oa'], s['ob'], s['oc'], s['wout'], tm_f)
        dza, g_hn, dlb[l] = _hgrn_bwd(s['z'], row(lbs[l]), row(w['hgrn_norm'][l]) + tok, s['o'], s['states'], doa)
        small['hgrn_norm'][l] = g_hn
        dzb, dcw, dvec, dwa, dwx = _lru_bwd(s['z'], s['hseq'], dob, s['cw'], s['vec'], s['wa'], s['wx'], tb)
        dconv[l] = dcw[:4]
        small['conv_b'][l], small['lru_ba'][l], small['lru_bx'][l] = dvec[0], dvec[1].reshape(4, GRP), dvec[2].reshape(4, GRP)
        small['lru_lambda'][l], small['lru_norm'][l] = dvec[3], dvec[4]
        small['lru_wa'][l], small['lru_wx'][l] = _diag_blocks(dwa), _diag_blocks(dwx)
        dzc, dsw, dbias, dgc = _sgu_bwd(s['z'], doc, w['sgu_w'][l], s['bias'], row(w['sgu_norm'][l]), tb)
        small['sgu_w'][l], small['sgu_b'][l], small['sgu_norm'][l] = dsw, dbias[:, ::GRP].T, dgc[0]
        dz = jnp.concatenate([dza, dzb, dzc], axis=1)
        dh, dwin, g = _inproj_bwd(dh, dz, s['h1'], row(w['mix_norm'][l]), s['xnm'], s['win'], tm_b)
        tok = exchange([dwin, dwout.reshape(NDEV, D // NDEV, D)], ('win', 'wout'), l)
        small['mix_norm'][l] = g
        dh, dwgu, dwd, g = _ffn_bwd(dh, s['h0'], row(w['ffn1_norm'][l]) + tok, s['xn1'], s['ab1'],
                                    s['wgu1'], s['wd1'], tm_b)
        tok = exchange([dwgu, dwd], ('wgu1', 'wd1'), l)
        small['ffn1_norm'][l] = g
    land(dh)
    grad_x = dh.reshape(1, t, D)
    small['hgrn_lb_logits'] = list(_lower_bounds_bwd(lb_soft, jnp.concatenate(dlb, axis=0)))

    out = {}
    for f, ku, kd in (('ffn1', 'wgu1', 'wd1'), ('ffn2', 'wgu2', 'wd2')):
        res = _adamw_big(recv[ku], [(w[f + '_wg'], m[f + '_wg'], v[f + '_wg'], 0),
                                    (w[f + '_wu'], m[f + '_wu'], v[f + '_wu'], FFP)], 64, "adamw_wgu")
        out[f + '_wg'], out[f + '_wu'] = res
        out[f + '_wd'] = _adamw_big(recv[kd], [(w[f + '_wd'], m[f + '_wd'], v[f + '_wd'], 0)], 32, "adamw_wd")[0]
    out['w_in'] = _adamw_big(recv['win'], [(w['w_in'], m['w_in'], v['w_in'], 0)], 64, "adamw_win")[0]
    out['w_out'] = _adamw_big(recv['wout'], [(w['w_out'], m['w_out'], v['w_out'], 0)], 64, "adamw_wout")[0]

    parts = [jnp.stack([small[k][l].reshape(w[k].shape[1:]) for l in range(depth)]) for k in SMALL if k != 'final_norm']
    parts += [g_final.reshape(D), jnp.stack(dconv)]
    total = _sum_devices(_all_gather([_pack(parts)], "gather_small")[0])
    like = [w[k] for k in SMALL] + [jax.ShapeDtypeStruct((depth, 4, DB), F32)]
    grads = _unpack(total, like)
    gsmall = dict(zip(SMALL, grads[:-1]))
    gsmall['conv_w'] = lax.dynamic_slice_in_dim(grads[-1], my * (DB // NDEV), DB // NDEV, axis=2)
    keys = SMALL + ['conv_w']
    dl, mm, vv = _adamw_small(_pack([w[k] for k in keys]), _pack([gsmall[k] for k in keys]),
                              _pack([m[k] for k in keys]), _pack([v[k] for k in keys]))
    like = [w[k] for k in keys]
    for k, d_, m_, v_ in zip(keys, _unpack(dl, like), _unpack(mm, like), _unpack(vv, like)):
        out[k] = (gsmall[k], d_, m_, v_)

    return (loss, grad_x, *[out[k][0] for k in NAMES], *[out[k][1] for k in NAMES],
            *[out[k][2] for k in NAMES], *[out[k][3] for k in NAMES])


def kernel(x, ffn1_norm, ffn1_wg, ffn1_wu, ffn1_wd, mix_norm, w_in, hgrn_lb_logits, hgrn_norm, conv_w, conv_b, lru_wa, lru_ba, lru_wx, lru_bx, lru_lambda, lru_norm, sgu_w, sgu_b, sgu_norm, w_out, ffn2_norm, ffn2_wg, ffn2_wu, ffn2_wd, final_norm, loss_target, m_ffn1_norm, m_ffn1_wg, m_ffn1_wu, m_ffn1_wd, m_mix_norm, m_w_in, m_hgrn_lb_logits, m_hgrn_norm, m_conv_w, m_conv_b, m_lru_wa, m_lru_ba, m_lru_wx, m_lru_bx, m_lru_lambda, m_lru_norm, m_sgu_w, m_sgu_b, m_sgu_norm, m_w_out, m_ffn2_norm, m_ffn2_wg, m_ffn2_wu, m_ffn2_wd, m_final_norm, v_ffn1_norm, v_ffn1_wg, v_ffn1_wu, v_ffn1_wd, v_mix_norm, v_w_in, v_hgrn_lb_logits, v_hgrn_norm, v_conv_w, v_conv_b, v_lru_wa, v_lru_ba, v_lru_wx, v_lru_bx, v_lru_lambda, v_lru_norm, v_sgu_w, v_sgu_b, v_sgu_norm, v_w_out, v_ffn2_norm, v_ffn2_wg, v_ffn2_wu, v_ffn2_wd, v_final_norm):
    args = locals()
    w = {k: args[k] for k in NAMES}
    m = {k: args['m_' + k] for k in NAMES}
    v = {k: args['v_' + k] for k in NAMES}
    return _step(x, loss_target, w, m, v)
```

```python
import functools

import jax
import jax.numpy as jnp
from jax import lax
from jax.experimental import pallas as pl
from jax.experimental.pallas import tpu as pltpu

F32 = jnp.float32
MXU = jnp.bfloat16
SAVE = jnp.bfloat16
WIRE = jnp.bfloat16

NDEV = 8
D = 1024
FF = 2816
FFS = FF // NDEV
FFP = 384
DIN = 3072
DINS = DIN // NDEV
DA, DB, DC = 512, 256, 256
HD = 128
NH = DA // HD
ACH = 64
ACB = 4
CCH = 128
GRP = 64
EPS = 1e-6
LRU_C = 8.0
VMEM_LIMIT = 60 * 1024 * 1024
TM_F = 1024
TM_B = 512
TB = 512
SUB = 256

ADAM_LR, ADAM_B1, ADAM_B2, ADAM_EPS, ADAM_WD, ADAM_STEP = 0.001, 0.9, 0.999, 1e-08, 0.01, 10

MESH = pl.DeviceIdType.MESH


def _mm(a, b):
    return jnp.dot(a.astype(MXU), b.astype(MXU), preferred_element_type=F32)


def _mm_nt(a, b):
    return lax.dot_general(a.astype(MXU), b.astype(MXU), (((1,), (1,)), ((), ())), preferred_element_type=F32)


def _mm_tn(a, b):
    return lax.dot_general(a.astype(MXU), b.astype(MXU), (((0,), (0,)), ((), ())), preferred_element_type=F32)


def _split3(x):
    x1 = x.astype(MXU)
    r1 = x - x1.astype(F32)
    x2 = r1.astype(MXU)
    r2 = r1 - x2.astype(F32)
    return x1, x2, r2.astype(MXU)


def _mm_exact_l(c, x):
    x1, x2, x3 = _split3(x)
    return _mm(c, x1) + _mm(c, x2) + _mm(c, x3)


def _mm_exact_r(x, c):
    x1, x2, x3 = _split3(x)
    return _mm(x1, c) + _mm(x2, c) + _mm(x3, c)


def _sigmoid(x):
    return 1.0 / (1.0 + jnp.exp(-x))


def _gelu(x):
    c, k = 0.7978845608028654, 0.044715
    th = jnp.tanh(c * (x + k * x * x * x))
    return 0.5 * x * (1.0 + th)


def _gelu_and_grad(x):
    c, k = 0.7978845608028654, 0.044715
    th = jnp.tanh(c * (x + k * x * x * x))
    g = 0.5 * x * (1.0 + th)
    dg = 0.5 * (1.0 + th) + 0.5 * x * (1.0 - th * th) * c * (1.0 + 3.0 * k * x * x)
    return g, dg


def _expm1(x):
    series = x * (1.0 + x * (0.5 + x * (1.0 / 6.0 + x * (1.0 / 24.0 + x * (1.0 / 120.0)))))
    return jnp.where(jnp.abs(x) < 0.05, series, jnp.exp(x) - 1.0)


def _iota(shape, dim):
    return lax.broadcasted_iota(jnp.int32, shape, dim)


def _tri(n, lower):
    r, c = _iota((n, n), 0), _iota((n, n), 1)
    return jnp.where((r >= c) if lower else (r <= c), 1.0, 0.0).astype(F32)


def _group_matrix(n, value):
    r, c = _iota((n, n), 0), _iota((n, n), 1)
    return jnp.where((r // GRP) == (c // GRP), value, 0.0).astype(F32)


def _row(x, k):
    r = _iota(x.shape, 0)
    return jnp.sum(jnp.where(r == k, x, 0.0), axis=0, keepdims=True)


def _rms_bwd(dxn, hh, gain):
    rstd = lax.rsqrt(jnp.mean(hh * hh, axis=-1, keepdims=True) + EPS)
    xhat = hh * rstd
    dxh = dxn * gain
    dh = rstd * (dxh - xhat * jnp.mean(dxh * xhat, axis=-1, keepdims=True))
    return dh, jnp.sum(dxn * xhat, axis=0, keepdims=True)


def _params(sem):
    return pltpu.CompilerParams(dimension_semantics=sem, vmem_limit_bytes=VMEM_LIMIT)


def _all_gather(arrs, name):
    n = len(arrs)

    def body(*refs):
        ins, outs = refs[:n], refs[n:2 * n]
        send_sems, recv_sems, local_sems = refs[2 * n:]
        x, y, c = lax.axis_index("x"), lax.axis_index("y"), lax.axis_index("c")
        me, sibling = (x, y, c), (x, y, 1 - c)
        chips = [(1 - x, y), (x, 1 - y), (1 - x, 1 - y)]

        def slot(px, py, pc):
            return 4 * px + 2 * py + pc

        def copy(a, k, block, to, src=None):
            dst = outs[a].at[slot(*block)]
            return pltpu.make_async_remote_copy(
                src_ref=dst if src is None else src, dst_ref=dst,
                send_sem=send_sems.at[a * 7 + k], recv_sem=recv_sems.at[a * 7 + k],
                device_id=to, device_id_type=MESH)

        started = []
        for a in range(n):
            mine = pltpu.make_async_copy(ins[a], outs[a].at[slot(*me)], local_sems.at[a])
            mine.start()
            started.append(mine)
        first = []
        for a in range(n):
            first.append(copy(a, 0, me, sibling, src=ins[a]))
            first += [copy(a, 1 + j, me, (*chip, c), src=ins[a]) for j, chip in enumerate(chips)]
        for cp in first:
            cp.start()
        passed = []
        for a in range(n):
            for j, chip in enumerate(chips):
                copy(a, 1 + j, (*chip, c), me).wait_recv()
                fwd = copy(a, 4 + j, (*chip, c), sibling)
                fwd.start()
                passed.append(fwd)
        for a in range(n):
            copy(a, 0, sibling, me).wait_recv()
            for j, chip in enumerate(chips):
                copy(a, 4 + j, (*chip, 1 - c), me).wait_recv()
        for cp in first + passed:
            cp.wait_send()
        for mine in started:
            mine.wait()

    hbm = pl.BlockSpec(memory_space=pl.ANY)
    return pl.pallas_call(
        body, name=name,
        out_shape=[jax.ShapeDtypeStruct((NDEV,) + a.shape, a.dtype) for a in arrs],
        in_specs=[hbm] * n, out_specs=[hbm] * n,
        scratch_shapes=[pltpu.SemaphoreType.DMA((7 * n,)), pltpu.SemaphoreType.DMA((7 * n,)),
                        pltpu.SemaphoreType.DMA((n,))],
    )(*arrs)


def _peers():
    x, y, c = lax.axis_index("x"), lax.axis_index("y"), lax.axis_index("c")
    peers = [(x ^ ((k >> 2) & 1), y ^ ((k >> 1) & 1), c ^ (k & 1)) for k in range(1, NDEV)]
    return (x, y, c), 4 * x + 2 * y + c, peers


_HBM = pl.BlockSpec(memory_space=pltpu.HBM)
_SEM = pl.BlockSpec(memory_space=pltpu.SEMAPHORE)
_EFFECT = pltpu.SideEffectType.DATAFLOW_SIDE_EFFECTING


def _transfer_start(arrs, gather, name, deps=()):
    n, nd = len(arrs), len(deps)
    shapes = [((NDEV,) + a.shape) if gather else a.shape for a in arrs]

    def body(*refs):
        ins, lands = refs[:n], refs[n:2 * n]
        send_sems, recv_sems, local_sems = refs[2 * n + nd:2 * n + nd + 3]
        token = refs[-1]
        _, my, peers = _peers()
        for a in range(n):
            own = ins[a] if gather else ins[a].at[my]
            pltpu.make_async_copy(own, lands[a].at[my], local_sems.at[a]).start()
        for a in range(n):
            for peer in peers:
                src = ins[a] if gather else ins[a].at[4 * peer[0] + 2 * peer[1] + peer[2]]
                pltpu.make_async_remote_copy(
                    src_ref=src, dst_ref=lands[a].at[my], send_sem=send_sems.at[a], recv_sem=recv_sems.at[a],
                    device_id=peer, device_id_type=MESH).start()
        token[...] = jnp.zeros_like(token)

    out_shape = [pltpu.SemaphoreType.DMA((n,))] * 3
    out_shape += [pltpu.HBM(a.shape, a.dtype) for a in arrs]
    out_shape += [pltpu.HBM(s, a.dtype) for s, a in zip(shapes, arrs)]
    out_shape += [jax.ShapeDtypeStruct((8, 128), F32)]
    operands = [pltpu.with_memory_space_constraint(a, pltpu.HBM) for a in arrs]
    operands += [pltpu.with_memory_space_constraint(lax.empty(s, a.dtype), pltpu.HBM) for s, a in zip(shapes, arrs)]
    res = pl.pallas_call(
        body, name=name, out_shape=out_shape,
        in_specs=[_HBM] * (2 * n) + [pl.BlockSpec(memory_space=pl.ANY)] * nd,
        out_specs=[_SEM] * 3 + [_HBM] * (2 * n) + [pl.BlockSpec(memory_space=pltpu.VMEM)],
        input_output_aliases={i: 3 + i for i in range(2 * n)},
        compiler_params=pltpu.CompilerParams(has_side_effects=_EFFECT),
    )(*operands, *deps)
    return dict(sems=res[:3], src=res[3:3 + n], lands=res[3 + n:3 + 2 * n], token=res[-1], n=n)


def _transfer_wait(handle, after, name):
    n = handle["n"]

    def body(*refs):
        srcs, lands = refs[:n], refs[n:2 * n]
        send_sems, recv_sems, local_sems = refs[2 * n:2 * n + 3]
        me, _, _ = _peers()
        for a in range(n):
            seven = lands[a].at[pl.ds(0, NDEV - 1)]
            both = pltpu.make_async_remote_copy(
                src_ref=seven, dst_ref=seven, send_sem=send_sems.at[a], recv_sem=recv_sems.at[a],
                device_id=me, device_id_type=MESH)
            both.wait_send()
            both.wait_recv()
            pltpu.make_async_copy(lands[a].at[0], lands[a].at[1], local_sems.at[a]).wait()

    src, lands = handle["src"], handle["lands"]
    res = pl.pallas_call(
        body, name=name,
        out_shape=[pltpu.HBM(a.shape, a.dtype) for a in list(src) + list(lands)],
        in_specs=[_HBM] * (2 * n) + [_SEM] * 3 + [pl.BlockSpec(memory_space=pl.ANY)],
        out_specs=[_HBM] * (2 * n),
        input_output_aliases={i: i for i in range(2 * n)},
        compiler_params=pltpu.CompilerParams(has_side_effects=_EFFECT),
    )(*src, *lands, *handle["sems"], after)
    return list(res[n:])


def _ffn_fwd(h, gain, wgu, wd, tm):
    t = h.shape[0]

    def body(h_ref, g_ref, wgu_ref, wd_ref, out_ref, xn_ref, ab_ref, acc_ref):
        j = pl.program_id(1)

        @pl.when(j == 0)
        def _():
            hh = h_ref[...]
            rstd = lax.rsqrt(jnp.mean(hh * hh, axis=-1, keepdims=True) + EPS)
            xn_ref[...] = (hh * rstd * g_ref[...]).astype(xn_ref.dtype)
            acc_ref[...] = jnp.zeros_like(acc_ref)

        sub = min(SUB, tm)
        for r in range(tm // sub):
            rows = slice(r * sub, (r + 1) * sub)
            ab = _mm_nt(xn_ref[rows, :], wgu_ref[...])
            ab_ref[rows, :] = ab.astype(ab_ref.dtype)
            a, b = ab[:, :FFP], ab[:, FFP:]
            s = a * _sigmoid(a) * b
            acc_ref[rows, :] += _mm(s, wd_ref[...])

        @pl.when(j == NDEV - 1)
        def _():
            out_ref[...] = h_ref[...] + 0.5 * acc_ref[...]

    return pl.pallas_call(
        body, name="ffn_fwd", grid=(t // tm, NDEV),
        in_specs=[pl.BlockSpec((tm, D), lambda i, j: (i, 0)),
                  pl.BlockSpec((1, D), lambda i, j: (0, 0)),
                  pl.BlockSpec((None, 2 * FFP, D), lambda i, j: (j, 0, 0)),
                  pl.BlockSpec((None, FFP, D), lambda i, j: (j, 0, 0))],
        out_specs=[pl.BlockSpec((tm, D), lambda i, j: (i, 0)),
                   pl.BlockSpec((tm, D), lambda i, j: (i, 0)),
                   pl.BlockSpec((tm, 2 * FFP), lambda i, j: (i, j))],
        out_shape=[jax.ShapeDtypeStruct((t, D), F32), jax.ShapeDtypeStruct((t, D), SAVE),
                   jax.ShapeDtypeStruct((t, NDEV * 2 * FFP), SAVE)],
        scratch_shapes=[pltpu.VMEM((tm, D), F32)],
        compiler_params=_params(("parallel", "arbitrary")),
    )(h, gain, wgu, wd)


def _ffn_bwd(dout, h, gain, xn, ab, wgu, wd, tm):
    t = h.shape[0]
    nt = t // tm
    last = NDEV - 1

    def body(dout_ref, h_ref, g_ref, xn_ref, ab_ref, wgu_ref, wd_ref,
             dh_ref, dwgu_ref, dwd_ref, dgain_ref, dxn_scr, agu_scr, awd_scr):
        j, i = pl.program_id(0), pl.program_id(1)

        @pl.when(i == 0)
        def _():
            agu_scr[...] = jnp.zeros_like(agu_scr)
            awd_scr[...] = jnp.zeros_like(awd_scr)

        @pl.when((i == 0) & (j == 0))
        def _():
            dgain_ref[...] = jnp.zeros_like(dgain_ref)

        sub = min(SUB, tm)
        dys, dabs, ss, parts = [], [], [], []
        for r in range(tm // sub):
            rws = slice(r * sub, (r + 1) * sub)
            dy = (0.5 * dout_ref[rws, :]).astype(MXU)
            ds = _mm_nt(dy, wd_ref[...])
            ab_ = ab_ref[rws, :].astype(F32)
            a, b = ab_[:, :FFP], ab_[:, FFP:]
            sg = _sigmoid(a)
            sa = a * sg
            db = ds * sa
            da = ds * b * (sg * (1.0 + a * (1.0 - sg)))
            dab = jnp.concatenate([da, db], axis=1).astype(MXU)
            dys.append(dy)
            dabs.append(dab)
            ss.append((sa * b).astype(MXU))
            parts.append(_mm(dab, wgu_ref[...]))
        dy, dab, s, part = [jnp.concatenate(v, axis=0) for v in (dys, dabs, ss, parts)]
        agu_scr[...] += _mm_tn(dab, xn_ref[...])
        awd_scr[...] += _mm_tn(s, dy)
        rows = pl.ds(pl.multiple_of(i * tm, tm), tm)

        @pl.when(j == 0)
        def _():
            dxn_scr[rows, :] = part

        @pl.when((j > 0) & (j < last))
        def _():
            dxn_scr[rows, :] += part

        @pl.when(j == last)
        def _():
            dxn = dxn_scr[rows, :] + part
            dh, dg = _rms_bwd(dxn, h_ref[...], g_ref[...])
            dh_ref[...] = dout_ref[...] + dh
            dgain_ref[...] += dg

        @pl.when(i == nt - 1)
        def _():
            dwgu_ref[...] = agu_scr[...].astype(dwgu_ref.dtype)
            dwd_ref[...] = awd_scr[...].astype(dwd_ref.dtype)

    def tail(j, i):
        return (jnp.where(j == last, i, 0), 0)

    return pl.pallas_call(
        body, name="ffn_bwd", grid=(NDEV, nt),
        in_specs=[pl.BlockSpec((tm, D), lambda j, i: (i, 0)),
                  pl.BlockSpec((tm, D), tail),
                  pl.BlockSpec((1, D), lambda j, i: (0, 0)),
                  pl.BlockSpec((tm, D), lambda j, i: (i, 0)),
                  pl.BlockSpec((tm, 2 * FFP), lambda j, i: (i, j)),
                  pl.BlockSpec((None, 2 * FFP, D), lambda j, i: (j, 0, 0)),
                  pl.BlockSpec((None, FFP, D), lambda j, i: (j, 0, 0))],
        out_specs=[pl.BlockSpec((tm, D), tail),
                   pl.BlockSpec((None, 2 * FFP, D), lambda j, i: (j, 0, 0)),
                   pl.BlockSpec((None, FFP, D), lambda j, i: (j, 0, 0)),
                   pl.BlockSpec((1, D), lambda j, i: (0, 0))],
        out_shape=[jax.ShapeDtypeStruct((t, D), F32),
                   jax.ShapeDtypeStruct((NDEV, 2 * FFP, D), WIRE),
                   jax.ShapeDtypeStruct((NDEV, FFP, D), WIRE),
                   jax.ShapeDtypeStruct((1, D), F32)],
        scratch_shapes=[pltpu.VMEM((t, D), F32), pltpu.VMEM((2 * FFP, D), F32), pltpu.VMEM((FFP, D), F32)],
        compiler_params=_params(("arbitrary", "arbitrary")),
    )(dout, h, gain, xn, ab, wgu, wd)


def _inproj_fwd(h, gain, win, tm):
    t = h.shape[0]

    def body(h_ref, g_ref, w_ref, z_ref, xn_ref):
        hh = h_ref[...]
        rstd = lax.rsqrt(jnp.mean(hh * hh, axis=-1, keepdims=True) + EPS)
        xn = (hh * rstd * g_ref[...]).astype(MXU)
        xn_ref[...] = xn.astype(xn_ref.dtype)
        for j in range(NDEV):
            z_ref[:, j * DINS:(j + 1) * DINS] = _mm(xn, w_ref[j])

    return pl.pallas_call(
        body, name="inproj_fwd", grid=(t // tm,),
        in_specs=[pl.BlockSpec((tm, D), lambda i: (i, 0)),
                  pl.BlockSpec((1, D), lambda i: (0, 0)),
                  pl.BlockSpec((NDEV, D, DINS), lambda i: (0, 0, 0))],
        out_specs=[pl.BlockSpec((tm, DIN), lambda i: (i, 0)),
                   pl.BlockSpec((tm, D), lambda i: (i, 0))],
        out_shape=[jax.ShapeDtypeStruct((t, DIN), F32), jax.ShapeDtypeStruct((t, D), SAVE)],
        compiler_params=_params(("parallel",)),
    )(h, gain, win)


def _inproj_bwd_x(dres, dz, h, gain, win, tm):
    t = h.shape[0]

    def body(dres_ref, dz_ref, h_ref, g_ref, w_ref, dh_ref, dgain_ref):
        @pl.when(pl.program_id(0) == 0)
        def _():
            dgain_ref[...] = jnp.zeros_like(dgain_ref)

        dxn = _mm_nt(dz_ref[:, :DINS], w_ref[0])
        for j in range(1, NDEV):
            dxn = dxn + _mm_nt(dz_ref[:, j * DINS:(j + 1) * DINS], w_ref[j])
        dh, dg = _rms_bwd(dxn, h_ref[...], g_ref[...])
        dh_ref[...] = dres_ref[...] + dh
        dgain_ref[...] += dg

    return pl.pallas_call(
        body, name="inproj_bwd_x", grid=(t // tm,),
        in_specs=[pl.BlockSpec((tm, D), lambda i: (i, 0)),
                  pl.BlockSpec((tm, DIN), lambda i: (i, 0)),
                  pl.BlockSpec((tm, D), lambda i: (i, 0)),
                  pl.BlockSpec((1, D), lambda i: (0, 0)),
                  pl.BlockSpec((NDEV, D, DINS), lambda i: (0, 0, 0))],
        out_specs=[pl.BlockSpec((tm, D), lambda i: (i, 0)),
                   pl.BlockSpec((1, D), lambda i: (0, 0))],
        out_shape=[jax.ShapeDtypeStruct((t, D), F32), jax.ShapeDtypeStruct((1, D), F32)],
        compiler_params=_params(("arbitrary",)),
    )(dres, dz, h, gain, win)


def _inproj_bwd_w(xn, dz, tm):
    t = xn.shape[0]
    nt = t // tm

    def body(xn_ref, dz_ref, dw_ref, acc_scr):
        i = pl.program_id(1)

        @pl.when(i == 0)
        def _():
            acc_scr[...] = jnp.zeros_like(acc_scr)

        acc_scr[...] += _mm_tn(xn_ref[...], dz_ref[...])

        @pl.when(i == nt - 1)
        def _():
            dw_ref[...] = acc_scr[...].astype(dw_ref.dtype)

    return pl.pallas_call(
        body, name="inproj_bwd_w", grid=(NDEV, nt),
        in_specs=[pl.BlockSpec((tm, D), lambda j, i: (i, 0)),
                  pl.BlockSpec((tm, DINS), lambda j, i: (i, j))],
        out_specs=pl.BlockSpec((None, D, DINS), lambda j, i: (j, 0, 0)),
        out_shape=jax.ShapeDtypeStruct((NDEV, D, DINS), WIRE),
        scratch_shapes=[pltpu.VMEM((D, DINS), F32)],
        compiler_params=_params(("parallel", "arbitrary")),
    )(xn, dz)


def _outproj_fwd(h, oa, ob, oc, wout, tm):
    t = h.shape[0]

    def body(h_ref, oa_ref, ob_ref, oc_ref, w_ref, out_ref):
        ym = jnp.concatenate([oa_ref[...], ob_ref[...], oc_ref[...]], axis=1)
        out_ref[...] = h_ref[...] + _mm(ym, w_ref[...])

    return pl.pallas_call(
        body, name="outproj_fwd", grid=(t // tm,),
        in_specs=[pl.BlockSpec((tm, D), lambda i: (i, 0)),
                  pl.BlockSpec((tm, DA), lambda i: (i, 0)),
                  pl.BlockSpec((tm, DB), lambda i: (i, 0)),
                  pl.BlockSpec((tm, DC), lambda i: (i, 0)),
                  pl.BlockSpec((D, D), lambda i: (0, 0))],
        out_specs=pl.BlockSpec((tm, D), lambda i: (i, 0)),
        out_shape=jax.ShapeDtypeStruct((t, D), F32),
        compiler_params=_params(("parallel",)),
    )(h, oa, ob, oc, wout)


def _outproj_bwd(dh, oa, ob, oc, wout, tm):
    t = dh.shape[0]
    nt = t // tm

    def body(dh_ref, oa_ref, ob_ref, oc_ref, w_ref, da_ref, db_ref, dc_ref, dw_ref, acc_scr):
        i = pl.program_id(0)

        @pl.when(i == 0)
        def _():
            acc_scr[...] = jnp.zeros_like(acc_scr)

        d16 = dh_ref[...].astype(MXU)
        dym = _mm_nt(d16, w_ref[...])
        da_ref[...] = dym[:, :DA]
        db_ref[...] = dym[:, DA:DA + DB]
        dc_ref[...] = dym[:, DA + DB:]
        ym = jnp.concatenate([oa_ref[...], ob_ref[...], oc_ref[...]], axis=1)
        acc_scr[...] += _mm_tn(ym, d16)

        @pl.when(i == nt - 1)
        def _():
            dw_ref[...] = acc_scr[...].astype(dw_ref.dtype)

    return pl.pallas_call(
        body, name="outproj_bwd", grid=(nt,),
        in_specs=[pl.BlockSpec((tm, D), lambda i: (i, 0)),
                  pl.BlockSpec((tm, DA), lambda i: (i, 0)),
                  pl.BlockSpec((tm, DB), lambda i: (i, 0)),
                  pl.BlockSpec((tm, DC), lambda i: (i, 0)),
                  pl.BlockSpec((D, D), lambda i: (0, 0))],
        out_specs=[pl.BlockSpec((tm, DA), lambda i: (i, 0)),
                   pl.BlockSpec((tm, DB), lambda i: (i, 0)),
                   pl.BlockSpec((tm, DC), lambda i: (i, 0)),
                   pl.BlockSpec((D, D), lambda i: (0, 0))],
        out_shape=[jax.ShapeDtypeStruct((t, DA), F32), jax.ShapeDtypeStruct((t, DB), F32),
                   jax.ShapeDtypeStruct((t, DC), F32), jax.ShapeDtypeStruct((D, D), WIRE)],
        scratch_shapes=[pltpu.VMEM((D, D), F32)],
        compiler_params=_params(("arbitrary",)),
    )(dh, oa, ob, oc, wout)


def _lower_bounds(logits):
    depth, n = logits.shape

    def body(l_ref, lb_ref, p_ref):
        rows = [l_ref[l:l + 1, :] for l in range(depth)]
        mx = functools.reduce(jnp.maximum, rows)
        ex = [jnp.exp(r - mx) for r in rows]
        den = functools.reduce(lambda u, v: u + v, ex)
        acc = jnp.zeros_like(den)
        for l in range(depth):
            p = ex[l] / den
            p_ref[l:l + 1, :] = p
            if l > 0:
                acc = acc + p
            lb_ref[l:l + 1, :] = acc

    return pl.pallas_call(
        body, name="lower_bounds",
        out_shape=[jax.ShapeDtypeStruct((depth, n), F32), jax.ShapeDtypeStruct((depth, n), F32)],
    )(logits)


def _lower_bounds_bwd(p, dlb):
    depth, n = p.shape

    def body(p_ref, d_ref, out_ref):
        ps = [p_ref[l:l + 1, :] for l in range(depth)]
        ds = [d_ref[l:l + 1, :] for l in range(depth)]
        dp = [jnp.zeros_like(ps[0]) for _ in range(depth)]
        run = jnp.zeros_like(ps[0])
        for l in range(depth - 1, 0, -1):
            run = run + ds[l]
            dp[l] = run
        dot = functools.reduce(lambda u, v: u + v, [ps[l] * dp[l] for l in range(depth)])
        for l in range(depth):
            out_ref[l:l + 1, :] = ps[l] * (dp[l] - dot)

    return pl.pallas_call(body, name="lower_bounds_bwd", out_shape=jax.ShapeDtypeStruct((depth, n), F32))(p, dlb)


def _hgrn_chunk(z_ref, lb_ref, hd, rows):
    c0 = hd * HD
    q = z_ref[rows, c0:c0 + HD]
    fl = z_ref[rows, DA + c0:DA + c0 + HD]
    v = z_ref[rows, 2 * DA + c0:2 * DA + c0 + HD]
    g = z_ref[rows, 3 * DA + c0:3 * DA + c0 + HD]
    lb = lb_ref[:, c0:c0 + HD]
    sq = _sigmoid(q)
    qs = q * sq
    sg = _sigmoid(fl)
    f = lb + (1.0 - lb) * sg
    k = 1.0 - f
    lf = jnp.log(f)
    b = _mm_exact_l(_tri(ACH, True).astype(MXU), lf)
    bend = jnp.sum(lf, axis=0, keepdims=True)
    r = 0.5 * bend
    eq, ek, eb, ed = jnp.exp(b - r), jnp.exp(r - b), jnp.exp(b), jnp.exp(bend - b)
    qt, kt, qe, kd = qs * eq, k * ek, qs * eb, k * ed
    causal = _iota((ACH, ACH), 0) >= _iota((ACH, ACH), 1)
    att = jnp.where(causal, _mm_nt(qt, kt), 0.0)
    return dict(q=q, v=v, g=g, lb=lb, sq=sq, qs=qs, sg=sg, f=f, k=k, bend=bend, eq=eq, ek=ek, eb=eb, ed=ed,
                qt=qt, kt=kt, qe=qe, kd=kd, att=att, causal=causal)


def _hgrn_fwd(z, lb, gain):
    t = z.shape[0]
    nc = t // ACH
    cb = min(ACB, nc)
    rb = cb * ACH

    def body(z_ref, lb_ref, g_ref, o_ref, oa_ref, st_ref, st_scr):
        @pl.when(pl.program_id(0) == 0)
        def _():
            st_scr[...] = jnp.zeros_like(st_scr)

        for hd in range(NH):
            st = st_scr[hd]
            cols = slice(hd * HD, (hd + 1) * HD)
            for cc in range(cb):
                rows = slice(cc * ACH, (cc + 1) * ACH)
                c = _hgrn_chunk(z_ref, lb_ref, hd, rows)
                st_ref[cc, hd] = st
                o = _mm(c["att"], c["v"]) + _mm_nt(c["qe"], st)
                st = st * jnp.exp(c["bend"]) + _mm_tn(c["v"], c["kd"])
                o_ref[rows, cols] = o
                rstd = lax.rsqrt(jnp.mean(o * o, axis=-1, keepdims=True) + EPS)
                gg = c["g"]
                oa_ref[rows, cols] = (o * rstd * g_ref[:, cols] * (gg * _sigmoid(gg))).astype(oa_ref.dtype)
            st_scr[hd] = st

    return pl.pallas_call(
        body, name="hgrn_fwd", grid=(nc // cb,),
        in_specs=[pl.BlockSpec((rb, 4 * DA), lambda c: (c, 0)),
                  pl.BlockSpec((1, DA), lambda c: (0, 0)),
                  pl.BlockSpec((1, DA), lambda c: (0, 0))],
        out_specs=[pl.BlockSpec((rb, DA), lambda c: (c, 0)),
                   pl.BlockSpec((rb, DA), lambda c: (c, 0)),
                   pl.BlockSpec((cb, NH, HD, HD), lambda c: (c, 0, 0, 0))],
        out_shape=[jax.ShapeDtypeStruct((t, DA), F32), jax.ShapeDtypeStruct((t, DA), SAVE),
                   jax.ShapeDtypeStruct((nc, NH, HD, HD), F32)],
        scratch_shapes=[pltpu.VMEM((NH, HD, HD), F32)],
        compiler_params=_params(("arbitrary",)),
    )(z, lb, gain)


def _hgrn_bwd(z, lb, gain, o, states, doa):
    t = z.shape[0]
    nc = t // ACH
    cb = min(ACB, nc)
    rb = cb * ACH
    nblk = nc // cb

    def body(z_ref, lb_ref, g_ref, o_ref, st_ref, doa_ref, dz_ref, dgain_ref, dlb_ref, dst_scr):
        @pl.when(pl.program_id(0) == 0)
        def _():
            dst_scr[...] = jnp.zeros_like(dst_scr)
            dgain_ref[...] = jnp.zeros_like(dgain_ref)
            dlb_ref[...] = jnp.zeros_like(dlb_ref)

        upper = _tri(ACH, False).astype(MXU)
        for hd in range(NH):
            cols = slice(hd * HD, (hd + 1) * HD)
            gain = g_ref[:, cols]
            dsp = dst_scr[hd]
            dgain = jnp.zeros((1, HD), F32)
            dlb = jnp.zeros((1, HD), F32)
            for cc in reversed(range(cb)):
                rows = slice(cc * ACH, (cc + 1) * ACH)
                c = _hgrn_chunk(z_ref, lb_ref, hd, rows)
                o = o_ref[rows, cols]
                do_a = doa_ref[rows, cols]
                gg = c["g"]
                sgg = _sigmoid(gg)
                silu_g = gg * sgg
                rstd = lax.rsqrt(jnp.mean(o * o, axis=-1, keepdims=True) + EPS)
                n = o * rstd
                dn = do_a * gain * silu_g
                dg = do_a * n * gain * (sgg * (1.0 + gg * (1.0 - sgg)))
                dgain = dgain + jnp.sum(do_a * silu_g * n, axis=0, keepdims=True)
                d_o = rstd * (dn - n * jnp.mean(dn * n, axis=-1, keepdims=True))

                st = st_ref[cc, hd]
                datt = jnp.where(c["causal"], _mm_nt(d_o, c["v"]), 0.0)
                dv = _mm_tn(c["att"], d_o) + _mm_nt(c["kd"], dsp)
                dqt = _mm(datt, c["kt"])
                dqe = _mm(d_o, st)
                dkt = _mm_tn(datt, c["qt"])
                dkd = _mm(c["v"], dsp)
                decay = jnp.exp(c["bend"])
                dbend = (decay * jnp.sum(st * dsp, axis=0, keepdims=True)
                         + jnp.sum(dkd * c["kd"], axis=0, keepdims=True))
                dsp = dsp * decay + _mm_tn(d_o, c["qe"])
                db = dqt * c["qt"] + dqe * c["qe"] - dkt * c["kt"] - dkd * c["kd"]
                dqs = dqt * c["eq"] + dqe * c["eb"]
                dk = dkt * c["ek"] + dkd * c["ed"]
                dlf = _mm_exact_l(upper, db) + dbend
                df = dlf / c["f"] - dk
                sg = c["sg"]
                dlb = dlb + jnp.sum(df * (1.0 - sg), axis=0, keepdims=True)
                dfl = df * (1.0 - c["lb"]) * sg * (1.0 - sg)
                sq, q = c["sq"], c["q"]
                dq = dqs * (sq * (1.0 + q * (1.0 - sq)))
                c0 = hd * HD
                dz_ref[rows, c0:c0 + HD] = dq.astype(dz_ref.dtype)
                dz_ref[rows, DA + c0:DA + c0 + HD] = dfl.astype(dz_ref.dtype)
                dz_ref[rows, 2 * DA + c0:2 * DA + c0 + HD] = dv.astype(dz_ref.dtype)
                dz_ref[rows, 3 * DA + c0:3 * DA + c0 + HD] = dg.astype(dz_ref.dtype)
            dst_scr[hd] = dsp
            dgain_ref[:, cols] += dgain
            dlb_ref[:, cols] += dlb

    rev = lambda c: (nblk - 1 - c, 0)
    return pl.pallas_call(
        body, name="hgrn_bwd", grid=(nblk,),
        in_specs=[pl.BlockSpec((rb, 4 * DA), rev),
                  pl.BlockSpec((1, DA), lambda c: (0, 0)),
                  pl.BlockSpec((1, DA), lambda c: (0, 0)),
                  pl.BlockSpec((rb, DA), rev),
                  pl.BlockSpec((cb, NH, HD, HD), lambda c: (nblk - 1 - c, 0, 0, 0)),
                  pl.BlockSpec((rb, DA), rev)],
        out_specs=[pl.BlockSpec((rb, 4 * DA), rev),
                   pl.BlockSpec((1, DA), lambda c: (0, 0)),
                   pl.BlockSpec((1, DA), lambda c: (0, 0))],
        out_shape=[jax.ShapeDtypeStruct((t, DIN), SAVE), jax.ShapeDtypeStruct((1, DA), F32),
                   jax.ShapeDtypeStruct((1, DA), F32)],
        scratch_shapes=[pltpu.VMEM((NH, HD, HD), F32)],
        compiler_params=_params(("arbitrary",)),
    )(z, lb, gain, o, states, doa)


def _shift_down(prev8, x, k):
    cat = jnp.concatenate([prev8, x], axis=0)
    return pltpu.roll(cat, k, axis=0)[8:, :]


def _shift_up(x, next8, k):
    n = x.shape[0]
    cat = jnp.concatenate([x, next8], axis=0)
    return pltpu.roll(cat, n + 8 - k, axis=0)[:n, :]


def _lru_gates(x, prev8, cw_ref, vec_ref, wa_ref, wx_ref):
    xs = [x, _shift_down(prev8, x, 1), _shift_down(prev8, x, 2), _shift_down(prev8, x, 3)]
    xc = vec_ref[0:1, :] + cw_ref[3:4, :] * xs[0] + cw_ref[2:3, :] * xs[1] + cw_ref[1:2, :] * xs[2] + cw_ref[0:1, :] * xs[3]
    r = _sigmoid(_mm(xc, wa_ref[...]) + vec_ref[1:2, :])
    gi = _sigmoid(_mm(xc, wx_ref[...]) + vec_ref[2:3, :])
    lam = vec_ref[3:4, :]
    sp = jnp.maximum(-lam, 0.0) + jnp.log(1.0 + jnp.exp(-jnp.abs(lam)))
    la = -LRU_C * r * sp
    a = jnp.exp(la)
    mult = jnp.sqrt(-_expm1(2.0 * la))
    return xs, xc, r, gi, sp, a, mult


def _scan_down(a, u):
    n = a.shape[0]
    row = _iota(a.shape, 0)
    s = 1
    while s < n:
        keep = row >= s
        ash = jnp.where(keep, pltpu.roll(a, s, axis=0), 1.0)
        ush = jnp.where(keep, pltpu.roll(u, s, axis=0), 0.0)
        u = a * ush + u
        a = a * ash
        s *= 2
    return a, u


def _scan_up(a, u):
    n = a.shape[0]
    row = _iota(a.shape, 0)
    s = 1
    while s < n:
        keep = row < n - s
        ash = jnp.where(keep, pltpu.roll(a, n - s, axis=0), 1.0)
        ush = jnp.where(keep, pltpu.roll(u, n - s, axis=0), 0.0)
        u = a * ush + u
        a = a * ash
        s *= 2
    return a, u


def _lru_fwd(z, cw, vec, wa, wx, tb):
    t = z.shape[0]
    xcol, gcol = (4 * DA) // DB, (4 * DA) // DB + 1

    def body(x_ref, gate_ref, cw_ref, vec_ref, wa_ref, wx_ref, ob_ref, h_ref, xprev_scr, hc_scr):
        @pl.when(pl.program_id(0) == 0)
        def _():
            xprev_scr[...] = jnp.zeros_like(xprev_scr)
            hc_scr[...] = jnp.zeros_like(hc_scr)

        x = x_ref[...]
        _, xc, _, gi, _, a, mult = _lru_gates(x, xprev_scr[...], cw_ref, vec_ref, wa_ref, wx_ref)
        acum, hloc = _scan_down(a, mult * gi * xc)
        h = hloc + acum * hc_scr[0:1, :]
        h_ref[...] = h
        hc_scr[...] = jnp.broadcast_to(_row(h, tb - 1), hc_scr.shape)
        xprev_scr[...] = x[tb - 8:, :]
        y = h * _gelu(gate_ref[...])
        ms = _mm_exact_r(y * y, _group_matrix(DB, 1.0 / GRP).astype(MXU))
        ob_ref[...] = (y * lax.rsqrt(ms + EPS) * vec_ref[4:5, :]).astype(ob_ref.dtype)

    return pl.pallas_call(
        body, name="lru_fwd", grid=(t // tb,),
        in_specs=[pl.BlockSpec((tb, DB), lambda i: (i, xcol)),
                  pl.BlockSpec((tb, DB), lambda i: (i, gcol)),
                  pl.BlockSpec((8, DB), lambda i: (0, 0)),
                  pl.BlockSpec((8, DB), lambda i: (0, 0)),
                  pl.BlockSpec((DB, DB), lambda i: (0, 0)),
                  pl.BlockSpec((DB, DB), lambda i: (0, 0))],
        out_specs=[pl.BlockSpec((tb, DB), lambda i: (i, 0)),
                   pl.BlockSpec((tb, DB), lambda i: (i, 0))],
        out_shape=[jax.ShapeDtypeStruct((t, DB), SAVE), jax.ShapeDtypeStruct((t, DB), F32)],
        scratch_shapes=[pltpu.VMEM((8, DB), F32), pltpu.VMEM((8, DB), F32)],
        compiler_params=_params(("arbitrary",)),
    )(z, z, cw, vec, wa, wx)


def _lru_bwd(z, hseq, dob, cw, vec, wa, wx, dz, tb):
    t = z.shape[0]
    nb = t // tb
    xcol, gcol = (4 * DA) // DB, (4 * DA) // DB + 1
    per = tb // 8

    def body(x_ref, xh_ref, gate_ref, h_ref, hh_ref, dob_ref, cw_ref, vec_ref, wa_ref, wx_ref, _,
             dz_ref, dcw_ref, dvec_ref, dwa_ref, dwx_ref, gc_scr, an_scr, dxc_scr):
        step = pl.program_id(0)
        blk = nb - 1 - step

        @pl.when(step == 0)
        def _():
            for ref in (gc_scr, an_scr, dxc_scr, dcw_ref, dvec_ref, dwa_ref, dwx_ref):
                ref[...] = jnp.zeros_like(ref)

        first = (blk > 0).astype(F32)
        x = x_ref[...]
        xs, xc, r, gi, sp, a, mult = _lru_gates(x, xh_ref[...] * first, cw_ref, vec_ref, wa_ref, wx_ref)
        h = h_ref[...]
        hprev = _shift_down(hh_ref[...] * first, h, 1)
        ge, dge = _gelu_and_grad(gate_ref[...])
        y = h * ge
        gmat = _group_matrix(DB, 1.0 / GRP).astype(MXU)
        rstd = lax.rsqrt(_mm_exact_r(y * y, gmat) + EPS)
        n = y * rstd
        d_ob = dob_ref[...]
        dn = d_ob * vec_ref[4:5, :]
        dvec_ref[4:5, :] += jnp.sum(d_ob * n, axis=0, keepdims=True)
        dy = rstd * (dn - n * _mm_exact_r(dn * n, gmat))
        dh = dy * ge
        dgate = dy * h * dge

        row = _iota(a.shape, 0)
        anext = jnp.where(row == tb - 1, an_scr[0:1, :], pltpu.roll(a, tb - 1, axis=0))
        acum, gloc = _scan_up(anext, dh)
        g = gloc + acum * gc_scr[0:1, :]
        gc_scr[...] = jnp.broadcast_to(_row(g, 0), gc_scr.shape)
        an_scr[...] = jnp.broadcast_to(_row(a, 0), an_scr.shape)

        da = g * hprev
        dmult = g * gi * xc
        dgi = g * mult * xc
        dxc = g * mult * gi
        dla = da * a - dmult * (a * a) / mult
        dr = dla * (-LRU_C * sp)
        dsp = jnp.sum(dla * (-LRU_C * r), axis=0, keepdims=True)
        lam = vec_ref[3:4, :]
        dvec_ref[3:4, :] += -dsp * _sigmoid(-lam)
        dpa = dr * r * (1.0 - r)
        dpx = dgi * gi * (1.0 - gi)
        dwa_ref[...] += _mm_tn(xc, dpa)
        dwx_ref[...] += _mm_tn(xc, dpx)
        dvec_ref[1:2, :] += jnp.sum(dpa, axis=0, keepdims=True)
        dvec_ref[2:3, :] += jnp.sum(dpx, axis=0, keepdims=True)
        dxc = dxc + _mm_nt(dpa, wa_ref[...]) + _mm_nt(dpx, wx_ref[...])
        dvec_ref[0:1, :] += jnp.sum(dxc, axis=0, keepdims=True)
        for tap in range(4):
            dcw_ref[tap:tap + 1, :] += jnp.sum(dxc * xs[3 - tap], axis=0, keepdims=True)
        nxt = dxc_scr[...]
        dx = (cw_ref[3:4, :] * dxc + cw_ref[2:3, :] * _shift_up(dxc, nxt, 1)
              + cw_ref[1:2, :] * _shift_up(dxc, nxt, 2) + cw_ref[0:1, :] * _shift_up(dxc, nxt, 3))
        dxc_scr[...] = dxc[:8, :]
        dz_ref[:, :DB] = dx.astype(dz_ref.dtype)
        dz_ref[:, DB:] = dgate.astype(dz_ref.dtype)

    def halo(col):
        return lambda s: (jnp.maximum((nb - 1 - s) * per - 1, 0), col)

    const = lambda s: (0, 0)
    return pl.pallas_call(
        body, name="lru_bwd", grid=(nb,),
        in_specs=[pl.BlockSpec((tb, DB), lambda s: (nb - 1 - s, xcol)),
                  pl.BlockSpec((8, DB), halo(xcol)),
                  pl.BlockSpec((tb, DB), lambda s: (nb - 1 - s, gcol)),
                  pl.BlockSpec((tb, DB), lambda s: (nb - 1 - s, 0)),
                  pl.BlockSpec((8, DB), halo(0)),
                  pl.BlockSpec((tb, DB), lambda s: (nb - 1 - s, 0)),
                  pl.BlockSpec((8, DB), const), pl.BlockSpec((8, DB), const),
                  pl.BlockSpec((DB, DB), const), pl.BlockSpec((DB, DB), const),
                  pl.BlockSpec(memory_space=pl.ANY)],
        out_specs=[pl.BlockSpec((tb, 2 * DB), lambda s: (nb - 1 - s, (4 * DA) // (2 * DB))),
                   pl.BlockSpec((8, DB), const), pl.BlockSpec((8, DB), const),
                   pl.BlockSpec((DB, DB), const), pl.BlockSpec((DB, DB), const)],
        out_shape=[jax.ShapeDtypeStruct((t, DIN), SAVE), jax.ShapeDtypeStruct((8, DB), F32),
                   jax.ShapeDtypeStruct((8, DB), F32), jax.ShapeDtypeStruct((DB, DB), F32),
                   jax.ShapeDtypeStruct((DB, DB), F32)],
        scratch_shapes=[pltpu.VMEM((8, DB), F32), pltpu.VMEM((8, DB), F32), pltpu.VMEM((8, DB), F32)],
        input_output_aliases={10: 0},
        compiler_params=_params(("arbitrary",)),
    )(z, z, z, hseq, hseq, dob, cw, vec, wa, wx, dz)


def _sgu_chunk(u_in, v_in, w_ref, bias, gmat):
    uu, duu = _gelu_and_grad(u_in)
    vv, dvv = _gelu_and_grad(v_in)
    mu = _mm_exact_r(vv, gmat)
    dlt = vv - mu
    rstd_v = lax.rsqrt(_mm_exact_r(dlt * dlt, gmat) + EPS)
    vn = dlt * rstd_v
    col = _iota((CCH, DC), 1) // GRP
    causal = _iota((CCH, CCH), 0) >= _iota((CCH, CCH), 1)
    ws = [jnp.where(causal, w_ref[g], 0.0) for g in range(DC // GRP)]
    zz = bias
    for g, w in enumerate(ws):
        zz = zz + jnp.where(col == g, _mm(w, vn), 0.0)
    return uu, duu, dvv, rstd_v, vn, zz, ws, col, causal


def _sgu_fwd(z, w, bias, gain, tb):
    t = z.shape[0]
    ucol, vcol = (4 * DA + 2 * DB) // DC, (4 * DA + 2 * DB) // DC + 1

    def body(u_ref, v_ref, w_ref, b_ref, g_ref, oc_ref):
        gmat = _group_matrix(DC, 1.0 / GRP).astype(MXU)
        for ch in range(tb // CCH):
            rows = slice(ch * CCH, (ch + 1) * CCH)
            uu, _, _, _, _, zz, _, _, _ = _sgu_chunk(u_ref[rows, :], v_ref[rows, :], w_ref, b_ref[...], gmat)
            y = uu * zz
            ms = _mm_exact_r(y * y, gmat)
            oc_ref[rows, :] = (y * lax.rsqrt(ms + EPS) * g_ref[...]).astype(oc_ref.dtype)

    const = lambda i: (0, 0)
    return pl.pallas_call(
        body, name="sgu_fwd", grid=(t // tb,),
        in_specs=[pl.BlockSpec((tb, DC), lambda i: (i, ucol)),
                  pl.BlockSpec((tb, DC), lambda i: (i, vcol)),
                  pl.BlockSpec((DC // GRP, CCH, CCH), lambda i: (0, 0, 0)),
                  pl.BlockSpec((CCH, DC), const), pl.BlockSpec((1, DC), const)],
        out_specs=pl.BlockSpec((tb, DC), lambda i: (i, 0)),
        out_shape=jax.ShapeDtypeStruct((t, DC), SAVE),
        compiler_params=_params(("parallel",)),
    )(z, z, w, bias, gain)


def _sgu_bwd(z, doc, w, bias, gain, dz, tb):
    t = z.shape[0]
    nb = t // tb
    ucol, vcol = (4 * DA + 2 * DB) // DC, (4 * DA + 2 * DB) // DC + 1
    ng = DC // GRP

    def body(u_ref, v_ref, doc_ref, w_ref, b_ref, g_ref, _, dz_ref, dw_ref, dbias_ref, dgain_ref, dbsum_scr):
        i = pl.program_id(0)

        @pl.when(i == 0)
        def _():
            for ref in (dw_ref, dgain_ref, dbsum_scr):
                ref[...] = jnp.zeros_like(ref)

        gmat = _group_matrix(DC, 1.0 / GRP).astype(MXU)
        for ch in range(tb // CCH):
            rows = slice(ch * CCH, (ch + 1) * CCH)
            uu, duu, dvv, rstd_v, vn, zz, ws, col, causal = _sgu_chunk(
                u_ref[rows, :], v_ref[rows, :], w_ref, b_ref[...], gmat)
            y = uu * zz
            rstd = lax.rsqrt(_mm_exact_r(y * y, gmat) + EPS)
            n = y * rstd
            d_oc = doc_ref[rows, :]
            dn = d_oc * g_ref[...]
            dgain_ref[0:1, :] += jnp.sum(d_oc * n, axis=0, keepdims=True)
            dy = rstd * (dn - n * _mm_exact_r(dn * n, gmat))
            dzz = dy * uu
            dz_ref[rows, :DC] = (dy * zz * duu).astype(dz_ref.dtype)
            dbsum_scr[...] += dzz
            dvn = jnp.zeros_like(dzz)
            for g in range(ng):
                sel = col == g
                dvn = dvn + jnp.where(sel, _mm_tn(ws[g], dzz), 0.0)
                dw_ref[g] += jnp.where(causal, _mm_nt(jnp.where(sel, dzz, 0.0), vn), 0.0)
            dv = rstd_v * (dvn - _mm_exact_r(dvn, gmat) - vn * _mm_exact_r(dvn * vn, gmat))
            dz_ref[rows, DC:] = (dv * dvv).astype(dz_ref.dtype)

        @pl.when(i == nb - 1)
        def _():
            dbias_ref[...] = _mm_exact_r(dbsum_scr[...], _group_matrix(DC, 1.0).astype(MXU))

    const = lambda i: (0, 0)
    return pl.pallas_call(
        body, name="sgu_bwd", grid=(nb,),
        in_specs=[pl.BlockSpec((tb, DC), lambda i: (i, ucol)),
                  pl.BlockSpec((tb, DC), lambda i: (i, vcol)),
                  pl.BlockSpec((tb, DC), lambda i: (i, 0)),
                  pl.BlockSpec((ng, CCH, CCH), lambda i: (0, 0, 0)),
                  pl.BlockSpec((CCH, DC), const), pl.BlockSpec((1, DC), const),
                  pl.BlockSpec(memory_space=pl.ANY)],
        out_specs=[pl.BlockSpec((tb, 2 * DC), lambda i: (i, (4 * DA + 2 * DB) // (2 * DC))),
                   pl.BlockSpec((ng, CCH, CCH), lambda i: (0, 0, 0)),
                   pl.BlockSpec((CCH, DC), const), pl.BlockSpec((8, DC), const)],
        out_shape=[jax.ShapeDtypeStruct((t, DIN), SAVE), jax.ShapeDtypeStruct((ng, CCH, CCH), F32),
                   jax.ShapeDtypeStruct((CCH, DC), F32), jax.ShapeDtypeStruct((8, DC), F32)],
        scratch_shapes=[pltpu.VMEM((CCH, DC), F32)],
        input_output_aliases={6: 0},
        compiler_params=_params(("arbitrary",)),
    )(z, z, doc, w, bias, gain, dz)


def _head(h, gain, target, tm):
    t = h.shape[0]

    def body(h_ref, g_ref, t_ref, dh_ref, loss_ref, dgain_ref):
        @pl.when(pl.program_id(0) == 0)
        def _():
            loss_ref[...] = jnp.zeros_like(loss_ref)
            dgain_ref[...] = jnp.zeros_like(dgain_ref)

        hh = h_ref[...]
        gain = g_ref[...]
        rstd = lax.rsqrt(jnp.mean(hh * hh, axis=-1, keepdims=True) + EPS)
        xhat = hh * rstd
        err = xhat * gain - t_ref[...]
        per_tok = jnp.mean(err * err, axis=-1, keepdims=True)
        loss_ref[...] += 0.5 * jnp.sum(per_tok, axis=0, keepdims=True)
        dy = err * (1.0 / D)
        dgain_ref[...] += jnp.sum(dy * xhat, axis=0, keepdims=True)
        dxh = dy * gain
        dh_ref[...] = rstd * (dxh - xhat * jnp.mean(dxh * xhat, axis=-1, keepdims=True))

    return pl.pallas_call(
        body, name="head", grid=(t // tm,),
        in_specs=[pl.BlockSpec((tm, D), lambda i: (i, 0)),
                  pl.BlockSpec((1, D), lambda i: (0, 0)),
                  pl.BlockSpec((tm, D), lambda i: (i, 0))],
        out_specs=[pl.BlockSpec((tm, D), lambda i: (i, 0)),
                   pl.BlockSpec((1, 128), lambda i: (0, 0)),
                   pl.BlockSpec((1, D), lambda i: (0, 0))],
        out_shape=[jax.ShapeDtypeStruct((t, D), F32), jax.ShapeDtypeStruct((1, 128), F32),
                   jax.ShapeDtypeStruct((1, D), F32)],
        compiler_params=_params(("arbitrary",)),
    )(h, gain, target)


def _adamw(w, g, m, v):
    m = ADAM_B1 * m + (1.0 - ADAM_B1) * g
    v = ADAM_B2 * v + (1.0 - ADAM_B2) * (g * g)
    m_hat = m / (1.0 - ADAM_B1 ** ADAM_STEP)
    v_hat = v / (1.0 - ADAM_B2 ** ADAM_STEP)
    delta = -ADAM_LR * (m_hat / (jnp.sqrt(v_hat) + ADAM_EPS) + ADAM_WD * w)
    return delta, m, v


def _adamw_big(recv, w, m, v, tr, row0, name):
    depth, rows, cols = w.shape
    off = row0 // tr

    def body(*refs):
        r_refs = refs[:depth]
        w_ref, m_ref, v_ref, g_out, d_out, m_out, v_out = refs[depth:]
        for l in range(depth):
            g = r_refs[l][0].astype(F32)
            for k in range(1, NDEV):
                g = g + r_refs[l][k].astype(F32)
            delta, m_, v_ = _adamw(w_ref[l], g, m_ref[l], v_ref[l])
            g_out[l] = g
            d_out[l] = delta
            m_out[l] = m_
            v_out[l] = v_

    spec = pl.BlockSpec((depth, tr, cols), lambda i: (0, i, 0))
    return pl.pallas_call(
        body, name=name, grid=(rows // tr,),
        in_specs=[pl.BlockSpec((NDEV, tr, cols), lambda i: (0, i + off, 0))] * depth + [spec] * 3,
        out_specs=[spec] * 4, out_shape=[jax.ShapeDtypeStruct((depth, rows, cols), F32)] * 4,
        compiler_params=_params(("parallel",)),
    )(*recv, w, m, v)


def _sum_devices(recv):
    _, r, _ = recv.shape

    def body(r_ref, out_ref):
        g = r_ref[0]
        for k in range(1, NDEV):
            g = g + r_ref[k]
        out_ref[...] = g

    return pl.pallas_call(body, name="sum_devices", out_shape=jax.ShapeDtypeStruct((r, 128), F32))(recv)


def _adamw_small(w, g, m, v):
    def body(w_ref, g_ref, m_ref, v_ref, d_out, m_out, v_out):
        delta, m_, v_ = _adamw(w_ref[...], g_ref[...], m_ref[...], v_ref[...])
        d_out[...] = delta
        m_out[...] = m_
        v_out[...] = v_

    return pl.pallas_call(body, name="adamw_small", out_shape=[jax.ShapeDtypeStruct(w.shape, F32)] * 3)(w, g, m, v)


def _pack(arrs):
    flat = jnp.concatenate([a.reshape(-1) for a in arrs])
    pad = (-flat.shape[0]) % 1024
    return jnp.pad(flat, (0, pad)).reshape(-1, 128)


def _unpack(buf, like):
    flat = buf.reshape(-1)
    out, off = [], 0
    for a in like:
        out.append(flat[off:off + a.size].reshape(a.shape))
        off += a.size
    return out


def _block_diag(w):
    nb, bd, _ = w.shape
    eye = jnp.eye(nb, dtype=w.dtype)
    return (eye[:, None, :, None] * w[:, :, None, :]).reshape(nb * bd, nb * bd)


def _diag_blocks(w):
    nb = w.shape[0] // GRP
    return jnp.stack([w[g * GRP:(g + 1) * GRP, g * GRP:(g + 1) * GRP] for g in range(nb)])


def _pad_cols(a, n):
    return jnp.pad(a, ((0, 0), (0, n - a.shape[1])))


SMALL = ['ffn1_norm', 'mix_norm', 'hgrn_lb_logits', 'hgrn_norm', 'conv_b', 'lru_wa', 'lru_ba', 'lru_wx', 'lru_bx',
         'lru_lambda', 'lru_norm', 'sgu_w', 'sgu_b', 'sgu_norm', 'ffn2_norm', 'final_norm']
NAMES = ['ffn1_norm', 'ffn1_wg', 'ffn1_wu', 'ffn1_wd', 'mix_norm', 'w_in', 'hgrn_lb_logits', 'hgrn_norm', 'conv_w',
         'conv_b', 'lru_wa', 'lru_ba', 'lru_wx', 'lru_bx', 'lru_lambda', 'lru_norm', 'sgu_w', 'sgu_b', 'sgu_norm',
         'w_out', 'ffn2_norm', 'ffn2_wg', 'ffn2_wu', 'ffn2_wd', 'final_norm']


def _step(x, target, w, m, v):
    depth = w['ffn1_wg'].shape[0]
    t = x.shape[1]
    h = x.reshape(t, D)
    target = target.reshape(t, D)
    tm_f, tm_b, tb = min(TM_F, t), min(TM_B, t), min(TB, t)
    my = 4 * lax.axis_index("x") + 2 * lax.axis_index("y") + lax.axis_index("c")

    cw_tile = jnp.pad(w['conv_w'].reshape(-1, 128), ((0, 8 - depth), (0, 0)))
    cw_all = _all_gather([cw_tile], "gather_conv")[0][:, :depth]
    conv_w = jnp.moveaxis(cw_all.reshape(NDEV, depth, 4, DB // NDEV), 0, 2).reshape(depth, 4, DB)
    lbs, lb_soft = _lower_bounds(w['hgrn_lb_logits'])

    def row(a):
        return a.reshape(1, -1)

    def tr(a):
        return jnp.swapaxes(a, -1, -2)

    def pad_rows(a):
        return jnp.pad(a, ((0, FFP - FFS), (0, 0)))

    def shards(l, unit):
        if unit == 1:
            return [w['w_in'][l].astype(WIRE), w['w_out'][l].astype(WIRE)]
        f = 'ffn1' if unit == 0 else 'ffn2'
        wgu = jnp.concatenate([pad_rows(tr(w[f + '_wg'][l])), pad_rows(tr(w[f + '_wu'][l]))], axis=0)
        return [wgu.astype(WIRE), pad_rows(w[f + '_wd'][l]).astype(WIRE)]

    units = [(l, u) for l in range(depth) for u in range(3)]
    state = dict(pending=_transfer_start(shards(0, 0), True, "gather_start_0_0"), idx=0)

    def next_weights(after):
        l, u = units[state['idx']]
        lands = _transfer_wait(state['pending'], after, f"gather_wait_{l}_{u}")
        state['idx'] += 1
        tok = 0.0
        if state['idx'] < len(units):
            nl, nu = units[state['idx']]
            state['pending'] = _transfer_start(shards(nl, nu), True, f"gather_start_{nl}_{nu}", deps=(lands[-1],))
            tok = state['pending']['token'][0, 0]
        return lands, tok

    saved = []
    for l in range(depth):
        (wgu1, wd1), tok = next_weights(h)
        s = dict(wgu1=wgu1, wd1=wd1, h0=h)
        h, s['xn1'], s['ab1'] = _ffn_fwd(h, row(w['ffn1_norm'][l]) + tok, wgu1, wd1, tm_f)
        s['h1'] = h
        (win, wout), tok = next_weights(h)
        wout = wout.reshape(D, D)
        s['win'], s['wout'] = win, wout
        z, s['xnm'] = _inproj_fwd(h, row(w['mix_norm'][l]) + tok, win, tm_f)
        s['z'] = z
        s['o'], oa, s['states'] = _hgrn_fwd(z, row(lbs[l]), row(w['hgrn_norm'][l]))
        s['cw'] = jnp.pad(conv_w[l], ((0, 4), (0, 0)))
        s['vec'] = jnp.concatenate([row(w['conv_b'][l]), row(w['lru_ba'][l]), row(w['lru_bx'][l]),
                                    row(w['lru_lambda'][l]), row(w['lru_norm'][l]), jnp.zeros((3, DB), F32)])
        s['wa'], s['wx'] = _block_diag(w['lru_wa'][l]), _block_diag(w['lru_wx'][l])
        ob, s['hseq'] = _lru_fwd(z, s['cw'], s['vec'], s['wa'], s['wx'], tb)
        s['bias'] = jnp.repeat(w['sgu_b'][l].T, GRP, axis=1)
        oc = _sgu_fwd(z, w['sgu_w'][l], s['bias'], row(w['sgu_norm'][l]), tb)
        s['oa'], s['ob'], s['oc'] = oa, ob, oc
        h = _outproj_fwd(h, oa, ob, oc, wout, tm_f)
        s['h2'] = h
        (wgu2, wd2), tok = next_weights(h)
        s['wgu2'], s['wd2'] = wgu2, wd2
        h, s['xn2'], s['ab2'] = _ffn_fwd(h, row(w['ffn2_norm'][l]) + tok, wgu2, wd2, tm_f)
        saved.append(s)

    dh, loss_part, g_final = _head(h, row(w['final_norm']), target, tm_f)
    loss = lax.psum(loss_part[0, 0], ("x", "y", "c"))

    recv = {k: [None] * depth for k in ('wgu1', 'wd1', 'wgu2', 'wd2', 'win', 'wout')}
    flight = []

    def land(after):
        handle, kinds, l = flight.pop()
        for k, a in zip(kinds, _transfer_wait(handle, after, f"exchange_wait_{kinds[0]}_{l}")):
            recv[k][l] = a

    def exchange(arrs, kinds, l):
        handle = _transfer_start(arrs, False, f"exchange_start_{kinds[0]}_{l}")
        if flight:
            land(handle['token'])
        flight.append((handle, kinds, l))
        return handle['token'][0, 0]

    small = {k: [None] * depth for k in SMALL if k != 'final_norm'}
    dconv = [None] * depth
    dlb = [None] * depth
    tok = 0.0
    for l in reversed(range(depth)):
        s = saved[l]
        dh, dwgu, dwd, g = _ffn_bwd(dh, s['h2'], row(w['ffn2_norm'][l]) + tok, s['xn2'], s['ab2'],
                                    s['wgu2'], s['wd2'], tm_b)
        tok = exchange([dwgu, dwd], ('wgu2', 'wd2'), l)
        small['ffn2_norm'][l] = g
        doa, dob, doc, dwout = _outproj_bwd(dh, s['oa'], s['ob'], s['oc'], s['wout'], tm_f)
        dz, g_hn, dlb[l] = _hgrn_bwd(s['z'], row(lbs[l]), row(w['hgrn_norm'][l]) + tok, s['o'], s['states'], doa)
        small['hgrn_norm'][l] = g_hn
        dz, dcw, dvec, dwa, dwx = _lru_bwd(s['z'], s['hseq'], dob, s['cw'], s['vec'], s['wa'], s['wx'], dz, tb)
        dconv[l] = dcw[:4]
        small['conv_b'][l], small['lru_ba'][l], small['lru_bx'][l] = dvec[0], dvec[1].reshape(4, GRP), dvec[2].reshape(4, GRP)
        small['lru_lambda'][l], small['lru_norm'][l] = dvec[3], dvec[4]
        small['lru_wa'][l], small['lru_wx'][l] = _diag_blocks(dwa), _diag_blocks(dwx)
        dz, dsw, dbias, dgc = _sgu_bwd(s['z'], doc, w['sgu_w'][l], s['bias'], row(w['sgu_norm'][l]), dz, tb)
        small['sgu_w'][l], small['sgu_b'][l], small['sgu_norm'][l] = dsw, dbias[:, ::GRP].T, dgc[0]
        dwin = _inproj_bwd_w(s['xnm'], dz, tm_f)
        tok = exchange([dwin, dwout.reshape(NDEV, D // NDEV, D)], ('win', 'wout'), l)
        dh, g = _inproj_bwd_x(dh, dz, s['h1'], row(w['mix_norm'][l]) + tok, s['win'], tm_b)
        small['mix_norm'][l] = g
        dh, dwgu, dwd, g = _ffn_bwd(dh, s['h0'], row(w['ffn1_norm'][l]), s['xn1'], s['ab1'],
                                    s['wgu1'], s['wd1'], tm_b)
        tok = exchange([dwgu, dwd], ('wgu1', 'wd1'), l)
        small['ffn1_norm'][l] = g
    land(dh)
    grad_x = dh.reshape(1, t, D)
    small['hgrn_lb_logits'] = list(_lower_bounds_bwd(lb_soft, jnp.concatenate(dlb, axis=0)))

    out = {}
    for f, ku, kd in (('ffn1', 'wgu1', 'wd1'), ('ffn2', 'wgu2', 'wd2')):
        for k, row0 in ((f + '_wg', 0), (f + '_wu', FFP)):
            res = _adamw_big(recv[ku], tr(w[k]), tr(m[k]), tr(v[k]), 32, row0, "adamw_wgu")
            out[k] = tuple(tr(a) for a in res)
        k = f + '_wd'
        out[k] = _adamw_big(recv[kd], w[k], m[k], v[k], 32, 0, "adamw_wd")
    out['w_in'] = _adamw_big(recv['win'], w['w_in'], m['w_in'], v['w_in'], 64, 0, "adamw_win")
    out['w_out'] = _adamw_big(recv['wout'], w['w_out'], m['w_out'], v['w_out'], 64, 0, "adamw_wout")

    parts = [jnp.stack([small[k][l].reshape(w[k].shape[1:]) for l in range(depth)]) for k in SMALL if k != 'final_norm']
    parts += [g_final.reshape(D), jnp.stack(dconv)]
    total = _sum_devices(_all_gather([_pack(parts)], "gather_small")[0])
    like = [w[k] for k in SMALL] + [jax.ShapeDtypeStruct((depth, 4, DB), F32)]
    grads = _unpack(total, like)
    gsmall = dict(zip(SMALL, grads[:-1]))
    gsmall['conv_w'] = lax.dynamic_slice_in_dim(grads[-1], my * (DB // NDEV), DB // NDEV, axis=2)
    keys = SMALL + ['conv_w']
    dl, mm, vv = _adamw_small(_pack([w[k] for k in keys]), _pack([gsmall[k] for k in keys]),
                              _pack([m[k] for k in keys]), _pack([v[k] for k in keys]))
    like = [w[k] for k in keys]
    for k, d_, m_, v_ in zip(keys, _unpack(dl, like), _unpack(mm, like), _unpack(vv, like)):
        out[k] = (gsmall[k], d_, m_, v_)

    return (loss, grad_x, *[out[k][0] for k in NAMES], *[out[k][1] for k in NAMES],
            *[out[k][2] for k in NAMES], *[out[k][3] for k in NAMES])


def kernel(x, ffn1_norm, ffn1_wg, ffn1_wu, ffn1_wd, mix_norm, w_in, hgrn_lb_logits, hgrn_norm, conv_w, conv_b, lru_wa, lru_ba, lru_wx, lru_bx, lru_lambda, lru_norm, sgu_w, sgu_b, sgu_norm, w_out, ffn2_norm, ffn2_wg, ffn2_wu, ffn2_wd, final_norm, loss_target, m_ffn1_norm, m_ffn1_wg, m_ffn1_wu, m_ffn1_wd, m_mix_norm, m_w_in, m_hgrn_lb_logits, m_hgrn_norm, m_conv_w, m_conv_b, m_lru_wa, m_lru_ba, m_lru_wx, m_lru_bx, m_lru_lambda, m_lru_norm, m_sgu_w, m_sgu_b, m_sgu_norm, m_w_out, m_ffn2_norm, m_ffn2_wg, m_ffn2_wu, m_ffn2_wd, m_final_norm, v_ffn1_norm, v_ffn1_wg, v_ffn1_wu, v_ffn1_wd, v_mix_norm, v_w_in, v_hgrn_lb_logits, v_hgrn_norm, v_conv_w, v_conv_b, v_lru_wa, v_lru_ba, v_lru_wx, v_lru_bx, v_lru_lambda, v_lru_norm, v_sgu_w, v_sgu_b, v_sgu_norm, v_w_out, v_ffn2_norm, v_ffn2_wg, v_ffn2_wu, v_ffn2_wd, v_final_norm):
    args = locals()
    w = {k: args[k] for k in NAMES}
    m = {k: args['m_' + k] for k in NAMES}
    v = {k: args['v_' + k] for k in NAMES}
    return _step(x, loss_target, w, m, v)
```

```python
import functools

import jax
import jax.numpy as jnp
from jax import lax
from jax.experimental import pallas as pl
from jax.experimental.pallas import tpu as pltpu

F32 = jnp.float32
MXU = jnp.bfloat16
SAVE = jnp.bfloat16
WIRE = jnp.bfloat16

NDEV = 8
D = 1024
FF = 2816
FFS = FF // NDEV
FFP = 384
DIN = 3072
DINS = DIN // NDEV
DA, DB, DC = 512, 256, 256
HD = 128
NH = DA // HD
ACH = 64
ACB = 4
CCH = 128
GRP = 64
EPS = 1e-6
LRU_C = 8.0
VMEM_LIMIT = 60 * 1024 * 1024
TM_F = 1024
TM_B = 512
TB = 512
SUB = 256

ADAM_LR, ADAM_B1, ADAM_B2, ADAM_EPS, ADAM_WD, ADAM_STEP = 0.001, 0.9, 0.999, 1e-08, 0.01, 10

MESH = pl.DeviceIdType.MESH


def _mm(a, b):
    return jnp.dot(a.astype(MXU), b.astype(MXU), preferred_element_type=F32)


def _mm_nt(a, b):
    return lax.dot_general(a.astype(MXU), b.astype(MXU), (((1,), (1,)), ((), ())), preferred_element_type=F32)


def _mm_tn(a, b):
    return lax.dot_general(a.astype(MXU), b.astype(MXU), (((0,), (0,)), ((), ())), preferred_element_type=F32)


def _split3(x):
    x1 = x.astype(MXU)
    r1 = x - x1.astype(F32)
    x2 = r1.astype(MXU)
    r2 = r1 - x2.astype(F32)
    return x1, x2, r2.astype(MXU)


def _mm_exact_l(c, x):
    x1, x2, x3 = _split3(x)
    return _mm(c, x1) + _mm(c, x2) + _mm(c, x3)


def _mm_exact_r(x, c):
    x1, x2, x3 = _split3(x)
    return _mm(x1, c) + _mm(x2, c) + _mm(x3, c)


def _sigmoid(x):
    return 1.0 / (1.0 + jnp.exp(-x))


def _gelu(x):
    c, k = 0.7978845608028654, 0.044715
    th = jnp.tanh(c * (x + k * x * x * x))
    return 0.5 * x * (1.0 + th)


def _gelu_and_grad(x):
    c, k = 0.7978845608028654, 0.044715
    th = jnp.tanh(c * (x + k * x * x * x))
    g = 0.5 * x * (1.0 + th)
    dg = 0.5 * (1.0 + th) + 0.5 * x * (1.0 - th * th) * c * (1.0 + 3.0 * k * x * x)
    return g, dg


def _expm1(x):
    series = x * (1.0 + x * (0.5 + x * (1.0 / 6.0 + x * (1.0 / 24.0 + x * (1.0 / 120.0)))))
    return jnp.where(jnp.abs(x) < 0.05, series, jnp.exp(x) - 1.0)


def _iota(shape, dim):
    return lax.broadcasted_iota(jnp.int32, shape, dim)


def _tri(n, lower):
    r, c = _iota((n, n), 0), _iota((n, n), 1)
    return jnp.where((r >= c) if lower else (r <= c), 1.0, 0.0).astype(F32)


def _group_matrix(n, value):
    r, c = _iota((n, n), 0), _iota((n, n), 1)
    return jnp.where((r // GRP) == (c // GRP), value, 0.0).astype(F32)


def _row(x, k):
    r = _iota(x.shape, 0)
    return jnp.sum(jnp.where(r == k, x, 0.0), axis=0, keepdims=True)


def _rms_bwd(dxn, hh, gain):
    rstd = lax.rsqrt(jnp.mean(hh * hh, axis=-1, keepdims=True) + EPS)
    xhat = hh * rstd
    dxh = dxn * gain
    dh = rstd * (dxh - xhat * jnp.mean(dxh * xhat, axis=-1, keepdims=True))
    return dh, jnp.sum(dxn * xhat, axis=0, keepdims=True)


def _params(sem):
    return pltpu.CompilerParams(dimension_semantics=sem, vmem_limit_bytes=VMEM_LIMIT)


def _all_gather(arrs, name):
    n = len(arrs)

    def body(*refs):
        ins, outs = refs[:n], refs[n:2 * n]
        send_sems, recv_sems, local_sems = refs[2 * n:]
        x, y, c = lax.axis_index("x"), lax.axis_index("y"), lax.axis_index("c")
        me, sibling = (x, y, c), (x, y, 1 - c)
        chips = [(1 - x, y), (x, 1 - y), (1 - x, 1 - y)]

        def slot(px, py, pc):
            return 4 * px + 2 * py + pc

        def copy(a, k, block, to, src=None):
            dst = outs[a].at[slot(*block)]
            return pltpu.make_async_remote_copy(
                src_ref=dst if src is None else src, dst_ref=dst,
                send_sem=send_sems.at[a * 7 + k], recv_sem=recv_sems.at[a * 7 + k],
                device_id=to, device_id_type=MESH)

        started = []
        for a in range(n):
            mine = pltpu.make_async_copy(ins[a], outs[a].at[slot(*me)], local_sems.at[a])
            mine.start()
            started.append(mine)
        first = []
        for a in range(n):
            first.append(copy(a, 0, me, sibling, src=ins[a]))
            first += [copy(a, 1 + j, me, (*chip, c), src=ins[a]) for j, chip in enumerate(chips)]
        for cp in first:
            cp.start()
        passed = []
        for a in range(n):
            for j, chip in enumerate(chips):
                copy(a, 1 + j, (*chip, c), me).wait_recv()
                fwd = copy(a, 4 + j, (*chip, c), sibling)
                fwd.start()
                passed.append(fwd)
        for a in range(n):
            copy(a, 0, sibling, me).wait_recv()
            for j, chip in enumerate(chips):
                copy(a, 4 + j, (*chip, 1 - c), me).wait_recv()
        for cp in first + passed:
            cp.wait_send()
        for mine in started:
            mine.wait()

    hbm = pl.BlockSpec(memory_space=pl.ANY)
    return pl.pallas_call(
        body, name=name,
        out_shape=[jax.ShapeDtypeStruct((NDEV,) + a.shape, a.dtype) for a in arrs],
        in_specs=[hbm] * n, out_specs=[hbm] * n,
        scratch_shapes=[pltpu.SemaphoreType.DMA((7 * n,)), pltpu.SemaphoreType.DMA((7 * n,)),
                        pltpu.SemaphoreType.DMA((n,))],
    )(*arrs)


def _peers():
    x, y, c = lax.axis_index("x"), lax.axis_index("y"), lax.axis_index("c")
    peers = [(x ^ ((k >> 2) & 1), y ^ ((k >> 1) & 1), c ^ (k & 1)) for k in range(1, NDEV)]
    return (x, y, c), 4 * x + 2 * y + c, peers


_HBM = pl.BlockSpec(memory_space=pltpu.HBM)
_SEM = pl.BlockSpec(memory_space=pltpu.SEMAPHORE)
_EFFECT = pltpu.SideEffectType.DATAFLOW_SIDE_EFFECTING


def _transfer_start(arrs, gather, name, deps=()):
    n, nd = len(arrs), len(deps)
    shapes = [((NDEV,) + a.shape) if gather else a.shape for a in arrs]

    def body(*refs):
        ins, lands = refs[:n], refs[n:2 * n]
        send_sems, recv_sems, local_sems = refs[2 * n + nd:2 * n + nd + 3]
        token = refs[-1]
        (x, y, c), my, peers = _peers()
        if gather:
            peers = [(x, y, 1 - c), (1 - x, y, c), (x, 1 - y, c), (1 - x, 1 - y, c)]
        for a in range(n):
            own = ins[a] if gather else ins[a].at[my]
            pltpu.make_async_copy(own, lands[a].at[my], local_sems.at[a]).start()
        for a in range(n):
            for peer in peers:
                src = ins[a] if gather else ins[a].at[4 * peer[0] + 2 * peer[1] + peer[2]]
                pltpu.make_async_remote_copy(
                    src_ref=src, dst_ref=lands[a].at[my], send_sem=send_sems.at[a], recv_sem=recv_sems.at[a],
                    device_id=peer, device_id_type=MESH).start()
        token[...] = jnp.zeros_like(token)

    out_shape = [pltpu.SemaphoreType.DMA((n,))] * 3
    out_shape += [pltpu.HBM(a.shape, a.dtype) for a in arrs]
    out_shape += [pltpu.HBM(s, a.dtype) for s, a in zip(shapes, arrs)]
    out_shape += [jax.ShapeDtypeStruct((8, 128), F32)]
    operands = [pltpu.with_memory_space_constraint(a, pltpu.HBM) for a in arrs]
    operands += [pltpu.with_memory_space_constraint(lax.empty(s, a.dtype), pltpu.HBM) for s, a in zip(shapes, arrs)]
    res = pl.pallas_call(
        body, name=name, out_shape=out_shape,
        in_specs=[_HBM] * (2 * n) + [pl.BlockSpec(memory_space=pl.ANY)] * nd,
        out_specs=[_SEM] * 3 + [_HBM] * (2 * n) + [pl.BlockSpec(memory_space=pltpu.VMEM)],
        input_output_aliases={i: 3 + i for i in range(2 * n)},
        compiler_params=pltpu.CompilerParams(has_side_effects=_EFFECT),
    )(*operands, *deps)
    return dict(sems=res[:3], src=res[3:3 + n], lands=res[3 + n:3 + 2 * n], token=res[-1], n=n,
                count=4 if gather else NDEV - 1)


def _forward_start(lands, name, deps=()):
    n, nd = len(lands), len(deps)

    def body(*refs):
        zone = refs[:n]
        send_sems, recv_sems = refs[n + nd:n + nd + 2]
        token = refs[-1]
        (x, y, c), _, _ = _peers()
        for a in range(n):
            for px, py in ((1 - x, y), (x, 1 - y), (1 - x, 1 - y)):
                block = zone[a].at[4 * px + 2 * py + c]
                pltpu.make_async_remote_copy(
                    src_ref=block, dst_ref=block, send_sem=send_sems.at[a], recv_sem=recv_sems.at[a],
                    device_id=(x, y, 1 - c), device_id_type=MESH).start()
        token[...] = jnp.zeros_like(token)

    res = pl.pallas_call(
        body, name=name,
        out_shape=[pltpu.SemaphoreType.DMA((n,))] * 2 + [pltpu.HBM(a.shape, a.dtype) for a in lands]
        + [jax.ShapeDtypeStruct((8, 128), F32)],
        in_specs=[_HBM] * n + [pl.BlockSpec(memory_space=pl.ANY)] * nd,
        out_specs=[_SEM] * 2 + [_HBM] * n + [pl.BlockSpec(memory_space=pltpu.VMEM)],
        input_output_aliases={i: 2 + i for i in range(n)},
        compiler_params=pltpu.CompilerParams(has_side_effects=_EFFECT),
    )(*lands, *deps)
    return dict(sems=res[:2], src=[], lands=res[2:2 + n], token=res[-1], n=n, count=3)


def _transfer_wait(handle, after, name):
    n, count = handle["n"], handle["count"]
    src, lands, sems = list(handle["src"]), list(handle["lands"]), list(handle["sems"])
    ns = len(src)

    def body(*refs):
        zone = refs[ns:ns + n]
        sem_refs = refs[ns + n:ns + n + len(sems)]
        me, _, _ = _peers()
        for a in range(n):
            moved = zone[a].at[pl.ds(0, count)]
            both = pltpu.make_async_remote_copy(
                src_ref=moved, dst_ref=moved, send_sem=sem_refs[0].at[a], recv_sem=sem_refs[1].at[a],
                device_id=me, device_id_type=MESH)
            both.wait_send()
            both.wait_recv()
            if len(sems) == 3:
                pltpu.make_async_copy(zone[a].at[0], zone[a].at[1], sem_refs[2].at[a]).wait()

    res = pl.pallas_call(
        body, name=name,
        out_shape=[pltpu.HBM(a.shape, a.dtype) for a in src + lands],
        in_specs=[_HBM] * (ns + n) + [_SEM] * len(sems) + [pl.BlockSpec(memory_space=pl.ANY)],
        out_specs=[_HBM] * (ns + n),
        input_output_aliases={i: i for i in range(ns + n)},
        compiler_params=pltpu.CompilerParams(has_side_effects=_EFFECT),
    )(*src, *lands, *sems, after)
    return list(res[ns:])


def _ffn_fwd(h, gain, wgu, wd, tm):
    t = h.shape[0]

    def body(h_ref, g_ref, wgu_ref, wd_ref, out_ref, xn_ref, ab_ref, acc_ref):
        j = pl.program_id(1)

        @pl.when(j == 0)
        def _():
            hh = h_ref[...]
            rstd = lax.rsqrt(jnp.mean(hh * hh, axis=-1, keepdims=True) + EPS)
            xn_ref[...] = (hh * rstd * g_ref[...]).astype(xn_ref.dtype)
            acc_ref[...] = jnp.zeros_like(acc_ref)

        sub = min(SUB, tm)
        for r in range(tm // sub):
            rows = slice(r * sub, (r + 1) * sub)
            ab = _mm_nt(xn_ref[rows, :], wgu_ref[...])
            ab_ref[rows, :] = ab.astype(ab_ref.dtype)
            a, b = ab[:, :FFP], ab[:, FFP:]
            s = a * _sigmoid(a) * b
            acc_ref[rows, :] += _mm(s, wd_ref[...])

        @pl.when(j == NDEV - 1)
        def _():
            out_ref[...] = h_ref[...] + 0.5 * acc_ref[...]

    return pl.pallas_call(
        body, name="ffn_fwd", grid=(t // tm, NDEV),
        in_specs=[pl.BlockSpec((tm, D), lambda i, j: (i, 0)),
                  pl.BlockSpec((1, D), lambda i, j: (0, 0)),
                  pl.BlockSpec((None, 2 * FFP, D), lambda i, j: (j, 0, 0)),
                  pl.BlockSpec((None, FFP, D), lambda i, j: (j, 0, 0))],
        out_specs=[pl.BlockSpec((tm, D), lambda i, j: (i, 0)),
                   pl.BlockSpec((tm, D), lambda i, j: (i, 0)),
                   pl.BlockSpec((tm, 2 * FFP), lambda i, j: (i, j))],
        out_shape=[jax.ShapeDtypeStruct((t, D), F32), jax.ShapeDtypeStruct((t, D), SAVE),
                   jax.ShapeDtypeStruct((t, NDEV * 2 * FFP), SAVE)],
        scratch_shapes=[pltpu.VMEM((tm, D), F32)],
        compiler_params=_params(("parallel", "arbitrary")),
    )(h, gain, wgu, wd)


def _ffn_bwd(dout, h, gain, xn, ab, wgu, wd, tm):
    t = h.shape[0]
    nt = t // tm
    last = NDEV - 1

    def body(dout_ref, h_ref, g_ref, xn_ref, ab_ref, wgu_ref, wd_ref,
             dh_ref, dwgu_ref, dwd_ref, dgain_ref, dxn_scr, agu_scr, awd_scr):
        j, i = pl.program_id(0), pl.program_id(1)

        @pl.when(i == 0)
        def _():
            agu_scr[...] = jnp.zeros_like(agu_scr)
            awd_scr[...] = jnp.zeros_like(awd_scr)

        @pl.when((i == 0) & (j == 0))
        def _():
            dgain_ref[...] = jnp.zeros_like(dgain_ref)

        sub = min(SUB, tm)
        dys, dabs, ss, parts = [], [], [], []
        for r in range(tm // sub):
            rws = slice(r * sub, (r + 1) * sub)
            dy = (0.5 * dout_ref[rws, :]).astype(MXU)
            ds = _mm_nt(dy, wd_ref[...])
            ab_ = ab_ref[rws, :].astype(F32)
            a, b = ab_[:, :FFP], ab_[:, FFP:]
            sg = _sigmoid(a)
            sa = a * sg
            db = ds * sa
            da = ds * b * (sg * (1.0 + a * (1.0 - sg)))
            dab = jnp.concatenate([da, db], axis=1).astype(MXU)
            dys.append(dy)
            dabs.append(dab)
            ss.append((sa * b).astype(MXU))
            parts.append(_mm(dab, wgu_ref[...]))
        dy, dab, s, part = [jnp.concatenate(v, axis=0) for v in (dys, dabs, ss, parts)]
        agu_scr[...] += _mm_tn(dab, xn_ref[...])
        awd_scr[...] += _mm_tn(s, dy)
        rows = pl.ds(pl.multiple_of(i * tm, tm), tm)

        @pl.when(j == 0)
        def _():
            dxn_scr[rows, :] = part

        @pl.when((j > 0) & (j < last))
        def _():
            dxn_scr[rows, :] += part

        @pl.when(j == last)
        def _():
            dxn = dxn_scr[rows, :] + part
            dh, dg = _rms_bwd(dxn, h_ref[...], g_ref[...])
            dh_ref[...] = dout_ref[...] + dh
            dgain_ref[...] += dg

        @pl.when(i == nt - 1)
        def _():
            dwgu_ref[...] = agu_scr[...].astype(dwgu_ref.dtype)
            dwd_ref[...] = awd_scr[...].astype(dwd_ref.dtype)

    def tail(j, i):
        return (jnp.where(j == last, i, 0), 0)

    return pl.pallas_call(
        body, name="ffn_bwd", grid=(NDEV, nt),
        in_specs=[pl.BlockSpec((tm, D), lambda j, i: (i, 0)),
                  pl.BlockSpec((tm, D), tail),
                  pl.BlockSpec((1, D), lambda j, i: (0, 0)),
                  pl.BlockSpec((tm, D), lambda j, i: (i, 0)),
                  pl.BlockSpec((tm, 2 * FFP), lambda j, i: (i, j)),
                  pl.BlockSpec((None, 2 * FFP, D), lambda j, i: (j, 0, 0)),
                  pl.BlockSpec((None, FFP, D), lambda j, i: (j, 0, 0))],
        out_specs=[pl.BlockSpec((tm, D), tail),
                   pl.BlockSpec((None, 2 * FFP, D), lambda j, i: (j, 0, 0)),
                   pl.BlockSpec((None, FFP, D), lambda j, i: (j, 0, 0)),
                   pl.BlockSpec((1, D), lambda j, i: (0, 0))],
        out_shape=[jax.ShapeDtypeStruct((t, D), F32),
                   jax.ShapeDtypeStruct((NDEV, 2 * FFP, D), WIRE),
                   jax.ShapeDtypeStruct((NDEV, FFP, D), WIRE),
                   jax.ShapeDtypeStruct((1, D), F32)],
        scratch_shapes=[pltpu.VMEM((t, D), F32), pltpu.VMEM((2 * FFP, D), F32), pltpu.VMEM((FFP, D), F32)],
        compiler_params=_params(("arbitrary", "arbitrary")),
    )(dout, h, gain, xn, ab, wgu, wd)


def _inproj_fwd(h, gain, win, tm):
    t = h.shape[0]

    def body(h_ref, g_ref, w_ref, z_ref, xn_ref):
        hh = h_ref[...]
        rstd = lax.rsqrt(jnp.mean(hh * hh, axis=-1, keepdims=True) + EPS)
        xn = (hh * rstd * g_ref[...]).astype(MXU)
        xn_ref[...] = xn.astype(xn_ref.dtype)
        for j in range(NDEV):
            z_ref[:, j * DINS:(j + 1) * DINS] = _mm(xn, w_ref[j])

    return pl.pallas_call(
        body, name="inproj_fwd", grid=(t // tm,),
        in_specs=[pl.BlockSpec((tm, D), lambda i: (i, 0)),
                  pl.BlockSpec((1, D), lambda i: (0, 0)),
                  pl.BlockSpec((NDEV, D, DINS), lambda i: (0, 0, 0))],
        out_specs=[pl.BlockSpec((tm, DIN), lambda i: (i, 0)),
                   pl.BlockSpec((tm, D), lambda i: (i, 0))],
        out_shape=[jax.ShapeDtypeStruct((t, DIN), F32), jax.ShapeDtypeStruct((t, D), SAVE)],
        compiler_params=_params(("parallel",)),
    )(h, gain, win)


def _inproj_bwd_x(dres, dz, h, gain, win, tm):
    t = h.shape[0]

    def body(dres_ref, dz_ref, h_ref, g_ref, w_ref, dh_ref, dgain_ref):
        @pl.when(pl.program_id(0) == 0)
        def _():
            dgain_ref[...] = jnp.zeros_like(dgain_ref)

        dxn = _mm_nt(dz_ref[:, :DINS], w_ref[0])
        for j in range(1, NDEV):
            dxn = dxn + _mm_nt(dz_ref[:, j * DINS:(j + 1) * DINS], w_ref[j])
        dh, dg = _rms_bwd(dxn, h_ref[...], g_ref[...])
        dh_ref[...] = dres_ref[...] + dh
        dgain_ref[...] += dg

    return pl.pallas_call(
        body, name="inproj_bwd_x", grid=(t // tm,),
        in_specs=[pl.BlockSpec((tm, D), lambda i: (i, 0)),
                  pl.BlockSpec((tm, DIN), lambda i: (i, 0)),
                  pl.BlockSpec((tm, D), lambda i: (i, 0)),
                  pl.BlockSpec((1, D), lambda i: (0, 0)),
                  pl.BlockSpec((NDEV, D, DINS), lambda i: (0, 0, 0))],
        out_specs=[pl.BlockSpec((tm, D), lambda i: (i, 0)),
                   pl.BlockSpec((1, D), lambda i: (0, 0))],
        out_shape=[jax.ShapeDtypeStruct((t, D), F32), jax.ShapeDtypeStruct((1, D), F32)],
        compiler_params=_params(("arbitrary",)),
    )(dres, dz, h, gain, win)


def _inproj_bwd_w(xn, dz, tm):
    t = xn.shape[0]
    nt = t // tm

    def body(xn_ref, dz_ref, dw_ref, acc_scr):
        i = pl.program_id(1)

        @pl.when(i == 0)
        def _():
            acc_scr[...] = jnp.zeros_like(acc_scr)

        acc_scr[...] += _mm_tn(xn_ref[...], dz_ref[...])

        @pl.when(i == nt - 1)
        def _():
            dw_ref[...] = acc_scr[...].astype(dw_ref.dtype)

    return pl.pallas_call(
        body, name="inproj_bwd_w", grid=(NDEV, nt),
        in_specs=[pl.BlockSpec((tm, D), lambda j, i: (i, 0)),
                  pl.BlockSpec((tm, DINS), lambda j, i: (i, j))],
        out_specs=pl.BlockSpec((None, D, DINS), lambda j, i: (j, 0, 0)),
        out_shape=jax.ShapeDtypeStruct((NDEV, D, DINS), WIRE),
        scratch_shapes=[pltpu.VMEM((D, DINS), F32)],
        compiler_params=_params(("parallel", "arbitrary")),
    )(xn, dz)


def _outproj_fwd(h, oa, ob, oc, wout, tm):
    t = h.shape[0]

    def body(h_ref, oa_ref, ob_ref, oc_ref, w_ref, out_ref):
        ym = jnp.concatenate([oa_ref[...], ob_ref[...], oc_ref[...]], axis=1)
        out_ref[...] = h_ref[...] + _mm(ym, w_ref[...])

    return pl.pallas_call(
        body, name="outproj_fwd", grid=(t // tm,),
        in_specs=[pl.BlockSpec((tm, D), lambda i: (i, 0)),
                  pl.BlockSpec((tm, DA), lambda i: (i, 0)),
                  pl.BlockSpec((tm, DB), lambda i: (i, 0)),
                  pl.BlockSpec((tm, DC), lambda i: (i, 0)),
                  pl.BlockSpec((D, D), lambda i: (0, 0))],
        out_specs=pl.BlockSpec((tm, D), lambda i: (i, 0)),
        out_shape=jax.ShapeDtypeStruct((t, D), F32),
        compiler_params=_params(("parallel",)),
    )(h, oa, ob, oc, wout)


def _outproj_bwd(dh, oa, ob, oc, wout, tm):
    t = dh.shape[0]
    nt = t // tm

    def body(dh_ref, oa_ref, ob_ref, oc_ref, w_ref, da_ref, db_ref, dc_ref, dw_ref, acc_scr):
        i = pl.program_id(0)

        @pl.when(i == 0)
        def _():
            acc_scr[...] = jnp.zeros_like(acc_scr)

        d16 = dh_ref[...].astype(MXU)
        dym = _mm_nt(d16, w_ref[...])
        da_ref[...] = dym[:, :DA]
        db_ref[...] = dym[:, DA:DA + DB]
        dc_ref[...] = dym[:, DA + DB:]
        ym = jnp.concatenate([oa_ref[...], ob_ref[...], oc_ref[...]], axis=1)
        acc_scr[...] += _mm_tn(ym, d16)

        @pl.when(i == nt - 1)
        def _():
            dw_ref[...] = acc_scr[...].astype(dw_ref.dtype)

    return pl.pallas_call(
        body, name="outproj_bwd", grid=(nt,),
        in_specs=[pl.BlockSpec((tm, D), lambda i: (i, 0)),
                  pl.BlockSpec((tm, DA), lambda i: (i, 0)),
                  pl.BlockSpec((tm, DB), lambda i: (i, 0)),
                  pl.BlockSpec((tm, DC), lambda i: (i, 0)),
                  pl.BlockSpec((D, D), lambda i: (0, 0))],
        out_specs=[pl.BlockSpec((tm, DA), lambda i: (i, 0)),
                   pl.BlockSpec((tm, DB), lambda i: (i, 0)),
                   pl.BlockSpec((tm, DC), lambda i: (i, 0)),
                   pl.BlockSpec((D, D), lambda i: (0, 0))],
        out_shape=[jax.ShapeDtypeStruct((t, DA), F32), jax.ShapeDtypeStruct((t, DB), F32),
                   jax.ShapeDtypeStruct((t, DC), F32), jax.ShapeDtypeStruct((D, D), WIRE)],
        scratch_shapes=[pltpu.VMEM((D, D), F32)],
        compiler_params=_params(("arbitrary",)),
    )(dh, oa, ob, oc, wout)


def _lower_bounds(logits):
    depth, n = logits.shape

    def body(l_ref, lb_ref, p_ref):
        rows = [l_ref[l:l + 1, :] for l in range(depth)]
        mx = functools.reduce(jnp.maximum, rows)
        ex = [jnp.exp(r - mx) for r in rows]
        den = functools.reduce(lambda u, v: u + v, ex)
        acc = jnp.zeros_like(den)
        for l in range(depth):
            p = ex[l] / den
            p_ref[l:l + 1, :] = p
            if l > 0:
                acc = acc + p
            lb_ref[l:l + 1, :] = acc

    return pl.pallas_call(
        body, name="lower_bounds",
        out_shape=[jax.ShapeDtypeStruct((depth, n), F32), jax.ShapeDtypeStruct((depth, n), F32)],
    )(logits)


def _lower_bounds_bwd(p, dlb):
    depth, n = p.shape

    def body(p_ref, d_ref, out_ref):
        ps = [p_ref[l:l + 1, :] for l in range(depth)]
        ds = [d_ref[l:l + 1, :] for l in range(depth)]
        dp = [jnp.zeros_like(ps[0]) for _ in range(depth)]
        run = jnp.zeros_like(ps[0])
        for l in range(depth - 1, 0, -1):
            run = run + ds[l]
            dp[l] = run
        dot = functools.reduce(lambda u, v: u + v, [ps[l] * dp[l] for l in range(depth)])
        for l in range(depth):
            out_ref[l:l + 1, :] = ps[l] * (dp[l] - dot)

    return pl.pallas_call(body, name="lower_bounds_bwd", out_shape=jax.ShapeDtypeStruct((depth, n), F32))(p, dlb)


def _hgrn_chunk(z_ref, lb_ref, hd, rows):
    c0 = hd * HD
    q = z_ref[rows, c0:c0 + HD]
    fl = z_ref[rows, DA + c0:DA + c0 + HD]
    v = z_ref[rows, 2 * DA + c0:2 * DA + c0 + HD]
    g = z_ref[rows, 3 * DA + c0:3 * DA + c0 + HD]
    lb = lb_ref[:, c0:c0 + HD]
    sq = _sigmoid(q)
    qs = q * sq
    sg = _sigmoid(fl)
    f = lb + (1.0 - lb) * sg
    k = 1.0 - f
    lf = jnp.log(f)
    b = _mm_exact_l(_tri(ACH, True).astype(MXU), lf)
    bend = jnp.sum(lf, axis=0, keepdims=True)
    r = 0.5 * bend
    eq, ek, eb, ed = jnp.exp(b - r), jnp.exp(r - b), jnp.exp(b), jnp.exp(bend - b)
    qt, kt, qe, kd = qs * eq, k * ek, qs * eb, k * ed
    causal = _iota((ACH, ACH), 0) >= _iota((ACH, ACH), 1)
    att = jnp.where(causal, _mm_nt(qt, kt), 0.0)
    return dict(q=q, v=v, g=g, lb=lb, sq=sq, qs=qs, sg=sg, f=f, k=k, bend=bend, eq=eq, ek=ek, eb=eb, ed=ed,
                qt=qt, kt=kt, qe=qe, kd=kd, att=att, causal=causal)


def _hgrn_fwd(z, lb, gain):
    t = z.shape[0]
    nc = t // ACH
    cb = min(ACB, nc)
    rb = cb * ACH

    def body(z_ref, lb_ref, g_ref, o_ref, oa_ref, st_ref, st_scr):
        @pl.when(pl.program_id(0) == 0)
        def _():
            st_scr[...] = jnp.zeros_like(st_scr)

        for hd in range(NH):
            st = st_scr[hd]
            cols = slice(hd * HD, (hd + 1) * HD)
            for cc in range(cb):
                rows = slice(cc * ACH, (cc + 1) * ACH)
                c = _hgrn_chunk(z_ref, lb_ref, hd, rows)
                st_ref[cc, hd] = st
                o = _mm(c["att"], c["v"]) + _mm_nt(c["qe"], st)
                st = st * jnp.exp(c["bend"]) + _mm_tn(c["v"], c["kd"])
                o_ref[rows, cols] = o
                rstd = lax.rsqrt(jnp.mean(o * o, axis=-1, keepdims=True) + EPS)
                gg = c["g"]
                oa_ref[rows, cols] = (o * rstd * g_ref[:, cols] * (gg * _sigmoid(gg))).astype(oa_ref.dtype)
            st_scr[hd] = st

    return pl.pallas_call(
        body, name="hgrn_fwd", grid=(nc // cb,),
        in_specs=[pl.BlockSpec((rb, 4 * DA), lambda c: (c, 0)),
                  pl.BlockSpec((1, DA), lambda c: (0, 0)),
                  pl.BlockSpec((1, DA), lambda c: (0, 0))],
        out_specs=[pl.BlockSpec((rb, DA), lambda c: (c, 0)),
                   pl.BlockSpec((rb, DA), lambda c: (c, 0)),
                   pl.BlockSpec((cb, NH, HD, HD), lambda c: (c, 0, 0, 0))],
        out_shape=[jax.ShapeDtypeStruct((t, DA), F32), jax.ShapeDtypeStruct((t, DA), SAVE),
                   jax.ShapeDtypeStruct((nc, NH, HD, HD), F32)],
        scratch_shapes=[pltpu.VMEM((NH, HD, HD), F32)],
        compiler_params=_params(("arbitrary",)),
    )(z, lb, gain)


def _hgrn_bwd(z, lb, gain, o, states, doa):
    t = z.shape[0]
    nc = t // ACH
    cb = min(ACB, nc)
    rb = cb * ACH
    nblk = nc // cb

    def body(z_ref, lb_ref, g_ref, o_ref, st_ref, doa_ref, dz_ref, dgain_ref, dlb_ref, dst_scr):
        @pl.when(pl.program_id(0) == 0)
        def _():
            dst_scr[...] = jnp.zeros_like(dst_scr)
            dgain_ref[...] = jnp.zeros_like(dgain_ref)
            dlb_ref[...] = jnp.zeros_like(dlb_ref)

        upper = _tri(ACH, False).astype(MXU)
        for hd in range(NH):
            cols = slice(hd * HD, (hd + 1) * HD)
            gain = g_ref[:, cols]
            dsp = dst_scr[hd]
            dgain = jnp.zeros((1, HD), F32)
            dlb = jnp.zeros((1, HD), F32)
            for cc in reversed(range(cb)):
                rows = slice(cc * ACH, (cc + 1) * ACH)
                c = _hgrn_chunk(z_ref, lb_ref, hd, rows)
                o = o_ref[rows, cols]
                do_a = doa_ref[rows, cols]
                gg = c["g"]
                sgg = _sigmoid(gg)
                silu_g = gg * sgg
                rstd = lax.rsqrt(jnp.mean(o * o, axis=-1, keepdims=True) + EPS)
                n = o * rstd
                dn = do_a * gain * silu_g
                dg = do_a * n * gain * (sgg * (1.0 + gg * (1.0 - sgg)))
                dgain = dgain + jnp.sum(do_a * silu_g * n, axis=0, keepdims=True)
                d_o = rstd * (dn - n * jnp.mean(dn * n, axis=-1, keepdims=True))

                st = st_ref[cc, hd]
                datt = jnp.where(c["causal"], _mm_nt(d_o, c["v"]), 0.0)
                dv = _mm_tn(c["att"], d_o) + _mm_nt(c["kd"], dsp)
                dqt = _mm(datt, c["kt"])
                dqe = _mm(d_o, st)
                dkt = _mm_tn(datt, c["qt"])
                dkd = _mm(c["v"], dsp)
                decay = jnp.exp(c["bend"])
                dbend = (decay * jnp.sum(st * dsp, axis=0, keepdims=True)
                         + jnp.sum(dkd * c["kd"], axis=0, keepdims=True))
                dsp = dsp * decay + _mm_tn(d_o, c["qe"])
                db = dqt * c["qt"] + dqe * c["qe"] - dkt * c["kt"] - dkd * c["kd"]
                dqs = dqt * c["eq"] + dqe * c["eb"]
                dk = dkt * c["ek"] + dkd * c["ed"]
                dlf = _mm_exact_l(upper, db) + dbend
                df = dlf / c["f"] - dk
                sg = c["sg"]
                dlb = dlb + jnp.sum(df * (1.0 - sg), axis=0, keepdims=True)
                dfl = df * (1.0 - c["lb"]) * sg * (1.0 - sg)
                sq, q = c["sq"], c["q"]
                dq = dqs * (sq * (1.0 + q * (1.0 - sq)))
                c0 = hd * HD
                dz_ref[rows, c0:c0 + HD] = dq.astype(dz_ref.dtype)
                dz_ref[rows, DA + c0:DA + c0 + HD] = dfl.astype(dz_ref.dtype)
                dz_ref[rows, 2 * DA + c0:2 * DA + c0 + HD] = dv.astype(dz_ref.dtype)
                dz_ref[rows, 3 * DA + c0:3 * DA + c0 + HD] = dg.astype(dz_ref.dtype)
            dst_scr[hd] = dsp
            dgain_ref[:, cols] += dgain
            dlb_ref[:, cols] += dlb

    rev = lambda c: (nblk - 1 - c, 0)
    return pl.pallas_call(
        body, name="hgrn_bwd", grid=(nblk,),
        in_specs=[pl.BlockSpec((rb, 4 * DA), rev),
                  pl.BlockSpec((1, DA), lambda c: (0, 0)),
                  pl.BlockSpec((1, DA), lambda c: (0, 0)),
                  pl.BlockSpec((rb, DA), rev),
                  pl.BlockSpec((cb, NH, HD, HD), lambda c: (nblk - 1 - c, 0, 0, 0)),
                  pl.BlockSpec((rb, DA), rev)],
        out_specs=[pl.BlockSpec((rb, 4 * DA), rev),
                   pl.BlockSpec((1, DA), lambda c: (0, 0)),
                   pl.BlockSpec((1, DA), lambda c: (0, 0))],
        out_shape=[jax.ShapeDtypeStruct((t, DIN), SAVE), jax.ShapeDtypeStruct((1, DA), F32),
                   jax.ShapeDtypeStruct((1, DA), F32)],
        scratch_shapes=[pltpu.VMEM((NH, HD, HD), F32)],
        compiler_params=_params(("arbitrary",)),
    )(z, lb, gain, o, states, doa)


def _shift_down(prev8, x, k):
    cat = jnp.concatenate([prev8, x], axis=0)
    return pltpu.roll(cat, k, axis=0)[8:, :]


def _shift_up(x, next8, k):
    n = x.shape[0]
    cat = jnp.concatenate([x, next8], axis=0)
    return pltpu.roll(cat, n + 8 - k, axis=0)[:n, :]


def _lru_gates(x, prev8, cw_ref, vec_ref, wa_ref, wx_ref):
    xs = [x, _shift_down(prev8, x, 1), _shift_down(prev8, x, 2), _shift_down(prev8, x, 3)]
    xc = vec_ref[0:1, :] + cw_ref[3:4, :] * xs[0] + cw_ref[2:3, :] * xs[1] + cw_ref[1:2, :] * xs[2] + cw_ref[0:1, :] * xs[3]
    r = _sigmoid(_mm(xc, wa_ref[...]) + vec_ref[1:2, :])
    gi = _sigmoid(_mm(xc, wx_ref[...]) + vec_ref[2:3, :])
    lam = vec_ref[3:4, :]
    sp = jnp.maximum(-lam, 0.0) + jnp.log(1.0 + jnp.exp(-jnp.abs(lam)))
    la = -LRU_C * r * sp
    a = jnp.exp(la)
    mult = jnp.sqrt(-_expm1(2.0 * la))
    return xs, xc, r, gi, sp, a, mult


def _scan_down(a, u):
    n = a.shape[0]
    row = _iota(a.shape, 0)
    s = 1
    while s < n:
        keep = row >= s
        ash = jnp.where(keep, pltpu.roll(a, s, axis=0), 1.0)
        ush = jnp.where(keep, pltpu.roll(u, s, axis=0), 0.0)
        u = a * ush + u
        a = a * ash
        s *= 2
    return a, u


def _scan_up(a, u):
    n = a.shape[0]
    row = _iota(a.shape, 0)
    s = 1
    while s < n:
        keep = row < n - s
        ash = jnp.where(keep, pltpu.roll(a, n - s, axis=0), 1.0)
        ush = jnp.where(keep, pltpu.roll(u, n - s, axis=0), 0.0)
        u = a * ush + u
        a = a * ash
        s *= 2
    return a, u


def _lru_fwd(z, cw, vec, wa, wx, tb):
    t = z.shape[0]
    xcol, gcol = (4 * DA) // DB, (4 * DA) // DB + 1

    def body(x_ref, gate_ref, cw_ref, vec_ref, wa_ref, wx_ref, ob_ref, h_ref, xprev_scr, hc_scr):
        @pl.when(pl.program_id(0) == 0)
        def _():
            xprev_scr[...] = jnp.zeros_like(xprev_scr)
            hc_scr[...] = jnp.zeros_like(hc_scr)

        x = x_ref[...]
        _, xc, _, gi, _, a, mult = _lru_gates(x, xprev_scr[...], cw_ref, vec_ref, wa_ref, wx_ref)
        acum, hloc = _scan_down(a, mult * gi * xc)
        h = hloc + acum * hc_scr[0:1, :]
        h_ref[...] = h
        hc_scr[...] = jnp.broadcast_to(_row(h, tb - 1), hc_scr.shape)
        xprev_scr[...] = x[tb - 8:, :]
        y = h * _gelu(gate_ref[...])
        ms = _mm_exact_r(y * y, _group_matrix(DB, 1.0 / GRP).astype(MXU))
        ob_ref[...] = (y * lax.rsqrt(ms + EPS) * vec_ref[4:5, :]).astype(ob_ref.dtype)

    return pl.pallas_call(
        body, name="lru_fwd", grid=(t // tb,),
        in_specs=[pl.BlockSpec((tb, DB), lambda i: (i, xcol)),
                  pl.BlockSpec((tb, DB), lambda i: (i, gcol)),
                  pl.BlockSpec((8, DB), lambda i: (0, 0)),
                  pl.BlockSpec((8, DB), lambda i: (0, 0)),
                  pl.BlockSpec((DB, DB), lambda i: (0, 0)),
                  pl.BlockSpec((DB, DB), lambda i: (0, 0))],
        out_specs=[pl.BlockSpec((tb, DB), lambda i: (i, 0)),
                   pl.BlockSpec((tb, DB), lambda i: (i, 0))],
        out_shape=[jax.ShapeDtypeStruct((t, DB), SAVE), jax.ShapeDtypeStruct((t, DB), F32)],
        scratch_shapes=[pltpu.VMEM((8, DB), F32), pltpu.VMEM((8, DB), F32)],
        compiler_params=_params(("arbitrary",)),
    )(z, z, cw, vec, wa, wx)


def _lru_bwd(z, hseq, dob, cw, vec, wa, wx, dz, tb):
    t = z.shape[0]
    nb = t // tb
    xcol, gcol = (4 * DA) // DB, (4 * DA) // DB + 1
    per = tb // 8

    def body(x_ref, xh_ref, gate_ref, h_ref, hh_ref, dob_ref, cw_ref, vec_ref, wa_ref, wx_ref, _,
             dz_ref, dcw_ref, dvec_ref, dwa_ref, dwx_ref, gc_scr, an_scr, dxc_scr):
        step = pl.program_id(0)
        blk = nb - 1 - step

        @pl.when(step == 0)
        def _():
            for ref in (gc_scr, an_scr, dxc_scr, dcw_ref, dvec_ref, dwa_ref, dwx_ref):
                ref[...] = jnp.zeros_like(ref)

        first = (blk > 0).astype(F32)
        x = x_ref[...]
        xs, xc, r, gi, sp, a, mult = _lru_gates(x, xh_ref[...] * first, cw_ref, vec_ref, wa_ref, wx_ref)
        h = h_ref[...]
        hprev = _shift_down(hh_ref[...] * first, h, 1)
        ge, dge = _gelu_and_grad(gate_ref[...])
        y = h * ge
        gmat = _group_matrix(DB, 1.0 / GRP).astype(MXU)
        rstd = lax.rsqrt(_mm_exact_r(y * y, gmat) + EPS)
        n = y * rstd
        d_ob = dob_ref[...]
        dn = d_ob * vec_ref[4:5, :]
        dvec_ref[4:5, :] += jnp.sum(d_ob * n, axis=0, keepdims=True)
        dy = rstd * (dn - n * _mm_exact_r(dn * n, gmat))
        dh = dy * ge
        dgate = dy * h * dge

        row = _iota(a.shape, 0)
        anext = jnp.where(row == tb - 1, an_scr[0:1, :], pltpu.roll(a, tb - 1, axis=0))
        acum, gloc = _scan_up(anext, dh)
        g = gloc + acum * gc_scr[0:1, :]
        gc_scr[...] = jnp.broadcast_to(_row(g, 0), gc_scr.shape)
        an_scr[...] = jnp.broadcast_to(_row(a, 0), an_scr.shape)

        da = g * hprev
        dmult = g * gi * xc
        dgi = g * mult * xc
        dxc = g * mult * gi
        dla = da * a - dmult * (a * a) / mult
        dr = dla * (-LRU_C * sp)
        dsp = jnp.sum(dla * (-LRU_C * r), axis=0, keepdims=True)
        lam = vec_ref[3:4, :]
        dvec_ref[3:4, :] += -dsp * _sigmoid(-lam)
        dpa = dr * r * (1.0 - r)
        dpx = dgi * gi * (1.0 - gi)
        dwa_ref[...] += _mm_tn(xc, dpa)
        dwx_ref[...] += _mm_tn(xc, dpx)
        dvec_ref[1:2, :] += jnp.sum(dpa, axis=0, keepdims=True)
        dvec_ref[2:3, :] += jnp.sum(dpx, axis=0, keepdims=True)
        dxc = dxc + _mm_nt(dpa, wa_ref[...]) + _mm_nt(dpx, wx_ref[...])
        dvec_ref[0:1, :] += jnp.sum(dxc, axis=0, keepdims=True)
        for tap in range(4):
            dcw_ref[tap:tap + 1, :] += jnp.sum(dxc * xs[3 - tap], axis=0, keepdims=True)
        nxt = dxc_scr[...]
        dx = (cw_ref[3:4, :] * dxc + cw_ref[2:3, :] * _shift_up(dxc, nxt, 1)
              + cw_ref[1:2, :] * _shift_up(dxc, nxt, 2) + cw_ref[0:1, :] * _shift_up(dxc, nxt, 3))
        dxc_scr[...] = dxc[:8, :]
        dz_ref[:, :DB] = dx.astype(dz_ref.dtype)
        dz_ref[:, DB:] = dgate.astype(dz_ref.dtype)

    def halo(col):
        return lambda s: (jnp.maximum((nb - 1 - s) * per - 1, 0), col)

    const = lambda s: (0, 0)
    return pl.pallas_call(
        body, name="lru_bwd", grid=(nb,),
        in_specs=[pl.BlockSpec((tb, DB), lambda s: (nb - 1 - s, xcol)),
                  pl.BlockSpec((8, DB), halo(xcol)),
                  pl.BlockSpec((tb, DB), lambda s: (nb - 1 - s, gcol)),
                  pl.BlockSpec((tb, DB), lambda s: (nb - 1 - s, 0)),
                  pl.BlockSpec((8, DB), halo(0)),
                  pl.BlockSpec((tb, DB), lambda s: (nb - 1 - s, 0)),
                  pl.BlockSpec((8, DB), const), pl.BlockSpec((8, DB), const),
                  pl.BlockSpec((DB, DB), const), pl.BlockSpec((DB, DB), const),
                  pl.BlockSpec(memory_space=pl.ANY)],
        out_specs=[pl.BlockSpec((tb, 2 * DB), lambda s: (nb - 1 - s, (4 * DA) // (2 * DB))),
                   pl.BlockSpec((8, DB), const), pl.BlockSpec((8, DB), const),
                   pl.BlockSpec((DB, DB), const), pl.BlockSpec((DB, DB), const)],
        out_shape=[jax.ShapeDtypeStruct((t, DIN), SAVE), jax.ShapeDtypeStruct((8, DB), F32),
                   jax.ShapeDtypeStruct((8, DB), F32), jax.ShapeDtypeStruct((DB, DB), F32),
                   jax.ShapeDtypeStruct((DB, DB), F32)],
        scratch_shapes=[pltpu.VMEM((8, DB), F32), pltpu.VMEM((8, DB), F32), pltpu.VMEM((8, DB), F32)],
        input_output_aliases={10: 0},
        compiler_params=_params(("arbitrary",)),
    )(z, z, z, hseq, hseq, dob, cw, vec, wa, wx, dz)


def _sgu_chunk(u_in, v_in, w_ref, bias, gmat):
    uu, duu = _gelu_and_grad(u_in)
    vv, dvv = _gelu_and_grad(v_in)
    mu = _mm_exact_r(vv, gmat)
    dlt = vv - mu
    rstd_v = lax.rsqrt(_mm_exact_r(dlt * dlt, gmat) + EPS)
    vn = dlt * rstd_v
    col = _iota((CCH, DC), 1) // GRP
    causal = _iota((CCH, CCH), 0) >= _iota((CCH, CCH), 1)
    ws = [jnp.where(causal, w_ref[g], 0.0) for g in range(DC // GRP)]
    zz = bias
    for g, w in enumerate(ws):
        zz = zz + jnp.where(col == g, _mm(w, vn), 0.0)
    return uu, duu, dvv, rstd_v, vn, zz, ws, col, causal


def _sgu_fwd(z, w, bias, gain, tb):
    t = z.shape[0]
    ucol, vcol = (4 * DA + 2 * DB) // DC, (4 * DA + 2 * DB) // DC + 1

    def body(u_ref, v_ref, w_ref, b_ref, g_ref, oc_ref):
        gmat = _group_matrix(DC, 1.0 / GRP).astype(MXU)
        for ch in range(tb // CCH):
            rows = slice(ch * CCH, (ch + 1) * CCH)
            uu, _, _, _, _, zz, _, _, _ = _sgu_chunk(u_ref[rows, :], v_ref[rows, :], w_ref, b_ref[...], gmat)
            y = uu * zz
            ms = _mm_exact_r(y * y, gmat)
            oc_ref[rows, :] = (y * lax.rsqrt(ms + EPS) * g_ref[...]).astype(oc_ref.dtype)

    const = lambda i: (0, 0)
    return pl.pallas_call(
        body, name="sgu_fwd", grid=(t // tb,),
        in_specs=[pl.BlockSpec((tb, DC), lambda i: (i, ucol)),
                  pl.BlockSpec((tb, DC), lambda i: (i, vcol)),
                  pl.BlockSpec((DC // GRP, CCH, CCH), lambda i: (0, 0, 0)),
                  pl.BlockSpec((CCH, DC), const), pl.BlockSpec((1, DC), const)],
        out_specs=pl.BlockSpec((tb, DC), lambda i: (i, 0)),
        out_shape=jax.ShapeDtypeStruct((t, DC), SAVE),
        compiler_params=_params(("parallel",)),
    )(z, z, w, bias, gain)


def _sgu_bwd(z, doc, w, bias, gain, dz, tb):
    t = z.shape[0]
    nb = t // tb
    ucol, vcol = (4 * DA + 2 * DB) // DC, (4 * DA + 2 * DB) // DC + 1
    ng = DC // GRP

    def body(u_ref, v_ref, doc_ref, w_ref, b_ref, g_ref, _, dz_ref, dw_ref, dbias_ref, dgain_ref, dbsum_scr):
        i = pl.program_id(0)

        @pl.when(i == 0)
        def _():
            for ref in (dw_ref, dgain_ref, dbsum_scr):
                ref[...] = jnp.zeros_like(ref)

        gmat = _group_matrix(DC, 1.0 / GRP).astype(MXU)
        for ch in range(tb // CCH):
            rows = slice(ch * CCH, (ch + 1) * CCH)
            uu, duu, dvv, rstd_v, vn, zz, ws, col, causal = _sgu_chunk(
                u_ref[rows, :], v_ref[rows, :], w_ref, b_ref[...], gmat)
            y = uu * zz
            rstd = lax.rsqrt(_mm_exact_r(y * y, gmat) + EPS)
            n = y * rstd
            d_oc = doc_ref[rows, :]
            dn = d_oc * g_ref[...]
            dgain_ref[0:1, :] += jnp.sum(d_oc * n, axis=0, keepdims=True)
            dy = rstd * (dn - n * _mm_exact_r(dn * n, gmat))
            dzz = dy * uu
            dz_ref[rows, :DC] = (dy * zz * duu).astype(dz_ref.dtype)
            dbsum_scr[...] += dzz
            dvn = jnp.zeros_like(dzz)
            for g in range(ng):
                sel = col == g
                dvn = dvn + jnp.where(sel, _mm_tn(ws[g], dzz), 0.0)
                dw_ref[g] += jnp.where(causal, _mm_nt(jnp.where(sel, dzz, 0.0), vn), 0.0)
            dv = rstd_v * (dvn - _mm_exact_r(dvn, gmat) - vn * _mm_exact_r(dvn * vn, gmat))
            dz_ref[rows, DC:] = (dv * dvv).astype(dz_ref.dtype)

        @pl.when(i == nb - 1)
        def _():
            dbias_ref[...] = _mm_exact_r(dbsum_scr[...], _group_matrix(DC, 1.0).astype(MXU))

    const = lambda i: (0, 0)
    return pl.pallas_call(
        body, name="sgu_bwd", grid=(nb,),
        in_specs=[pl.BlockSpec((tb, DC), lambda i: (i, ucol)),
                  pl.BlockSpec((tb, DC), lambda i: (i, vcol)),
                  pl.BlockSpec((tb, DC), lambda i: (i, 0)),
                  pl.BlockSpec((ng, CCH, CCH), lambda i: (0, 0, 0)),
                  pl.BlockSpec((CCH, DC), const), pl.BlockSpec((1, DC), const),
                  pl.BlockSpec(memory_space=pl.ANY)],
        out_specs=[pl.BlockSpec((tb, 2 * DC), lambda i: (i, (4 * DA + 2 * DB) // (2 * DC))),
                   pl.BlockSpec((ng, CCH, CCH), lambda i: (0, 0, 0)),
                   pl.BlockSpec((CCH, DC), const), pl.BlockSpec((8, DC), const)],
        out_shape=[jax.ShapeDtypeStruct((t, DIN), SAVE), jax.ShapeDtypeStruct((ng, CCH, CCH), F32),
                   jax.ShapeDtypeStruct((CCH, DC), F32), jax.ShapeDtypeStruct((8, DC), F32)],
        scratch_shapes=[pltpu.VMEM((CCH, DC), F32)],
        input_output_aliases={6: 0},
        compiler_params=_params(("arbitrary",)),
    )(z, z, doc, w, bias, gain, dz)


def _head(h, gain, target, tm):
    t = h.shape[0]

    def body(h_ref, g_ref, t_ref, dh_ref, loss_ref, dgain_ref):
        @pl.when(pl.program_id(0) == 0)
        def _():
            loss_ref[...] = jnp.zeros_like(loss_ref)
            dgain_ref[...] = jnp.zeros_like(dgain_ref)

        hh = h_ref[...]
        gain = g_ref[...]
        rstd = lax.rsqrt(jnp.mean(hh * hh, axis=-1, keepdims=True) + EPS)
        xhat = hh * rstd
        err = xhat * gain - t_ref[...]
        per_tok = jnp.mean(err * err, axis=-1, keepdims=True)
        loss_ref[...] += 0.5 * jnp.sum(per_tok, axis=0, keepdims=True)
        dy = err * (1.0 / D)
        dgain_ref[...] += jnp.sum(dy * xhat, axis=0, keepdims=True)
        dxh = dy * gain
        dh_ref[...] = rstd * (dxh - xhat * jnp.mean(dxh * xhat, axis=-1, keepdims=True))

    return pl.pallas_call(
        body, name="head", grid=(t // tm,),
        in_specs=[pl.BlockSpec((tm, D), lambda i: (i, 0)),
                  pl.BlockSpec((1, D), lambda i: (0, 0)),
                  pl.BlockSpec((tm, D), lambda i: (i, 0))],
        out_specs=[pl.BlockSpec((tm, D), lambda i: (i, 0)),
                   pl.BlockSpec((1, 128), lambda i: (0, 0)),
                   pl.BlockSpec((1, D), lambda i: (0, 0))],
        out_shape=[jax.ShapeDtypeStruct((t, D), F32), jax.ShapeDtypeStruct((1, 128), F32),
                   jax.ShapeDtypeStruct((1, D), F32)],
        compiler_params=_params(("arbitrary",)),
    )(h, gain, target)


def _adamw(w, g, m, v):
    m = ADAM_B1 * m + (1.0 - ADAM_B1) * g
    v = ADAM_B2 * v + (1.0 - ADAM_B2) * (g * g)
    m_hat = m / (1.0 - ADAM_B1 ** ADAM_STEP)
    v_hat = v / (1.0 - ADAM_B2 ** ADAM_STEP)
    delta = -ADAM_LR * (m_hat / (jnp.sqrt(v_hat) + ADAM_EPS) + ADAM_WD * w)
    return delta, m, v


def _adamw_big(recv, w, m, v, tr, row0, name, after):
    depth, rows, cols = w.shape
    off = row0 // tr

    def body(*refs):
        r_refs = refs[:depth]
        w_ref, m_ref, v_ref, _, g_out, d_out, m_out, v_out = refs[depth:]
        for l in range(depth):
            g = r_refs[l][0].astype(F32)
            for k in range(1, NDEV):
                g = g + r_refs[l][k].astype(F32)
            delta, m_, v_ = _adamw(w_ref[l], g, m_ref[l], v_ref[l])
            g_out[l] = g
            d_out[l] = delta
            m_out[l] = m_
            v_out[l] = v_

    spec = pl.BlockSpec((depth, tr, cols), lambda i: (0, i, 0))
    return pl.pallas_call(
        body, name=name, grid=(rows // tr,),
        in_specs=[pl.BlockSpec((NDEV, tr, cols), lambda i: (0, i + off, 0))] * depth + [spec] * 3
        + [pl.BlockSpec(memory_space=pl.ANY)],
        out_specs=[spec] * 4, out_shape=[jax.ShapeDtypeStruct((depth, rows, cols), F32)] * 4,
        compiler_params=_params(("parallel",)),
    )(*recv, w, m, v, after)


def _sum_devices(recv):
    _, r, _ = recv.shape

    def body(r_ref, out_ref):
        g = r_ref[0]
        for k in range(1, NDEV):
            g = g + r_ref[k]
        out_ref[...] = g

    return pl.pallas_call(body, name="sum_devices", out_shape=jax.ShapeDtypeStruct((r, 128), F32))(recv)


def _adamw_small(w, g, m, v):
    def body(w_ref, g_ref, m_ref, v_ref, d_out, m_out, v_out):
        delta, m_, v_ = _adamw(w_ref[...], g_ref[...], m_ref[...], v_ref[...])
        d_out[...] = delta
        m_out[...] = m_
        v_out[...] = v_

    return pl.pallas_call(body, name="adamw_small", out_shape=[jax.ShapeDtypeStruct(w.shape, F32)] * 3)(w, g, m, v)


def _pack(arrs):
    flat = jnp.concatenate([a.reshape(-1) for a in arrs])
    pad = (-flat.shape[0]) % 1024
    return jnp.pad(flat, (0, pad)).reshape(-1, 128)


def _unpack(buf, like):
    flat = buf.reshape(-1)
    out, off = [], 0
    for a in like:
        out.append(flat[off:off + a.size].reshape(a.shape))
        off += a.size
    return out


def _block_diag(w):
    nb, bd, _ = w.shape
    eye = jnp.eye(nb, dtype=w.dtype)
    return (eye[:, None, :, None] * w[:, :, None, :]).reshape(nb * bd, nb * bd)


def _diag_blocks(w):
    nb = w.shape[0] // GRP
    return jnp.stack([w[g * GRP:(g + 1) * GRP, g * GRP:(g + 1) * GRP] for g in range(nb)])


SMALL = ['ffn1_norm', 'mix_norm', 'hgrn_lb_logits', 'hgrn_norm', 'conv_b', 'lru_wa', 'lru_ba', 'lru_wx', 'lru_bx',
         'lru_lambda', 'lru_norm', 'sgu_w', 'sgu_b', 'sgu_norm', 'ffn2_norm', 'final_norm']
NAMES = ['ffn1_norm', 'ffn1_wg', 'ffn1_wu', 'ffn1_wd', 'mix_norm', 'w_in', 'hgrn_lb_logits', 'hgrn_norm', 'conv_w',
         'conv_b', 'lru_wa', 'lru_ba', 'lru_wx', 'lru_bx', 'lru_lambda', 'lru_norm', 'sgu_w', 'sgu_b', 'sgu_norm',
         'w_out', 'ffn2_norm', 'ffn2_wg', 'ffn2_wu', 'ffn2_wd', 'final_norm']


def _step(x, target, w, m, v):
    depth = w['ffn1_wg'].shape[0]
    t = x.shape[1]
    h = x.reshape(t, D)
    target = target.reshape(t, D)
    tm_f, tm_b, tb = min(TM_F, t), min(TM_B, t), min(TB, t)
    my = 4 * lax.axis_index("x") + 2 * lax.axis_index("y") + lax.axis_index("c")

    cw_tile = jnp.pad(w['conv_w'].reshape(-1, 128), ((0, 8 - depth), (0, 0)))
    cw_all = _all_gather([cw_tile], "gather_conv")[0][:, :depth]
    conv_w = jnp.moveaxis(cw_all.reshape(NDEV, depth, 4, DB // NDEV), 0, 2).reshape(depth, 4, DB)
    lbs, lb_soft = _lower_bounds(w['hgrn_lb_logits'])

    def row(a):
        return a.reshape(1, -1)

    def tr(a):
        return jnp.swapaxes(a, -1, -2)

    def pad_rows(a):
        return jnp.pad(a, ((0, FFP - FFS), (0, 0)))

    def shards(l, unit):
        if unit == 1:
            return [w['w_in'][l].astype(WIRE), w['w_out'][l].astype(WIRE)]
        f = 'ffn1' if unit == 0 else 'ffn2'
        wgu = jnp.concatenate([pad_rows(tr(w[f + '_wg'][l])), pad_rows(tr(w[f + '_wu'][l]))], axis=0)
        return [wgu.astype(WIRE), pad_rows(w[f + '_wd'][l]).astype(WIRE)]

    units = [(l, u) for l in range(depth) for u in range(3)]

    def start_ici(idx, deps=()):
        return _transfer_start(shards(*units[idx]), True, "gather_ici_%d_%d" % units[idx], deps=deps)

    def start_d2d(idx, handle, after):
        lands = _transfer_wait(handle, after, "gather_ici_wait_%d_%d" % units[idx])
        return _forward_start(lands, "gather_d2d_%d_%d" % units[idx])

    pipe = dict(idx=0)
    pipe['d2d'] = start_d2d(0, start_ici(0), h)
    pipe['ici'] = start_ici(1, deps=(pipe['d2d']['token'],))

    def next_weights(after):
        idx = pipe['idx']
        lands = _transfer_wait(pipe['d2d'], after, "gather_d2d_wait_%d_%d" % units[idx])
        pipe['idx'] = idx + 1
        tok = 0.0
        if idx + 1 < len(units):
            pipe['d2d'] = start_d2d(idx + 1, pipe['ici'], lands[-1])
            tok = pipe['d2d']['token'][0, 0]
            if idx + 2 < len(units):
                pipe['ici'] = start_ici(idx + 2, deps=(pipe['d2d']['token'],))
                tok = pipe['ici']['token'][0, 0]
        return lands, tok

    saved = []
    for l in range(depth):
        (wgu1, wd1), tok = next_weights(h)
        s = dict(wgu1=wgu1, wd1=wd1, h0=h)
        h, s['xn1'], s['ab1'] = _ffn_fwd(h, row(w['ffn1_norm'][l]) + tok, wgu1, wd1, tm_f)
        s['h1'] = h
        (win, wout), tok = next_weights(h)
        wout = wout.reshape(D, D)
        s['win'], s['wout'] = win, wout
        z, s['xnm'] = _inproj_fwd(h, row(w['mix_norm'][l]) + tok, win, tm_f)
        s['z'] = z
        s['o'], oa, s['states'] = _hgrn_fwd(z, row(lbs[l]), row(w['hgrn_norm'][l]))
        s['cw'] = jnp.pad(conv_w[l], ((0, 4), (0, 0)))
        s['vec'] = jnp.concatenate([row(w['conv_b'][l]), row(w['lru_ba'][l]), row(w['lru_bx'][l]),
                                    row(w['lru_lambda'][l]), row(w['lru_norm'][l]), jnp.zeros((3, DB), F32)])
        s['wa'], s['wx'] = _block_diag(w['lru_wa'][l]), _block_diag(w['lru_wx'][l])
        ob, s['hseq'] = _lru_fwd(z, s['cw'], s['vec'], s['wa'], s['wx'], tb)
        s['bias'] = jnp.repeat(w['sgu_b'][l].T, GRP, axis=1)
        oc = _sgu_fwd(z, w['sgu_w'][l], s['bias'], row(w['sgu_norm'][l]), tb)
        s['oa'], s['ob'], s['oc'] = oa, ob, oc
        h = _outproj_fwd(h, oa, ob, oc, wout, tm_f)
        s['h2'] = h
        (wgu2, wd2), tok = next_weights(h)
        s['wgu2'], s['wd2'] = wgu2, wd2
        h, s['xn2'], s['ab2'] = _ffn_fwd(h, row(w['ffn2_norm'][l]) + tok, wgu2, wd2, tm_f)
        saved.append(s)

    dh, loss_part, g_final = _head(h, row(w['final_norm']), target, tm_f)
    loss = lax.psum(loss_part[0, 0], ("x", "y", "c"))

    recv = {k: [None] * depth for k in ('wgu1', 'wd1', 'wgu2', 'wd2', 'win', 'wout')}
    flight = []

    def land(after):
        handle, kinds, l = flight.pop()
        for k, a in zip(kinds, _transfer_wait(handle, after, f"exchange_wait_{kinds[0]}_{l}")):
            recv[k][l] = a

    def exchange(arrs, kinds, l):
        handle = _transfer_start(arrs, False, f"exchange_start_{kinds[0]}_{l}")
        if flight:
            land(handle['token'])
        flight.append((handle, kinds, l))
        return handle['token'][0, 0]

    small = {k: [None] * depth for k in SMALL if k != 'final_norm'}
    dconv = [None] * depth
    dlb = [None] * depth
    tok = 0.0
    for l in reversed(range(depth)):
        s = saved[l]
        dh, dwgu, dwd, g = _ffn_bwd(dh, s['h2'], row(w['ffn2_norm'][l]) + tok, s['xn2'], s['ab2'],
                                    s['wgu2'], s['wd2'], tm_b)
        tok = exchange([dwgu, dwd], ('wgu2', 'wd2'), l)
        small['ffn2_norm'][l] = g
        doa, dob, doc, dwout = _outproj_bwd(dh, s['oa'], s['ob'], s['oc'], s['wout'], tm_f)
        dz, g_hn, dlb[l] = _hgrn_bwd(s['z'], row(lbs[l]), row(w['hgrn_norm'][l]) + tok, s['o'], s['states'], doa)
        small['hgrn_norm'][l] = g_hn
        dz, dcw, dvec, dwa, dwx = _lru_bwd(s['z'], s['hseq'], dob, s['cw'], s['vec'], s['wa'], s['wx'], dz, tb)
        dconv[l] = dcw[:4]
        small['conv_b'][l], small['lru_ba'][l], small['lru_bx'][l] = dvec[0], dvec[1].reshape(4, GRP), dvec[2].reshape(4, GRP)
        small['lru_lambda'][l], small['lru_norm'][l] = dvec[3], dvec[4]
        small['lru_wa'][l], small['lru_wx'][l] = _diag_blocks(dwa), _diag_blocks(dwx)
        dz, dsw, dbias, dgc = _sgu_bwd(s['z'], doc, w['sgu_w'][l], s['bias'], row(w['sgu_norm'][l]), dz, tb)
        small['sgu_w'][l], small['sgu_b'][l], small['sgu_norm'][l] = dsw, dbias[:, ::GRP].T, dgc[0]
        dwin = _inproj_bwd_w(s['xnm'], dz, tm_f)
        tok = exchange([dwin, dwout.reshape(NDEV, D // NDEV, D)], ('win', 'wout'), l)
        dh, g = _inproj_bwd_x(dh, dz, s['h1'], row(w['mix_norm'][l]) + tok, s['win'], tm_b)
        small['mix_norm'][l] = g
        dh, dwgu, dwd, g = _ffn_bwd(dh, s['h0'], row(w['ffn1_norm'][l]), s['xn1'], s['ab1'],
                                    s['wgu1'], s['wd1'], tm_b)
        tok = exchange([dwgu, dwd], ('wgu1', 'wd1'), l)
        small['ffn1_norm'][l] = g
    grad_x = dh.reshape(1, t, D)
    small['hgrn_lb_logits'] = list(_lower_bounds_bwd(lb_soft, jnp.concatenate(dlb, axis=0)))

    out = {}
    last = flight[0][0]['token']

    def ffn_update(f, ku, kd, after):
        for k, row0 in ((f + '_wg', 0), (f + '_wu', FFP)):
            res = _adamw_big(recv[ku], tr(w[k]), tr(m[k]), tr(v[k]), 32, row0, "adamw_wgu", after)
            out[k] = tuple(tr(a) for a in res)
        k = f + '_wd'
        out[k] = _adamw_big(recv[kd], w[k], m[k], v[k], 32, 0, "adamw_wd", after)

    ffn_update('ffn2', 'wgu2', 'wd2', last)
    out['w_in'] = _adamw_big(recv['win'], w['w_in'], m['w_in'], v['w_in'], 64, 0, "adamw_win", last)
    out['w_out'] = _adamw_big(recv['wout'], w['w_out'], m['w_out'], v['w_out'], 64, 0, "adamw_wout", last)

    parts = [jnp.stack([small[k][l].reshape(w[k].shape[1:]) for l in range(depth)]) for k in SMALL if k != 'final_norm']
    parts += [g_final.reshape(D), jnp.stack(dconv)]
    total = _sum_devices(_all_gather([_pack(parts)], "gather_small")[0])
    like = [w[k] for k in SMALL] + [jax.ShapeDtypeStruct((depth, 4, DB), F32)]
    grads = _unpack(total, like)
    gsmall = dict(zip(SMALL, grads[:-1]))
    gsmall['conv_w'] = lax.dynamic_slice_in_dim(grads[-1], my * (DB // NDEV), DB // NDEV, axis=2)
    keys = SMALL + ['conv_w']
    dl, mm, vv = _adamw_small(_pack([w[k] for k in keys]), _pack([gsmall[k] for k in keys]),
                              _pack([m[k] for k in keys]), _pack([v[k] for k in keys]))
    like = [w[k] for k in keys]
    for k, d_, m_, v_ in zip(keys, _unpack(dl, like), _unpack(mm, like), _unpack(vv, like)):
        out[k] = (gsmall[k], d_, m_, v_)
    done = [dl] + [out[k][1][0] for k in ('ffn2_wg', 'ffn2_wu', 'ffn2_wd', 'w_in', 'w_out')]
    land(functools.reduce(lambda p, q: p + q, [a[:1, :1] for a in done]))
    ffn_update('ffn1', 'wgu1', 'wd1', last)

    return (loss, grad_x, *[out[k][0] for k in NAMES], *[out[k][1] for k in NAMES],
            *[out[k][2] for k in NAMES], *[out[k][3] for k in NAMES])


def kernel(x, ffn1_norm, ffn1_wg, ffn1_wu, ffn1_wd, mix_norm, w_in, hgrn_lb_logits, hgrn_norm, conv_w, conv_b, lru_wa, lru_ba, lru_wx, lru_bx, lru_lambda, lru_norm, sgu_w, sgu_b, sgu_norm, w_out, ffn2_norm, ffn2_wg, ffn2_wu, ffn2_wd, final_norm, loss_target, m_ffn1_norm, m_ffn1_wg, m_ffn1_wu, m_ffn1_wd, m_mix_norm, m_w_in, m_hgrn_lb_logits, m_hgrn_norm, m_conv_w, m_conv_b, m_lru_wa, m_lru_ba, m_lru_wx, m_lru_bx, m_lru_lambda, m_lru_norm, m_sgu_w, m_sgu_b, m_sgu_norm, m_w_out, m_ffn2_norm, m_ffn2_wg, m_ffn2_wu, m_ffn2_wd, m_final_norm, v_ffn1_norm, v_ffn1_wg, v_ffn1_wu, v_ffn1_wd, v_mix_norm, v_w_in, v_hgrn_lb_logits, v_hgrn_norm, v_conv_w, v_conv_b, v_lru_wa, v_lru_ba, v_lru_wx, v_lru_bx, v_lru_lambda, v_lru_norm, v_sgu_w, v_sgu_b, v_sgu_norm, v_w_out, v_ffn2_norm, v_ffn2_wg, v_ffn2_wu, v_ffn2_wd, v_final_norm):
    args = locals()
    w = {k: args[k] for k in NAMES}
    m = {k: args['m_' + k] for k in NAMES}
    v = {k: args['v_' + k] for k in NAMES}
    return _step(x, loss_target, w, m, v)
```

```python
import functools

import jax
import jax.numpy as jnp
from jax import lax
from jax.experimental import pallas as pl
from jax.experimental.pallas import tpu as pltpu

F32 = jnp.float32
MXU = jnp.bfloat16
SAVE = jnp.bfloat16
WIRE = jnp.bfloat16

NDEV = 8
D = 1024
FF = 2816
FFS = FF // NDEV
FB = 256
DIN = 3072
DINS = DIN // NDEV
DA, DB, DC = 512, 256, 256
HD = 128
NH = DA // HD
ACH = 64
ACB = 4
CCH = 128
GRP = 64
EPS = 1e-6
LRU_C = 8.0
VMEM_LIMIT = 60 * 1024 * 1024
TM_F = 1024
TM_B = 512
TB = 512
SUB = 256

ADAM_LR, ADAM_B1, ADAM_B2, ADAM_EPS, ADAM_WD, ADAM_STEP = 0.001, 0.9, 0.999, 1e-08, 0.01, 10

MESH = pl.DeviceIdType.MESH


def _mm(a, b):
    return jnp.dot(a.astype(MXU), b.astype(MXU), preferred_element_type=F32)


def _mm_nt(a, b):
    return lax.dot_general(a.astype(MXU), b.astype(MXU), (((1,), (1,)), ((), ())), preferred_element_type=F32)


def _mm_tn(a, b):
    return lax.dot_general(a.astype(MXU), b.astype(MXU), (((0,), (0,)), ((), ())), preferred_element_type=F32)


def _split3(x):
    x1 = x.astype(MXU)
    r1 = x - x1.astype(F32)
    x2 = r1.astype(MXU)
    r2 = r1 - x2.astype(F32)
    return x1, x2, r2.astype(MXU)


def _mm_exact_l(c, x):
    x1, x2, x3 = _split3(x)
    return _mm(c, x1) + _mm(c, x2) + _mm(c, x3)


def _mm_exact_r(x, c):
    x1, x2, x3 = _split3(x)
    return _mm(x1, c) + _mm(x2, c) + _mm(x3, c)


def _sigmoid(x):
    return 1.0 / (1.0 + jnp.exp(-x))


def _gelu(x):
    c, k = 0.7978845608028654, 0.044715
    th = jnp.tanh(c * (x + k * x * x * x))
    return 0.5 * x * (1.0 + th)


def _gelu_and_grad(x):
    c, k = 0.7978845608028654, 0.044715
    th = jnp.tanh(c * (x + k * x * x * x))
    g = 0.5 * x * (1.0 + th)
    dg = 0.5 * (1.0 + th) + 0.5 * x * (1.0 - th * th) * c * (1.0 + 3.0 * k * x * x)
    return g, dg


def _expm1(x):
    series = x * (1.0 + x * (0.5 + x * (1.0 / 6.0 + x * (1.0 / 24.0 + x * (1.0 / 120.0)))))
    return jnp.where(jnp.abs(x) < 0.05, series, jnp.exp(x) - 1.0)


def _iota(shape, dim):
    return lax.broadcasted_iota(jnp.int32, shape, dim)


def _tri(n, lower):
    r, c = _iota((n, n), 0), _iota((n, n), 1)
    return jnp.where((r >= c) if lower else (r <= c), 1.0, 0.0).astype(F32)


def _group_matrix(n, value):
    r, c = _iota((n, n), 0), _iota((n, n), 1)
    return jnp.where((r // GRP) == (c // GRP), value, 0.0).astype(F32)


def _row(x, k):
    r = _iota(x.shape, 0)
    return jnp.sum(jnp.where(r == k, x, 0.0), axis=0, keepdims=True)


def _rms_bwd(dxn, hh, gain):
    rstd = lax.rsqrt(jnp.mean(hh * hh, axis=-1, keepdims=True) + EPS)
    xhat = hh * rstd
    dxh = dxn * gain
    dh = rstd * (dxh - xhat * jnp.mean(dxh * xhat, axis=-1, keepdims=True))
    return dh, jnp.sum(dxn * xhat, axis=0, keepdims=True)


def _params(sem):
    return pltpu.CompilerParams(dimension_semantics=sem, vmem_limit_bytes=VMEM_LIMIT)


def _all_gather(arrs, name):
    n = len(arrs)

    def body(*refs):
        ins, outs = refs[:n], refs[n:2 * n]
        send_sems, recv_sems, local_sems = refs[2 * n:]
        x, y, c = lax.axis_index("x"), lax.axis_index("y"), lax.axis_index("c")
        me, sibling = (x, y, c), (x, y, 1 - c)
        chips = [(1 - x, y), (x, 1 - y), (1 - x, 1 - y)]

        def slot(px, py, pc):
            return 4 * px + 2 * py + pc

        def copy(a, k, block, to, src=None):
            dst = outs[a].at[slot(*block)]
            return pltpu.make_async_remote_copy(
                src_ref=dst if src is None else src, dst_ref=dst,
                send_sem=send_sems.at[a * 7 + k], recv_sem=recv_sems.at[a * 7 + k],
                device_id=to, device_id_type=MESH)

        started = []
        for a in range(n):
            mine = pltpu.make_async_copy(ins[a], outs[a].at[slot(*me)], local_sems.at[a])
            mine.start()
            started.append(mine)
        first = []
        for a in range(n):
            first.append(copy(a, 0, me, sibling, src=ins[a]))
            first += [copy(a, 1 + j, me, (*chip, c), src=ins[a]) for j, chip in enumerate(chips)]
        for cp in first:
            cp.start()
        passed = []
        for a in range(n):
            for j, chip in enumerate(chips):
                copy(a, 1 + j, (*chip, c), me).wait_recv()
                fwd = copy(a, 4 + j, (*chip, c), sibling)
                fwd.start()
                passed.append(fwd)
        for a in range(n):
            copy(a, 0, sibling, me).wait_recv()
            for j, chip in enumerate(chips):
                copy(a, 4 + j, (*chip, 1 - c), me).wait_recv()
        for cp in first + passed:
            cp.wait_send()
        for mine in started:
            mine.wait()

    hbm = pl.BlockSpec(memory_space=pl.ANY)
    return pl.pallas_call(
        body, name=name,
        out_shape=[jax.ShapeDtypeStruct((NDEV,) + a.shape, a.dtype) for a in arrs],
        in_specs=[hbm] * n, out_specs=[hbm] * n,
        scratch_shapes=[pltpu.SemaphoreType.DMA((7 * n,)), pltpu.SemaphoreType.DMA((7 * n,)),
                        pltpu.SemaphoreType.DMA((n,))],
    )(*arrs)


def _peers():
    x, y, c = lax.axis_index("x"), lax.axis_index("y"), lax.axis_index("c")
    peers = [(x ^ ((k >> 2) & 1), y ^ ((k >> 1) & 1), c ^ (k & 1)) for k in range(1, NDEV)]
    return (x, y, c), 4 * x + 2 * y + c, peers


_HBM = pl.BlockSpec(memory_space=pltpu.HBM)
_SEM = pl.BlockSpec(memory_space=pltpu.SEMAPHORE)
_EFFECT = pltpu.SideEffectType.DATAFLOW_SIDE_EFFECTING


def _transfer_start(arrs, gather, name, deps=()):
    n, nd = len(arrs), len(deps)
    shapes = [((NDEV,) + a.shape) if gather else a.shape for a in arrs]

    def body(*refs):
        ins, lands = refs[:n], refs[n:2 * n]
        send_sems, recv_sems, local_sems = refs[2 * n + nd:2 * n + nd + 3]
        token = refs[-1]
        (x, y, c), my, peers = _peers()
        if gather:
            peers = [(x, y, 1 - c), (1 - x, y, c), (x, 1 - y, c), (1 - x, 1 - y, c)]
        for a in range(n):
            own = ins[a] if gather else ins[a].at[my]
            pltpu.make_async_copy(own, lands[a].at[my], local_sems.at[a]).start()
        for a in range(n):
            for peer in peers:
                src = ins[a] if gather else ins[a].at[4 * peer[0] + 2 * peer[1] + peer[2]]
                pltpu.make_async_remote_copy(
                    src_ref=src, dst_ref=lands[a].at[my], send_sem=send_sems.at[a], recv_sem=recv_sems.at[a],
                    device_id=peer, device_id_type=MESH).start()
        token[...] = jnp.zeros_like(token)

    out_shape = [pltpu.SemaphoreType.DMA((n,))] * 3
    out_shape += [pltpu.HBM(a.shape, a.dtype) for a in arrs]
    out_shape += [pltpu.HBM(s, a.dtype) for s, a in zip(shapes, arrs)]
    out_shape += [jax.ShapeDtypeStruct((8, 128), F32)]
    operands = [pltpu.with_memory_space_constraint(a, pltpu.HBM) for a in arrs]
    operands += [pltpu.with_memory_space_constraint(lax.empty(s, a.dtype), pltpu.HBM) for s, a in zip(shapes, arrs)]
    res = pl.pallas_call(
        body, name=name, out_shape=out_shape,
        in_specs=[_HBM] * (2 * n) + [pl.BlockSpec(memory_space=pl.ANY)] * nd,
        out_specs=[_SEM] * 3 + [_HBM] * (2 * n) + [pl.BlockSpec(memory_space=pltpu.VMEM)],
        input_output_aliases={i: 3 + i for i in range(2 * n)},
        compiler_params=pltpu.CompilerParams(has_side_effects=_EFFECT),
    )(*operands, *deps)
    return dict(sems=res[:3], src=res[3:3 + n], lands=res[3 + n:3 + 2 * n], token=res[-1], n=n,
                count=4 if gather else NDEV - 1)


def _forward_start(lands, name, deps=()):
    n, nd = len(lands), len(deps)

    def body(*refs):
        zone = refs[:n]
        send_sems, recv_sems = refs[n + nd:n + nd + 2]
        token = refs[-1]
        (x, y, c), _, _ = _peers()
        for a in range(n):
            for px, py in ((1 - x, y), (x, 1 - y), (1 - x, 1 - y)):
                block = zone[a].at[4 * px + 2 * py + c]
                pltpu.make_async_remote_copy(
                    src_ref=block, dst_ref=block, send_sem=send_sems.at[a], recv_sem=recv_sems.at[a],
                    device_id=(x, y, 1 - c), device_id_type=MESH).start()
        token[...] = jnp.zeros_like(token)

    res = pl.pallas_call(
        body, name=name,
        out_shape=[pltpu.SemaphoreType.DMA((n,))] * 2 + [pltpu.HBM(a.shape, a.dtype) for a in lands]
        + [jax.ShapeDtypeStruct((8, 128), F32)],
        in_specs=[_HBM] * n + [pl.BlockSpec(memory_space=pl.ANY)] * nd,
        out_specs=[_SEM] * 2 + [_HBM] * n + [pl.BlockSpec(memory_space=pltpu.VMEM)],
        input_output_aliases={i: 2 + i for i in range(n)},
        compiler_params=pltpu.CompilerParams(has_side_effects=_EFFECT),
    )(*lands, *deps)
    return dict(sems=res[:2], src=[], lands=res[2:2 + n], token=res[-1], n=n, count=3)


def _transfer_wait(handle, after, name):
    n, count = handle["n"], handle["count"]
    src, lands, sems = list(handle["src"]), list(handle["lands"]), list(handle["sems"])
    ns = len(src)

    def body(*refs):
        zone = refs[ns:ns + n]
        sem_refs = refs[ns + n:ns + n + len(sems)]
        me, _, _ = _peers()
        for a in range(n):
            moved = zone[a].at[pl.ds(0, count)]
            both = pltpu.make_async_remote_copy(
                src_ref=moved, dst_ref=moved, send_sem=sem_refs[0].at[a], recv_sem=sem_refs[1].at[a],
                device_id=me, device_id_type=MESH)
            both.wait_send()
            both.wait_recv()
            if len(sems) == 3:
                pltpu.make_async_copy(zone[a].at[0], zone[a].at[1], sem_refs[2].at[a]).wait()

    res = pl.pallas_call(
        body, name=name,
        out_shape=[pltpu.HBM(a.shape, a.dtype) for a in src + lands],
        in_specs=[_HBM] * (ns + n) + [_SEM] * len(sems) + [pl.BlockSpec(memory_space=pl.ANY)],
        out_specs=[_HBM] * (ns + n),
        input_output_aliases={i: i for i in range(ns + n)},
        compiler_params=pltpu.CompilerParams(has_side_effects=_EFFECT),
    )(*src, *lands, *sems, after)
    return list(res[ns:])


def _ffn_fwd(h, gain, wg, wu, wd, tm):
    t = h.shape[0]
    nj = FF // FB

    def body(h_ref, g_ref, wg_ref, wu_ref, wd_ref, out_ref, xn_ref, a_ref, b_ref, acc_ref):
        j = pl.program_id(1)

        @pl.when(j == 0)
        def _():
            hh = h_ref[...]
            rstd = lax.rsqrt(jnp.mean(hh * hh, axis=-1, keepdims=True) + EPS)
            xn_ref[...] = (hh * rstd * g_ref[...]).astype(xn_ref.dtype)
            acc_ref[...] = jnp.zeros_like(acc_ref)

        sub = min(SUB, tm)
        for r in range(tm // sub):
            rows = slice(r * sub, (r + 1) * sub)
            xn = xn_ref[rows, :]
            a = _mm_nt(xn, wg_ref[...])
            b = _mm_nt(xn, wu_ref[...])
            a_ref[rows, :] = a.astype(a_ref.dtype)
            b_ref[rows, :] = b.astype(b_ref.dtype)
            acc_ref[rows, :] += _mm(a * _sigmoid(a) * b, wd_ref[...])

        @pl.when(j == nj - 1)
        def _():
            out_ref[...] = h_ref[...] + 0.5 * acc_ref[...]

    wspec = pl.BlockSpec((FB, D), lambda i, j: (j, 0))
    return pl.pallas_call(
        body, name="ffn_fwd", grid=(t // tm, nj),
        in_specs=[pl.BlockSpec((tm, D), lambda i, j: (i, 0)),
                  pl.BlockSpec((1, D), lambda i, j: (0, 0)), wspec, wspec, wspec],
        out_specs=[pl.BlockSpec((tm, D), lambda i, j: (i, 0)),
                   pl.BlockSpec((tm, D), lambda i, j: (i, 0)),
                   pl.BlockSpec((tm, FB), lambda i, j: (i, j)),
                   pl.BlockSpec((tm, FB), lambda i, j: (i, j))],
        out_shape=[jax.ShapeDtypeStruct((t, D), F32), jax.ShapeDtypeStruct((t, D), SAVE),
                   jax.ShapeDtypeStruct((t, FF), SAVE), jax.ShapeDtypeStruct((t, FF), SAVE)],
        scratch_shapes=[pltpu.VMEM((tm, D), F32)],
        compiler_params=_params(("parallel", "arbitrary")),
    )(h, gain, wg, wu, wd)


def _ffn_bwd_x(dout, h, gain, a_sv, b_sv, wg, wu, wd, tm):
    t = h.shape[0]
    nj = FF // FB

    def body(dout_ref, h_ref, g_ref, a_ref, b_ref, wg_ref, wu_ref, wd_ref,
             dh_ref, dgain_ref, dy_ref, da_ref, db_ref, s_ref, acc_ref):
        i, j = pl.program_id(0), pl.program_id(1)

        @pl.when((i == 0) & (j == 0))
        def _():
            dgain_ref[...] = jnp.zeros_like(dgain_ref)

        @pl.when(j == 0)
        def _():
            dy_ref[...] = (0.5 * dout_ref[...]).astype(dy_ref.dtype)
            acc_ref[...] = jnp.zeros_like(acc_ref)

        sub = min(SUB, tm)
        for r in range(tm // sub):
            rows = slice(r * sub, (r + 1) * sub)
            ds = _mm_nt(dy_ref[rows, :], wd_ref[...])
            a, b = a_ref[rows, :].astype(F32), b_ref[rows, :].astype(F32)
            sg = _sigmoid(a)
            sa = a * sg
            da = (ds * b * (sg * (1.0 + a * (1.0 - sg)))).astype(MXU)
            db = (ds * sa).astype(MXU)
            da_ref[rows, :] = da.astype(da_ref.dtype)
            db_ref[rows, :] = db.astype(db_ref.dtype)
            s_ref[rows, :] = (sa * b).astype(s_ref.dtype)
            acc_ref[rows, :] += _mm(da, wg_ref[...]) + _mm(db, wu_ref[...])

        @pl.when(j == nj - 1)
        def _():
            dh, dg = _rms_bwd(acc_ref[...], h_ref[...], g_ref[...])
            dh_ref[...] = dout_ref[...] + dh
            dgain_ref[...] += dg

    tok = pl.BlockSpec((tm, D), lambda i, j: (i, 0))
    act = pl.BlockSpec((tm, FB), lambda i, j: (i, j))
    wspec = pl.BlockSpec((FB, D), lambda i, j: (j, 0))
    return pl.pallas_call(
        body, name="ffn_bwd_x", grid=(t // tm, nj),
        in_specs=[tok, tok, pl.BlockSpec((1, D), lambda i, j: (0, 0)), act, act, wspec, wspec, wspec],
        out_specs=[tok, pl.BlockSpec((1, D), lambda i, j: (0, 0)), tok, act, act, act],
        out_shape=[jax.ShapeDtypeStruct((t, D), F32), jax.ShapeDtypeStruct((1, D), F32),
                   jax.ShapeDtypeStruct((t, D), SAVE)] + [jax.ShapeDtypeStruct((t, FF), SAVE)] * 3,
        scratch_shapes=[pltpu.VMEM((tm, D), F32)],
        compiler_params=_params(("arbitrary", "arbitrary")),
    )(dout, h, gain, a_sv, b_sv, wg, wu, wd)


def _ffn_bwd_w(xn, dy, da, db, s, tm):
    t = xn.shape[0]
    nt = t // tm
    nj = FF // FB

    def body(xn_ref, dy_ref, da_ref, db_ref, s_ref, dwg_ref, dwu_ref, dwd_ref, ag_scr, au_scr, ad_scr):
        i, j = pl.program_id(0), pl.program_id(1)
        rows = pl.ds(pl.multiple_of(j * FB, FB), FB)
        xn = xn_ref[...]
        new = ((ag_scr, _mm_tn(da_ref[...], xn)), (au_scr, _mm_tn(db_ref[...], xn)),
               (ad_scr, _mm_tn(s_ref[...], dy_ref[...])))

        @pl.when(i == 0)
        def _():
            for ref, val in new:
                ref[rows, :] = val

        @pl.when(i > 0)
        def _():
            for ref, val in new:
                ref[rows, :] += val

        @pl.when(i == nt - 1)
        def _():
            for out, ref in ((dwg_ref, ag_scr), (dwu_ref, au_scr), (dwd_ref, ad_scr)):
                out[...] = ref[rows, :].astype(out.dtype)

    tok = pl.BlockSpec((tm, D), lambda i, j: (i, 0))
    act = pl.BlockSpec((tm, FB), lambda i, j: (i, j))
    wspec = pl.BlockSpec((FB, D), lambda i, j: (jnp.where(i == nt - 1, j, 0), 0))
    return pl.pallas_call(
        body, name="ffn_bwd_w", grid=(nt, nj),
        in_specs=[tok, tok, act, act, act], out_specs=[wspec] * 3,
        out_shape=[jax.ShapeDtypeStruct((FF, D), WIRE)] * 3,
        scratch_shapes=[pltpu.VMEM((FF, D), F32)] * 3,
        compiler_params=_params(("arbitrary", "arbitrary")),
    )(xn, dy, da, db, s)


def _inproj_fwd(h, gain, win, tm):
    t = h.shape[0]

    def body(h_ref, g_ref, w_ref, z_ref, xn_ref):
        hh = h_ref[...]
        rstd = lax.rsqrt(jnp.mean(hh * hh, axis=-1, keepdims=True) + EPS)
        xn = (hh * rstd * g_ref[...]).astype(MXU)
        xn_ref[...] = xn.astype(xn_ref.dtype)
        for j in range(NDEV):
            z_ref[:, j * DINS:(j + 1) * DINS] = _mm(xn, w_ref[j])

    return pl.pallas_call(
        body, name="inproj_fwd", grid=(t // tm,),
        in_specs=[pl.BlockSpec((tm, D), lambda i: (i, 0)),
                  pl.BlockSpec((1, D), lambda i: (0, 0)),
                  pl.BlockSpec((NDEV, D, DINS), lambda i: (0, 0, 0))],
        out_specs=[pl.BlockSpec((tm, DIN), lambda i: (i, 0)),
                   pl.BlockSpec((tm, D), lambda i: (i, 0))],
        out_shape=[jax.ShapeDtypeStruct((t, DIN), F32), jax.ShapeDtypeStruct((t, D), SAVE)],
        compiler_params=_params(("parallel",)),
    )(h, gain, win)


def _inproj_bwd_x(dres, dz, h, gain, win, tm):
    t = h.shape[0]

    def body(dres_ref, dz_ref, h_ref, g_ref, w_ref, dh_ref, dgain_ref):
        @pl.when(pl.program_id(0) == 0)
        def _():
            dgain_ref[...] = jnp.zeros_like(dgain_ref)

        dxn = _mm_nt(dz_ref[:, :DINS], w_ref[0])
        for j in range(1, NDEV):
            dxn = dxn + _mm_nt(dz_ref[:, j * DINS:(j + 1) * DINS], w_ref[j])
        dh, dg = _rms_bwd(dxn, h_ref[...], g_ref[...])
        dh_ref[...] = dres_ref[...] + dh
        dgain_ref[...] += dg

    return pl.pallas_call(
        body, name="inproj_bwd_x", grid=(t // tm,),
        in_specs=[pl.BlockSpec((tm, D), lambda i: (i, 0)),
                  pl.BlockSpec((tm, DIN), lambda i: (i, 0)),
                  pl.BlockSpec((tm, D), lambda i: (i, 0)),
                  pl.BlockSpec((1, D), lambda i: (0, 0)),
                  pl.BlockSpec((NDEV, D, DINS), lambda i: (0, 0, 0))],
        out_specs=[pl.BlockSpec((tm, D), lambda i: (i, 0)),
                   pl.BlockSpec((1, D), lambda i: (0, 0))],
        out_shape=[jax.ShapeDtypeStruct((t, D), F32), jax.ShapeDtypeStruct((1, D), F32)],
        compiler_params=_params(("arbitrary",)),
    )(dres, dz, h, gain, win)


def _inproj_bwd_w(xn, dz, tm):
    t = xn.shape[0]
    nt = t // tm

    def body(xn_ref, dz_ref, dw_ref, acc_scr):
        i = pl.program_id(1)

        @pl.when(i == 0)
        def _():
            acc_scr[...] = jnp.zeros_like(acc_scr)

        acc_scr[...] += _mm_tn(xn_ref[...], dz_ref[...])

        @pl.when(i == nt - 1)
        def _():
            dw_ref[...] = acc_scr[...].astype(dw_ref.dtype)

    return pl.pallas_call(
        body, name="inproj_bwd_w", grid=(NDEV, nt),
        in_specs=[pl.BlockSpec((tm, D), lambda j, i: (i, 0)),
                  pl.BlockSpec((tm, DINS), lambda j, i: (i, j))],
        out_specs=pl.BlockSpec((None, D, DINS), lambda j, i: (j, 0, 0)),
        out_shape=jax.ShapeDtypeStruct((NDEV, D, DINS), WIRE),
        scratch_shapes=[pltpu.VMEM((D, DINS), F32)],
        compiler_params=_params(("parallel", "arbitrary")),
    )(xn, dz)


def _outproj_fwd(h, oa, ob, oc, wout, tm):
    t = h.shape[0]

    def body(h_ref, oa_ref, ob_ref, oc_ref, w_ref, out_ref):
        ym = jnp.concatenate([oa_ref[...], ob_ref[...], oc_ref[...]], axis=1)
        out_ref[...] = h_ref[...] + _mm(ym, w_ref[...])

    return pl.pallas_call(
        body, name="outproj_fwd", grid=(t // tm,),
        in_specs=[pl.BlockSpec((tm, D), lambda i: (i, 0)),
                  pl.BlockSpec((tm, DA), lambda i: (i, 0)),
                  pl.BlockSpec((tm, DB), lambda i: (i, 0)),
                  pl.BlockSpec((tm, DC), lambda i: (i, 0)),
                  pl.BlockSpec((D, D), lambda i: (0, 0))],
        out_specs=pl.BlockSpec((tm, D), lambda i: (i, 0)),
        out_shape=jax.ShapeDtypeStruct((t, D), F32),
        compiler_params=_params(("parallel",)),
    )(h, oa, ob, oc, wout)


def _outproj_bwd(dh, oa, ob, oc, wout, tm):
    t = dh.shape[0]
    nt = t // tm

    def body(dh_ref, oa_ref, ob_ref, oc_ref, w_ref, da_ref, db_ref, dc_ref, dw_ref, acc_scr):
        i = pl.program_id(0)

        @pl.when(i == 0)
        def _():
            acc_scr[...] = jnp.zeros_like(acc_scr)

        d16 = dh_ref[...].astype(MXU)
        dym = _mm_nt(d16, w_ref[...])
        da_ref[...] = dym[:, :DA]
        db_ref[...] = dym[:, DA:DA + DB]
        dc_ref[...] = dym[:, DA + DB:]
        ym = jnp.concatenate([oa_ref[...], ob_ref[...], oc_ref[...]], axis=1)
        acc_scr[...] += _mm_tn(ym, d16)

        @pl.when(i == nt - 1)
        def _():
            dw_ref[...] = acc_scr[...].astype(dw_ref.dtype)

    return pl.pallas_call(
        body, name="outproj_bwd", grid=(nt,),
        in_specs=[pl.BlockSpec((tm, D), lambda i: (i, 0)),
                  pl.BlockSpec((tm, DA), lambda i: (i, 0)),
                  pl.BlockSpec((tm, DB), lambda i: (i, 0)),
                  pl.BlockSpec((tm, DC), lambda i: (i, 0)),
                  pl.BlockSpec((D, D), lambda i: (0, 0))],
        out_specs=[pl.BlockSpec((tm, DA), lambda i: (i, 0)),
                   pl.BlockSpec((tm, DB), lambda i: (i, 0)),
                   pl.BlockSpec((tm, DC), lambda i: (i, 0)),
                   pl.BlockSpec((D, D), lambda i: (0, 0))],
        out_shape=[jax.ShapeDtypeStruct((t, DA), F32), jax.ShapeDtypeStruct((t, DB), F32),
                   jax.ShapeDtypeStruct((t, DC), F32), jax.ShapeDtypeStruct((D, D), WIRE)],
        scratch_shapes=[pltpu.VMEM((D, D), F32)],
        compiler_params=_params(("arbitrary",)),
    )(dh, oa, ob, oc, wout)


def _lower_bounds(logits):
    depth, n = logits.shape

    def body(l_ref, lb_ref, p_ref):
        rows = [l_ref[l:l + 1, :] for l in range(depth)]
        mx = functools.reduce(jnp.maximum, rows)
        ex = [jnp.exp(r - mx) for r in rows]
        den = functools.reduce(lambda u, v: u + v, ex)
        acc = jnp.zeros_like(den)
        for l in range(depth):
            p = ex[l] / den
            p_ref[l:l + 1, :] = p
            if l > 0:
                acc = acc + p
            lb_ref[l:l + 1, :] = acc

    return pl.pallas_call(
        body, name="lower_bounds",
        out_shape=[jax.ShapeDtypeStruct((depth, n), F32), jax.ShapeDtypeStruct((depth, n), F32)],
    )(logits)


def _lower_bounds_bwd(p, dlb):
    depth, n = p.shape

    def body(p_ref, d_ref, out_ref):
        ps = [p_ref[l:l + 1, :] for l in range(depth)]
        ds = [d_ref[l:l + 1, :] for l in range(depth)]
        dp = [jnp.zeros_like(ps[0]) for _ in range(depth)]
        run = jnp.zeros_like(ps[0])
        for l in range(depth - 1, 0, -1):
            run = run + ds[l]
            dp[l] = run
        dot = functools.reduce(lambda u, v: u + v, [ps[l] * dp[l] for l in range(depth)])
        for l in range(depth):
            out_ref[l:l + 1, :] = ps[l] * (dp[l] - dot)

    return pl.pallas_call(body, name="lower_bounds_bwd", out_shape=jax.ShapeDtypeStruct((depth, n), F32))(p, dlb)


def _hgrn_chunk(z_ref, lb_ref, hd, rows):
    c0 = hd * HD
    q = z_ref[rows, c0:c0 + HD]
    fl = z_ref[rows, DA + c0:DA + c0 + HD]
    v = z_ref[rows, 2 * DA + c0:2 * DA + c0 + HD]
    g = z_ref[rows, 3 * DA + c0:3 * DA + c0 + HD]
    lb = lb_ref[:, c0:c0 + HD]
    sq = _sigmoid(q)
    qs = q * sq
    sg = _sigmoid(fl)
    f = lb + (1.0 - lb) * sg
    k = 1.0 - f
    lf = jnp.log(f)
    b = _mm_exact_l(_tri(ACH, True).astype(MXU), lf)
    bend = jnp.sum(lf, axis=0, keepdims=True)
    r = 0.5 * bend
    eq, ek, eb, ed = jnp.exp(b - r), jnp.exp(r - b), jnp.exp(b), jnp.exp(bend - b)
    qt, kt, qe, kd = qs * eq, k * ek, qs * eb, k * ed
    causal = _iota((ACH, ACH), 0) >= _iota((ACH, ACH), 1)
    att = jnp.where(causal, _mm_nt(qt, kt), 0.0)
    return dict(q=q, v=v, g=g, lb=lb, sq=sq, qs=qs, sg=sg, f=f, k=k, bend=bend, eq=eq, ek=ek, eb=eb, ed=ed,
                qt=qt, kt=kt, qe=qe, kd=kd, att=att, causal=causal)


def _hgrn_fwd(z, lb, gain):
    t = z.shape[0]
    nc = t // ACH
    cb = min(ACB, nc)
    rb = cb * ACH

    def body(z_ref, lb_ref, g_ref, o_ref, oa_ref, st_ref, st_scr):
        @pl.when(pl.program_id(0) == 0)
        def _():
            st_scr[...] = jnp.zeros_like(st_scr)

        for hd in range(NH):
            st = st_scr[hd]
            cols = slice(hd * HD, (hd + 1) * HD)
            for cc in range(cb):
                rows = slice(cc * ACH, (cc + 1) * ACH)
                c = _hgrn_chunk(z_ref, lb_ref, hd, rows)
                st_ref[cc, hd] = st
                o = _mm(c["att"], c["v"]) + _mm_nt(c["qe"], st)
                st = st * jnp.exp(c["bend"]) + _mm_tn(c["v"], c["kd"])
                o_ref[rows, cols] = o
                rstd = lax.rsqrt(jnp.mean(o * o, axis=-1, keepdims=True) + EPS)
                gg = c["g"]
                oa_ref[rows, cols] = (o * rstd * g_ref[:, cols] * (gg * _sigmoid(gg))).astype(oa_ref.dtype)
            st_scr[hd] = st

    return pl.pallas_call(
        body, name="hgrn_fwd", grid=(nc // cb,),
        in_specs=[pl.BlockSpec((rb, 4 * DA), lambda c: (c, 0)),
                  pl.BlockSpec((1, DA), lambda c: (0, 0)),
                  pl.BlockSpec((1, DA), lambda c: (0, 0))],
        out_specs=[pl.BlockSpec((rb, DA), lambda c: (c, 0)),
                   pl.BlockSpec((rb, DA), lambda c: (c, 0)),
                   pl.BlockSpec((cb, NH, HD, HD), lambda c: (c, 0, 0, 0))],
        out_shape=[jax.ShapeDtypeStruct((t, DA), F32), jax.ShapeDtypeStruct((t, DA), SAVE),
                   jax.ShapeDtypeStruct((nc, NH, HD, HD), F32)],
        scratch_shapes=[pltpu.VMEM((NH, HD, HD), F32)],
        compiler_params=_params(("arbitrary",)),
    )(z, lb, gain)


def _hgrn_bwd(z, lb, gain, o, states, doa):
    t = z.shape[0]
    nc = t // ACH
    cb = min(ACB, nc)
    rb = cb * ACH
    nblk = nc // cb

    def body(z_ref, lb_ref, g_ref, o_ref, st_ref, doa_ref, dz_ref, dgain_ref, dlb_ref, dst_scr):
        @pl.when(pl.program_id(0) == 0)
        def _():
            dst_scr[...] = jnp.zeros_like(dst_scr)
            dgain_ref[...] = jnp.zeros_like(dgain_ref)
            dlb_ref[...] = jnp.zeros_like(dlb_ref)

        upper = _tri(ACH, False).astype(MXU)
        for hd in range(NH):
            cols = slice(hd * HD, (hd + 1) * HD)
            gain = g_ref[:, cols]
            dsp = dst_scr[hd]
            dgain = jnp.zeros((1, HD), F32)
            dlb = jnp.zeros((1, HD), F32)
            for cc in reversed(range(cb)):
                rows = slice(cc * ACH, (cc + 1) * ACH)
                c = _hgrn_chunk(z_ref, lb_ref, hd, rows)
                o = o_ref[rows, cols]
                do_a = doa_ref[rows, cols]
                gg = c["g"]
                sgg = _sigmoid(gg)
                silu_g = gg * sgg
                rstd = lax.rsqrt(jnp.mean(o * o, axis=-1, keepdims=True) + EPS)
                n = o * rstd
                dn = do_a * gain * silu_g
                dg = do_a * n * gain * (sgg * (1.0 + gg * (1.0 - sgg)))
                dgain = dgain + jnp.sum(do_a * silu_g * n, axis=0, keepdims=True)
                d_o = rstd * (dn - n * jnp.mean(dn * n, axis=-1, keepdims=True))

                st = st_ref[cc, hd]
                datt = jnp.where(c["causal"], _mm_nt(d_o, c["v"]), 0.0)
                dv = _mm_tn(c["att"], d_o) + _mm_nt(c["kd"], dsp)
                dqt = _mm(datt, c["kt"])
                dqe = _mm(d_o, st)
                dkt = _mm_tn(datt, c["qt"])
                dkd = _mm(c["v"], dsp)
                decay = jnp.exp(c["bend"])
                dbend = (decay * jnp.sum(st * dsp, axis=0, keepdims=True)
                         + jnp.sum(dkd * c["kd"], axis=0, keepdims=True))
                dsp = dsp * decay + _mm_tn(d_o, c["qe"])
                db = dqt * c["qt"] + dqe * c["qe"] - dkt * c["kt"] - dkd * c["kd"]
                dqs = dqt * c["eq"] + dqe * c["eb"]
                dk = dkt * c["ek"] + dkd * c["ed"]
                dlf = _mm_exact_l(upper, db) + dbend
                df = dlf / c["f"] - dk
                sg = c["sg"]
                dlb = dlb + jnp.sum(df * (1.0 - sg), axis=0, keepdims=True)
                dfl = df * (1.0 - c["lb"]) * sg * (1.0 - sg)
                sq, q = c["sq"], c["q"]
                dq = dqs * (sq * (1.0 + q * (1.0 - sq)))
                c0 = hd * HD
                dz_ref[rows, c0:c0 + HD] = dq.astype(dz_ref.dtype)
                dz_ref[rows, DA + c0:DA + c0 + HD] = dfl.astype(dz_ref.dtype)
                dz_ref[rows, 2 * DA + c0:2 * DA + c0 + HD] = dv.astype(dz_ref.dtype)
                dz_ref[rows, 3 * DA + c0:3 * DA + c0 + HD] = dg.astype(dz_ref.dtype)
            dst_scr[hd] = dsp
            dgain_ref[:, cols] += dgain
            dlb_ref[:, cols] += dlb

    rev = lambda c: (nblk - 1 - c, 0)
    return pl.pallas_call(
        body, name="hgrn_bwd", grid=(nblk,),
        in_specs=[pl.BlockSpec((rb, 4 * DA), rev),
                  pl.BlockSpec((1, DA), lambda c: (0, 0)),
                  pl.BlockSpec((1, DA), lambda c: (0, 0)),
                  pl.BlockSpec((rb, DA), rev),
                  pl.BlockSpec((cb, NH, HD, HD), lambda c: (nblk - 1 - c, 0, 0, 0)),
                  pl.BlockSpec((rb, DA), rev)],
        out_specs=[pl.BlockSpec((rb, 4 * DA), rev),
                   pl.BlockSpec((1, DA), lambda c: (0, 0)),
                   pl.BlockSpec((1, DA), lambda c: (0, 0))],
        out_shape=[jax.ShapeDtypeStruct((t, DIN), SAVE), jax.ShapeDtypeStruct((1, DA), F32),
                   jax.ShapeDtypeStruct((1, DA), F32)],
        scratch_shapes=[pltpu.VMEM((NH, HD, HD), F32)],
        compiler_params=_params(("arbitrary",)),
    )(z, lb, gain, o, states, doa)


def _shift_down(prev8, x, k):
    cat = jnp.concatenate([prev8, x], axis=0)
    return pltpu.roll(cat, k, axis=0)[8:, :]


def _shift_up(x, next8, k):
    n = x.shape[0]
    cat = jnp.concatenate([x, next8], axis=0)
    return pltpu.roll(cat, n + 8 - k, axis=0)[:n, :]


def _lru_gates(x, prev8, cw_ref, vec_ref, wa_ref, wx_ref):
    xs = [x, _shift_down(prev8, x, 1), _shift_down(prev8, x, 2), _shift_down(prev8, x, 3)]
    xc = vec_ref[0:1, :] + cw_ref[3:4, :] * xs[0] + cw_ref[2:3, :] * xs[1] + cw_ref[1:2, :] * xs[2] + cw_ref[0:1, :] * xs[3]
    r = _sigmoid(_mm(xc, wa_ref[...]) + vec_ref[1:2, :])
    gi = _sigmoid(_mm(xc, wx_ref[...]) + vec_ref[2:3, :])
    lam = vec_ref[3:4, :]
    sp = jnp.maximum(-lam, 0.0) + jnp.log(1.0 + jnp.exp(-jnp.abs(lam)))
    la = -LRU_C * r * sp
    a = jnp.exp(la)
    mult = jnp.sqrt(-_expm1(2.0 * la))
    return xs, xc, r, gi, sp, a, mult


def _scan_down(a, u):
    n = a.shape[0]
    row = _iota(a.shape, 0)
    s = 1
    while s < n:
        keep = row >= s
        ash = jnp.where(keep, pltpu.roll(a, s, axis=0), 1.0)
        ush = jnp.where(keep, pltpu.roll(u, s, axis=0), 0.0)
        u = a * ush + u
        a = a * ash
        s *= 2
    return a, u


def _scan_up(a, u):
    n = a.shape[0]
    row = _iota(a.shape, 0)
    s = 1
    while s < n:
        keep = row < n - s
        ash = jnp.where(keep, pltpu.roll(a, n - s, axis=0), 1.0)
        ush = jnp.where(keep, pltpu.roll(u, n - s, axis=0), 0.0)
        u = a * ush + u
        a = a * ash
        s *= 2
    return a, u


def _lru_fwd(z, cw, vec, wa, wx, tb):
    t = z.shape[0]
    xcol, gcol = (4 * DA) // DB, (4 * DA) // DB + 1

    def body(x_ref, gate_ref, cw_ref, vec_ref, wa_ref, wx_ref, ob_ref, h_ref, xprev_scr, hc_scr):
        @pl.when(pl.program_id(0) == 0)
        def _():
            xprev_scr[...] = jnp.zeros_like(xprev_scr)
            hc_scr[...] = jnp.zeros_like(hc_scr)

        x = x_ref[...]
        _, xc, _, gi, _, a, mult = _lru_gates(x, xprev_scr[...], cw_ref, vec_ref, wa_ref, wx_ref)
        acum, hloc = _scan_down(a, mult * gi * xc)
        h = hloc + acum * hc_scr[0:1, :]
        h_ref[...] = h
        hc_scr[...] = jnp.broadcast_to(_row(h, tb - 1), hc_scr.shape)
        xprev_scr[...] = x[tb - 8:, :]
        y = h * _gelu(gate_ref[...])
        ms = _mm_exact_r(y * y, _group_matrix(DB, 1.0 / GRP).astype(MXU))
        ob_ref[...] = (y * lax.rsqrt(ms + EPS) * vec_ref[4:5, :]).astype(ob_ref.dtype)

    return pl.pallas_call(
        body, name="lru_fwd", grid=(t // tb,),
        in_specs=[pl.BlockSpec((tb, DB), lambda i: (i, xcol)),
                  pl.BlockSpec((tb, DB), lambda i: (i, gcol)),
                  pl.BlockSpec((8, DB), lambda i: (0, 0)),
                  pl.BlockSpec((8, DB), lambda i: (0, 0)),
                  pl.BlockSpec((DB, DB), lambda i: (0, 0)),
                  pl.BlockSpec((DB, DB), lambda i: (0, 0))],
        out_specs=[pl.BlockSpec((tb, DB), lambda i: (i, 0)),
                   pl.BlockSpec((tb, DB), lambda i: (i, 0))],
        out_shape=[jax.ShapeDtypeStruct((t, DB), SAVE), jax.ShapeDtypeStruct((t, DB), F32)],
        scratch_shapes=[pltpu.VMEM((8, DB), F32), pltpu.VMEM((8, DB), F32)],
        compiler_params=_params(("arbitrary",)),
    )(z, z, cw, vec, wa, wx)


def _lru_bwd(z, hseq, dob, cw, vec, wa, wx, dz, tb):
    t = z.shape[0]
    nb = t // tb
    xcol, gcol = (4 * DA) // DB, (4 * DA) // DB + 1
    per = tb // 8

    def body(x_ref, xh_ref, gate_ref, h_ref, hh_ref, dob_ref, cw_ref, vec_ref, wa_ref, wx_ref, _,
             dz_ref, dcw_ref, dvec_ref, dwa_ref, dwx_ref, gc_scr, an_scr, dxc_scr):
        step = pl.program_id(0)
        blk = nb - 1 - step

        @pl.when(step == 0)
        def _():
            for ref in (gc_scr, an_scr, dxc_scr, dcw_ref, dvec_ref, dwa_ref, dwx_ref):
                ref[...] = jnp.zeros_like(ref)

        first = (blk > 0).astype(F32)
        x = x_ref[...]
        xs, xc, r, gi, sp, a, mult = _lru_gates(x, xh_ref[...] * first, cw_ref, vec_ref, wa_ref, wx_ref)
        h = h_ref[...]
        hprev = _shift_down(hh_ref[...] * first, h, 1)
        ge, dge = _gelu_and_grad(gate_ref[...])
        y = h * ge
        gmat = _group_matrix(DB, 1.0 / GRP).astype(MXU)
        rstd = lax.rsqrt(_mm_exact_r(y * y, gmat) + EPS)
        n = y * rstd
        d_ob = dob_ref[...]
        dn = d_ob * vec_ref[4:5, :]
        dvec_ref[4:5, :] += jnp.sum(d_ob * n, axis=0, keepdims=True)
        dy = rstd * (dn - n * _mm_exact_r(dn * n, gmat))
        dh = dy * ge
        dgate = dy * h * dge

        row = _iota(a.shape, 0)
        anext = jnp.where(row == tb - 1, an_scr[0:1, :], pltpu.roll(a, tb - 1, axis=0))
        acum, gloc = _scan_up(anext, dh)
        g = gloc + acum * gc_scr[0:1, :]
        gc_scr[...] = jnp.broadcast_to(_row(g, 0), gc_scr.shape)
        an_scr[...] = jnp.broadcast_to(_row(a, 0), an_scr.shape)

        da = g * hprev
        dmult = g * gi * xc
        dgi = g * mult * xc
        dxc = g * mult * gi
        dla = da * a - dmult * (a * a) / mult
        dr = dla * (-LRU_C * sp)
        dsp = jnp.sum(dla * (-LRU_C * r), axis=0, keepdims=True)
        lam = vec_ref[3:4, :]
        dvec_ref[3:4, :] += -dsp * _sigmoid(-lam)
        dpa = dr * r * (1.0 - r)
        dpx = dgi * gi * (1.0 - gi)
        dwa_ref[...] += _mm_tn(xc, dpa)
        dwx_ref[...] += _mm_tn(xc, dpx)
        dvec_ref[1:2, :] += jnp.sum(dpa, axis=0, keepdims=True)
        dvec_ref[2:3, :] += jnp.sum(dpx, axis=0, keepdims=True)
        dxc = dxc + _mm_nt(dpa, wa_ref[...]) + _mm_nt(dpx, wx_ref[...])
        dvec_ref[0:1, :] += jnp.sum(dxc, axis=0, keepdims=True)
        for tap in range(4):
            dcw_ref[tap:tap + 1, :] += jnp.sum(dxc * xs[3 - tap], axis=0, keepdims=True)
        nxt = dxc_scr[...]
        dx = (cw_ref[3:4, :] * dxc + cw_ref[2:3, :] * _shift_up(dxc, nxt, 1)
              + cw_ref[1:2, :] * _shift_up(dxc, nxt, 2) + cw_ref[0:1, :] * _shift_up(dxc, nxt, 3))
        dxc_scr[...] = dxc[:8, :]
        dz_ref[:, :DB] = dx.astype(dz_ref.dtype)
        dz_ref[:, DB:] = dgate.astype(dz_ref.dtype)

    def halo(col):
        return lambda s: (jnp.maximum((nb - 1 - s) * per - 1, 0), col)

    const = lambda s: (0, 0)
    return pl.pallas_call(
        body, name="lru_bwd", grid=(nb,),
        in_specs=[pl.BlockSpec((tb, DB), lambda s: (nb - 1 - s, xcol)),
                  pl.BlockSpec((8, DB), halo(xcol)),
                  pl.BlockSpec((tb, DB), lambda s: (nb - 1 - s, gcol)),
                  pl.BlockSpec((tb, DB), lambda s: (nb - 1 - s, 0)),
                  pl.BlockSpec((8, DB), halo(0)),
                  pl.BlockSpec((tb, DB), lambda s: (nb - 1 - s, 0)),
                  pl.BlockSpec((8, DB), const), pl.BlockSpec((8, DB), const),
                  pl.BlockSpec((DB, DB), const), pl.BlockSpec((DB, DB), const),
                  pl.BlockSpec(memory_space=pl.ANY)],
        out_specs=[pl.BlockSpec((tb, 2 * DB), lambda s: (nb - 1 - s, (4 * DA) // (2 * DB))),
                   pl.BlockSpec((8, DB), const), pl.BlockSpec((8, DB), const),
                   pl.BlockSpec((DB, DB), const), pl.BlockSpec((DB, DB), const)],
        out_shape=[jax.ShapeDtypeStruct((t, DIN), SAVE), jax.ShapeDtypeStruct((8, DB), F32),
                   jax.ShapeDtypeStruct((8, DB), F32), jax.ShapeDtypeStruct((DB, DB), F32),
                   jax.ShapeDtypeStruct((DB, DB), F32)],
        scratch_shapes=[pltpu.VMEM((8, DB), F32), pltpu.VMEM((8, DB), F32), pltpu.VMEM((8, DB), F32)],
        input_output_aliases={10: 0},
        compiler_params=_params(("arbitrary",)),
    )(z, z, z, hseq, hseq, dob, cw, vec, wa, wx, dz)


def _sgu_chunk(u_in, v_in, w_ref, bias, gmat):
    uu, duu = _gelu_and_grad(u_in)
    vv, dvv = _gelu_and_grad(v_in)
    mu = _mm_exact_r(vv, gmat)
    dlt = vv - mu
    rstd_v = lax.rsqrt(_mm_exact_r(dlt * dlt, gmat) + EPS)
    vn = dlt * rstd_v
    col = _iota((CCH, DC), 1) // GRP
    causal = _iota((CCH, CCH), 0) >= _iota((CCH, CCH), 1)
    ws = [jnp.where(causal, w_ref[g], 0.0) for g in range(DC // GRP)]
    zz = bias
    for g, w in enumerate(ws):
        zz = zz + jnp.where(col == g, _mm(w, vn), 0.0)
    return uu, duu, dvv, rstd_v, vn, zz, ws, col, causal


def _sgu_fwd(z, w, bias, gain, tb):
    t = z.shape[0]
    ucol, vcol = (4 * DA + 2 * DB) // DC, (4 * DA + 2 * DB) // DC + 1

    def body(u_ref, v_ref, w_ref, b_ref, g_ref, oc_ref):
        gmat = _group_matrix(DC, 1.0 / GRP).astype(MXU)
        for ch in range(tb // CCH):
            rows = slice(ch * CCH, (ch + 1) * CCH)
            uu, _, _, _, _, zz, _, _, _ = _sgu_chunk(u_ref[rows, :], v_ref[rows, :], w_ref, b_ref[...], gmat)
            y = uu * zz
            ms = _mm_exact_r(y * y, gmat)
            oc_ref[rows, :] = (y * lax.rsqrt(ms + EPS) * g_ref[...]).astype(oc_ref.dtype)

    const = lambda i: (0, 0)
    return pl.pallas_call(
        body, name="sgu_fwd", grid=(t // tb,),
        in_specs=[pl.BlockSpec((tb, DC), lambda i: (i, ucol)),
                  pl.BlockSpec((tb, DC), lambda i: (i, vcol)),
                  pl.BlockSpec((DC // GRP, CCH, CCH), lambda i: (0, 0, 0)),
                  pl.BlockSpec((CCH, DC), const), pl.BlockSpec((1, DC), const)],
        out_specs=pl.BlockSpec((tb, DC), lambda i: (i, 0)),
        out_shape=jax.ShapeDtypeStruct((t, DC), SAVE),
        compiler_params=_params(("parallel",)),
    )(z, z, w, bias, gain)


def _sgu_bwd(z, doc, w, bias, gain, dz, tb):
    t = z.shape[0]
    nb = t // tb
    ucol, vcol = (4 * DA + 2 * DB) // DC, (4 * DA + 2 * DB) // DC + 1
    ng = DC // GRP

    def body(u_ref, v_ref, doc_ref, w_ref, b_ref, g_ref, _, dz_ref, dw_ref, dbias_ref, dgain_ref, dbsum_scr):
        i = pl.program_id(0)

        @pl.when(i == 0)
        def _():
            for ref in (dw_ref, dgain_ref, dbsum_scr):
                ref[...] = jnp.zeros_like(ref)

        gmat = _group_matrix(DC, 1.0 / GRP).astype(MXU)
        for ch in range(tb // CCH):
            rows = slice(ch * CCH, (ch + 1) * CCH)
            uu, duu, dvv, rstd_v, vn, zz, ws, col, causal = _sgu_chunk(
                u_ref[rows, :], v_ref[rows, :], w_ref, b_ref[...], gmat)
            y = uu * zz
            rstd = lax.rsqrt(_mm_exact_r(y * y, gmat) + EPS)
            n = y * rstd
            d_oc = doc_ref[rows, :]
            dn = d_oc * g_ref[...]
            dgain_ref[0:1, :] += jnp.sum(d_oc * n, axis=0, keepdims=True)
            dy = rstd * (dn - n * _mm_exact_r(dn * n, gmat))
            dzz = dy * uu
            dz_ref[rows, :DC] = (dy * zz * duu).astype(dz_ref.dtype)
            dbsum_scr[...] += dzz
            dvn = jnp.zeros_like(dzz)
            for g in range(ng):
                sel = col == g
                dvn = dvn + jnp.where(sel, _mm_tn(ws[g], dzz), 0.0)
                dw_ref[g] += jnp.where(causal, _mm_nt(jnp.where(sel, dzz, 0.0), vn), 0.0)
            dv = rstd_v * (dvn - _mm_exact_r(dvn, gmat) - vn * _mm_exact_r(dvn * vn, gmat))
            dz_ref[rows, DC:] = (dv * dvv).astype(dz_ref.dtype)

        @pl.when(i == nb - 1)
        def _():
            dbias_ref[...] = _mm_exact_r(dbsum_scr[...], _group_matrix(DC, 1.0).astype(MXU))

    const = lambda i: (0, 0)
    return pl.pallas_call(
        body, name="sgu_bwd", grid=(nb,),
        in_specs=[pl.BlockSpec((tb, DC), lambda i: (i, ucol)),
                  pl.BlockSpec((tb, DC), lambda i: (i, vcol)),
                  pl.BlockSpec((tb, DC), lambda i: (i, 0)),
                  pl.BlockSpec((ng, CCH, CCH), lambda i: (0, 0, 0)),
                  pl.BlockSpec((CCH, DC), const), pl.BlockSpec((1, DC), const),
                  pl.BlockSpec(memory_space=pl.ANY)],
        out_specs=[pl.BlockSpec((tb, 2 * DC), lambda i: (i, (4 * DA + 2 * DB) // (2 * DC))),
                   pl.BlockSpec((ng, CCH, CCH), lambda i: (0, 0, 0)),
                   pl.BlockSpec((CCH, DC), const), pl.BlockSpec((8, DC), const)],
        out_shape=[jax.ShapeDtypeStruct((t, DIN), SAVE), jax.ShapeDtypeStruct((ng, CCH, CCH), F32),
                   jax.ShapeDtypeStruct((CCH, DC), F32), jax.ShapeDtypeStruct((8, DC), F32)],
        scratch_shapes=[pltpu.VMEM((CCH, DC), F32)],
        input_output_aliases={6: 0},
        compiler_params=_params(("arbitrary",)),
    )(z, z, doc, w, bias, gain, dz)


def _head(h, gain, target, tm):
    t = h.shape[0]

    def body(h_ref, g_ref, t_ref, dh_ref, loss_ref, dgain_ref):
        @pl.when(pl.program_id(0) == 0)
        def _():
            loss_ref[...] = jnp.zeros_like(loss_ref)
            dgain_ref[...] = jnp.zeros_like(dgain_ref)

        hh = h_ref[...]
        gain = g_ref[...]
        rstd = lax.rsqrt(jnp.mean(hh * hh, axis=-1, keepdims=True) + EPS)
        xhat = hh * rstd
        err = xhat * gain - t_ref[...]
        per_tok = jnp.mean(err * err, axis=-1, keepdims=True)
        loss_ref[...] += 0.5 * jnp.sum(per_tok, axis=0, keepdims=True)
        dy = err * (1.0 / D)
        dgain_ref[...] += jnp.sum(dy * xhat, axis=0, keepdims=True)
        dxh = dy * gain
        dh_ref[...] = rstd * (dxh - xhat * jnp.mean(dxh * xhat, axis=-1, keepdims=True))

    return pl.pallas_call(
        body, name="head", grid=(t // tm,),
        in_specs=[pl.BlockSpec((tm, D), lambda i: (i, 0)),
                  pl.BlockSpec((1, D), lambda i: (0, 0)),
                  pl.BlockSpec((tm, D), lambda i: (i, 0))],
        out_specs=[pl.BlockSpec((tm, D), lambda i: (i, 0)),
                   pl.BlockSpec((1, 128), lambda i: (0, 0)),
                   pl.BlockSpec((1, D), lambda i: (0, 0))],
        out_shape=[jax.ShapeDtypeStruct((t, D), F32), jax.ShapeDtypeStruct((1, 128), F32),
                   jax.ShapeDtypeStruct((1, D), F32)],
        compiler_params=_params(("arbitrary",)),
    )(h, gain, target)


def _adamw(w, g, m, v):
    m = ADAM_B1 * m + (1.0 - ADAM_B1) * g
    v = ADAM_B2 * v + (1.0 - ADAM_B2) * (g * g)
    m_hat = m / (1.0 - ADAM_B1 ** ADAM_STEP)
    v_hat = v / (1.0 - ADAM_B2 ** ADAM_STEP)
    delta = -ADAM_LR * (m_hat / (jnp.sqrt(v_hat) + ADAM_EPS) + ADAM_WD * w)
    return delta, m, v


def _adamw_big(recv, w, m, v, tr, row0, name, after):
    depth, rows, cols = w.shape
    off = row0 // tr

    def body(*refs):
        r_refs = refs[:depth]
        w_ref, m_ref, v_ref, _, g_out, d_out, m_out, v_out = refs[depth:]
        for l in range(depth):
            g = r_refs[l][0].astype(F32)
            for k in range(1, NDEV):
                g = g + r_refs[l][k].astype(F32)
            delta, m_, v_ = _adamw(w_ref[l], g, m_ref[l], v_ref[l])
            g_out[l] = g
            d_out[l] = delta
            m_out[l] = m_
            v_out[l] = v_

    spec = pl.BlockSpec((depth, tr, cols), lambda i: (0, i, 0))
    return pl.pallas_call(
        body, name=name, grid=(rows // tr,),
        in_specs=[pl.BlockSpec((NDEV, tr, cols), lambda i: (0, i + off, 0))] * depth + [spec] * 3
        + [pl.BlockSpec(memory_space=pl.ANY)],
        out_specs=[spec] * 4, out_shape=[jax.ShapeDtypeStruct((depth, rows, cols), F32)] * 4,
        compiler_params=_params(("parallel",)),
    )(*recv, w, m, v, after)


def _sum_devices(recv):
    _, r, _ = recv.shape

    def body(r_ref, out_ref):
        g = r_ref[0]
        for k in range(1, NDEV):
            g = g + r_ref[k]
        out_ref[...] = g

    return pl.pallas_call(body, name="sum_devices", out_shape=jax.ShapeDtypeStruct((r, 128), F32))(recv)


def _adamw_small(w, g, m, v):
    def body(w_ref, g_ref, m_ref, v_ref, d_out, m_out, v_out):
        delta, m_, v_ = _adamw(w_ref[...], g_ref[...], m_ref[...], v_ref[...])
        d_out[...] = delta
        m_out[...] = m_
        v_out[...] = v_

    return pl.pallas_call(body, name="adamw_small", out_shape=[jax.ShapeDtypeStruct(w.shape, F32)] * 3)(w, g, m, v)


def _pack(arrs):
    flat = jnp.concatenate([a.reshape(-1) for a in arrs])
    pad = (-flat.shape[0]) % 1024
    return jnp.pad(flat, (0, pad)).reshape(-1, 128)


def _unpack(buf, like):
    flat = buf.reshape(-1)
    out, off = [], 0
    for a in like:
        out.append(flat[off:off + a.size].reshape(a.shape))
        off += a.size
    return out


def _block_diag(w):
    nb, bd, _ = w.shape
    eye = jnp.eye(nb, dtype=w.dtype)
    return (eye[:, None, :, None] * w[:, :, None, :]).reshape(nb * bd, nb * bd)


def _diag_blocks(w):
    nb = w.shape[0] // GRP
    return jnp.stack([w[g * GRP:(g + 1) * GRP, g * GRP:(g + 1) * GRP] for g in range(nb)])


SMALL = ['ffn1_norm', 'mix_norm', 'hgrn_lb_logits', 'hgrn_norm', 'conv_b', 'lru_wa', 'lru_ba', 'lru_wx', 'lru_bx',
         'lru_lambda', 'lru_norm', 'sgu_w', 'sgu_b', 'sgu_norm', 'ffn2_norm', 'final_norm']
NAMES = ['ffn1_norm', 'ffn1_wg', 'ffn1_wu', 'ffn1_wd', 'mix_norm', 'w_in', 'hgrn_lb_logits', 'hgrn_norm', 'conv_w',
         'conv_b', 'lru_wa', 'lru_ba', 'lru_wx', 'lru_bx', 'lru_lambda', 'lru_norm', 'sgu_w', 'sgu_b', 'sgu_norm',
         'w_out', 'ffn2_norm', 'ffn2_wg', 'ffn2_wu', 'ffn2_wd', 'final_norm']


def _step(x, target, w, m, v):
    depth = w['ffn1_wg'].shape[0]
    t = x.shape[1]
    h = x.reshape(t, D)
    target = target.reshape(t, D)
    tm_f, tm_b, tb = min(TM_F, t), min(TM_B, t), min(TB, t)
    my = 4 * lax.axis_index("x") + 2 * lax.axis_index("y") + lax.axis_index("c")

    cw_tile = jnp.pad(w['conv_w'].reshape(-1, 128), ((0, 8 - depth), (0, 0)))
    cw_all = _all_gather([cw_tile], "gather_conv")[0][:, :depth]
    conv_w = jnp.moveaxis(cw_all.reshape(NDEV, depth, 4, DB // NDEV), 0, 2).reshape(depth, 4, DB)
    lbs, lb_soft = _lower_bounds(w['hgrn_lb_logits'])

    def row(a):
        return a.reshape(1, -1)

    def tr(a):
        return jnp.swapaxes(a, -1, -2)

    def shards(l, unit):
        if unit == 1:
            return [w['w_in'][l].astype(WIRE), w['w_out'][l].astype(WIRE)]
        f = 'ffn1' if unit == 0 else 'ffn2'
        return [tr(w[f + '_wg'][l]).astype(WIRE), tr(w[f + '_wu'][l]).astype(WIRE), w[f + '_wd'][l].astype(WIRE)]

    units = [(l, u) for l in range(depth) for u in range(3)]

    def start_ici(idx, deps=()):
        return _transfer_start(shards(*units[idx]), True, "gather_ici_%d_%d" % units[idx], deps=deps)

    def start_d2d(idx, handle, after):
        lands = _transfer_wait(handle, after, "gather_ici_wait_%d_%d" % units[idx])
        return _forward_start(lands, "gather_d2d_%d_%d" % units[idx])

    pipe = dict(idx=0)
    pipe['d2d'] = start_d2d(0, start_ici(0), h)
    pipe['ici'] = start_ici(1, deps=(pipe['d2d']['token'],))

    def next_weights(after):
        idx = pipe['idx']
        lands = _transfer_wait(pipe['d2d'], after, "gather_d2d_wait_%d_%d" % units[idx])
        pipe['idx'] = idx + 1
        tok = 0.0
        if idx + 1 < len(units):
            pipe['d2d'] = start_d2d(idx + 1, pipe['ici'], lands[-1])
            tok = pipe['d2d']['token'][0, 0]
            if idx + 2 < len(units):
                pipe['ici'] = start_ici(idx + 2, deps=(pipe['d2d']['token'],))
                tok = pipe['ici']['token'][0, 0]
        return lands, tok

    saved = []
    for l in range(depth):
        lands, tok = next_weights(h)
        s = dict(ffn1=[a.reshape(FF, D) for a in lands], h0=h)
        h, s['xn1'], s['a1'], s['b1'] = _ffn_fwd(h, row(w['ffn1_norm'][l]) + tok, *s['ffn1'], tm_f)
        s['h1'] = h
        (win, wout), tok = next_weights(h)
        wout = wout.reshape(D, D)
        s['win'], s['wout'] = win, wout
        z, s['xnm'] = _inproj_fwd(h, row(w['mix_norm'][l]) + tok, win, tm_f)
        s['z'] = z
        s['o'], oa, s['states'] = _hgrn_fwd(z, row(lbs[l]), row(w['hgrn_norm'][l]))
        s['cw'] = jnp.pad(conv_w[l], ((0, 4), (0, 0)))
        s['vec'] = jnp.concatenate([row(w['conv_b'][l]), row(w['lru_ba'][l]), row(w['lru_bx'][l]),
                                    row(w['lru_lambda'][l]), row(w['lru_norm'][l]), jnp.zeros((3, DB), F32)])
        s['wa'], s['wx'] = _block_diag(w['lru_wa'][l]), _block_diag(w['lru_wx'][l])
        ob, s['hseq'] = _lru_fwd(z, s['cw'], s['vec'], s['wa'], s['wx'], tb)
        s['bias'] = jnp.repeat(w['sgu_b'][l].T, GRP, axis=1)
        oc = _sgu_fwd(z, w['sgu_w'][l], s['bias'], row(w['sgu_norm'][l]), tb)
        s['oa'], s['ob'], s['oc'] = oa, ob, oc
        h = _outproj_fwd(h, oa, ob, oc, wout, tm_f)
        s['h2'] = h
        lands, tok = next_weights(h)
        s['ffn2'] = [a.reshape(FF, D) for a in lands]
        h, s['xn2'], s['a2'], s['b2'] = _ffn_fwd(h, row(w['ffn2_norm'][l]) + tok, *s['ffn2'], tm_f)
        saved.append(s)

    dh, loss_part, g_final = _head(h, row(w['final_norm']), target, tm_f)
    loss = lax.psum(loss_part[0, 0], ("x", "y", "c"))

    recv = {k: [None] * depth for k in ('wg1', 'wu1', 'wd1', 'wg2', 'wu2', 'wd2', 'win', 'wout')}
    flight = []

    def land(after):
        handle, kinds, l = flight.pop()
        for k, a in zip(kinds, _transfer_wait(handle, after, f"exchange_wait_{kinds[0]}_{l}")):
            recv[k][l] = a

    def exchange(arrs, kinds, l):
        handle = _transfer_start(arrs, False, f"exchange_start_{kinds[0]}_{l}")
        if flight:
            land(handle['token'])
        flight.append((handle, kinds, l))
        return handle['token'][0, 0]

    small = {k: [None] * depth for k in SMALL if k != 'final_norm'}
    dconv = [None] * depth
    dlb = [None] * depth
    tok = 0.0
    for l in reversed(range(depth)):
        s = saved[l]
        dh, g, *cot = _ffn_bwd_x(dh, s['h2'], row(w['ffn2_norm'][l]) + tok, s['a2'], s['b2'], *s['ffn2'], tm_f)
        dws = _ffn_bwd_w(s['xn2'], *cot, tm_f)
        tok = exchange([a.reshape(NDEV, FFS, D) for a in dws], ('wg2', 'wu2', 'wd2'), l)
        small['ffn2_norm'][l] = g
        doa, dob, doc, dwout = _outproj_bwd(dh, s['oa'], s['ob'], s['oc'], s['wout'], tm_f)
        dz, g_hn, dlb[l] = _hgrn_bwd(s['z'], row(lbs[l]), row(w['hgrn_norm'][l]) + tok, s['o'], s['states'], doa)
        small['hgrn_norm'][l] = g_hn
        dz, dcw, dvec, dwa, dwx = _lru_bwd(s['z'], s['hseq'], dob, s['cw'], s['vec'], s['wa'], s['wx'], dz, tb)
        dconv[l] = dcw[:4]
        small['conv_b'][l], small['lru_ba'][l], small['lru_bx'][l] = dvec[0], dvec[1].reshape(4, GRP), dvec[2].reshape(4, GRP)
        small['lru_lambda'][l], small['lru_norm'][l] = dvec[3], dvec[4]
        small['lru_wa'][l], small['lru_wx'][l] = _diag_blocks(dwa), _diag_blocks(dwx)
        dz, dsw, dbias, dgc = _sgu_bwd(s['z'], doc, w['sgu_w'][l], s['bias'], row(w['sgu_norm'][l]), dz, tb)
        small['sgu_w'][l], small['sgu_b'][l], small['sgu_norm'][l] = dsw, dbias[:, ::GRP].T, dgc[0]
        dwin = _inproj_bwd_w(s['xnm'], dz, tm_f)
        tok = exchange([dwin, dwout.reshape(NDEV, D // NDEV, D)], ('win', 'wout'), l)
        dh, g = _inproj_bwd_x(dh, dz, s['h1'], row(w['mix_norm'][l]) + tok, s['win'], tm_b)
        small['mix_norm'][l] = g
        dh, g, *cot = _ffn_bwd_x(dh, s['h0'], row(w['ffn1_norm'][l]), s['a1'], s['b1'], *s['ffn1'], tm_f)
        dws = _ffn_bwd_w(s['xn1'], *cot, tm_f)
        tok = exchange([a.reshape(NDEV, FFS, D) for a in dws], ('wg1', 'wu1', 'wd1'), l)
        small['ffn1_norm'][l] = g
    grad_x = dh.reshape(1, t, D)
    small['hgrn_lb_logits'] = list(_lower_bounds_bwd(lb_soft, jnp.concatenate(dlb, axis=0)))

    out = {}
    last = flight[0][0]['token']

    def ffn_update(f, n, after):
        for kind in ('wg', 'wu'):
            k = f + '_' + kind
            res = _adamw_big(recv[kind + n], tr(w[k]), tr(m[k]), tr(v[k]), 32, 0, "adamw_ffn", after)
            out[k] = tuple(tr(a) for a in res)
        k = f + '_wd'
        out[k] = _adamw_big(recv['wd' + n], w[k], m[k], v[k], 32, 0, "adamw_ffn", after)

    ffn_update('ffn2', '2', last)
    out['w_in'] = _adamw_big(recv['win'], w['w_in'], m['w_in'], v['w_in'], 64, 0, "adamw_win", last)
    out['w_out'] = _adamw_big(recv['wout'], w['w_out'], m['w_out'], v['w_out'], 64, 0, "adamw_wout", last)

    parts = [jnp.stack([small[k][l].reshape(w[k].shape[1:]) for l in range(depth)]) for k in SMALL if k != 'final_norm']
    parts += [g_final.reshape(D), jnp.stack(dconv)]
    total = _sum_devices(_all_gather([_pack(parts)], "gather_small")[0])
    like = [w[k] for k in SMALL] + [jax.ShapeDtypeStruct((depth, 4, DB), F32)]
    grads = _unpack(total, like)
    gsmall = dict(zip(SMALL, grads[:-1]))
    gsmall['conv_w'] = lax.dynamic_slice_in_dim(grads[-1], my * (DB // NDEV), DB // NDEV, axis=2)
    keys = SMALL + ['conv_w']
    dl, mm, vv = _adamw_small(_pack([w[k] for k in keys]), _pack([gsmall[k] for k in keys]),
                              _pack([m[k] for k in keys]), _pack([v[k] for k in keys]))
    like = [w[k] for k in keys]
    for k, d_, m_, v_ in zip(keys, _unpack(dl, like), _unpack(mm, like), _unpack(vv, like)):
        out[k] = (gsmall[k], d_, m_, v_)
    done = [dl] + [out[k][1][0] for k in ('ffn2_wg', 'ffn2_wu', 'ffn2_wd', 'w_in', 'w_out')]
    land(functools.reduce(lambda p, q: p + q, [a[:1, :1] for a in done]))
    ffn_update('ffn1', '1', last)

    return (loss, grad_x, *[out[k][0] for k in NAMES], *[out[k][1] for k in NAMES],
            *[out[k][2] for k in NAMES], *[out[k][3] for k in NAMES])


def kernel(x, ffn1_norm, ffn1_wg, ffn1_wu, ffn1_wd, mix_norm, w_in, hgrn_lb_logits, hgrn_norm, conv_w, conv_b, lru_wa, lru_ba, lru_wx, lru_bx, lru_lambda, lru_norm, sgu_w, sgu_b, sgu_norm, w_out, ffn2_norm, ffn2_wg, ffn2_wu, ffn2_wd, final_norm, loss_target, m_ffn1_norm, m_ffn1_wg, m_ffn1_wu, m_ffn1_wd, m_mix_norm, m_w_in, m_hgrn_lb_logits, m_hgrn_norm, m_conv_w, m_conv_b, m_lru_wa, m_lru_ba, m_lru_wx, m_lru_bx, m_lru_lambda, m_lru_norm, m_sgu_w, m_sgu_b, m_sgu_norm, m_w_out, m_ffn2_norm, m_ffn2_wg, m_ffn2_wu, m_ffn2_wd, m_final_norm, v_ffn1_norm, v_ffn1_wg, v_ffn1_wu, v_ffn1_wd, v_mix_norm, v_w_in, v_hgrn_lb_logits, v_hgrn_norm, v_conv_w, v_conv_b, v_lru_wa, v_lru_ba, v_lru_wx, v_lru_bx, v_lru_lambda, v_lru_norm, v_sgu_w, v_sgu_b, v_sgu_norm, v_w_out, v_ffn2_norm, v_ffn2_wg, v_ffn2_wu, v_ffn2_wd, v_final_norm):
    args = locals()
    w = {k: args[k] for k in NAMES}
    m = {k: args['m_' + k] for k in NAMES}
    v = {k: args['v_' + k] for k in NAMES}
    return _step(x, loss_target, w, m, v)
```

```python
import functools

import jax
import jax.numpy as jnp
from jax import lax
from jax.experimental import pallas as pl
from jax.experimental.pallas import tpu as pltpu

F32 = jnp.float32
MXU = jnp.bfloat16
SAVE = jnp.bfloat16
WIRE = jnp.bfloat16

NDEV = 8
D = 1024
FF = 2816
FFS = FF // NDEV
FB = 256
DIN = 3072
DINS = DIN // NDEV
ZB = 512
DA, DB, DC = 512, 256, 256
HD = 128
NH = DA // HD
ACH = 64
ACB = 4
CCH = 128
GRP = 64
EPS = 1e-6
LRU_C = 8.0
VMEM_LIMIT = 60 * 1024 * 1024
TM_F = 1024
TM_B = 512
TB = 512
SUB = 256

ADAM_LR, ADAM_B1, ADAM_B2, ADAM_EPS, ADAM_WD, ADAM_STEP = 0.001, 0.9, 0.999, 1e-08, 0.01, 10

MESH = pl.DeviceIdType.MESH


def _mm(a, b):
    return jnp.dot(a.astype(MXU), b.astype(MXU), preferred_element_type=F32)


def _mm_nt(a, b):
    return lax.dot_general(a.astype(MXU), b.astype(MXU), (((1,), (1,)), ((), ())), preferred_element_type=F32)


def _mm_tn(a, b):
    return lax.dot_general(a.astype(MXU), b.astype(MXU), (((0,), (0,)), ((), ())), preferred_element_type=F32)


def _split3(x):
    x1 = x.astype(MXU)
    r1 = x - x1.astype(F32)
    x2 = r1.astype(MXU)
    r2 = r1 - x2.astype(F32)
    return x1, x2, r2.astype(MXU)


def _mm_exact_l(c, x):
    x1, x2, x3 = _split3(x)
    return _mm(c, x1) + _mm(c, x2) + _mm(c, x3)


def _mm_exact_r(x, c):
    x1, x2, x3 = _split3(x)
    return _mm(x1, c) + _mm(x2, c) + _mm(x3, c)


def _sigmoid(x):
    return 1.0 / (1.0 + jnp.exp(-x))


def _gelu(x):
    c, k = 0.7978845608028654, 0.044715
    th = jnp.tanh(c * (x + k * x * x * x))
    return 0.5 * x * (1.0 + th)


def _gelu_and_grad(x):
    c, k = 0.7978845608028654, 0.044715
    th = jnp.tanh(c * (x + k * x * x * x))
    g = 0.5 * x * (1.0 + th)
    dg = 0.5 * (1.0 + th) + 0.5 * x * (1.0 - th * th) * c * (1.0 + 3.0 * k * x * x)
    return g, dg


def _expm1(x):
    series = x * (1.0 + x * (0.5 + x * (1.0 / 6.0 + x * (1.0 / 24.0 + x * (1.0 / 120.0)))))
    return jnp.where(jnp.abs(x) < 0.05, series, jnp.exp(x) - 1.0)


def _iota(shape, dim):
    return lax.broadcasted_iota(jnp.int32, shape, dim)


def _tri(n, lower):
    r, c = _iota((n, n), 0), _iota((n, n), 1)
    return jnp.where((r >= c) if lower else (r <= c), 1.0, 0.0).astype(F32)


def _group_matrix(n, value):
    r, c = _iota((n, n), 0), _iota((n, n), 1)
    return jnp.where((r // GRP) == (c // GRP), value, 0.0).astype(F32)


def _row(x, k):
    r = _iota(x.shape, 0)
    return jnp.sum(jnp.where(r == k, x, 0.0), axis=0, keepdims=True)


def _rms_bwd(dxn, hh, gain):
    rstd = lax.rsqrt(jnp.mean(hh * hh, axis=-1, keepdims=True) + EPS)
    xhat = hh * rstd
    dxh = dxn * gain
    dh = rstd * (dxh - xhat * jnp.mean(dxh * xhat, axis=-1, keepdims=True))
    return dh, jnp.sum(dxn * xhat, axis=0, keepdims=True)


def _params(sem):
    return pltpu.CompilerParams(dimension_semantics=sem, vmem_limit_bytes=VMEM_LIMIT)


def _all_gather(arrs, name):
    n = len(arrs)

    def body(*refs):
        ins, outs = refs[:n], refs[n:2 * n]
        send_sems, recv_sems, local_sems = refs[2 * n:]
        x, y, c = lax.axis_index("x"), lax.axis_index("y"), lax.axis_index("c")
        me, sibling = (x, y, c), (x, y, 1 - c)
        chips = [(1 - x, y), (x, 1 - y), (1 - x, 1 - y)]

        def slot(px, py, pc):
            return 4 * px + 2 * py + pc

        def copy(a, k, block, to, src=None):
            dst = outs[a].at[slot(*block)]
            return pltpu.make_async_remote_copy(
                src_ref=dst if src is None else src, dst_ref=dst,
                send_sem=send_sems.at[a * 7 + k], recv_sem=recv_sems.at[a * 7 + k],
                device_id=to, device_id_type=MESH)

        started = []
        for a in range(n):
            mine = pltpu.make_async_copy(ins[a], outs[a].at[slot(*me)], local_sems.at[a])
            mine.start()
            started.append(mine)
        first = []
        for a in range(n):
            first.append(copy(a, 0, me, sibling, src=ins[a]))
            first += [copy(a, 1 + j, me, (*chip, c), src=ins[a]) for j, chip in enumerate(chips)]
        for cp in first:
            cp.start()
        passed = []
        for a in range(n):
            for j, chip in enumerate(chips):
                copy(a, 1 + j, (*chip, c), me).wait_recv()
                fwd = copy(a, 4 + j, (*chip, c), sibling)
                fwd.start()
                passed.append(fwd)
        for a in range(n):
            copy(a, 0, sibling, me).wait_recv()
            for j, chip in enumerate(chips):
                copy(a, 4 + j, (*chip, 1 - c), me).wait_recv()
        for cp in first + passed:
            cp.wait_send()
        for mine in started:
            mine.wait()

    hbm = pl.BlockSpec(memory_space=pl.ANY)
    return pl.pallas_call(
        body, name=name,
        out_shape=[jax.ShapeDtypeStruct((NDEV,) + a.shape, a.dtype) for a in arrs],
        in_specs=[hbm] * n, out_specs=[hbm] * n,
        scratch_shapes=[pltpu.SemaphoreType.DMA((7 * n,)), pltpu.SemaphoreType.DMA((7 * n,)),
                        pltpu.SemaphoreType.DMA((n,))],
    )(*arrs)


def _peers():
    x, y, c = lax.axis_index("x"), lax.axis_index("y"), lax.axis_index("c")
    peers = [(x ^ ((k >> 2) & 1), y ^ ((k >> 1) & 1), c ^ (k & 1)) for k in range(1, NDEV)]
    return (x, y, c), 4 * x + 2 * y + c, peers


_HBM = pl.BlockSpec(memory_space=pltpu.HBM)
_SEM = pl.BlockSpec(memory_space=pltpu.SEMAPHORE)
_EFFECT = pltpu.SideEffectType.DATAFLOW_SIDE_EFFECTING


def _transfer_start(arrs, gather, name, deps=(), direct=False):
    n, nd = len(arrs), len(deps)
    shapes = [((NDEV,) + a.shape) if gather else a.shape for a in arrs]

    def body(*refs):
        ins, lands = refs[:n], refs[n:2 * n]
        send_sems, recv_sems, local_sems = refs[2 * n + nd:2 * n + nd + 3]
        token = refs[-1]
        (x, y, c), my, peers = _peers()
        if gather and not direct:
            peers = [(x, y, 1 - c), (1 - x, y, c), (x, 1 - y, c), (1 - x, 1 - y, c)]
        for a in range(n):
            own = ins[a] if gather else ins[a].at[my]
            pltpu.make_async_copy(own, lands[a].at[my], local_sems.at[a]).start()
        for a in range(n):
            for peer in peers:
                src = ins[a] if gather else ins[a].at[4 * peer[0] + 2 * peer[1] + peer[2]]
                pltpu.make_async_remote_copy(
                    src_ref=src, dst_ref=lands[a].at[my], send_sem=send_sems.at[a], recv_sem=recv_sems.at[a],
                    device_id=peer, device_id_type=MESH).start()
        token[...] = jnp.zeros_like(token)

    out_shape = [pltpu.SemaphoreType.DMA((n,))] * 3
    out_shape += [pltpu.HBM(a.shape, a.dtype) for a in arrs]
    out_shape += [pltpu.HBM(s, a.dtype) for s, a in zip(shapes, arrs)]
    out_shape += [jax.ShapeDtypeStruct((8, 128), F32)]
    operands = [pltpu.with_memory_space_constraint(a, pltpu.HBM) for a in arrs]
    operands += [pltpu.with_memory_space_constraint(lax.empty(s, a.dtype), pltpu.HBM) for s, a in zip(shapes, arrs)]
    res = pl.pallas_call(
        body, name=name, out_shape=out_shape,
        in_specs=[_HBM] * (2 * n) + [pl.BlockSpec(memory_space=pl.ANY)] * nd,
        out_specs=[_SEM] * 3 + [_HBM] * (2 * n) + [pl.BlockSpec(memory_space=pltpu.VMEM)],
        input_output_aliases={i: 3 + i for i in range(2 * n)},
        compiler_params=pltpu.CompilerParams(has_side_effects=_EFFECT),
    )(*operands, *deps)
    return dict(sems=res[:3], src=res[3:3 + n], lands=res[3 + n:3 + 2 * n], token=res[-1], n=n,
                count=4 if gather and not direct else NDEV - 1)


def _forward_start(lands, name, deps=()):
    n, nd = len(lands), len(deps)

    def body(*refs):
        zone = refs[:n]
        send_sems, recv_sems = refs[n + nd:n + nd + 2]
        token = refs[-1]
        (x, y, c), _, _ = _peers()
        for a in range(n):
            for px, py in ((1 - x, y), (x, 1 - y), (1 - x, 1 - y)):
                block = zone[a].at[4 * px + 2 * py + c]
                pltpu.make_async_remote_copy(
                    src_ref=block, dst_ref=block, send_sem=send_sems.at[a], recv_sem=recv_sems.at[a],
                    device_id=(x, y, 1 - c), device_id_type=MESH).start()
        token[...] = jnp.zeros_like(token)

    res = pl.pallas_call(
        body, name=name,
        out_shape=[pltpu.SemaphoreType.DMA((n,))] * 2 + [pltpu.HBM(a.shape, a.dtype) for a in lands]
        + [jax.ShapeDtypeStruct((8, 128), F32)],
        in_specs=[_HBM] * n + [pl.BlockSpec(memory_space=pl.ANY)] * nd,
        out_specs=[_SEM] * 2 + [_HBM] * n + [pl.BlockSpec(memory_space=pltpu.VMEM)],
        input_output_aliases={i: 2 + i for i in range(n)},
        compiler_params=pltpu.CompilerParams(has_side_effects=_EFFECT),
    )(*lands, *deps)
    return dict(sems=res[:2], src=[], lands=res[2:2 + n], token=res[-1], n=n, count=3)


def _transfer_wait(handle, after, name):
    n, count = handle["n"], handle["count"]
    src, lands, sems = list(handle["src"]), list(handle["lands"]), list(handle["sems"])
    ns = len(src)

    def body(*refs):
        zone = refs[ns:ns + n]
        sem_refs = refs[ns + n:ns + n + len(sems)]
        me, _, _ = _peers()
        for a in range(n):
            moved = zone[a].at[pl.ds(0, count)]
            both = pltpu.make_async_remote_copy(
                src_ref=moved, dst_ref=moved, send_sem=sem_refs[0].at[a], recv_sem=sem_refs[1].at[a],
                device_id=me, device_id_type=MESH)
            both.wait_send()
            both.wait_recv()
            if len(sems) == 3:
                pltpu.make_async_copy(zone[a].at[0], zone[a].at[1], sem_refs[2].at[a]).wait()

    res = pl.pallas_call(
        body, name=name,
        out_shape=[pltpu.HBM(a.shape, a.dtype) for a in src + lands],
        in_specs=[_HBM] * (ns + n) + [_SEM] * len(sems) + [pl.BlockSpec(memory_space=pl.ANY)],
        out_specs=[_HBM] * (ns + n),
        input_output_aliases={i: i for i in range(ns + n)},
        compiler_params=pltpu.CompilerParams(has_side_effects=_EFFECT),
    )(*src, *lands, *sems, after)
    return list(res[ns:])


def _ffn_fwd(h, gain, wg, wu, wd, tm):
    t = h.shape[0]
    nj = FF // FB

    def body(h_ref, g_ref, wg_ref, wu_ref, wd_ref, out_ref, xn_ref, a_ref, b_ref, acc_ref):
        j = pl.program_id(1)

        @pl.when(j == 0)
        def _():
            hh = h_ref[...]
            rstd = lax.rsqrt(jnp.mean(hh * hh, axis=-1, keepdims=True) + EPS)
            xn_ref[...] = (hh * rstd * g_ref[...]).astype(xn_ref.dtype)
            acc_ref[...] = jnp.zeros_like(acc_ref)

        sub = min(SUB, tm)
        for r in range(tm // sub):
            rows = slice(r * sub, (r + 1) * sub)
            xn = xn_ref[rows, :]
            a = _mm_nt(xn, wg_ref[...])
            b = _mm_nt(xn, wu_ref[...])
            a_ref[rows, :] = a.astype(a_ref.dtype)
            b_ref[rows, :] = b.astype(b_ref.dtype)
            acc_ref[rows, :] += _mm(a * _sigmoid(a) * b, wd_ref[...])

        @pl.when(j == nj - 1)
        def _():
            out_ref[...] = h_ref[...] + 0.5 * acc_ref[...]

    wspec = pl.BlockSpec((FB, D), lambda i, j: (j, 0))
    return pl.pallas_call(
        body, name="ffn_fwd", grid=(t // tm, nj),
        in_specs=[pl.BlockSpec((tm, D), lambda i, j: (i, 0)),
                  pl.BlockSpec((1, D), lambda i, j: (0, 0)), wspec, wspec, wspec],
        out_specs=[pl.BlockSpec((tm, D), lambda i, j: (i, 0)),
                   pl.BlockSpec((tm, D), lambda i, j: (i, 0)),
                   pl.BlockSpec((tm, FB), lambda i, j: (i, j)),
                   pl.BlockSpec((tm, FB), lambda i, j: (i, j))],
        out_shape=[jax.ShapeDtypeStruct((t, D), F32), jax.ShapeDtypeStruct((t, D), SAVE),
                   jax.ShapeDtypeStruct((t, FF), SAVE), jax.ShapeDtypeStruct((t, FF), SAVE)],
        scratch_shapes=[pltpu.VMEM((tm, D), F32)],
        compiler_params=_params(("parallel", "arbitrary")),
    )(h, gain, wg, wu, wd)


def _ffn_bwd_x(dout, h, gain, a_sv, b_sv, wg, wu, wd, tm):
    t = h.shape[0]
    nj = FF // FB

    def body(dout_ref, h_ref, g_ref, a_ref, b_ref, wg_ref, wu_ref, wd_ref,
             dh_ref, dgain_ref, dy_ref, da_ref, db_ref, s_ref, acc_ref):
        i, j = pl.program_id(0), pl.program_id(1)

        @pl.when((i == 0) & (j == 0))
        def _():
            dgain_ref[...] = jnp.zeros_like(dgain_ref)

        @pl.when(j == 0)
        def _():
            dy_ref[...] = (0.5 * dout_ref[...]).astype(dy_ref.dtype)
            acc_ref[...] = jnp.zeros_like(acc_ref)

        sub = min(SUB, tm)
        for r in range(tm // sub):
            rows = slice(r * sub, (r + 1) * sub)
            ds = _mm_nt(dy_ref[rows, :], wd_ref[...])
            a, b = a_ref[rows, :].astype(F32), b_ref[rows, :].astype(F32)
            sg = _sigmoid(a)
            sa = a * sg
            da = (ds * b * (sg * (1.0 + a * (1.0 - sg)))).astype(MXU)
            db = (ds * sa).astype(MXU)
            da_ref[rows, :] = da.astype(da_ref.dtype)
            db_ref[rows, :] = db.astype(db_ref.dtype)
            s_ref[rows, :] = (sa * b).astype(s_ref.dtype)
            acc_ref[rows, :] += _mm(da, wg_ref[...]) + _mm(db, wu_ref[...])

        @pl.when(j == nj - 1)
        def _():
            dh, dg = _rms_bwd(acc_ref[...], h_ref[...], g_ref[...])
            dh_ref[...] = dout_ref[...] + dh
            dgain_ref[...] += dg

    tok = pl.BlockSpec((tm, D), lambda i, j: (i, 0))
    act = pl.BlockSpec((tm, FB), lambda i, j: (i, j))
    wspec = pl.BlockSpec((FB, D), lambda i, j: (j, 0))
    return pl.pallas_call(
        body, name="ffn_bwd_x", grid=(t // tm, nj),
        in_specs=[tok, tok, pl.BlockSpec((1, D), lambda i, j: (0, 0)), act, act, wspec, wspec, wspec],
        out_specs=[tok, pl.BlockSpec((1, D), lambda i, j: (0, 0)), tok, act, act, act],
        out_shape=[jax.ShapeDtypeStruct((t, D), F32), jax.ShapeDtypeStruct((1, D), F32),
                   jax.ShapeDtypeStruct((t, D), SAVE)] + [jax.ShapeDtypeStruct((t, FF), SAVE)] * 3,
        scratch_shapes=[pltpu.VMEM((tm, D), F32)],
        compiler_params=_params(("arbitrary", "arbitrary")),
    )(dout, h, gain, a_sv, b_sv, wg, wu, wd)


def _ffn_bwd_w(xn, dy, da, db, s, tm):
    t = xn.shape[0]
    nt = t // tm
    nj = FF // FB

    def body(xn_ref, dy_ref, da_ref, db_ref, s_ref, dwg_ref, dwu_ref, dwd_ref, ag_scr, au_scr, ad_scr):
        i, j = pl.program_id(0), pl.program_id(1)
        rows = pl.ds(pl.multiple_of(j * FB, FB), FB)
        xn = xn_ref[...]
        new = ((ag_scr, _mm_tn(da_ref[...], xn)), (au_scr, _mm_tn(db_ref[...], xn)),
               (ad_scr, _mm_tn(s_ref[...], dy_ref[...])))

        @pl.when(i == 0)
        def _():
            for ref, val in new:
                ref[rows, :] = val

        @pl.when(i > 0)
        def _():
            for ref, val in new:
                ref[rows, :] += val

        @pl.when(i == nt - 1)
        def _():
            for out, ref in ((dwg_ref, ag_scr), (dwu_ref, au_scr), (dwd_ref, ad_scr)):
                out[...] = ref[rows, :].astype(out.dtype)

    tok = pl.BlockSpec((tm, D), lambda i, j: (i, 0))
    act = pl.BlockSpec((tm, FB), lambda i, j: (i, j))
    wspec = pl.BlockSpec((FB, D), lambda i, j: (jnp.where(i == nt - 1, j, 0), 0))
    return pl.pallas_call(
        body, name="ffn_bwd_w", grid=(nt, nj),
        in_specs=[tok, tok, act, act, act], out_specs=[wspec] * 3,
        out_shape=[jax.ShapeDtypeStruct((FF, D), WIRE)] * 3,
        scratch_shapes=[pltpu.VMEM((FF, D), F32)] * 3,
        compiler_params=_params(("arbitrary", "arbitrary")),
    )(xn, dy, da, db, s)


def _inproj_fwd(h, gain, win, tm):
    t = h.shape[0]

    def body(h_ref, g_ref, w_ref, z_ref, xn_ref):
        hh = h_ref[...]
        rstd = lax.rsqrt(jnp.mean(hh * hh, axis=-1, keepdims=True) + EPS)
        xn = (hh * rstd * g_ref[...]).astype(MXU)
        xn_ref[...] = xn.astype(xn_ref.dtype)
        for j in range(DIN // ZB):
            z_ref[:, j * ZB:(j + 1) * ZB] = _mm_nt(xn, w_ref[j * ZB:(j + 1) * ZB, :])

    return pl.pallas_call(
        body, name="inproj_fwd", grid=(t // tm,),
        in_specs=[pl.BlockSpec((tm, D), lambda i: (i, 0)),
                  pl.BlockSpec((1, D), lambda i: (0, 0)),
                  pl.BlockSpec((DIN, D), lambda i: (0, 0))],
        out_specs=[pl.BlockSpec((tm, DIN), lambda i: (i, 0)),
                   pl.BlockSpec((tm, D), lambda i: (i, 0))],
        out_shape=[jax.ShapeDtypeStruct((t, DIN), F32), jax.ShapeDtypeStruct((t, D), SAVE)],
        compiler_params=_params(("parallel",)),
    )(h, gain, win)


def _inproj_bwd_x(dres, dz, h, gain, win, tm):
    t = h.shape[0]

    def body(dres_ref, dz_ref, h_ref, g_ref, w_ref, dh_ref, dgain_ref):
        @pl.when(pl.program_id(0) == 0)
        def _():
            dgain_ref[...] = jnp.zeros_like(dgain_ref)

        dh, dg = _rms_bwd(_mm(dz_ref[...], w_ref[...]), h_ref[...], g_ref[...])
        dh_ref[...] = dres_ref[...] + dh
        dgain_ref[...] += dg

    return pl.pallas_call(
        body, name="inproj_bwd_x", grid=(t // tm,),
        in_specs=[pl.BlockSpec((tm, D), lambda i: (i, 0)),
                  pl.BlockSpec((tm, DIN), lambda i: (i, 0)),
                  pl.BlockSpec((tm, D), lambda i: (i, 0)),
                  pl.BlockSpec((1, D), lambda i: (0, 0)),
                  pl.BlockSpec((DIN, D), lambda i: (0, 0))],
        out_specs=[pl.BlockSpec((tm, D), lambda i: (i, 0)),
                   pl.BlockSpec((1, D), lambda i: (0, 0))],
        out_shape=[jax.ShapeDtypeStruct((t, D), F32), jax.ShapeDtypeStruct((1, D), F32)],
        compiler_params=_params(("arbitrary",)),
    )(dres, dz, h, gain, win)


def _inproj_bwd_w(xn, dz, tm):
    t = xn.shape[0]
    nt = t // tm

    def body(xn_ref, dz_ref, dw_ref, acc_scr):
        i = pl.program_id(1)

        @pl.when(i == 0)
        def _():
            acc_scr[...] = jnp.zeros_like(acc_scr)

        acc_scr[...] += _mm_tn(dz_ref[...], xn_ref[...])

        @pl.when(i == nt - 1)
        def _():
            dw_ref[...] = acc_scr[...].astype(dw_ref.dtype)

    return pl.pallas_call(
        body, name="inproj_bwd_w", grid=(DIN // ZB, nt),
        in_specs=[pl.BlockSpec((tm, D), lambda j, i: (i, 0)),
                  pl.BlockSpec((tm, ZB), lambda j, i: (i, j))],
        out_specs=pl.BlockSpec((ZB, D), lambda j, i: (j, 0)),
        out_shape=jax.ShapeDtypeStruct((DIN, D), WIRE),
        scratch_shapes=[pltpu.VMEM((ZB, D), F32)],
        compiler_params=_params(("parallel", "arbitrary")),
    )(xn, dz)


def _outproj_fwd(h, oa, ob, oc, wout, tm):
    t = h.shape[0]

    def body(h_ref, oa_ref, ob_ref, oc_ref, w_ref, out_ref):
        ym = jnp.concatenate([oa_ref[...], ob_ref[...], oc_ref[...]], axis=1)
        out_ref[...] = h_ref[...] + _mm(ym, w_ref[...])

    return pl.pallas_call(
        body, name="outproj_fwd", grid=(t // tm,),
        in_specs=[pl.BlockSpec((tm, D), lambda i: (i, 0)),
                  pl.BlockSpec((tm, DA), lambda i: (i, 0)),
                  pl.BlockSpec((tm, DB), lambda i: (i, 0)),
                  pl.BlockSpec((tm, DC), lambda i: (i, 0)),
                  pl.BlockSpec((D, D), lambda i: (0, 0))],
        out_specs=pl.BlockSpec((tm, D), lambda i: (i, 0)),
        out_shape=jax.ShapeDtypeStruct((t, D), F32),
        compiler_params=_params(("parallel",)),
    )(h, oa, ob, oc, wout)


def _outproj_bwd(dh, oa, ob, oc, wout, tm):
    t = dh.shape[0]
    nt = t // tm

    def body(dh_ref, oa_ref, ob_ref, oc_ref, w_ref, da_ref, db_ref, dc_ref, dw_ref, acc_scr):
        i = pl.program_id(0)

        @pl.when(i == 0)
        def _():
            acc_scr[...] = jnp.zeros_like(acc_scr)

        d16 = dh_ref[...].astype(MXU)
        dym = _mm_nt(d16, w_ref[...])
        da_ref[...] = dym[:, :DA]
        db_ref[...] = dym[:, DA:DA + DB]
        dc_ref[...] = dym[:, DA + DB:]
        ym = jnp.concatenate([oa_ref[...], ob_ref[...], oc_ref[...]], axis=1)
        acc_scr[...] += _mm_tn(ym, d16)

        @pl.when(i == nt - 1)
        def _():
            dw_ref[...] = acc_scr[...].astype(dw_ref.dtype)

    return pl.pallas_call(
        body, name="outproj_bwd", grid=(nt,),
        in_specs=[pl.BlockSpec((tm, D), lambda i: (i, 0)),
                  pl.BlockSpec((tm, DA), lambda i: (i, 0)),
                  pl.BlockSpec((tm, DB), lambda i: (i, 0)),
                  pl.BlockSpec((tm, DC), lambda i: (i, 0)),
                  pl.BlockSpec((D, D), lambda i: (0, 0))],
        out_specs=[pl.BlockSpec((tm, DA), lambda i: (i, 0)),
                   pl.BlockSpec((tm, DB), lambda i: (i, 0)),
                   pl.BlockSpec((tm, DC), lambda i: (i, 0)),
                   pl.BlockSpec((D, D), lambda i: (0, 0))],
        out_shape=[jax.ShapeDtypeStruct((t, DA), F32), jax.ShapeDtypeStruct((t, DB), F32),
                   jax.ShapeDtypeStruct((t, DC), F32), jax.ShapeDtypeStruct((D, D), WIRE)],
        scratch_shapes=[pltpu.VMEM((D, D), F32)],
        compiler_params=_params(("arbitrary",)),
    )(dh, oa, ob, oc, wout)


def _lower_bounds(logits):
    depth, n = logits.shape

    def body(l_ref, lb_ref, p_ref):
        rows = [l_ref[l:l + 1, :] for l in range(depth)]
        mx = functools.reduce(jnp.maximum, rows)
        ex = [jnp.exp(r - mx) for r in rows]
        den = functools.reduce(lambda u, v: u + v, ex)
        acc = jnp.zeros_like(den)
        for l in range(depth):
            p = ex[l] / den
            p_ref[l:l + 1, :] = p
            if l > 0:
                acc = acc + p
            lb_ref[l:l + 1, :] = acc

    return pl.pallas_call(
        body, name="lower_bounds",
        out_shape=[jax.ShapeDtypeStruct((depth, n), F32), jax.ShapeDtypeStruct((depth, n), F32)],
    )(logits)


def _lower_bounds_bwd(p, dlb):
    depth, n = p.shape

    def body(p_ref, d_ref, out_ref):
        ps = [p_ref[l:l + 1, :] for l in range(depth)]
        ds = [d_ref[l:l + 1, :] for l in range(depth)]
        dp = [jnp.zeros_like(ps[0]) for _ in range(depth)]
        run = jnp.zeros_like(ps[0])
        for l in range(depth - 1, 0, -1):
            run = run + ds[l]
            dp[l] = run
        dot = functools.reduce(lambda u, v: u + v, [ps[l] * dp[l] for l in range(depth)])
        for l in range(depth):
            out_ref[l:l + 1, :] = ps[l] * (dp[l] - dot)

    return pl.pallas_call(body, name="lower_bounds_bwd", out_shape=jax.ShapeDtypeStruct((depth, n), F32))(p, dlb)


def _hgrn_chunk(z_ref, lb_ref, hd, rows):
    c0 = hd * HD
    q = z_ref[rows, c0:c0 + HD]
    fl = z_ref[rows, DA + c0:DA + c0 + HD]
    v = z_ref[rows, 2 * DA + c0:2 * DA + c0 + HD]
    g = z_ref[rows, 3 * DA + c0:3 * DA + c0 + HD]
    lb = lb_ref[:, c0:c0 + HD]
    sq = _sigmoid(q)
    qs = q * sq
    sg = _sigmoid(fl)
    f = lb + (1.0 - lb) * sg
    k = 1.0 - f
    lf = jnp.log(f)
    b = _mm_exact_l(_tri(ACH, True).astype(MXU), lf)
    bend = jnp.sum(lf, axis=0, keepdims=True)
    r = 0.5 * bend
    eq, ek, eb, ed = jnp.exp(b - r), jnp.exp(r - b), jnp.exp(b), jnp.exp(bend - b)
    qt, kt, qe, kd = qs * eq, k * ek, qs * eb, k * ed
    causal = _iota((ACH, ACH), 0) >= _iota((ACH, ACH), 1)
    att = jnp.where(causal, _mm_nt(qt, kt), 0.0)
    return dict(q=q, v=v, g=g, lb=lb, sq=sq, qs=qs, sg=sg, f=f, k=k, bend=bend, eq=eq, ek=ek, eb=eb, ed=ed,
                qt=qt, kt=kt, qe=qe, kd=kd, att=att, causal=causal)


def _hgrn_fwd(z, lb, gain):
    t = z.shape[0]
    nc = t // ACH
    cb = min(ACB, nc)
    rb = cb * ACH

    def body(z_ref, lb_ref, g_ref, o_ref, oa_ref, st_ref, st_scr):
        @pl.when(pl.program_id(0) == 0)
        def _():
            st_scr[...] = jnp.zeros_like(st_scr)

        for hd in range(NH):
            st = st_scr[hd]
            cols = slice(hd * HD, (hd + 1) * HD)
            for cc in range(cb):
                rows = slice(cc * ACH, (cc + 1) * ACH)
                c = _hgrn_chunk(z_ref, lb_ref, hd, rows)
                st_ref[cc, hd] = st
                o = _mm(c["att"], c["v"]) + _mm_nt(c["qe"], st)
                st = st * jnp.exp(c["bend"]) + _mm_tn(c["v"], c["kd"])
                o_ref[rows, cols] = o
                rstd = lax.rsqrt(jnp.mean(o * o, axis=-1, keepdims=True) + EPS)
                gg = c["g"]
                oa_ref[rows, cols] = (o * rstd * g_ref[:, cols] * (gg * _sigmoid(gg))).astype(oa_ref.dtype)
            st_scr[hd] = st

    return pl.pallas_call(
        body, name="hgrn_fwd", grid=(nc // cb,),
        in_specs=[pl.BlockSpec((rb, 4 * DA), lambda c: (c, 0)),
                  pl.BlockSpec((1, DA), lambda c: (0, 0)),
                  pl.BlockSpec((1, DA), lambda c: (0, 0))],
        out_specs=[pl.BlockSpec((rb, DA), lambda c: (c, 0)),
                   pl.BlockSpec((rb, DA), lambda c: (c, 0)),
                   pl.BlockSpec((cb, NH, HD, HD), lambda c: (c, 0, 0, 0))],
        out_shape=[jax.ShapeDtypeStruct((t, DA), F32), jax.ShapeDtypeStruct((t, DA), SAVE),
                   jax.ShapeDtypeStruct((nc, NH, HD, HD), F32)],
        scratch_shapes=[pltpu.VMEM((NH, HD, HD), F32)],
        compiler_params=_params(("arbitrary",)),
    )(z, lb, gain)


def _hgrn_bwd(z, lb, gain, o, states, doa):
    t = z.shape[0]
    nc = t // ACH
    cb = min(ACB, nc)
    rb = cb * ACH
    nblk = nc // cb

    def body(z_ref, lb_ref, g_ref, o_ref, st_ref, doa_ref, dz_ref, dgain_ref, dlb_ref, dst_scr):
        @pl.when(pl.program_id(0) == 0)
        def _():
            dst_scr[...] = jnp.zeros_like(dst_scr)
            dgain_ref[...] = jnp.zeros_like(dgain_ref)
            dlb_ref[...] = jnp.zeros_like(dlb_ref)

        upper = _tri(ACH, False).astype(MXU)
        for hd in range(NH):
            cols = slice(hd * HD, (hd + 1) * HD)
            gain = g_ref[:, cols]
            dsp = dst_scr[hd]
            dgain = jnp.zeros((1, HD), F32)
            dlb = jnp.zeros((1, HD), F32)
            for cc in reversed(range(cb)):
                rows = slice(cc * ACH, (cc + 1) * ACH)
                c = _hgrn_chunk(z_ref, lb_ref, hd, rows)
                o = o_ref[rows, cols]
                do_a = doa_ref[rows, cols]
                gg = c["g"]
                sgg = _sigmoid(gg)
                silu_g = gg * sgg
                rstd = lax.rsqrt(jnp.mean(o * o, axis=-1, keepdims=True) + EPS)
                n = o * rstd
                dn = do_a * gain * silu_g
                dg = do_a * n * gain * (sgg * (1.0 + gg * (1.0 - sgg)))
                dgain = dgain + jnp.sum(do_a * silu_g * n, axis=0, keepdims=True)
                d_o = rstd * (dn - n * jnp.mean(dn * n, axis=-1, keepdims=True))

                st = st_ref[cc, hd]
                datt = jnp.where(c["causal"], _mm_nt(d_o, c["v"]), 0.0)
                dv = _mm_tn(c["att"], d_o) + _mm_nt(c["kd"], dsp)
                dqt = _mm(datt, c["kt"])
                dqe = _mm(d_o, st)
                dkt = _mm_tn(datt, c["qt"])
                dkd = _mm(c["v"], dsp)
                decay = jnp.exp(c["bend"])
                dbend = (decay * jnp.sum(st * dsp, axis=0, keepdims=True)
                         + jnp.sum(dkd * c["kd"], axis=0, keepdims=True))
                dsp = dsp * decay + _mm_tn(d_o, c["qe"])
                db = dqt * c["qt"] + dqe * c["qe"] - dkt * c["kt"] - dkd * c["kd"]
                dqs = dqt * c["eq"] + dqe * c["eb"]
                dk = dkt * c["ek"] + dkd * c["ed"]
                dlf = _mm_exact_l(upper, db) + dbend
                df = dlf / c["f"] - dk
                sg = c["sg"]
                dlb = dlb + jnp.sum(df * (1.0 - sg), axis=0, keepdims=True)
                dfl = df * (1.0 - c["lb"]) * sg * (1.0 - sg)
                sq, q = c["sq"], c["q"]
                dq = dqs * (sq * (1.0 + q * (1.0 - sq)))
                c0 = hd * HD
                dz_ref[rows, c0:c0 + HD] = dq.astype(dz_ref.dtype)
                dz_ref[rows, DA + c0:DA + c0 + HD] = dfl.astype(dz_ref.dtype)
                dz_ref[rows, 2 * DA + c0:2 * DA + c0 + HD] = dv.astype(dz_ref.dtype)
                dz_ref[rows, 3 * DA + c0:3 * DA + c0 + HD] = dg.astype(dz_ref.dtype)
            dst_scr[hd] = dsp
            dgain_ref[:, cols] += dgain
            dlb_ref[:, cols] += dlb

    rev = lambda c: (nblk - 1 - c, 0)
    return pl.pallas_call(
        body, name="hgrn_bwd", grid=(nblk,),
        in_specs=[pl.BlockSpec((rb, 4 * DA), rev),
                  pl.BlockSpec((1, DA), lambda c: (0, 0)),
                  pl.BlockSpec((1, DA), lambda c: (0, 0)),
                  pl.BlockSpec((rb, DA), rev),
                  pl.BlockSpec((cb, NH, HD, HD), lambda c: (nblk - 1 - c, 0, 0, 0)),
                  pl.BlockSpec((rb, DA), rev)],
        out_specs=[pl.BlockSpec((rb, 4 * DA), rev),
                   pl.BlockSpec((1, DA), lambda c: (0, 0)),
                   pl.BlockSpec((1, DA), lambda c: (0, 0))],
        out_shape=[jax.ShapeDtypeStruct((t, DIN), SAVE), jax.ShapeDtypeStruct((1, DA), F32),
                   jax.ShapeDtypeStruct((1, DA), F32)],
        scratch_shapes=[pltpu.VMEM((NH, HD, HD), F32)],
        compiler_params=_params(("arbitrary",)),
    )(z, lb, gain, o, states, doa)


def _shift_down(prev8, x, k):
    cat = jnp.concatenate([prev8, x], axis=0)
    return pltpu.roll(cat, k, axis=0)[8:, :]


def _shift_up(x, next8, k):
    n = x.shape[0]
    cat = jnp.concatenate([x, next8], axis=0)
    return pltpu.roll(cat, n + 8 - k, axis=0)[:n, :]


def _lru_gates(x, prev8, cw_ref, vec_ref, wa_ref, wx_ref):
    xs = [x, _shift_down(prev8, x, 1), _shift_down(prev8, x, 2), _shift_down(prev8, x, 3)]
    xc = vec_ref[0:1, :] + cw_ref[3:4, :] * xs[0] + cw_ref[2:3, :] * xs[1] + cw_ref[1:2, :] * xs[2] + cw_ref[0:1, :] * xs[3]
    r = _sigmoid(_mm(xc, wa_ref[...]) + vec_ref[1:2, :])
    gi = _sigmoid(_mm(xc, wx_ref[...]) + vec_ref[2:3, :])
    lam = vec_ref[3:4, :]
    sp = jnp.maximum(-lam, 0.0) + jnp.log(1.0 + jnp.exp(-jnp.abs(lam)))
    la = -LRU_C * r * sp
    a = jnp.exp(la)
    mult = jnp.sqrt(-_expm1(2.0 * la))
    return xs, xc, r, gi, sp, a, mult


def _scan_down(a, u):
    n = a.shape[0]
    row = _iota(a.shape, 0)
    s = 1
    while s < n:
        keep = row >= s
        ash = jnp.where(keep, pltpu.roll(a, s, axis=0), 1.0)
        ush = jnp.where(keep, pltpu.roll(u, s, axis=0), 0.0)
        u = a * ush + u
        a = a * ash
        s *= 2
    return a, u


def _scan_up(a, u):
    n = a.shape[0]
    row = _iota(a.shape, 0)
    s = 1
    while s < n:
        keep = row < n - s
        ash = jnp.where(keep, pltpu.roll(a, n - s, axis=0), 1.0)
        ush = jnp.where(keep, pltpu.roll(u, n - s, axis=0), 0.0)
        u = a * ush + u
        a = a * ash
        s *= 2
    return a, u


def _lru_fwd(z, cw, vec, wa, wx, tb):
    t = z.shape[0]
    xcol, gcol = (4 * DA) // DB, (4 * DA) // DB + 1

    def body(x_ref, gate_ref, cw_ref, vec_ref, wa_ref, wx_ref, ob_ref, h_ref, xprev_scr, hc_scr):
        @pl.when(pl.program_id(0) == 0)
        def _():
            xprev_scr[...] = jnp.zeros_like(xprev_scr)
            hc_scr[...] = jnp.zeros_like(hc_scr)

        x = x_ref[...]
        _, xc, _, gi, _, a, mult = _lru_gates(x, xprev_scr[...], cw_ref, vec_ref, wa_ref, wx_ref)
        acum, hloc = _scan_down(a, mult * gi * xc)
        h = hloc + acum * hc_scr[0:1, :]
        h_ref[...] = h
        hc_scr[...] = jnp.broadcast_to(_row(h, tb - 1), hc_scr.shape)
        xprev_scr[...] = x[tb - 8:, :]
        y = h * _gelu(gate_ref[...])
        ms = _mm_exact_r(y * y, _group_matrix(DB, 1.0 / GRP).astype(MXU))
        ob_ref[...] = (y * lax.rsqrt(ms + EPS) * vec_ref[4:5, :]).astype(ob_ref.dtype)

    return pl.pallas_call(
        body, name="lru_fwd", grid=(t // tb,),
        in_specs=[pl.BlockSpec((tb, DB), lambda i: (i, xcol)),
                  pl.BlockSpec((tb, DB), lambda i: (i, gcol)),
                  pl.BlockSpec((8, DB), lambda i: (0, 0)),
                  pl.BlockSpec((8, DB), lambda i: (0, 0)),
                  pl.BlockSpec((DB, DB), lambda i: (0, 0)),
                  pl.BlockSpec((DB, DB), lambda i: (0, 0))],
        out_specs=[pl.BlockSpec((tb, DB), lambda i: (i, 0)),
                   pl.BlockSpec((tb, DB), lambda i: (i, 0))],
        out_shape=[jax.ShapeDtypeStruct((t, DB), SAVE), jax.ShapeDtypeStruct((t, DB), F32)],
        scratch_shapes=[pltpu.VMEM((8, DB), F32), pltpu.VMEM((8, DB), F32)],
        compiler_params=_params(("arbitrary",)),
    )(z, z, cw, vec, wa, wx)


def _lru_bwd(z, hseq, dob, cw, vec, wa, wx, dz, tb):
    t = z.shape[0]
    nb = t // tb
    xcol, gcol = (4 * DA) // DB, (4 * DA) // DB + 1
    per = tb // 8

    def body(x_ref, xh_ref, gate_ref, h_ref, hh_ref, dob_ref, cw_ref, vec_ref, wa_ref, wx_ref, _,
             dz_ref, dcw_ref, dvec_ref, dwa_ref, dwx_ref, gc_scr, an_scr, dxc_scr):
        step = pl.program_id(0)
        blk = nb - 1 - step

        @pl.when(step == 0)
        def _():
            for ref in (gc_scr, an_scr, dxc_scr, dcw_ref, dvec_ref, dwa_ref, dwx_ref):
                ref[...] = jnp.zeros_like(ref)

        first = (blk > 0).astype(F32)
        x = x_ref[...]
        xs, xc, r, gi, sp, a, mult = _lru_gates(x, xh_ref[...] * first, cw_ref, vec_ref, wa_ref, wx_ref)
        h = h_ref[...]
        hprev = _shift_down(hh_ref[...] * first, h, 1)
        ge, dge = _gelu_and_grad(gate_ref[...])
        y = h * ge
        gmat = _group_matrix(DB, 1.0 / GRP).astype(MXU)
        rstd = lax.rsqrt(_mm_exact_r(y * y, gmat) + EPS)
        n = y * rstd
        d_ob = dob_ref[...]
        dn = d_ob * vec_ref[4:5, :]
        dvec_ref[4:5, :] += jnp.sum(d_ob * n, axis=0, keepdims=True)
        dy = rstd * (dn - n * _mm_exact_r(dn * n, gmat))
        dh = dy * ge
        dgate = dy * h * dge

        row = _iota(a.shape, 0)
        anext = jnp.where(row == tb - 1, an_scr[0:1, :], pltpu.roll(a, tb - 1, axis=0))
        acum, gloc = _scan_up(anext, dh)
        g = gloc + acum * gc_scr[0:1, :]
        gc_scr[...] = jnp.broadcast_to(_row(g, 0), gc_scr.shape)
        an_scr[...] = jnp.broadcast_to(_row(a, 0), an_scr.shape)

        da = g * hprev
        dmult = g * gi * xc
        dgi = g * mult * xc
        dxc = g * mult * gi
        dla = da * a - dmult * (a * a) / mult
        dr = dla * (-LRU_C * sp)
        dsp = jnp.sum(dla * (-LRU_C * r), axis=0, keepdims=True)
        lam = vec_ref[3:4, :]
        dvec_ref[3:4, :] += -dsp * _sigmoid(-lam)
        dpa = dr * r * (1.0 - r)
        dpx = dgi * gi * (1.0 - gi)
        dwa_ref[...] += _mm_tn(xc, dpa)
        dwx_ref[...] += _mm_tn(xc, dpx)
        dvec_ref[1:2, :] += jnp.sum(dpa, axis=0, keepdims=True)
        dvec_ref[2:3, :] += jnp.sum(dpx, axis=0, keepdims=True)
        dxc = dxc + _mm_nt(dpa, wa_ref[...]) + _mm_nt(dpx, wx_ref[...])
        dvec_ref[0:1, :] += jnp.sum(dxc, axis=0, keepdims=True)
        for tap in range(4):
            dcw_ref[tap:tap + 1, :] += jnp.sum(dxc * xs[3 - tap], axis=0, keepdims=True)
        nxt = dxc_scr[...]
        dx = (cw_ref[3:4, :] * dxc + cw_ref[2:3, :] * _shift_up(dxc, nxt, 1)
              + cw_ref[1:2, :] * _shift_up(dxc, nxt, 2) + cw_ref[0:1, :] * _shift_up(dxc, nxt, 3))
        dxc_scr[...] = dxc[:8, :]
        dz_ref[:, :DB] = dx.astype(dz_ref.dtype)
        dz_ref[:, DB:] = dgate.astype(dz_ref.dtype)

    def halo(col):
        return lambda s: (jnp.maximum((nb - 1 - s) * per - 1, 0), col)

    const = lambda s: (0, 0)
    return pl.pallas_call(
        body, name="lru_bwd", grid=(nb,),
        in_specs=[pl.BlockSpec((tb, DB), lambda s: (nb - 1 - s, xcol)),
                  pl.BlockSpec((8, DB), halo(xcol)),
                  pl.BlockSpec((tb, DB), lambda s: (nb - 1 - s, gcol)),
                  pl.BlockSpec((tb, DB), lambda s: (nb - 1 - s, 0)),
                  pl.BlockSpec((8, DB), halo(0)),
                  pl.BlockSpec((tb, DB), lambda s: (nb - 1 - s, 0)),
                  pl.BlockSpec((8, DB), const), pl.BlockSpec((8, DB), const),
                  pl.BlockSpec((DB, DB), const), pl.BlockSpec((DB, DB), const),
                  pl.BlockSpec(memory_space=pl.ANY)],
        out_specs=[pl.BlockSpec((tb, 2 * DB), lambda s: (nb - 1 - s, (4 * DA) // (2 * DB))),
                   pl.BlockSpec((8, DB), const), pl.BlockSpec((8, DB), const),
                   pl.BlockSpec((DB, DB), const), pl.BlockSpec((DB, DB), const)],
        out_shape=[jax.ShapeDtypeStruct((t, DIN), SAVE), jax.ShapeDtypeStruct((8, DB), F32),
                   jax.ShapeDtypeStruct((8, DB), F32), jax.ShapeDtypeStruct((DB, DB), F32),
                   jax.ShapeDtypeStruct((DB, DB), F32)],
        scratch_shapes=[pltpu.VMEM((8, DB), F32), pltpu.VMEM((8, DB), F32), pltpu.VMEM((8, DB), F32)],
        input_output_aliases={10: 0},
        compiler_params=_params(("arbitrary",)),
    )(z, z, z, hseq, hseq, dob, cw, vec, wa, wx, dz)


def _sgu_chunk(u_in, v_in, w_ref, bias, gmat):
    uu, duu = _gelu_and_grad(u_in)
    vv, dvv = _gelu_and_grad(v_in)
    mu = _mm_exact_r(vv, gmat)
    dlt = vv - mu
    rstd_v = lax.rsqrt(_mm_exact_r(dlt * dlt, gmat) + EPS)
    vn = dlt * rstd_v
    col = _iota((CCH, DC), 1) // GRP
    causal = _iota((CCH, CCH), 0) >= _iota((CCH, CCH), 1)
    ws = [jnp.where(causal, w_ref[g], 0.0) for g in range(DC // GRP)]
    zz = bias
    for g, w in enumerate(ws):
        zz = zz + jnp.where(col == g, _mm(w, vn), 0.0)
    return uu, duu, dvv, rstd_v, vn, zz, ws, col, causal


def _sgu_fwd(z, w, bias, gain, tb):
    t = z.shape[0]
    ucol, vcol = (4 * DA + 2 * DB) // DC, (4 * DA + 2 * DB) // DC + 1

    def body(u_ref, v_ref, w_ref, b_ref, g_ref, oc_ref):
        gmat = _group_matrix(DC, 1.0 / GRP).astype(MXU)
        for ch in range(tb // CCH):
            rows = slice(ch * CCH, (ch + 1) * CCH)
            uu, _, _, _, _, zz, _, _, _ = _sgu_chunk(u_ref[rows, :], v_ref[rows, :], w_ref, b_ref[...], gmat)
            y = uu * zz
            ms = _mm_exact_r(y * y, gmat)
            oc_ref[rows, :] = (y * lax.rsqrt(ms + EPS) * g_ref[...]).astype(oc_ref.dtype)

    const = lambda i: (0, 0)
    return pl.pallas_call(
        body, name="sgu_fwd", grid=(t // tb,),
        in_specs=[pl.BlockSpec((tb, DC), lambda i: (i, ucol)),
                  pl.BlockSpec((tb, DC), lambda i: (i, vcol)),
                  pl.BlockSpec((DC // GRP, CCH, CCH), lambda i: (0, 0, 0)),
                  pl.BlockSpec((CCH, DC), const), pl.BlockSpec((1, DC), const)],
        out_specs=pl.BlockSpec((tb, DC), lambda i: (i, 0)),
        out_shape=jax.ShapeDtypeStruct((t, DC), SAVE),
        compiler_params=_params(("parallel",)),
    )(z, z, w, bias, gain)


def _sgu_bwd(z, doc, w, bias, gain, dz, tb):
    t = z.shape[0]
    nb = t // tb
    ucol, vcol = (4 * DA + 2 * DB) // DC, (4 * DA + 2 * DB) // DC + 1
    ng = DC // GRP

    def body(u_ref, v_ref, doc_ref, w_ref, b_ref, g_ref, _, dz_ref, dw_ref, dbias_ref, dgain_ref, dbsum_scr):
        i = pl.program_id(0)

        @pl.when(i == 0)
        def _():
            for ref in (dw_ref, dgain_ref, dbsum_scr):
                ref[...] = jnp.zeros_like(ref)

        gmat = _group_matrix(DC, 1.0 / GRP).astype(MXU)
        for ch in range(tb // CCH):
            rows = slice(ch * CCH, (ch + 1) * CCH)
            uu, duu, dvv, rstd_v, vn, zz, ws, col, causal = _sgu_chunk(
                u_ref[rows, :], v_ref[rows, :], w_ref, b_ref[...], gmat)
            y = uu * zz
            rstd = lax.rsqrt(_mm_exact_r(y * y, gmat) + EPS)
            n = y * rstd
            d_oc = doc_ref[rows, :]
            dn = d_oc * g_ref[...]
            dgain_ref[0:1, :] += jnp.sum(d_oc * n, axis=0, keepdims=True)
            dy = rstd * (dn - n * _mm_exact_r(dn * n, gmat))
            dzz = dy * uu
            dz_ref[rows, :DC] = (dy * zz * duu).astype(dz_ref.dtype)
            dbsum_scr[...] += dzz
            dvn = jnp.zeros_like(dzz)
            for g in range(ng):
                sel = col == g
                dvn = dvn + jnp.where(sel, _mm_tn(ws[g], dzz), 0.0)
                dw_ref[g] += jnp.where(causal, _mm_nt(jnp.where(sel, dzz, 0.0), vn), 0.0)
            dv = rstd_v * (dvn - _mm_exact_r(dvn, gmat) - vn * _mm_exact_r(dvn * vn, gmat))
            dz_ref[rows, DC:] = (dv * dvv).astype(dz_ref.dtype)

        @pl.when(i == nb - 1)
        def _():
            dbias_ref[...] = _mm_exact_r(dbsum_scr[...], _group_matrix(DC, 1.0).astype(MXU))

    const = lambda i: (0, 0)
    return pl.pallas_call(
        body, name="sgu_bwd", grid=(nb,),
        in_specs=[pl.BlockSpec((tb, DC), lambda i: (i, ucol)),
                  pl.BlockSpec((tb, DC), lambda i: (i, vcol)),
                  pl.BlockSpec((tb, DC), lambda i: (i, 0)),
                  pl.BlockSpec((ng, CCH, CCH), lambda i: (0, 0, 0)),
                  pl.BlockSpec((CCH, DC), const), pl.BlockSpec((1, DC), const),
                  pl.BlockSpec(memory_space=pl.ANY)],
        out_specs=[pl.BlockSpec((tb, 2 * DC), lambda i: (i, (4 * DA + 2 * DB) // (2 * DC))),
                   pl.BlockSpec((ng, CCH, CCH), lambda i: (0, 0, 0)),
                   pl.BlockSpec((CCH, DC), const), pl.BlockSpec((8, DC), const)],
        out_shape=[jax.ShapeDtypeStruct((t, DIN), SAVE), jax.ShapeDtypeStruct((ng, CCH, CCH), F32),
                   jax.ShapeDtypeStruct((CCH, DC), F32), jax.ShapeDtypeStruct((8, DC), F32)],
        scratch_shapes=[pltpu.VMEM((CCH, DC), F32)],
        input_output_aliases={6: 0},
        compiler_params=_params(("arbitrary",)),
    )(z, z, doc, w, bias, gain, dz)


def _head(h, gain, target, tm):
    t = h.shape[0]

    def body(h_ref, g_ref, t_ref, dh_ref, loss_ref, dgain_ref):
        @pl.when(pl.program_id(0) == 0)
        def _():
            loss_ref[...] = jnp.zeros_like(loss_ref)
            dgain_ref[...] = jnp.zeros_like(dgain_ref)

        hh = h_ref[...]
        gain = g_ref[...]
        rstd = lax.rsqrt(jnp.mean(hh * hh, axis=-1, keepdims=True) + EPS)
        xhat = hh * rstd
        err = xhat * gain - t_ref[...]
        per_tok = jnp.mean(err * err, axis=-1, keepdims=True)
        loss_ref[...] += 0.5 * jnp.sum(per_tok, axis=0, keepdims=True)
        dy = err * (1.0 / D)
        dgain_ref[...] += jnp.sum(dy * xhat, axis=0, keepdims=True)
        dxh = dy * gain
        dh_ref[...] = rstd * (dxh - xhat * jnp.mean(dxh * xhat, axis=-1, keepdims=True))

    return pl.pallas_call(
        body, name="head", grid=(t // tm,),
        in_specs=[pl.BlockSpec((tm, D), lambda i: (i, 0)),
                  pl.BlockSpec((1, D), lambda i: (0, 0)),
                  pl.BlockSpec((tm, D), lambda i: (i, 0))],
        out_specs=[pl.BlockSpec((tm, D), lambda i: (i, 0)),
                   pl.BlockSpec((1, 128), lambda i: (0, 0)),
                   pl.BlockSpec((1, D), lambda i: (0, 0))],
        out_shape=[jax.ShapeDtypeStruct((t, D), F32), jax.ShapeDtypeStruct((1, 128), F32),
                   jax.ShapeDtypeStruct((1, D), F32)],
        compiler_params=_params(("arbitrary",)),
    )(h, gain, target)


def _adamw(w, g, m, v):
    m = ADAM_B1 * m + (1.0 - ADAM_B1) * g
    v = ADAM_B2 * v + (1.0 - ADAM_B2) * (g * g)
    m_hat = m / (1.0 - ADAM_B1 ** ADAM_STEP)
    v_hat = v / (1.0 - ADAM_B2 ** ADAM_STEP)
    delta = -ADAM_LR * (m_hat / (jnp.sqrt(v_hat) + ADAM_EPS) + ADAM_WD * w)
    return delta, m, v


def _adamw_big(recv, w, m, v, tr, name, after, transposed=False):
    depth, rows, cols = w.shape
    rspec = (pl.BlockSpec((NDEV, cols, tr), lambda i: (0, 0, i)) if transposed
             else pl.BlockSpec((NDEV, tr, cols), lambda i: (0, i, 0)))

    def body(*refs):
        r_refs = refs[:depth]
        w_ref, m_ref, v_ref, _, g_out, d_out, m_out, v_out = refs[depth:]
        for l in range(depth):
            g = r_refs[l][0].astype(F32)
            for k in range(1, NDEV):
                g = g + r_refs[l][k].astype(F32)
            if transposed:
                g = g.T
            delta, m_, v_ = _adamw(w_ref[l], g, m_ref[l], v_ref[l])
            g_out[l] = g
            d_out[l] = delta
            m_out[l] = m_
            v_out[l] = v_

    spec = pl.BlockSpec((depth, tr, cols), lambda i: (0, i, 0))
    return pl.pallas_call(
        body, name=name, grid=(rows // tr,),
        in_specs=[rspec] * depth + [spec] * 3
        + [pl.BlockSpec(memory_space=pl.ANY)],
        out_specs=[spec] * 4, out_shape=[jax.ShapeDtypeStruct((depth, rows, cols), F32)] * 4,
        compiler_params=_params(("parallel",)),
    )(*recv, w, m, v, after)


def _sum_devices(recv):
    _, r, _ = recv.shape

    def body(r_ref, out_ref):
        g = r_ref[0]
        for k in range(1, NDEV):
            g = g + r_ref[k]
        out_ref[...] = g

    return pl.pallas_call(body, name="sum_devices", out_shape=jax.ShapeDtypeStruct((r, 128), F32))(recv)


def _adamw_small(w, g, m, v):
    def body(w_ref, g_ref, m_ref, v_ref, d_out, m_out, v_out):
        delta, m_, v_ = _adamw(w_ref[...], g_ref[...], m_ref[...], v_ref[...])
        d_out[...] = delta
        m_out[...] = m_
        v_out[...] = v_

    return pl.pallas_call(body, name="adamw_small", out_shape=[jax.ShapeDtypeStruct(w.shape, F32)] * 3)(w, g, m, v)


def _pack(arrs):
    flat = jnp.concatenate([a.reshape(-1) for a in arrs])
    pad = (-flat.shape[0]) % 1024
    return jnp.pad(flat, (0, pad)).reshape(-1, 128)


def _unpack(buf, like):
    flat = buf.reshape(-1)
    out, off = [], 0
    for a in like:
        out.append(flat[off:off + a.size].reshape(a.shape))
        off += a.size
    return out


def _block_diag(w):
    nb, bd, _ = w.shape
    eye = jnp.eye(nb, dtype=w.dtype)
    return (eye[:, None, :, None] * w[:, :, None, :]).reshape(nb * bd, nb * bd)


def _diag_blocks(w):
    nb = w.shape[0] // GRP
    return jnp.stack([w[g * GRP:(g + 1) * GRP, g * GRP:(g + 1) * GRP] for g in range(nb)])


SMALL = ['ffn1_norm', 'mix_norm', 'hgrn_lb_logits', 'hgrn_norm', 'conv_b', 'lru_wa', 'lru_ba', 'lru_wx', 'lru_bx',
         'lru_lambda', 'lru_norm', 'sgu_w', 'sgu_b', 'sgu_norm', 'ffn2_norm', 'final_norm']
NAMES = ['ffn1_norm', 'ffn1_wg', 'ffn1_wu', 'ffn1_wd', 'mix_norm', 'w_in', 'hgrn_lb_logits', 'hgrn_norm', 'conv_w',
         'conv_b', 'lru_wa', 'lru_ba', 'lru_wx', 'lru_bx', 'lru_lambda', 'lru_norm', 'sgu_w', 'sgu_b', 'sgu_norm',
         'w_out', 'ffn2_norm', 'ffn2_wg', 'ffn2_wu', 'ffn2_wd', 'final_norm']


def _step(x, target, w, m, v):
    depth = w['ffn1_wg'].shape[0]
    t = x.shape[1]
    h = x.reshape(t, D)
    target = target.reshape(t, D)
    tm_f, tm_b, tb = min(TM_F, t), min(TM_B, t), min(TB, t)
    my = 4 * lax.axis_index("x") + 2 * lax.axis_index("y") + lax.axis_index("c")

    cw_tile = jnp.pad(w['conv_w'].reshape(-1, 128), ((0, 8 - depth), (0, 0)))
    cw_all = _all_gather([cw_tile], "gather_conv")[0][:, :depth]
    conv_w = jnp.moveaxis(cw_all.reshape(NDEV, depth, 4, DB // NDEV), 0, 2).reshape(depth, 4, DB)
    lbs, lb_soft = _lower_bounds(w['hgrn_lb_logits'])

    def row(a):
        return a.reshape(1, -1)

    def tr(a):
        return jnp.swapaxes(a, -1, -2)

    def shards(l, unit):
        if unit == 1:
            return [tr(w['w_in'][l]).astype(WIRE), w['w_out'][l].astype(WIRE)]
        f = 'ffn1' if unit == 0 else 'ffn2'
        return [tr(w[f + '_wg'][l]).astype(WIRE), tr(w[f + '_wu'][l]).astype(WIRE), w[f + '_wd'][l].astype(WIRE)]

    units = [(l, u) for l in range(depth) for u in range(3)]

    def start_ici(idx, deps=()):
        return _transfer_start(shards(*units[idx]), True, "gather_ici_%d_%d" % units[idx], deps=deps)

    def start_d2d(idx, handle, after):
        lands = _transfer_wait(handle, after, "gather_ici_wait_%d_%d" % units[idx])
        return _forward_start(lands, "gather_d2d_%d_%d" % units[idx])

    pipe = dict(idx=0)
    pipe['d2d'] = start_d2d(0, start_ici(0), h)
    pipe['ici'] = start_ici(1, deps=(pipe['d2d']['token'],))

    def next_weights(after):
        idx = pipe['idx']
        lands = _transfer_wait(pipe['d2d'], after, "gather_d2d_wait_%d_%d" % units[idx])
        pipe['idx'] = idx + 1
        tok = 0.0
        if idx + 1 < len(units):
            pipe['d2d'] = start_d2d(idx + 1, pipe['ici'], lands[-1])
            tok = pipe['d2d']['token'][0, 0]
            if idx + 2 < len(units):
                pipe['ici'] = start_ici(idx + 2, deps=(pipe['d2d']['token'],))
                tok = pipe['ici']['token'][0, 0]
        return lands, tok

    saved = []
    for l in range(depth):
        lands, tok = next_weights(h)
        s = dict(ffn1=[a.reshape(FF, D) for a in lands], h0=h)
        h, s['xn1'], s['a1'], s['b1'] = _ffn_fwd(h, row(w['ffn1_norm'][l]) + tok, *s['ffn1'], tm_f)
        s['h1'] = h
        (win, wout), tok = next_weights(h)
        win, wout = win.reshape(DIN, D), wout.reshape(D, D)
        s['win'], s['wout'] = win, wout
        z, s['xnm'] = _inproj_fwd(h, row(w['mix_norm'][l]) + tok, win, tm_f)
        s['z'] = z
        s['o'], oa, s['states'] = _hgrn_fwd(z, row(lbs[l]), row(w['hgrn_norm'][l]))
        s['cw'] = jnp.pad(conv_w[l], ((0, 4), (0, 0)))
        s['vec'] = jnp.concatenate([row(w['conv_b'][l]), row(w['lru_ba'][l]), row(w['lru_bx'][l]),
                                    row(w['lru_lambda'][l]), row(w['lru_norm'][l]), jnp.zeros((3, DB), F32)])
        s['wa'], s['wx'] = _block_diag(w['lru_wa'][l]), _block_diag(w['lru_wx'][l])
        ob, s['hseq'] = _lru_fwd(z, s['cw'], s['vec'], s['wa'], s['wx'], tb)
        s['bias'] = jnp.repeat(w['sgu_b'][l].T, GRP, axis=1)
        oc = _sgu_fwd(z, w['sgu_w'][l], s['bias'], row(w['sgu_norm'][l]), tb)
        s['oa'], s['ob'], s['oc'] = oa, ob, oc
        h = _outproj_fwd(h, oa, ob, oc, wout, tm_f)
        s['h2'] = h
        lands, tok = next_weights(h)
        s['ffn2'] = [a.reshape(FF, D) for a in lands]
        h, s['xn2'], s['a2'], s['b2'] = _ffn_fwd(h, row(w['ffn2_norm'][l]) + tok, *s['ffn2'], tm_f)
        saved.append(s)

    dh, loss_part, g_final = _head(h, row(w['final_norm']), target, tm_f)
    loss = lax.psum(loss_part[0, 0], ("x", "y", "c"))

    recv = {k: [None] * depth for k in ('wg1', 'wu1', 'wd1', 'wg2', 'wu2', 'wd2', 'win', 'wout')}
    flight = []

    def land(after):
        handle, kinds, l = flight.pop()
        for k, a in zip(kinds, _transfer_wait(handle, after, f"exchange_wait_{kinds[0]}_{l}")):
            recv[k][l] = a

    def exchange(arrs, kinds, l, deps=()):
        handle = _transfer_start(arrs, False, f"exchange_start_{kinds[0]}_{l}", deps=deps)
        if flight:
            land(handle['token'])
        flight.append((handle, kinds, l))
        return handle['token'][0, 0]

    small = {k: [None] * depth for k in SMALL if k != 'final_norm'}
    dconv = [None] * depth
    dlb = [None] * depth
    tok = 0.0
    for l in reversed(range(depth)):
        s = saved[l]
        dh, g, *cot = _ffn_bwd_x(dh, s['h2'], row(w['ffn2_norm'][l]) + tok, s['a2'], s['b2'], *s['ffn2'], tm_f)
        dws = _ffn_bwd_w(s['xn2'], *cot, tm_f)
        tok = exchange([a.reshape(NDEV, FFS, D) for a in dws], ('wg2', 'wu2', 'wd2'), l)
        small['ffn2_norm'][l] = g
        doa, dob, doc, dwout = _outproj_bwd(dh, s['oa'], s['ob'], s['oc'], s['wout'], tm_f)
        dz, g_hn, dlb[l] = _hgrn_bwd(s['z'], row(lbs[l]), row(w['hgrn_norm'][l]) + tok, s['o'], s['states'], doa)
        small['hgrn_norm'][l] = g_hn
        dz, dcw, dvec, dwa, dwx = _lru_bwd(s['z'], s['hseq'], dob, s['cw'], s['vec'], s['wa'], s['wx'], dz, tb)
        dconv[l] = dcw[:4]
        small['conv_b'][l], small['lru_ba'][l], small['lru_bx'][l] = dvec[0], dvec[1].reshape(4, GRP), dvec[2].reshape(4, GRP)
        small['lru_lambda'][l], small['lru_norm'][l] = dvec[3], dvec[4]
        small['lru_wa'][l], small['lru_wx'][l] = _diag_blocks(dwa), _diag_blocks(dwx)
        dz, dsw, dbias, dgc = _sgu_bwd(s['z'], doc, w['sgu_w'][l], s['bias'], row(w['sgu_norm'][l]), dz, tb)
        small['sgu_w'][l], small['sgu_b'][l], small['sgu_norm'][l] = dsw, dbias[:, ::GRP].T, dgc[0]
        dwin = _inproj_bwd_w(s['xnm'], dz, tm_f)
        tok = exchange([dwin.reshape(NDEV, DINS, D), dwout.reshape(NDEV, D // NDEV, D)], ('win', 'wout'), l)
        dh, g = _inproj_bwd_x(dh, dz, s['h1'], row(w['mix_norm'][l]) + tok, s['win'], tm_b)
        small['mix_norm'][l] = g
        tok = 0.0
        if l == 0:
            small['ffn1_norm'][0] = jnp.zeros((1, D), F32)
            small['hgrn_lb_logits'] = list(_lower_bounds_bwd(lb_soft, jnp.concatenate(dlb, axis=0)))
            parts = [jnp.stack([small[k][j].reshape(w[k].shape[1:]) for j in range(depth)])
                     for k in SMALL if k != 'final_norm']
            parts += [g_final.reshape(D), jnp.stack(dconv)]
            small_flight = _transfer_start([_pack(parts)], True, "gather_small_start", direct=True)
            tok = small_flight['token'][0, 0]
        dh, g, *cot = _ffn_bwd_x(dh, s['h0'], row(w['ffn1_norm'][l]) + tok, s['a1'], s['b1'], *s['ffn1'], tm_f)
        dws = _ffn_bwd_w(s['xn1'], *cot, tm_f)
        before = ()
        if l == 0:
            g_last = _all_gather([g.reshape(8, 128)], "gather_last")[0]
            before = (g_last,)
        else:
            small['ffn1_norm'][l] = g
        tok = exchange([a.reshape(NDEV, FFS, D) for a in dws], ('wg1', 'wu1', 'wd1'), l, before)
    grad_x = dh.reshape(1, t, D)

    out = {}
    last = flight[0][0]['token']

    def ffn_update(f, n, after):
        for kind in ('wg', 'wu'):
            k = f + '_' + kind
            res = _adamw_big(recv[kind + n], tr(w[k]), tr(m[k]), tr(v[k]), 32, "adamw_ffn", after)
            out[k] = tuple(tr(a) for a in res)
        k = f + '_wd'
        out[k] = _adamw_big(recv['wd' + n], w[k], m[k], v[k], 32, "adamw_ffn", after)

    ffn_update('ffn2', '2', last)
    out['w_in'] = _adamw_big(recv['win'], w['w_in'], m['w_in'], v['w_in'], 128, "adamw_win", last, transposed=True)
    out['w_out'] = _adamw_big(recv['wout'], w['w_out'], m['w_out'], v['w_out'], 64, "adamw_wout", last)

    total = _sum_devices(_transfer_wait(small_flight, g_last, "gather_small_wait")[0])
    like = [w[k] for k in SMALL] + [jax.ShapeDtypeStruct((depth, 4, DB), F32)]
    grads = _unpack(total, like)
    gsmall = dict(zip(SMALL, grads[:-1]))
    gsmall['ffn1_norm'] = gsmall['ffn1_norm'].at[0].set(_sum_devices(g_last).reshape(D))
    gsmall['conv_w'] = lax.dynamic_slice_in_dim(grads[-1], my * (DB // NDEV), DB // NDEV, axis=2)
    keys = SMALL + ['conv_w']
    dl, mm, vv = _adamw_small(_pack([w[k] for k in keys]), _pack([gsmall[k] for k in keys]),
                              _pack([m[k] for k in keys]), _pack([v[k] for k in keys]))
    like = [w[k] for k in keys]
    for k, d_, m_, v_ in zip(keys, _unpack(dl, like), _unpack(mm, like), _unpack(vv, like)):
        out[k] = (gsmall[k], d_, m_, v_)
    done = [dl] + [out[k][1][0] for k in ('ffn2_wg', 'ffn2_wu', 'ffn2_wd', 'w_in', 'w_out')]
    land(functools.reduce(lambda p, q: p + q, [a[:1, :1] for a in done]))
    ffn_update('ffn1', '1', last)

    return (loss, grad_x, *[out[k][0] for k in NAMES], *[out[k][1] for k in NAMES],
            *[out[k][2] for k in NAMES], *[out[k][3] for k in NAMES])


def kernel(x, ffn1_norm, ffn1_wg, ffn1_wu, ffn1_wd, mix_norm, w_in, hgrn_lb_logits, hgrn_norm, conv_w, conv_b, lru_wa, lru_ba, lru_wx, lru_bx, lru_lambda, lru_norm, sgu_w, sgu_b, sgu_norm, w_out, ffn2_norm, ffn2_wg, ffn2_wu, ffn2_wd, final_norm, loss_target, m_ffn1_norm, m_ffn1_wg, m_ffn1_wu, m_ffn1_wd, m_mix_norm, m_w_in, m_hgrn_lb_logits, m_hgrn_norm, m_conv_w, m_conv_b, m_lru_wa, m_lru_ba, m_lru_wx, m_lru_bx, m_lru_lambda, m_lru_norm, m_sgu_w, m_sgu_b, m_sgu_norm, m_w_out, m_ffn2_norm, m_ffn2_wg, m_ffn2_wu, m_ffn2_wd, m_final_norm, v_ffn1_norm, v_ffn1_wg, v_ffn1_wu, v_ffn1_wd, v_mix_norm, v_w_in, v_hgrn_lb_logits, v_hgrn_norm, v_conv_w, v_conv_b, v_lru_wa, v_lru_ba, v_lru_wx, v_lru_bx, v_lru_lambda, v_lru_norm, v_sgu_w, v_sgu_b, v_sgu_norm, v_w_out, v_ffn2_norm, v_ffn2_wg, v_ffn2_wu, v_ffn2_wd, v_final_norm):
    args = locals()
    w = {k: args[k] for k in NAMES}
    m = {k: args['m_' + k] for k in NAMES}
    v = {k: args['v_' + k] for k in NAMES}
    return _step(x, loss_target, w, m, v)
```

```python
import functools

import jax
import jax.numpy as jnp
from jax import lax
from jax.experimental import pallas as pl
from jax.experimental.pallas import tpu as pltpu

F32 = jnp.float32
MXU = jnp.bfloat16
SAVE = jnp.bfloat16
WIRE = jnp.bfloat16

NDEV = 8
D = 1024
FF = 2816
FFS = FF // NDEV
FB = 256
DIN = 3072
DINS = DIN // NDEV
ZB = 512
DA, DB, DC = 512, 256, 256
HD = 128
NH = DA // HD
ACH = 64
ACB = 4
CCH = 128
GRP = 64
EPS = 1e-6
LRU_C = 8.0
VMEM_LIMIT = 60 * 1024 * 1024
TM_F = 1024
TM_B = 512
TB = 1024
SUB = 256

ADAM_LR, ADAM_B1, ADAM_B2, ADAM_EPS, ADAM_WD, ADAM_STEP = 0.001, 0.9, 0.999, 1e-08, 0.01, 10

MESH = pl.DeviceIdType.MESH


def _mm(a, b):
    return jnp.dot(a.astype(MXU), b.astype(MXU), preferred_element_type=F32)


def _mm_nt(a, b):
    return lax.dot_general(a.astype(MXU), b.astype(MXU), (((1,), (1,)), ((), ())), preferred_element_type=F32)


def _mm_tn(a, b):
    return lax.dot_general(a.astype(MXU), b.astype(MXU), (((0,), (0,)), ((), ())), preferred_element_type=F32)


def _split3(x):
    x1 = x.astype(MXU)
    r1 = x - x1.astype(F32)
    x2 = r1.astype(MXU)
    r2 = r1 - x2.astype(F32)
    return x1, x2, r2.astype(MXU)


def _mm_exact_l(c, x):
    x1, x2, x3 = _split3(x)
    return _mm(c, x1) + _mm(c, x2) + _mm(c, x3)


def _mm_exact_r(x, c):
    x1, x2, x3 = _split3(x)
    return _mm(x1, c) + _mm(x2, c) + _mm(x3, c)


def _sigmoid(x):
    return 1.0 / (1.0 + jnp.exp(-x))


def _gelu(x):
    c, k = 0.7978845608028654, 0.044715
    th = jnp.tanh(c * (x + k * x * x * x))
    return 0.5 * x * (1.0 + th)


def _gelu_and_grad(x):
    c, k = 0.7978845608028654, 0.044715
    th = jnp.tanh(c * (x + k * x * x * x))
    g = 0.5 * x * (1.0 + th)
    dg = 0.5 * (1.0 + th) + 0.5 * x * (1.0 - th * th) * c * (1.0 + 3.0 * k * x * x)
    return g, dg


def _expm1(x):
    series = x * (1.0 + x * (0.5 + x * (1.0 / 6.0 + x * (1.0 / 24.0 + x * (1.0 / 120.0)))))
    return jnp.where(jnp.abs(x) < 0.05, series, jnp.exp(x) - 1.0)


def _iota(shape, dim):
    return lax.broadcasted_iota(jnp.int32, shape, dim)


def _group_matrix(n, value):
    r, c = _iota((n, n), 0), _iota((n, n), 1)
    return jnp.where((r // GRP) == (c // GRP), value, 0.0).astype(F32)


def _row(x, k):
    r = _iota(x.shape, 0)
    return jnp.sum(jnp.where(r == k, x, 0.0), axis=0, keepdims=True)


def _rms_bwd(dxn, hh, gain):
    rstd = lax.rsqrt(jnp.mean(hh * hh, axis=-1, keepdims=True) + EPS)
    xhat = hh * rstd
    dxh = dxn * gain
    dh = rstd * (dxh - xhat * jnp.mean(dxh * xhat, axis=-1, keepdims=True))
    return dh, jnp.sum(dxn * xhat, axis=0, keepdims=True)


def _params(sem):
    return pltpu.CompilerParams(dimension_semantics=sem, vmem_limit_bytes=VMEM_LIMIT)


def _all_gather(arrs, name):
    n = len(arrs)

    def body(*refs):
        ins, outs = refs[:n], refs[n:2 * n]
        send_sems, recv_sems, local_sems = refs[2 * n:]
        x, y, c = lax.axis_index("x"), lax.axis_index("y"), lax.axis_index("c")
        me, sibling = (x, y, c), (x, y, 1 - c)
        chips = [(1 - x, y), (x, 1 - y), (1 - x, 1 - y)]

        def slot(px, py, pc):
            return 4 * px + 2 * py + pc

        def copy(a, k, block, to, src=None):
            dst = outs[a].at[slot(*block)]
            return pltpu.make_async_remote_copy(
                src_ref=dst if src is None else src, dst_ref=dst,
                send_sem=send_sems.at[a * 7 + k], recv_sem=recv_sems.at[a * 7 + k],
                device_id=to, device_id_type=MESH)

        started = []
        for a in range(n):
            mine = pltpu.make_async_copy(ins[a], outs[a].at[slot(*me)], local_sems.at[a])
            mine.start()
            started.append(mine)
        first = []
        for a in range(n):
            first.append(copy(a, 0, me, sibling, src=ins[a]))
            first += [copy(a, 1 + j, me, (*chip, c), src=ins[a]) for j, chip in enumerate(chips)]
        for cp in first:
            cp.start()
        passed = []
        for a in range(n):
            for j, chip in enumerate(chips):
                copy(a, 1 + j, (*chip, c), me).wait_recv()
                fwd = copy(a, 4 + j, (*chip, c), sibling)
                fwd.start()
                passed.append(fwd)
        for a in range(n):
            copy(a, 0, sibling, me).wait_recv()
            for j, chip in enumerate(chips):
                copy(a, 4 + j, (*chip, 1 - c), me).wait_recv()
        for cp in first + passed:
            cp.wait_send()
        for mine in started:
            mine.wait()

    hbm = pl.BlockSpec(memory_space=pl.ANY)
    return pl.pallas_call(
        body, name=name,
        out_shape=[jax.ShapeDtypeStruct((NDEV,) + a.shape, a.dtype) for a in arrs],
        in_specs=[hbm] * n, out_specs=[hbm] * n,
        scratch_shapes=[pltpu.SemaphoreType.DMA((7 * n,)), pltpu.SemaphoreType.DMA((7 * n,)),
                        pltpu.SemaphoreType.DMA((n,))],
    )(*arrs)


def _peers():
    x, y, c = lax.axis_index("x"), lax.axis_index("y"), lax.axis_index("c")
    peers = [(x ^ ((k >> 2) & 1), y ^ ((k >> 1) & 1), c ^ (k & 1)) for k in range(1, NDEV)]
    return (x, y, c), 4 * x + 2 * y + c, peers


_HBM = pl.BlockSpec(memory_space=pltpu.HBM)
_SEM = pl.BlockSpec(memory_space=pltpu.SEMAPHORE)
_EFFECT = pltpu.SideEffectType.DATAFLOW_SIDE_EFFECTING


def _transfer_start(arrs, gather, name, deps=(), direct=False):
    n, nd = len(arrs), len(deps)
    shapes = [((NDEV,) + a.shape) if gather else a.shape for a in arrs]

    def body(*refs):
        ins, lands = refs[:n], refs[n:2 * n]
        send_sems, recv_sems, local_sems = refs[2 * n + nd:2 * n + nd + 3]
        token = refs[-1]
        (x, y, c), my, peers = _peers()
        if gather and not direct:
            peers = [(x, y, 1 - c), (1 - x, y, c), (x, 1 - y, c), (1 - x, 1 - y, c)]
        for a in range(n):
            own = ins[a] if gather else ins[a].at[my]
            pltpu.make_async_copy(own, lands[a].at[my], local_sems.at[a]).start()
        for a in range(n):
            for peer in peers:
                src = ins[a] if gather else ins[a].at[4 * peer[0] + 2 * peer[1] + peer[2]]
                pltpu.make_async_remote_copy(
                    src_ref=src, dst_ref=lands[a].at[my], send_sem=send_sems.at[a], recv_sem=recv_sems.at[a],
                    device_id=peer, device_id_type=MESH).start()
        token[...] = jnp.zeros_like(token)

    out_shape = [pltpu.SemaphoreType.DMA((n,))] * 3
    out_shape += [pltpu.HBM(a.shape, a.dtype) for a in arrs]
    out_shape += [pltpu.HBM(s, a.dtype) for s, a in zip(shapes, arrs)]
    out_shape += [jax.ShapeDtypeStruct((8, 128), F32)]
    operands = [pltpu.with_memory_space_constraint(a, pltpu.HBM) for a in arrs]
    operands += [pltpu.with_memory_space_constraint(lax.empty(s, a.dtype), pltpu.HBM) for s, a in zip(shapes, arrs)]
    res = pl.pallas_call(
        body, name=name, out_shape=out_shape,
        in_specs=[_HBM] * (2 * n) + [pl.BlockSpec(memory_space=pl.ANY)] * nd,
        out_specs=[_SEM] * 3 + [_HBM] * (2 * n) + [pl.BlockSpec(memory_space=pltpu.VMEM)],
        input_output_aliases={i: 3 + i for i in range(2 * n)},
        compiler_params=pltpu.CompilerParams(has_side_effects=_EFFECT),
    )(*operands, *deps)
    return dict(sems=res[:3], src=res[3:3 + n], lands=res[3 + n:3 + 2 * n], token=res[-1], n=n,
                count=4 if gather and not direct else NDEV - 1)


def _forward_start(lands, name, deps=()):
    n, nd = len(lands), len(deps)

    def body(*refs):
        zone = refs[:n]
        send_sems, recv_sems = refs[n + nd:n + nd + 2]
        token = refs[-1]
        (x, y, c), _, _ = _peers()
        for a in range(n):
            for px, py in ((1 - x, y), (x, 1 - y), (1 - x, 1 - y)):
                block = zone[a].at[4 * px + 2 * py + c]
                pltpu.make_async_remote_copy(
                    src_ref=block, dst_ref=block, send_sem=send_sems.at[a], recv_sem=recv_sems.at[a],
                    device_id=(x, y, 1 - c), device_id_type=MESH).start()
        token[...] = jnp.zeros_like(token)

    res = pl.pallas_call(
        body, name=name,
        out_shape=[pltpu.SemaphoreType.DMA((n,))] * 2 + [pltpu.HBM(a.shape, a.dtype) for a in lands]
        + [jax.ShapeDtypeStruct((8, 128), F32)],
        in_specs=[_HBM] * n + [pl.BlockSpec(memory_space=pl.ANY)] * nd,
        out_specs=[_SEM] * 2 + [_HBM] * n + [pl.BlockSpec(memory_space=pltpu.VMEM)],
        input_output_aliases={i: 2 + i for i in range(n)},
        compiler_params=pltpu.CompilerParams(has_side_effects=_EFFECT),
    )(*lands, *deps)
    return dict(sems=res[:2], src=[], lands=res[2:2 + n], token=res[-1], n=n, count=3)


def _transfer_wait(handle, after, name):
    n, count = handle["n"], handle["count"]
    src, lands, sems = list(handle["src"]), list(handle["lands"]), list(handle["sems"])
    ns = len(src)

    def body(*refs):
        zone = refs[ns:ns + n]
        sem_refs = refs[ns + n:ns + n + len(sems)]
        me, _, _ = _peers()
        for a in range(n):
            moved = zone[a].at[pl.ds(0, count)]
            both = pltpu.make_async_remote_copy(
                src_ref=moved, dst_ref=moved, send_sem=sem_refs[0].at[a], recv_sem=sem_refs[1].at[a],
                device_id=me, device_id_type=MESH)
            both.wait_send()
            both.wait_recv()
            if len(sems) == 3:
                pltpu.make_async_copy(zone[a].at[0], zone[a].at[1], sem_refs[2].at[a]).wait()

    res = pl.pallas_call(
        body, name=name,
        out_shape=[pltpu.HBM(a.shape, a.dtype) for a in src + lands],
        in_specs=[_HBM] * (ns + n) + [_SEM] * len(sems) + [pl.BlockSpec(memory_space=pl.ANY)],
        out_specs=[_HBM] * (ns + n),
        input_output_aliases={i: i for i in range(ns + n)},
        compiler_params=pltpu.CompilerParams(has_side_effects=_EFFECT),
    )(*src, *lands, *sems, after)
    return list(res[ns:])


def _ffn_fwd(h, gain, wg, wu, wd, tm):
    t = h.shape[0]
    nj = FF // FB

    def body(h_ref, g_ref, wg_ref, wu_ref, wd_ref, out_ref, xn_ref, a_ref, b_ref, acc_ref):
        j = pl.program_id(1)

        @pl.when(j == 0)
        def _():
            hh = h_ref[...]
            rstd = lax.rsqrt(jnp.mean(hh * hh, axis=-1, keepdims=True) + EPS)
            xn_ref[...] = (hh * rstd * g_ref[...]).astype(xn_ref.dtype)
            acc_ref[...] = jnp.zeros_like(acc_ref)

        sub = min(SUB, tm)
        for r in range(tm // sub):
            rows = slice(r * sub, (r + 1) * sub)
            xn = xn_ref[rows, :]
            a = _mm_nt(xn, wg_ref[...])
            b = _mm_nt(xn, wu_ref[...])
            a_ref[rows, :] = a.astype(a_ref.dtype)
            b_ref[rows, :] = b.astype(b_ref.dtype)
            acc_ref[rows, :] += _mm(a * _sigmoid(a) * b, wd_ref[...])

        @pl.when(j == nj - 1)
        def _():
            out_ref[...] = h_ref[...] + 0.5 * acc_ref[...]

    wspec = pl.BlockSpec((FB, D), lambda i, j: (j, 0))
    return pl.pallas_call(
        body, name="ffn_fwd", grid=(t // tm, nj),
        in_specs=[pl.BlockSpec((tm, D), lambda i, j: (i, 0)),
                  pl.BlockSpec((1, D), lambda i, j: (0, 0)), wspec, wspec, wspec],
        out_specs=[pl.BlockSpec((tm, D), lambda i, j: (i, 0)),
                   pl.BlockSpec((tm, D), lambda i, j: (i, 0)),
                   pl.BlockSpec((tm, FB), lambda i, j: (i, j)),
                   pl.BlockSpec((tm, FB), lambda i, j: (i, j))],
        out_shape=[jax.ShapeDtypeStruct((t, D), F32), jax.ShapeDtypeStruct((t, D), SAVE),
                   jax.ShapeDtypeStruct((t, FF), SAVE), jax.ShapeDtypeStruct((t, FF), SAVE)],
        scratch_shapes=[pltpu.VMEM((tm, D), F32)],
        compiler_params=_params(("parallel", "arbitrary")),
    )(h, gain, wg, wu, wd)


def _ffn_bwd_x(dout, h, gain, a_sv, b_sv, wg, wu, wd, tm):
    t = h.shape[0]
    nj = FF // FB

    def body(dout_ref, h_ref, g_ref, a_ref, b_ref, wg_ref, wu_ref, wd_ref,
             dh_ref, dgain_ref, dy_ref, da_ref, db_ref, s_ref, acc_ref):
        i, j = pl.program_id(0), pl.program_id(1)

        @pl.when((i == 0) & (j == 0))
        def _():
            dgain_ref[...] = jnp.zeros_like(dgain_ref)

        @pl.when(j == 0)
        def _():
            dy_ref[...] = (0.5 * dout_ref[...]).astype(dy_ref.dtype)
            acc_ref[...] = jnp.zeros_like(acc_ref)

        sub = min(SUB, tm)
        for r in range(tm // sub):
            rows = slice(r * sub, (r + 1) * sub)
            ds = _mm_nt(dy_ref[rows, :], wd_ref[...])
            a, b = a_ref[rows, :].astype(F32), b_ref[rows, :].astype(F32)
            sg = _sigmoid(a)
            sa = a * sg
            da = (ds * b * (sg * (1.0 + a * (1.0 - sg)))).astype(MXU)
            db = (ds * sa).astype(MXU)
            da_ref[rows, :] = da.astype(da_ref.dtype)
            db_ref[rows, :] = db.astype(db_ref.dtype)
            s_ref[rows, :] = (sa * b).astype(s_ref.dtype)
            acc_ref[rows, :] += _mm(da, wg_ref[...]) + _mm(db, wu_ref[...])

        @pl.when(j == nj - 1)
        def _():
            dh, dg = _rms_bwd(acc_ref[...], h_ref[...], g_ref[...])
            dh_ref[...] = dout_ref[...] + dh
            dgain_ref[...] += dg

    tok = pl.BlockSpec((tm, D), lambda i, j: (i, 0))
    act = pl.BlockSpec((tm, FB), lambda i, j: (i, j))
    wspec = pl.BlockSpec((FB, D), lambda i, j: (j, 0))
    return pl.pallas_call(
        body, name="ffn_bwd_x", grid=(t // tm, nj),
        in_specs=[tok, tok, pl.BlockSpec((1, D), lambda i, j: (0, 0)), act, act, wspec, wspec, wspec],
        out_specs=[tok, pl.BlockSpec((1, D), lambda i, j: (0, 0)), tok, act, act, act],
        out_shape=[jax.ShapeDtypeStruct((t, D), F32), jax.ShapeDtypeStruct((1, D), F32),
                   jax.ShapeDtypeStruct((t, D), SAVE)] + [jax.ShapeDtypeStruct((t, FF), SAVE)] * 3,
        scratch_shapes=[pltpu.VMEM((tm, D), F32)],
        compiler_params=_params(("arbitrary", "arbitrary")),
    )(dout, h, gain, a_sv, b_sv, wg, wu, wd)


def _ffn_bwd_w(xn, dy, da, db, s, tm):
    t = xn.shape[0]
    nt = t // tm
    nj = FF // FB

    def body(xn_ref, dy_ref, da_ref, db_ref, s_ref, dwg_ref, dwu_ref, dwd_ref, ag_scr, au_scr, ad_scr):
        i, j = pl.program_id(0), pl.program_id(1)
        rows = pl.ds(pl.multiple_of(j * FB, FB), FB)
        xn = xn_ref[...]
        new = ((ag_scr, _mm_tn(da_ref[...], xn)), (au_scr, _mm_tn(db_ref[...], xn)),
               (ad_scr, _mm_tn(s_ref[...], dy_ref[...])))

        @pl.when(i == 0)
        def _():
            for ref, val in new:
                ref[rows, :] = val

        @pl.when(i > 0)
        def _():
            for ref, val in new:
                ref[rows, :] += val

        @pl.when(i == nt - 1)
        def _():
            for out, ref in ((dwg_ref, ag_scr), (dwu_ref, au_scr), (dwd_ref, ad_scr)):
                out[...] = ref[rows, :].astype(out.dtype)

    tok = pl.BlockSpec((tm, D), lambda i, j: (i, 0))
    act = pl.BlockSpec((tm, FB), lambda i, j: (i, j))
    wspec = pl.BlockSpec((FB, D), lambda i, j: (jnp.where(i == nt - 1, j, 0), 0))
    return pl.pallas_call(
        body, name="ffn_bwd_w", grid=(nt, nj),
        in_specs=[tok, tok, act, act, act], out_specs=[wspec] * 3,
        out_shape=[jax.ShapeDtypeStruct((FF, D), WIRE)] * 3,
        scratch_shapes=[pltpu.VMEM((FF, D), F32)] * 3,
        compiler_params=_params(("arbitrary", "arbitrary")),
    )(xn, dy, da, db, s)


def _inproj_fwd(h, gain, win, tm):
    t = h.shape[0]

    def body(h_ref, g_ref, w_ref, z_ref, xn_ref):
        hh = h_ref[...]
        rstd = lax.rsqrt(jnp.mean(hh * hh, axis=-1, keepdims=True) + EPS)
        xn = (hh * rstd * g_ref[...]).astype(MXU)
        xn_ref[...] = xn.astype(xn_ref.dtype)
        for j in range(DIN // ZB):
            z_ref[:, j * ZB:(j + 1) * ZB] = _mm_nt(xn, w_ref[j * ZB:(j + 1) * ZB, :])

    return pl.pallas_call(
        body, name="inproj_fwd", grid=(t // tm,),
        in_specs=[pl.BlockSpec((tm, D), lambda i: (i, 0)),
                  pl.BlockSpec((1, D), lambda i: (0, 0)),
                  pl.BlockSpec((DIN, D), lambda i: (0, 0))],
        out_specs=[pl.BlockSpec((tm, DIN), lambda i: (i, 0)),
                   pl.BlockSpec((tm, D), lambda i: (i, 0))],
        out_shape=[jax.ShapeDtypeStruct((t, DIN), F32), jax.ShapeDtypeStruct((t, D), SAVE)],
        compiler_params=_params(("parallel",)),
    )(h, gain, win)


def _inproj_bwd_x(dres, dz, h, gain, win, tm):
    t = h.shape[0]

    def body(dres_ref, dz_ref, h_ref, g_ref, w_ref, dh_ref, dgain_ref):
        @pl.when(pl.program_id(0) == 0)
        def _():
            dgain_ref[...] = jnp.zeros_like(dgain_ref)

        dh, dg = _rms_bwd(_mm(dz_ref[...], w_ref[...]), h_ref[...], g_ref[...])
        dh_ref[...] = dres_ref[...] + dh
        dgain_ref[...] += dg

    return pl.pallas_call(
        body, name="inproj_bwd_x", grid=(t // tm,),
        in_specs=[pl.BlockSpec((tm, D), lambda i: (i, 0)),
                  pl.BlockSpec((tm, DIN), lambda i: (i, 0)),
                  pl.BlockSpec((tm, D), lambda i: (i, 0)),
                  pl.BlockSpec((1, D), lambda i: (0, 0)),
                  pl.BlockSpec((DIN, D), lambda i: (0, 0))],
        out_specs=[pl.BlockSpec((tm, D), lambda i: (i, 0)),
                   pl.BlockSpec((1, D), lambda i: (0, 0))],
        out_shape=[jax.ShapeDtypeStruct((t, D), F32), jax.ShapeDtypeStruct((1, D), F32)],
        compiler_params=_params(("arbitrary",)),
    )(dres, dz, h, gain, win)


def _inproj_bwd_w(xn, dz, tm):
    t = xn.shape[0]
    nt = t // tm

    def body(xn_ref, dz_ref, dw_ref, acc_scr):
        i = pl.program_id(1)

        @pl.when(i == 0)
        def _():
            acc_scr[...] = jnp.zeros_like(acc_scr)

        acc_scr[...] += _mm_tn(dz_ref[...], xn_ref[...])

        @pl.when(i == nt - 1)
        def _():
            dw_ref[...] = acc_scr[...].astype(dw_ref.dtype)

    return pl.pallas_call(
        body, name="inproj_bwd_w", grid=(DIN // ZB, nt),
        in_specs=[pl.BlockSpec((tm, D), lambda j, i: (i, 0)),
                  pl.BlockSpec((tm, ZB), lambda j, i: (i, j))],
        out_specs=pl.BlockSpec((ZB, D), lambda j, i: (j, 0)),
        out_shape=jax.ShapeDtypeStruct((DIN, D), WIRE),
        scratch_shapes=[pltpu.VMEM((ZB, D), F32)],
        compiler_params=_params(("parallel", "arbitrary")),
    )(xn, dz)


def _outproj_fwd(h, oa, ob, oc, wout, tm):
    t = h.shape[0]

    def body(h_ref, oa_ref, ob_ref, oc_ref, w_ref, out_ref):
        ym = jnp.concatenate([oa_ref[...], ob_ref[...], oc_ref[...]], axis=1)
        out_ref[...] = h_ref[...] + _mm(ym, w_ref[...])

    return pl.pallas_call(
        body, name="outproj_fwd", grid=(t // tm,),
        in_specs=[pl.BlockSpec((tm, D), lambda i: (i, 0)),
                  pl.BlockSpec((tm, DA), lambda i: (i, 0)),
                  pl.BlockSpec((tm, DB), lambda i: (i, 0)),
                  pl.BlockSpec((tm, DC), lambda i: (i, 0)),
                  pl.BlockSpec((D, D), lambda i: (0, 0))],
        out_specs=pl.BlockSpec((tm, D), lambda i: (i, 0)),
        out_shape=jax.ShapeDtypeStruct((t, D), F32),
        compiler_params=_params(("parallel",)),
    )(h, oa, ob, oc, wout)


def _outproj_bwd(dh, oa, ob, oc, wout, tm):
    t = dh.shape[0]
    nt = t // tm

    def body(dh_ref, oa_ref, ob_ref, oc_ref, w_ref, da_ref, db_ref, dc_ref, dw_ref, acc_scr):
        i = pl.program_id(0)

        @pl.when(i == 0)
        def _():
            acc_scr[...] = jnp.zeros_like(acc_scr)

        d16 = dh_ref[...].astype(MXU)
        dym = _mm_nt(d16, w_ref[...])
        da_ref[...] = dym[:, :DA]
        db_ref[...] = dym[:, DA:DA + DB]
        dc_ref[...] = dym[:, DA + DB:]
        ym = jnp.concatenate([oa_ref[...], ob_ref[...], oc_ref[...]], axis=1)
        acc_scr[...] += _mm_tn(ym, d16)

        @pl.when(i == nt - 1)
        def _():
            dw_ref[...] = acc_scr[...].astype(dw_ref.dtype)

    return pl.pallas_call(
        body, name="outproj_bwd", grid=(nt,),
        in_specs=[pl.BlockSpec((tm, D), lambda i: (i, 0)),
                  pl.BlockSpec((tm, DA), lambda i: (i, 0)),
                  pl.BlockSpec((tm, DB), lambda i: (i, 0)),
                  pl.BlockSpec((tm, DC), lambda i: (i, 0)),
                  pl.BlockSpec((D, D), lambda i: (0, 0))],
        out_specs=[pl.BlockSpec((tm, DA), lambda i: (i, 0)),
                   pl.BlockSpec((tm, DB), lambda i: (i, 0)),
                   pl.BlockSpec((tm, DC), lambda i: (i, 0)),
                   pl.BlockSpec((D, D), lambda i: (0, 0))],
        out_shape=[jax.ShapeDtypeStruct((t, DA), F32), jax.ShapeDtypeStruct((t, DB), F32),
                   jax.ShapeDtypeStruct((t, DC), F32), jax.ShapeDtypeStruct((D, D), WIRE)],
        scratch_shapes=[pltpu.VMEM((D, D), F32)],
        compiler_params=_params(("arbitrary",)),
    )(dh, oa, ob, oc, wout)


def _lower_bounds(logits):
    depth, n = logits.shape

    def body(l_ref, lb_ref, p_ref):
        rows = [l_ref[l:l + 1, :] for l in range(depth)]
        mx = functools.reduce(jnp.maximum, rows)
        ex = [jnp.exp(r - mx) for r in rows]
        den = functools.reduce(lambda u, v: u + v, ex)
        acc = jnp.zeros_like(den)
        for l in range(depth):
            p = ex[l] / den
            p_ref[l:l + 1, :] = p
            if l > 0:
                acc = acc + p
            lb_ref[l:l + 1, :] = acc

    return pl.pallas_call(
        body, name="lower_bounds",
        out_shape=[jax.ShapeDtypeStruct((depth, n), F32), jax.ShapeDtypeStruct((depth, n), F32)],
    )(logits)


def _lower_bounds_bwd(p, dlb):
    depth, n = p.shape

    def body(p_ref, d_ref, out_ref):
        ps = [p_ref[l:l + 1, :] for l in range(depth)]
        ds = [d_ref[l:l + 1, :] for l in range(depth)]
        dp = [jnp.zeros_like(ps[0]) for _ in range(depth)]
        run = jnp.zeros_like(ps[0])
        for l in range(depth - 1, 0, -1):
            run = run + ds[l]
            dp[l] = run
        dot = functools.reduce(lambda u, v: u + v, [ps[l] * dp[l] for l in range(depth)])
        for l in range(depth):
            out_ref[l:l + 1, :] = ps[l] * (dp[l] - dot)

    return pl.pallas_call(body, name="lower_bounds_bwd", out_shape=jax.ShapeDtypeStruct((depth, n), F32))(p, dlb)


def _hgrn_block(z_ref, lb_ref, rb):
    q, fl = z_ref[:, 0:DA], z_ref[:, DA:2 * DA]
    lb = lb_ref[...]
    sq = _sigmoid(q)
    qs = q * sq
    sg = _sigmoid(fl)
    f = lb + (1.0 - lb) * sg
    k = 1.0 - f
    lf = jnp.log(f)
    row, col = _iota((rb, rb), 0), _iota((rb, rb), 1)
    same = (row // ACH) == (col // ACH)
    causal = same & (row >= col)
    b = _mm_exact_l(jnp.where(causal, 1.0, 0.0).astype(MXU), lf)
    bend = _mm_exact_l(jnp.where(same, 1.0, 0.0).astype(MXU), lf)
    r = 0.5 * bend
    eq, ek, eb, ed = jnp.exp(b - r), jnp.exp(r - b), jnp.exp(b), jnp.exp(bend - b)
    return dict(q=q, lb=lb, sq=sq, sg=sg, f=f, bend=bend, eq=eq, ek=ek, eb=eb, ed=ed,
                qt=qs * eq, kt=k * ek, qe=qs * eb, kd=k * ed, same=same, causal=causal)


def _hgrn_fwd(z, lb, gain):
    t = z.shape[0]
    nc = t // ACH
    cb = min(ACB, nc)
    rb = cb * ACH

    def body(z_ref, lb_ref, g_ref, o_ref, oa_ref, st_ref, st_scr):
        @pl.when(pl.program_id(0) == 0)
        def _():
            st_scr[...] = jnp.zeros_like(st_scr)

        c = _hgrn_block(z_ref, lb_ref, rb)
        for hd in range(NH):
            cols = slice(hd * HD, (hd + 1) * HD)
            v = z_ref[:, 2 * DA + hd * HD:2 * DA + (hd + 1) * HD]
            gg = z_ref[:, 3 * DA + hd * HD:3 * DA + (hd + 1) * HD]
            att = jnp.where(c["causal"], _mm_nt(c["qt"][:, cols], c["kt"][:, cols]), 0.0)
            o_in = _mm(att, v)
            qe, kd, bend = c["qe"][:, cols], c["kd"][:, cols], c["bend"][:, cols]
            st = st_scr[hd]
            outs = []
            for cc in range(cb):
                rows = slice(cc * ACH, (cc + 1) * ACH)
                st_ref[cc, hd] = st
                outs.append(o_in[rows] + _mm_nt(qe[rows], st))
                decay = jnp.exp(jnp.max(bend[rows], axis=0, keepdims=True))
                st = st * decay + _mm_tn(v[rows], kd[rows])
            st_scr[hd] = st
            o = jnp.concatenate(outs, axis=0)
            o_ref[:, cols] = o
            rstd = lax.rsqrt(jnp.mean(o * o, axis=-1, keepdims=True) + EPS)
            oa_ref[:, cols] = (o * rstd * g_ref[:, cols] * (gg * _sigmoid(gg))).astype(oa_ref.dtype)

    return pl.pallas_call(
        body, name="hgrn_fwd", grid=(nc // cb,),
        in_specs=[pl.BlockSpec((rb, 4 * DA), lambda c: (c, 0)),
                  pl.BlockSpec((1, DA), lambda c: (0, 0)),
                  pl.BlockSpec((1, DA), lambda c: (0, 0))],
        out_specs=[pl.BlockSpec((rb, DA), lambda c: (c, 0)),
                   pl.BlockSpec((rb, DA), lambda c: (c, 0)),
                   pl.BlockSpec((cb, NH, HD, HD), lambda c: (c, 0, 0, 0))],
        out_shape=[jax.ShapeDtypeStruct((t, DA), F32), jax.ShapeDtypeStruct((t, DA), SAVE),
                   jax.ShapeDtypeStruct((nc, NH, HD, HD), F32)],
        scratch_shapes=[pltpu.VMEM((NH, HD, HD), F32)],
        compiler_params=_params(("arbitrary",)),
    )(z, lb, gain)


def _hgrn_bwd(z, lb, gain, o, states, doa):
    t = z.shape[0]
    nc = t // ACH
    cb = min(ACB, nc)
    rb = cb * ACH
    nblk = nc // cb

    def body(z_ref, lb_ref, g_ref, o_ref, st_ref, doa_ref, dz_ref, dgain_ref, dlb_ref, dst_scr):
        @pl.when(pl.program_id(0) == 0)
        def _():
            dst_scr[...] = jnp.zeros_like(dst_scr)
            dgain_ref[...] = jnp.zeros_like(dgain_ref)
            dlb_ref[...] = jnp.zeros_like(dlb_ref)

        c = _hgrn_block(z_ref, lb_ref, rb)
        dbs, dqss, dks = [], [], []
        for hd in range(NH):
            cols = slice(hd * HD, (hd + 1) * HD)
            v = z_ref[:, 2 * DA + hd * HD:2 * DA + (hd + 1) * HD]
            gg = z_ref[:, 3 * DA + hd * HD:3 * DA + (hd + 1) * HD]
            qt, kt, qe, kd, bend = (c[n][:, cols] for n in ("qt", "kt", "qe", "kd", "bend"))
            o = o_ref[:, cols]
            do_a = doa_ref[:, cols]
            gain = g_ref[:, cols]
            sgg = _sigmoid(gg)
            silu_g = gg * sgg
            rstd = lax.rsqrt(jnp.mean(o * o, axis=-1, keepdims=True) + EPS)
            n = o * rstd
            dn = do_a * gain * silu_g
            dg = do_a * n * gain * (sgg * (1.0 + gg * (1.0 - sgg)))
            dgain_ref[:, cols] += jnp.sum(do_a * silu_g * n, axis=0, keepdims=True)
            d_o = rstd * (dn - n * jnp.mean(dn * n, axis=-1, keepdims=True))

            att = jnp.where(c["causal"], _mm_nt(qt, kt), 0.0)
            datt = jnp.where(c["causal"], _mm_nt(d_o, v), 0.0)
            dv_in = _mm_tn(att, d_o)
            dqt = _mm(datt, kt)
            dkt = _mm_tn(datt, qt)
            dsp = dst_scr[hd]
            dvs, dqes, dkds, dbends = [None] * cb, [None] * cb, [None] * cb, [None] * cb
            for cc in reversed(range(cb)):
                rows = slice(cc * ACH, (cc + 1) * ACH)
                st = st_ref[cc, hd]
                dvs[cc] = dv_in[rows] + _mm_nt(kd[rows], dsp)
                dqes[cc] = _mm(d_o[rows], st)
                dkds[cc] = _mm(v[rows], dsp)
                decay = jnp.exp(jnp.max(bend[rows], axis=0, keepdims=True))
                dbend = (decay * jnp.sum(st * dsp, axis=0, keepdims=True)
                         + jnp.sum(dkds[cc] * kd[rows], axis=0, keepdims=True))
                dbends[cc] = jnp.broadcast_to(dbend, (ACH, HD))
                dsp = dsp * decay + _mm_tn(d_o[rows], qe[rows])
            dst_scr[hd] = dsp
            dv, dqe, dkd, dbend = (jnp.concatenate(p, axis=0) for p in (dvs, dqes, dkds, dbends))
            dbs.append((dqt * qt + dqe * qe - dkt * kt - dkd * kd, dbend))
            dqss.append(dqt * c["eq"][:, cols] + dqe * c["eb"][:, cols])
            dks.append(dkt * c["ek"][:, cols] + dkd * c["ed"][:, cols])
            c0 = hd * HD
            dz_ref[:, 2 * DA + c0:2 * DA + c0 + HD] = dv.astype(dz_ref.dtype)
            dz_ref[:, 3 * DA + c0:3 * DA + c0 + HD] = dg.astype(dz_ref.dtype)

        db = jnp.concatenate([p[0] for p in dbs], axis=1)
        dbend = jnp.concatenate([p[1] for p in dbs], axis=1)
        dqs, dk = jnp.concatenate(dqss, axis=1), jnp.concatenate(dks, axis=1)
        row, col = _iota((rb, rb), 0), _iota((rb, rb), 1)
        upper = jnp.where(c["same"] & (row <= col), 1.0, 0.0).astype(MXU)
        dlf = _mm_exact_l(upper, db) + dbend
        df = dlf / c["f"] - dk
        sg, sq, q = c["sg"], c["sq"], c["q"]
        dlb_ref[...] += jnp.sum(df * (1.0 - sg), axis=0, keepdims=True)
        dz_ref[:, DA:2 * DA] = (df * (1.0 - c["lb"]) * sg * (1.0 - sg)).astype(dz_ref.dtype)
        dz_ref[:, 0:DA] = (dqs * (sq * (1.0 + q * (1.0 - sq)))).astype(dz_ref.dtype)

    rev = lambda c: (nblk - 1 - c, 0)
    return pl.pallas_call(
        body, name="hgrn_bwd", grid=(nblk,),
        in_specs=[pl.BlockSpec((rb, 4 * DA), rev),
                  pl.BlockSpec((1, DA), lambda c: (0, 0)),
                  pl.BlockSpec((1, DA), lambda c: (0, 0)),
                  pl.BlockSpec((rb, DA), rev),
                  pl.BlockSpec((cb, NH, HD, HD), lambda c: (nblk - 1 - c, 0, 0, 0)),
                  pl.BlockSpec((rb, DA), rev)],
        out_specs=[pl.BlockSpec((rb, 4 * DA), rev),
                   pl.BlockSpec((1, DA), lambda c: (0, 0)),
                   pl.BlockSpec((1, DA), lambda c: (0, 0))],
        out_shape=[jax.ShapeDtypeStruct((t, DIN), SAVE), jax.ShapeDtypeStruct((1, DA), F32),
                   jax.ShapeDtypeStruct((1, DA), F32)],
        scratch_shapes=[pltpu.VMEM((NH, HD, HD), F32)],
        compiler_params=_params(("arbitrary",)),
    )(z, lb, gain, o, states, doa)


def _shift_down(prev8, x, k):
    cat = jnp.concatenate([prev8, x], axis=0)
    return pltpu.roll(cat, k, axis=0)[8:, :]


def _shift_up(x, next8, k):
    n = x.shape[0]
    cat = jnp.concatenate([x, next8], axis=0)
    return pltpu.roll(cat, n + 8 - k, axis=0)[:n, :]


def _lru_gates(x, prev8, cw_ref, vec_ref, wa_ref, wx_ref):
    xs = [x, _shift_down(prev8, x, 1), _shift_down(prev8, x, 2), _shift_down(prev8, x, 3)]
    xc = vec_ref[0:1, :] + cw_ref[3:4, :] * xs[0] + cw_ref[2:3, :] * xs[1] + cw_ref[1:2, :] * xs[2] + cw_ref[0:1, :] * xs[3]
    r = _sigmoid(_mm(xc, wa_ref[...]) + vec_ref[1:2, :])
    gi = _sigmoid(_mm(xc, wx_ref[...]) + vec_ref[2:3, :])
    lam = vec_ref[3:4, :]
    sp = jnp.maximum(-lam, 0.0) + jnp.log(1.0 + jnp.exp(-jnp.abs(lam)))
    la = -LRU_C * r * sp
    a = jnp.exp(la)
    mult = jnp.sqrt(-_expm1(2.0 * la))
    return xs, xc, r, gi, sp, a, mult


def _scan_down(a, u):
    n = a.shape[0]
    row = _iota(a.shape, 0)
    s = 1
    while s < n:
        keep = row >= s
        ash = jnp.where(keep, pltpu.roll(a, s, axis=0), 1.0)
        ush = jnp.where(keep, pltpu.roll(u, s, axis=0), 0.0)
        u = a * ush + u
        a = a * ash
        s *= 2
    return a, u


def _scan_up(a, u):
    n = a.shape[0]
    row = _iota(a.shape, 0)
    s = 1
    while s < n:
        keep = row < n - s
        ash = jnp.where(keep, pltpu.roll(a, n - s, axis=0), 1.0)
        ush = jnp.where(keep, pltpu.roll(u, n - s, axis=0), 0.0)
        u = a * ush + u
        a = a * ash
        s *= 2
    return a, u


def _lru_fwd(z, cw, vec, wa, wx, tb):
    t = z.shape[0]
    xcol, gcol = (4 * DA) // DB, (4 * DA) // DB + 1

    def body(x_ref, gate_ref, cw_ref, vec_ref, wa_ref, wx_ref, ob_ref, h_ref, xprev_scr, hc_scr):
        @pl.when(pl.program_id(0) == 0)
        def _():
            xprev_scr[...] = jnp.zeros_like(xprev_scr)
            hc_scr[...] = jnp.zeros_like(hc_scr)

        x = x_ref[...]
        _, xc, _, gi, _, a, mult = _lru_gates(x, xprev_scr[...], cw_ref, vec_ref, wa_ref, wx_ref)
        acum, hloc = _scan_down(a, mult * gi * xc)
        h = hloc + acum * hc_scr[0:1, :]
        h_ref[...] = h
        hc_scr[...] = jnp.broadcast_to(_row(h, tb - 1), hc_scr.shape)
        xprev_scr[...] = x[tb - 8:, :]
        y = h * _gelu(gate_ref[...])
        ms = _mm_exact_r(y * y, _group_matrix(DB, 1.0 / GRP).astype(MXU))
        ob_ref[...] = (y * lax.rsqrt(ms + EPS) * vec_ref[4:5, :]).astype(ob_ref.dtype)

    return pl.pallas_call(
        body, name="lru_fwd", grid=(t // tb,),
        in_specs=[pl.BlockSpec((tb, DB), lambda i: (i, xcol)),
                  pl.BlockSpec((tb, DB), lambda i: (i, gcol)),
                  pl.BlockSpec((8, DB), lambda i: (0, 0)),
                  pl.BlockSpec((8, DB), lambda i: (0, 0)),
                  pl.BlockSpec((DB, DB), lambda i: (0, 0)),
                  pl.BlockSpec((DB, DB), lambda i: (0, 0))],
        out_specs=[pl.BlockSpec((tb, DB), lambda i: (i, 0)),
                   pl.BlockSpec((tb, DB), lambda i: (i, 0))],
        out_shape=[jax.ShapeDtypeStruct((t, DB), SAVE), jax.ShapeDtypeStruct((t, DB), F32)],
        scratch_shapes=[pltpu.VMEM((8, DB), F32), pltpu.VMEM((8, DB), F32)],
        compiler_params=_params(("arbitrary",)),
    )(z, z, cw, vec, wa, wx)


def _lru_bwd(z, hseq, dob, cw, vec, wa, wx, dz, tb):
    t = z.shape[0]
    nb = t // tb
    xcol, gcol = (4 * DA) // DB, (4 * DA) // DB + 1
    per = tb // 8

    def body(x_ref, xh_ref, gate_ref, h_ref, hh_ref, dob_ref, cw_ref, vec_ref, wa_ref, wx_ref, _,
             dz_ref, dcw_ref, dvec_ref, dwa_ref, dwx_ref, gc_scr, an_scr, dxc_scr):
        step = pl.program_id(0)
        blk = nb - 1 - step

        @pl.when(step == 0)
        def _():
            for ref in (gc_scr, an_scr, dxc_scr, dcw_ref, dvec_ref, dwa_ref, dwx_ref):
                ref[...] = jnp.zeros_like(ref)

        first = (blk > 0).astype(F32)
        x = x_ref[...]
        xs, xc, r, gi, sp, a, mult = _lru_gates(x, xh_ref[...] * first, cw_ref, vec_ref, wa_ref, wx_ref)
        h = h_ref[...]
        hprev = _shift_down(hh_ref[...] * first, h, 1)
        ge, dge = _gelu_and_grad(gate_ref[...])
        y = h * ge
        gmat = _group_matrix(DB, 1.0 / GRP).astype(MXU)
        rstd = lax.rsqrt(_mm_exact_r(y * y, gmat) + EPS)
        n = y * rstd
        d_ob = dob_ref[...]
        dn = d_ob * vec_ref[4:5, :]
        dvec_ref[4:5, :] += jnp.sum(d_ob * n, axis=0, keepdims=True)
        dy = rstd * (dn - n * _mm_exact_r(dn * n, gmat))
        dh = dy * ge
        dgate = dy * h * dge

        row = _iota(a.shape, 0)
        anext = jnp.where(row == tb - 1, an_scr[0:1, :], pltpu.roll(a, tb - 1, axis=0))
        acum, gloc = _scan_up(anext, dh)
        g = gloc + acum * gc_scr[0:1, :]
        gc_scr[...] = jnp.broadcast_to(_row(g, 0), gc_scr.shape)
        an_scr[...] = jnp.broadcast_to(_row(a, 0), an_scr.shape)

        da = g * hprev
        dmult = g * gi * xc
        dgi = g * mult * xc
        dxc = g * mult * gi
        dla = da * a - dmult * (a * a) / mult
        dr = dla * (-LRU_C * sp)
        dsp = jnp.sum(dla * (-LRU_C * r), axis=0, keepdims=True)
        lam = vec_ref[3:4, :]
        dvec_ref[3:4, :] += -dsp * _sigmoid(-lam)
        dpa = dr * r * (1.0 - r)
        dpx = dgi * gi * (1.0 - gi)
        dwa_ref[...] += _mm_tn(xc, dpa)
        dwx_ref[...] += _mm_tn(xc, dpx)
        dvec_ref[1:2, :] += jnp.sum(dpa, axis=0, keepdims=True)
        dvec_ref[2:3, :] += jnp.sum(dpx, axis=0, keepdims=True)
        dxc = dxc + _mm_nt(dpa, wa_ref[...]) + _mm_nt(dpx, wx_ref[...])
        dvec_ref[0:1, :] += jnp.sum(dxc, axis=0, keepdims=True)
        for tap in range(4):
            dcw_ref[tap:tap + 1, :] += jnp.sum(dxc * xs[3 - tap], axis=0, keepdims=True)
        nxt = dxc_scr[...]
        dx = (cw_ref[3:4, :] * dxc + cw_ref[2:3, :] * _shift_up(dxc, nxt, 1)
              + cw_ref[1:2, :] * _shift_up(dxc, nxt, 2) + cw_ref[0:1, :] * _shift_up(dxc, nxt, 3))
        dxc_scr[...] = dxc[:8, :]
        dz_ref[:, :DB] = dx.astype(dz_ref.dtype)
        dz_ref[:, DB:] = dgate.astype(dz_ref.dtype)

    def halo(col):
        return lambda s: (jnp.maximum((nb - 1 - s) * per - 1, 0), col)

    const = lambda s: (0, 0)
    return pl.pallas_call(
        body, name="lru_bwd", grid=(nb,),
        in_specs=[pl.BlockSpec((tb, DB), lambda s: (nb - 1 - s, xcol)),
                  pl.BlockSpec((8, DB), halo(xcol)),
                  pl.BlockSpec((tb, DB), lambda s: (nb - 1 - s, gcol)),
                  pl.BlockSpec((tb, DB), lambda s: (nb - 1 - s, 0)),
                  pl.BlockSpec((8, DB), halo(0)),
                  pl.BlockSpec((tb, DB), lambda s: (nb - 1 - s, 0)),
                  pl.BlockSpec((8, DB), const), pl.BlockSpec((8, DB), const),
                  pl.BlockSpec((DB, DB), const), pl.BlockSpec((DB, DB), const),
                  pl.BlockSpec(memory_space=pl.ANY)],
        out_specs=[pl.BlockSpec((tb, 2 * DB), lambda s: (nb - 1 - s, (4 * DA) // (2 * DB))),
                   pl.BlockSpec((8, DB), const), pl.BlockSpec((8, DB), const),
                   pl.BlockSpec((DB, DB), const), pl.BlockSpec((DB, DB), const)],
        out_shape=[jax.ShapeDtypeStruct((t, DIN), SAVE), jax.ShapeDtypeStruct((8, DB), F32),
                   jax.ShapeDtypeStruct((8, DB), F32), jax.ShapeDtypeStruct((DB, DB), F32),
                   jax.ShapeDtypeStruct((DB, DB), F32)],
        scratch_shapes=[pltpu.VMEM((8, DB), F32), pltpu.VMEM((8, DB), F32), pltpu.VMEM((8, DB), F32)],
        input_output_aliases={10: 0},
        compiler_params=_params(("arbitrary",)),
    )(z, z, z, hseq, hseq, dob, cw, vec, wa, wx, dz)


def _sgu_chunk(u_in, v_in, w_ref, bias, gmat):
    uu, duu = _gelu_and_grad(u_in)
    vv, dvv = _gelu_and_grad(v_in)
    mu = _mm_exact_r(vv, gmat)
    dlt = vv - mu
    rstd_v = lax.rsqrt(_mm_exact_r(dlt * dlt, gmat) + EPS)
    vn = dlt * rstd_v
    col = _iota((CCH, DC), 1) // GRP
    causal = _iota((CCH, CCH), 0) >= _iota((CCH, CCH), 1)
    ws = [jnp.where(causal, w_ref[g], 0.0) for g in range(DC // GRP)]
    zz = bias
    for g, w in enumerate(ws):
        zz = zz + jnp.where(col == g, _mm(w, vn), 0.0)
    return uu, duu, dvv, rstd_v, vn, zz, ws, col, causal


def _sgu_fwd(z, w, bias, gain, tb):
    t = z.shape[0]
    ucol, vcol = (4 * DA + 2 * DB) // DC, (4 * DA + 2 * DB) // DC + 1

    def body(u_ref, v_ref, w_ref, b_ref, g_ref, oc_ref):
        gmat = _group_matrix(DC, 1.0 / GRP).astype(MXU)
        for ch in range(tb // CCH):
            rows = slice(ch * CCH, (ch + 1) * CCH)
            uu, _, _, _, _, zz, _, _, _ = _sgu_chunk(u_ref[rows, :], v_ref[rows, :], w_ref, b_ref[...], gmat)
            y = uu * zz
            ms = _mm_exact_r(y * y, gmat)
            oc_ref[rows, :] = (y * lax.rsqrt(ms + EPS) * g_ref[...]).astype(oc_ref.dtype)

    const = lambda i: (0, 0)
    return pl.pallas_call(
        body, name="sgu_fwd", grid=(t // tb,),
        in_specs=[pl.BlockSpec((tb, DC), lambda i: (i, ucol)),
                  pl.BlockSpec((tb, DC), lambda i: (i, vcol)),
                  pl.BlockSpec((DC // GRP, CCH, CCH), lambda i: (0, 0, 0)),
                  pl.BlockSpec((CCH, DC), const), pl.BlockSpec((1, DC), const)],
        out_specs=pl.BlockSpec((tb, DC), lambda i: (i, 0)),
        out_shape=jax.ShapeDtypeStruct((t, DC), SAVE),
        compiler_params=_params(("parallel",)),
    )(z, z, w, bias, gain)


def _sgu_bwd(z, doc, w, bias, gain, dz, tb):
    t = z.shape[0]
    nb = t // tb
    ucol, vcol = (4 * DA + 2 * DB) // DC, (4 * DA + 2 * DB) // DC + 1
    ng = DC // GRP

    def body(u_ref, v_ref, doc_ref, w_ref, b_ref, g_ref, _, dz_ref, dw_ref, dbias_ref, dgain_ref, dbsum_scr):
        i = pl.program_id(0)

        @pl.when(i == 0)
        def _():
            for ref in (dw_ref, dgain_ref, dbsum_scr):
                ref[...] = jnp.zeros_like(ref)

        gmat = _group_matrix(DC, 1.0 / GRP).astype(MXU)
        for ch in range(tb // CCH):
            rows = slice(ch * CCH, (ch + 1) * CCH)
            uu, duu, dvv, rstd_v, vn, zz, ws, col, causal = _sgu_chunk(
                u_ref[rows, :], v_ref[rows, :], w_ref, b_ref[...], gmat)
            y = uu * zz
            rstd = lax.rsqrt(_mm_exact_r(y * y, gmat) + EPS)
            n = y * rstd
            d_oc = doc_ref[rows, :]
            dn = d_oc * g_ref[...]
            dgain_ref[0:1, :] += jnp.sum(d_oc * n, axis=0, keepdims=True)
            dy = rstd * (dn - n * _mm_exact_r(dn * n, gmat))
            dzz = dy * uu
            dz_ref[rows, :DC] = (dy * zz * duu).astype(dz_ref.dtype)
            dbsum_scr[...] += dzz
            dvn = jnp.zeros_like(dzz)
            for g in range(ng):
                sel = col == g
                dvn = dvn + jnp.where(sel, _mm_tn(ws[g], dzz), 0.0)
                dw_ref[g] += jnp.where(causal, _mm_nt(jnp.where(sel, dzz, 0.0), vn), 0.0)
            dv = rstd_v * (dvn - _mm_exact_r(dvn, gmat) - vn * _mm_exact_r(dvn * vn, gmat))
            dz_ref[rows, DC:] = (dv * dvv).astype(dz_ref.dtype)

        @pl.when(i == nb - 1)
        def _():
            dbias_ref[...] = _mm_exact_r(dbsum_scr[...], _group_matrix(DC, 1.0).astype(MXU))

    const = lambda i: (0, 0)
    return pl.pallas_call(
        body, name="sgu_bwd", grid=(nb,),
        in_specs=[pl.BlockSpec((tb, DC), lambda i: (i, ucol)),
                  pl.BlockSpec((tb, DC), lambda i: (i, vcol)),
                  pl.BlockSpec((tb, DC), lambda i: (i, 0)),
                  pl.BlockSpec((ng, CCH, CCH), lambda i: (0, 0, 0)),
                  pl.BlockSpec((CCH, DC), const), pl.BlockSpec((1, DC), const),
                  pl.BlockSpec(memory_space=pl.ANY)],
        out_specs=[pl.BlockSpec((tb, 2 * DC), lambda i: (i, (4 * DA + 2 * DB) // (2 * DC))),
                   pl.BlockSpec((ng, CCH, CCH), lambda i: (0, 0, 0)),
                   pl.BlockSpec((CCH, DC), const), pl.BlockSpec((8, DC), const)],
        out_shape=[jax.ShapeDtypeStruct((t, DIN), SAVE), jax.ShapeDtypeStruct((ng, CCH, CCH), F32),
                   jax.ShapeDtypeStruct((CCH, DC), F32), jax.ShapeDtypeStruct((8, DC), F32)],
        scratch_shapes=[pltpu.VMEM((CCH, DC), F32)],
        input_output_aliases={6: 0},
        compiler_params=_params(("arbitrary",)),
    )(z, z, doc, w, bias, gain, dz)


def _head(h, gain, target, tm):
    t = h.shape[0]

    def body(h_ref, g_ref, t_ref, dh_ref, loss_ref, dgain_ref):
        @pl.when(pl.program_id(0) == 0)
        def _():
            loss_ref[...] = jnp.zeros_like(loss_ref)
            dgain_ref[...] = jnp.zeros_like(dgain_ref)

        hh = h_ref[...]
        gain = g_ref[...]
        rstd = lax.rsqrt(jnp.mean(hh * hh, axis=-1, keepdims=True) + EPS)
        xhat = hh * rstd
        err = xhat * gain - t_ref[...]
        per_tok = jnp.mean(err * err, axis=-1, keepdims=True)
        loss_ref[...] += 0.5 * jnp.sum(per_tok, axis=0, keepdims=True)
        dy = err * (1.0 / D)
        dgain_ref[...] += jnp.sum(dy * xhat, axis=0, keepdims=True)
        dxh = dy * gain
        dh_ref[...] = rstd * (dxh - xhat * jnp.mean(dxh * xhat, axis=-1, keepdims=True))

    return pl.pallas_call(
        body, name="head", grid=(t // tm,),
        in_specs=[pl.BlockSpec((tm, D), lambda i: (i, 0)),
                  pl.BlockSpec((1, D), lambda i: (0, 0)),
                  pl.BlockSpec((tm, D), lambda i: (i, 0))],
        out_specs=[pl.BlockSpec((tm, D), lambda i: (i, 0)),
                   pl.BlockSpec((1, 128), lambda i: (0, 0)),
                   pl.BlockSpec((1, D), lambda i: (0, 0))],
        out_shape=[jax.ShapeDtypeStruct((t, D), F32), jax.ShapeDtypeStruct((1, 128), F32),
                   jax.ShapeDtypeStruct((1, D), F32)],
        compiler_params=_params(("arbitrary",)),
    )(h, gain, target)


def _adamw(w, g, m, v):
    m = ADAM_B1 * m + (1.0 - ADAM_B1) * g
    v = ADAM_B2 * v + (1.0 - ADAM_B2) * (g * g)
    m_hat = m / (1.0 - ADAM_B1 ** ADAM_STEP)
    v_hat = v / (1.0 - ADAM_B2 ** ADAM_STEP)
    delta = -ADAM_LR * (m_hat / (jnp.sqrt(v_hat) + ADAM_EPS) + ADAM_WD * w)
    return delta, m, v


def _adamw_big(recv, w, m, v, tr, name, after, transposed=False):
    depth, rows, cols = w.shape
    rspec = (pl.BlockSpec((NDEV, cols, tr), lambda i: (0, 0, i)) if transposed
             else pl.BlockSpec((NDEV, tr, cols), lambda i: (0, i, 0)))

    def body(*refs):
        r_refs = refs[:depth]
        w_ref, m_ref, v_ref, _, g_out, d_out, m_out, v_out = refs[depth:]
        for l in range(depth):
            g = r_refs[l][0].astype(F32)
            for k in range(1, NDEV):
                g = g + r_refs[l][k].astype(F32)
            if transposed:
                g = g.T
            delta, m_, v_ = _adamw(w_ref[l], g, m_ref[l], v_ref[l])
            g_out[l] = g
            d_out[l] = delta
            m_out[l] = m_
            v_out[l] = v_

    spec = pl.BlockSpec((depth, tr, cols), lambda i: (0, i, 0))
    return pl.pallas_call(
        body, name=name, grid=(rows // tr,),
        in_specs=[rspec] * depth + [spec] * 3
        + [pl.BlockSpec(memory_space=pl.ANY)],
        out_specs=[spec] * 4, out_shape=[jax.ShapeDtypeStruct((depth, rows, cols), F32)] * 4,
        compiler_params=_params(("parallel",)),
    )(*recv, w, m, v, after)


def _sum_devices(recv):
    _, r, _ = recv.shape

    def body(r_ref, out_ref):
        g = r_ref[0]
        for k in range(1, NDEV):
            g = g + r_ref[k]
        out_ref[...] = g

    return pl.pallas_call(body, name="sum_devices", out_shape=jax.ShapeDtypeStruct((r, 128), F32))(recv)


def _adamw_small(w, g, m, v):
    def body(w_ref, g_ref, m_ref, v_ref, d_out, m_out, v_out):
        delta, m_, v_ = _adamw(w_ref[...], g_ref[...], m_ref[...], v_ref[...])
        d_out[...] = delta
        m_out[...] = m_
        v_out[...] = v_

    return pl.pallas_call(body, name="adamw_small", out_shape=[jax.ShapeDtypeStruct(w.shape, F32)] * 3)(w, g, m, v)


def _pack(arrs):
    flat = jnp.concatenate([a.reshape(-1) for a in arrs])
    pad = (-flat.shape[0]) % 1024
    return jnp.pad(flat, (0, pad)).reshape(-1, 128)


def _unpack(buf, like):
    flat = buf.reshape(-1)
    out, off = [], 0
    for a in like:
        out.append(flat[off:off + a.size].reshape(a.shape))
        off += a.size
    return out


def _block_diag(w):
    nb, bd, _ = w.shape
    eye = jnp.eye(nb, dtype=w.dtype)
    return (eye[:, None, :, None] * w[:, :, None, :]).reshape(nb * bd, nb * bd)


def _diag_blocks(w):
    nb = w.shape[0] // GRP
    return jnp.stack([w[g * GRP:(g + 1) * GRP, g * GRP:(g + 1) * GRP] for g in range(nb)])


SMALL = ['ffn1_norm', 'mix_norm', 'hgrn_lb_logits', 'hgrn_norm', 'conv_b', 'lru_wa', 'lru_ba', 'lru_wx', 'lru_bx',
         'lru_lambda', 'lru_norm', 'sgu_w', 'sgu_b', 'sgu_norm', 'ffn2_norm', 'final_norm']
NAMES = ['ffn1_norm', 'ffn1_wg', 'ffn1_wu', 'ffn1_wd', 'mix_norm', 'w_in', 'hgrn_lb_logits', 'hgrn_norm', 'conv_w',
         'conv_b', 'lru_wa', 'lru_ba', 'lru_wx', 'lru_bx', 'lru_lambda', 'lru_norm', 'sgu_w', 'sgu_b', 'sgu_norm',
         'w_out', 'ffn2_norm', 'ffn2_wg', 'ffn2_wu', 'ffn2_wd', 'final_norm']


def _step(x, target, w, m, v):
    depth = w['ffn1_wg'].shape[0]
    t = x.shape[1]
    h = x.reshape(t, D)
    target = target.reshape(t, D)
    tm_f, tm_b, tb = min(TM_F, t), min(TM_B, t), min(TB, t)
    my = 4 * lax.axis_index("x") + 2 * lax.axis_index("y") + lax.axis_index("c")

    cw_tile = jnp.pad(w['conv_w'].reshape(-1, 128), ((0, 8 - depth), (0, 0)))
    lbs, lb_soft = _lower_bounds(w['hgrn_lb_logits'])

    def row(a):
        return a.reshape(1, -1)

    def tr(a):
        return jnp.swapaxes(a, -1, -2)

    def shards(l, unit):
        if unit == 1:
            return [tr(w['w_in'][l]).astype(WIRE), w['w_out'][l].astype(WIRE)]
        f = 'ffn1' if unit == 0 else 'ffn2'
        return [tr(w[f + '_wg'][l]).astype(WIRE), tr(w[f + '_wu'][l]).astype(WIRE), w[f + '_wd'][l].astype(WIRE)]

    units = [(l, u) for l in range(depth) for u in range(3)]

    def start_ici(idx, deps=()):
        return _transfer_start(shards(*units[idx]), True, "gather_ici_%d_%d" % units[idx], deps=deps)

    def start_d2d(idx, handle, after):
        lands = _transfer_wait(handle, after, "gather_ici_wait_%d_%d" % units[idx])
        return _forward_start(lands, "gather_d2d_%d_%d" % units[idx])

    pipe = dict(idx=0)
    pipe['d2d'] = start_d2d(0, start_ici(0), h)
    pipe['ici'] = start_ici(1, deps=(pipe['d2d']['token'],))
    conv_flight = _transfer_start([cw_tile], True, "gather_conv_start", deps=(pipe['ici']['token'],), direct=True)

    def next_weights(after):
        idx = pipe['idx']
        lands = _transfer_wait(pipe['d2d'], after, "gather_d2d_wait_%d_%d" % units[idx])
        pipe['idx'] = idx + 1
        tok = 0.0
        if idx + 1 < len(units):
            pipe['d2d'] = start_d2d(idx + 1, pipe['ici'], lands[-1])
            tok = pipe['d2d']['token'][0, 0]
            if idx + 2 < len(units):
                pipe['ici'] = start_ici(idx + 2, deps=(pipe['d2d']['token'],))
                tok = pipe['ici']['token'][0, 0]
        return lands, tok

    saved = []
    for l in range(depth):
        lands, tok = next_weights(h)
        s = dict(ffn1=[a.reshape(FF, D) for a in lands], h0=h)
        h, s['xn1'], s['a1'], s['b1'] = _ffn_fwd(h, row(w['ffn1_norm'][l]) + tok, *s['ffn1'], tm_f)
        s['h1'] = h
        (win, wout), tok = next_weights(h)
        win, wout = win.reshape(DIN, D), wout.reshape(D, D)
        s['win'], s['wout'] = win, wout
        z, s['xnm'] = _inproj_fwd(h, row(w['mix_norm'][l]) + tok, win, tm_f)
        s['z'] = z
        s['o'], oa, s['states'] = _hgrn_fwd(z, row(lbs[l]), row(w['hgrn_norm'][l]))
        if l == 0:
            cw_all = _transfer_wait(conv_flight, z, "gather_conv_wait")[0][:, :depth]
            conv_w = jnp.moveaxis(cw_all.reshape(NDEV, depth, 4, DB // NDEV), 0, 2).reshape(depth, 4, DB)
        s['cw'] = jnp.pad(conv_w[l], ((0, 4), (0, 0)))
        s['vec'] = jnp.concatenate([row(w['conv_b'][l]), row(w['lru_ba'][l]), row(w['lru_bx'][l]),
                                    row(w['lru_lambda'][l]), row(w['lru_norm'][l]), jnp.zeros((3, DB), F32)])
        s['wa'], s['wx'] = _block_diag(w['lru_wa'][l]), _block_diag(w['lru_wx'][l])
        ob, s['hseq'] = _lru_fwd(z, s['cw'], s['vec'], s['wa'], s['wx'], tb)
        s['bias'] = jnp.repeat(w['sgu_b'][l].T, GRP, axis=1)
        oc = _sgu_fwd(z, w['sgu_w'][l], s['bias'], row(w['sgu_norm'][l]), tb)
        s['oa'], s['ob'], s['oc'] = oa, ob, oc
        h = _outproj_fwd(h, oa, ob, oc, wout, tm_f)
        s['h2'] = h
        lands, tok = next_weights(h)
        s['ffn2'] = [a.reshape(FF, D) for a in lands]
        h, s['xn2'], s['a2'], s['b2'] = _ffn_fwd(h, row(w['ffn2_norm'][l]) + tok, *s['ffn2'], tm_f)
        saved.append(s)

    dh, loss_part, g_final = _head(h, row(w['final_norm']), target, tm_f)
    loss = lax.psum(loss_part[0, 0], ("x", "y", "c"))

    recv = {k: [None] * depth for k in ('wg1', 'wu1', 'wd1', 'wg2', 'wu2', 'wd2', 'win', 'wout')}
    flight = []

    def land(after):
        handle, kinds, l = flight.pop()
        for k, a in zip(kinds, _transfer_wait(handle, after, f"exchange_wait_{kinds[0]}_{l}")):
            recv[k][l] = a

    def exchange(arrs, kinds, l, deps=()):
        handle = _transfer_start(arrs, False, f"exchange_start_{kinds[0]}_{l}", deps=deps)
        if flight:
            land(handle['token'])
        flight.append((handle, kinds, l))
        return handle['token'][0, 0]

    small = {k: [None] * depth for k in SMALL if k != 'final_norm'}
    dconv = [None] * depth
    dlb = [None] * depth
    tok = 0.0
    for l in reversed(range(depth)):
        s = saved[l]
        dh, g, *cot = _ffn_bwd_x(dh, s['h2'], row(w['ffn2_norm'][l]) + tok, s['a2'], s['b2'], *s['ffn2'], tm_f)
        dws = _ffn_bwd_w(s['xn2'], *cot, tm_f)
        tok = exchange([a.reshape(NDEV, FFS, D) for a in dws], ('wg2', 'wu2', 'wd2'), l)
        small['ffn2_norm'][l] = g
        doa, dob, doc, dwout = _outproj_bwd(dh, s['oa'], s['ob'], s['oc'], s['wout'], tm_f)
        dz, g_hn, dlb[l] = _hgrn_bwd(s['z'], row(lbs[l]), row(w['hgrn_norm'][l]) + tok, s['o'], s['states'], doa)
        small['hgrn_norm'][l] = g_hn
        dz, dcw, dvec, dwa, dwx = _lru_bwd(s['z'], s['hseq'], dob, s['cw'], s['vec'], s['wa'], s['wx'], dz, tb)
        dconv[l] = dcw[:4]
        small['conv_b'][l], small['lru_ba'][l], small['lru_bx'][l] = dvec[0], dvec[1].reshape(4, GRP), dvec[2].reshape(4, GRP)
        small['lru_lambda'][l], small['lru_norm'][l] = dvec[3], dvec[4]
        small['lru_wa'][l], small['lru_wx'][l] = _diag_blocks(dwa), _diag_blocks(dwx)
        dz, dsw, dbias, dgc = _sgu_bwd(s['z'], doc, w['sgu_w'][l], s['bias'], row(w['sgu_norm'][l]), dz, tb)
        small['sgu_w'][l], small['sgu_b'][l], small['sgu_norm'][l] = dsw, dbias[:, ::GRP].T, dgc[0]
        dwin = _inproj_bwd_w(s['xnm'], dz, tm_f)
        tok = exchange([dwin.reshape(NDEV, DINS, D), dwout.reshape(NDEV, D // NDEV, D)], ('win', 'wout'), l)
        dh, g = _inproj_bwd_x(dh, dz, s['h1'], row(w['mix_norm'][l]) + tok, s['win'], tm_b)
        small['mix_norm'][l] = g
        tok = 0.0
        if l == 0:
            small['ffn1_norm'][0] = jnp.zeros((1, D), F32)
            small['hgrn_lb_logits'] = list(_lower_bounds_bwd(lb_soft, jnp.concatenate(dlb, axis=0)))
            parts = [jnp.stack([small[k][j].reshape(w[k].shape[1:]) for j in range(depth)])
                     for k in SMALL if k != 'final_norm']
            parts += [g_final.reshape(D), jnp.stack(dconv)]
            small_flight = _transfer_start([_pack(parts)], True, "gather_small_start", direct=True)
            tok = small_flight['token'][0, 0]
        dh, g, *cot = _ffn_bwd_x(dh, s['h0'], row(w['ffn1_norm'][l]) + tok, s['a1'], s['b1'], *s['ffn1'], tm_f)
        dws = _ffn_bwd_w(s['xn1'], *cot, tm_f)
        before = ()
        if l == 0:
            g_last = _all_gather([g.reshape(8, 128)], "gather_last")[0]
            before = (g_last,)
        else:
            small['ffn1_norm'][l] = g
        tok = exchange([a.reshape(NDEV, FFS, D) for a in dws], ('wg1', 'wu1', 'wd1'), l, before)
    grad_x = dh.reshape(1, t, D)

    out = {}
    last = flight[0][0]['token']

    def ffn_update(f, n, after):
        for kind in ('wg', 'wu'):
            k = f + '_' + kind
            res = _adamw_big(recv[kind + n], tr(w[k]), tr(m[k]), tr(v[k]), 32, "adamw_ffn", after)
            out[k] = tuple(tr(a) for a in res)
        k = f + '_wd'
        out[k] = _adamw_big(recv['wd' + n], w[k], m[k], v[k], 32, "adamw_ffn", after)

    ffn_update('ffn2', '2', last)
    out['w_in'] = _adamw_big(recv['win'], w['w_in'], m['w_in'], v['w_in'], 128, "adamw_win", last, transposed=True)
    out['w_out'] = _adamw_big(recv['wout'], w['w_out'], m['w_out'], v['w_out'], 64, "adamw_wout", last)

    total = _sum_devices(_transfer_wait(small_flight, g_last, "gather_small_wait")[0])
    like = [w[k] for k in SMALL] + [jax.ShapeDtypeStruct((depth, 4, DB), F32)]
    grads = _unpack(total, like)
    gsmall = dict(zip(SMALL, grads[:-1]))
    gsmall['ffn1_norm'] = gsmall['ffn1_norm'].at[0].set(_sum_devices(g_last).reshape(D))
    gsmall['conv_w'] = lax.dynamic_slice_in_dim(grads[-1], my * (DB // NDEV), DB // NDEV, axis=2)
    keys = SMALL + ['conv_w']
    dl, mm, vv = _adamw_small(_pack([w[k] for k in keys]), _pack([gsmall[k] for k in keys]),
                              _pack([m[k] for k in keys]), _pack([v[k] for k in keys]))
    like = [w[k] for k in keys]
    for k, d_, m_, v_ in zip(keys, _unpack(dl, like), _unpack(mm, like), _unpack(vv, like)):
        out[k] = (gsmall[k], d_, m_, v_)
    done = [dl] + [out[k][1][0] for k in ('ffn2_wg', 'ffn2_wu', 'ffn2_wd', 'w_in', 'w_out')]
    land(functools.reduce(lambda p, q: p + q, [a[:1, :1] for a in done]))
    ffn_update('ffn1', '1', last)

    return (loss, grad_x, *[out[k][0] for k in NAMES], *[out[k][1] for k in NAMES],
            *[out[k][2] for k in NAMES], *[out[k][3] for k in NAMES])


def kernel(x, ffn1_norm, ffn1_wg, ffn1_wu, ffn1_wd, mix_norm, w_in, hgrn_lb_logits, hgrn_norm, conv_w, conv_b, lru_wa, lru_ba, lru_wx, lru_bx, lru_lambda, lru_norm, sgu_w, sgu_b, sgu_norm, w_out, ffn2_norm, ffn2_wg, ffn2_wu, ffn2_wd, final_norm, loss_target, m_ffn1_norm, m_ffn1_wg, m_ffn1_wu, m_ffn1_wd, m_mix_norm, m_w_in, m_hgrn_lb_logits, m_hgrn_norm, m_conv_w, m_conv_b, m_lru_wa, m_lru_ba, m_lru_wx, m_lru_bx, m_lru_lambda, m_lru_norm, m_sgu_w, m_sgu_b, m_sgu_norm, m_w_out, m_ffn2_norm, m_ffn2_wg, m_ffn2_wu, m_ffn2_wd, m_final_norm, v_ffn1_norm, v_ffn1_wg, v_ffn1_wu, v_ffn1_wd, v_mix_norm, v_w_in, v_hgrn_lb_logits, v_hgrn_norm, v_conv_w, v_conv_b, v_lru_wa, v_lru_ba, v_lru_wx, v_lru_bx, v_lru_lambda, v_lru_norm, v_sgu_w, v_sgu_b, v_sgu_norm, v_w_out, v_ffn2_norm, v_ffn2_wg, v_ffn2_wu, v_ffn2_wd, v_final_norm):
    args = locals()
    w = {k: args[k] for k in NAMES}
    m = {k: args['m_' + k] for k in NAMES}
    v = {k: args['v_' + k] for k in NAMES}
    return _step(x, loss_target, w, m, v)
```

```python
import functools

import jax
import jax.numpy as jnp
from jax import lax
from jax.experimental import pallas as pl
from jax.experimental.pallas import tpu as pltpu

F32 = jnp.float32
MXU = jnp.bfloat16
SAVE = jnp.bfloat16
WIRE = jnp.bfloat16

NDEV = 8
D = 1024
FF = 2816
FFS = FF // NDEV
FB = 256
DIN = 3072
DINS = DIN // NDEV
ZB = 512
DA, DB, DC = 512, 256, 256
HD = 128
NH = DA // HD
ACH = 64
ACB = 4
CCH = 128
GRP = 64
EPS = 1e-6
LRU_C = 8.0
VMEM_LIMIT = 60 * 1024 * 1024
TM_F = 1024
TM_B = 512
TB = 1024
SUB = 256

ADAM_LR, ADAM_B1, ADAM_B2, ADAM_EPS, ADAM_WD, ADAM_STEP = 0.001, 0.9, 0.999, 1e-08, 0.01, 10

MESH = pl.DeviceIdType.MESH


def _mm(a, b):
    return jnp.dot(a.astype(MXU), b.astype(MXU), preferred_element_type=F32)


def _mm_nt(a, b):
    return lax.dot_general(a.astype(MXU), b.astype(MXU), (((1,), (1,)), ((), ())), preferred_element_type=F32)


def _mm_tn(a, b):
    return lax.dot_general(a.astype(MXU), b.astype(MXU), (((0,), (0,)), ((), ())), preferred_element_type=F32)


def _split3(x):
    x1 = x.astype(MXU)
    r1 = x - x1.astype(F32)
    x2 = r1.astype(MXU)
    r2 = r1 - x2.astype(F32)
    return x1, x2, r2.astype(MXU)


def _mm_exact_l(c, x):
    x1, x2, x3 = _split3(x)
    return _mm(c, x1) + _mm(c, x2) + _mm(c, x3)


def _mm_exact_r(x, c):
    x1, x2, x3 = _split3(x)
    return _mm(x1, c) + _mm(x2, c) + _mm(x3, c)


def _sigmoid(x):
    return 1.0 / (1.0 + jnp.exp(-x))


def _gelu(x):
    c, k = 0.7978845608028654, 0.044715
    th = jnp.tanh(c * (x + k * x * x * x))
    return 0.5 * x * (1.0 + th)


def _gelu_and_grad(x):
    c, k = 0.7978845608028654, 0.044715
    th = jnp.tanh(c * (x + k * x * x * x))
    g = 0.5 * x * (1.0 + th)
    dg = 0.5 * (1.0 + th) + 0.5 * x * (1.0 - th * th) * c * (1.0 + 3.0 * k * x * x)
    return g, dg


def _expm1(x):
    series = x * (1.0 + x * (0.5 + x * (1.0 / 6.0 + x * (1.0 / 24.0 + x * (1.0 / 120.0)))))
    return jnp.where(jnp.abs(x) < 0.05, series, jnp.exp(x) - 1.0)


def _iota(shape, dim):
    return lax.broadcasted_iota(jnp.int32, shape, dim)


def _group_matrix(n, value):
    r, c = _iota((n, n), 0), _iota((n, n), 1)
    return jnp.where((r // GRP) == (c // GRP), value, 0.0).astype(F32)


def _row(x, k):
    r = _iota(x.shape, 0)
    return jnp.sum(jnp.where(r == k, x, 0.0), axis=0, keepdims=True)


def _rms_bwd(dxn, hh, gain):
    rstd = lax.rsqrt(jnp.mean(hh * hh, axis=-1, keepdims=True) + EPS)
    xhat = hh * rstd
    dxh = dxn * gain
    dh = rstd * (dxh - xhat * jnp.mean(dxh * xhat, axis=-1, keepdims=True))
    return dh, jnp.sum(dxn * xhat, axis=0, keepdims=True)


def _params(sem):
    return pltpu.CompilerParams(dimension_semantics=sem, vmem_limit_bytes=VMEM_LIMIT)


def _all_gather(arrs, name):
    n = len(arrs)

    def body(*refs):
        ins, outs = refs[:n], refs[n:2 * n]
        send_sems, recv_sems, local_sems = refs[2 * n:]
        x, y, c = lax.axis_index("x"), lax.axis_index("y"), lax.axis_index("c")
        me, sibling = (x, y, c), (x, y, 1 - c)
        chips = [(1 - x, y), (x, 1 - y), (1 - x, 1 - y)]

        def slot(px, py, pc):
            return 4 * px + 2 * py + pc

        def copy(a, k, block, to, src=None):
            dst = outs[a].at[slot(*block)]
            return pltpu.make_async_remote_copy(
                src_ref=dst if src is None else src, dst_ref=dst,
                send_sem=send_sems.at[a * 7 + k], recv_sem=recv_sems.at[a * 7 + k],
                device_id=to, device_id_type=MESH)

        started = []
        for a in range(n):
            mine = pltpu.make_async_copy(ins[a], outs[a].at[slot(*me)], local_sems.at[a])
            mine.start()
            started.append(mine)
        first = []
        for a in range(n):
            first.append(copy(a, 0, me, sibling, src=ins[a]))
            first += [copy(a, 1 + j, me, (*chip, c), src=ins[a]) for j, chip in enumerate(chips)]
        for cp in first:
            cp.start()
        passed = []
        for a in range(n):
            for j, chip in enumerate(chips):
                copy(a, 1 + j, (*chip, c), me).wait_recv()
                fwd = copy(a, 4 + j, (*chip, c), sibling)
                fwd.start()
                passed.append(fwd)
        for a in range(n):
            copy(a, 0, sibling, me).wait_recv()
            for j, chip in enumerate(chips):
                copy(a, 4 + j, (*chip, 1 - c), me).wait_recv()
        for cp in first + passed:
            cp.wait_send()
        for mine in started:
            mine.wait()

    hbm = pl.BlockSpec(memory_space=pl.ANY)
    return pl.pallas_call(
        body, name=name,
        out_shape=[jax.ShapeDtypeStruct((NDEV,) + a.shape, a.dtype) for a in arrs],
        in_specs=[hbm] * n, out_specs=[hbm] * n,
        scratch_shapes=[pltpu.SemaphoreType.DMA((7 * n,)), pltpu.SemaphoreType.DMA((7 * n,)),
                        pltpu.SemaphoreType.DMA((n,))],
    )(*arrs)


def _peers():
    x, y, c = lax.axis_index("x"), lax.axis_index("y"), lax.axis_index("c")
    peers = [(x ^ ((k >> 2) & 1), y ^ ((k >> 1) & 1), c ^ (k & 1)) for k in range(1, NDEV)]
    return (x, y, c), 4 * x + 2 * y + c, peers


_HBM = pl.BlockSpec(memory_space=pltpu.HBM)
_SEM = pl.BlockSpec(memory_space=pltpu.SEMAPHORE)
_EFFECT = pltpu.SideEffectType.DATAFLOW_SIDE_EFFECTING


def _transfer_start(arrs, gather, name, deps=(), direct=False):
    n, nd = len(arrs), len(deps)
    shapes = [((NDEV,) + a.shape) if gather else a.shape for a in arrs]

    def body(*refs):
        ins, lands = refs[:n], refs[n:2 * n]
        send_sems, recv_sems, local_sems = refs[2 * n + nd:2 * n + nd + 3]
        token = refs[-1]
        (x, y, c), my, peers = _peers()
        if gather and not direct:
            peers = [(x, y, 1 - c), (1 - x, y, c), (x, 1 - y, c), (1 - x, 1 - y, c)]
        for a in range(n):
            own = ins[a] if gather else ins[a].at[my]
            pltpu.make_async_copy(own, lands[a].at[my], local_sems.at[a]).start()
        for a in range(n):
            for peer in peers:
                src = ins[a] if gather else ins[a].at[4 * peer[0] + 2 * peer[1] + peer[2]]
                pltpu.make_async_remote_copy(
                    src_ref=src, dst_ref=lands[a].at[my], send_sem=send_sems.at[a], recv_sem=recv_sems.at[a],
                    device_id=peer, device_id_type=MESH).start()
        token[...] = jnp.zeros_like(token)

    out_shape = [pltpu.SemaphoreType.DMA((n,))] * 3
    out_shape += [pltpu.HBM(a.shape, a.dtype) for a in arrs]
    out_shape += [pltpu.HBM(s, a.dtype) for s, a in zip(shapes, arrs)]
    out_shape += [jax.ShapeDtypeStruct((8, 128), F32)]
    operands = [pltpu.with_memory_space_constraint(a, pltpu.HBM) for a in arrs]
    operands += [pltpu.with_memory_space_constraint(lax.empty(s, a.dtype), pltpu.HBM) for s, a in zip(shapes, arrs)]
    res = pl.pallas_call(
        body, name=name, out_shape=out_shape,
        in_specs=[_HBM] * (2 * n) + [pl.BlockSpec(memory_space=pl.ANY)] * nd,
        out_specs=[_SEM] * 3 + [_HBM] * (2 * n) + [pl.BlockSpec(memory_space=pltpu.VMEM)],
        input_output_aliases={i: 3 + i for i in range(2 * n)},
        compiler_params=pltpu.CompilerParams(has_side_effects=_EFFECT),
    )(*operands, *deps)
    return dict(sems=res[:3], src=res[3:3 + n], lands=res[3 + n:3 + 2 * n], token=res[-1], n=n,
                count=4 if gather and not direct else NDEV - 1)


def _forward_start(lands, name, deps=()):
    n, nd = len(lands), len(deps)

    def body(*refs):
        zone = refs[:n]
        send_sems, recv_sems = refs[n + nd:n + nd + 2]
        token = refs[-1]
        (x, y, c), _, _ = _peers()
        for a in range(n):
            for px, py in ((1 - x, y), (x, 1 - y), (1 - x, 1 - y)):
                block = zone[a].at[4 * px + 2 * py + c]
                pltpu.make_async_remote_copy(
                    src_ref=block, dst_ref=block, send_sem=send_sems.at[a], recv_sem=recv_sems.at[a],
                    device_id=(x, y, 1 - c), device_id_type=MESH).start()
        token[...] = jnp.zeros_like(token)

    res = pl.pallas_call(
        body, name=name,
        out_shape=[pltpu.SemaphoreType.DMA((n,))] * 2 + [pltpu.HBM(a.shape, a.dtype) for a in lands]
        + [jax.ShapeDtypeStruct((8, 128), F32)],
        in_specs=[_HBM] * n + [pl.BlockSpec(memory_space=pl.ANY)] * nd,
        out_specs=[_SEM] * 2 + [_HBM] * n + [pl.BlockSpec(memory_space=pltpu.VMEM)],
        input_output_aliases={i: 2 + i for i in range(n)},
        compiler_params=pltpu.CompilerParams(has_side_effects=_EFFECT),
    )(*lands, *deps)
    return dict(sems=res[:2], src=[], lands=res[2:2 + n], token=res[-1], n=n, count=3)


def _transfer_wait(handle, after, name):
    n, count = handle["n"], handle["count"]
    src, lands, sems = list(handle["src"]), list(handle["lands"]), list(handle["sems"])
    ns = len(src)

    def body(*refs):
        zone = refs[ns:ns + n]
        sem_refs = refs[ns + n:ns + n + len(sems)]
        me, _, _ = _peers()
        for a in range(n):
            moved = zone[a].at[pl.ds(0, count)]
            both = pltpu.make_async_remote_copy(
                src_ref=moved, dst_ref=moved, send_sem=sem_refs[0].at[a], recv_sem=sem_refs[1].at[a],
                device_id=me, device_id_type=MESH)
            both.wait_send()
            both.wait_recv()
            if len(sems) == 3:
                pltpu.make_async_copy(zone[a].at[0], zone[a].at[1], sem_refs[2].at[a]).wait()

    res = pl.pallas_call(
        body, name=name,
        out_shape=[pltpu.HBM(a.shape, a.dtype) for a in src + lands],
        in_specs=[_HBM] * (ns + n) + [_SEM] * len(sems) + [pl.BlockSpec(memory_space=pl.ANY)],
        out_specs=[_HBM] * (ns + n),
        input_output_aliases={i: i for i in range(ns + n)},
        compiler_params=pltpu.CompilerParams(has_side_effects=_EFFECT),
    )(*src, *lands, *sems, after)
    return list(res[ns:])


def _ffn_fwd(h, gain, wg, wu, wd, tm):
    t = h.shape[0]
    nj = FF // FB

    def body(h_ref, g_ref, wg_ref, wu_ref, wd_ref, out_ref, xn_ref, a_ref, b_ref, acc_ref):
        j = pl.program_id(1)

        @pl.when(j == 0)
        def _():
            hh = h_ref[...]
            rstd = lax.rsqrt(jnp.mean(hh * hh, axis=-1, keepdims=True) + EPS)
            xn_ref[...] = (hh * rstd * g_ref[...]).astype(xn_ref.dtype)
            acc_ref[...] = jnp.zeros_like(acc_ref)

        sub = min(SUB, tm)
        for r in range(tm // sub):
            rows = slice(r * sub, (r + 1) * sub)
            xn = xn_ref[rows, :]
            a = _mm_nt(xn, wg_ref[...])
            b = _mm_nt(xn, wu_ref[...])
            a_ref[rows, :] = a.astype(a_ref.dtype)
            b_ref[rows, :] = b.astype(b_ref.dtype)
            acc_ref[rows, :] += _mm(a * _sigmoid(a) * b, wd_ref[...])

        @pl.when(j == nj - 1)
        def _():
            out_ref[...] = h_ref[...] + 0.5 * acc_ref[...]

    wspec = pl.BlockSpec((FB, D), lambda i, j: (j, 0))
    return pl.pallas_call(
        body, name="ffn_fwd", grid=(t // tm, nj),
        in_specs=[pl.BlockSpec((tm, D), lambda i, j: (i, 0)),
                  pl.BlockSpec((1, D), lambda i, j: (0, 0)), wspec, wspec, wspec],
        out_specs=[pl.BlockSpec((tm, D), lambda i, j: (i, 0)),
                   pl.BlockSpec((tm, D), lambda i, j: (i, 0)),
                   pl.BlockSpec((tm, FB), lambda i, j: (i, j)),
                   pl.BlockSpec((tm, FB), lambda i, j: (i, j))],
        out_shape=[jax.ShapeDtypeStruct((t, D), F32), jax.ShapeDtypeStruct((t, D), SAVE),
                   jax.ShapeDtypeStruct((t, FF), SAVE), jax.ShapeDtypeStruct((t, FF), SAVE)],
        scratch_shapes=[pltpu.VMEM((tm, D), F32)],
        compiler_params=_params(("parallel", "arbitrary")),
    )(h, gain, wg, wu, wd)


def _ffn_bwd_x(dout, h, gain, a_sv, b_sv, wg, wu, wd, tm):
    t = h.shape[0]
    nj = FF // FB

    def body(dout_ref, h_ref, g_ref, a_ref, b_ref, wg_ref, wu_ref, wd_ref,
             dh_ref, dgain_ref, dy_ref, da_ref, db_ref, s_ref, acc_ref):
        i, j = pl.program_id(0), pl.program_id(1)

        @pl.when((i == 0) & (j == 0))
        def _():
            dgain_ref[...] = jnp.zeros_like(dgain_ref)

        @pl.when(j == 0)
        def _():
            dy_ref[...] = (0.5 * dout_ref[...]).astype(dy_ref.dtype)
            acc_ref[...] = jnp.zeros_like(acc_ref)

        sub = min(SUB, tm)
        for r in range(tm // sub):
            rows = slice(r * sub, (r + 1) * sub)
            ds = _mm_nt(dy_ref[rows, :], wd_ref[...])
            a, b = a_ref[rows, :].astype(F32), b_ref[rows, :].astype(F32)
            sg = _sigmoid(a)
            sa = a * sg
            da = (ds * b * (sg * (1.0 + a * (1.0 - sg)))).astype(MXU)
            db = (ds * sa).astype(MXU)
            da_ref[rows, :] = da.astype(da_ref.dtype)
            db_ref[rows, :] = db.astype(db_ref.dtype)
            s_ref[rows, :] = (sa * b).astype(s_ref.dtype)
            acc_ref[rows, :] += _mm(da, wg_ref[...]) + _mm(db, wu_ref[...])

        @pl.when(j == nj - 1)
        def _():
            dh, dg = _rms_bwd(acc_ref[...], h_ref[...], g_ref[...])
            dh_ref[...] = dout_ref[...] + dh
            dgain_ref[...] += dg

    tok = pl.BlockSpec((tm, D), lambda i, j: (i, 0))
    act = pl.BlockSpec((tm, FB), lambda i, j: (i, j))
    wspec = pl.BlockSpec((FB, D), lambda i, j: (j, 0))
    return pl.pallas_call(
        body, name="ffn_bwd_x", grid=(t // tm, nj),
        in_specs=[tok, tok, pl.BlockSpec((1, D), lambda i, j: (0, 0)), act, act, wspec, wspec, wspec],
        out_specs=[tok, pl.BlockSpec((1, D), lambda i, j: (0, 0)), tok, act, act, act],
        out_shape=[jax.ShapeDtypeStruct((t, D), F32), jax.ShapeDtypeStruct((1, D), F32),
                   jax.ShapeDtypeStruct((t, D), SAVE)] + [jax.ShapeDtypeStruct((t, FF), SAVE)] * 3,
        scratch_shapes=[pltpu.VMEM((tm, D), F32)],
        compiler_params=_params(("arbitrary", "arbitrary")),
    )(dout, h, gain, a_sv, b_sv, wg, wu, wd)


def _ffn_bwd_w(xn, dy, da, db, s, tm):
    t = xn.shape[0]
    nt = t // tm
    nj = FF // FB

    def body(xn_ref, dy_ref, da_ref, db_ref, s_ref, dwg_ref, dwu_ref, dwd_ref, ag_scr, au_scr, ad_scr):
        i, j = pl.program_id(0), pl.program_id(1)
        rows = pl.ds(pl.multiple_of(j * FB, FB), FB)
        xn = xn_ref[...]
        new = ((ag_scr, _mm_tn(da_ref[...], xn)), (au_scr, _mm_tn(db_ref[...], xn)),
               (ad_scr, _mm_tn(s_ref[...], dy_ref[...])))

        @pl.when(i == 0)
        def _():
            for ref, val in new:
                ref[rows, :] = val

        @pl.when(i > 0)
        def _():
            for ref, val in new:
                ref[rows, :] += val

        @pl.when(i == nt - 1)
        def _():
            for out, ref in ((dwg_ref, ag_scr), (dwu_ref, au_scr), (dwd_ref, ad_scr)):
                out[...] = ref[rows, :].astype(out.dtype)

    tok = pl.BlockSpec((tm, D), lambda i, j: (i, 0))
    act = pl.BlockSpec((tm, FB), lambda i, j: (i, j))
    wspec = pl.BlockSpec((FB, D), lambda i, j: (jnp.where(i == nt - 1, j, 0), 0))
    return pl.pallas_call(
        body, name="ffn_bwd_w", grid=(nt, nj),
        in_specs=[tok, tok, act, act, act], out_specs=[wspec] * 3,
        out_shape=[jax.ShapeDtypeStruct((FF, D), WIRE)] * 3,
        scratch_shapes=[pltpu.VMEM((FF, D), F32)] * 3,
        compiler_params=_params(("arbitrary", "arbitrary")),
    )(xn, dy, da, db, s)


def _inproj_fwd(h, gain, win, tm):
    t = h.shape[0]

    def body(h_ref, g_ref, w_ref, z_ref, xn_ref):
        hh = h_ref[...]
        rstd = lax.rsqrt(jnp.mean(hh * hh, axis=-1, keepdims=True) + EPS)
        xn = (hh * rstd * g_ref[...]).astype(MXU)
        xn_ref[...] = xn.astype(xn_ref.dtype)
        for j in range(DIN // ZB):
            z_ref[:, j * ZB:(j + 1) * ZB] = _mm_nt(xn, w_ref[j * ZB:(j + 1) * ZB, :])

    return pl.pallas_call(
        body, name="inproj_fwd", grid=(t // tm,),
        in_specs=[pl.BlockSpec((tm, D), lambda i: (i, 0)),
                  pl.BlockSpec((1, D), lambda i: (0, 0)),
                  pl.BlockSpec((DIN, D), lambda i: (0, 0))],
        out_specs=[pl.BlockSpec((tm, DIN), lambda i: (i, 0)),
                   pl.BlockSpec((tm, D), lambda i: (i, 0))],
        out_shape=[jax.ShapeDtypeStruct((t, DIN), F32), jax.ShapeDtypeStruct((t, D), SAVE)],
        compiler_params=_params(("parallel",)),
    )(h, gain, win)


def _inproj_bwd_x(dres, dz, h, gain, win, tm):
    t = h.shape[0]

    def body(dres_ref, dz_ref, h_ref, g_ref, w_ref, dh_ref, dgain_ref):
        @pl.when(pl.program_id(0) == 0)
        def _():
            dgain_ref[...] = jnp.zeros_like(dgain_ref)

        dh, dg = _rms_bwd(_mm(dz_ref[...], w_ref[...]), h_ref[...], g_ref[...])
        dh_ref[...] = dres_ref[...] + dh
        dgain_ref[...] += dg

    return pl.pallas_call(
        body, name="inproj_bwd_x", grid=(t // tm,),
        in_specs=[pl.BlockSpec((tm, D), lambda i: (i, 0)),
                  pl.BlockSpec((tm, DIN), lambda i: (i, 0)),
                  pl.BlockSpec((tm, D), lambda i: (i, 0)),
                  pl.BlockSpec((1, D), lambda i: (0, 0)),
                  pl.BlockSpec((DIN, D), lambda i: (0, 0))],
        out_specs=[pl.BlockSpec((tm, D), lambda i: (i, 0)),
                   pl.BlockSpec((1, D), lambda i: (0, 0))],
        out_shape=[jax.ShapeDtypeStruct((t, D), F32), jax.ShapeDtypeStruct((1, D), F32)],
        compiler_params=_params(("arbitrary",)),
    )(dres, dz, h, gain, win)


def _inproj_bwd_w(xn, dz, tm):
    t = xn.shape[0]
    nt = t // tm

    def body(xn_ref, dz_ref, dw_ref, acc_scr):
        i = pl.program_id(1)

        @pl.when(i == 0)
        def _():
            acc_scr[...] = jnp.zeros_like(acc_scr)

        acc_scr[...] += _mm_tn(dz_ref[...], xn_ref[...])

        @pl.when(i == nt - 1)
        def _():
            dw_ref[...] = acc_scr[...].astype(dw_ref.dtype)

    return pl.pallas_call(
        body, name="inproj_bwd_w", grid=(DIN // ZB, nt),
        in_specs=[pl.BlockSpec((tm, D), lambda j, i: (i, 0)),
                  pl.BlockSpec((tm, ZB), lambda j, i: (i, j))],
        out_specs=pl.BlockSpec((ZB, D), lambda j, i: (j, 0)),
        out_shape=jax.ShapeDtypeStruct((DIN, D), WIRE),
        scratch_shapes=[pltpu.VMEM((ZB, D), F32)],
        compiler_params=_params(("parallel", "arbitrary")),
    )(xn, dz)


def _outproj_fwd(h, oa, ob, oc, wout, tm):
    t = h.shape[0]

    def body(h_ref, oa_ref, ob_ref, oc_ref, w_ref, out_ref):
        ym = jnp.concatenate([oa_ref[...], ob_ref[...], oc_ref[...]], axis=1)
        out_ref[...] = h_ref[...] + _mm(ym, w_ref[...])

    return pl.pallas_call(
        body, name="outproj_fwd", grid=(t // tm,),
        in_specs=[pl.BlockSpec((tm, D), lambda i: (i, 0)),
                  pl.BlockSpec((tm, DA), lambda i: (i, 0)),
                  pl.BlockSpec((tm, DB), lambda i: (i, 0)),
                  pl.BlockSpec((tm, DC), lambda i: (i, 0)),
                  pl.BlockSpec((D, D), lambda i: (0, 0))],
        out_specs=pl.BlockSpec((tm, D), lambda i: (i, 0)),
        out_shape=jax.ShapeDtypeStruct((t, D), F32),
        compiler_params=_params(("parallel",)),
    )(h, oa, ob, oc, wout)


def _outproj_bwd(dh, oa, ob, oc, wout, tm):
    t = dh.shape[0]
    nt = t // tm

    def body(dh_ref, oa_ref, ob_ref, oc_ref, w_ref, da_ref, db_ref, dc_ref, dw_ref, acc_scr):
        i = pl.program_id(0)

        @pl.when(i == 0)
        def _():
            acc_scr[...] = jnp.zeros_like(acc_scr)

        d16 = dh_ref[...].astype(MXU)
        dym = _mm_nt(d16, w_ref[...])
        da_ref[...] = dym[:, :DA]
        db_ref[...] = dym[:, DA:DA + DB]
        dc_ref[...] = dym[:, DA + DB:]
        ym = jnp.concatenate([oa_ref[...], ob_ref[...], oc_ref[...]], axis=1)
        acc_scr[...] += _mm_tn(ym, d16)

        @pl.when(i == nt - 1)
        def _():
            dw_ref[...] = acc_scr[...].astype(dw_ref.dtype)

    return pl.pallas_call(
        body, name="outproj_bwd", grid=(nt,),
        in_specs=[pl.BlockSpec((tm, D), lambda i: (i, 0)),
                  pl.BlockSpec((tm, DA), lambda i: (i, 0)),
                  pl.BlockSpec((tm, DB), lambda i: (i, 0)),
                  pl.BlockSpec((tm, DC), lambda i: (i, 0)),
                  pl.BlockSpec((D, D), lambda i: (0, 0))],
        out_specs=[pl.BlockSpec((tm, DA), lambda i: (i, 0)),
                   pl.BlockSpec((tm, DB), lambda i: (i, 0)),
                   pl.BlockSpec((tm, DC), lambda i: (i, 0)),
                   pl.BlockSpec((D, D), lambda i: (0, 0))],
        out_shape=[jax.ShapeDtypeStruct((t, DA), F32), jax.ShapeDtypeStruct((t, DB), F32),
                   jax.ShapeDtypeStruct((t, DC), F32), jax.ShapeDtypeStruct((D, D), WIRE)],
        scratch_shapes=[pltpu.VMEM((D, D), F32)],
        compiler_params=_params(("arbitrary",)),
    )(dh, oa, ob, oc, wout)


def _lower_bounds(logits):
    depth, n = logits.shape

    def body(l_ref, lb_ref, p_ref):
        rows = [l_ref[l:l + 1, :] for l in range(depth)]
        mx = functools.reduce(jnp.maximum, rows)
        ex = [jnp.exp(r - mx) for r in rows]
        den = functools.reduce(lambda u, v: u + v, ex)
        acc = jnp.zeros_like(den)
        for l in range(depth):
            p = ex[l] / den
            p_ref[l:l + 1, :] = p
            if l > 0:
                acc = acc + p
            lb_ref[l:l + 1, :] = acc

    return pl.pallas_call(
        body, name="lower_bounds",
        out_shape=[jax.ShapeDtypeStruct((depth, n), F32), jax.ShapeDtypeStruct((depth, n), F32)],
    )(logits)


def _lower_bounds_bwd(p, dlb):
    depth, n = p.shape

    def body(p_ref, d_ref, out_ref):
        ps = [p_ref[l:l + 1, :] for l in range(depth)]
        ds = [d_ref[l:l + 1, :] for l in range(depth)]
        dp = [jnp.zeros_like(ps[0]) for _ in range(depth)]
        run = jnp.zeros_like(ps[0])
        for l in range(depth - 1, 0, -1):
            run = run + ds[l]
            dp[l] = run
        dot = functools.reduce(lambda u, v: u + v, [ps[l] * dp[l] for l in range(depth)])
        for l in range(depth):
            out_ref[l:l + 1, :] = ps[l] * (dp[l] - dot)

    return pl.pallas_call(body, name="lower_bounds_bwd", out_shape=jax.ShapeDtypeStruct((depth, n), F32))(p, dlb)


def _hgrn_block(z_ref, lb_ref, rb):
    q, fl = z_ref[:, 0:DA], z_ref[:, DA:2 * DA]
    lb = lb_ref[...]
    sq = _sigmoid(q)
    qs = q * sq
    sg = _sigmoid(fl)
    f = lb + (1.0 - lb) * sg
    k = 1.0 - f
    lf = jnp.log(f)
    row, col = _iota((rb, rb), 0), _iota((rb, rb), 1)
    same = (row // ACH) == (col // ACH)
    causal = same & (row >= col)
    b = _mm_exact_l(jnp.where(causal, 1.0, 0.0).astype(MXU), lf)
    bend = _mm_exact_l(jnp.where(same, 1.0, 0.0).astype(MXU), lf)
    r = 0.5 * bend
    eq, ek, eb, ed = jnp.exp(b - r), jnp.exp(r - b), jnp.exp(b), jnp.exp(bend - b)
    return dict(q=q, lb=lb, sq=sq, sg=sg, f=f, bend=bend, eq=eq, ek=ek, eb=eb, ed=ed,
                qt=qs * eq, kt=k * ek, qe=qs * eb, kd=k * ed, same=same, causal=causal)


def _hgrn_fwd(z, lb, gain):
    t = z.shape[0]
    nc = t // ACH
    cb = min(ACB, nc)
    rb = cb * ACH

    def body(z_ref, lb_ref, g_ref, o_ref, oa_ref, st_ref, st_scr):
        @pl.when(pl.program_id(0) == 0)
        def _():
            st_scr[...] = jnp.zeros_like(st_scr)

        c = _hgrn_block(z_ref, lb_ref, rb)
        for hd in range(NH):
            cols = slice(hd * HD, (hd + 1) * HD)
            v = z_ref[:, 2 * DA + hd * HD:2 * DA + (hd + 1) * HD]
            gg = z_ref[:, 3 * DA + hd * HD:3 * DA + (hd + 1) * HD]
            att = jnp.where(c["causal"], _mm_nt(c["qt"][:, cols], c["kt"][:, cols]), 0.0)
            o_in = _mm(att, v)
            qe, kd, bend = c["qe"][:, cols], c["kd"][:, cols], c["bend"][:, cols]
            st = st_scr[hd]
            outs = []
            for cc in range(cb):
                rows = slice(cc * ACH, (cc + 1) * ACH)
                st_ref[cc, hd] = st
                outs.append(o_in[rows] + _mm_nt(qe[rows], st))
                decay = jnp.exp(jnp.max(bend[rows], axis=0, keepdims=True))
                st = st * decay + _mm_tn(v[rows], kd[rows])
            st_scr[hd] = st
            o = jnp.concatenate(outs, axis=0)
            o_ref[:, cols] = o
            rstd = lax.rsqrt(jnp.mean(o * o, axis=-1, keepdims=True) + EPS)
            oa_ref[:, cols] = (o * rstd * g_ref[:, cols] * (gg * _sigmoid(gg))).astype(oa_ref.dtype)

    return pl.pallas_call(
        body, name="hgrn_fwd", grid=(nc // cb,),
        in_specs=[pl.BlockSpec((rb, 4 * DA), lambda c: (c, 0)),
                  pl.BlockSpec((1, DA), lambda c: (0, 0)),
                  pl.BlockSpec((1, DA), lambda c: (0, 0))],
        out_specs=[pl.BlockSpec((rb, DA), lambda c: (c, 0)),
                   pl.BlockSpec((rb, DA), lambda c: (c, 0)),
                   pl.BlockSpec((cb, NH, HD, HD), lambda c: (c, 0, 0, 0))],
        out_shape=[jax.ShapeDtypeStruct((t, DA), F32), jax.ShapeDtypeStruct((t, DA), SAVE),
                   jax.ShapeDtypeStruct((nc, NH, HD, HD), F32)],
        scratch_shapes=[pltpu.VMEM((NH, HD, HD), F32)],
        compiler_params=_params(("arbitrary",)),
    )(z, lb, gain)


def _hgrn_bwd(z, lb, gain, o, states, doa):
    t = z.shape[0]
    nc = t // ACH
    cb = min(ACB, nc)
    rb = cb * ACH
    nblk = nc // cb

    def body(z_ref, lb_ref, g_ref, o_ref, st_ref, doa_ref, dz_ref, dgain_ref, dlb_ref, dst_scr):
        @pl.when(pl.program_id(0) == 0)
        def _():
            dst_scr[...] = jnp.zeros_like(dst_scr)
            dgain_ref[...] = jnp.zeros_like(dgain_ref)
            dlb_ref[...] = jnp.zeros_like(dlb_ref)

        c = _hgrn_block(z_ref, lb_ref, rb)
        dbs, dqss, dks = [], [], []
        for hd in range(NH):
            cols = slice(hd * HD, (hd + 1) * HD)
            v = z_ref[:, 2 * DA + hd * HD:2 * DA + (hd + 1) * HD]
            gg = z_ref[:, 3 * DA + hd * HD:3 * DA + (hd + 1) * HD]
            qt, kt, qe, kd, bend = (c[n][:, cols] for n in ("qt", "kt", "qe", "kd", "bend"))
            o = o_ref[:, cols]
            do_a = doa_ref[:, cols]
            gain = g_ref[:, cols]
            sgg = _sigmoid(gg)
            silu_g = gg * sgg
            rstd = lax.rsqrt(jnp.mean(o * o, axis=-1, keepdims=True) + EPS)
            n = o * rstd
            dn = do_a * gain * silu_g
            dg = do_a * n * gain * (sgg * (1.0 + gg * (1.0 - sgg)))
            dgain_ref[:, cols] += jnp.sum(do_a * silu_g * n, axis=0, keepdims=True)
            d_o = rstd * (dn - n * jnp.mean(dn * n, axis=-1, keepdims=True))

            att = jnp.where(c["causal"], _mm_nt(qt, kt), 0.0)
            datt = jnp.where(c["causal"], _mm_nt(d_o, v), 0.0)
            dv_in = _mm_tn(att, d_o)
            dqt = _mm(datt, kt)
            dkt = _mm_tn(datt, qt)
            dsp = dst_scr[hd]
            dvs, dqes, dkds, dbends = [None] * cb, [None] * cb, [None] * cb, [None] * cb
            for cc in reversed(range(cb)):
                rows = slice(cc * ACH, (cc + 1) * ACH)
                st = st_ref[cc, hd]
                dvs[cc] = dv_in[rows] + _mm_nt(kd[rows], dsp)
                dqes[cc] = _mm(d_o[rows], st)
                dkds[cc] = _mm(v[rows], dsp)
                decay = jnp.exp(jnp.max(bend[rows], axis=0, keepdims=True))
                dbend = (decay * jnp.sum(st * dsp, axis=0, keepdims=True)
                         + jnp.sum(dkds[cc] * kd[rows], axis=0, keepdims=True))
                dbends[cc] = jnp.broadcast_to(dbend, (ACH, HD))
                dsp = dsp * decay + _mm_tn(d_o[rows], qe[rows])
            dst_scr[hd] = dsp
            dv, dqe, dkd, dbend = (jnp.concatenate(p, axis=0) for p in (dvs, dqes, dkds, dbends))
            dbs.append((dqt * qt + dqe * qe - dkt * kt - dkd * kd, dbend))
            dqss.append(dqt * c["eq"][:, cols] + dqe * c["eb"][:, cols])
            dks.append(dkt * c["ek"][:, cols] + dkd * c["ed"][:, cols])
            c0 = hd * HD
            dz_ref[:, 2 * DA + c0:2 * DA + c0 + HD] = dv.astype(dz_ref.dtype)
            dz_ref[:, 3 * DA + c0:3 * DA + c0 + HD] = dg.astype(dz_ref.dtype)

        db = jnp.concatenate([p[0] for p in dbs], axis=1)
        dbend = jnp.concatenate([p[1] for p in dbs], axis=1)
        dqs, dk = jnp.concatenate(dqss, axis=1), jnp.concatenate(dks, axis=1)
        row, col = _iota((rb, rb), 0), _iota((rb, rb), 1)
        upper = jnp.where(c["same"] & (row <= col), 1.0, 0.0).astype(MXU)
        dlf = _mm_exact_l(upper, db) + dbend
        df = dlf / c["f"] - dk
        sg, sq, q = c["sg"], c["sq"], c["q"]
        dlb_ref[...] += jnp.sum(df * (1.0 - sg), axis=0, keepdims=True)
        dz_ref[:, DA:2 * DA] = (df * (1.0 - c["lb"]) * sg * (1.0 - sg)).astype(dz_ref.dtype)
        dz_ref[:, 0:DA] = (dqs * (sq * (1.0 + q * (1.0 - sq)))).astype(dz_ref.dtype)

    rev = lambda c: (nblk - 1 - c, 0)
    return pl.pallas_call(
        body, name="hgrn_bwd", grid=(nblk,),
        in_specs=[pl.BlockSpec((rb, 4 * DA), rev),
                  pl.BlockSpec((1, DA), lambda c: (0, 0)),
                  pl.BlockSpec((1, DA), lambda c: (0, 0)),
                  pl.BlockSpec((rb, DA), rev),
                  pl.BlockSpec((cb, NH, HD, HD), lambda c: (nblk - 1 - c, 0, 0, 0)),
                  pl.BlockSpec((rb, DA), rev)],
        out_specs=[pl.BlockSpec((rb, 4 * DA), rev),
                   pl.BlockSpec((1, DA), lambda c: (0, 0)),
                   pl.BlockSpec((1, DA), lambda c: (0, 0))],
        out_shape=[jax.ShapeDtypeStruct((t, DIN), SAVE), jax.ShapeDtypeStruct((1, DA), F32),
                   jax.ShapeDtypeStruct((1, DA), F32)],
        scratch_shapes=[pltpu.VMEM((NH, HD, HD), F32)],
        compiler_params=_params(("arbitrary",)),
    )(z, lb, gain, o, states, doa)


def _shift_down(prev8, x, k):
    cat = jnp.concatenate([prev8, x], axis=0)
    return pltpu.roll(cat, k, axis=0)[8:, :]


def _shift_up(x, next8, k):
    n = x.shape[0]
    cat = jnp.concatenate([x, next8], axis=0)
    return pltpu.roll(cat, n + 8 - k, axis=0)[:n, :]


def _lru_gates(x, prev8, cw_ref, vec_ref, wa_ref, wx_ref):
    xs = [x, _shift_down(prev8, x, 1), _shift_down(prev8, x, 2), _shift_down(prev8, x, 3)]
    xc = vec_ref[0:1, :] + cw_ref[3:4, :] * xs[0] + cw_ref[2:3, :] * xs[1] + cw_ref[1:2, :] * xs[2] + cw_ref[0:1, :] * xs[3]
    r = _sigmoid(_mm(xc, wa_ref[...]) + vec_ref[1:2, :])
    gi = _sigmoid(_mm(xc, wx_ref[...]) + vec_ref[2:3, :])
    lam = vec_ref[3:4, :]
    sp = jnp.maximum(-lam, 0.0) + jnp.log(1.0 + jnp.exp(-jnp.abs(lam)))
    la = -LRU_C * r * sp
    a = jnp.exp(la)
    mult = jnp.sqrt(-_expm1(2.0 * la))
    return xs, xc, r, gi, sp, a, mult


def _scan_down(a, u):
    n = a.shape[0]
    row = _iota(a.shape, 0)
    s = 1
    while s < n:
        keep = row >= s
        ash = jnp.where(keep, pltpu.roll(a, s, axis=0), 1.0)
        ush = jnp.where(keep, pltpu.roll(u, s, axis=0), 0.0)
        u = a * ush + u
        a = a * ash
        s *= 2
    return a, u


def _scan_up(a, u):
    n = a.shape[0]
    row = _iota(a.shape, 0)
    s = 1
    while s < n:
        keep = row < n - s
        ash = jnp.where(keep, pltpu.roll(a, n - s, axis=0), 1.0)
        ush = jnp.where(keep, pltpu.roll(u, n - s, axis=0), 0.0)
        u = a * ush + u
        a = a * ash
        s *= 2
    return a, u


def _lru_fwd(z, cw, vec, wa, wx, tb):
    t = z.shape[0]
    xcol, gcol = (4 * DA) // DB, (4 * DA) // DB + 1

    def body(x_ref, gate_ref, cw_ref, vec_ref, wa_ref, wx_ref, ob_ref, h_ref, xprev_scr, hc_scr):
        @pl.when(pl.program_id(0) == 0)
        def _():
            xprev_scr[...] = jnp.zeros_like(xprev_scr)
            hc_scr[...] = jnp.zeros_like(hc_scr)

        x = x_ref[...]
        _, xc, _, gi, _, a, mult = _lru_gates(x, xprev_scr[...], cw_ref, vec_ref, wa_ref, wx_ref)
        acum, hloc = _scan_down(a, mult * gi * xc)
        h = hloc + acum * hc_scr[0:1, :]
        h_ref[...] = h
        hc_scr[...] = jnp.broadcast_to(_row(h, tb - 1), hc_scr.shape)
        xprev_scr[...] = x[tb - 8:, :]
        y = h * _gelu(gate_ref[...])
        ms = _mm_exact_r(y * y, _group_matrix(DB, 1.0 / GRP).astype(MXU))
        ob_ref[...] = (y * lax.rsqrt(ms + EPS) * vec_ref[4:5, :]).astype(ob_ref.dtype)

    return pl.pallas_call(
        body, name="lru_fwd", grid=(t // tb,),
        in_specs=[pl.BlockSpec((tb, DB), lambda i: (i, xcol)),
                  pl.BlockSpec((tb, DB), lambda i: (i, gcol)),
                  pl.BlockSpec((8, DB), lambda i: (0, 0)),
                  pl.BlockSpec((8, DB), lambda i: (0, 0)),
                  pl.BlockSpec((DB, DB), lambda i: (0, 0)),
                  pl.BlockSpec((DB, DB), lambda i: (0, 0))],
        out_specs=[pl.BlockSpec((tb, DB), lambda i: (i, 0)),
                   pl.BlockSpec((tb, DB), lambda i: (i, 0))],
        out_shape=[jax.ShapeDtypeStruct((t, DB), SAVE), jax.ShapeDtypeStruct((t, DB), F32)],
        scratch_shapes=[pltpu.VMEM((8, DB), F32), pltpu.VMEM((8, DB), F32)],
        compiler_params=_params(("arbitrary",)),
    )(z, z, cw, vec, wa, wx)


def _lru_bwd(z, hseq, dob, cw, vec, wa, wx, dz, tb):
    t = z.shape[0]
    nb = t // tb
    xcol, gcol = (4 * DA) // DB, (4 * DA) // DB + 1
    per = tb // 8

    def body(x_ref, xh_ref, gate_ref, h_ref, hh_ref, dob_ref, cw_ref, vec_ref, wa_ref, wx_ref, _,
             dz_ref, dcw_ref, dvec_ref, dwa_ref, dwx_ref, gc_scr, an_scr, dxc_scr):
        step = pl.program_id(0)
        blk = nb - 1 - step

        @pl.when(step == 0)
        def _():
            for ref in (gc_scr, an_scr, dxc_scr, dcw_ref, dvec_ref, dwa_ref, dwx_ref):
                ref[...] = jnp.zeros_like(ref)

        first = (blk > 0).astype(F32)
        x = x_ref[...]
        xs, xc, r, gi, sp, a, mult = _lru_gates(x, xh_ref[...] * first, cw_ref, vec_ref, wa_ref, wx_ref)
        h = h_ref[...]
        hprev = _shift_down(hh_ref[...] * first, h, 1)
        ge, dge = _gelu_and_grad(gate_ref[...])
        y = h * ge
        gmat = _group_matrix(DB, 1.0 / GRP).astype(MXU)
        rstd = lax.rsqrt(_mm_exact_r(y * y, gmat) + EPS)
        n = y * rstd
        d_ob = dob_ref[...]
        dn = d_ob * vec_ref[4:5, :]
        dvec_ref[4:5, :] += jnp.sum(d_ob * n, axis=0, keepdims=True)
        dy = rstd * (dn - n * _mm_exact_r(dn * n, gmat))
        dh = dy * ge
        dgate = dy * h * dge

        row = _iota(a.shape, 0)
        anext = jnp.where(row == tb - 1, an_scr[0:1, :], pltpu.roll(a, tb - 1, axis=0))
        acum, gloc = _scan_up(anext, dh)
        g = gloc + acum * gc_scr[0:1, :]
        gc_scr[...] = jnp.broadcast_to(_row(g, 0), gc_scr.shape)
        an_scr[...] = jnp.broadcast_to(_row(a, 0), an_scr.shape)

        da = g * hprev
        dmult = g * gi * xc
        dgi = g * mult * xc
        dxc = g * mult * gi
        dla = da * a - dmult * (a * a) / mult
        dr = dla * (-LRU_C * sp)
        dsp = jnp.sum(dla * (-LRU_C * r), axis=0, keepdims=True)
        lam = vec_ref[3:4, :]
        dvec_ref[3:4, :] += -dsp * _sigmoid(-lam)
        dpa = dr * r * (1.0 - r)
        dpx = dgi * gi * (1.0 - gi)
        dwa_ref[...] += _mm_tn(xc, dpa)
        dwx_ref[...] += _mm_tn(xc, dpx)
        dvec_ref[1:2, :] += jnp.sum(dpa, axis=0, keepdims=True)
        dvec_ref[2:3, :] += jnp.sum(dpx, axis=0, keepdims=True)
        dxc = dxc + _mm_nt(dpa, wa_ref[...]) + _mm_nt(dpx, wx_ref[...])
        dvec_ref[0:1, :] += jnp.sum(dxc, axis=0, keepdims=True)
        for tap in range(4):
            dcw_ref[tap:tap + 1, :] += jnp.sum(dxc * xs[3 - tap], axis=0, keepdims=True)
        nxt = dxc_scr[...]
        dx = (cw_ref[3:4, :] * dxc + cw_ref[2:3, :] * _shift_up(dxc, nxt, 1)
              + cw_ref[1:2, :] * _shift_up(dxc, nxt, 2) + cw_ref[0:1, :] * _shift_up(dxc, nxt, 3))
        dxc_scr[...] = dxc[:8, :]
        dz_ref[:, :DB] = dx.astype(dz_ref.dtype)
        dz_ref[:, DB:] = dgate.astype(dz_ref.dtype)

    def halo(col):
        return lambda s: (jnp.maximum((nb - 1 - s) * per - 1, 0), col)

    const = lambda s: (0, 0)
    return pl.pallas_call(
        body, name="lru_bwd", grid=(nb,),
        in_specs=[pl.BlockSpec((tb, DB), lambda s: (nb - 1 - s, xcol)),
                  pl.BlockSpec((8, DB), halo(xcol)),
                  pl.BlockSpec((tb, DB), lambda s: (nb - 1 - s, gcol)),
                  pl.BlockSpec((tb, DB), lambda s: (nb - 1 - s, 0)),
                  pl.BlockSpec((8, DB), halo(0)),
                  pl.BlockSpec((tb, DB), lambda s: (nb - 1 - s, 0)),
                  pl.BlockSpec((8, DB), const), pl.BlockSpec((8, DB), const),
                  pl.BlockSpec((DB, DB), const), pl.BlockSpec((DB, DB), const),
                  pl.BlockSpec(memory_space=pl.ANY)],
        out_specs=[pl.BlockSpec((tb, 2 * DB), lambda s: (nb - 1 - s, (4 * DA) // (2 * DB))),
                   pl.BlockSpec((8, DB), const), pl.BlockSpec((8, DB), const),
                   pl.BlockSpec((DB, DB), const), pl.BlockSpec((DB, DB), const)],
        out_shape=[jax.ShapeDtypeStruct((t, DIN), SAVE), jax.ShapeDtypeStruct((8, DB), F32),
                   jax.ShapeDtypeStruct((8, DB), F32), jax.ShapeDtypeStruct((DB, DB), F32),
                   jax.ShapeDtypeStruct((DB, DB), F32)],
        scratch_shapes=[pltpu.VMEM((8, DB), F32), pltpu.VMEM((8, DB), F32), pltpu.VMEM((8, DB), F32)],
        input_output_aliases={10: 0},
        compiler_params=_params(("arbitrary",)),
    )(z, z, z, hseq, hseq, dob, cw, vec, wa, wx, dz)


def _sgu_block(u_ref, v_ref, w_ref, b_ref, gmat, tb):
    uu, duu = _gelu_and_grad(u_ref[...])
    vv, dvv = _gelu_and_grad(v_ref[...])
    dlt = vv - _mm_exact_r(vv, gmat)
    rstd_v = lax.rsqrt(_mm_exact_r(dlt * dlt, gmat) + EPS)
    vn = dlt * rstd_v
    col = _iota((CCH, DC), 1) // GRP
    causal = _iota((CCH, CCH), 0) >= _iota((CCH, CCH), 1)
    ws = [jnp.where(causal, w_ref[g], 0.0) for g in range(DC // GRP)]
    zs = []
    for ch in range(tb // CCH):
        vn_c = vn[ch * CCH:(ch + 1) * CCH]
        zz = b_ref[...]
        for g, w in enumerate(ws):
            zz = zz + jnp.where(col == g, _mm(w, vn_c), 0.0)
        zs.append(zz)
    return uu, duu, dvv, rstd_v, vn, jnp.concatenate(zs, axis=0), ws, col, causal


def _sgu_fwd(z, w, bias, gain, tb):
    t = z.shape[0]
    ucol, vcol = (4 * DA + 2 * DB) // DC, (4 * DA + 2 * DB) // DC + 1

    def body(u_ref, v_ref, w_ref, b_ref, g_ref, oc_ref):
        gmat = _group_matrix(DC, 1.0 / GRP).astype(MXU)
        uu, _, _, _, _, zz, _, _, _ = _sgu_block(u_ref, v_ref, w_ref, b_ref, gmat, tb)
        y = uu * zz
        ms = _mm_exact_r(y * y, gmat)
        oc_ref[...] = (y * lax.rsqrt(ms + EPS) * g_ref[...]).astype(oc_ref.dtype)

    const = lambda i: (0, 0)
    return pl.pallas_call(
        body, name="sgu_fwd", grid=(t // tb,),
        in_specs=[pl.BlockSpec((tb, DC), lambda i: (i, ucol)),
                  pl.BlockSpec((tb, DC), lambda i: (i, vcol)),
                  pl.BlockSpec((DC // GRP, CCH, CCH), lambda i: (0, 0, 0)),
                  pl.BlockSpec((CCH, DC), const), pl.BlockSpec((1, DC), const)],
        out_specs=pl.BlockSpec((tb, DC), lambda i: (i, 0)),
        out_shape=jax.ShapeDtypeStruct((t, DC), SAVE),
        compiler_params=_params(("parallel",)),
    )(z, z, w, bias, gain)


def _sgu_bwd(z, doc, w, bias, gain, dz, tb):
    t = z.shape[0]
    nb = t // tb
    ucol, vcol = (4 * DA + 2 * DB) // DC, (4 * DA + 2 * DB) // DC + 1
    ng = DC // GRP

    def body(u_ref, v_ref, doc_ref, w_ref, b_ref, g_ref, _, dz_ref, dw_ref, dbias_ref, dgain_ref, dbsum_scr):
        i = pl.program_id(0)

        @pl.when(i == 0)
        def _():
            for ref in (dw_ref, dgain_ref, dbsum_scr):
                ref[...] = jnp.zeros_like(ref)

        gmat = _group_matrix(DC, 1.0 / GRP).astype(MXU)
        uu, duu, dvv, rstd_v, vn, zz, ws, col, causal = _sgu_block(u_ref, v_ref, w_ref, b_ref, gmat, tb)
        y = uu * zz
        rstd = lax.rsqrt(_mm_exact_r(y * y, gmat) + EPS)
        n = y * rstd
        d_oc = doc_ref[...]
        dn = d_oc * g_ref[...]
        dgain_ref[0:1, :] += jnp.sum(d_oc * n, axis=0, keepdims=True)
        dy = rstd * (dn - n * _mm_exact_r(dn * n, gmat))
        dzz = dy * uu
        dz_ref[:, :DC] = (dy * zz * duu).astype(dz_ref.dtype)
        dvns = []
        for ch in range(tb // CCH):
            rows = slice(ch * CCH, (ch + 1) * CCH)
            dzz_c, vn_c = dzz[rows], vn[rows]
            dbsum_scr[...] += dzz_c
            dvn = jnp.zeros_like(dzz_c)
            for g in range(ng):
                sel = col == g
                dvn = dvn + jnp.where(sel, _mm_tn(ws[g], dzz_c), 0.0)
                dw_ref[g] += jnp.where(causal, _mm_nt(jnp.where(sel, dzz_c, 0.0), vn_c), 0.0)
            dvns.append(dvn)
        dvn = jnp.concatenate(dvns, axis=0)
        dv = rstd_v * (dvn - _mm_exact_r(dvn, gmat) - vn * _mm_exact_r(dvn * vn, gmat))
        dz_ref[:, DC:] = (dv * dvv).astype(dz_ref.dtype)

        @pl.when(i == nb - 1)
        def _():
            dbias_ref[...] = _mm_exact_r(dbsum_scr[...], _group_matrix(DC, 1.0).astype(MXU))

    const = lambda i: (0, 0)
    return pl.pallas_call(
        body, name="sgu_bwd", grid=(nb,),
        in_specs=[pl.BlockSpec((tb, DC), lambda i: (i, ucol)),
                  pl.BlockSpec((tb, DC), lambda i: (i, vcol)),
                  pl.BlockSpec((tb, DC), lambda i: (i, 0)),
                  pl.BlockSpec((ng, CCH, CCH), lambda i: (0, 0, 0)),
                  pl.BlockSpec((CCH, DC), const), pl.BlockSpec((1, DC), const),
                  pl.BlockSpec(memory_space=pl.ANY)],
        out_specs=[pl.BlockSpec((tb, 2 * DC), lambda i: (i, (4 * DA + 2 * DB) // (2 * DC))),
                   pl.BlockSpec((ng, CCH, CCH), lambda i: (0, 0, 0)),
                   pl.BlockSpec((CCH, DC), const), pl.BlockSpec((8, DC), const)],
        out_shape=[jax.ShapeDtypeStruct((t, DIN), SAVE), jax.ShapeDtypeStruct((ng, CCH, CCH), F32),
                   jax.ShapeDtypeStruct((CCH, DC), F32), jax.ShapeDtypeStruct((8, DC), F32)],
        scratch_shapes=[pltpu.VMEM((CCH, DC), F32)],
        input_output_aliases={6: 0},
        compiler_params=_params(("arbitrary",)),
    )(z, z, doc, w, bias, gain, dz)


def _head(h, gain, target, tm):
    t = h.shape[0]

    def body(h_ref, g_ref, t_ref, dh_ref, loss_ref, dgain_ref):
        @pl.when(pl.program_id(0) == 0)
        def _():
            loss_ref[...] = jnp.zeros_like(loss_ref)
            dgain_ref[...] = jnp.zeros_like(dgain_ref)

        hh = h_ref[...]
        gain = g_ref[...]
        rstd = lax.rsqrt(jnp.mean(hh * hh, axis=-1, keepdims=True) + EPS)
        xhat = hh * rstd
        err = xhat * gain - t_ref[...]
        per_tok = jnp.mean(err * err, axis=-1, keepdims=True)
        loss_ref[...] += 0.5 * jnp.sum(per_tok, axis=0, keepdims=True)
        dy = err * (1.0 / D)
        dgain_ref[...] += jnp.sum(dy * xhat, axis=0, keepdims=True)
        dxh = dy * gain
        dh_ref[...] = rstd * (dxh - xhat * jnp.mean(dxh * xhat, axis=-1, keepdims=True))

    return pl.pallas_call(
        body, name="head", grid=(t // tm,),
        in_specs=[pl.BlockSpec((tm, D), lambda i: (i, 0)),
                  pl.BlockSpec((1, D), lambda i: (0, 0)),
                  pl.BlockSpec((tm, D), lambda i: (i, 0))],
        out_specs=[pl.BlockSpec((tm, D), lambda i: (i, 0)),
                   pl.BlockSpec((1, 128), lambda i: (0, 0)),
                   pl.BlockSpec((1, D), lambda i: (0, 0))],
        out_shape=[jax.ShapeDtypeStruct((t, D), F32), jax.ShapeDtypeStruct((1, 128), F32),
                   jax.ShapeDtypeStruct((1, D), F32)],
        compiler_params=_params(("arbitrary",)),
    )(h, gain, target)


def _adamw(w, g, m, v):
    m = ADAM_B1 * m + (1.0 - ADAM_B1) * g
    v = ADAM_B2 * v + (1.0 - ADAM_B2) * (g * g)
    m_hat = m / (1.0 - ADAM_B1 ** ADAM_STEP)
    v_hat = v / (1.0 - ADAM_B2 ** ADAM_STEP)
    delta = -ADAM_LR * (m_hat / (jnp.sqrt(v_hat) + ADAM_EPS) + ADAM_WD * w)
    return delta, m, v


def _adamw_big(recv, w, m, v, tr, name, after, transposed=False):
    depth, rows, cols = w.shape
    rspec = (pl.BlockSpec((NDEV, cols, tr), lambda i: (0, 0, i)) if transposed
             else pl.BlockSpec((NDEV, tr, cols), lambda i: (0, i, 0)))

    def body(*refs):
        r_refs = refs[:depth]
        w_ref, m_ref, v_ref, _, g_out, d_out, m_out, v_out = refs[depth:]
        for l in range(depth):
            g = r_refs[l][0].astype(F32)
            for k in range(1, NDEV):
                g = g + r_refs[l][k].astype(F32)
            if transposed:
                g = g.T
            delta, m_, v_ = _adamw(w_ref[l], g, m_ref[l], v_ref[l])
            g_out[l] = g
            d_out[l] = delta
            m_out[l] = m_
            v_out[l] = v_

    spec = pl.BlockSpec((depth, tr, cols), lambda i: (0, i, 0))
    return pl.pallas_call(
        body, name=name, grid=(rows // tr,),
        in_specs=[rspec] * depth + [spec] * 3
        + [pl.BlockSpec(memory_space=pl.ANY)],
        out_specs=[spec] * 4, out_shape=[jax.ShapeDtypeStruct((depth, rows, cols), F32)] * 4,
        compiler_params=_params(("parallel",)),
    )(*recv, w, m, v, after)


def _sum_devices(recv):
    _, r, _ = recv.shape

    def body(r_ref, out_ref):
        g = r_ref[0]
        for k in range(1, NDEV):
            g = g + r_ref[k]
        out_ref[...] = g

    return pl.pallas_call(body, name="sum_devices", out_shape=jax.ShapeDtypeStruct((r, 128), F32))(recv)


def _adamw_small(w, g, m, v):
    def body(w_ref, g_ref, m_ref, v_ref, d_out, m_out, v_out):
        delta, m_, v_ = _adamw(w_ref[...], g_ref[...], m_ref[...], v_ref[...])
        d_out[...] = delta
        m_out[...] = m_
        v_out[...] = v_

    return pl.pallas_call(body, name="adamw_small", out_shape=[jax.ShapeDtypeStruct(w.shape, F32)] * 3)(w, g, m, v)


def _pack(arrs):
    flat = jnp.concatenate([a.reshape(-1) for a in arrs])
    pad = (-flat.shape[0]) % 1024
    return jnp.pad(flat, (0, pad)).reshape(-1, 128)


def _unpack(buf, like):
    flat = buf.reshape(-1)
    out, off = [], 0
    for a in like:
        out.append(flat[off:off + a.size].reshape(a.shape))
        off += a.size
    return out


def _block_diag(w):
    nb, bd, _ = w.shape
    eye = jnp.eye(nb, dtype=w.dtype)
    return (eye[:, None, :, None] * w[:, :, None, :]).reshape(nb * bd, nb * bd)


def _diag_blocks(w):
    nb = w.shape[0] // GRP
    return jnp.stack([w[g * GRP:(g + 1) * GRP, g * GRP:(g + 1) * GRP] for g in range(nb)])


SMALL = ['ffn1_norm', 'mix_norm', 'hgrn_lb_logits', 'hgrn_norm', 'conv_b', 'lru_wa', 'lru_ba', 'lru_wx', 'lru_bx',
         'lru_lambda', 'lru_norm', 'sgu_w', 'sgu_b', 'sgu_norm', 'ffn2_norm', 'final_norm']
NAMES = ['ffn1_norm', 'ffn1_wg', 'ffn1_wu', 'ffn1_wd', 'mix_norm', 'w_in', 'hgrn_lb_logits', 'hgrn_norm', 'conv_w',
         'conv_b', 'lru_wa', 'lru_ba', 'lru_wx', 'lru_bx', 'lru_lambda', 'lru_norm', 'sgu_w', 'sgu_b', 'sgu_norm',
         'w_out', 'ffn2_norm', 'ffn2_wg', 'ffn2_wu', 'ffn2_wd', 'final_norm']


def _step(x, target, w, m, v):
    depth = w['ffn1_wg'].shape[0]
    t = x.shape[1]
    h = x.reshape(t, D)
    target = target.reshape(t, D)
    tm_f, tm_b, tb = min(TM_F, t), min(TM_B, t), min(TB, t)
    my = 4 * lax.axis_index("x") + 2 * lax.axis_index("y") + lax.axis_index("c")

    cw_tile = jnp.pad(w['conv_w'].reshape(-1, 128), ((0, 8 - depth), (0, 0)))
    lbs, lb_soft = _lower_bounds(w['hgrn_lb_logits'])

    def row(a):
        return a.reshape(1, -1)

    def tr(a):
        return jnp.swapaxes(a, -1, -2)

    def shards(l, unit):
        if unit == 1:
            return [tr(w['w_in'][l]).astype(WIRE), w['w_out'][l].astype(WIRE)]
        f = 'ffn1' if unit == 0 else 'ffn2'
        return [tr(w[f + '_wg'][l]).astype(WIRE), tr(w[f + '_wu'][l]).astype(WIRE), w[f + '_wd'][l].astype(WIRE)]

    units = [(l, u) for l in range(depth) for u in range(3)]

    def start_ici(idx, deps=()):
        return _transfer_start(shards(*units[idx]), True, "gather_ici_%d_%d" % units[idx], deps=deps)

    def start_d2d(idx, handle, after):
        lands = _transfer_wait(handle, after, "gather_ici_wait_%d_%d" % units[idx])
        return _forward_start(lands, "gather_d2d_%d_%d" % units[idx])

    pipe = dict(idx=0)
    first = start_ici(0)
    pipe['ici'] = start_ici(1, deps=(first['token'],))
    conv_flight = _transfer_start([cw_tile], True, "gather_conv_start", deps=(pipe['ici']['token'],), direct=True)
    pipe['d2d'] = start_d2d(0, first, conv_flight['token'])

    def next_weights(after):
        idx = pipe['idx']
        lands = _transfer_wait(pipe['d2d'], after, "gather_d2d_wait_%d_%d" % units[idx])
        pipe['idx'] = idx + 1
        tok = 0.0
        if idx + 1 < len(units):
            pipe['d2d'] = start_d2d(idx + 1, pipe['ici'], lands[-1])
            tok = pipe['d2d']['token'][0, 0]
            if idx + 2 < len(units):
                pipe['ici'] = start_ici(idx + 2, deps=(pipe['d2d']['token'],))
                tok = pipe['ici']['token'][0, 0]
        return lands, tok

    saved = []
    for l in range(depth):
        lands, tok = next_weights(h)
        s = dict(ffn1=[a.reshape(FF, D) for a in lands], h0=h)
        h, s['xn1'], s['a1'], s['b1'] = _ffn_fwd(h, row(w['ffn1_norm'][l]) + tok, *s['ffn1'], tm_f)
        s['h1'] = h
        (win, wout), tok = next_weights(h)
        win, wout = win.reshape(DIN, D), wout.reshape(D, D)
        s['win'], s['wout'] = win, wout
        z, s['xnm'] = _inproj_fwd(h, row(w['mix_norm'][l]) + tok, win, tm_f)
        s['z'] = z
        s['o'], oa, s['states'] = _hgrn_fwd(z, row(lbs[l]), row(w['hgrn_norm'][l]))
        if l == 0:
            cw_all = _transfer_wait(conv_flight, z, "gather_conv_wait")[0][:, :depth]
            conv_w = jnp.moveaxis(cw_all.reshape(NDEV, depth, 4, DB // NDEV), 0, 2).reshape(depth, 4, DB)
        s['cw'] = jnp.pad(conv_w[l], ((0, 4), (0, 0)))
        s['vec'] = jnp.concatenate([row(w['conv_b'][l]), row(w['lru_ba'][l]), row(w['lru_bx'][l]),
                                    row(w['lru_lambda'][l]), row(w['lru_norm'][l]), jnp.zeros((3, DB), F32)])
        s['wa'], s['wx'] = _block_diag(w['lru_wa'][l]), _block_diag(w['lru_wx'][l])
        ob, s['hseq'] = _lru_fwd(z, s['cw'], s['vec'], s['wa'], s['wx'], tb)
        s['bias'] = jnp.repeat(w['sgu_b'][l].T, GRP, axis=1)
        oc = _sgu_fwd(z, w['sgu_w'][l], s['bias'], row(w['sgu_norm'][l]), tb)
        s['oa'], s['ob'], s['oc'] = oa, ob, oc
        h = _outproj_fwd(h, oa, ob, oc, wout, tm_f)
        s['h2'] = h
        lands, tok = next_weights(h)
        s['ffn2'] = [a.reshape(FF, D) for a in lands]
        h, s['xn2'], s['a2'], s['b2'] = _ffn_fwd(h, row(w['ffn2_norm'][l]) + tok, *s['ffn2'], tm_f)
        saved.append(s)

    dh, loss_part, g_final = _head(h, row(w['final_norm']), target, tm_f)
    loss = lax.psum(loss_part[0, 0], ("x", "y", "c"))

    recv = {k: [None] * depth for k in ('wg1', 'wu1', 'wd1', 'wg2', 'wu2', 'wd2', 'win', 'wout')}
    flight = []

    def land(after):
        handle, kinds, l = flight.pop()
        for k, a in zip(kinds, _transfer_wait(handle, after, f"exchange_wait_{kinds[0]}_{l}")):
            recv[k][l] = a

    def exchange(arrs, kinds, l, deps=()):
        handle = _transfer_start(arrs, False, f"exchange_start_{kinds[0]}_{l}", deps=deps)
        if flight:
            land(handle['token'])
        flight.append((handle, kinds, l))
        return handle['token'][0, 0]

    small = {k: [None] * depth for k in SMALL if k != 'final_norm'}
    dconv = [None] * depth
    dlb = [None] * depth
    tok = 0.0
    for l in reversed(range(depth)):
        s = saved[l]
        dh, g, *cot = _ffn_bwd_x(dh, s['h2'], row(w['ffn2_norm'][l]) + tok, s['a2'], s['b2'], *s['ffn2'], tm_f)
        dws = _ffn_bwd_w(s['xn2'], *cot, tm_f)
        tok = exchange([a.reshape(NDEV, FFS, D) for a in dws], ('wg2', 'wu2', 'wd2'), l)
        small['ffn2_norm'][l] = g
        doa, dob, doc, dwout = _outproj_bwd(dh, s['oa'], s['ob'], s['oc'], s['wout'], tm_f)
        dz, g_hn, dlb[l] = _hgrn_bwd(s['z'], row(lbs[l]), row(w['hgrn_norm'][l]) + tok, s['o'], s['states'], doa)
        small['hgrn_norm'][l] = g_hn
        dz, dcw, dvec, dwa, dwx = _lru_bwd(s['z'], s['hseq'], dob, s['cw'], s['vec'], s['wa'], s['wx'], dz, tb)
        dconv[l] = dcw[:4]
        small['conv_b'][l], small['lru_ba'][l], small['lru_bx'][l] = dvec[0], dvec[1].reshape(4, GRP), dvec[2].reshape(4, GRP)
        small['lru_lambda'][l], small['lru_norm'][l] = dvec[3], dvec[4]
        small['lru_wa'][l], small['lru_wx'][l] = _diag_blocks(dwa), _diag_blocks(dwx)
        dz, dsw, dbias, dgc = _sgu_bwd(s['z'], doc, w['sgu_w'][l], s['bias'], row(w['sgu_norm'][l]), dz, tb)
        small['sgu_w'][l], small['sgu_b'][l], small['sgu_norm'][l] = dsw, dbias[:, ::GRP].T, dgc[0]
        dwin = _inproj_bwd_w(s['xnm'], dz, tm_f)
        tok = exchange([dwin.reshape(NDEV, DINS, D), dwout.reshape(NDEV, D // NDEV, D)], ('win', 'wout'), l)
        dh, g = _inproj_bwd_x(dh, dz, s['h1'], row(w['mix_norm'][l]) + tok, s['win'], tm_b)
        small['mix_norm'][l] = g
        tok = 0.0
        if l == 0:
            small['ffn1_norm'][0] = jnp.zeros((1, D), F32)
            small['hgrn_lb_logits'] = list(_lower_bounds_bwd(lb_soft, jnp.concatenate(dlb, axis=0)))
            parts = [jnp.stack([small[k][j].reshape(w[k].shape[1:]) for j in range(depth)])
                     for k in SMALL if k != 'final_norm']
            parts += [g_final.reshape(D), jnp.stack(dconv)]
            small_flight = _transfer_start([_pack(parts)], True, "gather_small_start", direct=True)
            tok = small_flight['token'][0, 0]
        dh, g, *cot = _ffn_bwd_x(dh, s['h0'], row(w['ffn1_norm'][l]) + tok, s['a1'], s['b1'], *s['ffn1'], tm_f)
        dws = _ffn_bwd_w(s['xn1'], *cot, tm_f)
        before = ()
        if l == 0:
            g_last = _all_gather([g.reshape(8, 128)], "gather_last")[0]
            before = (g_last,)
        else:
            small['ffn1_norm'][l] = g
        tok = exchange([a.reshape(NDEV, FFS, D) for a in dws], ('wg1', 'wu1', 'wd1'), l, before)
    grad_x = dh.reshape(1, t, D)

    out = {}
    last = flight[0][0]['token']

    def ffn_update(f, n, after):
        for kind in ('wg', 'wu'):
            k = f + '_' + kind
            res = _adamw_big(recv[kind + n], tr(w[k]), tr(m[k]), tr(v[k]), 32, "adamw_ffn", after)
            out[k] = tuple(tr(a) for a in res)
        k = f + '_wd'
        out[k] = _adamw_big(recv['wd' + n], w[k], m[k], v[k], 32, "adamw_ffn", after)

    ffn_update('ffn2', '2', last)
    out['w_in'] = _adamw_big(recv['win'], w['w_in'], m['w_in'], v['w_in'], 128, "adamw_win", last, transposed=True)
    out['w_out'] = _adamw_big(recv['wout'], w['w_out'], m['w_out'], v['w_out'], 64, "adamw_wout", last)

    total = _sum_devices(_transfer_wait(small_flight, g_last, "gather_small_wait")[0])
    like = [w[k] for k in SMALL] + [jax.ShapeDtypeStruct((depth, 4, DB), F32)]
    grads = _unpack(total, like)
    gsmall = dict(zip(SMALL, grads[:-1]))
    gsmall['ffn1_norm'] = gsmall['ffn1_norm'].at[0].set(_sum_devices(g_last).reshape(D))
    gsmall['conv_w'] = lax.dynamic_slice_in_dim(grads[-1], my * (DB // NDEV), DB // NDEV, axis=2)
    keys = SMALL + ['conv_w']
    dl, mm, vv = _adamw_small(_pack([w[k] for k in keys]), _pack([gsmall[k] for k in keys]),
                              _pack([m[k] for k in keys]), _pack([v[k] for k in keys]))
    like = [w[k] for k in keys]
    for k, d_, m_, v_ in zip(keys, _unpack(dl, like), _unpack(mm, like), _unpack(vv, like)):
        out[k] = (gsmall[k], d_, m_, v_)
    done = [dl] + [out[k][1][0] for k in ('ffn2_wg', 'ffn2_wu', 'ffn2_wd', 'w_in', 'w_out')]
    land(functools.reduce(lambda p, q: p + q, [a[:1, :1] for a in done]))
    ffn_update('ffn1', '1', last)

    return (loss, grad_x, *[out[k][0] for k in NAMES], *[out[k][1] for k in NAMES],
            *[out[k][2] for k in NAMES], *[out[k][3] for k in NAMES])


def kernel(x, ffn1_norm, ffn1_wg, ffn1_wu, ffn1_wd, mix_norm, w_in, hgrn_lb_logits, hgrn_norm, conv_w, conv_b, lru_wa, lru_ba, lru_wx, lru_bx, lru_lambda, lru_norm, sgu_w, sgu_b, sgu_norm, w_out, ffn2_norm, ffn2_wg, ffn2_wu, ffn2_wd, final_norm, loss_target, m_ffn1_norm, m_ffn1_wg, m_ffn1_wu, m_ffn1_wd, m_mix_norm, m_w_in, m_hgrn_lb_logits, m_hgrn_norm, m_conv_w, m_conv_b, m_lru_wa, m_lru_ba, m_lru_wx, m_lru_bx, m_lru_lambda, m_lru_norm, m_sgu_w, m_sgu_b, m_sgu_norm, m_w_out, m_ffn2_norm, m_ffn2_wg, m_ffn2_wu, m_ffn2_wd, m_final_norm, v_ffn1_norm, v_ffn1_wg, v_ffn1_wu, v_ffn1_wd, v_mix_norm, v_w_in, v_hgrn_lb_logits, v_hgrn_norm, v_conv_w, v_conv_b, v_lru_wa, v_lru_ba, v_lru_wx, v_lru_bx, v_lru_lambda, v_lru_norm, v_sgu_w, v_sgu_b, v_sgu_norm, v_w_out, v_ffn2_norm, v_ffn2_wg, v_ffn2_wu, v_ffn2_wd, v_final_norm):
    args = locals()
    w = {k: args[k] for k in NAMES}
    m = {k: args['m_' + k] for k in NAMES}
    v = {k: args['v_' + k] for k in NAMES}
    return _step(x, loss_target, w, m, v)
```

```python
import functools

import jax
import jax.numpy as jnp
from jax import lax
from jax.experimental import pallas as pl
from jax.experimental.pallas import tpu as pltpu

F32 = jnp.float32
MXU = jnp.bfloat16
SAVE = jnp.bfloat16
WIRE = jnp.bfloat16

NDEV = 8
D = 1024
FF = 2816
FFS = FF // NDEV
FB = 256
FBX = FF // 2
DIN = 3072
DINS = DIN // NDEV
ZB = 512
DA, DB, DC = 512, 256, 256
HD = 128
NH = DA // HD
ACH = 64
ACB = 4
CCH = 128
GRP = 64
EPS = 1e-6
LRU_C = 8.0
VMEM_LIMIT = 60 * 1024 * 1024
TM_F = 1024
TM_B = 512
TB = 1024
SUB = 256

ADAM_LR, ADAM_B1, ADAM_B2, ADAM_EPS, ADAM_WD, ADAM_STEP = 0.001, 0.9, 0.999, 1e-08, 0.01, 10

MESH = pl.DeviceIdType.MESH


def _mm(a, b):
    return jnp.dot(a.astype(MXU), b.astype(MXU), preferred_element_type=F32)


def _mm_nt(a, b):
    return lax.dot_general(a.astype(MXU), b.astype(MXU), (((1,), (1,)), ((), ())), preferred_element_type=F32)


def _mm_tn(a, b):
    return lax.dot_general(a.astype(MXU), b.astype(MXU), (((0,), (0,)), ((), ())), preferred_element_type=F32)


def _split3(x):
    x1 = x.astype(MXU)
    r1 = x - x1.astype(F32)
    x2 = r1.astype(MXU)
    r2 = r1 - x2.astype(F32)
    return x1, x2, r2.astype(MXU)


def _mm_exact_l(c, x):
    x1, x2, x3 = _split3(x)
    return _mm(c, x1) + _mm(c, x2) + _mm(c, x3)


def _mm_exact_r(x, c):
    x1, x2, x3 = _split3(x)
    return _mm(x1, c) + _mm(x2, c) + _mm(x3, c)


def _sigmoid(x):
    return 1.0 / (1.0 + jnp.exp(-x))


def _gelu(x):
    c, k = 0.7978845608028654, 0.044715
    th = jnp.tanh(c * (x + k * x * x * x))
    return 0.5 * x * (1.0 + th)


def _gelu_and_grad(x):
    c, k = 0.7978845608028654, 0.044715
    th = jnp.tanh(c * (x + k * x * x * x))
    g = 0.5 * x * (1.0 + th)
    dg = 0.5 * (1.0 + th) + 0.5 * x * (1.0 - th * th) * c * (1.0 + 3.0 * k * x * x)
    return g, dg


def _expm1(x):
    series = x * (1.0 + x * (0.5 + x * (1.0 / 6.0 + x * (1.0 / 24.0 + x * (1.0 / 120.0)))))
    return jnp.where(jnp.abs(x) < 0.05, series, jnp.exp(x) - 1.0)


def _iota(shape, dim):
    return lax.broadcasted_iota(jnp.int32, shape, dim)


def _group_matrix(n, value):
    r, c = _iota((n, n), 0), _iota((n, n), 1)
    return jnp.where((r // GRP) == (c // GRP), value, 0.0).astype(F32)


def _row(x, k):
    r = _iota(x.shape, 0)
    return jnp.sum(jnp.where(r == k, x, 0.0), axis=0, keepdims=True)


def _rms_bwd(dxn, hh, gain):
    rstd = lax.rsqrt(jnp.mean(hh * hh, axis=-1, keepdims=True) + EPS)
    xhat = hh * rstd
    dxh = dxn * gain
    dh = rstd * (dxh - xhat * jnp.mean(dxh * xhat, axis=-1, keepdims=True))
    return dh, jnp.sum(dxn * xhat, axis=0, keepdims=True)


def _params(sem):
    return pltpu.CompilerParams(dimension_semantics=sem, vmem_limit_bytes=VMEM_LIMIT)


def _all_gather(arrs, name):
    n = len(arrs)

    def body(*refs):
        ins, outs = refs[:n], refs[n:2 * n]
        send_sems, recv_sems, local_sems = refs[2 * n:]
        x, y, c = lax.axis_index("x"), lax.axis_index("y"), lax.axis_index("c")
        me, sibling = (x, y, c), (x, y, 1 - c)
        chips = [(1 - x, y), (x, 1 - y), (1 - x, 1 - y)]

        def slot(px, py, pc):
            return 4 * px + 2 * py + pc

        def copy(a, k, block, to, src=None):
            dst = outs[a].at[slot(*block)]
            return pltpu.make_async_remote_copy(
                src_ref=dst if src is None else src, dst_ref=dst,
                send_sem=send_sems.at[a * 7 + k], recv_sem=recv_sems.at[a * 7 + k],
                device_id=to, device_id_type=MESH)

        started = []
        for a in range(n):
            mine = pltpu.make_async_copy(ins[a], outs[a].at[slot(*me)], local_sems.at[a])
            mine.start()
            started.append(mine)
        first = []
        for a in range(n):
            first.append(copy(a, 0, me, sibling, src=ins[a]))
            first += [copy(a, 1 + j, me, (*chip, c), src=ins[a]) for j, chip in enumerate(chips)]
        for cp in first:
            cp.start()
        passed = []
        for a in range(n):
            for j, chip in enumerate(chips):
                copy(a, 1 + j, (*chip, c), me).wait_recv()
                fwd = copy(a, 4 + j, (*chip, c), sibling)
                fwd.start()
                passed.append(fwd)
        for a in range(n):
            copy(a, 0, sibling, me).wait_recv()
            for j, chip in enumerate(chips):
                copy(a, 4 + j, (*chip, 1 - c), me).wait_recv()
        for cp in first + passed:
            cp.wait_send()
        for mine in started:
            mine.wait()

    hbm = pl.BlockSpec(memory_space=pl.ANY)
    return pl.pallas_call(
        body, name=name,
        out_shape=[jax.ShapeDtypeStruct((NDEV,) + a.shape, a.dtype) for a in arrs],
        in_specs=[hbm] * n, out_specs=[hbm] * n,
        scratch_shapes=[pltpu.SemaphoreType.DMA((7 * n,)), pltpu.SemaphoreType.DMA((7 * n,)),
                        pltpu.SemaphoreType.DMA((n,))],
    )(*arrs)


def _peers():
    x, y, c = lax.axis_index("x"), lax.axis_index("y"), lax.axis_index("c")
    peers = [(x ^ ((k >> 2) & 1), y ^ ((k >> 1) & 1), c ^ (k & 1)) for k in range(1, NDEV)]
    return (x, y, c), 4 * x + 2 * y + c, peers


_HBM = pl.BlockSpec(memory_space=pltpu.HBM)
_SEM = pl.BlockSpec(memory_space=pltpu.SEMAPHORE)
_EFFECT = pltpu.SideEffectType.DATAFLOW_SIDE_EFFECTING


def _transfer_start(arrs, gather, name, deps=(), direct=False):
    n, nd = len(arrs), len(deps)
    shapes = [((NDEV,) + a.shape) if gather else a.shape for a in arrs]

    def body(*refs):
        ins, lands = refs[:n], refs[n:2 * n]
        send_sems, recv_sems, local_sems = refs[2 * n + nd:2 * n + nd + 3]
        token = refs[-1]
        (x, y, c), my, peers = _peers()
        if gather and not direct:
            peers = [(x, y, 1 - c), (1 - x, y, c), (x, 1 - y, c), (1 - x, 1 - y, c)]
        for a in range(n):
            own = ins[a] if gather else ins[a].at[my]
            pltpu.make_async_copy(own, lands[a].at[my], local_sems.at[a]).start()
        for a in range(n):
            for peer in peers:
                src = ins[a] if gather else ins[a].at[4 * peer[0] + 2 * peer[1] + peer[2]]
                pltpu.make_async_remote_copy(
                    src_ref=src, dst_ref=lands[a].at[my], send_sem=send_sems.at[a], recv_sem=recv_sems.at[a],
                    device_id=peer, device_id_type=MESH).start()
        token[...] = jnp.zeros_like(token)

    out_shape = [pltpu.SemaphoreType.DMA((n,))] * 3
    out_shape += [pltpu.HBM(a.shape, a.dtype) for a in arrs]
    out_shape += [pltpu.HBM(s, a.dtype) for s, a in zip(shapes, arrs)]
    out_shape += [jax.ShapeDtypeStruct((8, 128), F32)]
    operands = [pltpu.with_memory_space_constraint(a, pltpu.HBM) for a in arrs]
    operands += [pltpu.with_memory_space_constraint(lax.empty(s, a.dtype), pltpu.HBM) for s, a in zip(shapes, arrs)]
    res = pl.pallas_call(
        body, name=name, out_shape=out_shape,
        in_specs=[_HBM] * (2 * n) + [pl.BlockSpec(memory_space=pl.ANY)] * nd,
        out_specs=[_SEM] * 3 + [_HBM] * (2 * n) + [pl.BlockSpec(memory_space=pltpu.VMEM)],
        input_output_aliases={i: 3 + i for i in range(2 * n)},
        compiler_params=pltpu.CompilerParams(has_side_effects=_EFFECT),
    )(*operands, *deps)
    return dict(sems=res[:3], src=res[3:3 + n], lands=res[3 + n:3 + 2 * n], token=res[-1], n=n,
                count=4 if gather and not direct else NDEV - 1)


def _forward_start(lands, name, deps=()):
    n, nd = len(lands), len(deps)

    def body(*refs):
        zone = refs[:n]
        send_sems, recv_sems = refs[n + nd:n + nd + 2]
        token = refs[-1]
        (x, y, c), _, _ = _peers()
        for a in range(n):
            for px, py in ((1 - x, y), (x, 1 - y), (1 - x, 1 - y)):
                block = zone[a].at[4 * px + 2 * py + c]
                pltpu.make_async_remote_copy(
                    src_ref=block, dst_ref=block, send_sem=send_sems.at[a], recv_sem=recv_sems.at[a],
                    device_id=(x, y, 1 - c), device_id_type=MESH).start()
        token[...] = jnp.zeros_like(token)

    res = pl.pallas_call(
        body, name=name,
        out_shape=[pltpu.SemaphoreType.DMA((n,))] * 2 + [pltpu.HBM(a.shape, a.dtype) for a in lands]
        + [jax.ShapeDtypeStruct((8, 128), F32)],
        in_specs=[_HBM] * n + [pl.BlockSpec(memory_space=pl.ANY)] * nd,
        out_specs=[_SEM] * 2 + [_HBM] * n + [pl.BlockSpec(memory_space=pltpu.VMEM)],
        input_output_aliases={i: 2 + i for i in range(n)},
        compiler_params=pltpu.CompilerParams(has_side_effects=_EFFECT),
    )(*lands, *deps)
    return dict(sems=res[:2], src=[], lands=res[2:2 + n], token=res[-1], n=n, count=3)


def _transfer_wait(handle, after, name):
    n, count = handle["n"], handle["count"]
    src, lands, sems = list(handle["src"]), list(handle["lands"]), list(handle["sems"])
    ns = len(src)

    def body(*refs):
        zone = refs[ns:ns + n]
        sem_refs = refs[ns + n:ns + n + len(sems)]
        me, _, _ = _peers()
        for a in range(n):
            moved = zone[a].at[pl.ds(0, count)]
            both = pltpu.make_async_remote_copy(
                src_ref=moved, dst_ref=moved, send_sem=sem_refs[0].at[a], recv_sem=sem_refs[1].at[a],
                device_id=me, device_id_type=MESH)
            both.wait_send()
            both.wait_recv()
            if len(sems) == 3:
                pltpu.make_async_copy(zone[a].at[0], zone[a].at[1], sem_refs[2].at[a]).wait()

    res = pl.pallas_call(
        body, name=name,
        out_shape=[pltpu.HBM(a.shape, a.dtype) for a in src + lands],
        in_specs=[_HBM] * (ns + n) + [_SEM] * len(sems) + [pl.BlockSpec(memory_space=pl.ANY)],
        out_specs=[_HBM] * (ns + n),
        input_output_aliases={i: i for i in range(ns + n)},
        compiler_params=pltpu.CompilerParams(has_side_effects=_EFFECT),
    )(*src, *lands, *sems, after)
    return list(res[ns:])


def _ffn_fwd(h, gain, wg, wu, wd, tm):
    t = h.shape[0]
    nj = FF // FBX

    def body(h_ref, g_ref, wg_ref, wu_ref, wd_ref, out_ref, xn_ref, a_ref, b_ref, acc_ref):
        j = pl.program_id(1)

        @pl.when(j == 0)
        def _():
            hh = h_ref[...]
            rstd = lax.rsqrt(jnp.mean(hh * hh, axis=-1, keepdims=True) + EPS)
            xn_ref[...] = (hh * rstd * g_ref[...]).astype(xn_ref.dtype)
            acc_ref[...] = jnp.zeros_like(acc_ref)

        sub = min(SUB, tm)
        for r in range(tm // sub):
            rows = slice(r * sub, (r + 1) * sub)
            xn = xn_ref[rows, :]
            y = None
            for c0 in range(0, FBX, FB):
                cols = slice(c0, min(c0 + FB, FBX))
                a = _mm_nt(xn, wg_ref[cols, :])
                b = _mm_nt(xn, wu_ref[cols, :])
                a_ref[rows, cols] = a.astype(a_ref.dtype)
                b_ref[rows, cols] = b.astype(b_ref.dtype)
                part = _mm(a * _sigmoid(a) * b, wd_ref[cols, :])
                y = part if y is None else y + part
            acc_ref[rows, :] += y

        @pl.when(j == nj - 1)
        def _():
            out_ref[...] = h_ref[...] + 0.5 * acc_ref[...]

    wspec = pl.BlockSpec((FBX, D), lambda i, j: (j, 0))
    return pl.pallas_call(
        body, name="ffn_fwd", grid=(t // tm, nj),
        in_specs=[pl.BlockSpec((tm, D), lambda i, j: (i, 0)),
                  pl.BlockSpec((1, D), lambda i, j: (0, 0)), wspec, wspec, wspec],
        out_specs=[pl.BlockSpec((tm, D), lambda i, j: (i, 0)),
                   pl.BlockSpec((tm, D), lambda i, j: (i, 0)),
                   pl.BlockSpec((tm, FBX), lambda i, j: (i, j)),
                   pl.BlockSpec((tm, FBX), lambda i, j: (i, j))],
        out_shape=[jax.ShapeDtypeStruct((t, D), F32), jax.ShapeDtypeStruct((t, D), SAVE),
                   jax.ShapeDtypeStruct((t, FF), SAVE), jax.ShapeDtypeStruct((t, FF), SAVE)],
        scratch_shapes=[pltpu.VMEM((tm, D), F32)],
        compiler_params=_params(("parallel", "arbitrary")),
    )(h, gain, wg, wu, wd)


def _ffn_bwd_x(dout, h, gain, a_sv, b_sv, wg, wu, wd, tm):
    t = h.shape[0]
    nj = FF // FBX

    def body(dout_ref, h_ref, g_ref, a_ref, b_ref, wg_ref, wu_ref, wd_ref,
             dh_ref, dgain_ref, dy_ref, da_ref, db_ref, s_ref, acc_ref):
        i, j = pl.program_id(0), pl.program_id(1)

        @pl.when((i == 0) & (j == 0))
        def _():
            dgain_ref[...] = jnp.zeros_like(dgain_ref)

        @pl.when(j == 0)
        def _():
            dy_ref[...] = (0.5 * dout_ref[...]).astype(dy_ref.dtype)
            acc_ref[...] = jnp.zeros_like(acc_ref)

        sub = min(SUB, tm)
        for r in range(tm // sub):
            rows = slice(r * sub, (r + 1) * sub)
            dy = dy_ref[rows, :]
            dx = None
            for c0 in range(0, FBX, FB):
                cols = slice(c0, min(c0 + FB, FBX))
                ds = _mm_nt(dy, wd_ref[cols, :])
                a, b = a_ref[rows, cols].astype(F32), b_ref[rows, cols].astype(F32)
                sg = _sigmoid(a)
                sa = a * sg
                da = (ds * b * (sg * (1.0 + a * (1.0 - sg)))).astype(MXU)
                db = (ds * sa).astype(MXU)
                da_ref[rows, cols] = da.astype(da_ref.dtype)
                db_ref[rows, cols] = db.astype(db_ref.dtype)
                s_ref[rows, cols] = (sa * b).astype(s_ref.dtype)
                part = _mm(da, wg_ref[cols, :]) + _mm(db, wu_ref[cols, :])
                dx = part if dx is None else dx + part
            acc_ref[rows, :] += dx

        @pl.when(j == nj - 1)
        def _():
            dh, dg = _rms_bwd(acc_ref[...], h_ref[...], g_ref[...])
            dh_ref[...] = dout_ref[...] + dh
            dgain_ref[...] += dg

    tok = pl.BlockSpec((tm, D), lambda i, j: (i, 0))
    act = pl.BlockSpec((tm, FBX), lambda i, j: (i, j))
    wspec = pl.BlockSpec((FBX, D), lambda i, j: (j, 0))
    return pl.pallas_call(
        body, name="ffn_bwd_x", grid=(t // tm, nj),
        in_specs=[tok, tok, pl.BlockSpec((1, D), lambda i, j: (0, 0)), act, act, wspec, wspec, wspec],
        out_specs=[tok, pl.BlockSpec((1, D), lambda i, j: (0, 0)), tok, act, act, act],
        out_shape=[jax.ShapeDtypeStruct((t, D), F32), jax.ShapeDtypeStruct((1, D), F32),
                   jax.ShapeDtypeStruct((t, D), SAVE)] + [jax.ShapeDtypeStruct((t, FF), SAVE)] * 3,
        scratch_shapes=[pltpu.VMEM((tm, D), F32)],
        compiler_params=_params(("arbitrary", "arbitrary")),
    )(dout, h, gain, a_sv, b_sv, wg, wu, wd)


def _ffn_bwd_w(xn, dy, da, db, s, tm):
    t = xn.shape[0]
    nt = t // tm
    nj = FF // FB

    def body(xn_ref, dy_ref, da_ref, db_ref, s_ref, dwg_ref, dwu_ref, dwd_ref, ag_scr, au_scr, ad_scr):
        i, j = pl.program_id(0), pl.program_id(1)
        rows = pl.ds(pl.multiple_of(j * FB, FB), FB)
        xn = xn_ref[...]
        new = ((ag_scr, _mm_tn(da_ref[...], xn)), (au_scr, _mm_tn(db_ref[...], xn)),
               (ad_scr, _mm_tn(s_ref[...], dy_ref[...])))

        @pl.when(i == 0)
        def _():
            for ref, val in new:
                ref[rows, :] = val

        @pl.when(i > 0)
        def _():
            for ref, val in new:
                ref[rows, :] += val

        @pl.when(i == nt - 1)
        def _():
            for out, ref in ((dwg_ref, ag_scr), (dwu_ref, au_scr), (dwd_ref, ad_scr)):
                out[...] = ref[rows, :].astype(out.dtype)

    tok = pl.BlockSpec((tm, D), lambda i, j: (i, 0))
    act = pl.BlockSpec((tm, FB), lambda i, j: (i, j))
    wspec = pl.BlockSpec((FB, D), lambda i, j: (jnp.where(i == nt - 1, j, 0), 0))
    return pl.pallas_call(
        body, name="ffn_bwd_w", grid=(nt, nj),
        in_specs=[tok, tok, act, act, act], out_specs=[wspec] * 3,
        out_shape=[jax.ShapeDtypeStruct((FF, D), WIRE)] * 3,
        scratch_shapes=[pltpu.VMEM((FF, D), F32)] * 3,
        compiler_params=_params(("arbitrary", "arbitrary")),
    )(xn, dy, da, db, s)


def _inproj_fwd(h, gain, win, tm):
    t = h.shape[0]

    def body(h_ref, g_ref, w_ref, z_ref, xn_ref):
        hh = h_ref[...]
        rstd = lax.rsqrt(jnp.mean(hh * hh, axis=-1, keepdims=True) + EPS)
        xn = (hh * rstd * g_ref[...]).astype(MXU)
        xn_ref[...] = xn.astype(xn_ref.dtype)
        for j in range(DIN // ZB):
            z_ref[:, j * ZB:(j + 1) * ZB] = _mm_nt(xn, w_ref[j * ZB:(j + 1) * ZB, :])

    return pl.pallas_call(
        body, name="inproj_fwd", grid=(t // tm,),
        in_specs=[pl.BlockSpec((tm, D), lambda i: (i, 0)),
                  pl.BlockSpec((1, D), lambda i: (0, 0)),
                  pl.BlockSpec((DIN, D), lambda i: (0, 0))],
        out_specs=[pl.BlockSpec((tm, DIN), lambda i: (i, 0)),
                   pl.BlockSpec((tm, D), lambda i: (i, 0))],
        out_shape=[jax.ShapeDtypeStruct((t, DIN), F32), jax.ShapeDtypeStruct((t, D), SAVE)],
        compiler_params=_params(("parallel",)),
    )(h, gain, win)


def _inproj_bwd_x(dres, dz, h, gain, win, tm):
    t = h.shape[0]

    def body(dres_ref, dz_ref, h_ref, g_ref, w_ref, dh_ref, dgain_ref):
        @pl.when(pl.program_id(0) == 0)
        def _():
            dgain_ref[...] = jnp.zeros_like(dgain_ref)

        dh, dg = _rms_bwd(_mm(dz_ref[...], w_ref[...]), h_ref[...], g_ref[...])
        dh_ref[...] = dres_ref[...] + dh
        dgain_ref[...] += dg

    return pl.pallas_call(
        body, name="inproj_bwd_x", grid=(t // tm,),
        in_specs=[pl.BlockSpec((tm, D), lambda i: (i, 0)),
                  pl.BlockSpec((tm, DIN), lambda i: (i, 0)),
                  pl.BlockSpec((tm, D), lambda i: (i, 0)),
                  pl.BlockSpec((1, D), lambda i: (0, 0)),
                  pl.BlockSpec((DIN, D), lambda i: (0, 0))],
        out_specs=[pl.BlockSpec((tm, D), lambda i: (i, 0)),
                   pl.BlockSpec((1, D), lambda i: (0, 0))],
        out_shape=[jax.ShapeDtypeStruct((t, D), F32), jax.ShapeDtypeStruct((1, D), F32)],
        compiler_params=_params(("arbitrary",)),
    )(dres, dz, h, gain, win)


def _inproj_bwd_w(xn, dz, tm):
    t = xn.shape[0]
    nt = t // tm

    def body(xn_ref, dz_ref, dw_ref, acc_scr):
        i = pl.program_id(1)

        @pl.when(i == 0)
        def _():
            acc_scr[...] = jnp.zeros_like(acc_scr)

        acc_scr[...] += _mm_tn(dz_ref[...], xn_ref[...])

        @pl.when(i == nt - 1)
        def _():
            dw_ref[...] = acc_scr[...].astype(dw_ref.dtype)

    return pl.pallas_call(
        body, name="inproj_bwd_w", grid=(DIN // ZB, nt),
        in_specs=[pl.BlockSpec((tm, D), lambda j, i: (i, 0)),
                  pl.BlockSpec((tm, ZB), lambda j, i: (i, j))],
        out_specs=pl.BlockSpec((ZB, D), lambda j, i: (j, 0)),
        out_shape=jax.ShapeDtypeStruct((DIN, D), WIRE),
        scratch_shapes=[pltpu.VMEM((ZB, D), F32)],
        compiler_params=_params(("parallel", "arbitrary")),
    )(xn, dz)


def _outproj_fwd(h, oa, ob, oc, wout, tm):
    t = h.shape[0]

    def body(h_ref, oa_ref, ob_ref, oc_ref, w_ref, out_ref):
        ym = jnp.concatenate([oa_ref[...], ob_ref[...], oc_ref[...]], axis=1)
        out_ref[...] = h_ref[...] + _mm(ym, w_ref[...])

    return pl.pallas_call(
        body, name="outproj_fwd", grid=(t // tm,),
        in_specs=[pl.BlockSpec((tm, D), lambda i: (i, 0)),
                  pl.BlockSpec((tm, DA), lambda i: (i, 0)),
                  pl.BlockSpec((tm, DB), lambda i: (i, 0)),
                  pl.BlockSpec((tm, DC), lambda i: (i, 0)),
                  pl.BlockSpec((D, D), lambda i: (0, 0))],
        out_specs=pl.BlockSpec((tm, D), lambda i: (i, 0)),
        out_shape=jax.ShapeDtypeStruct((t, D), F32),
        compiler_params=_params(("parallel",)),
    )(h, oa, ob, oc, wout)


def _outproj_bwd(dh, oa, ob, oc, wout, tm):
    t = dh.shape[0]
    nt = t // tm

    def body(dh_ref, oa_ref, ob_ref, oc_ref, w_ref, da_ref, db_ref, dc_ref, dw_ref, acc_scr):
        i = pl.program_id(0)

        @pl.when(i == 0)
        def _():
            acc_scr[...] = jnp.zeros_like(acc_scr)

        d16 = dh_ref[...].astype(MXU)
        dym = _mm_nt(d16, w_ref[...])
        da_ref[...] = dym[:, :DA]
        db_ref[...] = dym[:, DA:DA + DB]
        dc_ref[...] = dym[:, DA + DB:]
        ym = jnp.concatenate([oa_ref[...], ob_ref[...], oc_ref[...]], axis=1)
        acc_scr[...] += _mm_tn(ym, d16)

        @pl.when(i == nt - 1)
        def _():
            dw_ref[...] = acc_scr[...].astype(dw_ref.dtype)

    return pl.pallas_call(
        body, name="outproj_bwd", grid=(nt,),
        in_specs=[pl.BlockSpec((tm, D), lambda i: (i, 0)),
                  pl.BlockSpec((tm, DA), lambda i: (i, 0)),
                  pl.BlockSpec((tm, DB), lambda i: (i, 0)),
                  pl.BlockSpec((tm, DC), lambda i: (i, 0)),
                  pl.BlockSpec((D, D), lambda i: (0, 0))],
        out_specs=[pl.BlockSpec((tm, DA), lambda i: (i, 0)),
                   pl.BlockSpec((tm, DB), lambda i: (i, 0)),
                   pl.BlockSpec((tm, DC), lambda i: (i, 0)),
                   pl.BlockSpec((D, D), lambda i: (0, 0))],
        out_shape=[jax.ShapeDtypeStruct((t, DA), F32), jax.ShapeDtypeStruct((t, DB), F32),
                   jax.ShapeDtypeStruct((t, DC), F32), jax.ShapeDtypeStruct((D, D), WIRE)],
        scratch_shapes=[pltpu.VMEM((D, D), F32)],
        compiler_params=_params(("arbitrary",)),
    )(dh, oa, ob, oc, wout)


def _lower_bounds(logits):
    depth, n = logits.shape

    def body(l_ref, lb_ref, p_ref):
        rows = [l_ref[l:l + 1, :] for l in range(depth)]
        mx = functools.reduce(jnp.maximum, rows)
        ex = [jnp.exp(r - mx) for r in rows]
        den = functools.reduce(lambda u, v: u + v, ex)
        acc = jnp.zeros_like(den)
        for l in range(depth):
            p = ex[l] / den
            p_ref[l:l + 1, :] = p
            if l > 0:
                acc = acc + p
            lb_ref[l:l + 1, :] = acc

    return pl.pallas_call(
        body, name="lower_bounds",
        out_shape=[jax.ShapeDtypeStruct((depth, n), F32), jax.ShapeDtypeStruct((depth, n), F32)],
    )(logits)


def _lower_bounds_bwd(p, dlb):
    depth, n = p.shape

    def body(p_ref, d_ref, out_ref):
        ps = [p_ref[l:l + 1, :] for l in range(depth)]
        ds = [d_ref[l:l + 1, :] for l in range(depth)]
        dp = [jnp.zeros_like(ps[0]) for _ in range(depth)]
        run = jnp.zeros_like(ps[0])
        for l in range(depth - 1, 0, -1):
            run = run + ds[l]
            dp[l] = run
        dot = functools.reduce(lambda u, v: u + v, [ps[l] * dp[l] for l in range(depth)])
        for l in range(depth):
            out_ref[l:l + 1, :] = ps[l] * (dp[l] - dot)

    return pl.pallas_call(body, name="lower_bounds_bwd", out_shape=jax.ShapeDtypeStruct((depth, n), F32))(p, dlb)


def _hgrn_block(z_ref, lb_ref, rb):
    q, fl = z_ref[:, 0:DA], z_ref[:, DA:2 * DA]
    lb = lb_ref[...]
    sq = _sigmoid(q)
    qs = q * sq
    sg = _sigmoid(fl)
    f = lb + (1.0 - lb) * sg
    k = 1.0 - f
    lf = jnp.log(f)
    row, col = _iota((rb, rb), 0), _iota((rb, rb), 1)
    same = (row // ACH) == (col // ACH)
    causal = same & (row >= col)
    b = _mm_exact_l(jnp.where(causal, 1.0, 0.0).astype(MXU), lf)
    bend = _mm_exact_l(jnp.where(same, 1.0, 0.0).astype(MXU), lf)
    r = 0.5 * bend
    eq, ek, eb, ed = jnp.exp(b - r), jnp.exp(r - b), jnp.exp(b), jnp.exp(bend - b)
    return dict(q=q, lb=lb, sq=sq, sg=sg, f=f, bend=bend, eq=eq, ek=ek, eb=eb, ed=ed,
                qt=qs * eq, kt=k * ek, qe=qs * eb, kd=k * ed, same=same, causal=causal)


def _hgrn_fwd(z, lb, gain):
    t = z.shape[0]
    nc = t // ACH
    cb = min(ACB, nc)
    rb = cb * ACH

    def body(z_ref, lb_ref, g_ref, o_ref, oa_ref, st_ref, st_scr):
        @pl.when(pl.program_id(0) == 0)
        def _():
            st_scr[...] = jnp.zeros_like(st_scr)

        c = _hgrn_block(z_ref, lb_ref, rb)
        for hd in range(NH):
            cols = slice(hd * HD, (hd + 1) * HD)
            v = z_ref[:, 2 * DA + hd * HD:2 * DA + (hd + 1) * HD]
            gg = z_ref[:, 3 * DA + hd * HD:3 * DA + (hd + 1) * HD]
            att = jnp.where(c["causal"], _mm_nt(c["qt"][:, cols], c["kt"][:, cols]), 0.0)
            o_in = _mm(att, v)
            qe, kd, bend = c["qe"][:, cols], c["kd"][:, cols], c["bend"][:, cols]
            st = st_scr[hd]
            outs = []
            for cc in range(cb):
                rows = slice(cc * ACH, (cc + 1) * ACH)
                st_ref[cc, hd] = st
                outs.append(o_in[rows] + _mm_nt(qe[rows], st))
                decay = jnp.exp(jnp.max(bend[rows], axis=0, keepdims=True))
                st = st * decay + _mm_tn(v[rows], kd[rows])
            st_scr[hd] = st
            o = jnp.concatenate(outs, axis=0)
            o_ref[:, cols] = o
            rstd = lax.rsqrt(jnp.mean(o * o, axis=-1, keepdims=True) + EPS)
            oa_ref[:, cols] = (o * rstd * g_ref[:, cols] * (gg * _sigmoid(gg))).astype(oa_ref.dtype)

    return pl.pallas_call(
        body, name="hgrn_fwd", grid=(nc // cb,),
        in_specs=[pl.BlockSpec((rb, 4 * DA), lambda c: (c, 0)),
                  pl.BlockSpec((1, DA), lambda c: (0, 0)),
                  pl.BlockSpec((1, DA), lambda c: (0, 0))],
        out_specs=[pl.BlockSpec((rb, DA), lambda c: (c, 0)),
                   pl.BlockSpec((rb, DA), lambda c: (c, 0)),
                   pl.BlockSpec((cb, NH, HD, HD), lambda c: (c, 0, 0, 0))],
        out_shape=[jax.ShapeDtypeStruct((t, DA), F32), jax.ShapeDtypeStruct((t, DA), SAVE),
                   jax.ShapeDtypeStruct((nc, NH, HD, HD), F32)],
        scratch_shapes=[pltpu.VMEM((NH, HD, HD), F32)],
        compiler_params=_params(("arbitrary",)),
    )(z, lb, gain)


def _hgrn_bwd(z, lb, gain, o, states, doa):
    t = z.shape[0]
    nc = t // ACH
    cb = min(ACB, nc)
    rb = cb * ACH
    nblk = nc // cb

    def body(z_ref, lb_ref, g_ref, o_ref, st_ref, doa_ref, dz_ref, dgain_ref, dlb_ref, dst_scr):
        @pl.when(pl.program_id(0) == 0)
        def _():
            dst_scr[...] = jnp.zeros_like(dst_scr)
            dgain_ref[...] = jnp.zeros_like(dgain_ref)
            dlb_ref[...] = jnp.zeros_like(dlb_ref)

        c = _hgrn_block(z_ref, lb_ref, rb)
        dbs, dqss, dks = [], [], []
        for hd in range(NH):
            cols = slice(hd * HD, (hd + 1) * HD)
            v = z_ref[:, 2 * DA + hd * HD:2 * DA + (hd + 1) * HD]
            gg = z_ref[:, 3 * DA + hd * HD:3 * DA + (hd + 1) * HD]
            qt, kt, qe, kd, bend = (c[n][:, cols] for n in ("qt", "kt", "qe", "kd", "bend"))
            o = o_ref[:, cols]
            do_a = doa_ref[:, cols]
            gain = g_ref[:, cols]
            sgg = _sigmoid(gg)
            silu_g = gg * sgg
            rstd = lax.rsqrt(jnp.mean(o * o, axis=-1, keepdims=True) + EPS)
            n = o * rstd
            dn = do_a * gain * silu_g
            dg = do_a * n * gain * (sgg * (1.0 + gg * (1.0 - sgg)))
            dgain_ref[:, cols] += jnp.sum(do_a * silu_g * n, axis=0, keepdims=True)
            d_o = rstd * (dn - n * jnp.mean(dn * n, axis=-1, keepdims=True))

            att = jnp.where(c["causal"], _mm_nt(qt, kt), 0.0)
            datt = jnp.where(c["causal"], _mm_nt(d_o, v), 0.0)
            dv_in = _mm_tn(att, d_o)
            dqt = _mm(datt, kt)
            dkt = _mm_tn(datt, qt)
            dsp = dst_scr[hd]
            dvs, dqes, dkds, dbends = [None] * cb, [None] * cb, [None] * cb, [None] * cb
            for cc in reversed(range(cb)):
                rows = slice(cc * ACH, (cc + 1) * ACH)
                st = st_ref[cc, hd]
                dvs[cc] = dv_in[rows] + _mm_nt(kd[rows], dsp)
                dqes[cc] = _mm(d_o[rows], st)
                dkds[cc] = _mm(v[rows], dsp)
                decay = jnp.exp(jnp.max(bend[rows], axis=0, keepdims=True))
                dbend = (decay * jnp.sum(st * dsp, axis=0, keepdims=True)
                         + jnp.sum(dkds[cc] * kd[rows], axis=0, keepdims=True))
                dbends[cc] = jnp.broadcast_to(dbend, (ACH, HD))
                dsp = dsp * decay + _mm_tn(d_o[rows], qe[rows])
            dst_scr[hd] = dsp
            dv, dqe, dkd, dbend = (jnp.concatenate(p, axis=0) for p in (dvs, dqes, dkds, dbends))
            dbs.append((dqt * qt + dqe * qe - dkt * kt - dkd * kd, dbend))
            dqss.append(dqt * c["eq"][:, cols] + dqe * c["eb"][:, cols])
            dks.append(dkt * c["ek"][:, cols] + dkd * c["ed"][:, cols])
            c0 = hd * HD
            dz_ref[:, 2 * DA + c0:2 * DA + c0 + HD] = dv.astype(dz_ref.dtype)
            dz_ref[:, 3 * DA + c0:3 * DA + c0 + HD] = dg.astype(dz_ref.dtype)

        db = jnp.concatenate([p[0] for p in dbs], axis=1)
        dbend = jnp.concatenate([p[1] for p in dbs], axis=1)
        dqs, dk = jnp.concatenate(dqss, axis=1), jnp.concatenate(dks, axis=1)
        row, col = _iota((rb, rb), 0), _iota((rb, rb), 1)
        upper = jnp.where(c["same"] & (row <= col), 1.0, 0.0).astype(MXU)
        dlf = _mm_exact_l(upper, db) + dbend
        df = dlf / c["f"] - dk
        sg, sq, q = c["sg"], c["sq"], c["q"]
        dlb_ref[...] += jnp.sum(df * (1.0 - sg), axis=0, keepdims=True)
        dz_ref[:, DA:2 * DA] = (df * (1.0 - c["lb"]) * sg * (1.0 - sg)).astype(dz_ref.dtype)
        dz_ref[:, 0:DA] = (dqs * (sq * (1.0 + q * (1.0 - sq)))).astype(dz_ref.dtype)

    rev = lambda c: (nblk - 1 - c, 0)
    return pl.pallas_call(
        body, name="hgrn_bwd", grid=(nblk,),
        in_specs=[pl.BlockSpec((rb, 4 * DA), rev),
                  pl.BlockSpec((1, DA), lambda c: (0, 0)),
                  pl.BlockSpec((1, DA), lambda c: (0, 0)),
                  pl.BlockSpec((rb, DA), rev),
                  pl.BlockSpec((cb, NH, HD, HD), lambda c: (nblk - 1 - c, 0, 0, 0)),
                  pl.BlockSpec((rb, DA), rev)],
        out_specs=[pl.BlockSpec((rb, 4 * DA), rev),
                   pl.BlockSpec((1, DA), lambda c: (0, 0)),
                   pl.BlockSpec((1, DA), lambda c: (0, 0))],
        out_shape=[jax.ShapeDtypeStruct((t, DIN), SAVE), jax.ShapeDtypeStruct((1, DA), F32),
                   jax.ShapeDtypeStruct((1, DA), F32)],
        scratch_shapes=[pltpu.VMEM((NH, HD, HD), F32)],
        compiler_params=_params(("arbitrary",)),
    )(z, lb, gain, o, states, doa)


def _shift_down(prev8, x, k):
    cat = jnp.concatenate([prev8, x], axis=0)
    return pltpu.roll(cat, k, axis=0)[8:, :]


def _shift_up(x, next8, k):
    n = x.shape[0]
    cat = jnp.concatenate([x, next8], axis=0)
    return pltpu.roll(cat, n + 8 - k, axis=0)[:n, :]


def _lru_gates(x, prev8, cw_ref, vec_ref, wa_ref, wx_ref):
    xs = [x, _shift_down(prev8, x, 1), _shift_down(prev8, x, 2), _shift_down(prev8, x, 3)]
    xc = vec_ref[0:1, :] + cw_ref[3:4, :] * xs[0] + cw_ref[2:3, :] * xs[1] + cw_ref[1:2, :] * xs[2] + cw_ref[0:1, :] * xs[3]
    r = _sigmoid(_mm(xc, wa_ref[...]) + vec_ref[1:2, :])
    gi = _sigmoid(_mm(xc, wx_ref[...]) + vec_ref[2:3, :])
    lam = vec_ref[3:4, :]
    sp = jnp.maximum(-lam, 0.0) + jnp.log(1.0 + jnp.exp(-jnp.abs(lam)))
    la = -LRU_C * r * sp
    a = jnp.exp(la)
    mult = jnp.sqrt(-_expm1(2.0 * la))
    return xs, xc, r, gi, sp, a, mult


def _scan_down(a, u):
    n = a.shape[0]
    row = _iota(a.shape, 0)
    s = 1
    while s < n:
        keep = row >= s
        ash = jnp.where(keep, pltpu.roll(a, s, axis=0), 1.0)
        ush = jnp.where(keep, pltpu.roll(u, s, axis=0), 0.0)
        u = a * ush + u
        a = a * ash
        s *= 2
    return a, u


def _scan_up(a, u):
    n = a.shape[0]
    row = _iota(a.shape, 0)
    s = 1
    while s < n:
        keep = row < n - s
        ash = jnp.where(keep, pltpu.roll(a, n - s, axis=0), 1.0)
        ush = jnp.where(keep, pltpu.roll(u, n - s, axis=0), 0.0)
        u = a * ush + u
        a = a * ash
        s *= 2
    return a, u


def _lru_fwd(z, cw, vec, wa, wx, tb):
    t = z.shape[0]
    xcol, gcol = (4 * DA) // DB, (4 * DA) // DB + 1

    def body(x_ref, gate_ref, cw_ref, vec_ref, wa_ref, wx_ref, ob_ref, h_ref, xprev_scr, hc_scr):
        @pl.when(pl.program_id(0) == 0)
        def _():
            xprev_scr[...] = jnp.zeros_like(xprev_scr)
            hc_scr[...] = jnp.zeros_like(hc_scr)

        x = x_ref[...]
        _, xc, _, gi, _, a, mult = _lru_gates(x, xprev_scr[...], cw_ref, vec_ref, wa_ref, wx_ref)
        acum, hloc = _scan_down(a, mult * gi * xc)
        h = hloc + acum * hc_scr[0:1, :]
        h_ref[...] = h
        hc_scr[...] = jnp.broadcast_to(_row(h, tb - 1), hc_scr.shape)
        xprev_scr[...] = x[tb - 8:, :]
        y = h * _gelu(gate_ref[...])
        ms = _mm_exact_r(y * y, _group_matrix(DB, 1.0 / GRP).astype(MXU))
        ob_ref[...] = (y * lax.rsqrt(ms + EPS) * vec_ref[4:5, :]).astype(ob_ref.dtype)

    return pl.pallas_call(
        body, name="lru_fwd", grid=(t // tb,),
        in_specs=[pl.BlockSpec((tb, DB), lambda i: (i, xcol)),
                  pl.BlockSpec((tb, DB), lambda i: (i, gcol)),
                  pl.BlockSpec((8, DB), lambda i: (0, 0)),
                  pl.BlockSpec((8, DB), lambda i: (0, 0)),
                  pl.BlockSpec((DB, DB), lambda i: (0, 0)),
                  pl.BlockSpec((DB, DB), lambda i: (0, 0))],
        out_specs=[pl.BlockSpec((tb, DB), lambda i: (i, 0)),
                   pl.BlockSpec((tb, DB), lambda i: (i, 0))],
        out_shape=[jax.ShapeDtypeStruct((t, DB), SAVE), jax.ShapeDtypeStruct((t, DB), F32)],
        scratch_shapes=[pltpu.VMEM((8, DB), F32), pltpu.VMEM((8, DB), F32)],
        compiler_params=_params(("arbitrary",)),
    )(z, z, cw, vec, wa, wx)


def _lru_bwd(z, hseq, dob, cw, vec, wa, wx, dz, tb):
    t = z.shape[0]
    nb = t // tb
    xcol, gcol = (4 * DA) // DB, (4 * DA) // DB + 1
    per = tb // 8

    def body(x_ref, xh_ref, gate_ref, h_ref, hh_ref, dob_ref, cw_ref, vec_ref, wa_ref, wx_ref, _,
             dz_ref, dcw_ref, dvec_ref, dwa_ref, dwx_ref, gc_scr, an_scr, dxc_scr):
        step = pl.program_id(0)
        blk = nb - 1 - step

        @pl.when(step == 0)
        def _():
            for ref in (gc_scr, an_scr, dxc_scr, dcw_ref, dvec_ref, dwa_ref, dwx_ref):
                ref[...] = jnp.zeros_like(ref)

        first = (blk > 0).astype(F32)
        x = x_ref[...]
        xs, xc, r, gi, sp, a, mult = _lru_gates(x, xh_ref[...] * first, cw_ref, vec_ref, wa_ref, wx_ref)
        h = h_ref[...]
        hprev = _shift_down(hh_ref[...] * first, h, 1)
        ge, dge = _gelu_and_grad(gate_ref[...])
        y = h * ge
        gmat = _group_matrix(DB, 1.0 / GRP).astype(MXU)
        rstd = lax.rsqrt(_mm_exact_r(y * y, gmat) + EPS)
        n = y * rstd
        d_ob = dob_ref[...]
        dn = d_ob * vec_ref[4:5, :]
        dvec_ref[4:5, :] += jnp.sum(d_ob * n, axis=0, keepdims=True)
        dy = rstd * (dn - n * _mm_exact_r(dn * n, gmat))
        dh = dy * ge
        dgate = dy * h * dge

        row = _iota(a.shape, 0)
        anext = jnp.where(row == tb - 1, an_scr[0:1, :], pltpu.roll(a, tb - 1, axis=0))
        acum, gloc = _scan_up(anext, dh)
        g = gloc + acum * gc_scr[0:1, :]
        gc_scr[...] = jnp.broadcast_to(_row(g, 0), gc_scr.shape)
        an_scr[...] = jnp.broadcast_to(_row(a, 0), an_scr.shape)

        da = g * hprev
        dmult = g * gi * xc
        dgi = g * mult * xc
        dxc = g * mult * gi
        dla = da * a - dmult * (a * a) / mult
        dr = dla * (-LRU_C * sp)
        dsp = jnp.sum(dla * (-LRU_C * r), axis=0, keepdims=True)
        lam = vec_ref[3:4, :]
        dvec_ref[3:4, :] += -dsp * _sigmoid(-lam)
        dpa = dr * r * (1.0 - r)
        dpx = dgi * gi * (1.0 - gi)
        dwa_ref[...] += _mm_tn(xc, dpa)
        dwx_ref[...] += _mm_tn(xc, dpx)
        dvec_ref[1:2, :] += jnp.sum(dpa, axis=0, keepdims=True)
        dvec_ref[2:3, :] += jnp.sum(dpx, axis=0, keepdims=True)
        dxc = dxc + _mm_nt(dpa, wa_ref[...]) + _mm_nt(dpx, wx_ref[...])
        dvec_ref[0:1, :] += jnp.sum(dxc, axis=0, keepdims=True)
        for tap in range(4):
            dcw_ref[tap:tap + 1, :] += jnp.sum(dxc * xs[3 - tap], axis=0, keepdims=True)
        nxt = dxc_scr[...]
        dx = (cw_ref[3:4, :] * dxc + cw_ref[2:3, :] * _shift_up(dxc, nxt, 1)
              + cw_ref[1:2, :] * _shift_up(dxc, nxt, 2) + cw_ref[0:1, :] * _shift_up(dxc, nxt, 3))
        dxc_scr[...] = dxc[:8, :]
        dz_ref[:, :DB] = dx.astype(dz_ref.dtype)
        dz_ref[:, DB:] = dgate.astype(dz_ref.dtype)

    def halo(col):
        return lambda s: (jnp.maximum((nb - 1 - s) * per - 1, 0), col)

    const = lambda s: (0, 0)
    return pl.pallas_call(
        body, name="lru_bwd", grid=(nb,),
        in_specs=[pl.BlockSpec((tb, DB), lambda s: (nb - 1 - s, xcol)),
                  pl.BlockSpec((8, DB), halo(xcol)),
                  pl.BlockSpec((tb, DB), lambda s: (nb - 1 - s, gcol)),
                  pl.BlockSpec((tb, DB), lambda s: (nb - 1 - s, 0)),
                  pl.BlockSpec((8, DB), halo(0)),
                  pl.BlockSpec((tb, DB), lambda s: (nb - 1 - s, 0)),
                  pl.BlockSpec((8, DB), const), pl.BlockSpec((8, DB), const),
                  pl.BlockSpec((DB, DB), const), pl.BlockSpec((DB, DB), const),
                  pl.BlockSpec(memory_space=pl.ANY)],
        out_specs=[pl.BlockSpec((tb, 2 * DB), lambda s: (nb - 1 - s, (4 * DA) // (2 * DB))),
                   pl.BlockSpec((8, DB), const), pl.BlockSpec((8, DB), const),
                   pl.BlockSpec((DB, DB), const), pl.BlockSpec((DB, DB), const)],
        out_shape=[jax.ShapeDtypeStruct((t, DIN), SAVE), jax.ShapeDtypeStruct((8, DB), F32),
                   jax.ShapeDtypeStruct((8, DB), F32), jax.ShapeDtypeStruct((DB, DB), F32),
                   jax.ShapeDtypeStruct((DB, DB), F32)],
        scratch_shapes=[pltpu.VMEM((8, DB), F32), pltpu.VMEM((8, DB), F32), pltpu.VMEM((8, DB), F32)],
        input_output_aliases={10: 0},
        compiler_params=_params(("arbitrary",)),
    )(z, z, z, hseq, hseq, dob, cw, vec, wa, wx, dz)


def _sgu_block(u_ref, v_ref, w_ref, b_ref, gmat, tb):
    uu, duu = _gelu_and_grad(u_ref[...])
    vv, dvv = _gelu_and_grad(v_ref[...])
    dlt = vv - _mm_exact_r(vv, gmat)
    rstd_v = lax.rsqrt(_mm_exact_r(dlt * dlt, gmat) + EPS)
    vn = dlt * rstd_v
    col = _iota((CCH, DC), 1) // GRP
    causal = _iota((CCH, CCH), 0) >= _iota((CCH, CCH), 1)
    ws = [jnp.where(causal, w_ref[g], 0.0) for g in range(DC // GRP)]
    zs = []
    for ch in range(tb // CCH):
        vn_c = vn[ch * CCH:(ch + 1) * CCH]
        zz = b_ref[...]
        for g, w in enumerate(ws):
            zz = zz + jnp.where(col == g, _mm(w, vn_c), 0.0)
        zs.append(zz)
    return uu, duu, dvv, rstd_v, vn, jnp.concatenate(zs, axis=0), ws, col, causal


def _sgu_fwd(z, w, bias, gain, tb):
    t = z.shape[0]
    ucol, vcol = (4 * DA + 2 * DB) // DC, (4 * DA + 2 * DB) // DC + 1

    def body(u_ref, v_ref, w_ref, b_ref, g_ref, oc_ref):
        gmat = _group_matrix(DC, 1.0 / GRP).astype(MXU)
        uu, _, _, _, _, zz, _, _, _ = _sgu_block(u_ref, v_ref, w_ref, b_ref, gmat, tb)
        y = uu * zz
        ms = _mm_exact_r(y * y, gmat)
        oc_ref[...] = (y * lax.rsqrt(ms + EPS) * g_ref[...]).astype(oc_ref.dtype)

    const = lambda i: (0, 0)
    return pl.pallas_call(
        body, name="sgu_fwd", grid=(t // tb,),
        in_specs=[pl.BlockSpec((tb, DC), lambda i: (i, ucol)),
                  pl.BlockSpec((tb, DC), lambda i: (i, vcol)),
                  pl.BlockSpec((DC // GRP, CCH, CCH), lambda i: (0, 0, 0)),
                  pl.BlockSpec((CCH, DC), const), pl.BlockSpec((1, DC), const)],
        out_specs=pl.BlockSpec((tb, DC), lambda i: (i, 0)),
        out_shape=jax.ShapeDtypeStruct((t, DC), SAVE),
        compiler_params=_params(("parallel",)),
    )(z, z, w, bias, gain)


def _sgu_bwd(z, doc, w, bias, gain, dz, tb):
    t = z.shape[0]
    nb = t // tb
    ucol, vcol = (4 * DA + 2 * DB) // DC, (4 * DA + 2 * DB) // DC + 1
    ng = DC // GRP

    def body(u_ref, v_ref, doc_ref, w_ref, b_ref, g_ref, _, dz_ref, dw_ref, dbias_ref, dgain_ref, dbsum_scr):
        i = pl.program_id(0)

        @pl.when(i == 0)
        def _():
            for ref in (dw_ref, dgain_ref, dbsum_scr):
                ref[...] = jnp.zeros_like(ref)

        gmat = _group_matrix(DC, 1.0 / GRP).astype(MXU)
        uu, duu, dvv, rstd_v, vn, zz, ws, col, causal = _sgu_block(u_ref, v_ref, w_ref, b_ref, gmat, tb)
        y = uu * zz
        rstd = lax.rsqrt(_mm_exact_r(y * y, gmat) + EPS)
        n = y * rstd
        d_oc = doc_ref[...]
        dn = d_oc * g_ref[...]
        dgain_ref[0:1, :] += jnp.sum(d_oc * n, axis=0, keepdims=True)
        dy = rstd * (dn - n * _mm_exact_r(dn * n, gmat))
        dzz = dy * uu
        dz_ref[:, :DC] = (dy * zz * duu).astype(dz_ref.dtype)
        dvns = []
        for ch in range(tb // CCH):
            rows = slice(ch * CCH, (ch + 1) * CCH)
            dzz_c, vn_c = dzz[rows], vn[rows]
            dbsum_scr[...] += dzz_c
            dvn = jnp.zeros_like(dzz_c)
            for g in range(ng):
                sel = col == g
                dvn = dvn + jnp.where(sel, _mm_tn(ws[g], dzz_c), 0.0)
                dw_ref[g] += jnp.where(causal, _mm_nt(jnp.where(sel, dzz_c, 0.0), vn_c), 0.0)
            dvns.append(dvn)
        dvn = jnp.concatenate(dvns, axis=0)
        dv = rstd_v * (dvn - _mm_exact_r(dvn, gmat) - vn * _mm_exact_r(dvn * vn, gmat))
        dz_ref[:, DC:] = (dv * dvv).astype(dz_ref.dtype)

        @pl.when(i == nb - 1)
        def _():
            dbias_ref[...] = _mm_exact_r(dbsum_scr[...], _group_matrix(DC, 1.0).astype(MXU))

    const = lambda i: (0, 0)
    return pl.pallas_call(
        body, name="sgu_bwd", grid=(nb,),
        in_specs=[pl.BlockSpec((tb, DC), lambda i: (i, ucol)),
                  pl.BlockSpec((tb, DC), lambda i: (i, vcol)),
                  pl.BlockSpec((tb, DC), lambda i: (i, 0)),
                  pl.BlockSpec((ng, CCH, CCH), lambda i: (0, 0, 0)),
                  pl.BlockSpec((CCH, DC), const), pl.BlockSpec((1, DC), const),
                  pl.BlockSpec(memory_space=pl.ANY)],
        out_specs=[pl.BlockSpec((tb, 2 * DC), lambda i: (i, (4 * DA + 2 * DB) // (2 * DC))),
                   pl.BlockSpec((ng, CCH, CCH), lambda i: (0, 0, 0)),
                   pl.BlockSpec((CCH, DC), const), pl.BlockSpec((8, DC), const)],
        out_shape=[jax.ShapeDtypeStruct((t, DIN), SAVE), jax.ShapeDtypeStruct((ng, CCH, CCH), F32),
                   jax.ShapeDtypeStruct((CCH, DC), F32), jax.ShapeDtypeStruct((8, DC), F32)],
        scratch_shapes=[pltpu.VMEM((CCH, DC), F32)],
        input_output_aliases={6: 0},
        compiler_params=_params(("arbitrary",)),
    )(z, z, doc, w, bias, gain, dz)


def _head(h, gain, target, tm):
    t = h.shape[0]

    def body(h_ref, g_ref, t_ref, dh_ref, loss_ref, dgain_ref):
        @pl.when(pl.program_id(0) == 0)
        def _():
            loss_ref[...] = jnp.zeros_like(loss_ref)
            dgain_ref[...] = jnp.zeros_like(dgain_ref)

        hh = h_ref[...]
        gain = g_ref[...]
        rstd = lax.rsqrt(jnp.mean(hh * hh, axis=-1, keepdims=True) + EPS)
        xhat = hh * rstd
        err = xhat * gain - t_ref[...]
        per_tok = jnp.mean(err * err, axis=-1, keepdims=True)
        loss_ref[...] += 0.5 * jnp.sum(per_tok, axis=0, keepdims=True)
        dy = err * (1.0 / D)
        dgain_ref[...] += jnp.sum(dy * xhat, axis=0, keepdims=True)
        dxh = dy * gain
        dh_ref[...] = rstd * (dxh - xhat * jnp.mean(dxh * xhat, axis=-1, keepdims=True))

    return pl.pallas_call(
        body, name="head", grid=(t // tm,),
        in_specs=[pl.BlockSpec((tm, D), lambda i: (i, 0)),
                  pl.BlockSpec((1, D), lambda i: (0, 0)),
                  pl.BlockSpec((tm, D), lambda i: (i, 0))],
        out_specs=[pl.BlockSpec((tm, D), lambda i: (i, 0)),
                   pl.BlockSpec((1, 128), lambda i: (0, 0)),
                   pl.BlockSpec((1, D), lambda i: (0, 0))],
        out_shape=[jax.ShapeDtypeStruct((t, D), F32), jax.ShapeDtypeStruct((1, 128), F32),
                   jax.ShapeDtypeStruct((1, D), F32)],
        compiler_params=_params(("arbitrary",)),
    )(h, gain, target)


def _adamw(w, g, m, v):
    m = ADAM_B1 * m + (1.0 - ADAM_B1) * g
    v = ADAM_B2 * v + (1.0 - ADAM_B2) * (g * g)
    m_hat = m / (1.0 - ADAM_B1 ** ADAM_STEP)
    v_hat = v / (1.0 - ADAM_B2 ** ADAM_STEP)
    delta = -ADAM_LR * (m_hat / (jnp.sqrt(v_hat) + ADAM_EPS) + ADAM_WD * w)
    return delta, m, v


def _adamw_big(recv, w, m, v, tr, name, after, transposed=False):
    depth, rows, cols = w.shape
    rspec = (pl.BlockSpec((NDEV, cols, tr), lambda i: (0, 0, i)) if transposed
             else pl.BlockSpec((NDEV, tr, cols), lambda i: (0, i, 0)))

    def body(*refs):
        r_refs = refs[:depth]
        w_ref, m_ref, v_ref, _, g_out, d_out, m_out, v_out = refs[depth:]
        for l in range(depth):
            g = r_refs[l][0].astype(F32)
            for k in range(1, NDEV):
                g = g + r_refs[l][k].astype(F32)
            if transposed:
                g = g.T
            delta, m_, v_ = _adamw(w_ref[l], g, m_ref[l], v_ref[l])
            g_out[l] = g
            d_out[l] = delta
            m_out[l] = m_
            v_out[l] = v_

    spec = pl.BlockSpec((depth, tr, cols), lambda i: (0, i, 0))
    return pl.pallas_call(
        body, name=name, grid=(rows // tr,),
        in_specs=[rspec] * depth + [spec] * 3
        + [pl.BlockSpec(memory_space=pl.ANY)],
        out_specs=[spec] * 4, out_shape=[jax.ShapeDtypeStruct((depth, rows, cols), F32)] * 4,
        compiler_params=_params(("parallel",)),
    )(*recv, w, m, v, after)


def _sum_devices(recv):
    _, r, _ = recv.shape

    def body(r_ref, out_ref):
        g = r_ref[0]
        for k in range(1, NDEV):
            g = g + r_ref[k]
        out_ref[...] = g

    return pl.pallas_call(body, name="sum_devices", out_shape=jax.ShapeDtypeStruct((r, 128), F32))(recv)


def _adamw_small(w, g, m, v):
    def body(w_ref, g_ref, m_ref, v_ref, d_out, m_out, v_out):
        delta, m_, v_ = _adamw(w_ref[...], g_ref[...], m_ref[...], v_ref[...])
        d_out[...] = delta
        m_out[...] = m_
        v_out[...] = v_

    return pl.pallas_call(body, name="adamw_small", out_shape=[jax.ShapeDtypeStruct(w.shape, F32)] * 3)(w, g, m, v)


def _pack(arrs):
    flat = jnp.concatenate([a.reshape(-1) for a in arrs])
    pad = (-flat.shape[0]) % 1024
    return jnp.pad(flat, (0, pad)).reshape(-1, 128)


def _unpack(buf, like):
    flat = buf.reshape(-1)
    out, off = [], 0
    for a in like:
        out.append(flat[off:off + a.size].reshape(a.shape))
        off += a.size
    return out


def _block_diag(w):
    nb, bd, _ = w.shape
    eye = jnp.eye(nb, dtype=w.dtype)
    return (eye[:, None, :, None] * w[:, :, None, :]).reshape(nb * bd, nb * bd)


def _diag_blocks(w):
    nb = w.shape[0] // GRP
    return jnp.stack([w[g * GRP:(g + 1) * GRP, g * GRP:(g + 1) * GRP] for g in range(nb)])


SMALL = ['ffn1_norm', 'mix_norm', 'hgrn_lb_logits', 'hgrn_norm', 'conv_b', 'lru_wa', 'lru_ba', 'lru_wx', 'lru_bx',
         'lru_lambda', 'lru_norm', 'sgu_w', 'sgu_b', 'sgu_norm', 'ffn2_norm', 'final_norm']
NAMES = ['ffn1_norm', 'ffn1_wg', 'ffn1_wu', 'ffn1_wd', 'mix_norm', 'w_in', 'hgrn_lb_logits', 'hgrn_norm', 'conv_w',
         'conv_b', 'lru_wa', 'lru_ba', 'lru_wx', 'lru_bx', 'lru_lambda', 'lru_norm', 'sgu_w', 'sgu_b', 'sgu_norm',
         'w_out', 'ffn2_norm', 'ffn2_wg', 'ffn2_wu', 'ffn2_wd', 'final_norm']


def _step(x, target, w, m, v):
    depth = w['ffn1_wg'].shape[0]
    t = x.shape[1]
    h = x.reshape(t, D)
    target = target.reshape(t, D)
    tm_f, tm_b, tb = min(TM_F, t), min(TM_B, t), min(TB, t)
    my = 4 * lax.axis_index("x") + 2 * lax.axis_index("y") + lax.axis_index("c")

    cw_tile = jnp.pad(w['conv_w'].reshape(-1, 128), ((0, 8 - depth), (0, 0)))
    lbs, lb_soft = _lower_bounds(w['hgrn_lb_logits'])

    def row(a):
        return a.reshape(1, -1)

    def tr(a):
        return jnp.swapaxes(a, -1, -2)

    def shards(l, unit):
        if unit == 1:
            return [tr(w['w_in'][l]).astype(WIRE), w['w_out'][l].astype(WIRE)]
        f = 'ffn1' if unit == 0 else 'ffn2'
        return [tr(w[f + '_wg'][l]).astype(WIRE), tr(w[f + '_wu'][l]).astype(WIRE), w[f + '_wd'][l].astype(WIRE)]

    units = [(l, u) for l in range(depth) for u in range(3)]

    def start_ici(idx, deps=()):
        return _transfer_start(shards(*units[idx]), True, "gather_ici_%d_%d" % units[idx], deps=deps)

    def start_d2d(idx, handle, after):
        lands = _transfer_wait(handle, after, "gather_ici_wait_%d_%d" % units[idx])
        return _forward_start(lands, "gather_d2d_%d_%d" % units[idx])

    pipe = dict(idx=0)
    first = start_ici(0)
    pipe['ici'] = start_ici(1, deps=(first['token'],))
    conv_flight = _transfer_start([cw_tile], True, "gather_conv_start", deps=(pipe['ici']['token'],), direct=True)
    pipe['d2d'] = start_d2d(0, first, conv_flight['token'])

    def next_weights(after):
        idx = pipe['idx']
        lands = _transfer_wait(pipe['d2d'], after, "gather_d2d_wait_%d_%d" % units[idx])
        pipe['idx'] = idx + 1
        tok = 0.0
        if idx + 1 < len(units):
            pipe['d2d'] = start_d2d(idx + 1, pipe['ici'], lands[-1])
            tok = pipe['d2d']['token'][0, 0]
            if idx + 2 < len(units):
                pipe['ici'] = start_ici(idx + 2, deps=(pipe['d2d']['token'],))
                tok = pipe['ici']['token'][0, 0]
        return lands, tok

    saved = []
    for l in range(depth):
        lands, tok = next_weights(h)
        s = dict(ffn1=[a.reshape(FF, D) for a in lands], h0=h)
        h, s['xn1'], s['a1'], s['b1'] = _ffn_fwd(h, row(w['ffn1_norm'][l]) + tok, *s['ffn1'], tm_f)
        s['h1'] = h
        (win, wout), tok = next_weights(h)
        win, wout = win.reshape(DIN, D), wout.reshape(D, D)
        s['win'], s['wout'] = win, wout
        z, s['xnm'] = _inproj_fwd(h, row(w['mix_norm'][l]) + tok, win, tm_f)
        s['z'] = z
        s['o'], oa, s['states'] = _hgrn_fwd(z, row(lbs[l]), row(w['hgrn_norm'][l]))
        if l == 0:
            cw_all = _transfer_wait(conv_flight, z, "gather_conv_wait")[0][:, :depth]
            conv_w = jnp.moveaxis(cw_all.reshape(NDEV, depth, 4, DB // NDEV), 0, 2).reshape(depth, 4, DB)
        s['cw'] = jnp.pad(conv_w[l], ((0, 4), (0, 0)))
        s['vec'] = jnp.concatenate([row(w['conv_b'][l]), row(w['lru_ba'][l]), row(w['lru_bx'][l]),
                                    row(w['lru_lambda'][l]), row(w['lru_norm'][l]), jnp.zeros((3, DB), F32)])
        s['wa'], s['wx'] = _block_diag(w['lru_wa'][l]), _block_diag(w['lru_wx'][l])
        ob, s['hseq'] = _lru_fwd(z, s['cw'], s['vec'], s['wa'], s['wx'], tb)
        s['bias'] = jnp.repeat(w['sgu_b'][l].T, GRP, axis=1)
        oc = _sgu_fwd(z, w['sgu_w'][l], s['bias'], row(w['sgu_norm'][l]), tb)
        s['oa'], s['ob'], s['oc'] = oa, ob, oc
        h = _outproj_fwd(h, oa, ob, oc, wout, tm_f)
        s['h2'] = h
        lands, tok = next_weights(h)
        s['ffn2'] = [a.reshape(FF, D) for a in lands]
        h, s['xn2'], s['a2'], s['b2'] = _ffn_fwd(h, row(w['ffn2_norm'][l]) + tok, *s['ffn2'], tm_f)
        saved.append(s)

    dh, loss_part, g_final = _head(h, row(w['final_norm']), target, tm_f)
    loss = lax.psum(loss_part[0, 0], ("x", "y", "c"))

    recv = {k: [None] * depth for k in ('wg1', 'wu1', 'wd1', 'wg2', 'wu2', 'wd2', 'win', 'wout')}
    flight = []

    def land(after):
        handle, kinds, l = flight.pop()
        for k, a in zip(kinds, _transfer_wait(handle, after, f"exchange_wait_{kinds[0]}_{l}")):
            recv[k][l] = a

    def exchange(arrs, kinds, l, deps=()):
        handle = _transfer_start(arrs, False, f"exchange_start_{kinds[0]}_{l}", deps=deps)
        if flight:
            land(handle['token'])
        flight.append((handle, kinds, l))
        return handle['token'][0, 0]

    small = {k: [None] * depth for k in SMALL if k != 'final_norm'}
    dconv = [None] * depth
    dlb = [None] * depth
    tok = 0.0
    for l in reversed(range(depth)):
        s = saved[l]
        dh, g, *cot = _ffn_bwd_x(dh, s['h2'], row(w['ffn2_norm'][l]) + tok, s['a2'], s['b2'], *s['ffn2'], tm_b)
        dws = _ffn_bwd_w(s['xn2'], *cot, tm_f)
        tok = exchange([a.reshape(NDEV, FFS, D) for a in dws], ('wg2', 'wu2', 'wd2'), l)
        small['ffn2_norm'][l] = g
        doa, dob, doc, dwout = _outproj_bwd(dh, s['oa'], s['ob'], s['oc'], s['wout'], tm_f)
        dz, g_hn, dlb[l] = _hgrn_bwd(s['z'], row(lbs[l]), row(w['hgrn_norm'][l]) + tok, s['o'], s['states'], doa)
        small['hgrn_norm'][l] = g_hn
        dz, dcw, dvec, dwa, dwx = _lru_bwd(s['z'], s['hseq'], dob, s['cw'], s['vec'], s['wa'], s['wx'], dz, tb)
        dconv[l] = dcw[:4]
        small['conv_b'][l], small['lru_ba'][l], small['lru_bx'][l] = dvec[0], dvec[1].reshape(4, GRP), dvec[2].reshape(4, GRP)
        small['lru_lambda'][l], small['lru_norm'][l] = dvec[3], dvec[4]
        small['lru_wa'][l], small['lru_wx'][l] = _diag_blocks(dwa), _diag_blocks(dwx)
        dz, dsw, dbias, dgc = _sgu_bwd(s['z'], doc, w['sgu_w'][l], s['bias'], row(w['sgu_norm'][l]), dz, tb)
        small['sgu_w'][l], small['sgu_b'][l], small['sgu_norm'][l] = dsw, dbias[:, ::GRP].T, dgc[0]
        dwin = _inproj_bwd_w(s['xnm'], dz, tm_f)
        tok = exchange([dwin.reshape(NDEV, DINS, D), dwout.reshape(NDEV, D // NDEV, D)], ('win', 'wout'), l)
        dh, g = _inproj_bwd_x(dh, dz, s['h1'], row(w['mix_norm'][l]) + tok, s['win'], tm_b)
        small['mix_norm'][l] = g
        tok = 0.0
        if l == 0:
            small['ffn1_norm'][0] = jnp.zeros((1, D), F32)
            small['hgrn_lb_logits'] = list(_lower_bounds_bwd(lb_soft, jnp.concatenate(dlb, axis=0)))
            parts = [jnp.stack([small[k][j].reshape(w[k].shape[1:]) for j in range(depth)])
                     for k in SMALL if k != 'final_norm']
            parts += [g_final.reshape(D), jnp.stack(dconv)]
            small_flight = _transfer_start([_pack(parts)], True, "gather_small_start", direct=True)
            tok = small_flight['token'][0, 0]
        dh, g, *cot = _ffn_bwd_x(dh, s['h0'], row(w['ffn1_norm'][l]) + tok, s['a1'], s['b1'], *s['ffn1'], tm_b)
        dws = _ffn_bwd_w(s['xn1'], *cot, tm_f)
        before = ()
        if l == 0:
            g_last = _all_gather([g.reshape(8, 128)], "gather_last")[0]
            before = (g_last,)
        else:
            small['ffn1_norm'][l] = g
        tok = exchange([a.reshape(NDEV, FFS, D) for a in dws], ('wg1', 'wu1', 'wd1'), l, before)
    grad_x = dh.reshape(1, t, D)

    out = {}
    last = flight[0][0]['token']

    def ffn_update(f, n, after):
        for kind in ('wg', 'wu'):
            k = f + '_' + kind
            res = _adamw_big(recv[kind + n], tr(w[k]), tr(m[k]), tr(v[k]), 32, "adamw_ffn", after)
            out[k] = tuple(tr(a) for a in res)
        k = f + '_wd'
        out[k] = _adamw_big(recv['wd' + n], w[k], m[k], v[k], 32, "adamw_ffn", after)

    ffn_update('ffn2', '2', last)
    out['w_in'] = _adamw_big(recv['win'], w['w_in'], m['w_in'], v['w_in'], 128, "adamw_win", last, transposed=True)
    out['w_out'] = _adamw_big(recv['wout'], w['w_out'], m['w_out'], v['w_out'], 64, "adamw_wout", last)

    total = _sum_devices(_transfer_wait(small_flight, g_last, "gather_small_wait")[0])
    like = [w[k] for k in SMALL] + [jax.ShapeDtypeStruct((depth, 4, DB), F32)]
    grads = _unpack(total, like)
    gsmall = dict(zip(SMALL, grads[:-1]))
    gsmall['ffn1_norm'] = gsmall['ffn1_norm'].at[0].set(_sum_devices(g_last).reshape(D))
    gsmall['conv_w'] = lax.dynamic_slice_in_dim(grads[-1], my * (DB // NDEV), DB // NDEV, axis=2)
    keys = SMALL + ['conv_w']
    dl, mm, vv = _adamw_small(_pack([w[k] for k in keys]), _pack([gsmall[k] for k in keys]),
                              _pack([m[k] for k in keys]), _pack([v[k] for k in keys]))
    like = [w[k] for k in keys]
    for k, d_, m_, v_ in zip(keys, _unpack(dl, like), _unpack(mm, like), _unpack(vv, like)):
        out[k] = (gsmall[k], d_, m_, v_)
    done = [dl] + [out[k][1][0] for k in ('ffn2_wg', 'ffn2_wu', 'ffn2_wd', 'w_in', 'w_out')]
    land(functools.reduce(lambda p, q: p + q, [a[:1, :1] for a in done]))
    ffn_update('ffn1', '1', last)

    return (loss, grad_x, *[out[k][0] for k in NAMES], *[out[k][1] for k in NAMES],
            *[out[k][2] for k in NAMES], *[out[k][3] for k in NAMES])


def kernel(x, ffn1_norm, ffn1_wg, ffn1_wu, ffn1_wd, mix_norm, w_in, hgrn_lb_logits, hgrn_norm, conv_w, conv_b, lru_wa, lru_ba, lru_wx, lru_bx, lru_lambda, lru_norm, sgu_w, sgu_b, sgu_norm, w_out, ffn2_norm, ffn2_wg, ffn2_wu, ffn2_wd, final_norm, loss_target, m_ffn1_norm, m_ffn1_wg, m_ffn1_wu, m_ffn1_wd, m_mix_norm, m_w_in, m_hgrn_lb_logits, m_hgrn_norm, m_conv_w, m_conv_b, m_lru_wa, m_lru_ba, m_lru_wx, m_lru_bx, m_lru_lambda, m_lru_norm, m_sgu_w, m_sgu_b, m_sgu_norm, m_w_out, m_ffn2_norm, m_ffn2_wg, m_ffn2_wu, m_ffn2_wd, m_final_norm, v_ffn1_norm, v_ffn1_wg, v_ffn1_wu, v_ffn1_wd, v_mix_norm, v_w_in, v_hgrn_lb_logits, v_hgrn_norm, v_conv_w, v_conv_b, v_lru_wa, v_lru_ba, v_lru_wx, v_lru_bx, v_lru_lambda, v_lru_norm, v_sgu_w, v_sgu_b, v_sgu_norm, v_w_out, v_ffn2_norm, v_ffn2_wg, v_ffn2_wu, v_ffn2_wd, v_final_norm):
    args = locals()
    w = {k: args[k] for k in NAMES}
    m = {k: args['m_' + k] for k in NAMES}
    v = {k: args['v_' + k] for k in NAMES}
    return _step(x, loss_target, w, m, v)
```

```python
import functools

import jax
import jax.numpy as jnp
from jax import lax
from jax.experimental import pallas as pl
from jax.experimental.pallas import tpu as pltpu

F32 = jnp.float32
MXU = jnp.bfloat16
SAVE = jnp.bfloat16
WIRE = jnp.bfloat16

NDEV = 8
D = 1024
FF = 2816
FFS = FF // NDEV
FB = 256
FBX = FF // 2
DIN = 3072
DINS = DIN // NDEV
ZB = 512
DA, DB, DC = 512, 256, 256
HD = 128
NH = DA // HD
ACH = 64
ACB = 4
CCH = 128
GRP = 64
EPS = 1e-6
LRU_C = 8.0
VMEM_LIMIT = 60 * 1024 * 1024
TM_F = 1024
TM_B = 512
TB = 1024
SUB = 256

ADAM_LR, ADAM_B1, ADAM_B2, ADAM_EPS, ADAM_WD, ADAM_STEP = 0.001, 0.9, 0.999, 1e-08, 0.01, 10

MESH = pl.DeviceIdType.MESH


def _mm(a, b):
    return jnp.dot(a.astype(MXU), b.astype(MXU), preferred_element_type=F32)


def _mm_nt(a, b):
    return lax.dot_general(a.astype(MXU), b.astype(MXU), (((1,), (1,)), ((), ())), preferred_element_type=F32)


def _mm_tn(a, b):
    return lax.dot_general(a.astype(MXU), b.astype(MXU), (((0,), (0,)), ((), ())), preferred_element_type=F32)


def _split3(x):
    x1 = x.astype(MXU)
    r1 = x - x1.astype(F32)
    x2 = r1.astype(MXU)
    r2 = r1 - x2.astype(F32)
    return x1, x2, r2.astype(MXU)


def _mm_exact_l(c, x):
    x1, x2, x3 = _split3(x)
    return _mm(c, x1) + _mm(c, x2) + _mm(c, x3)


def _mm_exact_r(x, c):
    x1, x2, x3 = _split3(x)
    return _mm(x1, c) + _mm(x2, c) + _mm(x3, c)


def _sigmoid(x):
    return 1.0 / (1.0 + jnp.exp(-x))


def _gelu(x):
    c, k = 0.7978845608028654, 0.044715
    th = jnp.tanh(c * (x + k * x * x * x))
    return 0.5 * x * (1.0 + th)


def _gelu_and_grad(x):
    c, k = 0.7978845608028654, 0.044715
    th = jnp.tanh(c * (x + k * x * x * x))
    g = 0.5 * x * (1.0 + th)
    dg = 0.5 * (1.0 + th) + 0.5 * x * (1.0 - th * th) * c * (1.0 + 3.0 * k * x * x)
    return g, dg


def _expm1(x):
    series = x * (1.0 + x * (0.5 + x * (1.0 / 6.0 + x * (1.0 / 24.0 + x * (1.0 / 120.0)))))
    return jnp.where(jnp.abs(x) < 0.05, series, jnp.exp(x) - 1.0)


def _iota(shape, dim):
    return lax.broadcasted_iota(jnp.int32, shape, dim)


def _group_matrix(n, value):
    r, c = _iota((n, n), 0), _iota((n, n), 1)
    return jnp.where((r // GRP) == (c // GRP), value, 0.0).astype(F32)


def _row(x, k):
    r = _iota(x.shape, 0)
    return jnp.sum(jnp.where(r == k, x, 0.0), axis=0, keepdims=True)


def _rms_bwd(dxn, hh, gain):
    rstd = lax.rsqrt(jnp.mean(hh * hh, axis=-1, keepdims=True) + EPS)
    xhat = hh * rstd
    dxh = dxn * gain
    dh = rstd * (dxh - xhat * jnp.mean(dxh * xhat, axis=-1, keepdims=True))
    return dh, jnp.sum(dxn * xhat, axis=0, keepdims=True)


def _params(sem):
    return pltpu.CompilerParams(dimension_semantics=sem, vmem_limit_bytes=VMEM_LIMIT)


def _all_gather(arrs, name):
    n = len(arrs)

    def body(*refs):
        ins, outs = refs[:n], refs[n:2 * n]
        send_sems, recv_sems, local_sems = refs[2 * n:]
        x, y, c = lax.axis_index("x"), lax.axis_index("y"), lax.axis_index("c")
        me, sibling = (x, y, c), (x, y, 1 - c)
        chips = [(1 - x, y), (x, 1 - y), (1 - x, 1 - y)]

        def slot(px, py, pc):
            return 4 * px + 2 * py + pc

        def copy(a, k, block, to, src=None):
            dst = outs[a].at[slot(*block)]
            return pltpu.make_async_remote_copy(
                src_ref=dst if src is None else src, dst_ref=dst,
                send_sem=send_sems.at[a * 7 + k], recv_sem=recv_sems.at[a * 7 + k],
                device_id=to, device_id_type=MESH)

        started = []
        for a in range(n):
            mine = pltpu.make_async_copy(ins[a], outs[a].at[slot(*me)], local_sems.at[a])
            mine.start()
            started.append(mine)
        first = []
        for a in range(n):
            first.append(copy(a, 0, me, sibling, src=ins[a]))
            first += [copy(a, 1 + j, me, (*chip, c), src=ins[a]) for j, chip in enumerate(chips)]
        for cp in first:
            cp.start()
        passed = []
        for a in range(n):
            for j, chip in enumerate(chips):
                copy(a, 1 + j, (*chip, c), me).wait_recv()
                fwd = copy(a, 4 + j, (*chip, c), sibling)
                fwd.start()
                passed.append(fwd)
        for a in range(n):
            copy(a, 0, sibling, me).wait_recv()
            for j, chip in enumerate(chips):
                copy(a, 4 + j, (*chip, 1 - c), me).wait_recv()
        for cp in first + passed:
            cp.wait_send()
        for mine in started:
            mine.wait()

    hbm = pl.BlockSpec(memory_space=pl.ANY)
    return pl.pallas_call(
        body, name=name,
        out_shape=[jax.ShapeDtypeStruct((NDEV,) + a.shape, a.dtype) for a in arrs],
        in_specs=[hbm] * n, out_specs=[hbm] * n,
        scratch_shapes=[pltpu.SemaphoreType.DMA((7 * n,)), pltpu.SemaphoreType.DMA((7 * n,)),
                        pltpu.SemaphoreType.DMA((n,))],
    )(*arrs)


def _peers():
    x, y, c = lax.axis_index("x"), lax.axis_index("y"), lax.axis_index("c")
    peers = [(x ^ ((k >> 2) & 1), y ^ ((k >> 1) & 1), c ^ (k & 1)) for k in range(1, NDEV)]
    return (x, y, c), 4 * x + 2 * y + c, peers


_HBM = pl.BlockSpec(memory_space=pltpu.HBM)
_SEM = pl.BlockSpec(memory_space=pltpu.SEMAPHORE)
_EFFECT = pltpu.SideEffectType.DATAFLOW_SIDE_EFFECTING


def _transfer_start(arrs, gather, name, deps=(), direct=False):
    n, nd = len(arrs), len(deps)
    shapes = [((NDEV,) + a.shape) if gather else a.shape for a in arrs]

    def body(*refs):
        ins, lands = refs[:n], refs[n:2 * n]
        send_sems, recv_sems, local_sems = refs[2 * n + nd:2 * n + nd + 3]
        token = refs[-1]
        (x, y, c), my, peers = _peers()
        if gather and not direct:
            peers = [(x, y, 1 - c), (1 - x, y, c), (x, 1 - y, c), (1 - x, 1 - y, c)]
        for a in range(n):
            own = ins[a] if gather else ins[a].at[my]
            pltpu.make_async_copy(own, lands[a].at[my], local_sems.at[a]).start()
        for a in range(n):
            for peer in peers:
                src = ins[a] if gather else ins[a].at[4 * peer[0] + 2 * peer[1] + peer[2]]
                pltpu.make_async_remote_copy(
                    src_ref=src, dst_ref=lands[a].at[my], send_sem=send_sems.at[a], recv_sem=recv_sems.at[a],
                    device_id=peer, device_id_type=MESH).start()
        token[...] = jnp.zeros_like(token)

    out_shape = [pltpu.SemaphoreType.DMA((n,))] * 3
    out_shape += [pltpu.HBM(a.shape, a.dtype) for a in arrs]
    out_shape += [pltpu.HBM(s, a.dtype) for s, a in zip(shapes, arrs)]
    out_shape += [jax.ShapeDtypeStruct((8, 128), F32)]
    operands = [pltpu.with_memory_space_constraint(a, pltpu.HBM) for a in arrs]
    operands += [pltpu.with_memory_space_constraint(lax.empty(s, a.dtype), pltpu.HBM) for s, a in zip(shapes, arrs)]
    res = pl.pallas_call(
        body, name=name, out_shape=out_shape,
        in_specs=[_HBM] * (2 * n) + [pl.BlockSpec(memory_space=pl.ANY)] * nd,
        out_specs=[_SEM] * 3 + [_HBM] * (2 * n) + [pl.BlockSpec(memory_space=pltpu.VMEM)],
        input_output_aliases={i: 3 + i for i in range(2 * n)},
        compiler_params=pltpu.CompilerParams(has_side_effects=_EFFECT),
    )(*operands, *deps)
    return dict(sems=res[:3], src=res[3:3 + n], lands=res[3 + n:3 + 2 * n], token=res[-1], n=n,
                count=4 if gather and not direct else NDEV - 1)


def _forward_start(lands, name, deps=()):
    n, nd = len(lands), len(deps)

    def body(*refs):
        zone = refs[:n]
        send_sems, recv_sems = refs[n + nd:n + nd + 2]
        token = refs[-1]
        (x, y, c), _, _ = _peers()
        for a in range(n):
            for px, py in ((1 - x, y), (x, 1 - y), (1 - x, 1 - y)):
                block = zone[a].at[4 * px + 2 * py + c]
                pltpu.make_async_remote_copy(
                    src_ref=block, dst_ref=block, send_sem=send_sems.at[a], recv_sem=recv_sems.at[a],
                    device_id=(x, y, 1 - c), device_id_type=MESH).start()
        token[...] = jnp.zeros_like(token)

    res = pl.pallas_call(
        body, name=name,
        out_shape=[pltpu.SemaphoreType.DMA((n,))] * 2 + [pltpu.HBM(a.shape, a.dtype) for a in lands]
        + [jax.ShapeDtypeStruct((8, 128), F32)],
        in_specs=[_HBM] * n + [pl.BlockSpec(memory_space=pl.ANY)] * nd,
        out_specs=[_SEM] * 2 + [_HBM] * n + [pl.BlockSpec(memory_space=pltpu.VMEM)],
        input_output_aliases={i: 2 + i for i in range(n)},
        compiler_params=pltpu.CompilerParams(has_side_effects=_EFFECT),
    )(*lands, *deps)
    return dict(sems=res[:2], src=[], lands=res[2:2 + n], token=res[-1], n=n, count=3)


def _transfer_wait(handle, after, name):
    n, count = handle["n"], handle["count"]
    src, lands, sems = list(handle["src"]), list(handle["lands"]), list(handle["sems"])
    ns = len(src)

    def body(*refs):
        zone = refs[ns:ns + n]
        sem_refs = refs[ns + n:ns + n + len(sems)]
        me, _, _ = _peers()
        for a in range(n):
            moved = zone[a].at[pl.ds(0, count)]
            both = pltpu.make_async_remote_copy(
                src_ref=moved, dst_ref=moved, send_sem=sem_refs[0].at[a], recv_sem=sem_refs[1].at[a],
                device_id=me, device_id_type=MESH)
            both.wait_send()
            both.wait_recv()
            if len(sems) == 3:
                pltpu.make_async_copy(zone[a].at[0], zone[a].at[1], sem_refs[2].at[a]).wait()

    res = pl.pallas_call(
        body, name=name,
        out_shape=[pltpu.HBM(a.shape, a.dtype) for a in src + lands],
        in_specs=[_HBM] * (ns + n) + [_SEM] * len(sems) + [pl.BlockSpec(memory_space=pl.ANY)],
        out_specs=[_HBM] * (ns + n),
        input_output_aliases={i: i for i in range(ns + n)},
        compiler_params=pltpu.CompilerParams(has_side_effects=_EFFECT),
    )(*src, *lands, *sems, after)
    return list(res[ns:])


def _ffn_fwd(h, gain, wg, wu, wd, tm):
    t = h.shape[0]
    nj = FF // FBX

    def body(h_ref, g_ref, wg_ref, wu_ref, wd_ref, out_ref, xn_ref, a_ref, b_ref, acc_ref):
        j = pl.program_id(1)

        @pl.when(j == 0)
        def _():
            hh = h_ref[...]
            rstd = lax.rsqrt(jnp.mean(hh * hh, axis=-1, keepdims=True) + EPS)
            xn_ref[...] = (hh * rstd * g_ref[...]).astype(xn_ref.dtype)
            acc_ref[...] = jnp.zeros_like(acc_ref)

        sub = min(SUB, tm)
        for r in range(tm // sub):
            rows = slice(r * sub, (r + 1) * sub)
            xn = xn_ref[rows, :]
            y = None
            for c0 in range(0, FBX, FB):
                cols = slice(c0, min(c0 + FB, FBX))
                a = _mm_nt(xn, wg_ref[cols, :])
                b = _mm_nt(xn, wu_ref[cols, :])
                a_ref[rows, cols] = a.astype(a_ref.dtype)
                b_ref[rows, cols] = b.astype(b_ref.dtype)
                part = _mm(a * _sigmoid(a) * b, wd_ref[cols, :])
                y = part if y is None else y + part
            acc_ref[rows, :] += y

        @pl.when(j == nj - 1)
        def _():
            out_ref[...] = h_ref[...] + 0.5 * acc_ref[...]

    wspec = pl.BlockSpec((FBX, D), lambda i, j: (j, 0))
    return pl.pallas_call(
        body, name="ffn_fwd", grid=(t // tm, nj),
        in_specs=[pl.BlockSpec((tm, D), lambda i, j: (i, 0)),
                  pl.BlockSpec((1, D), lambda i, j: (0, 0)), wspec, wspec, wspec],
        out_specs=[pl.BlockSpec((tm, D), lambda i, j: (i, 0)),
                   pl.BlockSpec((tm, D), lambda i, j: (i, 0)),
                   pl.BlockSpec((tm, FBX), lambda i, j: (i, j)),
                   pl.BlockSpec((tm, FBX), lambda i, j: (i, j))],
        out_shape=[jax.ShapeDtypeStruct((t, D), F32), jax.ShapeDtypeStruct((t, D), SAVE),
                   jax.ShapeDtypeStruct((t, FF), SAVE), jax.ShapeDtypeStruct((t, FF), SAVE)],
        scratch_shapes=[pltpu.VMEM((tm, D), F32)],
        compiler_params=_params(("parallel", "arbitrary")),
    )(h, gain, wg, wu, wd)


def _ffn_bwd_x(dout, h, gain, a_sv, b_sv, wg, wu, wd, tm):
    t = h.shape[0]
    nj = FF // FBX

    def body(dout_ref, h_ref, g_ref, a_ref, b_ref, wg_ref, wu_ref, wd_ref,
             dh_ref, dgain_ref, dy_ref, da_ref, db_ref, s_ref, acc_ref):
        i, j = pl.program_id(0), pl.program_id(1)

        @pl.when((i == 0) & (j == 0))
        def _():
            dgain_ref[...] = jnp.zeros_like(dgain_ref)

        @pl.when(j == 0)
        def _():
            dy_ref[...] = (0.5 * dout_ref[...]).astype(dy_ref.dtype)
            acc_ref[...] = jnp.zeros_like(acc_ref)

        sub = min(SUB, tm)
        for r in range(tm // sub):
            rows = slice(r * sub, (r + 1) * sub)
            dy = dy_ref[rows, :]
            dx = None
            for c0 in range(0, FBX, FB):
                cols = slice(c0, min(c0 + FB, FBX))
                ds = _mm_nt(dy, wd_ref[cols, :])
                a, b = a_ref[rows, cols].astype(F32), b_ref[rows, cols].astype(F32)
                sg = _sigmoid(a)
                sa = a * sg
                da = (ds * b * (sg * (1.0 + a * (1.0 - sg)))).astype(MXU)
                db = (ds * sa).astype(MXU)
                da_ref[rows, cols] = da.astype(da_ref.dtype)
                db_ref[rows, cols] = db.astype(db_ref.dtype)
                s_ref[rows, cols] = (sa * b).astype(s_ref.dtype)
                part = _mm(da, wg_ref[cols, :]) + _mm(db, wu_ref[cols, :])
                dx = part if dx is None else dx + part
            acc_ref[rows, :] += dx

        @pl.when(j == nj - 1)
        def _():
            dh, dg = _rms_bwd(acc_ref[...], h_ref[...], g_ref[...])
            dh_ref[...] = dout_ref[...] + dh
            dgain_ref[...] += dg

    tok = pl.BlockSpec((tm, D), lambda i, j: (i, 0))
    act = pl.BlockSpec((tm, FBX), lambda i, j: (i, j))
    wspec = pl.BlockSpec((FBX, D), lambda i, j: (j, 0))
    return pl.pallas_call(
        body, name="ffn_bwd_x", grid=(t // tm, nj),
        in_specs=[tok, tok, pl.BlockSpec((1, D), lambda i, j: (0, 0)), act, act, wspec, wspec, wspec],
        out_specs=[tok, pl.BlockSpec((1, D), lambda i, j: (0, 0)), tok, act, act, act],
        out_shape=[jax.ShapeDtypeStruct((t, D), F32), jax.ShapeDtypeStruct((1, D), F32),
                   jax.ShapeDtypeStruct((t, D), SAVE)] + [jax.ShapeDtypeStruct((t, FF), SAVE)] * 3,
        scratch_shapes=[pltpu.VMEM((tm, D), F32)],
        compiler_params=_params(("arbitrary", "arbitrary")),
    )(dout, h, gain, a_sv, b_sv, wg, wu, wd)


def _ffn_bwd_w(xn, dy, da, db, s, tm):
    t = xn.shape[0]
    nt = t // tm
    nj = FF // FBX

    def body(xn_ref, dy_ref, da_ref, db_ref, s_ref, dwg_ref, dwu_ref, dwd_ref, ag_scr, au_scr, ad_scr):
        i = pl.program_id(1)

        @pl.when(i == 0)
        def _():
            for ref in (ag_scr, au_scr, ad_scr):
                ref[...] = jnp.zeros_like(ref)

        xn, dy = xn_ref[...], dy_ref[...]
        for c0 in range(0, FBX, FB):
            rows = slice(c0, min(c0 + FB, FBX))
            ag_scr[rows, :] += _mm_tn(da_ref[:, rows], xn)
            au_scr[rows, :] += _mm_tn(db_ref[:, rows], xn)
            ad_scr[rows, :] += _mm_tn(s_ref[:, rows], dy)

        @pl.when(i == nt - 1)
        def _():
            for out, ref in ((dwg_ref, ag_scr), (dwu_ref, au_scr), (dwd_ref, ad_scr)):
                out[...] = ref[...].astype(out.dtype)

    tok = pl.BlockSpec((tm, D), lambda j, i: (i, 0))
    act = pl.BlockSpec((tm, FBX), lambda j, i: (i, j))
    wspec = pl.BlockSpec((FBX, D), lambda j, i: (j, 0))
    return pl.pallas_call(
        body, name="ffn_bwd_w", grid=(nj, nt),
        in_specs=[tok, tok, act, act, act], out_specs=[wspec] * 3,
        out_shape=[jax.ShapeDtypeStruct((FF, D), WIRE)] * 3,
        scratch_shapes=[pltpu.VMEM((FBX, D), F32)] * 3,
        compiler_params=_params(("parallel", "arbitrary")),
    )(xn, dy, da, db, s)


def _inproj_fwd(h, gain, win, tm):
    t = h.shape[0]

    def body(h_ref, g_ref, w_ref, z_ref, xn_ref):
        hh = h_ref[...]
        rstd = lax.rsqrt(jnp.mean(hh * hh, axis=-1, keepdims=True) + EPS)
        xn = (hh * rstd * g_ref[...]).astype(MXU)
        xn_ref[...] = xn.astype(xn_ref.dtype)
        for j in range(DIN // ZB):
            z_ref[:, j * ZB:(j + 1) * ZB] = _mm_nt(xn, w_ref[j * ZB:(j + 1) * ZB, :])

    return pl.pallas_call(
        body, name="inproj_fwd", grid=(t // tm,),
        in_specs=[pl.BlockSpec((tm, D), lambda i: (i, 0)),
                  pl.BlockSpec((1, D), lambda i: (0, 0)),
                  pl.BlockSpec((DIN, D), lambda i: (0, 0))],
        out_specs=[pl.BlockSpec((tm, DIN), lambda i: (i, 0)),
                   pl.BlockSpec((tm, D), lambda i: (i, 0))],
        out_shape=[jax.ShapeDtypeStruct((t, DIN), F32), jax.ShapeDtypeStruct((t, D), SAVE)],
        compiler_params=_params(("parallel",)),
    )(h, gain, win)


def _inproj_bwd_x(dres, dz, h, gain, win, tm):
    t = h.shape[0]

    def body(dres_ref, dz_ref, h_ref, g_ref, w_ref, dh_ref, dgain_ref):
        @pl.when(pl.program_id(0) == 0)
        def _():
            dgain_ref[...] = jnp.zeros_like(dgain_ref)

        dh, dg = _rms_bwd(_mm(dz_ref[...], w_ref[...]), h_ref[...], g_ref[...])
        dh_ref[...] = dres_ref[...] + dh
        dgain_ref[...] += dg

    return pl.pallas_call(
        body, name="inproj_bwd_x", grid=(t // tm,),
        in_specs=[pl.BlockSpec((tm, D), lambda i: (i, 0)),
                  pl.BlockSpec((tm, DIN), lambda i: (i, 0)),
                  pl.BlockSpec((tm, D), lambda i: (i, 0)),
                  pl.BlockSpec((1, D), lambda i: (0, 0)),
                  pl.BlockSpec((DIN, D), lambda i: (0, 0))],
        out_specs=[pl.BlockSpec((tm, D), lambda i: (i, 0)),
                   pl.BlockSpec((1, D), lambda i: (0, 0))],
        out_shape=[jax.ShapeDtypeStruct((t, D), F32), jax.ShapeDtypeStruct((1, D), F32)],
        compiler_params=_params(("arbitrary",)),
    )(dres, dz, h, gain, win)


def _inproj_bwd_w(xn, dz, tm):
    t = xn.shape[0]
    nt = t // tm

    def body(xn_ref, dz_ref, dw_ref, acc_scr):
        i = pl.program_id(1)

        @pl.when(i == 0)
        def _():
            acc_scr[...] = jnp.zeros_like(acc_scr)

        acc_scr[...] += _mm_tn(dz_ref[...], xn_ref[...])

        @pl.when(i == nt - 1)
        def _():
            dw_ref[...] = acc_scr[...].astype(dw_ref.dtype)

    return pl.pallas_call(
        body, name="inproj_bwd_w", grid=(DIN // ZB, nt),
        in_specs=[pl.BlockSpec((tm, D), lambda j, i: (i, 0)),
                  pl.BlockSpec((tm, ZB), lambda j, i: (i, j))],
        out_specs=pl.BlockSpec((ZB, D), lambda j, i: (j, 0)),
        out_shape=jax.ShapeDtypeStruct((DIN, D), WIRE),
        scratch_shapes=[pltpu.VMEM((ZB, D), F32)],
        compiler_params=_params(("parallel", "arbitrary")),
    )(xn, dz)


def _outproj_fwd(h, oa, ob, oc, wout, tm):
    t = h.shape[0]

    def body(h_ref, oa_ref, ob_ref, oc_ref, w_ref, out_ref):
        ym = jnp.concatenate([oa_ref[...], ob_ref[...], oc_ref[...]], axis=1)
        out_ref[...] = h_ref[...] + _mm(ym, w_ref[...])

    return pl.pallas_call(
        body, name="outproj_fwd", grid=(t // tm,),
        in_specs=[pl.BlockSpec((tm, D), lambda i: (i, 0)),
                  pl.BlockSpec((tm, DA), lambda i: (i, 0)),
                  pl.BlockSpec((tm, DB), lambda i: (i, 0)),
                  pl.BlockSpec((tm, DC), lambda i: (i, 0)),
                  pl.BlockSpec((D, D), lambda i: (0, 0))],
        out_specs=pl.BlockSpec((tm, D), lambda i: (i, 0)),
        out_shape=jax.ShapeDtypeStruct((t, D), F32),
        compiler_params=_params(("parallel",)),
    )(h, oa, ob, oc, wout)


def _outproj_bwd(dh, oa, ob, oc, wout, tm):
    t = dh.shape[0]
    nt = t // tm

    def body(dh_ref, oa_ref, ob_ref, oc_ref, w_ref, da_ref, db_ref, dc_ref, dw_ref, acc_scr):
        i = pl.program_id(0)

        @pl.when(i == 0)
        def _():
            acc_scr[...] = jnp.zeros_like(acc_scr)

        d16 = dh_ref[...].astype(MXU)
        dym = _mm_nt(d16, w_ref[...])
        da_ref[...] = dym[:, :DA]
        db_ref[...] = dym[:, DA:DA + DB]
        dc_ref[...] = dym[:, DA + DB:]
        ym = jnp.concatenate([oa_ref[...], ob_ref[...], oc_ref[...]], axis=1)
        acc_scr[...] += _mm_tn(ym, d16)

        @pl.when(i == nt - 1)
        def _():
            dw_ref[...] = acc_scr[...].astype(dw_ref.dtype)

    return pl.pallas_call(
        body, name="outproj_bwd", grid=(nt,),
        in_specs=[pl.BlockSpec((tm, D), lambda i: (i, 0)),
                  pl.BlockSpec((tm, DA), lambda i: (i, 0)),
                  pl.BlockSpec((tm, DB), lambda i: (i, 0)),
                  pl.BlockSpec((tm, DC), lambda i: (i, 0)),
                  pl.BlockSpec((D, D), lambda i: (0, 0))],
        out_specs=[pl.BlockSpec((tm, DA), lambda i: (i, 0)),
                   pl.BlockSpec((tm, DB), lambda i: (i, 0)),
                   pl.BlockSpec((tm, DC), lambda i: (i, 0)),
                   pl.BlockSpec((D, D), lambda i: (0, 0))],
        out_shape=[jax.ShapeDtypeStruct((t, DA), F32), jax.ShapeDtypeStruct((t, DB), F32),
                   jax.ShapeDtypeStruct((t, DC), F32), jax.ShapeDtypeStruct((D, D), WIRE)],
        scratch_shapes=[pltpu.VMEM((D, D), F32)],
        compiler_params=_params(("arbitrary",)),
    )(dh, oa, ob, oc, wout)


def _lower_bounds(logits):
    depth, n = logits.shape

    def body(l_ref, lb_ref, p_ref):
        rows = [l_ref[l:l + 1, :] for l in range(depth)]
        mx = functools.reduce(jnp.maximum, rows)
        ex = [jnp.exp(r - mx) for r in rows]
        den = functools.reduce(lambda u, v: u + v, ex)
        acc = jnp.zeros_like(den)
        for l in range(depth):
            p = ex[l] / den
            p_ref[l:l + 1, :] = p
            if l > 0:
                acc = acc + p
            lb_ref[l:l + 1, :] = acc

    return pl.pallas_call(
        body, name="lower_bounds",
        out_shape=[jax.ShapeDtypeStruct((depth, n), F32), jax.ShapeDtypeStruct((depth, n), F32)],
    )(logits)


def _lower_bounds_bwd(p, dlb):
    depth, n = p.shape

    def body(p_ref, d_ref, out_ref):
        ps = [p_ref[l:l + 1, :] for l in range(depth)]
        ds = [d_ref[l:l + 1, :] for l in range(depth)]
        dp = [jnp.zeros_like(ps[0]) for _ in range(depth)]
        run = jnp.zeros_like(ps[0])
        for l in range(depth - 1, 0, -1):
            run = run + ds[l]
            dp[l] = run
        dot = functools.reduce(lambda u, v: u + v, [ps[l] * dp[l] for l in range(depth)])
        for l in range(depth):
            out_ref[l:l + 1, :] = ps[l] * (dp[l] - dot)

    return pl.pallas_call(body, name="lower_bounds_bwd", out_shape=jax.ShapeDtypeStruct((depth, n), F32))(p, dlb)


def _hgrn_block(z_ref, lb_ref, rb):
    q, fl = z_ref[:, 0:DA], z_ref[:, DA:2 * DA]
    lb = lb_ref[...]
    sq = _sigmoid(q)
    qs = q * sq
    sg = _sigmoid(fl)
    f = lb + (1.0 - lb) * sg
    k = 1.0 - f
    lf = jnp.log(f)
    row, col = _iota((rb, rb), 0), _iota((rb, rb), 1)
    same = (row // ACH) == (col // ACH)
    causal = same & (row >= col)
    b = _mm_exact_l(jnp.where(causal, 1.0, 0.0).astype(MXU), lf)
    bend = _mm_exact_l(jnp.where(same, 1.0, 0.0).astype(MXU), lf)
    r = 0.5 * bend
    eq, ek, eb, ed = jnp.exp(b - r), jnp.exp(r - b), jnp.exp(b), jnp.exp(bend - b)
    return dict(q=q, lb=lb, sq=sq, sg=sg, f=f, bend=bend, eq=eq, ek=ek, eb=eb, ed=ed,
                qt=qs * eq, kt=k * ek, qe=qs * eb, kd=k * ed, same=same, causal=causal)


def _hgrn_fwd(z, lb, gain):
    t = z.shape[0]
    nc = t // ACH
    cb = min(ACB, nc)
    rb = cb * ACH

    def body(z_ref, lb_ref, g_ref, o_ref, oa_ref, st_ref, st_scr):
        @pl.when(pl.program_id(0) == 0)
        def _():
            st_scr[...] = jnp.zeros_like(st_scr)

        c = _hgrn_block(z_ref, lb_ref, rb)
        for hd in range(NH):
            cols = slice(hd * HD, (hd + 1) * HD)
            v = z_ref[:, 2 * DA + hd * HD:2 * DA + (hd + 1) * HD]
            gg = z_ref[:, 3 * DA + hd * HD:3 * DA + (hd + 1) * HD]
            att = jnp.where(c["causal"], _mm_nt(c["qt"][:, cols], c["kt"][:, cols]), 0.0)
            o_in = _mm(att, v)
            qe, kd, bend = c["qe"][:, cols], c["kd"][:, cols], c["bend"][:, cols]
            st = st_scr[hd]
            outs = []
            for cc in range(cb):
                rows = slice(cc * ACH, (cc + 1) * ACH)
                st_ref[cc, hd] = st
                outs.append(o_in[rows] + _mm_nt(qe[rows], st))
                decay = jnp.exp(jnp.max(bend[rows], axis=0, keepdims=True))
                st = st * decay + _mm_tn(v[rows], kd[rows])
            st_scr[hd] = st
            o = jnp.concatenate(outs, axis=0)
            o_ref[:, cols] = o
            rstd = lax.rsqrt(jnp.mean(o * o, axis=-1, keepdims=True) + EPS)
            oa_ref[:, cols] = (o * rstd * g_ref[:, cols] * (gg * _sigmoid(gg))).astype(oa_ref.dtype)

    return pl.pallas_call(
        body, name="hgrn_fwd", grid=(nc // cb,),
        in_specs=[pl.BlockSpec((rb, 4 * DA), lambda c: (c, 0)),
                  pl.BlockSpec((1, DA), lambda c: (0, 0)),
                  pl.BlockSpec((1, DA), lambda c: (0, 0))],
        out_specs=[pl.BlockSpec((rb, DA), lambda c: (c, 0)),
                   pl.BlockSpec((rb, DA), lambda c: (c, 0)),
                   pl.BlockSpec((cb, NH, HD, HD), lambda c: (c, 0, 0, 0))],
        out_shape=[jax.ShapeDtypeStruct((t, DA), F32), jax.ShapeDtypeStruct((t, DA), SAVE),
                   jax.ShapeDtypeStruct((nc, NH, HD, HD), F32)],
        scratch_shapes=[pltpu.VMEM((NH, HD, HD), F32)],
        compiler_params=_params(("arbitrary",)),
    )(z, lb, gain)


def _hgrn_bwd(z, lb, gain, o, states, doa):
    t = z.shape[0]
    nc = t // ACH
    cb = min(ACB, nc)
    rb = cb * ACH
    nblk = nc // cb

    def body(z_ref, lb_ref, g_ref, o_ref, st_ref, doa_ref, dz_ref, dgain_ref, dlb_ref, dst_scr):
        @pl.when(pl.program_id(0) == 0)
        def _():
            dst_scr[...] = jnp.zeros_like(dst_scr)
            dgain_ref[...] = jnp.zeros_like(dgain_ref)
            dlb_ref[...] = jnp.zeros_like(dlb_ref)

        c = _hgrn_block(z_ref, lb_ref, rb)
        dbs, dqss, dks = [], [], []
        for hd in range(NH):
            cols = slice(hd * HD, (hd + 1) * HD)
            v = z_ref[:, 2 * DA + hd * HD:2 * DA + (hd + 1) * HD]
            gg = z_ref[:, 3 * DA + hd * HD:3 * DA + (hd + 1) * HD]
            qt, kt, qe, kd, bend = (c[n][:, cols] for n in ("qt", "kt", "qe", "kd", "bend"))
            o = o_ref[:, cols]
            do_a = doa_ref[:, cols]
            gain = g_ref[:, cols]
            sgg = _sigmoid(gg)
            silu_g = gg * sgg
            rstd = lax.rsqrt(jnp.mean(o * o, axis=-1, keepdims=True) + EPS)
            n = o * rstd
            dn = do_a * gain * silu_g
            dg = do_a * n * gain * (sgg * (1.0 + gg * (1.0 - sgg)))
            dgain_ref[:, cols] += jnp.sum(do_a * silu_g * n, axis=0, keepdims=True)
            d_o = rstd * (dn - n * jnp.mean(dn * n, axis=-1, keepdims=True))

            att = jnp.where(c["causal"], _mm_nt(qt, kt), 0.0)
            datt = jnp.where(c["causal"], _mm_nt(d_o, v), 0.0)
            dv_in = _mm_tn(att, d_o)
            dqt = _mm(datt, kt)
            dkt = _mm_tn(datt, qt)
            dsp = dst_scr[hd]
            dvs, dqes, dkds, dbends = [None] * cb, [None] * cb, [None] * cb, [None] * cb
            for cc in reversed(range(cb)):
                rows = slice(cc * ACH, (cc + 1) * ACH)
                st = st_ref[cc, hd]
                dvs[cc] = dv_in[rows] + _mm_nt(kd[rows], dsp)
                dqes[cc] = _mm(d_o[rows], st)
                dkds[cc] = _mm(v[rows], dsp)
                decay = jnp.exp(jnp.max(bend[rows], axis=0, keepdims=True))
                dbend = (decay * jnp.sum(st * dsp, axis=0, keepdims=True)
                         + jnp.sum(dkds[cc] * kd[rows], axis=0, keepdims=True))
                dbends[cc] = jnp.broadcast_to(dbend, (ACH, HD))
                dsp = dsp * decay + _mm_tn(d_o[rows], qe[rows])
            dst_scr[hd] = dsp
            dv, dqe, dkd, dbend = (jnp.concatenate(p, axis=0) for p in (dvs, dqes, dkds, dbends))
            dbs.append((dqt * qt + dqe * qe - dkt * kt - dkd * kd, dbend))
            dqss.append(dqt * c["eq"][:, cols] + dqe * c["eb"][:, cols])
            dks.append(dkt * c["ek"][:, cols] + dkd * c["ed"][:, cols])
            c0 = hd * HD
            dz_ref[:, 2 * DA + c0:2 * DA + c0 + HD] = dv.astype(dz_ref.dtype)
            dz_ref[:, 3 * DA + c0:3 * DA + c0 + HD] = dg.astype(dz_ref.dtype)

        db = jnp.concatenate([p[0] for p in dbs], axis=1)
        dbend = jnp.concatenate([p[1] for p in dbs], axis=1)
        dqs, dk = jnp.concatenate(dqss, axis=1), jnp.concatenate(dks, axis=1)
        row, col = _iota((rb, rb), 0), _iota((rb, rb), 1)
        upper = jnp.where(c["same"] & (row <= col), 1.0, 0.0).astype(MXU)
        dlf = _mm_exact_l(upper, db) + dbend
        df = dlf / c["f"] - dk
        sg, sq, q = c["sg"], c["sq"], c["q"]
        dlb_ref[...] += jnp.sum(df * (1.0 - sg), axis=0, keepdims=True)
        dz_ref[:, DA:2 * DA] = (df * (1.0 - c["lb"]) * sg * (1.0 - sg)).astype(dz_ref.dtype)
        dz_ref[:, 0:DA] = (dqs * (sq * (1.0 + q * (1.0 - sq)))).astype(dz_ref.dtype)

    rev = lambda c: (nblk - 1 - c, 0)
    return pl.pallas_call(
        body, name="hgrn_bwd", grid=(nblk,),
        in_specs=[pl.BlockSpec((rb, 4 * DA), rev),
                  pl.BlockSpec((1, DA), lambda c: (0, 0)),
                  pl.BlockSpec((1, DA), lambda c: (0, 0)),
                  pl.BlockSpec((rb, DA), rev),
                  pl.BlockSpec((cb, NH, HD, HD), lambda c: (nblk - 1 - c, 0, 0, 0)),
                  pl.BlockSpec((rb, DA), rev)],
        out_specs=[pl.BlockSpec((rb, 4 * DA), rev),
                   pl.BlockSpec((1, DA), lambda c: (0, 0)),
                   pl.BlockSpec((1, DA), lambda c: (0, 0))],
        out_shape=[jax.ShapeDtypeStruct((t, DIN), SAVE), jax.ShapeDtypeStruct((1, DA), F32),
                   jax.ShapeDtypeStruct((1, DA), F32)],
        scratch_shapes=[pltpu.VMEM((NH, HD, HD), F32)],
        compiler_params=_params(("arbitrary",)),
    )(z, lb, gain, o, states, doa)


def _shift_down(prev8, x, k):
    cat = jnp.concatenate([prev8, x], axis=0)
    return pltpu.roll(cat, k, axis=0)[8:, :]


def _shift_up(x, next8, k):
    n = x.shape[0]
    cat = jnp.concatenate([x, next8], axis=0)
    return pltpu.roll(cat, n + 8 - k, axis=0)[:n, :]


def _lru_gates(x, prev8, cw_ref, vec_ref, wa_ref, wx_ref):
    xs = [x, _shift_down(prev8, x, 1), _shift_down(prev8, x, 2), _shift_down(prev8, x, 3)]
    xc = vec_ref[0:1, :] + cw_ref[3:4, :] * xs[0] + cw_ref[2:3, :] * xs[1] + cw_ref[1:2, :] * xs[2] + cw_ref[0:1, :] * xs[3]
    r = _sigmoid(_mm(xc, wa_ref[...]) + vec_ref[1:2, :])
    gi = _sigmoid(_mm(xc, wx_ref[...]) + vec_ref[2:3, :])
    lam = vec_ref[3:4, :]
    sp = jnp.maximum(-lam, 0.0) + jnp.log(1.0 + jnp.exp(-jnp.abs(lam)))
    la = -LRU_C * r * sp
    a = jnp.exp(la)
    mult = jnp.sqrt(-_expm1(2.0 * la))
    return xs, xc, r, gi, sp, a, mult


def _scan_down(a, u):
    n = a.shape[0]
    row = _iota(a.shape, 0)
    s = 1
    while s < n:
        keep = row >= s
        ash = jnp.where(keep, pltpu.roll(a, s, axis=0), 1.0)
        ush = jnp.where(keep, pltpu.roll(u, s, axis=0), 0.0)
        u = a * ush + u
        a = a * ash
        s *= 2
    return a, u


def _scan_up(a, u):
    n = a.shape[0]
    row = _iota(a.shape, 0)
    s = 1
    while s < n:
        keep = row < n - s
        ash = jnp.where(keep, pltpu.roll(a, n - s, axis=0), 1.0)
        ush = jnp.where(keep, pltpu.roll(u, n - s, axis=0), 0.0)
        u = a * ush + u
        a = a * ash
        s *= 2
    return a, u


def _lru_fwd(z, cw, vec, wa, wx, tb):
    t = z.shape[0]
    xcol, gcol = (4 * DA) // DB, (4 * DA) // DB + 1

    def body(x_ref, gate_ref, cw_ref, vec_ref, wa_ref, wx_ref, ob_ref, h_ref, xprev_scr, hc_scr):
        @pl.when(pl.program_id(0) == 0)
        def _():
            xprev_scr[...] = jnp.zeros_like(xprev_scr)
            hc_scr[...] = jnp.zeros_like(hc_scr)

        x = x_ref[...]
        _, xc, _, gi, _, a, mult = _lru_gates(x, xprev_scr[...], cw_ref, vec_ref, wa_ref, wx_ref)
        acum, hloc = _scan_down(a, mult * gi * xc)
        h = hloc + acum * hc_scr[0:1, :]
        h_ref[...] = h
        hc_scr[...] = jnp.broadcast_to(_row(h, tb - 1), hc_scr.shape)
        xprev_scr[...] = x[tb - 8:, :]
        y = h * _gelu(gate_ref[...])
        ms = _mm_exact_r(y * y, _group_matrix(DB, 1.0 / GRP).astype(MXU))
        ob_ref[...] = (y * lax.rsqrt(ms + EPS) * vec_ref[4:5, :]).astype(ob_ref.dtype)

    return pl.pallas_call(
        body, name="lru_fwd", grid=(t // tb,),
        in_specs=[pl.BlockSpec((tb, DB), lambda i: (i, xcol)),
                  pl.BlockSpec((tb, DB), lambda i: (i, gcol)),
                  pl.BlockSpec((8, DB), lambda i: (0, 0)),
                  pl.BlockSpec((8, DB), lambda i: (0, 0)),
                  pl.BlockSpec((DB, DB), lambda i: (0, 0)),
                  pl.BlockSpec((DB, DB), lambda i: (0, 0))],
        out_specs=[pl.BlockSpec((tb, DB), lambda i: (i, 0)),
                   pl.BlockSpec((tb, DB), lambda i: (i, 0))],
        out_shape=[jax.ShapeDtypeStruct((t, DB), SAVE), jax.ShapeDtypeStruct((t, DB), F32)],
        scratch_shapes=[pltpu.VMEM((8, DB), F32), pltpu.VMEM((8, DB), F32)],
        compiler_params=_params(("arbitrary",)),
    )(z, z, cw, vec, wa, wx)


def _lru_bwd(z, hseq, dob, cw, vec, wa, wx, dz, tb):
    t = z.shape[0]
    nb = t // tb
    xcol, gcol = (4 * DA) // DB, (4 * DA) // DB + 1
    per = tb // 8

    def body(x_ref, xh_ref, gate_ref, h_ref, hh_ref, dob_ref, cw_ref, vec_ref, wa_ref, wx_ref, _,
             dz_ref, dcw_ref, dvec_ref, dwa_ref, dwx_ref, gc_scr, an_scr, dxc_scr):
        step = pl.program_id(0)
        blk = nb - 1 - step

        @pl.when(step == 0)
        def _():
            for ref in (gc_scr, an_scr, dxc_scr, dcw_ref, dvec_ref, dwa_ref, dwx_ref):
                ref[...] = jnp.zeros_like(ref)

        first = (blk > 0).astype(F32)
        x = x_ref[...]
        xs, xc, r, gi, sp, a, mult = _lru_gates(x, xh_ref[...] * first, cw_ref, vec_ref, wa_ref, wx_ref)
        h = h_ref[...]
        hprev = _shift_down(hh_ref[...] * first, h, 1)
        ge, dge = _gelu_and_grad(gate_ref[...])
        y = h * ge
        gmat = _group_matrix(DB, 1.0 / GRP).astype(MXU)
        rstd = lax.rsqrt(_mm_exact_r(y * y, gmat) + EPS)
        n = y * rstd
        d_ob = dob_ref[...]
        dn = d_ob * vec_ref[4:5, :]
        dvec_ref[4:5, :] += jnp.sum(d_ob * n, axis=0, keepdims=True)
        dy = rstd * (dn - n * _mm_exact_r(dn * n, gmat))
        dh = dy * ge
        dgate = dy * h * dge

        row = _iota(a.shape, 0)
        anext = jnp.where(row == tb - 1, an_scr[0:1, :], pltpu.roll(a, tb - 1, axis=0))
        acum, gloc = _scan_up(anext, dh)
        g = gloc + acum * gc_scr[0:1, :]
        gc_scr[...] = jnp.broadcast_to(_row(g, 0), gc_scr.shape)
        an_scr[...] = jnp.broadcast_to(_row(a, 0), an_scr.shape)

        da = g * hprev
        dmult = g * gi * xc
        dgi = g * mult * xc
        dxc = g * mult * gi
        dla = da * a - dmult * (a * a) / mult
        dr = dla * (-LRU_C * sp)
        dsp = jnp.sum(dla * (-LRU_C * r), axis=0, keepdims=True)
        lam = vec_ref[3:4, :]
        dvec_ref[3:4, :] += -dsp * _sigmoid(-lam)
        dpa = dr * r * (1.0 - r)
        dpx = dgi * gi * (1.0 - gi)
        dwa_ref[...] += _mm_tn(xc, dpa)
        dwx_ref[...] += _mm_tn(xc, dpx)
        dvec_ref[1:2, :] += jnp.sum(dpa, axis=0, keepdims=True)
        dvec_ref[2:3, :] += jnp.sum(dpx, axis=0, keepdims=True)
        dxc = dxc + _mm_nt(dpa, wa_ref[...]) + _mm_nt(dpx, wx_ref[...])
        dvec_ref[0:1, :] += jnp.sum(dxc, axis=0, keepdims=True)
        for tap in range(4):
            dcw_ref[tap:tap + 1, :] += jnp.sum(dxc * xs[3 - tap], axis=0, keepdims=True)
        nxt = dxc_scr[...]
        dx = (cw_ref[3:4, :] * dxc + cw_ref[2:3, :] * _shift_up(dxc, nxt, 1)
              + cw_ref[1:2, :] * _shift_up(dxc, nxt, 2) + cw_ref[0:1, :] * _shift_up(dxc, nxt, 3))
        dxc_scr[...] = dxc[:8, :]
        dz_ref[:, :DB] = dx.astype(dz_ref.dtype)
        dz_ref[:, DB:] = dgate.astype(dz_ref.dtype)

    def halo(col):
        return lambda s: (jnp.maximum((nb - 1 - s) * per - 1, 0), col)

    const = lambda s: (0, 0)
    return pl.pallas_call(
        body, name="lru_bwd", grid=(nb,),
        in_specs=[pl.BlockSpec((tb, DB), lambda s: (nb - 1 - s, xcol)),
                  pl.BlockSpec((8, DB), halo(xcol)),
                  pl.BlockSpec((tb, DB), lambda s: (nb - 1 - s, gcol)),
                  pl.BlockSpec((tb, DB), lambda s: (nb - 1 - s, 0)),
                  pl.BlockSpec((8, DB), halo(0)),
                  pl.BlockSpec((tb, DB), lambda s: (nb - 1 - s, 0)),
                  pl.BlockSpec((8, DB), const), pl.BlockSpec((8, DB), const),
                  pl.BlockSpec((DB, DB), const), pl.BlockSpec((DB, DB), const),
                  pl.BlockSpec(memory_space=pl.ANY)],
        out_specs=[pl.BlockSpec((tb, 2 * DB), lambda s: (nb - 1 - s, (4 * DA) // (2 * DB))),
                   pl.BlockSpec((8, DB), const), pl.BlockSpec((8, DB), const),
                   pl.BlockSpec((DB, DB), const), pl.BlockSpec((DB, DB), const)],
        out_shape=[jax.ShapeDtypeStruct((t, DIN), SAVE), jax.ShapeDtypeStruct((8, DB), F32),
                   jax.ShapeDtypeStruct((8, DB), F32), jax.ShapeDtypeStruct((DB, DB), F32),
                   jax.ShapeDtypeStruct((DB, DB), F32)],
        scratch_shapes=[pltpu.VMEM((8, DB), F32), pltpu.VMEM((8, DB), F32), pltpu.VMEM((8, DB), F32)],
        input_output_aliases={10: 0},
        compiler_params=_params(("arbitrary",)),
    )(z, z, z, hseq, hseq, dob, cw, vec, wa, wx, dz)


def _sgu_block(u_ref, v_ref, w_ref, b_ref, gmat, tb):
    uu, duu = _gelu_and_grad(u_ref[...])
    vv, dvv = _gelu_and_grad(v_ref[...])
    dlt = vv - _mm_exact_r(vv, gmat)
    rstd_v = lax.rsqrt(_mm_exact_r(dlt * dlt, gmat) + EPS)
    vn = dlt * rstd_v
    col = _iota((CCH, DC), 1) // GRP
    causal = _iota((CCH, CCH), 0) >= _iota((CCH, CCH), 1)
    ws = [jnp.where(causal, w_ref[g], 0.0) for g in range(DC // GRP)]
    zs = []
    for ch in range(tb // CCH):
        vn_c = vn[ch * CCH:(ch + 1) * CCH]
        zz = b_ref[...]
        for g, w in enumerate(ws):
            zz = zz + jnp.where(col == g, _mm(w, vn_c), 0.0)
        zs.append(zz)
    return uu, duu, dvv, rstd_v, vn, jnp.concatenate(zs, axis=0), ws, col, causal


def _sgu_fwd(z, w, bias, gain, tb):
    t = z.shape[0]
    ucol, vcol = (4 * DA + 2 * DB) // DC, (4 * DA + 2 * DB) // DC + 1

    def body(u_ref, v_ref, w_ref, b_ref, g_ref, oc_ref):
        gmat = _group_matrix(DC, 1.0 / GRP).astype(MXU)
        uu, _, _, _, _, zz, _, _, _ = _sgu_block(u_ref, v_ref, w_ref, b_ref, gmat, tb)
        y = uu * zz
        ms = _mm_exact_r(y * y, gmat)
        oc_ref[...] = (y * lax.rsqrt(ms + EPS) * g_ref[...]).astype(oc_ref.dtype)

    const = lambda i: (0, 0)
    return pl.pallas_call(
        body, name="sgu_fwd", grid=(t // tb,),
        in_specs=[pl.BlockSpec((tb, DC), lambda i: (i, ucol)),
                  pl.BlockSpec((tb, DC), lambda i: (i, vcol)),
                  pl.BlockSpec((DC // GRP, CCH, CCH), lambda i: (0, 0, 0)),
                  pl.BlockSpec((CCH, DC), const), pl.BlockSpec((1, DC), const)],
        out_specs=pl.BlockSpec((tb, DC), lambda i: (i, 0)),
        out_shape=jax.ShapeDtypeStruct((t, DC), SAVE),
        compiler_params=_params(("parallel",)),
    )(z, z, w, bias, gain)


def _sgu_bwd(z, doc, w, bias, gain, dz, tb):
    t = z.shape[0]
    nb = t // tb
    ucol, vcol = (4 * DA + 2 * DB) // DC, (4 * DA + 2 * DB) // DC + 1
    ng = DC // GRP

    def body(u_ref, v_ref, doc_ref, w_ref, b_ref, g_ref, _, dz_ref, dw_ref, dbias_ref, dgain_ref, dbsum_scr):
        i = pl.program_id(0)

        @pl.when(i == 0)
        def _():
            for ref in (dw_ref, dgain_ref, dbsum_scr):
                ref[...] = jnp.zeros_like(ref)

        gmat = _group_matrix(DC, 1.0 / GRP).astype(MXU)
        uu, duu, dvv, rstd_v, vn, zz, ws, col, causal = _sgu_block(u_ref, v_ref, w_ref, b_ref, gmat, tb)
        y = uu * zz
        rstd = lax.rsqrt(_mm_exact_r(y * y, gmat) + EPS)
        n = y * rstd
        d_oc = doc_ref[...]
        dn = d_oc * g_ref[...]
        dgain_ref[0:1, :] += jnp.sum(d_oc * n, axis=0, keepdims=True)
        dy = rstd * (dn - n * _mm_exact_r(dn * n, gmat))
        dzz = dy * uu
        dz_ref[:, :DC] = (dy * zz * duu).astype(dz_ref.dtype)
        dvns = []
        for ch in range(tb // CCH):
            rows = slice(ch * CCH, (ch + 1) * CCH)
            dzz_c, vn_c = dzz[rows], vn[rows]
            dbsum_scr[...] += dzz_c
            dvn = jnp.zeros_like(dzz_c)
            for g in range(ng):
                sel = col == g
                dvn = dvn + jnp.where(sel, _mm_tn(ws[g], dzz_c), 0.0)
                dw_ref[g] += jnp.where(causal, _mm_nt(jnp.where(sel, dzz_c, 0.0), vn_c), 0.0)
            dvns.append(dvn)
        dvn = jnp.concatenate(dvns, axis=0)
        dv = rstd_v * (dvn - _mm_exact_r(dvn, gmat) - vn * _mm_exact_r(dvn * vn, gmat))
        dz_ref[:, DC:] = (dv * dvv).astype(dz_ref.dtype)

        @pl.when(i == nb - 1)
        def _():
            dbias_ref[...] = _mm_exact_r(dbsum_scr[...], _group_matrix(DC, 1.0).astype(MXU))

    const = lambda i: (0, 0)
    return pl.pallas_call(
        body, name="sgu_bwd", grid=(nb,),
        in_specs=[pl.BlockSpec((tb, DC), lambda i: (i, ucol)),
                  pl.BlockSpec((tb, DC), lambda i: (i, vcol)),
                  pl.BlockSpec((tb, DC), lambda i: (i, 0)),
                  pl.BlockSpec((ng, CCH, CCH), lambda i: (0, 0, 0)),
                  pl.BlockSpec((CCH, DC), const), pl.BlockSpec((1, DC), const),
                  pl.BlockSpec(memory_space=pl.ANY)],
        out_specs=[pl.BlockSpec((tb, 2 * DC), lambda i: (i, (4 * DA + 2 * DB) // (2 * DC))),
                   pl.BlockSpec((ng, CCH, CCH), lambda i: (0, 0, 0)),
                   pl.BlockSpec((CCH, DC), const), pl.BlockSpec((8, DC), const)],
        out_shape=[jax.ShapeDtypeStruct((t, DIN), SAVE), jax.ShapeDtypeStruct((ng, CCH, CCH), F32),
                   jax.ShapeDtypeStruct((CCH, DC), F32), jax.ShapeDtypeStruct((8, DC), F32)],
        scratch_shapes=[pltpu.VMEM((CCH, DC), F32)],
        input_output_aliases={6: 0},
        compiler_params=_params(("arbitrary",)),
    )(z, z, doc, w, bias, gain, dz)


def _head(h, gain, target, tm):
    t = h.shape[0]

    def body(h_ref, g_ref, t_ref, dh_ref, loss_ref, dgain_ref):
        @pl.when(pl.program_id(0) == 0)
        def _():
            loss_ref[...] = jnp.zeros_like(loss_ref)
            dgain_ref[...] = jnp.zeros_like(dgain_ref)

        hh = h_ref[...]
        gain = g_ref[...]
        rstd = lax.rsqrt(jnp.mean(hh * hh, axis=-1, keepdims=True) + EPS)
        xhat = hh * rstd
        err = xhat * gain - t_ref[...]
        per_tok = jnp.mean(err * err, axis=-1, keepdims=True)
        loss_ref[...] += 0.5 * jnp.sum(per_tok, axis=0, keepdims=True)
        dy = err * (1.0 / D)
        dgain_ref[...] += jnp.sum(dy * xhat, axis=0, keepdims=True)
        dxh = dy * gain
        dh_ref[...] = rstd * (dxh - xhat * jnp.mean(dxh * xhat, axis=-1, keepdims=True))

    return pl.pallas_call(
        body, name="head", grid=(t // tm,),
        in_specs=[pl.BlockSpec((tm, D), lambda i: (i, 0)),
                  pl.BlockSpec((1, D), lambda i: (0, 0)),
                  pl.BlockSpec((tm, D), lambda i: (i, 0))],
        out_specs=[pl.BlockSpec((tm, D), lambda i: (i, 0)),
                   pl.BlockSpec((1, 128), lambda i: (0, 0)),
                   pl.BlockSpec((1, D), lambda i: (0, 0))],
        out_shape=[jax.ShapeDtypeStruct((t, D), F32), jax.ShapeDtypeStruct((1, 128), F32),
                   jax.ShapeDtypeStruct((1, D), F32)],
        compiler_params=_params(("arbitrary",)),
    )(h, gain, target)


def _adamw(w, g, m, v):
    m = ADAM_B1 * m + (1.0 - ADAM_B1) * g
    v = ADAM_B2 * v + (1.0 - ADAM_B2) * (g * g)
    m_hat = m / (1.0 - ADAM_B1 ** ADAM_STEP)
    v_hat = v / (1.0 - ADAM_B2 ** ADAM_STEP)
    delta = -ADAM_LR * (m_hat / (jnp.sqrt(v_hat) + ADAM_EPS) + ADAM_WD * w)
    return delta, m, v


def _adamw_big(recv, w, m, v, tr, name, after, transposed=False):
    depth, rows, cols = w.shape
    rspec = (pl.BlockSpec((NDEV, cols, tr), lambda i: (0, 0, i)) if transposed
             else pl.BlockSpec((NDEV, tr, cols), lambda i: (0, i, 0)))

    def body(*refs):
        r_refs = refs[:depth]
        w_ref, m_ref, v_ref, _, g_out, d_out, m_out, v_out = refs[depth:]
        for l in range(depth):
            g = r_refs[l][0].astype(F32)
            for k in range(1, NDEV):
                g = g + r_refs[l][k].astype(F32)
            if transposed:
                g = g.T
            delta, m_, v_ = _adamw(w_ref[l], g, m_ref[l], v_ref[l])
            g_out[l] = g
            d_out[l] = delta
            m_out[l] = m_
            v_out[l] = v_

    spec = pl.BlockSpec((depth, tr, cols), lambda i: (0, i, 0))
    return pl.pallas_call(
        body, name=name, grid=(rows // tr,),
        in_specs=[rspec] * depth + [spec] * 3
        + [pl.BlockSpec(memory_space=pl.ANY)],
        out_specs=[spec] * 4, out_shape=[jax.ShapeDtypeStruct((depth, rows, cols), F32)] * 4,
        compiler_params=_params(("parallel",)),
    )(*recv, w, m, v, after)


def _sum_devices(recv):
    _, r, _ = recv.shape

    def body(r_ref, out_ref):
        g = r_ref[0]
        for k in range(1, NDEV):
            g = g + r_ref[k]
        out_ref[...] = g

    return pl.pallas_call(body, name="sum_devices", out_shape=jax.ShapeDtypeStruct((r, 128), F32))(recv)


def _adamw_small(w, g, m, v):
    def body(w_ref, g_ref, m_ref, v_ref, d_out, m_out, v_out):
        delta, m_, v_ = _adamw(w_ref[...], g_ref[...], m_ref[...], v_ref[...])
        d_out[...] = delta
        m_out[...] = m_
        v_out[...] = v_

    return pl.pallas_call(body, name="adamw_small", out_shape=[jax.ShapeDtypeStruct(w.shape, F32)] * 3)(w, g, m, v)


def _pack(arrs):
    flat = jnp.concatenate([a.reshape(-1) for a in arrs])
    pad = (-flat.shape[0]) % 1024
    return jnp.pad(flat, (0, pad)).reshape(-1, 128)


def _unpack(buf, like):
    flat = buf.reshape(-1)
    out, off = [], 0
    for a in like:
        out.append(flat[off:off + a.size].reshape(a.shape))
        off += a.size
    return out


def _block_diag(w):
    nb, bd, _ = w.shape
    eye = jnp.eye(nb, dtype=w.dtype)
    return (eye[:, None, :, None] * w[:, :, None, :]).reshape(nb * bd, nb * bd)


def _diag_blocks(w):
    nb = w.shape[0] // GRP
    return jnp.stack([w[g * GRP:(g + 1) * GRP, g * GRP:(g + 1) * GRP] for g in range(nb)])


SMALL = ['ffn1_norm', 'mix_norm', 'hgrn_lb_logits', 'hgrn_norm', 'conv_b', 'lru_wa', 'lru_ba', 'lru_wx', 'lru_bx',
         'lru_lambda', 'lru_norm', 'sgu_w', 'sgu_b', 'sgu_norm', 'ffn2_norm', 'final_norm']
NAMES = ['ffn1_norm', 'ffn1_wg', 'ffn1_wu', 'ffn1_wd', 'mix_norm', 'w_in', 'hgrn_lb_logits', 'hgrn_norm', 'conv_w',
         'conv_b', 'lru_wa', 'lru_ba', 'lru_wx', 'lru_bx', 'lru_lambda', 'lru_norm', 'sgu_w', 'sgu_b', 'sgu_norm',
         'w_out', 'ffn2_norm', 'ffn2_wg', 'ffn2_wu', 'ffn2_wd', 'final_norm']


def _step(x, target, w, m, v):
    depth = w['ffn1_wg'].shape[0]
    t = x.shape[1]
    h = x.reshape(t, D)
    target = target.reshape(t, D)
    tm_f, tm_b, tb = min(TM_F, t), min(TM_B, t), min(TB, t)
    my = 4 * lax.axis_index("x") + 2 * lax.axis_index("y") + lax.axis_index("c")

    cw_tile = jnp.pad(w['conv_w'].reshape(-1, 128), ((0, 8 - depth), (0, 0)))
    lbs, lb_soft = _lower_bounds(w['hgrn_lb_logits'])

    def row(a):
        return a.reshape(1, -1)

    def tr(a):
        return jnp.swapaxes(a, -1, -2)

    def shards(l, unit):
        if unit == 1:
            return [tr(w['w_in'][l]).astype(WIRE), w['w_out'][l].astype(WIRE)]
        f = 'ffn1' if unit == 0 else 'ffn2'
        return [tr(w[f + '_wg'][l]).astype(WIRE), tr(w[f + '_wu'][l]).astype(WIRE), w[f + '_wd'][l].astype(WIRE)]

    units = [(l, u) for l in range(depth) for u in range(3)]

    def start_ici(idx, deps=()):
        return _transfer_start(shards(*units[idx]), True, "gather_ici_%d_%d" % units[idx], deps=deps)

    def start_d2d(idx, handle, after):
        lands = _transfer_wait(handle, after, "gather_ici_wait_%d_%d" % units[idx])
        return _forward_start(lands, "gather_d2d_%d_%d" % units[idx])

    pipe = dict(idx=0)
    first = start_ici(0)
    pipe['ici'] = start_ici(1, deps=(first['token'],))
    conv_flight = _transfer_start([cw_tile], True, "gather_conv_start", deps=(pipe['ici']['token'],), direct=True)
    pipe['d2d'] = start_d2d(0, first, conv_flight['token'])

    def next_weights(after):
        idx = pipe['idx']
        lands = _transfer_wait(pipe['d2d'], after, "gather_d2d_wait_%d_%d" % units[idx])
        pipe['idx'] = idx + 1
        tok = 0.0
        if idx + 1 < len(units):
            pipe['d2d'] = start_d2d(idx + 1, pipe['ici'], lands[-1])
            tok = pipe['d2d']['token'][0, 0]
            if idx + 2 < len(units):
                pipe['ici'] = start_ici(idx + 2, deps=(pipe['d2d']['token'],))
                tok = pipe['ici']['token'][0, 0]
        return lands, tok

    saved = []
    for l in range(depth):
        lands, tok = next_weights(h)
        s = dict(ffn1=[a.reshape(FF, D) for a in lands], h0=h)
        h, s['xn1'], s['a1'], s['b1'] = _ffn_fwd(h, row(w['ffn1_norm'][l]) + tok, *s['ffn1'], tm_f)
        s['h1'] = h
        (win, wout), tok = next_weights(h)
        win, wout = win.reshape(DIN, D), wout.reshape(D, D)
        s['win'], s['wout'] = win, wout
        z, s['xnm'] = _inproj_fwd(h, row(w['mix_norm'][l]) + tok, win, tm_f)
        s['z'] = z
        s['o'], oa, s['states'] = _hgrn_fwd(z, row(lbs[l]), row(w['hgrn_norm'][l]))
        if l == 0:
            cw_all = _transfer_wait(conv_flight, z, "gather_conv_wait")[0][:, :depth]
            conv_w = jnp.moveaxis(cw_all.reshape(NDEV, depth, 4, DB // NDEV), 0, 2).reshape(depth, 4, DB)
        s['cw'] = jnp.pad(conv_w[l], ((0, 4), (0, 0)))
        s['vec'] = jnp.concatenate([row(w['conv_b'][l]), row(w['lru_ba'][l]), row(w['lru_bx'][l]),
                                    row(w['lru_lambda'][l]), row(w['lru_norm'][l]), jnp.zeros((3, DB), F32)])
        s['wa'], s['wx'] = _block_diag(w['lru_wa'][l]), _block_diag(w['lru_wx'][l])
        ob, s['hseq'] = _lru_fwd(z, s['cw'], s['vec'], s['wa'], s['wx'], tb)
        s['bias'] = jnp.repeat(w['sgu_b'][l].T, GRP, axis=1)
        oc = _sgu_fwd(z, w['sgu_w'][l], s['bias'], row(w['sgu_norm'][l]), tb)
        s['oa'], s['ob'], s['oc'] = oa, ob, oc
        h = _outproj_fwd(h, oa, ob, oc, wout, tm_f)
        s['h2'] = h
        lands, tok = next_weights(h)
        s['ffn2'] = [a.reshape(FF, D) for a in lands]
        h, s['xn2'], s['a2'], s['b2'] = _ffn_fwd(h, row(w['ffn2_norm'][l]) + tok, *s['ffn2'], tm_f)
        saved.append(s)

    dh, loss_part, g_final = _head(h, row(w['final_norm']), target, tm_f)
    loss = lax.psum(loss_part[0, 0], ("x", "y", "c"))

    recv = {k: [None] * depth for k in ('wg1', 'wu1', 'wd1', 'wg2', 'wu2', 'wd2', 'win', 'wout')}
    flight = []

    def land(after):
        handle, kinds, l = flight.pop()
        for k, a in zip(kinds, _transfer_wait(handle, after, f"exchange_wait_{kinds[0]}_{l}")):
            recv[k][l] = a

    def exchange(arrs, kinds, l, deps=()):
        handle = _transfer_start(arrs, False, f"exchange_start_{kinds[0]}_{l}", deps=deps)
        if flight:
            land(handle['token'])
        flight.append((handle, kinds, l))
        return handle['token'][0, 0]

    small = {k: [None] * depth for k in SMALL if k != 'final_norm'}
    dconv = [None] * depth
    dlb = [None] * depth
    tok = 0.0
    for l in reversed(range(depth)):
        s = saved[l]
        dh, g, *cot = _ffn_bwd_x(dh, s['h2'], row(w['ffn2_norm'][l]) + tok, s['a2'], s['b2'], *s['ffn2'], tm_b)
        dws = _ffn_bwd_w(s['xn2'], *cot, tm_b)
        tok = exchange([a.reshape(NDEV, FFS, D) for a in dws], ('wg2', 'wu2', 'wd2'), l)
        small['ffn2_norm'][l] = g
        doa, dob, doc, dwout = _outproj_bwd(dh, s['oa'], s['ob'], s['oc'], s['wout'], tm_f)
        dz, g_hn, dlb[l] = _hgrn_bwd(s['z'], row(lbs[l]), row(w['hgrn_norm'][l]) + tok, s['o'], s['states'], doa)
        small['hgrn_norm'][l] = g_hn
        dz, dcw, dvec, dwa, dwx = _lru_bwd(s['z'], s['hseq'], dob, s['cw'], s['vec'], s['wa'], s['wx'], dz, tb)
        dconv[l] = dcw[:4]
        small['conv_b'][l], small['lru_ba'][l], small['lru_bx'][l] = dvec[0], dvec[1].reshape(4, GRP), dvec[2].reshape(4, GRP)
        small['lru_lambda'][l], small['lru_norm'][l] = dvec[3], dvec[4]
        small['lru_wa'][l], small['lru_wx'][l] = _diag_blocks(dwa), _diag_blocks(dwx)
        dz, dsw, dbias, dgc = _sgu_bwd(s['z'], doc, w['sgu_w'][l], s['bias'], row(w['sgu_norm'][l]), dz, tb)
        small['sgu_w'][l], small['sgu_b'][l], small['sgu_norm'][l] = dsw, dbias[:, ::GRP].T, dgc[0]
        dwin = _inproj_bwd_w(s['xnm'], dz, tm_f)
        tok = exchange([dwin.reshape(NDEV, DINS, D), dwout.reshape(NDEV, D // NDEV, D)], ('win', 'wout'), l)
        dh, g = _inproj_bwd_x(dh, dz, s['h1'], row(w['mix_norm'][l]) + tok, s['win'], tm_b)
        small['mix_norm'][l] = g
        tok = 0.0
        if l == 0:
            small['ffn1_norm'][0] = jnp.zeros((1, D), F32)
            small['hgrn_lb_logits'] = list(_lower_bounds_bwd(lb_soft, jnp.concatenate(dlb, axis=0)))
            parts = [jnp.stack([small[k][j].reshape(w[k].shape[1:]) for j in range(depth)])
                     for k in SMALL if k != 'final_norm']
            parts += [g_final.reshape(D), jnp.stack(dconv)]
            small_flight = _transfer_start([_pack(parts)], True, "gather_small_start", direct=True)
            tok = small_flight['token'][0, 0]
        dh, g, *cot = _ffn_bwd_x(dh, s['h0'], row(w['ffn1_norm'][l]) + tok, s['a1'], s['b1'], *s['ffn1'], tm_b)
        dws = _ffn_bwd_w(s['xn1'], *cot, tm_b)
        before = ()
        if l == 0:
            g_last = _all_gather([g.reshape(8, 128)], "gather_last")[0]
            before = (g_last,)
        else:
            small['ffn1_norm'][l] = g
        tok = exchange([a.reshape(NDEV, FFS, D) for a in dws], ('wg1', 'wu1', 'wd1'), l, before)
    grad_x = dh.reshape(1, t, D)

    out = {}
    last = flight[0][0]['token']

    def ffn_update(f, n, after):
        for kind in ('wg', 'wu'):
            k = f + '_' + kind
            res = _adamw_big(recv[kind + n], tr(w[k]), tr(m[k]), tr(v[k]), 32, "adamw_ffn", after)
            out[k] = tuple(tr(a) for a in res)
        k = f + '_wd'
        out[k] = _adamw_big(recv['wd' + n], w[k], m[k], v[k], 32, "adamw_ffn", after)

    ffn_update('ffn2', '2', last)
    out['w_in'] = _adamw_big(recv['win'], w['w_in'], m['w_in'], v['w_in'], 128, "adamw_win", last, transposed=True)
    out['w_out'] = _adamw_big(recv['wout'], w['w_out'], m['w_out'], v['w_out'], 64, "adamw_wout", last)

    total = _sum_devices(_transfer_wait(small_flight, g_last, "gather_small_wait")[0])
    like = [w[k] for k in SMALL] + [jax.ShapeDtypeStruct((depth, 4, DB), F32)]
    grads = _unpack(total, like)
    gsmall = dict(zip(SMALL, grads[:-1]))
    gsmall['ffn1_norm'] = gsmall['ffn1_norm'].at[0].set(_sum_devices(g_last).reshape(D))
    gsmall['conv_w'] = lax.dynamic_slice_in_dim(grads[-1], my * (DB // NDEV), DB // NDEV, axis=2)
    keys = SMALL + ['conv_w']
    dl, mm, vv = _adamw_small(_pack([w[k] for k in keys]), _pack([gsmall[k] for k in keys]),
                              _pack([m[k] for k in keys]), _pack([v[k] for k in keys]))
    like = [w[k] for k in keys]
    for k, d_, m_, v_ in zip(keys, _unpack(dl, like), _unpack(mm, like), _unpack(vv, like)):
        out[k] = (gsmall[k], d_, m_, v_)
    done = [dl] + [out[k][1][0] for k in ('ffn2_wg', 'ffn2_wu', 'ffn2_wd', 'w_in', 'w_out')]
    land(functools.reduce(lambda p, q: p + q, [a[:1, :1] for a in done]))
    ffn_update('ffn1', '1', last)

    return (loss, grad_x, *[out[k][0] for k in NAMES], *[out[k][1] for k in NAMES],
            *[out[k][2] for k in NAMES], *[out[k][3] for k in NAMES])


def kernel(x, ffn1_norm, ffn1_wg, ffn1_wu, ffn1_wd, mix_norm, w_in, hgrn_lb_logits, hgrn_norm, conv_w, conv_b, lru_wa, lru_ba, lru_wx, lru_bx, lru_lambda, lru_norm, sgu_w, sgu_b, sgu_norm, w_out, ffn2_norm, ffn2_wg, ffn2_wu, ffn2_wd, final_norm, loss_target, m_ffn1_norm, m_ffn1_wg, m_ffn1_wu, m_ffn1_wd, m_mix_norm, m_w_in, m_hgrn_lb_logits, m_hgrn_norm, m_conv_w, m_conv_b, m_lru_wa, m_lru_ba, m_lru_wx, m_lru_bx, m_lru_lambda, m_lru_norm, m_sgu_w, m_sgu_b, m_sgu_norm, m_w_out, m_ffn2_norm, m_ffn2_wg, m_ffn2_wu, m_ffn2_wd, m_final_norm, v_ffn1_norm, v_ffn1_wg, v_ffn1_wu, v_ffn1_wd, v_mix_norm, v_w_in, v_hgrn_lb_logits, v_hgrn_norm, v_conv_w, v_conv_b, v_lru_wa, v_lru_ba, v_lru_wx, v_lru_bx, v_lru_lambda, v_lru_norm, v_sgu_w, v_sgu_b, v_sgu_norm, v_w_out, v_ffn2_norm, v_ffn2_wg, v_ffn2_wu, v_ffn2_wd, v_final_norm):
    args = locals()
    w = {k: args[k] for k in NAMES}
    m = {k: args['m_' + k] for k in NAMES}
    v = {k: args['v_' + k] for k in NAMES}
    return _step(x, loss_target, w, m, v)
```

```python
import functools

import jax
import jax.numpy as jnp
from jax import lax
from jax.experimental import pallas as pl
from jax.experimental.pallas import tpu as pltpu

F32 = jnp.float32
MXU = jnp.bfloat16
SAVE = jnp.bfloat16
WIRE = jnp.bfloat16

NDEV = 8
D = 1024
FF = 2816
FFS = FF // NDEV
FB = 256
FBX = FF // 2
DIN = 3072
DINS = DIN // NDEV
ZB = 512
ZBW = 1024
DA, DB, DC = 512, 256, 256
HD = 128
NH = DA // HD
ACH = 64
ACB = 4
CCH = 128
GRP = 64
EPS = 1e-6
LRU_C = 8.0
VMEM_LIMIT = 60 * 1024 * 1024
TM_F = 1024
TM_B = 512
TB = 1024
SUB = 256

ADAM_LR, ADAM_B1, ADAM_B2, ADAM_EPS, ADAM_WD, ADAM_STEP = 0.001, 0.9, 0.999, 1e-08, 0.01, 10

MESH = pl.DeviceIdType.MESH


def _mm(a, b):
    return jnp.dot(a.astype(MXU), b.astype(MXU), preferred_element_type=F32)


def _mm_nt(a, b):
    return lax.dot_general(a.astype(MXU), b.astype(MXU), (((1,), (1,)), ((), ())), preferred_element_type=F32)


def _mm_tn(a, b):
    return lax.dot_general(a.astype(MXU), b.astype(MXU), (((0,), (0,)), ((), ())), preferred_element_type=F32)


def _split3(x):
    x1 = x.astype(MXU)
    r1 = x - x1.astype(F32)
    x2 = r1.astype(MXU)
    r2 = r1 - x2.astype(F32)
    return x1, x2, r2.astype(MXU)


def _mm_exact_l(c, x):
    x1, x2, x3 = _split3(x)
    return _mm(c, x1) + _mm(c, x2) + _mm(c, x3)


def _mm_exact_r(x, c):
    x1, x2, x3 = _split3(x)
    return _mm(x1, c) + _mm(x2, c) + _mm(x3, c)


def _sigmoid(x):
    return 1.0 / (1.0 + jnp.exp(-x))


def _gelu(x):
    c, k = 0.7978845608028654, 0.044715
    th = jnp.tanh(c * (x + k * x * x * x))
    return 0.5 * x * (1.0 + th)


def _gelu_and_grad(x):
    c, k = 0.7978845608028654, 0.044715
    th = jnp.tanh(c * (x + k * x * x * x))
    g = 0.5 * x * (1.0 + th)
    dg = 0.5 * (1.0 + th) + 0.5 * x * (1.0 - th * th) * c * (1.0 + 3.0 * k * x * x)
    return g, dg


def _expm1(x):
    series = x * (1.0 + x * (0.5 + x * (1.0 / 6.0 + x * (1.0 / 24.0 + x * (1.0 / 120.0)))))
    return jnp.where(jnp.abs(x) < 0.05, series, jnp.exp(x) - 1.0)


def _iota(shape, dim):
    return lax.broadcasted_iota(jnp.int32, shape, dim)


def _group_matrix(n, value):
    r, c = _iota((n, n), 0), _iota((n, n), 1)
    return jnp.where((r // GRP) == (c // GRP), value, 0.0).astype(F32)


def _row(x, k):
    r = _iota(x.shape, 0)
    return jnp.sum(jnp.where(r == k, x, 0.0), axis=0, keepdims=True)


def _rms_bwd(dxn, hh, gain):
    rstd = lax.rsqrt(jnp.mean(hh * hh, axis=-1, keepdims=True) + EPS)
    xhat = hh * rstd
    dxh = dxn * gain
    dh = rstd * (dxh - xhat * jnp.mean(dxh * xhat, axis=-1, keepdims=True))
    return dh, jnp.sum(dxn * xhat, axis=0, keepdims=True)


def _params(sem):
    return pltpu.CompilerParams(dimension_semantics=sem, vmem_limit_bytes=VMEM_LIMIT)


def _all_gather(arrs, name):
    n = len(arrs)

    def body(*refs):
        ins, outs = refs[:n], refs[n:2 * n]
        send_sems, recv_sems, local_sems = refs[2 * n:]
        x, y, c = lax.axis_index("x"), lax.axis_index("y"), lax.axis_index("c")
        me, sibling = (x, y, c), (x, y, 1 - c)
        chips = [(1 - x, y), (x, 1 - y), (1 - x, 1 - y)]

        def slot(px, py, pc):
            return 4 * px + 2 * py + pc

        def copy(a, k, block, to, src=None):
            dst = outs[a].at[slot(*block)]
            return pltpu.make_async_remote_copy(
                src_ref=dst if src is None else src, dst_ref=dst,
                send_sem=send_sems.at[a * 7 + k], recv_sem=recv_sems.at[a * 7 + k],
                device_id=to, device_id_type=MESH)

        started = []
        for a in range(n):
            mine = pltpu.make_async_copy(ins[a], outs[a].at[slot(*me)], local_sems.at[a])
            mine.start()
            started.append(mine)
        first = []
        for a in range(n):
            first.append(copy(a, 0, me, sibling, src=ins[a]))
            first += [copy(a, 1 + j, me, (*chip, c), src=ins[a]) for j, chip in enumerate(chips)]
        for cp in first:
            cp.start()
        passed = []
        for a in range(n):
            for j, chip in enumerate(chips):
                copy(a, 1 + j, (*chip, c), me).wait_recv()
                fwd = copy(a, 4 + j, (*chip, c), sibling)
                fwd.start()
                passed.append(fwd)
        for a in range(n):
            copy(a, 0, sibling, me).wait_recv()
            for j, chip in enumerate(chips):
                copy(a, 4 + j, (*chip, 1 - c), me).wait_recv()
        for cp in first + passed:
            cp.wait_send()
        for mine in started:
            mine.wait()

    hbm = pl.BlockSpec(memory_space=pl.ANY)
    return pl.pallas_call(
        body, name=name,
        out_shape=[jax.ShapeDtypeStruct((NDEV,) + a.shape, a.dtype) for a in arrs],
        in_specs=[hbm] * n, out_specs=[hbm] * n,
        scratch_shapes=[pltpu.SemaphoreType.DMA((7 * n,)), pltpu.SemaphoreType.DMA((7 * n,)),
                        pltpu.SemaphoreType.DMA((n,))],
    )(*arrs)


def _peers():
    x, y, c = lax.axis_index("x"), lax.axis_index("y"), lax.axis_index("c")
    peers = [(x ^ ((k >> 2) & 1), y ^ ((k >> 1) & 1), c ^ (k & 1)) for k in range(1, NDEV)]
    return (x, y, c), 4 * x + 2 * y + c, peers


_HBM = pl.BlockSpec(memory_space=pltpu.HBM)
_SEM = pl.BlockSpec(memory_space=pltpu.SEMAPHORE)
_EFFECT = pltpu.SideEffectType.DATAFLOW_SIDE_EFFECTING


def _transfer_start(arrs, gather, name, deps=(), direct=False):
    n, nd = len(arrs), len(deps)
    shapes = [((NDEV,) + a.shape) if gather else a.shape for a in arrs]

    def body(*refs):
        ins, lands = refs[:n], refs[n:2 * n]
        send_sems, recv_sems, local_sems = refs[2 * n + nd:2 * n + nd + 3]
        token = refs[-1]
        (x, y, c), my, peers = _peers()
        if gather and not direct:
            peers = [(x, y, 1 - c), (1 - x, y, c), (x, 1 - y, c), (1 - x, 1 - y, c)]
        for a in range(n):
            own = ins[a] if gather else ins[a].at[my]
            pltpu.make_async_copy(own, lands[a].at[my], local_sems.at[a]).start()
        for a in range(n):
            for peer in peers:
                src = ins[a] if gather else ins[a].at[4 * peer[0] + 2 * peer[1] + peer[2]]
                pltpu.make_async_remote_copy(
                    src_ref=src, dst_ref=lands[a].at[my], send_sem=send_sems.at[a], recv_sem=recv_sems.at[a],
                    device_id=peer, device_id_type=MESH).start()
        token[...] = jnp.zeros_like(token)

    out_shape = [pltpu.SemaphoreType.DMA((n,))] * 3
    out_shape += [pltpu.HBM(a.shape, a.dtype) for a in arrs]
    out_shape += [pltpu.HBM(s, a.dtype) for s, a in zip(shapes, arrs)]
    out_shape += [jax.ShapeDtypeStruct((8, 128), F32)]
    operands = [pltpu.with_memory_space_constraint(a, pltpu.HBM) for a in arrs]
    operands += [pltpu.with_memory_space_constraint(lax.empty(s, a.dtype), pltpu.HBM) for s, a in zip(shapes, arrs)]
    res = pl.pallas_call(
        body, name=name, out_shape=out_shape,
        in_specs=[_HBM] * (2 * n) + [pl.BlockSpec(memory_space=pl.ANY)] * nd,
        out_specs=[_SEM] * 3 + [_HBM] * (2 * n) + [pl.BlockSpec(memory_space=pltpu.VMEM)],
        input_output_aliases={i: 3 + i for i in range(2 * n)},
        compiler_params=pltpu.CompilerParams(has_side_effects=_EFFECT),
    )(*operands, *deps)
    return dict(sems=res[:3], src=res[3:3 + n], lands=res[3 + n:3 + 2 * n], token=res[-1], n=n,
                count=4 if gather and not direct else NDEV - 1)


def _forward_start(lands, name, shards=()):
    n, m = len(lands), len(shards)
    zones = [(NDEV,) + a.shape for a in shards]

    def body(*refs):
        zone, ins, fresh = refs[:n], refs[n:n + m], refs[n + m:n + 2 * m]
        sems = refs[n + 2 * m:n + 2 * m + (5 if m else 2)]
        token = refs[-1]
        (x, y, c), my, _ = _peers()
        for a in range(n):
            for px, py in ((1 - x, y), (x, 1 - y), (1 - x, 1 - y)):
                block = zone[a].at[4 * px + 2 * py + c]
                pltpu.make_async_remote_copy(
                    src_ref=block, dst_ref=block, send_sem=sems[0].at[a], recv_sem=sems[1].at[a],
                    device_id=(x, y, 1 - c), device_id_type=MESH).start()
        for a in range(m):
            pltpu.make_async_copy(ins[a], fresh[a].at[my], sems[4].at[a]).start()
            for peer in ((x, y, 1 - c), (1 - x, y, c), (x, 1 - y, c), (1 - x, 1 - y, c)):
                pltpu.make_async_remote_copy(
                    src_ref=ins[a], dst_ref=fresh[a].at[my], send_sem=sems[2].at[a], recv_sem=sems[3].at[a],
                    device_id=peer, device_id_type=MESH).start()
        token[...] = jnp.zeros_like(token)

    sem_shapes = [pltpu.SemaphoreType.DMA((n,))] * 2 + ([pltpu.SemaphoreType.DMA((m,))] * 3 if m else [])
    ns = len(sem_shapes)
    thru = list(lands) + list(shards)
    operands = thru + [pltpu.with_memory_space_constraint(lax.empty(s, a.dtype), pltpu.HBM)
                       for s, a in zip(zones, shards)]
    res = pl.pallas_call(
        body, name=name,
        out_shape=sem_shapes + [pltpu.HBM(a.shape, a.dtype) for a in thru]
        + [pltpu.HBM(s, a.dtype) for s, a in zip(zones, shards)] + [jax.ShapeDtypeStruct((8, 128), F32)],
        in_specs=[_HBM] * (n + 2 * m),
        out_specs=[_SEM] * ns + [_HBM] * (n + 2 * m) + [pl.BlockSpec(memory_space=pltpu.VMEM)],
        input_output_aliases={i: ns + i for i in range(n + 2 * m)},
        compiler_params=pltpu.CompilerParams(has_side_effects=_EFFECT),
    )(*[pltpu.with_memory_space_constraint(a, pltpu.HBM) for a in thru], *operands[n + m:])
    forward = dict(sems=res[:2], src=[], lands=res[ns:ns + n], token=res[-1], n=n, count=3)
    nxt = None
    if m:
        nxt = dict(sems=res[2:5], src=res[ns + n:ns + n + m], lands=res[ns + n + m:ns + n + 2 * m],
                   token=res[-1], n=m, count=4)
    return forward, nxt


def _transfer_wait(handle, after, name):
    n, count = handle["n"], handle["count"]
    src, lands, sems = list(handle["src"]), list(handle["lands"]), list(handle["sems"])
    ns = len(src)

    def body(*refs):
        zone = refs[ns:ns + n]
        sem_refs = refs[ns + n:ns + n + len(sems)]
        me, _, _ = _peers()
        for a in range(n):
            moved = zone[a].at[pl.ds(0, count)]
            both = pltpu.make_async_remote_copy(
                src_ref=moved, dst_ref=moved, send_sem=sem_refs[0].at[a], recv_sem=sem_refs[1].at[a],
                device_id=me, device_id_type=MESH)
            both.wait_send()
            both.wait_recv()
            if len(sems) == 3:
                pltpu.make_async_copy(zone[a].at[0], zone[a].at[1], sem_refs[2].at[a]).wait()

    res = pl.pallas_call(
        body, name=name,
        out_shape=[pltpu.HBM(a.shape, a.dtype) for a in src + lands],
        in_specs=[_HBM] * (ns + n) + [_SEM] * len(sems) + [pl.BlockSpec(memory_space=pl.ANY)],
        out_specs=[_HBM] * (ns + n),
        input_output_aliases={i: i for i in range(ns + n)},
        compiler_params=pltpu.CompilerParams(has_side_effects=_EFFECT),
    )(*src, *lands, *sems, after)
    return list(res[ns:])


def _ffn_fwd(h, gain, wg, wu, wd, tm):
    t = h.shape[0]
    nj = FF // FBX

    def body(h_ref, g_ref, wg_ref, wu_ref, wd_ref, out_ref, xn_ref, a_ref, b_ref, acc_ref):
        j = pl.program_id(1)

        @pl.when(j == 0)
        def _():
            hh = h_ref[...]
            rstd = lax.rsqrt(jnp.mean(hh * hh, axis=-1, keepdims=True) + EPS)
            xn_ref[...] = (hh * rstd * g_ref[...]).astype(xn_ref.dtype)
            acc_ref[...] = jnp.zeros_like(acc_ref)

        sub = min(SUB, tm)
        for r in range(tm // sub):
            rows = slice(r * sub, (r + 1) * sub)
            xn = xn_ref[rows, :]
            y = None
            for c0 in range(0, FBX, FB):
                cols = slice(c0, min(c0 + FB, FBX))
                a = _mm_nt(xn, wg_ref[cols, :])
                b = _mm_nt(xn, wu_ref[cols, :])
                a_ref[rows, cols] = a.astype(a_ref.dtype)
                b_ref[rows, cols] = b.astype(b_ref.dtype)
                part = _mm(a * _sigmoid(a) * b, wd_ref[cols, :])
                y = part if y is None else y + part
            acc_ref[rows, :] += y

        @pl.when(j == nj - 1)
        def _():
            out_ref[...] = h_ref[...] + 0.5 * acc_ref[...]

    wspec = pl.BlockSpec((FBX, D), lambda i, j: (j, 0))
    return pl.pallas_call(
        body, name="ffn_fwd", grid=(t // tm, nj),
        in_specs=[pl.BlockSpec((tm, D), lambda i, j: (i, 0)),
                  pl.BlockSpec((1, D), lambda i, j: (0, 0)), wspec, wspec, wspec],
        out_specs=[pl.BlockSpec((tm, D), lambda i, j: (i, 0)),
                   pl.BlockSpec((tm, D), lambda i, j: (i, 0)),
                   pl.BlockSpec((tm, FBX), lambda i, j: (i, j)),
                   pl.BlockSpec((tm, FBX), lambda i, j: (i, j))],
        out_shape=[jax.ShapeDtypeStruct((t, D), F32), jax.ShapeDtypeStruct((t, D), SAVE),
                   jax.ShapeDtypeStruct((t, FF), SAVE), jax.ShapeDtypeStruct((t, FF), SAVE)],
        scratch_shapes=[pltpu.VMEM((tm, D), F32)],
        compiler_params=_params(("parallel", "arbitrary")),
    )(h, gain, wg, wu, wd)


def _ffn_bwd_x(dout, h, gain, a_sv, b_sv, wg, wu, wd, tm):
    t = h.shape[0]
    nj = FF // FBX

    def body(dout_ref, h_ref, g_ref, a_ref, b_ref, wg_ref, wu_ref, wd_ref,
             dh_ref, dgain_ref, dy_ref, da_ref, db_ref, s_ref, acc_ref):
        i, j = pl.program_id(0), pl.program_id(1)

        @pl.when((i == 0) & (j == 0))
        def _():
            dgain_ref[...] = jnp.zeros_like(dgain_ref)

        @pl.when(j == 0)
        def _():
            dy_ref[...] = (0.5 * dout_ref[...]).astype(dy_ref.dtype)
            acc_ref[...] = jnp.zeros_like(acc_ref)

        sub = min(SUB, tm)
        for r in range(tm // sub):
            rows = slice(r * sub, (r + 1) * sub)
            dy = dy_ref[rows, :]
            dx = None
            for c0 in range(0, FBX, FB):
                cols = slice(c0, min(c0 + FB, FBX))
                ds = _mm_nt(dy, wd_ref[cols, :])
                a, b = a_ref[rows, cols].astype(F32), b_ref[rows, cols].astype(F32)
                sg = _sigmoid(a)
                sa = a * sg
                da = (ds * b * (sg * (1.0 + a * (1.0 - sg)))).astype(MXU)
                db = (ds * sa).astype(MXU)
                da_ref[rows, cols] = da.astype(da_ref.dtype)
                db_ref[rows, cols] = db.astype(db_ref.dtype)
                s_ref[rows, cols] = (sa * b).astype(s_ref.dtype)
                part = _mm(da, wg_ref[cols, :]) + _mm(db, wu_ref[cols, :])
                dx = part if dx is None else dx + part
            acc_ref[rows, :] += dx

        @pl.when(j == nj - 1)
        def _():
            dh, dg = _rms_bwd(acc_ref[...], h_ref[...], g_ref[...])
            dh_ref[...] = dout_ref[...] + dh
            dgain_ref[...] += dg

    tok = pl.BlockSpec((tm, D), lambda i, j: (i, 0))
    act = pl.BlockSpec((tm, FBX), lambda i, j: (i, j))
    wspec = pl.BlockSpec((FBX, D), lambda i, j: (j, 0))
    return pl.pallas_call(
        body, name="ffn_bwd_x", grid=(t // tm, nj),
        in_specs=[tok, tok, pl.BlockSpec((1, D), lambda i, j: (0, 0)), act, act, wspec, wspec, wspec],
        out_specs=[tok, pl.BlockSpec((1, D), lambda i, j: (0, 0)), tok, act, act, act],
        out_shape=[jax.ShapeDtypeStruct((t, D), F32), jax.ShapeDtypeStruct((1, D), F32),
                   jax.ShapeDtypeStruct((t, D), SAVE)] + [jax.ShapeDtypeStruct((t, FF), SAVE)] * 3,
        scratch_shapes=[pltpu.VMEM((tm, D), F32)],
        compiler_params=_params(("arbitrary", "arbitrary")),
    )(dout, h, gain, a_sv, b_sv, wg, wu, wd)


def _ffn_bwd_w(xn, dy, da, db, s, tm):
    t = xn.shape[0]
    nt = t // tm
    nj = FF // FBX

    def body(xn_ref, dy_ref, da_ref, db_ref, s_ref, dwg_ref, dwu_ref, dwd_ref, ag_scr, au_scr, ad_scr):
        i = pl.program_id(1)

        @pl.when(i == 0)
        def _():
            for ref in (ag_scr, au_scr, ad_scr):
                ref[...] = jnp.zeros_like(ref)

        xn, dy = xn_ref[...], dy_ref[...]
        for c0 in range(0, FBX, FB):
            rows = slice(c0, min(c0 + FB, FBX))
            ag_scr[rows, :] += _mm_tn(da_ref[:, rows], xn)
            au_scr[rows, :] += _mm_tn(db_ref[:, rows], xn)
            ad_scr[rows, :] += _mm_tn(s_ref[:, rows], dy)

        @pl.when(i == nt - 1)
        def _():
            for out, ref in ((dwg_ref, ag_scr), (dwu_ref, au_scr), (dwd_ref, ad_scr)):
                out[...] = ref[...].astype(out.dtype)

    tok = pl.BlockSpec((tm, D), lambda j, i: (i, 0))
    act = pl.BlockSpec((tm, FBX), lambda j, i: (i, j))
    wspec = pl.BlockSpec((FBX, D), lambda j, i: (j, 0))
    return pl.pallas_call(
        body, name="ffn_bwd_w", grid=(nj, nt),
        in_specs=[tok, tok, act, act, act], out_specs=[wspec] * 3,
        out_shape=[jax.ShapeDtypeStruct((FF, D), WIRE)] * 3,
        scratch_shapes=[pltpu.VMEM((FBX, D), F32)] * 3,
        compiler_params=_params(("parallel", "arbitrary")),
    )(xn, dy, da, db, s)


def _inproj_fwd(h, gain, win, tm):
    t = h.shape[0]

    def body(h_ref, g_ref, w_ref, z_ref, xn_ref):
        hh = h_ref[...]
        rstd = lax.rsqrt(jnp.mean(hh * hh, axis=-1, keepdims=True) + EPS)
        xn = (hh * rstd * g_ref[...]).astype(MXU)
        xn_ref[...] = xn.astype(xn_ref.dtype)
        for j in range(DIN // ZB):
            z_ref[:, j * ZB:(j + 1) * ZB] = _mm_nt(xn, w_ref[j * ZB:(j + 1) * ZB, :])

    return pl.pallas_call(
        body, name="inproj_fwd", grid=(t // tm,),
        in_specs=[pl.BlockSpec((tm, D), lambda i: (i, 0)),
                  pl.BlockSpec((1, D), lambda i: (0, 0)),
                  pl.BlockSpec((DIN, D), lambda i: (0, 0))],
        out_specs=[pl.BlockSpec((tm, DIN), lambda i: (i, 0)),
                   pl.BlockSpec((tm, D), lambda i: (i, 0))],
        out_shape=[jax.ShapeDtypeStruct((t, DIN), F32), jax.ShapeDtypeStruct((t, D), SAVE)],
        compiler_params=_params(("parallel",)),
    )(h, gain, win)


def _inproj_bwd_x(dres, dz, h, gain, win, tm):
    t = h.shape[0]

    def body(dres_ref, dz_ref, h_ref, g_ref, w_ref, dh_ref, dgain_ref):
        @pl.when(pl.program_id(0) == 0)
        def _():
            dgain_ref[...] = jnp.zeros_like(dgain_ref)

        dh, dg = _rms_bwd(_mm(dz_ref[...], w_ref[...]), h_ref[...], g_ref[...])
        dh_ref[...] = dres_ref[...] + dh
        dgain_ref[...] += dg

    return pl.pallas_call(
        body, name="inproj_bwd_x", grid=(t // tm,),
        in_specs=[pl.BlockSpec((tm, D), lambda i: (i, 0)),
                  pl.BlockSpec((tm, DIN), lambda i: (i, 0)),
                  pl.BlockSpec((tm, D), lambda i: (i, 0)),
                  pl.BlockSpec((1, D), lambda i: (0, 0)),
                  pl.BlockSpec((DIN, D), lambda i: (0, 0))],
        out_specs=[pl.BlockSpec((tm, D), lambda i: (i, 0)),
                   pl.BlockSpec((1, D), lambda i: (0, 0))],
        out_shape=[jax.ShapeDtypeStruct((t, D), F32), jax.ShapeDtypeStruct((1, D), F32)],
        compiler_params=_params(("arbitrary",)),
    )(dres, dz, h, gain, win)


def _inproj_bwd_w(xn, dz, tm):
    t = xn.shape[0]
    nt = t // tm

    def body(xn_ref, dz_ref, dw_ref, acc_scr):
        i = pl.program_id(1)

        @pl.when(i == 0)
        def _():
            acc_scr[...] = jnp.zeros_like(acc_scr)

        acc_scr[...] += _mm_tn(dz_ref[...], xn_ref[...])

        @pl.when(i == nt - 1)
        def _():
            dw_ref[...] = acc_scr[...].astype(dw_ref.dtype)

    return pl.pallas_call(
        body, name="inproj_bwd_w", grid=(DIN // ZBW, nt),
        in_specs=[pl.BlockSpec((tm, D), lambda j, i: (i, 0)),
                  pl.BlockSpec((tm, ZBW), lambda j, i: (i, j))],
        out_specs=pl.BlockSpec((ZBW, D), lambda j, i: (j, 0)),
        out_shape=jax.ShapeDtypeStruct((DIN, D), WIRE),
        scratch_shapes=[pltpu.VMEM((ZBW, D), F32)],
        compiler_params=_params(("parallel", "arbitrary")),
    )(xn, dz)


def _outproj_fwd(h, oa, ob, oc, wout, tm):
    t = h.shape[0]

    def body(h_ref, oa_ref, ob_ref, oc_ref, w_ref, out_ref):
        ym = jnp.concatenate([oa_ref[...], ob_ref[...], oc_ref[...]], axis=1)
        out_ref[...] = h_ref[...] + _mm(ym, w_ref[...])

    return pl.pallas_call(
        body, name="outproj_fwd", grid=(t // tm,),
        in_specs=[pl.BlockSpec((tm, D), lambda i: (i, 0)),
                  pl.BlockSpec((tm, DA), lambda i: (i, 0)),
                  pl.BlockSpec((tm, DB), lambda i: (i, 0)),
                  pl.BlockSpec((tm, DC), lambda i: (i, 0)),
                  pl.BlockSpec((D, D), lambda i: (0, 0))],
        out_specs=pl.BlockSpec((tm, D), lambda i: (i, 0)),
        out_shape=jax.ShapeDtypeStruct((t, D), F32),
        compiler_params=_params(("parallel",)),
    )(h, oa, ob, oc, wout)


def _outproj_bwd(dh, oa, ob, oc, wout, tm):
    t = dh.shape[0]
    nt = t // tm

    def body(dh_ref, oa_ref, ob_ref, oc_ref, w_ref, da_ref, db_ref, dc_ref, dw_ref, acc_scr):
        i = pl.program_id(0)

        @pl.when(i == 0)
        def _():
            acc_scr[...] = jnp.zeros_like(acc_scr)

        d16 = dh_ref[...].astype(MXU)
        dym = _mm_nt(d16, w_ref[...])
        da_ref[...] = dym[:, :DA]
        db_ref[...] = dym[:, DA:DA + DB]
        dc_ref[...] = dym[:, DA + DB:]
        ym = jnp.concatenate([oa_ref[...], ob_ref[...], oc_ref[...]], axis=1)
        acc_scr[...] += _mm_tn(ym, d16)

        @pl.when(i == nt - 1)
        def _():
            dw_ref[...] = acc_scr[...].astype(dw_ref.dtype)

    return pl.pallas_call(
        body, name="outproj_bwd", grid=(nt,),
        in_specs=[pl.BlockSpec((tm, D), lambda i: (i, 0)),
                  pl.BlockSpec((tm, DA), lambda i: (i, 0)),
                  pl.BlockSpec((tm, DB), lambda i: (i, 0)),
                  pl.BlockSpec((tm, DC), lambda i: (i, 0)),
                  pl.BlockSpec((D, D), lambda i: (0, 0))],
        out_specs=[pl.BlockSpec((tm, DA), lambda i: (i, 0)),
                   pl.BlockSpec((tm, DB), lambda i: (i, 0)),
                   pl.BlockSpec((tm, DC), lambda i: (i, 0)),
                   pl.BlockSpec((D, D), lambda i: (0, 0))],
        out_shape=[jax.ShapeDtypeStruct((t, DA), F32), jax.ShapeDtypeStruct((t, DB), F32),
                   jax.ShapeDtypeStruct((t, DC), F32), jax.ShapeDtypeStruct((D, D), WIRE)],
        scratch_shapes=[pltpu.VMEM((D, D), F32)],
        compiler_params=_params(("arbitrary",)),
    )(dh, oa, ob, oc, wout)


def _lower_bounds(logits):
    depth, n = logits.shape

    def body(l_ref, lb_ref, p_ref):
        rows = [l_ref[l:l + 1, :] for l in range(depth)]
        mx = functools.reduce(jnp.maximum, rows)
        ex = [jnp.exp(r - mx) for r in rows]
        den = functools.reduce(lambda u, v: u + v, ex)
        acc = jnp.zeros_like(den)
        for l in range(depth):
            p = ex[l] / den
            p_ref[l:l + 1, :] = p
            if l > 0:
                acc = acc + p
            lb_ref[l:l + 1, :] = acc

    return pl.pallas_call(
        body, name="lower_bounds",
        out_shape=[jax.ShapeDtypeStruct((depth, n), F32), jax.ShapeDtypeStruct((depth, n), F32)],
    )(logits)


def _lower_bounds_bwd(p, dlb):
    depth, n = p.shape

    def body(p_ref, d_ref, out_ref):
        ps = [p_ref[l:l + 1, :] for l in range(depth)]
        ds = [d_ref[l:l + 1, :] for l in range(depth)]
        dp = [jnp.zeros_like(ps[0]) for _ in range(depth)]
        run = jnp.zeros_like(ps[0])
        for l in range(depth - 1, 0, -1):
            run = run + ds[l]
            dp[l] = run
        dot = functools.reduce(lambda u, v: u + v, [ps[l] * dp[l] for l in range(depth)])
        for l in range(depth):
            out_ref[l:l + 1, :] = ps[l] * (dp[l] - dot)

    return pl.pallas_call(body, name="lower_bounds_bwd", out_shape=jax.ShapeDtypeStruct((depth, n), F32))(p, dlb)


def _hgrn_block(z_ref, lb_ref, rb):
    q, fl = z_ref[:, 0:DA], z_ref[:, DA:2 * DA]
    lb = lb_ref[...]
    sq = _sigmoid(q)
    qs = q * sq
    sg = _sigmoid(fl)
    f = lb + (1.0 - lb) * sg
    k = 1.0 - f
    lf = jnp.log(f)
    row, col = _iota((rb, rb), 0), _iota((rb, rb), 1)
    same = (row // ACH) == (col // ACH)
    causal = same & (row >= col)
    b = _mm_exact_l(jnp.where(causal, 1.0, 0.0).astype(MXU), lf)
    bend = _mm_exact_l(jnp.where(same, 1.0, 0.0).astype(MXU), lf)
    r = 0.5 * bend
    eq, ek, eb, ed = jnp.exp(b - r), jnp.exp(r - b), jnp.exp(b), jnp.exp(bend - b)
    return dict(q=q, lb=lb, sq=sq, sg=sg, f=f, bend=bend, eq=eq, ek=ek, eb=eb, ed=ed,
                qt=qs * eq, kt=k * ek, qe=qs * eb, kd=k * ed, same=same, causal=causal)


def _hgrn_fwd(z, lb, gain):
    t = z.shape[0]
    nc = t // ACH
    cb = min(ACB, nc)
    rb = cb * ACH

    def body(z_ref, lb_ref, g_ref, o_ref, oa_ref, st_ref, st_scr):
        @pl.when(pl.program_id(0) == 0)
        def _():
            st_scr[...] = jnp.zeros_like(st_scr)

        c = _hgrn_block(z_ref, lb_ref, rb)
        for hd in range(NH):
            cols = slice(hd * HD, (hd + 1) * HD)
            v = z_ref[:, 2 * DA + hd * HD:2 * DA + (hd + 1) * HD]
            gg = z_ref[:, 3 * DA + hd * HD:3 * DA + (hd + 1) * HD]
            att = jnp.where(c["causal"], _mm_nt(c["qt"][:, cols], c["kt"][:, cols]), 0.0)
            o_in = _mm(att, v)
            qe, kd, bend = c["qe"][:, cols], c["kd"][:, cols], c["bend"][:, cols]
            st = st_scr[hd]
            outs = []
            for cc in range(cb):
                rows = slice(cc * ACH, (cc + 1) * ACH)
                st_ref[cc, hd] = st
                outs.append(o_in[rows] + _mm_nt(qe[rows], st))
                decay = jnp.exp(jnp.max(bend[rows], axis=0, keepdims=True))
                st = st * decay + _mm_tn(v[rows], kd[rows])
            st_scr[hd] = st
            o = jnp.concatenate(outs, axis=0)
            o_ref[:, cols] = o
            rstd = lax.rsqrt(jnp.mean(o * o, axis=-1, keepdims=True) + EPS)
            oa_ref[:, cols] = (o * rstd * g_ref[:, cols] * (gg * _sigmoid(gg))).astype(oa_ref.dtype)

    return pl.pallas_call(
        body, name="hgrn_fwd", grid=(nc // cb,),
        in_specs=[pl.BlockSpec((rb, 4 * DA), lambda c: (c, 0)),
                  pl.BlockSpec((1, DA), lambda c: (0, 0)),
                  pl.BlockSpec((1, DA), lambda c: (0, 0))],
        out_specs=[pl.BlockSpec((rb, DA), lambda c: (c, 0)),
                   pl.BlockSpec((rb, DA), lambda c: (c, 0)),
                   pl.BlockSpec((cb, NH, HD, HD), lambda c: (c, 0, 0, 0))],
        out_shape=[jax.ShapeDtypeStruct((t, DA), F32), jax.ShapeDtypeStruct((t, DA), SAVE),
                   jax.ShapeDtypeStruct((nc, NH, HD, HD), F32)],
        scratch_shapes=[pltpu.VMEM((NH, HD, HD), F32)],
        compiler_params=_params(("arbitrary",)),
    )(z, lb, gain)


def _hgrn_bwd(z, lb, gain, o, states, doa):
    t = z.shape[0]
    nc = t // ACH
    cb = min(ACB, nc)
    rb = cb * ACH
    nblk = nc // cb

    def body(z_ref, lb_ref, g_ref, o_ref, st_ref, doa_ref, dz_ref, dgain_ref, dlb_ref, dst_scr):
        @pl.when(pl.program_id(0) == 0)
        def _():
            dst_scr[...] = jnp.zeros_like(dst_scr)
            dgain_ref[...] = jnp.zeros_like(dgain_ref)
            dlb_ref[...] = jnp.zeros_like(dlb_ref)

        c = _hgrn_block(z_ref, lb_ref, rb)
        dbs, dqss, dks = [], [], []
        for hd in range(NH):
            cols = slice(hd * HD, (hd + 1) * HD)
            v = z_ref[:, 2 * DA + hd * HD:2 * DA + (hd + 1) * HD]
            gg = z_ref[:, 3 * DA + hd * HD:3 * DA + (hd + 1) * HD]
            qt, kt, qe, kd, bend = (c[n][:, cols] for n in ("qt", "kt", "qe", "kd", "bend"))
            o = o_ref[:, cols]
            do_a = doa_ref[:, cols]
            gain = g_ref[:, cols]
            sgg = _sigmoid(gg)
            silu_g = gg * sgg
            rstd = lax.rsqrt(jnp.mean(o * o, axis=-1, keepdims=True) + EPS)
            n = o * rstd
            dn = do_a * gain * silu_g
            dg = do_a * n * gain * (sgg * (1.0 + gg * (1.0 - sgg)))
            dgain_ref[:, cols] += jnp.sum(do_a * silu_g * n, axis=0, keepdims=True)
            d_o = rstd * (dn - n * jnp.mean(dn * n, axis=-1, keepdims=True))

            att = jnp.where(c["causal"], _mm_nt(qt, kt), 0.0)
            datt = jnp.where(c["causal"], _mm_nt(d_o, v), 0.0)
            dv_in = _mm_tn(att, d_o)
            dqt = _mm(datt, kt)
            dkt = _mm_tn(datt, qt)
            dsp = dst_scr[hd]
            dvs, dqes, dkds, dbends = [None] * cb, [None] * cb, [None] * cb, [None] * cb
            for cc in reversed(range(cb)):
                rows = slice(cc * ACH, (cc + 1) * ACH)
                st = st_ref[cc, hd]
                dvs[cc] = dv_in[rows] + _mm_nt(kd[rows], dsp)
                dqes[cc] = _mm(d_o[rows], st)
                dkds[cc] = _mm(v[rows], dsp)
                decay = jnp.exp(jnp.max(bend[rows], axis=0, keepdims=True))
                dbend = (decay * jnp.sum(st * dsp, axis=0, keepdims=True)
                         + jnp.sum(dkds[cc] * kd[rows], axis=0, keepdims=True))
                dbends[cc] = jnp.broadcast_to(dbend, (ACH, HD))
                dsp = dsp * decay + _mm_tn(d_o[rows], qe[rows])
            dst_scr[hd] = dsp
            dv, dqe, dkd, dbend = (jnp.concatenate(p, axis=0) for p in (dvs, dqes, dkds, dbends))
            dbs.append((dqt * qt + dqe * qe - dkt * kt - dkd * kd, dbend))
            dqss.append(dqt * c["eq"][:, cols] + dqe * c["eb"][:, cols])
            dks.append(dkt * c["ek"][:, cols] + dkd * c["ed"][:, cols])
            c0 = hd * HD
            dz_ref[:, 2 * DA + c0:2 * DA + c0 + HD] = dv.astype(dz_ref.dtype)
            dz_ref[:, 3 * DA + c0:3 * DA + c0 + HD] = dg.astype(dz_ref.dtype)

        db = jnp.concatenate([p[0] for p in dbs], axis=1)
        dbend = jnp.concatenate([p[1] for p in dbs], axis=1)
        dqs, dk = jnp.concatenate(dqss, axis=1), jnp.concatenate(dks, axis=1)
        row, col = _iota((rb, rb), 0), _iota((rb, rb), 1)
        upper = jnp.where(c["same"] & (row <= col), 1.0, 0.0).astype(MXU)
        dlf = _mm_exact_l(upper, db) + dbend
        df = dlf / c["f"] - dk
        sg, sq, q = c["sg"], c["sq"], c["q"]
        dlb_ref[...] += jnp.sum(df * (1.0 - sg), axis=0, keepdims=True)
        dz_ref[:, DA:2 * DA] = (df * (1.0 - c["lb"]) * sg * (1.0 - sg)).astype(dz_ref.dtype)
        dz_ref[:, 0:DA] = (dqs * (sq * (1.0 + q * (1.0 - sq)))).astype(dz_ref.dtype)

    rev = lambda c: (nblk - 1 - c, 0)
    return pl.pallas_call(
        body, name="hgrn_bwd", grid=(nblk,),
        in_specs=[pl.BlockSpec((rb, 4 * DA), rev),
                  pl.BlockSpec((1, DA), lambda c: (0, 0)),
                  pl.BlockSpec((1, DA), lambda c: (0, 0)),
                  pl.BlockSpec((rb, DA), rev),
                  pl.BlockSpec((cb, NH, HD, HD), lambda c: (nblk - 1 - c, 0, 0, 0)),
                  pl.BlockSpec((rb, DA), rev)],
        out_specs=[pl.BlockSpec((rb, 4 * DA), rev),
                   pl.BlockSpec((1, DA), lambda c: (0, 0)),
                   pl.BlockSpec((1, DA), lambda c: (0, 0))],
        out_shape=[jax.ShapeDtypeStruct((t, DIN), SAVE), jax.ShapeDtypeStruct((1, DA), F32),
                   jax.ShapeDtypeStruct((1, DA), F32)],
        scratch_shapes=[pltpu.VMEM((NH, HD, HD), F32)],
        compiler_params=_params(("arbitrary",)),
    )(z, lb, gain, o, states, doa)


def _shift_down(prev8, x, k):
    cat = jnp.concatenate([prev8, x], axis=0)
    return pltpu.roll(cat, k, axis=0)[8:, :]


def _shift_up(x, next8, k):
    n = x.shape[0]
    cat = jnp.concatenate([x, next8], axis=0)
    return pltpu.roll(cat, n + 8 - k, axis=0)[:n, :]


def _lru_gates(x, prev8, cw_ref, vec_ref, wa_ref, wx_ref):
    xs = [x, _shift_down(prev8, x, 1), _shift_down(prev8, x, 2), _shift_down(prev8, x, 3)]
    xc = vec_ref[0:1, :] + cw_ref[3:4, :] * xs[0] + cw_ref[2:3, :] * xs[1] + cw_ref[1:2, :] * xs[2] + cw_ref[0:1, :] * xs[3]
    r = _sigmoid(_mm(xc, wa_ref[...]) + vec_ref[1:2, :])
    gi = _sigmoid(_mm(xc, wx_ref[...]) + vec_ref[2:3, :])
    lam = vec_ref[3:4, :]
    sp = jnp.maximum(-lam, 0.0) + jnp.log(1.0 + jnp.exp(-jnp.abs(lam)))
    la = -LRU_C * r * sp
    a = jnp.exp(la)
    mult = jnp.sqrt(-_expm1(2.0 * la))
    return xs, xc, r, gi, sp, a, mult


def _scan_down(a, u):
    n = a.shape[0]
    row = _iota(a.shape, 0)
    s = 1
    while s < n:
        keep = row >= s
        ash = jnp.where(keep, pltpu.roll(a, s, axis=0), 1.0)
        ush = jnp.where(keep, pltpu.roll(u, s, axis=0), 0.0)
        u = a * ush + u
        a = a * ash
        s *= 2
    return a, u


def _scan_up(a, u):
    n = a.shape[0]
    row = _iota(a.shape, 0)
    s = 1
    while s < n:
        keep = row < n - s
        ash = jnp.where(keep, pltpu.roll(a, n - s, axis=0), 1.0)
        ush = jnp.where(keep, pltpu.roll(u, n - s, axis=0), 0.0)
        u = a * ush + u
        a = a * ash
        s *= 2
    return a, u


def _lru_fwd(z, cw, vec, wa, wx, tb):
    t = z.shape[0]
    xcol, gcol = (4 * DA) // DB, (4 * DA) // DB + 1

    def body(x_ref, gate_ref, cw_ref, vec_ref, wa_ref, wx_ref, ob_ref, h_ref, xprev_scr, hc_scr):
        @pl.when(pl.program_id(0) == 0)
        def _():
            xprev_scr[...] = jnp.zeros_like(xprev_scr)
            hc_scr[...] = jnp.zeros_like(hc_scr)

        x = x_ref[...]
        _, xc, _, gi, _, a, mult = _lru_gates(x, xprev_scr[...], cw_ref, vec_ref, wa_ref, wx_ref)
        acum, hloc = _scan_down(a, mult * gi * xc)
        h = hloc + acum * hc_scr[0:1, :]
        h_ref[...] = h
        hc_scr[...] = jnp.broadcast_to(_row(h, tb - 1), hc_scr.shape)
        xprev_scr[...] = x[tb - 8:, :]
        y = h * _gelu(gate_ref[...])
        ms = _mm_exact_r(y * y, _group_matrix(DB, 1.0 / GRP).astype(MXU))
        ob_ref[...] = (y * lax.rsqrt(ms + EPS) * vec_ref[4:5, :]).astype(ob_ref.dtype)

    return pl.pallas_call(
        body, name="lru_fwd", grid=(t // tb,),
        in_specs=[pl.BlockSpec((tb, DB), lambda i: (i, xcol)),
                  pl.BlockSpec((tb, DB), lambda i: (i, gcol)),
                  pl.BlockSpec((8, DB), lambda i: (0, 0)),
                  pl.BlockSpec((8, DB), lambda i: (0, 0)),
                  pl.BlockSpec((DB, DB), lambda i: (0, 0)),
                  pl.BlockSpec((DB, DB), lambda i: (0, 0))],
        out_specs=[pl.BlockSpec((tb, DB), lambda i: (i, 0)),
                   pl.BlockSpec((tb, DB), lambda i: (i, 0))],
        out_shape=[jax.ShapeDtypeStruct((t, DB), SAVE), jax.ShapeDtypeStruct((t, DB), F32)],
        scratch_shapes=[pltpu.VMEM((8, DB), F32), pltpu.VMEM((8, DB), F32)],
        compiler_params=_params(("arbitrary",)),
    )(z, z, cw, vec, wa, wx)


def _lru_bwd(z, hseq, dob, cw, vec, wa, wx, dz, tb):
    t = z.shape[0]
    nb = t // tb
    xcol, gcol = (4 * DA) // DB, (4 * DA) // DB + 1
    per = tb // 8

    def body(x_ref, xh_ref, gate_ref, h_ref, hh_ref, dob_ref, cw_ref, vec_ref, wa_ref, wx_ref, _,
             dz_ref, dcw_ref, dvec_ref, dwa_ref, dwx_ref, gc_scr, an_scr, dxc_scr):
        step = pl.program_id(0)
        blk = nb - 1 - step

        @pl.when(step == 0)
        def _():
            for ref in (gc_scr, an_scr, dxc_scr, dcw_ref, dvec_ref, dwa_ref, dwx_ref):
                ref[...] = jnp.zeros_like(ref)

        first = (blk > 0).astype(F32)
        x = x_ref[...]
        xs, xc, r, gi, sp, a, mult = _lru_gates(x, xh_ref[...] * first, cw_ref, vec_ref, wa_ref, wx_ref)
        h = h_ref[...]
        hprev = _shift_down(hh_ref[...] * first, h, 1)
        ge, dge = _gelu_and_grad(gate_ref[...])
        y = h * ge
        gmat = _group_matrix(DB, 1.0 / GRP).astype(MXU)
        rstd = lax.rsqrt(_mm_exact_r(y * y, gmat) + EPS)
        n = y * rstd
        d_ob = dob_ref[...]
        dn = d_ob * vec_ref[4:5, :]
        dvec_ref[4:5, :] += jnp.sum(d_ob * n, axis=0, keepdims=True)
        dy = rstd * (dn - n * _mm_exact_r(dn * n, gmat))
        dh = dy * ge
        dgate = dy * h * dge

        row = _iota(a.shape, 0)
        anext = jnp.where(row == tb - 1, an_scr[0:1, :], pltpu.roll(a, tb - 1, axis=0))
        acum, gloc = _scan_up(anext, dh)
        g = gloc + acum * gc_scr[0:1, :]
        gc_scr[...] = jnp.broadcast_to(_row(g, 0), gc_scr.shape)
        an_scr[...] = jnp.broadcast_to(_row(a, 0), an_scr.shape)

        da = g * hprev
        dmult = g * gi * xc
        dgi = g * mult * xc
        dxc = g * mult * gi
        dla = da * a - dmult * (a * a) / mult
        dr = dla * (-LRU_C * sp)
        dsp = jnp.sum(dla * (-LRU_C * r), axis=0, keepdims=True)
        lam = vec_ref[3:4, :]
        dvec_ref[3:4, :] += -dsp * _sigmoid(-lam)
        dpa = dr * r * (1.0 - r)
        dpx = dgi * gi * (1.0 - gi)
        dwa_ref[...] += _mm_tn(xc, dpa)
        dwx_ref[...] += _mm_tn(xc, dpx)
        dvec_ref[1:2, :] += jnp.sum(dpa, axis=0, keepdims=True)
        dvec_ref[2:3, :] += jnp.sum(dpx, axis=0, keepdims=True)
        dxc = dxc + _mm_nt(dpa, wa_ref[...]) + _mm_nt(dpx, wx_ref[...])
        dvec_ref[0:1, :] += jnp.sum(dxc, axis=0, keepdims=True)
        for tap in range(4):
            dcw_ref[tap:tap + 1, :] += jnp.sum(dxc * xs[3 - tap], axis=0, keepdims=True)
        nxt = dxc_scr[...]
        dx = (cw_ref[3:4, :] * dxc + cw_ref[2:3, :] * _shift_up(dxc, nxt, 1)
              + cw_ref[1:2, :] * _shift_up(dxc, nxt, 2) + cw_ref[0:1, :] * _shift_up(dxc, nxt, 3))
        dxc_scr[...] = dxc[:8, :]
        dz_ref[:, :DB] = dx.astype(dz_ref.dtype)
        dz_ref[:, DB:] = dgate.astype(dz_ref.dtype)

    def halo(col):
        return lambda s: (jnp.maximum((nb - 1 - s) * per - 1, 0), col)

    const = lambda s: (0, 0)
    return pl.pallas_call(
        body, name="lru_bwd", grid=(nb,),
        in_specs=[pl.BlockSpec((tb, DB), lambda s: (nb - 1 - s, xcol)),
                  pl.BlockSpec((8, DB), halo(xcol)),
                  pl.BlockSpec((tb, DB), lambda s: (nb - 1 - s, gcol)),
                  pl.BlockSpec((tb, DB), lambda s: (nb - 1 - s, 0)),
                  pl.BlockSpec((8, DB), halo(0)),
                  pl.BlockSpec((tb, DB), lambda s: (nb - 1 - s, 0)),
                  pl.BlockSpec((8, DB), const), pl.BlockSpec((8, DB), const),
                  pl.BlockSpec((DB, DB), const), pl.BlockSpec((DB, DB), const),
                  pl.BlockSpec(memory_space=pl.ANY)],
        out_specs=[pl.BlockSpec((tb, 2 * DB), lambda s: (nb - 1 - s, (4 * DA) // (2 * DB))),
                   pl.BlockSpec((8, DB), const), pl.BlockSpec((8, DB), const),
                   pl.BlockSpec((DB, DB), const), pl.BlockSpec((DB, DB), const)],
        out_shape=[jax.ShapeDtypeStruct((t, DIN), SAVE), jax.ShapeDtypeStruct((8, DB), F32),
                   jax.ShapeDtypeStruct((8, DB), F32), jax.ShapeDtypeStruct((DB, DB), F32),
                   jax.ShapeDtypeStruct((DB, DB), F32)],
        scratch_shapes=[pltpu.VMEM((8, DB), F32), pltpu.VMEM((8, DB), F32), pltpu.VMEM((8, DB), F32)],
        input_output_aliases={10: 0},
        compiler_params=_params(("arbitrary",)),
    )(z, z, z, hseq, hseq, dob, cw, vec, wa, wx, dz)


def _sgu_block(u_ref, v_ref, w_ref, b_ref, gmat, tb):
    uu, duu = _gelu_and_grad(u_ref[...])
    vv, dvv = _gelu_and_grad(v_ref[...])
    dlt = vv - _mm_exact_r(vv, gmat)
    rstd_v = lax.rsqrt(_mm_exact_r(dlt * dlt, gmat) + EPS)
    vn = dlt * rstd_v
    col = _iota((CCH, DC), 1) // GRP
    causal = _iota((CCH, CCH), 0) >= _iota((CCH, CCH), 1)
    ws = [jnp.where(causal, w_ref[g], 0.0) for g in range(DC // GRP)]
    zs = []
    for ch in range(tb // CCH):
        vn_c = vn[ch * CCH:(ch + 1) * CCH]
        zz = b_ref[...]
        for g, w in enumerate(ws):
            zz = zz + jnp.where(col == g, _mm(w, vn_c), 0.0)
        zs.append(zz)
    return uu, duu, dvv, rstd_v, vn, jnp.concatenate(zs, axis=0), ws, col, causal


def _sgu_fwd(z, w, bias, gain, tb):
    t = z.shape[0]
    ucol, vcol = (4 * DA + 2 * DB) // DC, (4 * DA + 2 * DB) // DC + 1

    def body(u_ref, v_ref, w_ref, b_ref, g_ref, oc_ref):
        gmat = _group_matrix(DC, 1.0 / GRP).astype(MXU)
        uu, _, _, _, _, zz, _, _, _ = _sgu_block(u_ref, v_ref, w_ref, b_ref, gmat, tb)
        y = uu * zz
        ms = _mm_exact_r(y * y, gmat)
        oc_ref[...] = (y * lax.rsqrt(ms + EPS) * g_ref[...]).astype(oc_ref.dtype)

    const = lambda i: (0, 0)
    return pl.pallas_call(
        body, name="sgu_fwd", grid=(t // tb,),
        in_specs=[pl.BlockSpec((tb, DC), lambda i: (i, ucol)),
                  pl.BlockSpec((tb, DC), lambda i: (i, vcol)),
                  pl.BlockSpec((DC // GRP, CCH, CCH), lambda i: (0, 0, 0)),
                  pl.BlockSpec((CCH, DC), const), pl.BlockSpec((1, DC), const)],
        out_specs=pl.BlockSpec((tb, DC), lambda i: (i, 0)),
        out_shape=jax.ShapeDtypeStruct((t, DC), SAVE),
        compiler_params=_params(("parallel",)),
    )(z, z, w, bias, gain)


def _sgu_bwd(z, doc, w, bias, gain, dz, tb):
    t = z.shape[0]
    nb = t // tb
    ucol, vcol = (4 * DA + 2 * DB) // DC, (4 * DA + 2 * DB) // DC + 1
    ng = DC // GRP

    def body(u_ref, v_ref, doc_ref, w_ref, b_ref, g_ref, _, dz_ref, dw_ref, dbias_ref, dgain_ref, dbsum_scr):
        i = pl.program_id(0)

        @pl.when(i == 0)
        def _():
            for ref in (dw_ref, dgain_ref, dbsum_scr):
                ref[...] = jnp.zeros_like(ref)

        gmat = _group_matrix(DC, 1.0 / GRP).astype(MXU)
        uu, duu, dvv, rstd_v, vn, zz, ws, col, causal = _sgu_block(u_ref, v_ref, w_ref, b_ref, gmat, tb)
        y = uu * zz
        rstd = lax.rsqrt(_mm_exact_r(y * y, gmat) + EPS)
        n = y * rstd
        d_oc = doc_ref[...]
        dn = d_oc * g_ref[...]
        dgain_ref[0:1, :] += jnp.sum(d_oc * n, axis=0, keepdims=True)
        dy = rstd * (dn - n * _mm_exact_r(dn * n, gmat))
        dzz = dy * uu
        dz_ref[:, :DC] = (dy * zz * duu).astype(dz_ref.dtype)
        dvns = []
        for ch in range(tb // CCH):
            rows = slice(ch * CCH, (ch + 1) * CCH)
            dzz_c, vn_c = dzz[rows], vn[rows]
            dbsum_scr[...] += dzz_c
            dvn = jnp.zeros_like(dzz_c)
            for g in range(ng):
                sel = col == g
                dvn = dvn + jnp.where(sel, _mm_tn(ws[g], dzz_c), 0.0)
                dw_ref[g] += jnp.where(causal, _mm_nt(jnp.where(sel, dzz_c, 0.0), vn_c), 0.0)
            dvns.append(dvn)
        dvn = jnp.concatenate(dvns, axis=0)
        dv = rstd_v * (dvn - _mm_exact_r(dvn, gmat) - vn * _mm_exact_r(dvn * vn, gmat))
        dz_ref[:, DC:] = (dv * dvv).astype(dz_ref.dtype)

        @pl.when(i == nb - 1)
        def _():
            dbias_ref[...] = _mm_exact_r(dbsum_scr[...], _group_matrix(DC, 1.0).astype(MXU))

    const = lambda i: (0, 0)
    return pl.pallas_call(
        body, name="sgu_bwd", grid=(nb,),
        in_specs=[pl.BlockSpec((tb, DC), lambda i: (i, ucol)),
                  pl.BlockSpec((tb, DC), lambda i: (i, vcol)),
                  pl.BlockSpec((tb, DC), lambda i: (i, 0)),
                  pl.BlockSpec((ng, CCH, CCH), lambda i: (0, 0, 0)),
                  pl.BlockSpec((CCH, DC), const), pl.BlockSpec((1, DC), const),
                  pl.BlockSpec(memory_space=pl.ANY)],
        out_specs=[pl.BlockSpec((tb, 2 * DC), lambda i: (i, (4 * DA + 2 * DB) // (2 * DC))),
                   pl.BlockSpec((ng, CCH, CCH), lambda i: (0, 0, 0)),
                   pl.BlockSpec((CCH, DC), const), pl.BlockSpec((8, DC), const)],
        out_shape=[jax.ShapeDtypeStruct((t, DIN), SAVE), jax.ShapeDtypeStruct((ng, CCH, CCH), F32),
                   jax.ShapeDtypeStruct((CCH, DC), F32), jax.ShapeDtypeStruct((8, DC), F32)],
        scratch_shapes=[pltpu.VMEM((CCH, DC), F32)],
        input_output_aliases={6: 0},
        compiler_params=_params(("arbitrary",)),
    )(z, z, doc, w, bias, gain, dz)


def _head(h, gain, target, tm):
    t = h.shape[0]

    def body(h_ref, g_ref, t_ref, dh_ref, loss_ref, dgain_ref):
        @pl.when(pl.program_id(0) == 0)
        def _():
            loss_ref[...] = jnp.zeros_like(loss_ref)
            dgain_ref[...] = jnp.zeros_like(dgain_ref)

        hh = h_ref[...]
        gain = g_ref[...]
        rstd = lax.rsqrt(jnp.mean(hh * hh, axis=-1, keepdims=True) + EPS)
        xhat = hh * rstd
        err = xhat * gain - t_ref[...]
        per_tok = jnp.mean(err * err, axis=-1, keepdims=True)
        loss_ref[...] += 0.5 * jnp.sum(per_tok, axis=0, keepdims=True)
        dy = err * (1.0 / D)
        dgain_ref[...] += jnp.sum(dy * xhat, axis=0, keepdims=True)
        dxh = dy * gain
        dh_ref[...] = rstd * (dxh - xhat * jnp.mean(dxh * xhat, axis=-1, keepdims=True))

    return pl.pallas_call(
        body, name="head", grid=(t // tm,),
        in_specs=[pl.BlockSpec((tm, D), lambda i: (i, 0)),
                  pl.BlockSpec((1, D), lambda i: (0, 0)),
                  pl.BlockSpec((tm, D), lambda i: (i, 0))],
        out_specs=[pl.BlockSpec((tm, D), lambda i: (i, 0)),
                   pl.BlockSpec((1, 128), lambda i: (0, 0)),
                   pl.BlockSpec((1, D), lambda i: (0, 0))],
        out_shape=[jax.ShapeDtypeStruct((t, D), F32), jax.ShapeDtypeStruct((1, 128), F32),
                   jax.ShapeDtypeStruct((1, D), F32)],
        compiler_params=_params(("arbitrary",)),
    )(h, gain, target)


def _adamw(w, g, m, v):
    m = ADAM_B1 * m + (1.0 - ADAM_B1) * g
    v = ADAM_B2 * v + (1.0 - ADAM_B2) * (g * g)
    m_hat = m / (1.0 - ADAM_B1 ** ADAM_STEP)
    v_hat = v / (1.0 - ADAM_B2 ** ADAM_STEP)
    delta = -ADAM_LR * (m_hat / (jnp.sqrt(v_hat) + ADAM_EPS) + ADAM_WD * w)
    return delta, m, v


def _adamw_big(recv, w, m, v, tr, name, after, transposed=False):
    depth, rows, cols = w.shape
    rspec = (pl.BlockSpec((NDEV, cols, tr), lambda i: (0, 0, i)) if transposed
             else pl.BlockSpec((NDEV, tr, cols), lambda i: (0, i, 0)))

    def body(*refs):
        r_refs = refs[:depth]
        w_ref, m_ref, v_ref, _, g_out, d_out, m_out, v_out = refs[depth:]
        for l in range(depth):
            g = r_refs[l][0].astype(F32)
            for k in range(1, NDEV):
                g = g + r_refs[l][k].astype(F32)
            if transposed:
                g = g.T
            delta, m_, v_ = _adamw(w_ref[l], g, m_ref[l], v_ref[l])
            g_out[l] = g
            d_out[l] = delta
            m_out[l] = m_
            v_out[l] = v_

    spec = pl.BlockSpec((depth, tr, cols), lambda i: (0, i, 0))
    return pl.pallas_call(
        body, name=name, grid=(rows // tr,),
        in_specs=[rspec] * depth + [spec] * 3
        + [pl.BlockSpec(memory_space=pl.ANY)],
        out_specs=[spec] * 4, out_shape=[jax.ShapeDtypeStruct((depth, rows, cols), F32)] * 4,
        compiler_params=_params(("parallel",)),
    )(*recv, w, m, v, after)


def _sum_devices(recv):
    _, r, _ = recv.shape

    def body(r_ref, out_ref):
        g = r_ref[0]
        for k in range(1, NDEV):
            g = g + r_ref[k]
        out_ref[...] = g

    return pl.pallas_call(body, name="sum_devices", out_shape=jax.ShapeDtypeStruct((r, 128), F32))(recv)


def _adamw_small(w, g, m, v):
    def body(w_ref, g_ref, m_ref, v_ref, d_out, m_out, v_out):
        delta, m_, v_ = _adamw(w_ref[...], g_ref[...], m_ref[...], v_ref[...])
        d_out[...] = delta
        m_out[...] = m_
        v_out[...] = v_

    return pl.pallas_call(body, name="adamw_small", out_shape=[jax.ShapeDtypeStruct(w.shape, F32)] * 3)(w, g, m, v)


def _pack(arrs):
    flat = jnp.concatenate([a.reshape(-1) for a in arrs])
    pad = (-flat.shape[0]) % 1024
    return jnp.pad(flat, (0, pad)).reshape(-1, 128)


def _unpack(buf, like):
    flat = buf.reshape(-1)
    out, off = [], 0
    for a in like:
        out.append(flat[off:off + a.size].reshape(a.shape))
        off += a.size
    return out


def _block_diag(w):
    nb, bd, _ = w.shape
    eye = jnp.eye(nb, dtype=w.dtype)
    return (eye[:, None, :, None] * w[:, :, None, :]).reshape(nb * bd, nb * bd)


def _diag_blocks(w):
    nb = w.shape[0] // GRP
    return jnp.stack([w[g * GRP:(g + 1) * GRP, g * GRP:(g + 1) * GRP] for g in range(nb)])


SMALL = ['ffn1_norm', 'mix_norm', 'hgrn_lb_logits', 'hgrn_norm', 'conv_b', 'lru_wa', 'lru_ba', 'lru_wx', 'lru_bx',
         'lru_lambda', 'lru_norm', 'sgu_w', 'sgu_b', 'sgu_norm', 'ffn2_norm', 'final_norm']
NAMES = ['ffn1_norm', 'ffn1_wg', 'ffn1_wu', 'ffn1_wd', 'mix_norm', 'w_in', 'hgrn_lb_logits', 'hgrn_norm', 'conv_w',
         'conv_b', 'lru_wa', 'lru_ba', 'lru_wx', 'lru_bx', 'lru_lambda', 'lru_norm', 'sgu_w', 'sgu_b', 'sgu_norm',
         'w_out', 'ffn2_norm', 'ffn2_wg', 'ffn2_wu', 'ffn2_wd', 'final_norm']


def _step(x, target, w, m, v):
    depth = w['ffn1_wg'].shape[0]
    t = x.shape[1]
    h = x.reshape(t, D)
    target = target.reshape(t, D)
    tm_f, tm_b, tb = min(TM_F, t), min(TM_B, t), min(TB, t)
    my = 4 * lax.axis_index("x") + 2 * lax.axis_index("y") + lax.axis_index("c")

    cw_tile = jnp.pad(w['conv_w'].reshape(-1, 128), ((0, 8 - depth), (0, 0)))
    lbs, lb_soft = _lower_bounds(w['hgrn_lb_logits'])

    def row(a):
        return a.reshape(1, -1)

    def tr(a):
        return jnp.swapaxes(a, -1, -2)

    def shards(l, unit):
        if unit == 1:
            return [tr(w['w_in'][l]).astype(WIRE), w['w_out'][l].astype(WIRE)]
        f = 'ffn1' if unit == 0 else 'ffn2'
        return [tr(w[f + '_wg'][l]).astype(WIRE), tr(w[f + '_wu'][l]).astype(WIRE), w[f + '_wd'][l].astype(WIRE)]

    units = [(l, u) for l in range(depth) for u in range(3)]

    def start_ici(idx, deps=()):
        return _transfer_start(shards(*units[idx]), True, "gather_ici_%d_%d" % units[idx], deps=deps)

    def relay(idx, handle, after, then=None):
        lands = _transfer_wait(handle, after, "gather_ici_wait_%d_%d" % units[idx])
        more = shards(*units[then]) if then is not None and then < len(units) else ()
        return _forward_start(lands, "gather_d2d_%d_%d" % units[idx], more)

    pipe = dict(idx=0)
    first = start_ici(0)
    pipe['ici'] = start_ici(1, deps=(first['token'],))
    conv_flight = _transfer_start([cw_tile], True, "gather_conv_start", deps=(pipe['ici']['token'],), direct=True)
    pipe['d2d'], _ = relay(0, first, conv_flight['token'])

    def next_weights(after):
        idx = pipe['idx']
        lands = _transfer_wait(pipe['d2d'], after, "gather_d2d_wait_%d_%d" % units[idx])
        pipe['idx'] = idx + 1
        tok = 0.0
        if idx + 1 < len(units):
            pipe['d2d'], pipe['ici'] = relay(idx + 1, pipe['ici'], lands[-1], idx + 2)
            tok = pipe['d2d']['token'][0, 0]
        return lands, tok

    saved = []
    for l in range(depth):
        lands, tok = next_weights(h)
        s = dict(ffn1=[a.reshape(FF, D) for a in lands], h0=h)
        h, s['xn1'], s['a1'], s['b1'] = _ffn_fwd(h, row(w['ffn1_norm'][l]) + tok, *s['ffn1'], tm_f)
        s['h1'] = h
        (win, wout), tok = next_weights(h)
        win, wout = win.reshape(DIN, D), wout.reshape(D, D)
        s['win'], s['wout'] = win, wout
        z, s['xnm'] = _inproj_fwd(h, row(w['mix_norm'][l]) + tok, win, tm_f)
        s['z'] = z
        s['o'], oa, s['states'] = _hgrn_fwd(z, row(lbs[l]), row(w['hgrn_norm'][l]))
        if l == 0:
            cw_all = _transfer_wait(conv_flight, z, "gather_conv_wait")[0][:, :depth]
            conv_w = jnp.moveaxis(cw_all.reshape(NDEV, depth, 4, DB // NDEV), 0, 2).reshape(depth, 4, DB)
        s['cw'] = jnp.pad(conv_w[l], ((0, 4), (0, 0)))
        s['vec'] = jnp.concatenate([row(w['conv_b'][l]), row(w['lru_ba'][l]), row(w['lru_bx'][l]),
                                    row(w['lru_lambda'][l]), row(w['lru_norm'][l]), jnp.zeros((3, DB), F32)])
        s['wa'], s['wx'] = _block_diag(w['lru_wa'][l]), _block_diag(w['lru_wx'][l])
        ob, s['hseq'] = _lru_fwd(z, s['cw'], s['vec'], s['wa'], s['wx'], tb)
        s['bias'] = jnp.repeat(w['sgu_b'][l].T, GRP, axis=1)
        oc = _sgu_fwd(z, w['sgu_w'][l], s['bias'], row(w['sgu_norm'][l]), tb)
        s['oa'], s['ob'], s['oc'] = oa, ob, oc
        h = _outproj_fwd(h, oa, ob, oc, wout, tm_f)
        s['h2'] = h
        lands, tok = next_weights(h)
        s['ffn2'] = [a.reshape(FF, D) for a in lands]
        h, s['xn2'], s['a2'], s['b2'] = _ffn_fwd(h, row(w['ffn2_norm'][l]) + tok, *s['ffn2'], tm_f)
        saved.append(s)

    dh, loss_part, g_final = _head(h, row(w['final_norm']), target, tm_f)
    loss = lax.psum(loss_part[0, 0], ("x", "y", "c"))

    recv = {k: [None] * depth for k in ('wg1', 'wu1', 'wd1', 'wg2', 'wu2', 'wd2', 'win', 'wout')}
    flight = []

    def land(after):
        handle, kinds, l = flight.pop()
        for k, a in zip(kinds, _transfer_wait(handle, after, f"exchange_wait_{kinds[0]}_{l}")):
            recv[k][l] = a

    def exchange(arrs, kinds, l, deps=()):
        handle = _transfer_start(arrs, False, f"exchange_start_{kinds[0]}_{l}", deps=deps)
        if flight:
            land(handle['token'])
        flight.append((handle, kinds, l))
        return handle['token'][0, 0]

    small = {k: [None] * depth for k in SMALL if k != 'final_norm'}
    dconv = [None] * depth
    dlb = [None] * depth
    tok = 0.0
    for l in reversed(range(depth)):
        s = saved[l]
        dh, g, *cot = _ffn_bwd_x(dh, s['h2'], row(w['ffn2_norm'][l]) + tok, s['a2'], s['b2'], *s['ffn2'], tm_b)
        dws = _ffn_bwd_w(s['xn2'], *cot, tm_b)
        tok = exchange([a.reshape(NDEV, FFS, D) for a in dws], ('wg2', 'wu2', 'wd2'), l)
        small['ffn2_norm'][l] = g
        doa, dob, doc, dwout = _outproj_bwd(dh, s['oa'], s['ob'], s['oc'], s['wout'], tm_f)
        dz, g_hn, dlb[l] = _hgrn_bwd(s['z'], row(lbs[l]), row(w['hgrn_norm'][l]) + tok, s['o'], s['states'], doa)
        small['hgrn_norm'][l] = g_hn
        dz, dcw, dvec, dwa, dwx = _lru_bwd(s['z'], s['hseq'], dob, s['cw'], s['vec'], s['wa'], s['wx'], dz, tb)
        dconv[l] = dcw[:4]
        small['conv_b'][l], small['lru_ba'][l], small['lru_bx'][l] = dvec[0], dvec[1].reshape(4, GRP), dvec[2].reshape(4, GRP)
        small['lru_lambda'][l], small['lru_norm'][l] = dvec[3], dvec[4]
        small['lru_wa'][l], small['lru_wx'][l] = _diag_blocks(dwa), _diag_blocks(dwx)
        dz, dsw, dbias, dgc = _sgu_bwd(s['z'], doc, w['sgu_w'][l], s['bias'], row(w['sgu_norm'][l]), dz, tb)
        small['sgu_w'][l], small['sgu_b'][l], small['sgu_norm'][l] = dsw, dbias[:, ::GRP].T, dgc[0]
        dwin = _inproj_bwd_w(s['xnm'], dz, tm_f)
        tok = exchange([dwin.reshape(NDEV, DINS, D), dwout.reshape(NDEV, D // NDEV, D)], ('win', 'wout'), l)
        dh, g = _inproj_bwd_x(dh, dz, s['h1'], row(w['mix_norm'][l]) + tok, s['win'], tm_b)
        small['mix_norm'][l] = g
        tok = 0.0
        if l == 0:
            small['ffn1_norm'][0] = jnp.zeros((1, D), F32)
            small['hgrn_lb_logits'] = list(_lower_bounds_bwd(lb_soft, jnp.concatenate(dlb, axis=0)))
            parts = [jnp.stack([small[k][j].reshape(w[k].shape[1:]) for j in range(depth)])
                     for k in SMALL if k != 'final_norm']
            parts += [g_final.reshape(D), jnp.stack(dconv)]
            small_flight = _transfer_start([_pack(parts)], True, "gather_small_start", direct=True)
            tok = small_flight['token'][0, 0]
        dh, g, *cot = _ffn_bwd_x(dh, s['h0'], row(w['ffn1_norm'][l]) + tok, s['a1'], s['b1'], *s['ffn1'], tm_b)
        dws = _ffn_bwd_w(s['xn1'], *cot, tm_b)
        before = ()
        if l == 0:
            g_last = _all_gather([g.reshape(8, 128)], "gather_last")[0]
            before = (g_last,)
        else:
            small['ffn1_norm'][l] = g
        tok = exchange([a.reshape(NDEV, FFS, D) for a in dws], ('wg1', 'wu1', 'wd1'), l, before)
    grad_x = dh.reshape(1, t, D)

    out = {}
    last = flight[0][0]['token']

    def ffn_update(f, n, after):
        for kind in ('wg', 'wu'):
            k = f + '_' + kind
            res = _adamw_big(recv[kind + n], tr(w[k]), tr(m[k]), tr(v[k]), 32, "adamw_ffn", after)
            out[k] = tuple(tr(a) for a in res)
        k = f + '_wd'
        out[k] = _adamw_big(recv['wd' + n], w[k], m[k], v[k], 32, "adamw_ffn", after)

    ffn_update('ffn2', '2', last)
    out['w_in'] = _adamw_big(recv['win'], w['w_in'], m['w_in'], v['w_in'], 128, "adamw_win", last, transposed=True)
    out['w_out'] = _adamw_big(recv['wout'], w['w_out'], m['w_out'], v['w_out'], 64, "adamw_wout", last)

    total = _sum_devices(_transfer_wait(small_flight, g_last, "gather_small_wait")[0])
    like = [w[k] for k in SMALL] + [jax.ShapeDtypeStruct((depth, 4, DB), F32)]
    grads = _unpack(total, like)
    gsmall = dict(zip(SMALL, grads[:-1]))
    gsmall['ffn1_norm'] = gsmall['ffn1_norm'].at[0].set(_sum_devices(g_last).reshape(D))
    gsmall['conv_w'] = lax.dynamic_slice_in_dim(grads[-1], my * (DB // NDEV), DB // NDEV, axis=2)
    keys = SMALL + ['conv_w']
    dl, mm, vv = _adamw_small(_pack([w[k] for k in keys]), _pack([gsmall[k] for k in keys]),
                              _pack([m[k] for k in keys]), _pack([v[k] for k in keys]))
    like = [w[k] for k in keys]
    for k, d_, m_, v_ in zip(keys, _unpack(dl, like), _unpack(mm, like), _unpack(vv, like)):
        out[k] = (gsmall[k], d_, m_, v_)
    done = [dl] + [out[k][1][0] for k in ('ffn2_wg', 'ffn2_wu', 'ffn2_wd', 'w_in', 'w_out')]
    land(functools.reduce(lambda p, q: p + q, [a[:1, :1] for a in done]))
    ffn_update('ffn1', '1', last)

    return (loss, grad_x, *[out[k][0] for k in NAMES], *[out[k][1] for k in NAMES],
            *[out[k][2] for k in NAMES], *[out[k][3] for k in NAMES])


def kernel(x, ffn1_norm, ffn1_wg, ffn1_wu, ffn1_wd, mix_norm, w_in, hgrn_lb_logits, hgrn_norm, conv_w, conv_b, lru_wa, lru_ba, lru_wx, lru_bx, lru_lambda, lru_norm, sgu_w, sgu_b, sgu_norm, w_out, ffn2_norm, ffn2_wg, ffn2_wu, ffn2_wd, final_norm, loss_target, m_ffn1_norm, m_ffn1_wg, m_ffn1_wu, m_ffn1_wd, m_mix_norm, m_w_in, m_hgrn_lb_logits, m_hgrn_norm, m_conv_w, m_conv_b, m_lru_wa, m_lru_ba, m_lru_wx, m_lru_bx, m_lru_lambda, m_lru_norm, m_sgu_w, m_sgu_b, m_sgu_norm, m_w_out, m_ffn2_norm, m_ffn2_wg, m_ffn2_wu, m_ffn2_wd, m_final_norm, v_ffn1_norm, v_ffn1_wg, v_ffn1_wu, v_ffn1_wd, v_mix_norm, v_w_in, v_hgrn_lb_logits, v_hgrn_norm, v_conv_w, v_conv_b, v_lru_wa, v_lru_ba, v_lru_wx, v_lru_bx, v_lru_lambda, v_lru_norm, v_sgu_w, v_sgu_b, v_sgu_norm, v_w_out, v_ffn2_norm, v_ffn2_wg, v_ffn2_wu, v_ffn2_wd, v_final_norm):
    args = locals()
    w = {k: args[k] for k in NAMES}
    m = {k: args['m_' + k] for k in NAMES}
    v = {k: args['v_' + k] for k in NAMES}
    return _step(x, loss_target, w, m, v)
```

```python
import functools

import jax
import jax.numpy as jnp
from jax import lax
from jax.experimental import pallas as pl
from jax.experimental.pallas import tpu as pltpu

F32 = jnp.float32
MXU = jnp.bfloat16
SAVE = jnp.bfloat16
WIRE = jnp.bfloat16

NDEV = 8
D = 1024
FF = 2816
FFS = FF // NDEV
FB = 256
FBX = FF // 2
DIN = 3072
DINS = DIN // NDEV
ZB = 512
ZBW = 1024
DA, DB, DC = 512, 256, 256
HD = 128
NH = DA // HD
ACH = 64
ACB = 4
CCH = 128
GRP = 64
EPS = 1e-6
LRU_C = 8.0
VMEM_LIMIT = 60 * 1024 * 1024
TM_F = 1024
TM_B = 512
TB = 1024
SUB = 512
SUB_X = 256

ADAM_LR, ADAM_B1, ADAM_B2, ADAM_EPS, ADAM_WD, ADAM_STEP = 0.001, 0.9, 0.999, 1e-08, 0.01, 10

MESH = pl.DeviceIdType.MESH


def _mm(a, b):
    return jnp.dot(a.astype(MXU), b.astype(MXU), preferred_element_type=F32)


def _mm_nt(a, b):
    return lax.dot_general(a.astype(MXU), b.astype(MXU), (((1,), (1,)), ((), ())), preferred_element_type=F32)


def _mm_tn(a, b):
    return lax.dot_general(a.astype(MXU), b.astype(MXU), (((0,), (0,)), ((), ())), preferred_element_type=F32)


def _split3(x):
    x1 = x.astype(MXU)
    r1 = x - x1.astype(F32)
    x2 = r1.astype(MXU)
    r2 = r1 - x2.astype(F32)
    return x1, x2, r2.astype(MXU)


def _mm_exact_l(c, x):
    x1, x2, x3 = _split3(x)
    return _mm(c, x1) + _mm(c, x2) + _mm(c, x3)


def _mm_exact_r(x, c):
    x1, x2, x3 = _split3(x)
    return _mm(x1, c) + _mm(x2, c) + _mm(x3, c)


def _sigmoid(x):
    return 1.0 / (1.0 + jnp.exp(-x))


def _gelu(x):
    c, k = 0.7978845608028654, 0.044715
    th = jnp.tanh(c * (x + k * x * x * x))
    return 0.5 * x * (1.0 + th)


def _gelu_and_grad(x):
    c, k = 0.7978845608028654, 0.044715
    th = jnp.tanh(c * (x + k * x * x * x))
    g = 0.5 * x * (1.0 + th)
    dg = 0.5 * (1.0 + th) + 0.5 * x * (1.0 - th * th) * c * (1.0 + 3.0 * k * x * x)
    return g, dg


def _expm1(x):
    series = x * (1.0 + x * (0.5 + x * (1.0 / 6.0 + x * (1.0 / 24.0 + x * (1.0 / 120.0)))))
    return jnp.where(jnp.abs(x) < 0.05, series, jnp.exp(x) - 1.0)


def _iota(shape, dim):
    return lax.broadcasted_iota(jnp.int32, shape, dim)


def _group_matrix(n, value):
    r, c = _iota((n, n), 0), _iota((n, n), 1)
    return jnp.where((r // GRP) == (c // GRP), value, 0.0).astype(F32)


def _row(x, k):
    r = _iota(x.shape, 0)
    return jnp.sum(jnp.where(r == k, x, 0.0), axis=0, keepdims=True)


def _rms_bwd(dxn, hh, gain):
    rstd = lax.rsqrt(jnp.mean(hh * hh, axis=-1, keepdims=True) + EPS)
    xhat = hh * rstd
    dxh = dxn * gain
    dh = rstd * (dxh - xhat * jnp.mean(dxh * xhat, axis=-1, keepdims=True))
    return dh, jnp.sum(dxn * xhat, axis=0, keepdims=True)


def _params(sem):
    return pltpu.CompilerParams(dimension_semantics=sem, vmem_limit_bytes=VMEM_LIMIT)


def _all_gather(arrs, name):
    n = len(arrs)

    def body(*refs):
        ins, outs = refs[:n], refs[n:2 * n]
        send_sems, recv_sems, local_sems = refs[2 * n:]
        x, y, c = lax.axis_index("x"), lax.axis_index("y"), lax.axis_index("c")
        me, sibling = (x, y, c), (x, y, 1 - c)
        chips = [(1 - x, y), (x, 1 - y), (1 - x, 1 - y)]

        def slot(px, py, pc):
            return 4 * px + 2 * py + pc

        def copy(a, k, block, to, src=None):
            dst = outs[a].at[slot(*block)]
            return pltpu.make_async_remote_copy(
                src_ref=dst if src is None else src, dst_ref=dst,
                send_sem=send_sems.at[a * 7 + k], recv_sem=recv_sems.at[a * 7 + k],
                device_id=to, device_id_type=MESH)

        started = []
        for a in range(n):
            mine = pltpu.make_async_copy(ins[a], outs[a].at[slot(*me)], local_sems.at[a])
            mine.start()
            started.append(mine)
        first = []
        for a in range(n):
            first.append(copy(a, 0, me, sibling, src=ins[a]))
            first += [copy(a, 1 + j, me, (*chip, c), src=ins[a]) for j, chip in enumerate(chips)]
        for cp in first:
            cp.start()
        passed = []
        for a in range(n):
            for j, chip in enumerate(chips):
                copy(a, 1 + j, (*chip, c), me).wait_recv()
                fwd = copy(a, 4 + j, (*chip, c), sibling)
                fwd.start()
                passed.append(fwd)
        for a in range(n):
            copy(a, 0, sibling, me).wait_recv()
            for j, chip in enumerate(chips):
                copy(a, 4 + j, (*chip, 1 - c), me).wait_recv()
        for cp in first + passed:
            cp.wait_send()
        for mine in started:
            mine.wait()

    hbm = pl.BlockSpec(memory_space=pl.ANY)
    return pl.pallas_call(
        body, name=name,
        out_shape=[jax.ShapeDtypeStruct((NDEV,) + a.shape, a.dtype) for a in arrs],
        in_specs=[hbm] * n, out_specs=[hbm] * n,
        scratch_shapes=[pltpu.SemaphoreType.DMA((7 * n,)), pltpu.SemaphoreType.DMA((7 * n,)),
                        pltpu.SemaphoreType.DMA((n,))],
    )(*arrs)


def _peers():
    x, y, c = lax.axis_index("x"), lax.axis_index("y"), lax.axis_index("c")
    peers = [(x ^ ((k >> 2) & 1), y ^ ((k >> 1) & 1), c ^ (k & 1)) for k in range(1, NDEV)]
    return (x, y, c), 4 * x + 2 * y + c, peers


_HBM = pl.BlockSpec(memory_space=pltpu.HBM)
_SEM = pl.BlockSpec(memory_space=pltpu.SEMAPHORE)
_EFFECT = pltpu.SideEffectType.DATAFLOW_SIDE_EFFECTING


def _transfer_start(arrs, gather, name, deps=(), direct=False):
    n, nd = len(arrs), len(deps)
    shapes = [((NDEV,) + a.shape) if gather else a.shape for a in arrs]

    def body(*refs):
        ins, lands = refs[:n], refs[n:2 * n]
        send_sems, recv_sems, local_sems = refs[2 * n + nd:2 * n + nd + 3]
        token = refs[-1]
        (x, y, c), my, peers = _peers()
        if gather and not direct:
            peers = [(x, y, 1 - c), (1 - x, y, c), (x, 1 - y, c), (1 - x, 1 - y, c)]
        for a in range(n):
            own = ins[a] if gather else ins[a].at[my]
            pltpu.make_async_copy(own, lands[a].at[my], local_sems.at[a]).start()
        for a in range(n):
            for peer in peers:
                src = ins[a] if gather else ins[a].at[4 * peer[0] + 2 * peer[1] + peer[2]]
                pltpu.make_async_remote_copy(
                    src_ref=src, dst_ref=lands[a].at[my], send_sem=send_sems.at[a], recv_sem=recv_sems.at[a],
                    device_id=peer, device_id_type=MESH).start()
        token[...] = jnp.zeros_like(token)

    out_shape = [pltpu.SemaphoreType.DMA((n,))] * 3
    out_shape += [pltpu.HBM(a.shape, a.dtype) for a in arrs]
    out_shape += [pltpu.HBM(s, a.dtype) for s, a in zip(shapes, arrs)]
    out_shape += [jax.ShapeDtypeStruct((8, 128), F32)]
    operands = [pltpu.with_memory_space_constraint(a, pltpu.HBM) for a in arrs]
    operands += [pltpu.with_memory_space_constraint(lax.empty(s, a.dtype), pltpu.HBM) for s, a in zip(shapes, arrs)]
    res = pl.pallas_call(
        body, name=name, out_shape=out_shape,
        in_specs=[_HBM] * (2 * n) + [pl.BlockSpec(memory_space=pl.ANY)] * nd,
        out_specs=[_SEM] * 3 + [_HBM] * (2 * n) + [pl.BlockSpec(memory_space=pltpu.VMEM)],
        input_output_aliases={i: 3 + i for i in range(2 * n)},
        compiler_params=pltpu.CompilerParams(has_side_effects=_EFFECT),
    )(*operands, *deps)
    return dict(sems=res[:3], src=res[3:3 + n], lands=res[3 + n:3 + 2 * n], token=res[-1], n=n,
                count=4 if gather and not direct else NDEV - 1)


def _forward_start(lands, name, shards=()):
    n, m = len(lands), len(shards)
    zones = [(NDEV,) + a.shape for a in shards]

    def body(*refs):
        zone, ins, fresh = refs[:n], refs[n:n + m], refs[n + m:n + 2 * m]
        sems = refs[n + 2 * m:n + 2 * m + (5 if m else 2)]
        token = refs[-1]
        (x, y, c), my, _ = _peers()
        for a in range(n):
            for px, py in ((1 - x, y), (x, 1 - y), (1 - x, 1 - y)):
                block = zone[a].at[4 * px + 2 * py + c]
                pltpu.make_async_remote_copy(
                    src_ref=block, dst_ref=block, send_sem=sems[0].at[a], recv_sem=sems[1].at[a],
                    device_id=(x, y, 1 - c), device_id_type=MESH).start()
        for a in range(m):
            pltpu.make_async_copy(ins[a], fresh[a].at[my], sems[4].at[a]).start()
            for peer in ((x, y, 1 - c), (1 - x, y, c), (x, 1 - y, c), (1 - x, 1 - y, c)):
                pltpu.make_async_remote_copy(
                    src_ref=ins[a], dst_ref=fresh[a].at[my], send_sem=sems[2].at[a], recv_sem=sems[3].at[a],
                    device_id=peer, device_id_type=MESH).start()
        token[...] = jnp.zeros_like(token)

    sem_shapes = [pltpu.SemaphoreType.DMA((n,))] * 2 + ([pltpu.SemaphoreType.DMA((m,))] * 3 if m else [])
    ns = len(sem_shapes)
    thru = list(lands) + list(shards)
    operands = thru + [pltpu.with_memory_space_constraint(lax.empty(s, a.dtype), pltpu.HBM)
                       for s, a in zip(zones, shards)]
    res = pl.pallas_call(
        body, name=name,
        out_shape=sem_shapes + [pltpu.HBM(a.shape, a.dtype) for a in thru]
        + [pltpu.HBM(s, a.dtype) for s, a in zip(zones, shards)] + [jax.ShapeDtypeStruct((8, 128), F32)],
        in_specs=[_HBM] * (n + 2 * m),
        out_specs=[_SEM] * ns + [_HBM] * (n + 2 * m) + [pl.BlockSpec(memory_space=pltpu.VMEM)],
        input_output_aliases={i: ns + i for i in range(n + 2 * m)},
        compiler_params=pltpu.CompilerParams(has_side_effects=_EFFECT),
    )(*[pltpu.with_memory_space_constraint(a, pltpu.HBM) for a in thru], *operands[n + m:])
    forward = dict(sems=res[:2], src=[], lands=res[ns:ns + n], token=res[-1], n=n, count=3)
    nxt = None
    if m:
        nxt = dict(sems=res[2:5], src=res[ns + n:ns + n + m], lands=res[ns + n + m:ns + n + 2 * m],
                   token=res[-1], n=m, count=4)
    return forward, nxt


def _transfer_wait(handle, after, name):
    n, count = handle["n"], handle["count"]
    src, lands, sems = list(handle["src"]), list(handle["lands"]), list(handle["sems"])
    ns = len(src)

    def body(*refs):
        zone = refs[ns:ns + n]
        sem_refs = refs[ns + n:ns + n + len(sems)]
        me, _, _ = _peers()
        for a in range(n):
            moved = zone[a].at[pl.ds(0, count)]
            both = pltpu.make_async_remote_copy(
                src_ref=moved, dst_ref=moved, send_sem=sem_refs[0].at[a], recv_sem=sem_refs[1].at[a],
                device_id=me, device_id_type=MESH)
            both.wait_send()
            both.wait_recv()
            if len(sems) == 3:
                pltpu.make_async_copy(zone[a].at[0], zone[a].at[1], sem_refs[2].at[a]).wait()

    res = pl.pallas_call(
        body, name=name,
        out_shape=[pltpu.HBM(a.shape, a.dtype) for a in src + lands],
        in_specs=[_HBM] * (ns + n) + [_SEM] * len(sems) + [pl.BlockSpec(memory_space=pl.ANY)],
        out_specs=[_HBM] * (ns + n),
        input_output_aliases={i: i for i in range(ns + n)},
        compiler_params=pltpu.CompilerParams(has_side_effects=_EFFECT),
    )(*src, *lands, *sems, after)
    return list(res[ns:])


def _ffn_fwd(h, gain, wg, wu, wd, tm):
    t = h.shape[0]
    nj = FF // FBX

    def body(h_ref, g_ref, wg_ref, wu_ref, wd_ref, out_ref, xn_ref, a_ref, b_ref, acc_ref):
        j = pl.program_id(1)

        @pl.when(j == 0)
        def _():
            hh = h_ref[...]
            rstd = lax.rsqrt(jnp.mean(hh * hh, axis=-1, keepdims=True) + EPS)
            xn_ref[...] = (hh * rstd * g_ref[...]).astype(xn_ref.dtype)
            acc_ref[...] = jnp.zeros_like(acc_ref)

        sub = min(SUB, tm)
        for r in range(tm // sub):
            rows = slice(r * sub, (r + 1) * sub)
            xn = xn_ref[rows, :]
            y = None
            for c0 in range(0, FBX, FB):
                cols = slice(c0, min(c0 + FB, FBX))
                a = _mm_nt(xn, wg_ref[cols, :])
                b = _mm_nt(xn, wu_ref[cols, :])
                a_ref[rows, cols] = a.astype(a_ref.dtype)
                b_ref[rows, cols] = b.astype(b_ref.dtype)
                part = _mm(a * _sigmoid(a) * b, wd_ref[cols, :])
                y = part if y is None else y + part
            acc_ref[rows, :] += y

        @pl.when(j == nj - 1)
        def _():
            out_ref[...] = h_ref[...] + 0.5 * acc_ref[...]

    wspec = pl.BlockSpec((FBX, D), lambda i, j: (j, 0))
    return pl.pallas_call(
        body, name="ffn_fwd", grid=(t // tm, nj),
        in_specs=[pl.BlockSpec((tm, D), lambda i, j: (i, 0)),
                  pl.BlockSpec((1, D), lambda i, j: (0, 0)), wspec, wspec, wspec],
        out_specs=[pl.BlockSpec((tm, D), lambda i, j: (i, 0)),
                   pl.BlockSpec((tm, D), lambda i, j: (i, 0)),
                   pl.BlockSpec((tm, FBX), lambda i, j: (i, j)),
                   pl.BlockSpec((tm, FBX), lambda i, j: (i, j))],
        out_shape=[jax.ShapeDtypeStruct((t, D), F32), jax.ShapeDtypeStruct((t, D), SAVE),
                   jax.ShapeDtypeStruct((t, FF), SAVE), jax.ShapeDtypeStruct((t, FF), SAVE)],
        scratch_shapes=[pltpu.VMEM((tm, D), F32)],
        compiler_params=_params(("parallel", "arbitrary")),
    )(h, gain, wg, wu, wd)


def _ffn_bwd_x(dout, h, gain, a_sv, b_sv, wg, wu, wd, tm):
    t = h.shape[0]
    nj = FF // FBX

    def body(dout_ref, h_ref, g_ref, a_ref, b_ref, wg_ref, wu_ref, wd_ref,
             dh_ref, dgain_ref, dy_ref, da_ref, db_ref, s_ref, acc_ref):
        i, j = pl.program_id(0), pl.program_id(1)

        @pl.when((i == 0) & (j == 0))
        def _():
            dgain_ref[...] = jnp.zeros_like(dgain_ref)

        @pl.when(j == 0)
        def _():
            dy_ref[...] = (0.5 * dout_ref[...]).astype(dy_ref.dtype)
            acc_ref[...] = jnp.zeros_like(acc_ref)

        sub = min(SUB_X, tm)
        for r in range(tm // sub):
            rows = slice(r * sub, (r + 1) * sub)
            dy = dy_ref[rows, :]
            dx = None
            for c0 in range(0, FBX, FB):
                cols = slice(c0, min(c0 + FB, FBX))
                ds = _mm_nt(dy, wd_ref[cols, :])
                a, b = a_ref[rows, cols].astype(F32), b_ref[rows, cols].astype(F32)
                sg = _sigmoid(a)
                sa = a * sg
                da = (ds * b * (sg * (1.0 + a * (1.0 - sg)))).astype(MXU)
                db = (ds * sa).astype(MXU)
                da_ref[rows, cols] = da.astype(da_ref.dtype)
                db_ref[rows, cols] = db.astype(db_ref.dtype)
                s_ref[rows, cols] = (sa * b).astype(s_ref.dtype)
                part = _mm(da, wg_ref[cols, :]) + _mm(db, wu_ref[cols, :])
                dx = part if dx is None else dx + part
            acc_ref[rows, :] += dx

        @pl.when(j == nj - 1)
        def _():
            dh, dg = _rms_bwd(acc_ref[...], h_ref[...], g_ref[...])
            dh_ref[...] = dout_ref[...] + dh
            dgain_ref[...] += dg

    tok = pl.BlockSpec((tm, D), lambda i, j: (i, 0))
    act = pl.BlockSpec((tm, FBX), lambda i, j: (i, j))
    wspec = pl.BlockSpec((FBX, D), lambda i, j: (j, 0))
    return pl.pallas_call(
        body, name="ffn_bwd_x", grid=(t // tm, nj),
        in_specs=[tok, tok, pl.BlockSpec((1, D), lambda i, j: (0, 0)), act, act, wspec, wspec, wspec],
        out_specs=[tok, pl.BlockSpec((1, D), lambda i, j: (0, 0)), tok, act, act, act],
        out_shape=[jax.ShapeDtypeStruct((t, D), F32), jax.ShapeDtypeStruct((1, D), F32),
                   jax.ShapeDtypeStruct((t, D), SAVE)] + [jax.ShapeDtypeStruct((t, FF), SAVE)] * 3,
        scratch_shapes=[pltpu.VMEM((tm, D), F32)],
        compiler_params=_params(("arbitrary", "arbitrary")),
    )(dout, h, gain, a_sv, b_sv, wg, wu, wd)


def _ffn_bwd_w(xn, dy, da, db, s, tm):
    t = xn.shape[0]
    nt = t // tm
    nj = FF // FBX

    def body(xn_ref, dy_ref, da_ref, db_ref, s_ref, dwg_ref, dwu_ref, dwd_ref, ag_scr, au_scr, ad_scr):
        i = pl.program_id(1)

        @pl.when(i == 0)
        def _():
            for ref in (ag_scr, au_scr, ad_scr):
                ref[...] = jnp.zeros_like(ref)

        xn, dy = xn_ref[...], dy_ref[...]
        for c0 in range(0, FBX, FB):
            rows = slice(c0, min(c0 + FB, FBX))
            ag_scr[rows, :] += _mm_tn(da_ref[:, rows], xn)
            au_scr[rows, :] += _mm_tn(db_ref[:, rows], xn)
            ad_scr[rows, :] += _mm_tn(s_ref[:, rows], dy)

        @pl.when(i == nt - 1)
        def _():
            for out, ref in ((dwg_ref, ag_scr), (dwu_ref, au_scr), (dwd_ref, ad_scr)):
                out[...] = ref[...].astype(out.dtype)

    tok = pl.BlockSpec((tm, D), lambda j, i: (i, 0))
    act = pl.BlockSpec((tm, FBX), lambda j, i: (i, j))
    wspec = pl.BlockSpec((FBX, D), lambda j, i: (j, 0))
    return pl.pallas_call(
        body, name="ffn_bwd_w", grid=(nj, nt),
        in_specs=[tok, tok, act, act, act], out_specs=[wspec] * 3,
        out_shape=[jax.ShapeDtypeStruct((FF, D), WIRE)] * 3,
        scratch_shapes=[pltpu.VMEM((FBX, D), F32)] * 3,
        compiler_params=_params(("parallel", "arbitrary")),
    )(xn, dy, da, db, s)


def _inproj_fwd(h, gain, win, tm):
    t = h.shape[0]

    def body(h_ref, g_ref, w_ref, z_ref, xn_ref):
        hh = h_ref[...]
        rstd = lax.rsqrt(jnp.mean(hh * hh, axis=-1, keepdims=True) + EPS)
        xn = (hh * rstd * g_ref[...]).astype(MXU)
        xn_ref[...] = xn.astype(xn_ref.dtype)
        for j in range(DIN // ZB):
            z_ref[:, j * ZB:(j + 1) * ZB] = _mm_nt(xn, w_ref[j * ZB:(j + 1) * ZB, :])

    return pl.pallas_call(
        body, name="inproj_fwd", grid=(t // tm,),
        in_specs=[pl.BlockSpec((tm, D), lambda i: (i, 0)),
                  pl.BlockSpec((1, D), lambda i: (0, 0)),
                  pl.BlockSpec((DIN, D), lambda i: (0, 0))],
        out_specs=[pl.BlockSpec((tm, DIN), lambda i: (i, 0)),
                   pl.BlockSpec((tm, D), lambda i: (i, 0))],
        out_shape=[jax.ShapeDtypeStruct((t, DIN), F32), jax.ShapeDtypeStruct((t, D), SAVE)],
        compiler_params=_params(("parallel",)),
    )(h, gain, win)


def _inproj_bwd_x(dres, dz, h, gain, win, tm):
    t = h.shape[0]

    def body(dres_ref, dz_ref, h_ref, g_ref, w_ref, dh_ref, dgain_ref):
        @pl.when(pl.program_id(0) == 0)
        def _():
            dgain_ref[...] = jnp.zeros_like(dgain_ref)

        dh, dg = _rms_bwd(_mm(dz_ref[...], w_ref[...]), h_ref[...], g_ref[...])
        dh_ref[...] = dres_ref[...] + dh
        dgain_ref[...] += dg

    return pl.pallas_call(
        body, name="inproj_bwd_x", grid=(t // tm,),
        in_specs=[pl.BlockSpec((tm, D), lambda i: (i, 0)),
                  pl.BlockSpec((tm, DIN), lambda i: (i, 0)),
                  pl.BlockSpec((tm, D), lambda i: (i, 0)),
                  pl.BlockSpec((1, D), lambda i: (0, 0)),
                  pl.BlockSpec((DIN, D), lambda i: (0, 0))],
        out_specs=[pl.BlockSpec((tm, D), lambda i: (i, 0)),
                   pl.BlockSpec((1, D), lambda i: (0, 0))],
        out_shape=[jax.ShapeDtypeStruct((t, D), F32), jax.ShapeDtypeStruct((1, D), F32)],
        compiler_params=_params(("arbitrary",)),
    )(dres, dz, h, gain, win)


def _inproj_bwd_w(xn, dz, tm):
    t = xn.shape[0]
    nt = t // tm

    def body(xn_ref, dz_ref, dw_ref, acc_scr):
        i = pl.program_id(1)

        @pl.when(i == 0)
        def _():
            acc_scr[...] = jnp.zeros_like(acc_scr)

        acc_scr[...] += _mm_tn(dz_ref[...], xn_ref[...])

        @pl.when(i == nt - 1)
        def _():
            dw_ref[...] = acc_scr[...].astype(dw_ref.dtype)

    return pl.pallas_call(
        body, name="inproj_bwd_w", grid=(DIN // ZBW, nt),
        in_specs=[pl.BlockSpec((tm, D), lambda j, i: (i, 0)),
                  pl.BlockSpec((tm, ZBW), lambda j, i: (i, j))],
        out_specs=pl.BlockSpec((ZBW, D), lambda j, i: (j, 0)),
        out_shape=jax.ShapeDtypeStruct((DIN, D), WIRE),
        scratch_shapes=[pltpu.VMEM((ZBW, D), F32)],
        compiler_params=_params(("parallel", "arbitrary")),
    )(xn, dz)


def _outproj_fwd(h, oa, ob, oc, wout, tm):
    t = h.shape[0]

    def body(h_ref, oa_ref, ob_ref, oc_ref, w_ref, out_ref):
        ym = jnp.concatenate([oa_ref[...], ob_ref[...], oc_ref[...]], axis=1)
        out_ref[...] = h_ref[...] + _mm(ym, w_ref[...])

    return pl.pallas_call(
        body, name="outproj_fwd", grid=(t // tm,),
        in_specs=[pl.BlockSpec((tm, D), lambda i: (i, 0)),
                  pl.BlockSpec((tm, DA), lambda i: (i, 0)),
                  pl.BlockSpec((tm, DB), lambda i: (i, 0)),
                  pl.BlockSpec((tm, DC), lambda i: (i, 0)),
                  pl.BlockSpec((D, D), lambda i: (0, 0))],
        out_specs=pl.BlockSpec((tm, D), lambda i: (i, 0)),
        out_shape=jax.ShapeDtypeStruct((t, D), F32),
        compiler_params=_params(("parallel",)),
    )(h, oa, ob, oc, wout)


def _outproj_bwd(dh, oa, ob, oc, wout, tm):
    t = dh.shape[0]
    nt = t // tm

    def body(dh_ref, oa_ref, ob_ref, oc_ref, w_ref, da_ref, db_ref, dc_ref, dw_ref, acc_scr):
        i = pl.program_id(0)

        @pl.when(i == 0)
        def _():
            acc_scr[...] = jnp.zeros_like(acc_scr)

        d16 = dh_ref[...].astype(MXU)
        dym = _mm_nt(d16, w_ref[...])
        da_ref[...] = dym[:, :DA]
        db_ref[...] = dym[:, DA:DA + DB]
        dc_ref[...] = dym[:, DA + DB:]
        ym = jnp.concatenate([oa_ref[...], ob_ref[...], oc_ref[...]], axis=1)
        acc_scr[...] += _mm_tn(ym, d16)

        @pl.when(i == nt - 1)
        def _():
            dw_ref[...] = acc_scr[...].astype(dw_ref.dtype)

    return pl.pallas_call(
        body, name="outproj_bwd", grid=(nt,),
        in_specs=[pl.BlockSpec((tm, D), lambda i: (i, 0)),
                  pl.BlockSpec((tm, DA), lambda i: (i, 0)),
                  pl.BlockSpec((tm, DB), lambda i: (i, 0)),
                  pl.BlockSpec((tm, DC), lambda i: (i, 0)),
                  pl.BlockSpec((D, D), lambda i: (0, 0))],
        out_specs=[pl.BlockSpec((tm, DA), lambda i: (i, 0)),
                   pl.BlockSpec((tm, DB), lambda i: (i, 0)),
                   pl.BlockSpec((tm, DC), lambda i: (i, 0)),
                   pl.BlockSpec((D, D), lambda i: (0, 0))],
        out_shape=[jax.ShapeDtypeStruct((t, DA), F32), jax.ShapeDtypeStruct((t, DB), F32),
                   jax.ShapeDtypeStruct((t, DC), F32), jax.ShapeDtypeStruct((D, D), WIRE)],
        scratch_shapes=[pltpu.VMEM((D, D), F32)],
        compiler_params=_params(("arbitrary",)),
    )(dh, oa, ob, oc, wout)


def _lower_bounds(logits):
    depth, n = logits.shape

    def body(l_ref, lb_ref, p_ref):
        rows = [l_ref[l:l + 1, :] for l in range(depth)]
        mx = functools.reduce(jnp.maximum, rows)
        ex = [jnp.exp(r - mx) for r in rows]
        den = functools.reduce(lambda u, v: u + v, ex)
        acc = jnp.zeros_like(den)
        for l in range(depth):
            p = ex[l] / den
            p_ref[l:l + 1, :] = p
            if l > 0:
                acc = acc + p
            lb_ref[l:l + 1, :] = acc

    return pl.pallas_call(
        body, name="lower_bounds",
        out_shape=[jax.ShapeDtypeStruct((depth, n), F32), jax.ShapeDtypeStruct((depth, n), F32)],
    )(logits)


def _lower_bounds_bwd(p, dlb):
    depth, n = p.shape

    def body(p_ref, d_ref, out_ref):
        ps = [p_ref[l:l + 1, :] for l in range(depth)]
        ds = [d_ref[l:l + 1, :] for l in range(depth)]
        dp = [jnp.zeros_like(ps[0]) for _ in range(depth)]
        run = jnp.zeros_like(ps[0])
        for l in range(depth - 1, 0, -1):
            run = run + ds[l]
            dp[l] = run
        dot = functools.reduce(lambda u, v: u + v, [ps[l] * dp[l] for l in range(depth)])
        for l in range(depth):
            out_ref[l:l + 1, :] = ps[l] * (dp[l] - dot)

    return pl.pallas_call(body, name="lower_bounds_bwd", out_shape=jax.ShapeDtypeStruct((depth, n), F32))(p, dlb)


def _hgrn_block(z_ref, lb_ref, rb):
    q, fl = z_ref[:, 0:DA], z_ref[:, DA:2 * DA]
    lb = lb_ref[...]
    sq = _sigmoid(q)
    qs = q * sq
    sg = _sigmoid(fl)
    f = lb + (1.0 - lb) * sg
    k = 1.0 - f
    lf = jnp.log(f)
    row, col = _iota((rb, rb), 0), _iota((rb, rb), 1)
    same = (row // ACH) == (col // ACH)
    causal = same & (row >= col)
    b = _mm_exact_l(jnp.where(causal, 1.0, 0.0).astype(MXU), lf)
    bend = _mm_exact_l(jnp.where(same, 1.0, 0.0).astype(MXU), lf)
    r = 0.5 * bend
    eq, ek, eb, ed = jnp.exp(b - r), jnp.exp(r - b), jnp.exp(b), jnp.exp(bend - b)
    return dict(q=q, lb=lb, sq=sq, sg=sg, f=f, bend=bend, eq=eq, ek=ek, eb=eb, ed=ed,
                qt=qs * eq, kt=k * ek, qe=qs * eb, kd=k * ed, same=same, causal=causal)


def _hgrn_fwd(z, lb, gain):
    t = z.shape[0]
    nc = t // ACH
    cb = min(ACB, nc)
    rb = cb * ACH

    def body(z_ref, lb_ref, g_ref, o_ref, oa_ref, st_ref, st_scr):
        @pl.when(pl.program_id(0) == 0)
        def _():
            st_scr[...] = jnp.zeros_like(st_scr)

        c = _hgrn_block(z_ref, lb_ref, rb)
        for hd in range(NH):
            cols = slice(hd * HD, (hd + 1) * HD)
            v = z_ref[:, 2 * DA + hd * HD:2 * DA + (hd + 1) * HD]
            gg = z_ref[:, 3 * DA + hd * HD:3 * DA + (hd + 1) * HD]
            att = jnp.where(c["causal"], _mm_nt(c["qt"][:, cols], c["kt"][:, cols]), 0.0)
            o_in = _mm(att, v)
            qe, kd, bend = c["qe"][:, cols], c["kd"][:, cols], c["bend"][:, cols]
            st = st_scr[hd]
            outs = []
            for cc in range(cb):
                rows = slice(cc * ACH, (cc + 1) * ACH)
                st_ref[cc, hd] = st
                outs.append(o_in[rows] + _mm_nt(qe[rows], st))
                decay = jnp.exp(jnp.max(bend[rows], axis=0, keepdims=True))
                st = st * decay + _mm_tn(v[rows], kd[rows])
            st_scr[hd] = st
            o = jnp.concatenate(outs, axis=0)
            o_ref[:, cols] = o
            rstd = lax.rsqrt(jnp.mean(o * o, axis=-1, keepdims=True) + EPS)
            oa_ref[:, cols] = (o * rstd * g_ref[:, cols] * (gg * _sigmoid(gg))).astype(oa_ref.dtype)

    return pl.pallas_call(
        body, name="hgrn_fwd", grid=(nc // cb,),
        in_specs=[pl.BlockSpec((rb, 4 * DA), lambda c: (c, 0)),
                  pl.BlockSpec((1, DA), lambda c: (0, 0)),
                  pl.BlockSpec((1, DA), lambda c: (0, 0))],
        out_specs=[pl.BlockSpec((rb, DA), lambda c: (c, 0)),
                   pl.BlockSpec((rb, DA), lambda c: (c, 0)),
                   pl.BlockSpec((cb, NH, HD, HD), lambda c: (c, 0, 0, 0))],
        out_shape=[jax.ShapeDtypeStruct((t, DA), F32), jax.ShapeDtypeStruct((t, DA), SAVE),
                   jax.ShapeDtypeStruct((nc, NH, HD, HD), F32)],
        scratch_shapes=[pltpu.VMEM((NH, HD, HD), F32)],
        compiler_params=_params(("arbitrary",)),
    )(z, lb, gain)


def _hgrn_bwd(z, lb, gain, o, states, doa):
    t = z.shape[0]
    nc = t // ACH
    cb = min(ACB, nc)
    rb = cb * ACH
    nblk = nc // cb

    def body(z_ref, lb_ref, g_ref, o_ref, st_ref, doa_ref, dz_ref, dgain_ref, dlb_ref, dst_scr):
        @pl.when(pl.program_id(0) == 0)
        def _():
            dst_scr[...] = jnp.zeros_like(dst_scr)
            dgain_ref[...] = jnp.zeros_like(dgain_ref)
            dlb_ref[...] = jnp.zeros_like(dlb_ref)

        c = _hgrn_block(z_ref, lb_ref, rb)
        dbs, dqss, dks = [], [], []
        for hd in range(NH):
            cols = slice(hd * HD, (hd + 1) * HD)
            v = z_ref[:, 2 * DA + hd * HD:2 * DA + (hd + 1) * HD]
            gg = z_ref[:, 3 * DA + hd * HD:3 * DA + (hd + 1) * HD]
            qt, kt, qe, kd, bend = (c[n][:, cols] for n in ("qt", "kt", "qe", "kd", "bend"))
            o = o_ref[:, cols]
            do_a = doa_ref[:, cols]
            gain = g_ref[:, cols]
            sgg = _sigmoid(gg)
            silu_g = gg * sgg
            rstd = lax.rsqrt(jnp.mean(o * o, axis=-1, keepdims=True) + EPS)
            n = o * rstd
            dn = do_a * gain * silu_g
            dg = do_a * n * gain * (sgg * (1.0 + gg * (1.0 - sgg)))
            dgain_ref[:, cols] += jnp.sum(do_a * silu_g * n, axis=0, keepdims=True)
            d_o = rstd * (dn - n * jnp.mean(dn * n, axis=-1, keepdims=True))

            att = jnp.where(c["causal"], _mm_nt(qt, kt), 0.0)
            datt = jnp.where(c["causal"], _mm_nt(d_o, v), 0.0)
            dv_in = _mm_tn(att, d_o)
            dqt = _mm(datt, kt)
            dkt = _mm_tn(datt, qt)
            dsp = dst_scr[hd]
            dvs, dqes, dkds, dbends = [None] * cb, [None] * cb, [None] * cb, [None] * cb
            for cc in reversed(range(cb)):
                rows = slice(cc * ACH, (cc + 1) * ACH)
                st = st_ref[cc, hd]
                dvs[cc] = dv_in[rows] + _mm_nt(kd[rows], dsp)
                dqes[cc] = _mm(d_o[rows], st)
                dkds[cc] = _mm(v[rows], dsp)
                decay = jnp.exp(jnp.max(bend[rows], axis=0, keepdims=True))
                dbend = (decay * jnp.sum(st * dsp, axis=0, keepdims=True)
                         + jnp.sum(dkds[cc] * kd[rows], axis=0, keepdims=True))
                dbends[cc] = jnp.broadcast_to(dbend, (ACH, HD))
                dsp = dsp * decay + _mm_tn(d_o[rows], qe[rows])
            dst_scr[hd] = dsp
            dv, dqe, dkd, dbend = (jnp.concatenate(p, axis=0) for p in (dvs, dqes, dkds, dbends))
            dbs.append((dqt * qt + dqe * qe - dkt * kt - dkd * kd, dbend))
            dqss.append(dqt * c["eq"][:, cols] + dqe * c["eb"][:, cols])
            dks.append(dkt * c["ek"][:, cols] + dkd * c["ed"][:, cols])
            c0 = hd * HD
            dz_ref[:, 2 * DA + c0:2 * DA + c0 + HD] = dv.astype(dz_ref.dtype)
            dz_ref[:, 3 * DA + c0:3 * DA + c0 + HD] = dg.astype(dz_ref.dtype)

        db = jnp.concatenate([p[0] for p in dbs], axis=1)
        dbend = jnp.concatenate([p[1] for p in dbs], axis=1)
        dqs, dk = jnp.concatenate(dqss, axis=1), jnp.concatenate(dks, axis=1)
        row, col = _iota((rb, rb), 0), _iota((rb, rb), 1)
        upper = jnp.where(c["same"] & (row <= col), 1.0, 0.0).astype(MXU)
        dlf = _mm_exact_l(upper, db) + dbend
        df = dlf / c["f"] - dk
        sg, sq, q = c["sg"], c["sq"], c["q"]
        dlb_ref[...] += jnp.sum(df * (1.0 - sg), axis=0, keepdims=True)
        dz_ref[:, DA:2 * DA] = (df * (1.0 - c["lb"]) * sg * (1.0 - sg)).astype(dz_ref.dtype)
        dz_ref[:, 0:DA] = (dqs * (sq * (1.0 + q * (1.0 - sq)))).astype(dz_ref.dtype)

    rev = lambda c: (nblk - 1 - c, 0)
    return pl.pallas_call(
        body, name="hgrn_bwd", grid=(nblk,),
        in_specs=[pl.BlockSpec((rb, 4 * DA), rev),
                  pl.BlockSpec((1, DA), lambda c: (0, 0)),
                  pl.BlockSpec((1, DA), lambda c: (0, 0)),
                  pl.BlockSpec((rb, DA), rev),
                  pl.BlockSpec((cb, NH, HD, HD), lambda c: (nblk - 1 - c, 0, 0, 0)),
                  pl.BlockSpec((rb, DA), rev)],
        out_specs=[pl.BlockSpec((rb, 4 * DA), rev),
                   pl.BlockSpec((1, DA), lambda c: (0, 0)),
                   pl.BlockSpec((1, DA), lambda c: (0, 0))],
        out_shape=[jax.ShapeDtypeStruct((t, DIN), SAVE), jax.ShapeDtypeStruct((1, DA), F32),
                   jax.ShapeDtypeStruct((1, DA), F32)],
        scratch_shapes=[pltpu.VMEM((NH, HD, HD), F32)],
        compiler_params=_params(("arbitrary",)),
    )(z, lb, gain, o, states, doa)


def _shift_down(prev8, x, k):
    cat = jnp.concatenate([prev8, x], axis=0)
    return pltpu.roll(cat, k, axis=0)[8:, :]


def _shift_up(x, next8, k):
    n = x.shape[0]
    cat = jnp.concatenate([x, next8], axis=0)
    return pltpu.roll(cat, n + 8 - k, axis=0)[:n, :]


def _lru_gates(x, prev8, cw_ref, vec_ref, wa_ref, wx_ref):
    xs = [x, _shift_down(prev8, x, 1), _shift_down(prev8, x, 2), _shift_down(prev8, x, 3)]
    xc = vec_ref[0:1, :] + cw_ref[3:4, :] * xs[0] + cw_ref[2:3, :] * xs[1] + cw_ref[1:2, :] * xs[2] + cw_ref[0:1, :] * xs[3]
    r = _sigmoid(_mm(xc, wa_ref[...]) + vec_ref[1:2, :])
    gi = _sigmoid(_mm(xc, wx_ref[...]) + vec_ref[2:3, :])
    lam = vec_ref[3:4, :]
    sp = jnp.maximum(-lam, 0.0) + jnp.log(1.0 + jnp.exp(-jnp.abs(lam)))
    la = -LRU_C * r * sp
    a = jnp.exp(la)
    mult = jnp.sqrt(-_expm1(2.0 * la))
    return xs, xc, r, gi, sp, a, mult


def _scan_down(a, u):
    n = a.shape[0]
    row = _iota(a.shape, 0)
    s = 1
    while s < n:
        keep = row >= s
        ash = jnp.where(keep, pltpu.roll(a, s, axis=0), 1.0)
        ush = jnp.where(keep, pltpu.roll(u, s, axis=0), 0.0)
        u = a * ush + u
        a = a * ash
        s *= 2
    return a, u


def _scan_up(a, u):
    n = a.shape[0]
    row = _iota(a.shape, 0)
    s = 1
    while s < n:
        keep = row < n - s
        ash = jnp.where(keep, pltpu.roll(a, n - s, axis=0), 1.0)
        ush = jnp.where(keep, pltpu.roll(u, n - s, axis=0), 0.0)
        u = a * ush + u
        a = a * ash
        s *= 2
    return a, u


def _lru_fwd(z, cw, vec, wa, wx, tb):
    t = z.shape[0]
    xcol, gcol = (4 * DA) // DB, (4 * DA) // DB + 1

    def body(x_ref, gate_ref, cw_ref, vec_ref, wa_ref, wx_ref, ob_ref, h_ref, xprev_scr, hc_scr):
        @pl.when(pl.program_id(0) == 0)
        def _():
            xprev_scr[...] = jnp.zeros_like(xprev_scr)
            hc_scr[...] = jnp.zeros_like(hc_scr)

        x = x_ref[...]
        _, xc, _, gi, _, a, mult = _lru_gates(x, xprev_scr[...], cw_ref, vec_ref, wa_ref, wx_ref)
        acum, hloc = _scan_down(a, mult * gi * xc)
        h = hloc + acum * hc_scr[0:1, :]
        h_ref[...] = h
        hc_scr[...] = jnp.broadcast_to(_row(h, tb - 1), hc_scr.shape)
        xprev_scr[...] = x[tb - 8:, :]
        y = h * _gelu(gate_ref[...])
        ms = _mm_exact_r(y * y, _group_matrix(DB, 1.0 / GRP).astype(MXU))
        ob_ref[...] = (y * lax.rsqrt(ms + EPS) * vec_ref[4:5, :]).astype(ob_ref.dtype)

    return pl.pallas_call(
        body, name="lru_fwd", grid=(t // tb,),
        in_specs=[pl.BlockSpec((tb, DB), lambda i: (i, xcol)),
                  pl.BlockSpec((tb, DB), lambda i: (i, gcol)),
                  pl.BlockSpec((8, DB), lambda i: (0, 0)),
                  pl.BlockSpec((8, DB), lambda i: (0, 0)),
                  pl.BlockSpec((DB, DB), lambda i: (0, 0)),
                  pl.BlockSpec((DB, DB), lambda i: (0, 0))],
        out_specs=[pl.BlockSpec((tb, DB), lambda i: (i, 0)),
                   pl.BlockSpec((tb, DB), lambda i: (i, 0))],
        out_shape=[jax.ShapeDtypeStruct((t, DB), SAVE), jax.ShapeDtypeStruct((t, DB), F32)],
        scratch_shapes=[pltpu.VMEM((8, DB), F32), pltpu.VMEM((8, DB), F32)],
        compiler_params=_params(("arbitrary",)),
    )(z, z, cw, vec, wa, wx)


def _lru_bwd(z, hseq, dob, cw, vec, wa, wx, dz, tb):
    t = z.shape[0]
    nb = t // tb
    xcol, gcol = (4 * DA) // DB, (4 * DA) // DB + 1
    per = tb // 8

    def body(x_ref, xh_ref, gate_ref, h_ref, hh_ref, dob_ref, cw_ref, vec_ref, wa_ref, wx_ref, _,
             dz_ref, dcw_ref, dvec_ref, dwa_ref, dwx_ref, gc_scr, an_scr, dxc_scr):
        step = pl.program_id(0)
        blk = nb - 1 - step

        @pl.when(step == 0)
        def _():
            for ref in (gc_scr, an_scr, dxc_scr, dcw_ref, dvec_ref, dwa_ref, dwx_ref):
                ref[...] = jnp.zeros_like(ref)

        first = (blk > 0).astype(F32)
        x = x_ref[...]
        xs, xc, r, gi, sp, a, mult = _lru_gates(x, xh_ref[...] * first, cw_ref, vec_ref, wa_ref, wx_ref)
        h = h_ref[...]
        hprev = _shift_down(hh_ref[...] * first, h, 1)
        ge, dge = _gelu_and_grad(gate_ref[...])
        y = h * ge
        gmat = _group_matrix(DB, 1.0 / GRP).astype(MXU)
        rstd = lax.rsqrt(_mm_exact_r(y * y, gmat) + EPS)
        n = y * rstd
        d_ob = dob_ref[...]
        dn = d_ob * vec_ref[4:5, :]
        dvec_ref[4:5, :] += jnp.sum(d_ob * n, axis=0, keepdims=True)
        dy = rstd * (dn - n * _mm_exact_r(dn * n, gmat))
        dh = dy * ge
        dgate = dy * h * dge

        row = _iota(a.shape, 0)
        anext = jnp.where(row == tb - 1, an_scr[0:1, :], pltpu.roll(a, tb - 1, axis=0))
        acum, gloc = _scan_up(anext, dh)
        g = gloc + acum * gc_scr[0:1, :]
        gc_scr[...] = jnp.broadcast_to(_row(g, 0), gc_scr.shape)
        an_scr[...] = jnp.broadcast_to(_row(a, 0), an_scr.shape)

        da = g * hprev
        dmult = g * gi * xc
        dgi = g * mult * xc
        dxc = g * mult * gi
        dla = da * a - dmult * (a * a) / mult
        dr = dla * (-LRU_C * sp)
        dsp = jnp.sum(dla * (-LRU_C * r), axis=0, keepdims=True)
        lam = vec_ref[3:4, :]
        dvec_ref[3:4, :] += -dsp * _sigmoid(-lam)
        dpa = dr * r * (1.0 - r)
        dpx = dgi * gi * (1.0 - gi)
        dwa_ref[...] += _mm_tn(xc, dpa)
        dwx_ref[...] += _mm_tn(xc, dpx)
        dvec_ref[1:2, :] += jnp.sum(dpa, axis=0, keepdims=True)
        dvec_ref[2:3, :] += jnp.sum(dpx, axis=0, keepdims=True)
        dxc = dxc + _mm_nt(dpa, wa_ref[...]) + _mm_nt(dpx, wx_ref[...])
        dvec_ref[0:1, :] += jnp.sum(dxc, axis=0, keepdims=True)
        for tap in range(4):
            dcw_ref[tap:tap + 1, :] += jnp.sum(dxc * xs[3 - tap], axis=0, keepdims=True)
        nxt = dxc_scr[...]
        dx = (cw_ref[3:4, :] * dxc + cw_ref[2:3, :] * _shift_up(dxc, nxt, 1)
              + cw_ref[1:2, :] * _shift_up(dxc, nxt, 2) + cw_ref[0:1, :] * _shift_up(dxc, nxt, 3))
        dxc_scr[...] = dxc[:8, :]
        dz_ref[:, :DB] = dx.astype(dz_ref.dtype)
        dz_ref[:, DB:] = dgate.astype(dz_ref.dtype)

    def halo(col):
        return lambda s: (jnp.maximum((nb - 1 - s) * per - 1, 0), col)

    const = lambda s: (0, 0)
    return pl.pallas_call(
        body, name="lru_bwd", grid=(nb,),
        in_specs=[pl.BlockSpec((tb, DB), lambda s: (nb - 1 - s, xcol)),
                  pl.BlockSpec((8, DB), halo(xcol)),
                  pl.BlockSpec((tb, DB), lambda s: (nb - 1 - s, gcol)),
                  pl.BlockSpec((tb, DB), lambda s: (nb - 1 - s, 0)),
                  pl.BlockSpec((8, DB), halo(0)),
                  pl.BlockSpec((tb, DB), lambda s: (nb - 1 - s, 0)),
                  pl.BlockSpec((8, DB), const), pl.BlockSpec((8, DB), const),
                  pl.BlockSpec((DB, DB), const), pl.BlockSpec((DB, DB), const),
                  pl.BlockSpec(memory_space=pl.ANY)],
        out_specs=[pl.BlockSpec((tb, 2 * DB), lambda s: (nb - 1 - s, (4 * DA) // (2 * DB))),
                   pl.BlockSpec((8, DB), const), pl.BlockSpec((8, DB), const),
                   pl.BlockSpec((DB, DB), const), pl.BlockSpec((DB, DB), const)],
        out_shape=[jax.ShapeDtypeStruct((t, DIN), SAVE), jax.ShapeDtypeStruct((8, DB), F32),
                   jax.ShapeDtypeStruct((8, DB), F32), jax.ShapeDtypeStruct((DB, DB), F32),
                   jax.ShapeDtypeStruct((DB, DB), F32)],
        scratch_shapes=[pltpu.VMEM((8, DB), F32), pltpu.VMEM((8, DB), F32), pltpu.VMEM((8, DB), F32)],
        input_output_aliases={10: 0},
        compiler_params=_params(("arbitrary",)),
    )(z, z, z, hseq, hseq, dob, cw, vec, wa, wx, dz)


def _sgu_block(u_ref, v_ref, w_ref, b_ref, gmat, tb):
    uu, duu = _gelu_and_grad(u_ref[...])
    vv, dvv = _gelu_and_grad(v_ref[...])
    dlt = vv - _mm_exact_r(vv, gmat)
    rstd_v = lax.rsqrt(_mm_exact_r(dlt * dlt, gmat) + EPS)
    vn = dlt * rstd_v
    col = _iota((CCH, DC), 1) // GRP
    causal = _iota((CCH, CCH), 0) >= _iota((CCH, CCH), 1)
    ws = [jnp.where(causal, w_ref[g], 0.0) for g in range(DC // GRP)]
    zs = []
    for ch in range(tb // CCH):
        vn_c = vn[ch * CCH:(ch + 1) * CCH]
        zz = b_ref[...]
        for g, w in enumerate(ws):
            zz = zz + jnp.where(col == g, _mm(w, vn_c), 0.0)
        zs.append(zz)
    return uu, duu, dvv, rstd_v, vn, jnp.concatenate(zs, axis=0), ws, col, causal


def _sgu_fwd(z, w, bias, gain, tb):
    t = z.shape[0]
    ucol, vcol = (4 * DA + 2 * DB) // DC, (4 * DA + 2 * DB) // DC + 1

    def body(u_ref, v_ref, w_ref, b_ref, g_ref, oc_ref):
        gmat = _group_matrix(DC, 1.0 / GRP).astype(MXU)
        uu, _, _, _, _, zz, _, _, _ = _sgu_block(u_ref, v_ref, w_ref, b_ref, gmat, tb)
        y = uu * zz
        ms = _mm_exact_r(y * y, gmat)
        oc_ref[...] = (y * lax.rsqrt(ms + EPS) * g_ref[...]).astype(oc_ref.dtype)

    const = lambda i: (0, 0)
    return pl.pallas_call(
        body, name="sgu_fwd", grid=(t // tb,),
        in_specs=[pl.BlockSpec((tb, DC), lambda i: (i, ucol)),
                  pl.BlockSpec((tb, DC), lambda i: (i, vcol)),
                  pl.BlockSpec((DC // GRP, CCH, CCH), lambda i: (0, 0, 0)),
                  pl.BlockSpec((CCH, DC), const), pl.BlockSpec((1, DC), const)],
        out_specs=pl.BlockSpec((tb, DC), lambda i: (i, 0)),
        out_shape=jax.ShapeDtypeStruct((t, DC), SAVE),
        compiler_params=_params(("parallel",)),
    )(z, z, w, bias, gain)


def _sgu_bwd(z, doc, w, bias, gain, dz, tb):
    t = z.shape[0]
    nb = t // tb
    ucol, vcol = (4 * DA + 2 * DB) // DC, (4 * DA + 2 * DB) // DC + 1
    ng = DC // GRP

    def body(u_ref, v_ref, doc_ref, w_ref, b_ref, g_ref, _, dz_ref, dw_ref, dbias_ref, dgain_ref, dbsum_scr):
        i = pl.program_id(0)

        @pl.when(i == 0)
        def _():
            for ref in (dw_ref, dgain_ref, dbsum_scr):
                ref[...] = jnp.zeros_like(ref)

        gmat = _group_matrix(DC, 1.0 / GRP).astype(MXU)
        uu, duu, dvv, rstd_v, vn, zz, ws, col, causal = _sgu_block(u_ref, v_ref, w_ref, b_ref, gmat, tb)
        y = uu * zz
        rstd = lax.rsqrt(_mm_exact_r(y * y, gmat) + EPS)
        n = y * rstd
        d_oc = doc_ref[...]
        dn = d_oc * g_ref[...]
        dgain_ref[0:1, :] += jnp.sum(d_oc * n, axis=0, keepdims=True)
        dy = rstd * (dn - n * _mm_exact_r(dn * n, gmat))
        dzz = dy * uu
        dz_ref[:, :DC] = (dy * zz * duu).astype(dz_ref.dtype)
        dvns = []
        for ch in range(tb // CCH):
            rows = slice(ch * CCH, (ch + 1) * CCH)
            dzz_c, vn_c = dzz[rows], vn[rows]
            dbsum_scr[...] += dzz_c
            dvn = jnp.zeros_like(dzz_c)
            for g in range(ng):
                sel = col == g
                dvn = dvn + jnp.where(sel, _mm_tn(ws[g], dzz_c), 0.0)
                dw_ref[g] += jnp.where(causal, _mm_nt(jnp.where(sel, dzz_c, 0.0), vn_c), 0.0)
            dvns.append(dvn)
        dvn = jnp.concatenate(dvns, axis=0)
        dv = rstd_v * (dvn - _mm_exact_r(dvn, gmat) - vn * _mm_exact_r(dvn * vn, gmat))
        dz_ref[:, DC:] = (dv * dvv).astype(dz_ref.dtype)

        @pl.when(i == nb - 1)
        def _():
            dbias_ref[...] = _mm_exact_r(dbsum_scr[...], _group_matrix(DC, 1.0).astype(MXU))

    const = lambda i: (0, 0)
    return pl.pallas_call(
        body, name="sgu_bwd", grid=(nb,),
        in_specs=[pl.BlockSpec((tb, DC), lambda i: (i, ucol)),
                  pl.BlockSpec((tb, DC), lambda i: (i, vcol)),
                  pl.BlockSpec((tb, DC), lambda i: (i, 0)),
                  pl.BlockSpec((ng, CCH, CCH), lambda i: (0, 0, 0)),
                  pl.BlockSpec((CCH, DC), const), pl.BlockSpec((1, DC), const),
                  pl.BlockSpec(memory_space=pl.ANY)],
        out_specs=[pl.BlockSpec((tb, 2 * DC), lambda i: (i, (4 * DA + 2 * DB) // (2 * DC))),
                   pl.BlockSpec((ng, CCH, CCH), lambda i: (0, 0, 0)),
                   pl.BlockSpec((CCH, DC), const), pl.BlockSpec((8, DC), const)],
        out_shape=[jax.ShapeDtypeStruct((t, DIN), SAVE), jax.ShapeDtypeStruct((ng, CCH, CCH), F32),
                   jax.ShapeDtypeStruct((CCH, DC), F32), jax.ShapeDtypeStruct((8, DC), F32)],
        scratch_shapes=[pltpu.VMEM((CCH, DC), F32)],
        input_output_aliases={6: 0},
        compiler_params=_params(("arbitrary",)),
    )(z, z, doc, w, bias, gain, dz)


def _head(h, gain, target, tm):
    t = h.shape[0]

    def body(h_ref, g_ref, t_ref, dh_ref, loss_ref, dgain_ref):
        @pl.when(pl.program_id(0) == 0)
        def _():
            loss_ref[...] = jnp.zeros_like(loss_ref)
            dgain_ref[...] = jnp.zeros_like(dgain_ref)

        hh = h_ref[...]
        gain = g_ref[...]
        rstd = lax.rsqrt(jnp.mean(hh * hh, axis=-1, keepdims=True) + EPS)
        xhat = hh * rstd
        err = xhat * gain - t_ref[...]
        per_tok = jnp.mean(err * err, axis=-1, keepdims=True)
        loss_ref[...] += 0.5 * jnp.sum(per_tok, axis=0, keepdims=True)
        dy = err * (1.0 / D)
        dgain_ref[...] += jnp.sum(dy * xhat, axis=0, keepdims=True)
        dxh = dy * gain
        dh_ref[...] = rstd * (dxh - xhat * jnp.mean(dxh * xhat, axis=-1, keepdims=True))

    return pl.pallas_call(
        body, name="head", grid=(t // tm,),
        in_specs=[pl.BlockSpec((tm, D), lambda i: (i, 0)),
                  pl.BlockSpec((1, D), lambda i: (0, 0)),
                  pl.BlockSpec((tm, D), lambda i: (i, 0))],
        out_specs=[pl.BlockSpec((tm, D), lambda i: (i, 0)),
                   pl.BlockSpec((1, 128), lambda i: (0, 0)),
                   pl.BlockSpec((1, D), lambda i: (0, 0))],
        out_shape=[jax.ShapeDtypeStruct((t, D), F32), jax.ShapeDtypeStruct((1, 128), F32),
                   jax.ShapeDtypeStruct((1, D), F32)],
        compiler_params=_params(("arbitrary",)),
    )(h, gain, target)


def _adamw(w, g, m, v):
    m = ADAM_B1 * m + (1.0 - ADAM_B1) * g
    v = ADAM_B2 * v + (1.0 - ADAM_B2) * (g * g)
    m_hat = m / (1.0 - ADAM_B1 ** ADAM_STEP)
    v_hat = v / (1.0 - ADAM_B2 ** ADAM_STEP)
    delta = -ADAM_LR * (m_hat / (jnp.sqrt(v_hat) + ADAM_EPS) + ADAM_WD * w)
    return delta, m, v


def _adamw_big(recv, w, m, v, tr, name, after, transposed=False):
    depth, rows, cols = w.shape
    rspec = (pl.BlockSpec((NDEV, cols, tr), lambda i: (0, 0, i)) if transposed
             else pl.BlockSpec((NDEV, tr, cols), lambda i: (0, i, 0)))

    def body(*refs):
        r_refs = refs[:depth]
        w_ref, m_ref, v_ref, _, g_out, d_out, m_out, v_out = refs[depth:]
        for l in range(depth):
            g = r_refs[l][0].astype(F32)
            for k in range(1, NDEV):
                g = g + r_refs[l][k].astype(F32)
            if transposed:
                g = g.T
            delta, m_, v_ = _adamw(w_ref[l], g, m_ref[l], v_ref[l])
            g_out[l] = g
            d_out[l] = delta
            m_out[l] = m_
            v_out[l] = v_

    spec = pl.BlockSpec((depth, tr, cols), lambda i: (0, i, 0))
    return pl.pallas_call(
        body, name=name, grid=(rows // tr,),
        in_specs=[rspec] * depth + [spec] * 3
        + [pl.BlockSpec(memory_space=pl.ANY)],
        out_specs=[spec] * 4, out_shape=[jax.ShapeDtypeStruct((depth, rows, cols), F32)] * 4,
        compiler_params=_params(("parallel",)),
    )(*recv, w, m, v, after)


def _sum_devices(recv):
    _, r, _ = recv.shape

    def body(r_ref, out_ref):
        g = r_ref[0]
        for k in range(1, NDEV):
            g = g + r_ref[k]
        out_ref[...] = g

    return pl.pallas_call(body, name="sum_devices", out_shape=jax.ShapeDtypeStruct((r, 128), F32))(recv)


def _adamw_small(w, g, m, v):
    def body(w_ref, g_ref, m_ref, v_ref, d_out, m_out, v_out):
        delta, m_, v_ = _adamw(w_ref[...], g_ref[...], m_ref[...], v_ref[...])
        d_out[...] = delta
        m_out[...] = m_
        v_out[...] = v_

    return pl.pallas_call(body, name="adamw_small", out_shape=[jax.ShapeDtypeStruct(w.shape, F32)] * 3)(w, g, m, v)


def _pack(arrs):
    flat = jnp.concatenate([a.reshape(-1) for a in arrs])
    pad = (-flat.shape[0]) % 1024
    return jnp.pad(flat, (0, pad)).reshape(-1, 128)


def _unpack(buf, like):
    flat = buf.reshape(-1)
    out, off = [], 0
    for a in like:
        out.append(flat[off:off + a.size].reshape(a.shape))
        off += a.size
    return out


def _block_diag(w):
    nb, bd, _ = w.shape
    eye = jnp.eye(nb, dtype=w.dtype)
    return (eye[:, None, :, None] * w[:, :, None, :]).reshape(nb * bd, nb * bd)


def _diag_blocks(w):
    nb = w.shape[0] // GRP
    return jnp.stack([w[g * GRP:(g + 1) * GRP, g * GRP:(g + 1) * GRP] for g in range(nb)])


SMALL = ['ffn1_norm', 'mix_norm', 'hgrn_lb_logits', 'hgrn_norm', 'conv_b', 'lru_wa', 'lru_ba', 'lru_wx', 'lru_bx',
         'lru_lambda', 'lru_norm', 'sgu_w', 'sgu_b', 'sgu_norm', 'ffn2_norm', 'final_norm']
NAMES = ['ffn1_norm', 'ffn1_wg', 'ffn1_wu', 'ffn1_wd', 'mix_norm', 'w_in', 'hgrn_lb_logits', 'hgrn_norm', 'conv_w',
         'conv_b', 'lru_wa', 'lru_ba', 'lru_wx', 'lru_bx', 'lru_lambda', 'lru_norm', 'sgu_w', 'sgu_b', 'sgu_norm',
         'w_out', 'ffn2_norm', 'ffn2_wg', 'ffn2_wu', 'ffn2_wd', 'final_norm']


def _step(x, target, w, m, v):
    depth = w['ffn1_wg'].shape[0]
    t = x.shape[1]
    h = x.reshape(t, D)
    target = target.reshape(t, D)
    tm_f, tm_b, tb = min(TM_F, t), min(TM_B, t), min(TB, t)
    my = 4 * lax.axis_index("x") + 2 * lax.axis_index("y") + lax.axis_index("c")

    cw_tile = jnp.pad(w['conv_w'].reshape(-1, 128), ((0, 8 - depth), (0, 0)))
    lbs, lb_soft = _lower_bounds(w['hgrn_lb_logits'])

    def row(a):
        return a.reshape(1, -1)

    def tr(a):
        return jnp.swapaxes(a, -1, -2)

    def shards(l, unit):
        if unit == 1:
            return [tr(w['w_in'][l]).astype(WIRE), w['w_out'][l].astype(WIRE)]
        f = 'ffn1' if unit == 0 else 'ffn2'
        return [tr(w[f + '_wg'][l]).astype(WIRE), tr(w[f + '_wu'][l]).astype(WIRE), w[f + '_wd'][l].astype(WIRE)]

    units = [(l, u) for l in range(depth) for u in range(3)]

    def start_ici(idx, deps=()):
        return _transfer_start(shards(*units[idx]), True, "gather_ici_%d_%d" % units[idx], deps=deps)

    def relay(idx, handle, after, then=None):
        lands = _transfer_wait(handle, after, "gather_ici_wait_%d_%d" % units[idx])
        more = shards(*units[then]) if then is not None and then < len(units) else ()
        return _forward_start(lands, "gather_d2d_%d_%d" % units[idx], more)

    pipe = dict(idx=0)
    first = start_ici(0)
    pipe['ici'] = start_ici(1, deps=(first['token'],))
    conv_flight = _transfer_start([cw_tile], True, "gather_conv_start", deps=(pipe['ici']['token'],), direct=True)
    pipe['d2d'], _ = relay(0, first, conv_flight['token'])

    def next_weights(after):
        idx = pipe['idx']
        lands = _transfer_wait(pipe['d2d'], after, "gather_d2d_wait_%d_%d" % units[idx])
        pipe['idx'] = idx + 1
        tok = 0.0
        if idx + 1 < len(units):
            pipe['d2d'], pipe['ici'] = relay(idx + 1, pipe['ici'], lands[-1], idx + 2)
            tok = pipe['d2d']['token'][0, 0]
        return lands, tok

    saved = []
    for l in range(depth):
        lands, tok = next_weights(h)
        s = dict(ffn1=[a.reshape(FF, D) for a in lands], h0=h)
        h, s['xn1'], s['a1'], s['b1'] = _ffn_fwd(h, row(w['ffn1_norm'][l]) + tok, *s['ffn1'], tm_f)
        s['h1'] = h
        (win, wout), tok = next_weights(h)
        win, wout = win.reshape(DIN, D), wout.reshape(D, D)
        s['win'], s['wout'] = win, wout
        z, s['xnm'] = _inproj_fwd(h, row(w['mix_norm'][l]) + tok, win, tm_f)
        s['z'] = z
        s['o'], oa, s['states'] = _hgrn_fwd(z, row(lbs[l]), row(w['hgrn_norm'][l]))
        if l == 0:
            cw_all = _transfer_wait(conv_flight, z, "gather_conv_wait")[0][:, :depth]
            conv_w = jnp.moveaxis(cw_all.reshape(NDEV, depth, 4, DB // NDEV), 0, 2).reshape(depth, 4, DB)
        s['cw'] = jnp.pad(conv_w[l], ((0, 4), (0, 0)))
        s['vec'] = jnp.concatenate([row(w['conv_b'][l]), row(w['lru_ba'][l]), row(w['lru_bx'][l]),
                                    row(w['lru_lambda'][l]), row(w['lru_norm'][l]), jnp.zeros((3, DB), F32)])
        s['wa'], s['wx'] = _block_diag(w['lru_wa'][l]), _block_diag(w['lru_wx'][l])
        ob, s['hseq'] = _lru_fwd(z, s['cw'], s['vec'], s['wa'], s['wx'], tb)
        s['bias'] = jnp.repeat(w['sgu_b'][l].T, GRP, axis=1)
        oc = _sgu_fwd(z, w['sgu_w'][l], s['bias'], row(w['sgu_norm'][l]), tb)
        s['oa'], s['ob'], s['oc'] = oa, ob, oc
        h = _outproj_fwd(h, oa, ob, oc, wout, tm_f)
        s['h2'] = h
        lands, tok = next_weights(h)
        s['ffn2'] = [a.reshape(FF, D) for a in lands]
        h, s['xn2'], s['a2'], s['b2'] = _ffn_fwd(h, row(w['ffn2_norm'][l]) + tok, *s['ffn2'], tm_f)
        saved.append(s)

    dh, loss_part, g_final = _head(h, row(w['final_norm']), target, tm_f)
    loss = lax.psum(loss_part[0, 0], ("x", "y", "c"))

    recv = {k: [None] * depth for k in ('wg1', 'wu1', 'wd1', 'wg2', 'wu2', 'wd2', 'win', 'wout')}
    flight = []

    def land(after):
        handle, kinds, l = flight.pop()
        for k, a in zip(kinds, _transfer_wait(handle, after, f"exchange_wait_{kinds[0]}_{l}")):
            recv[k][l] = a

    def exchange(arrs, kinds, l, deps=()):
        handle = _transfer_start(arrs, False, f"exchange_start_{kinds[0]}_{l}", deps=deps)
        if flight:
            land(handle['token'])
        flight.append((handle, kinds, l))
        return handle['token'][0, 0]

    small = {k: [None] * depth for k in SMALL if k != 'final_norm'}
    dconv = [None] * depth
    dlb = [None] * depth
    tok = 0.0
    for l in reversed(range(depth)):
        s = saved[l]
        dh, g, *cot = _ffn_bwd_x(dh, s['h2'], row(w['ffn2_norm'][l]) + tok, s['a2'], s['b2'], *s['ffn2'], tm_b)
        dws = _ffn_bwd_w(s['xn2'], *cot, tm_b)
        tok = exchange([a.reshape(NDEV, FFS, D) for a in dws], ('wg2', 'wu2', 'wd2'), l)
        small['ffn2_norm'][l] = g
        doa, dob, doc, dwout = _outproj_bwd(dh, s['oa'], s['ob'], s['oc'], s['wout'], tm_f)
        dz, g_hn, dlb[l] = _hgrn_bwd(s['z'], row(lbs[l]), row(w['hgrn_norm'][l]) + tok, s['o'], s['states'], doa)
        small['hgrn_norm'][l] = g_hn
        dz, dcw, dvec, dwa, dwx = _lru_bwd(s['z'], s['hseq'], dob, s['cw'], s['vec'], s['wa'], s['wx'], dz, tb)
        dconv[l] = dcw[:4]
        small['conv_b'][l], small['lru_ba'][l], small['lru_bx'][l] = dvec[0], dvec[1].reshape(4, GRP), dvec[2].reshape(4, GRP)
        small['lru_lambda'][l], small['lru_norm'][l] = dvec[3], dvec[4]
        small['lru_wa'][l], small['lru_wx'][l] = _diag_blocks(dwa), _diag_blocks(dwx)
        dz, dsw, dbias, dgc = _sgu_bwd(s['z'], doc, w['sgu_w'][l], s['bias'], row(w['sgu_norm'][l]), dz, tb)
        small['sgu_w'][l], small['sgu_b'][l], small['sgu_norm'][l] = dsw, dbias[:, ::GRP].T, dgc[0]
        dwin = _inproj_bwd_w(s['xnm'], dz, tm_f)
        tok = exchange([dwin.reshape(NDEV, DINS, D), dwout.reshape(NDEV, D // NDEV, D)], ('win', 'wout'), l)
        dh, g = _inproj_bwd_x(dh, dz, s['h1'], row(w['mix_norm'][l]) + tok, s['win'], tm_b)
        small['mix_norm'][l] = g
        tok = 0.0
        if l == 0:
            small['ffn1_norm'][0] = jnp.zeros((1, D), F32)
            small['hgrn_lb_logits'] = list(_lower_bounds_bwd(lb_soft, jnp.concatenate(dlb, axis=0)))
            parts = [jnp.stack([small[k][j].reshape(w[k].shape[1:]) for j in range(depth)])
                     for k in SMALL if k != 'final_norm']
            parts += [g_final.reshape(D), jnp.stack(dconv)]
            small_flight = _transfer_start([_pack(parts)], True, "gather_small_start", direct=True)
            tok = small_flight['token'][0, 0]
        dh, g, *cot = _ffn_bwd_x(dh, s['h0'], row(w['ffn1_norm'][l]) + tok, s['a1'], s['b1'], *s['ffn1'], tm_b)
        dws = _ffn_bwd_w(s['xn1'], *cot, tm_b)
        before = ()
        if l == 0:
            g_last = _all_gather([g.reshape(8, 128)], "gather_last")[0]
            before = (g_last,)
        else:
            small['ffn1_norm'][l] = g
        tok = exchange([a.reshape(NDEV, FFS, D) for a in dws], ('wg1', 'wu1', 'wd1'), l, before)
    grad_x = dh.reshape(1, t, D)

    out = {}
    last = flight[0][0]['token']

    def ffn_update(f, n, after):
        for kind in ('wg', 'wu'):
            k = f + '_' + kind
            res = _adamw_big(recv[kind + n], tr(w[k]), tr(m[k]), tr(v[k]), 32, "adamw_ffn", after)
            out[k] = tuple(tr(a) for a in res)
        k = f + '_wd'
        out[k] = _adamw_big(recv['wd' + n], w[k], m[k], v[k], 32, "adamw_ffn", after)

    ffn_update('ffn2', '2', last)
    out['w_in'] = _adamw_big(recv['win'], w['w_in'], m['w_in'], v['w_in'], 128, "adamw_win", last, transposed=True)
    out['w_out'] = _adamw_big(recv['wout'], w['w_out'], m['w_out'], v['w_out'], 64, "adamw_wout", last)

    total = _sum_devices(_transfer_wait(small_flight, g_last, "gather_small_wait")[0])
    like = [w[k] for k in SMALL] + [jax.ShapeDtypeStruct((depth, 4, DB), F32)]
    grads = _unpack(total, like)
    gsmall = dict(zip(SMALL, grads[:-1]))
    gsmall['ffn1_norm'] = gsmall['ffn1_norm'].at[0].set(_sum_devices(g_last).reshape(D))
    gsmall['conv_w'] = lax.dynamic_slice_in_dim(grads[-1], my * (DB // NDEV), DB // NDEV, axis=2)
    keys = SMALL + ['conv_w']
    dl, mm, vv = _adamw_small(_pack([w[k] for k in keys]), _pack([gsmall[k] for k in keys]),
                              _pack([m[k] for k in keys]), _pack([v[k] for k in keys]))
    like = [w[k] for k in keys]
    for k, d_, m_, v_ in zip(keys, _unpack(dl, like), _unpack(mm, like), _unpack(vv, like)):
        out[k] = (gsmall[k], d_, m_, v_)
    done = [dl] + [out[k][1][0] for k in ('ffn2_wg', 'ffn2_wu', 'ffn2_wd', 'w_in', 'w_out')]
    land(functools.reduce(lambda p, q: p + q, [a[:1, :1] for a in done]))
    ffn_update('ffn1', '1', last)

    return (loss, grad_x, *[out[k][0] for k in NAMES], *[out[k][1] for k in NAMES],
            *[out[k][2] for k in NAMES], *[out[k][3] for k in NAMES])


def kernel(x, ffn1_norm, ffn1_wg, ffn1_wu, ffn1_wd, mix_norm, w_in, hgrn_lb_logits, hgrn_norm, conv_w, conv_b, lru_wa, lru_ba, lru_wx, lru_bx, lru_lambda, lru_norm, sgu_w, sgu_b, sgu_norm, w_out, ffn2_norm, ffn2_wg, ffn2_wu, ffn2_wd, final_norm, loss_target, m_ffn1_norm, m_ffn1_wg, m_ffn1_wu, m_ffn1_wd, m_mix_norm, m_w_in, m_hgrn_lb_logits, m_hgrn_norm, m_conv_w, m_conv_b, m_lru_wa, m_lru_ba, m_lru_wx, m_lru_bx, m_lru_lambda, m_lru_norm, m_sgu_w, m_sgu_b, m_sgu_norm, m_w_out, m_ffn2_norm, m_ffn2_wg, m_ffn2_wu, m_ffn2_wd, m_final_norm, v_ffn1_norm, v_ffn1_wg, v_ffn1_wu, v_ffn1_wd, v_mix_norm, v_w_in, v_hgrn_lb_logits, v_hgrn_norm, v_conv_w, v_conv_b, v_lru_wa, v_lru_ba, v_lru_wx, v_lru_bx, v_lru_lambda, v_lru_norm, v_sgu_w, v_sgu_b, v_sgu_norm, v_w_out, v_ffn2_norm, v_ffn2_wg, v_ffn2_wu, v_ffn2_wd, v_final_norm):
    args = locals()
    w = {k: args[k] for k in NAMES}
    m = {k: args['m_' + k] for k in NAMES}
    v = {k: args['v_' + k] for k in NAMES}
    return _step(x, loss_target, w, m, v)
```

```python
import functools

import jax
import jax.numpy as jnp
from jax import lax
from jax.experimental import pallas as pl
from jax.experimental.pallas import tpu as pltpu

F32 = jnp.float32
MXU = jnp.bfloat16
SAVE = jnp.bfloat16
WIRE = jnp.bfloat16

NDEV = 8
D = 1024
FF = 2816
FFS = FF // NDEV
FB = 256
FBX = FF // 2
DIN = 3072
DINS = DIN // NDEV
ZB = 512
ZBW = 1024
DA, DB, DC = 512, 256, 256
HD = 128
NH = DA // HD
ACH = 64
ACB = 4
CCH = 128
GRP = 64
EPS = 1e-6
LRU_C = 8.0
VMEM_LIMIT = 60 * 1024 * 1024
TM_F = 1024
TM_B = 512
TB = 1024
SUB = 512
SUB_X = 256

ADAM_LR, ADAM_B1, ADAM_B2, ADAM_EPS, ADAM_WD, ADAM_STEP = 0.001, 0.9, 0.999, 1e-08, 0.01, 10

MESH = pl.DeviceIdType.MESH


def _mm(a, b):
    return jnp.dot(a.astype(MXU), b.astype(MXU), preferred_element_type=F32)


def _mm_nt(a, b):
    return lax.dot_general(a.astype(MXU), b.astype(MXU), (((1,), (1,)), ((), ())), preferred_element_type=F32)


def _mm_tn(a, b):
    return lax.dot_general(a.astype(MXU), b.astype(MXU), (((0,), (0,)), ((), ())), preferred_element_type=F32)


def _split3(x):
    x1 = x.astype(MXU)
    r1 = x - x1.astype(F32)
    x2 = r1.astype(MXU)
    r2 = r1 - x2.astype(F32)
    return x1, x2, r2.astype(MXU)


def _mm_exact_l(c, x):
    x1, x2, x3 = _split3(x)
    return _mm(c, x1) + _mm(c, x2) + _mm(c, x3)


def _mm_exact_r(x, c):
    x1, x2, x3 = _split3(x)
    return _mm(x1, c) + _mm(x2, c) + _mm(x3, c)


def _sigmoid(x):
    return 1.0 / (1.0 + jnp.exp(-x))


def _gelu(x):
    c, k = 0.7978845608028654, 0.044715
    th = jnp.tanh(c * (x + k * x * x * x))
    return 0.5 * x * (1.0 + th)


def _gelu_and_grad(x):
    c, k = 0.7978845608028654, 0.044715
    th = jnp.tanh(c * (x + k * x * x * x))
    g = 0.5 * x * (1.0 + th)
    dg = 0.5 * (1.0 + th) + 0.5 * x * (1.0 - th * th) * c * (1.0 + 3.0 * k * x * x)
    return g, dg


def _expm1(x):
    series = x * (1.0 + x * (0.5 + x * (1.0 / 6.0 + x * (1.0 / 24.0 + x * (1.0 / 120.0)))))
    return jnp.where(jnp.abs(x) < 0.05, series, jnp.exp(x) - 1.0)


def _iota(shape, dim):
    return lax.broadcasted_iota(jnp.int32, shape, dim)


def _group_matrix(n, value):
    r, c = _iota((n, n), 0), _iota((n, n), 1)
    return jnp.where((r // GRP) == (c // GRP), value, 0.0).astype(F32)


def _row(x, k):
    r = _iota(x.shape, 0)
    return jnp.sum(jnp.where(r == k, x, 0.0), axis=0, keepdims=True)


def _rms_bwd(dxn, hh, gain):
    rstd = lax.rsqrt(jnp.mean(hh * hh, axis=-1, keepdims=True) + EPS)
    xhat = hh * rstd
    dxh = dxn * gain
    dh = rstd * (dxh - xhat * jnp.mean(dxh * xhat, axis=-1, keepdims=True))
    return dh, jnp.sum(dxn * xhat, axis=0, keepdims=True)


def _params(sem):
    return pltpu.CompilerParams(dimension_semantics=sem, vmem_limit_bytes=VMEM_LIMIT)


def _all_gather(arrs, name):
    n = len(arrs)

    def body(*refs):
        ins, outs = refs[:n], refs[n:2 * n]
        send_sems, recv_sems, local_sems = refs[2 * n:]
        x, y, c = lax.axis_index("x"), lax.axis_index("y"), lax.axis_index("c")
        me, sibling = (x, y, c), (x, y, 1 - c)
        chips = [(1 - x, y), (x, 1 - y), (1 - x, 1 - y)]

        def slot(px, py, pc):
            return 4 * px + 2 * py + pc

        def copy(a, k, block, to, src=None):
            dst = outs[a].at[slot(*block)]
            return pltpu.make_async_remote_copy(
                src_ref=dst if src is None else src, dst_ref=dst,
                send_sem=send_sems.at[a * 7 + k], recv_sem=recv_sems.at[a * 7 + k],
                device_id=to, device_id_type=MESH)

        started = []
        for a in range(n):
            mine = pltpu.make_async_copy(ins[a], outs[a].at[slot(*me)], local_sems.at[a])
            mine.start()
            started.append(mine)
        first = []
        for a in range(n):
            first.append(copy(a, 0, me, sibling, src=ins[a]))
            first += [copy(a, 1 + j, me, (*chip, c), src=ins[a]) for j, chip in enumerate(chips)]
        for cp in first:
            cp.start()
        passed = []
        for a in range(n):
            for j, chip in enumerate(chips):
                copy(a, 1 + j, (*chip, c), me).wait_recv()
                fwd = copy(a, 4 + j, (*chip, c), sibling)
                fwd.start()
                passed.append(fwd)
        for a in range(n):
            copy(a, 0, sibling, me).wait_recv()
            for j, chip in enumerate(chips):
                copy(a, 4 + j, (*chip, 1 - c), me).wait_recv()
        for cp in first + passed:
            cp.wait_send()
        for mine in started:
            mine.wait()

    hbm = pl.BlockSpec(memory_space=pl.ANY)
    return pl.pallas_call(
        body, name=name,
        out_shape=[jax.ShapeDtypeStruct((NDEV,) + a.shape, a.dtype) for a in arrs],
        in_specs=[hbm] * n, out_specs=[hbm] * n,
        scratch_shapes=[pltpu.SemaphoreType.DMA((7 * n,)), pltpu.SemaphoreType.DMA((7 * n,)),
                        pltpu.SemaphoreType.DMA((n,))],
    )(*arrs)


def _peers():
    x, y, c = lax.axis_index("x"), lax.axis_index("y"), lax.axis_index("c")
    peers = [(x ^ ((k >> 2) & 1), y ^ ((k >> 1) & 1), c ^ (k & 1)) for k in range(1, NDEV)]
    return (x, y, c), 4 * x + 2 * y + c, peers


_HBM = pl.BlockSpec(memory_space=pltpu.HBM)
_SEM = pl.BlockSpec(memory_space=pltpu.SEMAPHORE)
_ANY = pl.BlockSpec(memory_space=pl.ANY)
_EFFECT = pltpu.SideEffectType.DATAFLOW_SIDE_EFFECTING


def _transfer_start(arrs, gather, name, deps=(), direct=False):
    n, nd = len(arrs), len(deps)
    shapes = [((NDEV,) + a.shape) if gather else a.shape for a in arrs]

    def body(*refs):
        ins, lands = refs[:n], refs[n:2 * n]
        send_sems, recv_sems, local_sems = refs[2 * n + nd:2 * n + nd + 3]
        token = refs[-1]
        (x, y, c), my, peers = _peers()
        if gather and not direct:
            peers = [(x, y, 1 - c), (1 - x, y, c), (x, 1 - y, c), (1 - x, 1 - y, c)]
        for a in range(n):
            own = ins[a] if gather else ins[a].at[my]
            pltpu.make_async_copy(own, lands[a].at[my], local_sems.at[a]).start()
        for a in range(n):
            for peer in peers:
                src = ins[a] if gather else ins[a].at[4 * peer[0] + 2 * peer[1] + peer[2]]
                pltpu.make_async_remote_copy(
                    src_ref=src, dst_ref=lands[a].at[my], send_sem=send_sems.at[a], recv_sem=recv_sems.at[a],
                    device_id=peer, device_id_type=MESH).start()
        token[...] = jnp.zeros_like(token)

    out_shape = [pltpu.SemaphoreType.DMA((n,))] * 3
    out_shape += [pltpu.HBM(a.shape, a.dtype) for a in arrs]
    out_shape += [pltpu.HBM(s, a.dtype) for s, a in zip(shapes, arrs)]
    out_shape += [jax.ShapeDtypeStruct((8, 128), F32)]
    operands = [pltpu.with_memory_space_constraint(a, pltpu.HBM) for a in arrs]
    operands += [pltpu.with_memory_space_constraint(lax.empty(s, a.dtype), pltpu.HBM) for s, a in zip(shapes, arrs)]
    res = pl.pallas_call(
        body, name=name, out_shape=out_shape,
        in_specs=[_HBM] * (2 * n) + [pl.BlockSpec(memory_space=pl.ANY)] * nd,
        out_specs=[_SEM] * 3 + [_HBM] * (2 * n) + [pl.BlockSpec(memory_space=pltpu.VMEM)],
        input_output_aliases={i: 3 + i for i in range(2 * n)},
        compiler_params=pltpu.CompilerParams(has_side_effects=_EFFECT),
    )(*operands, *deps)
    return dict(sems=res[:3], src=res[3:3 + n], lands=res[3 + n:3 + 2 * n], token=res[-1], n=n,
                count=4 if gather and not direct else NDEV - 1)


def _forward_start(lands, name, shards=()):
    n, m = len(lands), len(shards)
    zones = [(NDEV,) + a.shape for a in shards]

    def body(*refs):
        zone, ins, fresh = refs[:n], refs[n:n + m], refs[n + m:n + 2 * m]
        sems = refs[n + 2 * m:n + 2 * m + (5 if m else 2)]
        token = refs[-1]
        (x, y, c), my, _ = _peers()
        for a in range(n):
            for px, py in ((1 - x, y), (x, 1 - y), (1 - x, 1 - y)):
                block = zone[a].at[4 * px + 2 * py + c]
                pltpu.make_async_remote_copy(
                    src_ref=block, dst_ref=block, send_sem=sems[0].at[a], recv_sem=sems[1].at[a],
                    device_id=(x, y, 1 - c), device_id_type=MESH).start()
        for a in range(m):
            pltpu.make_async_copy(ins[a], fresh[a].at[my], sems[4].at[a]).start()
            for peer in ((x, y, 1 - c), (1 - x, y, c), (x, 1 - y, c), (1 - x, 1 - y, c)):
                pltpu.make_async_remote_copy(
                    src_ref=ins[a], dst_ref=fresh[a].at[my], send_sem=sems[2].at[a], recv_sem=sems[3].at[a],
                    device_id=peer, device_id_type=MESH).start()
        token[...] = jnp.zeros_like(token)

    sem_shapes = [pltpu.SemaphoreType.DMA((n,))] * 2 + ([pltpu.SemaphoreType.DMA((m,))] * 3 if m else [])
    ns = len(sem_shapes)
    thru = list(lands) + list(shards)
    operands = thru + [pltpu.with_memory_space_constraint(lax.empty(s, a.dtype), pltpu.HBM)
                       for s, a in zip(zones, shards)]
    res = pl.pallas_call(
        body, name=name,
        out_shape=sem_shapes + [pltpu.HBM(a.shape, a.dtype) for a in thru]
        + [pltpu.HBM(s, a.dtype) for s, a in zip(zones, shards)] + [jax.ShapeDtypeStruct((8, 128), F32)],
        in_specs=[_HBM] * (n + 2 * m),
        out_specs=[_SEM] * ns + [_HBM] * (n + 2 * m) + [pl.BlockSpec(memory_space=pltpu.VMEM)],
        input_output_aliases={i: ns + i for i in range(n + 2 * m)},
        compiler_params=pltpu.CompilerParams(has_side_effects=_EFFECT),
    )(*[pltpu.with_memory_space_constraint(a, pltpu.HBM) for a in thru], *operands[n + m:])
    forward = dict(sems=res[:2], src=[], lands=res[ns:ns + n], token=res[-1], n=n, count=3)
    nxt = None
    if m:
        nxt = dict(sems=res[2:5], src=res[ns + n:ns + n + m], lands=res[ns + n + m:ns + n + 2 * m],
                   token=res[-1], n=m, count=4)
    return forward, nxt


def _transfer_wait(handle, after, name):
    n, count = handle["n"], handle["count"]
    src, lands, sems = list(handle["src"]), list(handle["lands"]), list(handle["sems"])
    ns = len(src)

    def body(*refs):
        zone = refs[ns:ns + n]
        sem_refs = refs[ns + n:ns + n + len(sems)]
        me, _, _ = _peers()
        for a in range(n):
            moved = zone[a].at[pl.ds(0, count)]
            both = pltpu.make_async_remote_copy(
                src_ref=moved, dst_ref=moved, send_sem=sem_refs[0].at[a], recv_sem=sem_refs[1].at[a],
                device_id=me, device_id_type=MESH)
            both.wait_send()
            both.wait_recv()
            if len(sems) == 3:
                pltpu.make_async_copy(zone[a].at[0], zone[a].at[1], sem_refs[2].at[a]).wait()

    res = pl.pallas_call(
        body, name=name,
        out_shape=[pltpu.HBM(a.shape, a.dtype) for a in src + lands],
        in_specs=[_HBM] * (ns + n) + [_SEM] * len(sems) + [pl.BlockSpec(memory_space=pl.ANY)],
        out_specs=[_HBM] * (ns + n),
        input_output_aliases={i: i for i in range(ns + n)},
        compiler_params=pltpu.CompilerParams(has_side_effects=_EFFECT),
    )(*src, *lands, *sems, after)
    return list(res[ns:])


def _ffn_fwd(h, gain, wg, wu, wd, tm, after):
    t = h.shape[0]
    nj = FF // FBX

    def body(h_ref, g_ref, wg_ref, wu_ref, wd_ref, _, out_ref, xn_ref, a_ref, b_ref, acc_ref):
        j = pl.program_id(1)

        @pl.when(j == 0)
        def _():
            hh = h_ref[...]
            rstd = lax.rsqrt(jnp.mean(hh * hh, axis=-1, keepdims=True) + EPS)
            xn_ref[...] = (hh * rstd * g_ref[...]).astype(xn_ref.dtype)
            acc_ref[...] = jnp.zeros_like(acc_ref)

        sub = min(SUB, tm)
        for r in range(tm // sub):
            rows = slice(r * sub, (r + 1) * sub)
            xn = xn_ref[rows, :]
            y = None
            for c0 in range(0, FBX, FB):
                cols = slice(c0, min(c0 + FB, FBX))
                a = _mm_nt(xn, wg_ref[cols, :])
                b = _mm_nt(xn, wu_ref[cols, :])
                a_ref[rows, cols] = a.astype(a_ref.dtype)
                b_ref[rows, cols] = b.astype(b_ref.dtype)
                part = _mm(a * _sigmoid(a) * b, wd_ref[cols, :])
                y = part if y is None else y + part
            acc_ref[rows, :] += y

        @pl.when(j == nj - 1)
        def _():
            out_ref[...] = h_ref[...] + 0.5 * acc_ref[...]

    wspec = pl.BlockSpec((FBX, D), lambda i, j: (j, 0))
    return pl.pallas_call(
        body, name="ffn_fwd", grid=(t // tm, nj),
        in_specs=[pl.BlockSpec((tm, D), lambda i, j: (i, 0)),
                  pl.BlockSpec((1, D), lambda i, j: (0, 0)), wspec, wspec, wspec, _ANY],
        out_specs=[pl.BlockSpec((tm, D), lambda i, j: (i, 0)),
                   pl.BlockSpec((tm, D), lambda i, j: (i, 0)),
                   pl.BlockSpec((tm, FBX), lambda i, j: (i, j)),
                   pl.BlockSpec((tm, FBX), lambda i, j: (i, j))],
        out_shape=[jax.ShapeDtypeStruct((t, D), F32), jax.ShapeDtypeStruct((t, D), SAVE),
                   jax.ShapeDtypeStruct((t, FF), SAVE), jax.ShapeDtypeStruct((t, FF), SAVE)],
        scratch_shapes=[pltpu.VMEM((tm, D), F32)],
        compiler_params=_params(("parallel", "arbitrary")),
    )(h, gain, wg, wu, wd, after)


def _ffn_bwd_x(dout, h, gain, a_sv, b_sv, wg, wu, wd, tm, after):
    t = h.shape[0]
    nj = FF // FBX

    def body(dout_ref, h_ref, g_ref, a_ref, b_ref, wg_ref, wu_ref, wd_ref, _,
             dh_ref, dgain_ref, dy_ref, da_ref, db_ref, s_ref, acc_ref):
        i, j = pl.program_id(0), pl.program_id(1)

        @pl.when((i == 0) & (j == 0))
        def _():
            dgain_ref[...] = jnp.zeros_like(dgain_ref)

        @pl.when(j == 0)
        def _():
            dy_ref[...] = (0.5 * dout_ref[...]).astype(dy_ref.dtype)
            acc_ref[...] = jnp.zeros_like(acc_ref)

        sub = min(SUB_X, tm)
        for r in range(tm // sub):
            rows = slice(r * sub, (r + 1) * sub)
            dy = dy_ref[rows, :]
            dx = None
            for c0 in range(0, FBX, FB):
                cols = slice(c0, min(c0 + FB, FBX))
                ds = _mm_nt(dy, wd_ref[cols, :])
                a, b = a_ref[rows, cols].astype(F32), b_ref[rows, cols].astype(F32)
                sg = _sigmoid(a)
                sa = a * sg
                da = (ds * b * (sg * (1.0 + a * (1.0 - sg)))).astype(MXU)
                db = (ds * sa).astype(MXU)
                da_ref[rows, cols] = da.astype(da_ref.dtype)
                db_ref[rows, cols] = db.astype(db_ref.dtype)
                s_ref[rows, cols] = (sa * b).astype(s_ref.dtype)
                part = _mm(da, wg_ref[cols, :]) + _mm(db, wu_ref[cols, :])
                dx = part if dx is None else dx + part
            acc_ref[rows, :] += dx

        @pl.when(j == nj - 1)
        def _():
            dh, dg = _rms_bwd(acc_ref[...], h_ref[...], g_ref[...])
            dh_ref[...] = dout_ref[...] + dh
            dgain_ref[...] += dg

    tok = pl.BlockSpec((tm, D), lambda i, j: (i, 0))
    act = pl.BlockSpec((tm, FBX), lambda i, j: (i, j))
    wspec = pl.BlockSpec((FBX, D), lambda i, j: (j, 0))
    return pl.pallas_call(
        body, name="ffn_bwd_x", grid=(t // tm, nj),
        in_specs=[tok, tok, pl.BlockSpec((1, D), lambda i, j: (0, 0)), act, act, wspec, wspec, wspec, _ANY],
        out_specs=[tok, pl.BlockSpec((1, D), lambda i, j: (0, 0)), tok, act, act, act],
        out_shape=[jax.ShapeDtypeStruct((t, D), F32), jax.ShapeDtypeStruct((1, D), F32),
                   jax.ShapeDtypeStruct((t, D), SAVE)] + [jax.ShapeDtypeStruct((t, FF), SAVE)] * 3,
        scratch_shapes=[pltpu.VMEM((tm, D), F32)],
        compiler_params=_params(("arbitrary", "arbitrary")),
    )(dout, h, gain, a_sv, b_sv, wg, wu, wd, after)


def _ffn_bwd_w(xn, dy, da, db, s, tm):
    t = xn.shape[0]
    nt = t // tm
    nj = FF // FBX

    def body(xn_ref, dy_ref, da_ref, db_ref, s_ref, dwg_ref, dwu_ref, dwd_ref, ag_scr, au_scr, ad_scr):
        i = pl.program_id(1)

        @pl.when(i == 0)
        def _():
            for ref in (ag_scr, au_scr, ad_scr):
                ref[...] = jnp.zeros_like(ref)

        xn, dy = xn_ref[...], dy_ref[...]
        for c0 in range(0, FBX, FB):
            rows = slice(c0, min(c0 + FB, FBX))
            ag_scr[rows, :] += _mm_tn(da_ref[:, rows], xn)
            au_scr[rows, :] += _mm_tn(db_ref[:, rows], xn)
            ad_scr[rows, :] += _mm_tn(s_ref[:, rows], dy)

        @pl.when(i == nt - 1)
        def _():
            for out, ref in ((dwg_ref, ag_scr), (dwu_ref, au_scr), (dwd_ref, ad_scr)):
                out[...] = ref[...].astype(out.dtype)

    tok = pl.BlockSpec((tm, D), lambda j, i: (i, 0))
    act = pl.BlockSpec((tm, FBX), lambda j, i: (i, j))
    wspec = pl.BlockSpec((FBX, D), lambda j, i: (j, 0))
    return pl.pallas_call(
        body, name="ffn_bwd_w", grid=(nj, nt),
        in_specs=[tok, tok, act, act, act], out_specs=[wspec] * 3,
        out_shape=[jax.ShapeDtypeStruct((FF, D), WIRE)] * 3,
        scratch_shapes=[pltpu.VMEM((FBX, D), F32)] * 3,
        compiler_params=_params(("parallel", "arbitrary")),
    )(xn, dy, da, db, s)


def _inproj_fwd(h, gain, win, tm, after):
    t = h.shape[0]

    def body(h_ref, g_ref, w_ref, _, z_ref, xn_ref):
        hh = h_ref[...]
        rstd = lax.rsqrt(jnp.mean(hh * hh, axis=-1, keepdims=True) + EPS)
        xn = (hh * rstd * g_ref[...]).astype(MXU)
        xn_ref[...] = xn.astype(xn_ref.dtype)
        for j in range(DIN // ZB):
            z_ref[:, j * ZB:(j + 1) * ZB] = _mm_nt(xn, w_ref[j * ZB:(j + 1) * ZB, :])

    return pl.pallas_call(
        body, name="inproj_fwd", grid=(t // tm,),
        in_specs=[pl.BlockSpec((tm, D), lambda i: (i, 0)),
                  pl.BlockSpec((1, D), lambda i: (0, 0)),
                  pl.BlockSpec((DIN, D), lambda i: (0, 0)), _ANY],
        out_specs=[pl.BlockSpec((tm, DIN), lambda i: (i, 0)),
                   pl.BlockSpec((tm, D), lambda i: (i, 0))],
        out_shape=[jax.ShapeDtypeStruct((t, DIN), F32), jax.ShapeDtypeStruct((t, D), SAVE)],
        compiler_params=_params(("parallel",)),
    )(h, gain, win, after)


def _inproj_bwd_x(dres, dz, h, gain, win, tm, after):
    t = h.shape[0]

    def body(dres_ref, dz_ref, h_ref, g_ref, w_ref, _, dh_ref, dgain_ref):
        @pl.when(pl.program_id(0) == 0)
        def _():
            dgain_ref[...] = jnp.zeros_like(dgain_ref)

        dh, dg = _rms_bwd(_mm(dz_ref[...], w_ref[...]), h_ref[...], g_ref[...])
        dh_ref[...] = dres_ref[...] + dh
        dgain_ref[...] += dg

    return pl.pallas_call(
        body, name="inproj_bwd_x", grid=(t // tm,),
        in_specs=[pl.BlockSpec((tm, D), lambda i: (i, 0)),
                  pl.BlockSpec((tm, DIN), lambda i: (i, 0)),
                  pl.BlockSpec((tm, D), lambda i: (i, 0)),
                  pl.BlockSpec((1, D), lambda i: (0, 0)),
                  pl.BlockSpec((DIN, D), lambda i: (0, 0)), _ANY],
        out_specs=[pl.BlockSpec((tm, D), lambda i: (i, 0)),
                   pl.BlockSpec((1, D), lambda i: (0, 0))],
        out_shape=[jax.ShapeDtypeStruct((t, D), F32), jax.ShapeDtypeStruct((1, D), F32)],
        compiler_params=_params(("arbitrary",)),
    )(dres, dz, h, gain, win, after)


def _inproj_bwd_w(xn, dz, tm):
    t = xn.shape[0]
    nt = t // tm

    def body(xn_ref, dz_ref, dw_ref, acc_scr):
        i = pl.program_id(1)

        @pl.when(i == 0)
        def _():
            acc_scr[...] = jnp.zeros_like(acc_scr)

        acc_scr[...] += _mm_tn(dz_ref[...], xn_ref[...])

        @pl.when(i == nt - 1)
        def _():
            dw_ref[...] = acc_scr[...].astype(dw_ref.dtype)

    return pl.pallas_call(
        body, name="inproj_bwd_w", grid=(DIN // ZBW, nt),
        in_specs=[pl.BlockSpec((tm, D), lambda j, i: (i, 0)),
                  pl.BlockSpec((tm, ZBW), lambda j, i: (i, j))],
        out_specs=pl.BlockSpec((ZBW, D), lambda j, i: (j, 0)),
        out_shape=jax.ShapeDtypeStruct((DIN, D), WIRE),
        scratch_shapes=[pltpu.VMEM((ZBW, D), F32)],
        compiler_params=_params(("parallel", "arbitrary")),
    )(xn, dz)


def _outproj_fwd(h, oa, ob, oc, wout, tm):
    t = h.shape[0]

    def body(h_ref, oa_ref, ob_ref, oc_ref, w_ref, out_ref):
        ym = jnp.concatenate([oa_ref[...], ob_ref[...], oc_ref[...]], axis=1)
        out_ref[...] = h_ref[...] + _mm(ym, w_ref[...])

    return pl.pallas_call(
        body, name="outproj_fwd", grid=(t // tm,),
        in_specs=[pl.BlockSpec((tm, D), lambda i: (i, 0)),
                  pl.BlockSpec((tm, DA), lambda i: (i, 0)),
                  pl.BlockSpec((tm, DB), lambda i: (i, 0)),
                  pl.BlockSpec((tm, DC), lambda i: (i, 0)),
                  pl.BlockSpec((D, D), lambda i: (0, 0))],
        out_specs=pl.BlockSpec((tm, D), lambda i: (i, 0)),
        out_shape=jax.ShapeDtypeStruct((t, D), F32),
        compiler_params=_params(("parallel",)),
    )(h, oa, ob, oc, wout)


def _outproj_bwd(dh, oa, ob, oc, wout, tm):
    t = dh.shape[0]
    nt = t // tm

    def body(dh_ref, oa_ref, ob_ref, oc_ref, w_ref, da_ref, db_ref, dc_ref, dw_ref, acc_scr):
        i = pl.program_id(0)

        @pl.when(i == 0)
        def _():
            acc_scr[...] = jnp.zeros_like(acc_scr)

        d16 = dh_ref[...].astype(MXU)
        dym = _mm_nt(d16, w_ref[...])
        da_ref[...] = dym[:, :DA]
        db_ref[...] = dym[:, DA:DA + DB]
        dc_ref[...] = dym[:, DA + DB:]
        ym = jnp.concatenate([oa_ref[...], ob_ref[...], oc_ref[...]], axis=1)
        acc_scr[...] += _mm_tn(ym, d16)

        @pl.when(i == nt - 1)
        def _():
            dw_ref[...] = acc_scr[...].astype(dw_ref.dtype)

    return pl.pallas_call(
        body, name="outproj_bwd", grid=(nt,),
        in_specs=[pl.BlockSpec((tm, D), lambda i: (i, 0)),
                  pl.BlockSpec((tm, DA), lambda i: (i, 0)),
                  pl.BlockSpec((tm, DB), lambda i: (i, 0)),
                  pl.BlockSpec((tm, DC), lambda i: (i, 0)),
                  pl.BlockSpec((D, D), lambda i: (0, 0))],
        out_specs=[pl.BlockSpec((tm, DA), lambda i: (i, 0)),
                   pl.BlockSpec((tm, DB), lambda i: (i, 0)),
                   pl.BlockSpec((tm, DC), lambda i: (i, 0)),
                   pl.BlockSpec((D, D), lambda i: (0, 0))],
        out_shape=[jax.ShapeDtypeStruct((t, DA), F32), jax.ShapeDtypeStruct((t, DB), F32),
                   jax.ShapeDtypeStruct((t, DC), F32), jax.ShapeDtypeStruct((D, D), WIRE)],
        scratch_shapes=[pltpu.VMEM((D, D), F32)],
        compiler_params=_params(("arbitrary",)),
    )(dh, oa, ob, oc, wout)


def _lower_bounds(logits):
    depth, n = logits.shape

    def body(l_ref, lb_ref, p_ref):
        rows = [l_ref[l:l + 1, :] for l in range(depth)]
        mx = functools.reduce(jnp.maximum, rows)
        ex = [jnp.exp(r - mx) for r in rows]
        den = functools.reduce(lambda u, v: u + v, ex)
        acc = jnp.zeros_like(den)
        for l in range(depth):
            p = ex[l] / den
            p_ref[l:l + 1, :] = p
            if l > 0:
                acc = acc + p
            lb_ref[l:l + 1, :] = acc

    return pl.pallas_call(
        body, name="lower_bounds",
        out_shape=[jax.ShapeDtypeStruct((depth, n), F32), jax.ShapeDtypeStruct((depth, n), F32)],
    )(logits)


def _lower_bounds_bwd(p, dlb):
    depth, n = p.shape

    def body(p_ref, d_ref, out_ref):
        ps = [p_ref[l:l + 1, :] for l in range(depth)]
        ds = [d_ref[l:l + 1, :] for l in range(depth)]
        dp = [jnp.zeros_like(ps[0]) for _ in range(depth)]
        run = jnp.zeros_like(ps[0])
        for l in range(depth - 1, 0, -1):
            run = run + ds[l]
            dp[l] = run
        dot = functools.reduce(lambda u, v: u + v, [ps[l] * dp[l] for l in range(depth)])
        for l in range(depth):
            out_ref[l:l + 1, :] = ps[l] * (dp[l] - dot)

    return pl.pallas_call(body, name="lower_bounds_bwd", out_shape=jax.ShapeDtypeStruct((depth, n), F32))(p, dlb)


def _hgrn_block(z_ref, lb_ref, rb):
    q, fl = z_ref[:, 0:DA], z_ref[:, DA:2 * DA]
    lb = lb_ref[...]
    sq = _sigmoid(q)
    qs = q * sq
    sg = _sigmoid(fl)
    f = lb + (1.0 - lb) * sg
    k = 1.0 - f
    lf = jnp.log(f)
    row, col = _iota((rb, rb), 0), _iota((rb, rb), 1)
    same = (row // ACH) == (col // ACH)
    causal = same & (row >= col)
    b = _mm_exact_l(jnp.where(causal, 1.0, 0.0).astype(MXU), lf)
    bend = _mm_exact_l(jnp.where(same, 1.0, 0.0).astype(MXU), lf)
    r = 0.5 * bend
    eq, ek, eb, ed = jnp.exp(b - r), jnp.exp(r - b), jnp.exp(b), jnp.exp(bend - b)
    return dict(q=q, lb=lb, sq=sq, sg=sg, f=f, bend=bend, eq=eq, ek=ek, eb=eb, ed=ed,
                qt=qs * eq, kt=k * ek, qe=qs * eb, kd=k * ed, same=same, causal=causal)


def _hgrn_fwd(z, lb, gain):
    t = z.shape[0]
    nc = t // ACH
    cb = min(ACB, nc)
    rb = cb * ACH

    def body(z_ref, lb_ref, g_ref, o_ref, oa_ref, st_ref, st_scr):
        @pl.when(pl.program_id(0) == 0)
        def _():
            st_scr[...] = jnp.zeros_like(st_scr)

        c = _hgrn_block(z_ref, lb_ref, rb)
        for hd in range(NH):
            cols = slice(hd * HD, (hd + 1) * HD)
            v = z_ref[:, 2 * DA + hd * HD:2 * DA + (hd + 1) * HD]
            gg = z_ref[:, 3 * DA + hd * HD:3 * DA + (hd + 1) * HD]
            att = jnp.where(c["causal"], _mm_nt(c["qt"][:, cols], c["kt"][:, cols]), 0.0)
            o_in = _mm(att, v)
            qe, kd, bend = c["qe"][:, cols], c["kd"][:, cols], c["bend"][:, cols]
            st = st_scr[hd]
            outs = []
            for cc in range(cb):
                rows = slice(cc * ACH, (cc + 1) * ACH)
                st_ref[cc, hd] = st
                outs.append(o_in[rows] + _mm_nt(qe[rows], st))
                decay = jnp.exp(jnp.max(bend[rows], axis=0, keepdims=True))
                st = st * decay + _mm_tn(v[rows], kd[rows])
            st_scr[hd] = st
            o = jnp.concatenate(outs, axis=0)
            o_ref[:, cols] = o
            rstd = lax.rsqrt(jnp.mean(o * o, axis=-1, keepdims=True) + EPS)
            oa_ref[:, cols] = (o * rstd * g_ref[:, cols] * (gg * _sigmoid(gg))).astype(oa_ref.dtype)

    return pl.pallas_call(
        body, name="hgrn_fwd", grid=(nc // cb,),
        in_specs=[pl.BlockSpec((rb, 4 * DA), lambda c: (c, 0)),
                  pl.BlockSpec((1, DA), lambda c: (0, 0)),
                  pl.BlockSpec((1, DA), lambda c: (0, 0))],
        out_specs=[pl.BlockSpec((rb, DA), lambda c: (c, 0)),
                   pl.BlockSpec((rb, DA), lambda c: (c, 0)),
                   pl.BlockSpec((cb, NH, HD, HD), lambda c: (c, 0, 0, 0))],
        out_shape=[jax.ShapeDtypeStruct((t, DA), F32), jax.ShapeDtypeStruct((t, DA), SAVE),
                   jax.ShapeDtypeStruct((nc, NH, HD, HD), F32)],
        scratch_shapes=[pltpu.VMEM((NH, HD, HD), F32)],
        compiler_params=_params(("arbitrary",)),
    )(z, lb, gain)


def _hgrn_bwd(z, lb, gain, o, states, doa, after):
    t = z.shape[0]
    nc = t // ACH
    cb = min(ACB, nc)
    rb = cb * ACH
    nblk = nc // cb

    def body(z_ref, lb_ref, g_ref, o_ref, st_ref, doa_ref, _, dz_ref, dgain_ref, dlb_ref, dst_scr):
        @pl.when(pl.program_id(0) == 0)
        def _():
            dst_scr[...] = jnp.zeros_like(dst_scr)
            dgain_ref[...] = jnp.zeros_like(dgain_ref)
            dlb_ref[...] = jnp.zeros_like(dlb_ref)

        c = _hgrn_block(z_ref, lb_ref, rb)
        dbs, dqss, dks = [], [], []
        for hd in range(NH):
            cols = slice(hd * HD, (hd + 1) * HD)
            v = z_ref[:, 2 * DA + hd * HD:2 * DA + (hd + 1) * HD]
            gg = z_ref[:, 3 * DA + hd * HD:3 * DA + (hd + 1) * HD]
            qt, kt, qe, kd, bend = (c[n][:, cols] for n in ("qt", "kt", "qe", "kd", "bend"))
            o = o_ref[:, cols]
            do_a = doa_ref[:, cols]
            gain = g_ref[:, cols]
            sgg = _sigmoid(gg)
            silu_g = gg * sgg
            rstd = lax.rsqrt(jnp.mean(o * o, axis=-1, keepdims=True) + EPS)
            n = o * rstd
            dn = do_a * gain * silu_g
            dg = do_a * n * gain * (sgg * (1.0 + gg * (1.0 - sgg)))
            dgain_ref[:, cols] += jnp.sum(do_a * silu_g * n, axis=0, keepdims=True)
            d_o = rstd * (dn - n * jnp.mean(dn * n, axis=-1, keepdims=True))

            att = jnp.where(c["causal"], _mm_nt(qt, kt), 0.0)
            datt = jnp.where(c["causal"], _mm_nt(d_o, v), 0.0)
            dv_in = _mm_tn(att, d_o)
            dqt = _mm(datt, kt)
            dkt = _mm_tn(datt, qt)
            dsp = dst_scr[hd]
            dvs, dqes, dkds, dbends = [None] * cb, [None] * cb, [None] * cb, [None] * cb
            for cc in reversed(range(cb)):
                rows = slice(cc * ACH, (cc + 1) * ACH)
                st = st_ref[cc, hd]
                dvs[cc] = dv_in[rows] + _mm_nt(kd[rows], dsp)
                dqes[cc] = _mm(d_o[rows], st)
                dkds[cc] = _mm(v[rows], dsp)
                decay = jnp.exp(jnp.max(bend[rows], axis=0, keepdims=True))
                dbend = (decay * jnp.sum(st * dsp, axis=0, keepdims=True)
                         + jnp.sum(dkds[cc] * kd[rows], axis=0, keepdims=True))
                dbends[cc] = jnp.broadcast_to(dbend, (ACH, HD))
                dsp = dsp * decay + _mm_tn(d_o[rows], qe[rows])
            dst_scr[hd] = dsp
            dv, dqe, dkd, dbend = (jnp.concatenate(p, axis=0) for p in (dvs, dqes, dkds, dbends))
            dbs.append((dqt * qt + dqe * qe - dkt * kt - dkd * kd, dbend))
            dqss.append(dqt * c["eq"][:, cols] + dqe * c["eb"][:, cols])
            dks.append(dkt * c["ek"][:, cols] + dkd * c["ed"][:, cols])
            c0 = hd * HD
            dz_ref[:, 2 * DA + c0:2 * DA + c0 + HD] = dv.astype(dz_ref.dtype)
            dz_ref[:, 3 * DA + c0:3 * DA + c0 + HD] = dg.astype(dz_ref.dtype)

        db = jnp.concatenate([p[0] for p in dbs], axis=1)
        dbend = jnp.concatenate([p[1] for p in dbs], axis=1)
        dqs, dk = jnp.concatenate(dqss, axis=1), jnp.concatenate(dks, axis=1)
        row, col = _iota((rb, rb), 0), _iota((rb, rb), 1)
        upper = jnp.where(c["same"] & (row <= col), 1.0, 0.0).astype(MXU)
        dlf = _mm_exact_l(upper, db) + dbend
        df = dlf / c["f"] - dk
        sg, sq, q = c["sg"], c["sq"], c["q"]
        dlb_ref[...] += jnp.sum(df * (1.0 - sg), axis=0, keepdims=True)
        dz_ref[:, DA:2 * DA] = (df * (1.0 - c["lb"]) * sg * (1.0 - sg)).astype(dz_ref.dtype)
        dz_ref[:, 0:DA] = (dqs * (sq * (1.0 + q * (1.0 - sq)))).astype(dz_ref.dtype)

    rev = lambda c: (nblk - 1 - c, 0)
    return pl.pallas_call(
        body, name="hgrn_bwd", grid=(nblk,),
        in_specs=[pl.BlockSpec((rb, 4 * DA), rev),
                  pl.BlockSpec((1, DA), lambda c: (0, 0)),
                  pl.BlockSpec((1, DA), lambda c: (0, 0)),
                  pl.BlockSpec((rb, DA), rev),
                  pl.BlockSpec((cb, NH, HD, HD), lambda c: (nblk - 1 - c, 0, 0, 0)),
                  pl.BlockSpec((rb, DA), rev), _ANY],
        out_specs=[pl.BlockSpec((rb, 4 * DA), rev),
                   pl.BlockSpec((1, DA), lambda c: (0, 0)),
                   pl.BlockSpec((1, DA), lambda c: (0, 0))],
        out_shape=[jax.ShapeDtypeStruct((t, DIN), SAVE), jax.ShapeDtypeStruct((1, DA), F32),
                   jax.ShapeDtypeStruct((1, DA), F32)],
        scratch_shapes=[pltpu.VMEM((NH, HD, HD), F32)],
        compiler_params=_params(("arbitrary",)),
    )(z, lb, gain, o, states, doa, after)


def _shift_down(prev8, x, k):
    cat = jnp.concatenate([prev8, x], axis=0)
    return pltpu.roll(cat, k, axis=0)[8:, :]


def _shift_up(x, next8, k):
    n = x.shape[0]
    cat = jnp.concatenate([x, next8], axis=0)
    return pltpu.roll(cat, n + 8 - k, axis=0)[:n, :]


def _lru_gates(x, prev8, cw_ref, vec_ref, wa_ref, wx_ref):
    xs = [x, _shift_down(prev8, x, 1), _shift_down(prev8, x, 2), _shift_down(prev8, x, 3)]
    xc = vec_ref[0:1, :] + cw_ref[3:4, :] * xs[0] + cw_ref[2:3, :] * xs[1] + cw_ref[1:2, :] * xs[2] + cw_ref[0:1, :] * xs[3]
    r = _sigmoid(_mm(xc, wa_ref[...]) + vec_ref[1:2, :])
    gi = _sigmoid(_mm(xc, wx_ref[...]) + vec_ref[2:3, :])
    lam = vec_ref[3:4, :]
    sp = jnp.maximum(-lam, 0.0) + jnp.log(1.0 + jnp.exp(-jnp.abs(lam)))
    la = -LRU_C * r * sp
    a = jnp.exp(la)
    mult = jnp.sqrt(-_expm1(2.0 * la))
    return xs, xc, r, gi, sp, a, mult


def _scan_down(a, u):
    n = a.shape[0]
    row = _iota(a.shape, 0)
    s = 1
    while s < n:
        keep = row >= s
        ash = jnp.where(keep, pltpu.roll(a, s, axis=0), 1.0)
        ush = jnp.where(keep, pltpu.roll(u, s, axis=0), 0.0)
        u = a * ush + u
        a = a * ash
        s *= 2
    return a, u


def _scan_up(a, u):
    n = a.shape[0]
    row = _iota(a.shape, 0)
    s = 1
    while s < n:
        keep = row < n - s
        ash = jnp.where(keep, pltpu.roll(a, n - s, axis=0), 1.0)
        ush = jnp.where(keep, pltpu.roll(u, n - s, axis=0), 0.0)
        u = a * ush + u
        a = a * ash
        s *= 2
    return a, u


def _lru_fwd(z, cw, vec, wa, wx, tb):
    t = z.shape[0]
    xcol, gcol = (4 * DA) // DB, (4 * DA) // DB + 1

    def body(x_ref, gate_ref, cw_ref, vec_ref, wa_ref, wx_ref, ob_ref, h_ref, xprev_scr, hc_scr):
        @pl.when(pl.program_id(0) == 0)
        def _():
            xprev_scr[...] = jnp.zeros_like(xprev_scr)
            hc_scr[...] = jnp.zeros_like(hc_scr)

        x = x_ref[...]
        _, xc, _, gi, _, a, mult = _lru_gates(x, xprev_scr[...], cw_ref, vec_ref, wa_ref, wx_ref)
        acum, hloc = _scan_down(a, mult * gi * xc)
        h = hloc + acum * hc_scr[0:1, :]
        h_ref[...] = h
        hc_scr[...] = jnp.broadcast_to(_row(h, tb - 1), hc_scr.shape)
        xprev_scr[...] = x[tb - 8:, :]
        y = h * _gelu(gate_ref[...])
        ms = _mm_exact_r(y * y, _group_matrix(DB, 1.0 / GRP).astype(MXU))
        ob_ref[...] = (y * lax.rsqrt(ms + EPS) * vec_ref[4:5, :]).astype(ob_ref.dtype)

    return pl.pallas_call(
        body, name="lru_fwd", grid=(t // tb,),
        in_specs=[pl.BlockSpec((tb, DB), lambda i: (i, xcol)),
                  pl.BlockSpec((tb, DB), lambda i: (i, gcol)),
                  pl.BlockSpec((8, DB), lambda i: (0, 0)),
                  pl.BlockSpec((8, DB), lambda i: (0, 0)),
                  pl.BlockSpec((DB, DB), lambda i: (0, 0)),
                  pl.BlockSpec((DB, DB), lambda i: (0, 0))],
        out_specs=[pl.BlockSpec((tb, DB), lambda i: (i, 0)),
                   pl.BlockSpec((tb, DB), lambda i: (i, 0))],
        out_shape=[jax.ShapeDtypeStruct((t, DB), SAVE), jax.ShapeDtypeStruct((t, DB), F32)],
        scratch_shapes=[pltpu.VMEM((8, DB), F32), pltpu.VMEM((8, DB), F32)],
        compiler_params=_params(("arbitrary",)),
    )(z, z, cw, vec, wa, wx)


def _lru_bwd(z, hseq, dob, cw, vec, wa, wx, dz, tb):
    t = z.shape[0]
    nb = t // tb
    xcol, gcol = (4 * DA) // DB, (4 * DA) // DB + 1
    per = tb // 8

    def body(x_ref, xh_ref, gate_ref, h_ref, hh_ref, dob_ref, cw_ref, vec_ref, wa_ref, wx_ref, _,
             dz_ref, dcw_ref, dvec_ref, dwa_ref, dwx_ref, gc_scr, an_scr, dxc_scr):
        step = pl.program_id(0)
        blk = nb - 1 - step

        @pl.when(step == 0)
        def _():
            for ref in (gc_scr, an_scr, dxc_scr, dcw_ref, dvec_ref, dwa_ref, dwx_ref):
                ref[...] = jnp.zeros_like(ref)

        first = (blk > 0).astype(F32)
        x = x_ref[...]
        xs, xc, r, gi, sp, a, mult = _lru_gates(x, xh_ref[...] * first, cw_ref, vec_ref, wa_ref, wx_ref)
        h = h_ref[...]
        hprev = _shift_down(hh_ref[...] * first, h, 1)
        ge, dge = _gelu_and_grad(gate_ref[...])
        y = h * ge
        gmat = _group_matrix(DB, 1.0 / GRP).astype(MXU)
        rstd = lax.rsqrt(_mm_exact_r(y * y, gmat) + EPS)
        n = y * rstd
        d_ob = dob_ref[...]
        dn = d_ob * vec_ref[4:5, :]
        dvec_ref[4:5, :] += jnp.sum(d_ob * n, axis=0, keepdims=True)
        dy = rstd * (dn - n * _mm_exact_r(dn * n, gmat))
        dh = dy * ge
        dgate = dy * h * dge

        row = _iota(a.shape, 0)
        anext = jnp.where(row == tb - 1, an_scr[0:1, :], pltpu.roll(a, tb - 1, axis=0))
        acum, gloc = _scan_up(anext, dh)
        g = gloc + acum * gc_scr[0:1, :]
        gc_scr[...] = jnp.broadcast_to(_row(g, 0), gc_scr.shape)
        an_scr[...] = jnp.broadcast_to(_row(a, 0), an_scr.shape)

        da = g * hprev
        dmult = g * gi * xc
        dgi = g * mult * xc
        dxc = g * mult * gi
        dla = da * a - dmult * (a * a) / mult
        dr = dla * (-LRU_C * sp)
        dsp = jnp.sum(dla * (-LRU_C * r), axis=0, keepdims=True)
        lam = vec_ref[3:4, :]
        dvec_ref[3:4, :] += -dsp * _sigmoid(-lam)
        dpa = dr * r * (1.0 - r)
        dpx = dgi * gi * (1.0 - gi)
        dwa_ref[...] += _mm_tn(xc, dpa)
        dwx_ref[...] += _mm_tn(xc, dpx)
        dvec_ref[1:2, :] += jnp.sum(dpa, axis=0, keepdims=True)
        dvec_ref[2:3, :] += jnp.sum(dpx, axis=0, keepdims=True)
        dxc = dxc + _mm_nt(dpa, wa_ref[...]) + _mm_nt(dpx, wx_ref[...])
        dvec_ref[0:1, :] += jnp.sum(dxc, axis=0, keepdims=True)
        for tap in range(4):
            dcw_ref[tap:tap + 1, :] += jnp.sum(dxc * xs[3 - tap], axis=0, keepdims=True)
        nxt = dxc_scr[...]
        dx = (cw_ref[3:4, :] * dxc + cw_ref[2:3, :] * _shift_up(dxc, nxt, 1)
              + cw_ref[1:2, :] * _shift_up(dxc, nxt, 2) + cw_ref[0:1, :] * _shift_up(dxc, nxt, 3))
        dxc_scr[...] = dxc[:8, :]
        dz_ref[:, :DB] = dx.astype(dz_ref.dtype)
        dz_ref[:, DB:] = dgate.astype(dz_ref.dtype)

    def halo(col):
        return lambda s: (jnp.maximum((nb - 1 - s) * per - 1, 0), col)

    const = lambda s: (0, 0)
    return pl.pallas_call(
        body, name="lru_bwd", grid=(nb,),
        in_specs=[pl.BlockSpec((tb, DB), lambda s: (nb - 1 - s, xcol)),
                  pl.BlockSpec((8, DB), halo(xcol)),
                  pl.BlockSpec((tb, DB), lambda s: (nb - 1 - s, gcol)),
                  pl.BlockSpec((tb, DB), lambda s: (nb - 1 - s, 0)),
                  pl.BlockSpec((8, DB), halo(0)),
                  pl.BlockSpec((tb, DB), lambda s: (nb - 1 - s, 0)),
                  pl.BlockSpec((8, DB), const), pl.BlockSpec((8, DB), const),
                  pl.BlockSpec((DB, DB), const), pl.BlockSpec((DB, DB), const),
                  pl.BlockSpec(memory_space=pl.ANY)],
        out_specs=[pl.BlockSpec((tb, 2 * DB), lambda s: (nb - 1 - s, (4 * DA) // (2 * DB))),
                   pl.BlockSpec((8, DB), const), pl.BlockSpec((8, DB), const),
                   pl.BlockSpec((DB, DB), const), pl.BlockSpec((DB, DB), const)],
        out_shape=[jax.ShapeDtypeStruct((t, DIN), SAVE), jax.ShapeDtypeStruct((8, DB), F32),
                   jax.ShapeDtypeStruct((8, DB), F32), jax.ShapeDtypeStruct((DB, DB), F32),
                   jax.ShapeDtypeStruct((DB, DB), F32)],
        scratch_shapes=[pltpu.VMEM((8, DB), F32), pltpu.VMEM((8, DB), F32), pltpu.VMEM((8, DB), F32)],
        input_output_aliases={10: 0},
        compiler_params=_params(("arbitrary",)),
    )(z, z, z, hseq, hseq, dob, cw, vec, wa, wx, dz)


def _sgu_block(u_ref, v_ref, w_ref, b_ref, gmat, tb):
    uu, duu = _gelu_and_grad(u_ref[...])
    vv, dvv = _gelu_and_grad(v_ref[...])
    dlt = vv - _mm_exact_r(vv, gmat)
    rstd_v = lax.rsqrt(_mm_exact_r(dlt * dlt, gmat) + EPS)
    vn = dlt * rstd_v
    col = _iota((CCH, DC), 1) // GRP
    causal = _iota((CCH, CCH), 0) >= _iota((CCH, CCH), 1)
    ws = [jnp.where(causal, w_ref[g], 0.0) for g in range(DC // GRP)]
    zs = []
    for ch in range(tb // CCH):
        vn_c = vn[ch * CCH:(ch + 1) * CCH]
        zz = b_ref[...]
        for g, w in enumerate(ws):
            zz = zz + jnp.where(col == g, _mm(w, vn_c), 0.0)
        zs.append(zz)
    return uu, duu, dvv, rstd_v, vn, jnp.concatenate(zs, axis=0), ws, col, causal


def _sgu_fwd(z, w, bias, gain, tb):
    t = z.shape[0]
    ucol, vcol = (4 * DA + 2 * DB) // DC, (4 * DA + 2 * DB) // DC + 1

    def body(u_ref, v_ref, w_ref, b_ref, g_ref, oc_ref):
        gmat = _group_matrix(DC, 1.0 / GRP).astype(MXU)
        uu, _, _, _, _, zz, _, _, _ = _sgu_block(u_ref, v_ref, w_ref, b_ref, gmat, tb)
        y = uu * zz
        ms = _mm_exact_r(y * y, gmat)
        oc_ref[...] = (y * lax.rsqrt(ms + EPS) * g_ref[...]).astype(oc_ref.dtype)

    const = lambda i: (0, 0)
    return pl.pallas_call(
        body, name="sgu_fwd", grid=(t // tb,),
        in_specs=[pl.BlockSpec((tb, DC), lambda i: (i, ucol)),
                  pl.BlockSpec((tb, DC), lambda i: (i, vcol)),
                  pl.BlockSpec((DC // GRP, CCH, CCH), lambda i: (0, 0, 0)),
                  pl.BlockSpec((CCH, DC), const), pl.BlockSpec((1, DC), const)],
        out_specs=pl.BlockSpec((tb, DC), lambda i: (i, 0)),
        out_shape=jax.ShapeDtypeStruct((t, DC), SAVE),
        compiler_params=_params(("parallel",)),
    )(z, z, w, bias, gain)


def _sgu_bwd(z, doc, w, bias, gain, dz, tb):
    t = z.shape[0]
    nb = t // tb
    ucol, vcol = (4 * DA + 2 * DB) // DC, (4 * DA + 2 * DB) // DC + 1
    ng = DC // GRP

    def body(u_ref, v_ref, doc_ref, w_ref, b_ref, g_ref, _, dz_ref, dw_ref, dbias_ref, dgain_ref, dbsum_scr):
        i = pl.program_id(0)

        @pl.when(i == 0)
        def _():
            for ref in (dw_ref, dgain_ref, dbsum_scr):
                ref[...] = jnp.zeros_like(ref)

        gmat = _group_matrix(DC, 1.0 / GRP).astype(MXU)
        uu, duu, dvv, rstd_v, vn, zz, ws, col, causal = _sgu_block(u_ref, v_ref, w_ref, b_ref, gmat, tb)
        y = uu * zz
        rstd = lax.rsqrt(_mm_exact_r(y * y, gmat) + EPS)
        n = y * rstd
        d_oc = doc_ref[...]
        dn = d_oc * g_ref[...]
        dgain_ref[0:1, :] += jnp.sum(d_oc * n, axis=0, keepdims=True)
        dy = rstd * (dn - n * _mm_exact_r(dn * n, gmat))
        dzz = dy * uu
        dz_ref[:, :DC] = (dy * zz * duu).astype(dz_ref.dtype)
        dvns = []
        for ch in range(tb // CCH):
            rows = slice(ch * CCH, (ch + 1) * CCH)
            dzz_c, vn_c = dzz[rows], vn[rows]
            dbsum_scr[...] += dzz_c
            dvn = jnp.zeros_like(dzz_c)
            for g in range(ng):
                sel = col == g
                dvn = dvn + jnp.where(sel, _mm_tn(ws[g], dzz_c), 0.0)
                dw_ref[g] += jnp.where(causal, _mm_nt(jnp.where(sel, dzz_c, 0.0), vn_c), 0.0)
            dvns.append(dvn)
        dvn = jnp.concatenate(dvns, axis=0)
        dv = rstd_v * (dvn - _mm_exact_r(dvn, gmat) - vn * _mm_exact_r(dvn * vn, gmat))
        dz_ref[:, DC:] = (dv * dvv).astype(dz_ref.dtype)

        @pl.when(i == nb - 1)
        def _():
            dbias_ref[...] = _mm_exact_r(dbsum_scr[...], _group_matrix(DC, 1.0).astype(MXU))

    const = lambda i: (0, 0)
    return pl.pallas_call(
        body, name="sgu_bwd", grid=(nb,),
        in_specs=[pl.BlockSpec((tb, DC), lambda i: (i, ucol)),
                  pl.BlockSpec((tb, DC), lambda i: (i, vcol)),
                  pl.BlockSpec((tb, DC), lambda i: (i, 0)),
                  pl.BlockSpec((ng, CCH, CCH), lambda i: (0, 0, 0)),
                  pl.BlockSpec((CCH, DC), const), pl.BlockSpec((1, DC), const),
                  pl.BlockSpec(memory_space=pl.ANY)],
        out_specs=[pl.BlockSpec((tb, 2 * DC), lambda i: (i, (4 * DA + 2 * DB) // (2 * DC))),
                   pl.BlockSpec((ng, CCH, CCH), lambda i: (0, 0, 0)),
                   pl.BlockSpec((CCH, DC), const), pl.BlockSpec((8, DC), const)],
        out_shape=[jax.ShapeDtypeStruct((t, DIN), SAVE), jax.ShapeDtypeStruct((ng, CCH, CCH), F32),
                   jax.ShapeDtypeStruct((CCH, DC), F32), jax.ShapeDtypeStruct((8, DC), F32)],
        scratch_shapes=[pltpu.VMEM((CCH, DC), F32)],
        input_output_aliases={6: 0},
        compiler_params=_params(("arbitrary",)),
    )(z, z, doc, w, bias, gain, dz)


def _head(h, gain, target, tm):
    t = h.shape[0]

    def body(h_ref, g_ref, t_ref, dh_ref, loss_ref, dgain_ref):
        @pl.when(pl.program_id(0) == 0)
        def _():
            loss_ref[...] = jnp.zeros_like(loss_ref)
            dgain_ref[...] = jnp.zeros_like(dgain_ref)

        hh = h_ref[...]
        gain = g_ref[...]
        rstd = lax.rsqrt(jnp.mean(hh * hh, axis=-1, keepdims=True) + EPS)
        xhat = hh * rstd
        err = xhat * gain - t_ref[...]
        per_tok = jnp.mean(err * err, axis=-1, keepdims=True)
        loss_ref[...] += 0.5 * jnp.sum(per_tok, axis=0, keepdims=True)
        dy = err * (1.0 / D)
        dgain_ref[...] += jnp.sum(dy * xhat, axis=0, keepdims=True)
        dxh = dy * gain
        dh_ref[...] = rstd * (dxh - xhat * jnp.mean(dxh * xhat, axis=-1, keepdims=True))

    return pl.pallas_call(
        body, name="head", grid=(t // tm,),
        in_specs=[pl.BlockSpec((tm, D), lambda i: (i, 0)),
                  pl.BlockSpec((1, D), lambda i: (0, 0)),
                  pl.BlockSpec((tm, D), lambda i: (i, 0))],
        out_specs=[pl.BlockSpec((tm, D), lambda i: (i, 0)),
                   pl.BlockSpec((1, 128), lambda i: (0, 0)),
                   pl.BlockSpec((1, D), lambda i: (0, 0))],
        out_shape=[jax.ShapeDtypeStruct((t, D), F32), jax.ShapeDtypeStruct((1, 128), F32),
                   jax.ShapeDtypeStruct((1, D), F32)],
        compiler_params=_params(("arbitrary",)),
    )(h, gain, target)


def _adamw(w, g, m, v):
    m = ADAM_B1 * m + (1.0 - ADAM_B1) * g
    v = ADAM_B2 * v + (1.0 - ADAM_B2) * (g * g)
    m_hat = m / (1.0 - ADAM_B1 ** ADAM_STEP)
    v_hat = v / (1.0 - ADAM_B2 ** ADAM_STEP)
    delta = -ADAM_LR * (m_hat / (jnp.sqrt(v_hat) + ADAM_EPS) + ADAM_WD * w)
    return delta, m, v


def _adamw_big(recv, w, m, v, tr, name, after, transposed=False):
    depth, rows, cols = w.shape
    rspec = (pl.BlockSpec((NDEV, cols, tr), lambda i: (0, 0, i)) if transposed
             else pl.BlockSpec((NDEV, tr, cols), lambda i: (0, i, 0)))

    def body(*refs):
        r_refs = refs[:depth]
        w_ref, m_ref, v_ref, _, g_out, d_out, m_out, v_out = refs[depth:]
        for l in range(depth):
            g = r_refs[l][0].astype(F32)
            for k in range(1, NDEV):
                g = g + r_refs[l][k].astype(F32)
            if transposed:
                g = g.T
            delta, m_, v_ = _adamw(w_ref[l], g, m_ref[l], v_ref[l])
            g_out[l] = g
            d_out[l] = delta
            m_out[l] = m_
            v_out[l] = v_

    spec = pl.BlockSpec((depth, tr, cols), lambda i: (0, i, 0))
    return pl.pallas_call(
        body, name=name, grid=(rows // tr,),
        in_specs=[rspec] * depth + [spec] * 3
        + [pl.BlockSpec(memory_space=pl.ANY)],
        out_specs=[spec] * 4, out_shape=[jax.ShapeDtypeStruct((depth, rows, cols), F32)] * 4,
        compiler_params=_params(("parallel",)),
    )(*recv, w, m, v, after)


def _sum_devices(recv):
    _, r, _ = recv.shape

    def body(r_ref, out_ref):
        g = r_ref[0]
        for k in range(1, NDEV):
            g = g + r_ref[k]
        out_ref[...] = g

    return pl.pallas_call(body, name="sum_devices", out_shape=jax.ShapeDtypeStruct((r, 128), F32))(recv)


def _adamw_small(w, g, m, v):
    def body(w_ref, g_ref, m_ref, v_ref, d_out, m_out, v_out):
        delta, m_, v_ = _adamw(w_ref[...], g_ref[...], m_ref[...], v_ref[...])
        d_out[...] = delta
        m_out[...] = m_
        v_out[...] = v_

    return pl.pallas_call(body, name="adamw_small", out_shape=[jax.ShapeDtypeStruct(w.shape, F32)] * 3)(w, g, m, v)


def _pack(arrs):
    flat = jnp.concatenate([a.reshape(-1) for a in arrs])
    pad = (-flat.shape[0]) % 1024
    return jnp.pad(flat, (0, pad)).reshape(-1, 128)


def _unpack(buf, like):
    flat = buf.reshape(-1)
    out, off = [], 0
    for a in like:
        out.append(flat[off:off + a.size].reshape(a.shape))
        off += a.size
    return out


def _block_diag(w):
    nb, bd, _ = w.shape
    eye = jnp.eye(nb, dtype=w.dtype)
    return (eye[:, None, :, None] * w[:, :, None, :]).reshape(nb * bd, nb * bd)


def _diag_blocks(w):
    nb = w.shape[0] // GRP
    return jnp.stack([w[g * GRP:(g + 1) * GRP, g * GRP:(g + 1) * GRP] for g in range(nb)])


SMALL = ['ffn1_norm', 'mix_norm', 'hgrn_lb_logits', 'hgrn_norm', 'conv_b', 'lru_wa', 'lru_ba', 'lru_wx', 'lru_bx',
         'lru_lambda', 'lru_norm', 'sgu_w', 'sgu_b', 'sgu_norm', 'ffn2_norm', 'final_norm']
NAMES = ['ffn1_norm', 'ffn1_wg', 'ffn1_wu', 'ffn1_wd', 'mix_norm', 'w_in', 'hgrn_lb_logits', 'hgrn_norm', 'conv_w',
         'conv_b', 'lru_wa', 'lru_ba', 'lru_wx', 'lru_bx', 'lru_lambda', 'lru_norm', 'sgu_w', 'sgu_b', 'sgu_norm',
         'w_out', 'ffn2_norm', 'ffn2_wg', 'ffn2_wu', 'ffn2_wd', 'final_norm']


def _step(x, target, w, m, v):
    depth = w['ffn1_wg'].shape[0]
    t = x.shape[1]
    h = x.reshape(t, D)
    target = target.reshape(t, D)
    tm_f, tm_b, tb = min(TM_F, t), min(TM_B, t), min(TB, t)
    my = 4 * lax.axis_index("x") + 2 * lax.axis_index("y") + lax.axis_index("c")

    cw_tile = jnp.pad(w['conv_w'].reshape(-1, 128), ((0, 8 - depth), (0, 0)))
    lbs, lb_soft = _lower_bounds(w['hgrn_lb_logits'])

    def row(a):
        return a.reshape(1, -1)

    none = jnp.zeros((8, 128), F32)

    def tr(a):
        return jnp.swapaxes(a, -1, -2)

    def shards(l, unit):
        if unit == 1:
            return [tr(w['w_in'][l]).astype(WIRE), w['w_out'][l].astype(WIRE)]
        f = 'ffn1' if unit == 0 else 'ffn2'
        return [tr(w[f + '_wg'][l]).astype(WIRE), tr(w[f + '_wu'][l]).astype(WIRE), w[f + '_wd'][l].astype(WIRE)]

    units = [(l, u) for l in range(depth) for u in range(3)]

    def start_ici(idx, deps=()):
        return _transfer_start(shards(*units[idx]), True, "gather_ici_%d_%d" % units[idx], deps=deps)

    def relay(idx, handle, after, then=None):
        lands = _transfer_wait(handle, after, "gather_ici_wait_%d_%d" % units[idx])
        more = shards(*units[then]) if then is not None and then < len(units) else ()
        return _forward_start(lands, "gather_d2d_%d_%d" % units[idx], more)

    pipe = dict(idx=0)
    first = start_ici(0)
    pipe['ici'] = start_ici(1, deps=(first['token'],))
    conv_flight = _transfer_start([cw_tile], True, "gather_conv_start", deps=(pipe['ici']['token'],), direct=True)
    pipe['d2d'], _ = relay(0, first, conv_flight['token'])

    def next_weights(after):
        idx = pipe['idx']
        lands = _transfer_wait(pipe['d2d'], after, "gather_d2d_wait_%d_%d" % units[idx])
        pipe['idx'] = idx + 1
        tok = none
        if idx + 1 < len(units):
            pipe['d2d'], pipe['ici'] = relay(idx + 1, pipe['ici'], lands[-1], idx + 2)
            tok = pipe['d2d']['token']
        return lands, tok

    saved = []
    for l in range(depth):
        lands, tok = next_weights(h)
        s = dict(ffn1=[a.reshape(FF, D) for a in lands], h0=h)
        h, s['xn1'], s['a1'], s['b1'] = _ffn_fwd(h, row(w['ffn1_norm'][l]), *s['ffn1'], tm_f, tok)
        s['h1'] = h
        (win, wout), tok = next_weights(h)
        win, wout = win.reshape(DIN, D), wout.reshape(D, D)
        s['win'], s['wout'] = win, wout
        z, s['xnm'] = _inproj_fwd(h, row(w['mix_norm'][l]), win, tm_f, tok)
        s['z'] = z
        s['o'], oa, s['states'] = _hgrn_fwd(z, row(lbs[l]), row(w['hgrn_norm'][l]))
        if l == 0:
            cw_all = _transfer_wait(conv_flight, z, "gather_conv_wait")[0][:, :depth]
            conv_w = jnp.moveaxis(cw_all.reshape(NDEV, depth, 4, DB // NDEV), 0, 2).reshape(depth, 4, DB)
        s['cw'] = jnp.pad(conv_w[l], ((0, 4), (0, 0)))
        s['vec'] = jnp.concatenate([row(w['conv_b'][l]), row(w['lru_ba'][l]), row(w['lru_bx'][l]),
                                    row(w['lru_lambda'][l]), row(w['lru_norm'][l]), jnp.zeros((3, DB), F32)])
        s['wa'], s['wx'] = _block_diag(w['lru_wa'][l]), _block_diag(w['lru_wx'][l])
        ob, s['hseq'] = _lru_fwd(z, s['cw'], s['vec'], s['wa'], s['wx'], tb)
        s['bias'] = jnp.repeat(w['sgu_b'][l].T, GRP, axis=1)
        oc = _sgu_fwd(z, w['sgu_w'][l], s['bias'], row(w['sgu_norm'][l]), tb)
        s['oa'], s['ob'], s['oc'] = oa, ob, oc
        h = _outproj_fwd(h, oa, ob, oc, wout, tm_f)
        s['h2'] = h
        lands, tok = next_weights(h)
        s['ffn2'] = [a.reshape(FF, D) for a in lands]
        h, s['xn2'], s['a2'], s['b2'] = _ffn_fwd(h, row(w['ffn2_norm'][l]), *s['ffn2'], tm_f, tok)
        saved.append(s)

    dh, loss_part, g_final = _head(h, row(w['final_norm']), target, tm_f)
    loss = lax.psum(loss_part[0, 0], ("x", "y", "c"))

    recv = {k: [None] * depth for k in ('wg1', 'wu1', 'wd1', 'wg2', 'wu2', 'wd2', 'win', 'wout')}
    flight = []

    def land(after):
        handle, kinds, l = flight.pop()
        for k, a in zip(kinds, _transfer_wait(handle, after, f"exchange_wait_{kinds[0]}_{l}")):
            recv[k][l] = a

    def exchange(arrs, kinds, l, deps=()):
        handle = _transfer_start(arrs, False, f"exchange_start_{kinds[0]}_{l}", deps=deps)
        if flight:
            land(handle['token'])
        flight.append((handle, kinds, l))
        return handle['token']

    small = {k: [None] * depth for k in SMALL if k != 'final_norm'}
    dconv = [None] * depth
    dlb = [None] * depth
    tok = none
    for l in reversed(range(depth)):
        s = saved[l]
        dh, g, *cot = _ffn_bwd_x(dh, s['h2'], row(w['ffn2_norm'][l]), s['a2'], s['b2'], *s['ffn2'], tm_b, tok)
        dws = _ffn_bwd_w(s['xn2'], *cot, tm_b)
        tok = exchange([a.reshape(NDEV, FFS, D) for a in dws], ('wg2', 'wu2', 'wd2'), l)
        small['ffn2_norm'][l] = g
        doa, dob, doc, dwout = _outproj_bwd(dh, s['oa'], s['ob'], s['oc'], s['wout'], tm_f)
        dz, g_hn, dlb[l] = _hgrn_bwd(s['z'], row(lbs[l]), row(w['hgrn_norm'][l]), s['o'], s['states'], doa, tok)
        small['hgrn_norm'][l] = g_hn
        dz, dcw, dvec, dwa, dwx = _lru_bwd(s['z'], s['hseq'], dob, s['cw'], s['vec'], s['wa'], s['wx'], dz, tb)
        dconv[l] = dcw[:4]
        small['conv_b'][l], small['lru_ba'][l], small['lru_bx'][l] = dvec[0], dvec[1].reshape(4, GRP), dvec[2].reshape(4, GRP)
        small['lru_lambda'][l], small['lru_norm'][l] = dvec[3], dvec[4]
        small['lru_wa'][l], small['lru_wx'][l] = _diag_blocks(dwa), _diag_blocks(dwx)
        dz, dsw, dbias, dgc = _sgu_bwd(s['z'], doc, w['sgu_w'][l], s['bias'], row(w['sgu_norm'][l]), dz, tb)
        small['sgu_w'][l], small['sgu_b'][l], small['sgu_norm'][l] = dsw, dbias[:, ::GRP].T, dgc[0]
        dwin = _inproj_bwd_w(s['xnm'], dz, tm_f)
        tok = exchange([dwin.reshape(NDEV, DINS, D), dwout.reshape(NDEV, D // NDEV, D)], ('win', 'wout'), l)
        dh, g = _inproj_bwd_x(dh, dz, s['h1'], row(w['mix_norm'][l]), s['win'], tm_f, tok)
        small['mix_norm'][l] = g
        tok = none
        if l == 0:
            small['ffn1_norm'][0] = jnp.zeros((1, D), F32)
            small['hgrn_lb_logits'] = list(_lower_bounds_bwd(lb_soft, jnp.concatenate(dlb, axis=0)))
            parts = [jnp.stack([small[k][j].reshape(w[k].shape[1:]) for j in range(depth)])
                     for k in SMALL if k != 'final_norm']
            parts += [g_final.reshape(D), jnp.stack(dconv)]
            small_flight = _transfer_start([_pack(parts)], True, "gather_small_start", direct=True)
            tok = small_flight['token']
        dh, g, *cot = _ffn_bwd_x(dh, s['h0'], row(w['ffn1_norm'][l]), s['a1'], s['b1'], *s['ffn1'], tm_b, tok)
        dws = _ffn_bwd_w(s['xn1'], *cot, tm_b)
        before = ()
        if l == 0:
            g_last = _all_gather([g.reshape(8, 128)], "gather_last")[0]
            before = (g_last,)
        else:
            small['ffn1_norm'][l] = g
        tok = exchange([a.reshape(NDEV, FFS, D) for a in dws], ('wg1', 'wu1', 'wd1'), l, before)
    grad_x = dh.reshape(1, t, D)

    out = {}
    last = flight[0][0]['token']

    def ffn_update(f, n, after):
        for kind in ('wg', 'wu'):
            k = f + '_' + kind
            res = _adamw_big(recv[kind + n], tr(w[k]), tr(m[k]), tr(v[k]), 32, "adamw_ffn", after)
            out[k] = tuple(tr(a) for a in res)
        k = f + '_wd'
        out[k] = _adamw_big(recv['wd' + n], w[k], m[k], v[k], 32, "adamw_ffn", after)

    ffn_update('ffn2', '2', last)
    out['w_in'] = _adamw_big(recv['win'], w['w_in'], m['w_in'], v['w_in'], 128, "adamw_win", last, transposed=True)
    out['w_out'] = _adamw_big(recv['wout'], w['w_out'], m['w_out'], v['w_out'], 64, "adamw_wout", last)

    total = _sum_devices(_transfer_wait(small_flight, g_last, "gather_small_wait")[0])
    like = [w[k] for k in SMALL] + [jax.ShapeDtypeStruct((depth, 4, DB), F32)]
    grads = _unpack(total, like)
    gsmall = dict(zip(SMALL, grads[:-1]))
    gsmall['ffn1_norm'] = gsmall['ffn1_norm'].at[0].set(_sum_devices(g_last).reshape(D))
    gsmall['conv_w'] = lax.dynamic_slice_in_dim(grads[-1], my * (DB // NDEV), DB // NDEV, axis=2)
    keys = SMALL + ['conv_w']
    dl, mm, vv = _adamw_small(_pack([w[k] for k in keys]), _pack([gsmall[k] for k in keys]),
                              _pack([m[k] for k in keys]), _pack([v[k] for k in keys]))
    like = [w[k] for k in keys]
    for k, d_, m_, v_ in zip(keys, _unpack(dl, like), _unpack(mm, like), _unpack(vv, like)):
        out[k] = (gsmall[k], d_, m_, v_)
    done = [dl] + [out[k][1][0] for k in ('ffn2_wg', 'ffn2_wu', 'ffn2_wd', 'w_in', 'w_out')]
    land(functools.reduce(lambda p, q: p + q, [a[:1, :1] for a in done]))
    ffn_update('ffn1', '1', last)

    return (loss, grad_x, *[out[k][0] for k in NAMES], *[out[k][1] for k in NAMES],
            *[out[k][2] for k in NAMES], *[out[k][3] for k in NAMES])


def kernel(x, ffn1_norm, ffn1_wg, ffn1_wu, ffn1_wd, mix_norm, w_in, hgrn_lb_logits, hgrn_norm, conv_w, conv_b, lru_wa, lru_ba, lru_wx, lru_bx, lru_lambda, lru_norm, sgu_w, sgu_b, sgu_norm, w_out, ffn2_norm, ffn2_wg, ffn2_wu, ffn2_wd, final_norm, loss_target, m_ffn1_norm, m_ffn1_wg, m_ffn1_wu, m_ffn1_wd, m_mix_norm, m_w_in, m_hgrn_lb_logits, m_hgrn_norm, m_conv_w, m_conv_b, m_lru_wa, m_lru_ba, m_lru_wx, m_lru_bx, m_lru_lambda, m_lru_norm, m_sgu_w, m_sgu_b, m_sgu_norm, m_w_out, m_ffn2_norm, m_ffn2_wg, m_ffn2_wu, m_ffn2_wd, m_final_norm, v_ffn1_norm, v_ffn1_wg, v_ffn1_wu, v_ffn1_wd, v_mix_norm, v_w_in, v_hgrn_lb_logits, v_hgrn_norm, v_conv_w, v_conv_b, v_lru_wa, v_lru_ba, v_lru_wx, v_lru_bx, v_lru_lambda, v_lru_norm, v_sgu_w, v_sgu_b, v_sgu_norm, v_w_out, v_ffn2_norm, v_ffn2_wg, v_ffn2_wu, v_ffn2_wd, v_final_norm):
    args = locals()
    w = {k: args[k] for k in NAMES}
    m = {k: args['m_' + k] for k in NAMES}
    v = {k: args['v_' + k] for k in NAMES}
    return _step(x, loss_target, w, m, v)
```

```python
import functools

import jax
import jax.numpy as jnp
from jax import lax
from jax.experimental import pallas as pl
from jax.experimental.pallas import tpu as pltpu

F32 = jnp.float32
MXU = jnp.bfloat16
SAVE = jnp.bfloat16
WIRE = jnp.bfloat16

NDEV = 8
D = 1024
FF = 2816
FFS = FF // NDEV
FB = 256
FBX = FF // 2
DIN = 3072
DINS = DIN // NDEV
ZB = 512
ZBW = 1024
DA, DB, DC = 512, 256, 256
HD = 128
NH = DA // HD
ACH = 64
ACB = 4
CCH = 128
GRP = 64
EPS = 1e-6
LRU_C = 8.0
VMEM_LIMIT = 60 * 1024 * 1024
TM_F = 1024
TM_B = 512
TB = 1024
SUB = 512
SUB_X = 256

ADAM_LR, ADAM_B1, ADAM_B2, ADAM_EPS, ADAM_WD, ADAM_STEP = 0.001, 0.9, 0.999, 1e-08, 0.01, 10

MESH = pl.DeviceIdType.MESH


def _mm(a, b):
    return jnp.dot(a.astype(MXU), b.astype(MXU), preferred_element_type=F32)


def _mm_nt(a, b):
    return lax.dot_general(a.astype(MXU), b.astype(MXU), (((1,), (1,)), ((), ())), preferred_element_type=F32)


def _mm_tn(a, b):
    return lax.dot_general(a.astype(MXU), b.astype(MXU), (((0,), (0,)), ((), ())), preferred_element_type=F32)


def _split3(x):
    x1 = x.astype(MXU)
    r1 = x - x1.astype(F32)
    x2 = r1.astype(MXU)
    r2 = r1 - x2.astype(F32)
    return x1, x2, r2.astype(MXU)


def _mm_exact_l(c, x):
    x1, x2, x3 = _split3(x)
    return _mm(c, x1) + _mm(c, x2) + _mm(c, x3)


def _mm_exact_r(x, c):
    x1, x2, x3 = _split3(x)
    return _mm(x1, c) + _mm(x2, c) + _mm(x3, c)


def _sigmoid(x):
    return 1.0 / (1.0 + jnp.exp(-x))


def _gelu(x):
    c, k = 0.7978845608028654, 0.044715
    th = jnp.tanh(c * (x + k * x * x * x))
    return 0.5 * x * (1.0 + th)


def _gelu_and_grad(x):
    c, k = 0.7978845608028654, 0.044715
    th = jnp.tanh(c * (x + k * x * x * x))
    g = 0.5 * x * (1.0 + th)
    dg = 0.5 * (1.0 + th) + 0.5 * x * (1.0 - th * th) * c * (1.0 + 3.0 * k * x * x)
    return g, dg


def _expm1(x):
    series = x * (1.0 + x * (0.5 + x * (1.0 / 6.0 + x * (1.0 / 24.0 + x * (1.0 / 120.0)))))
    return jnp.where(jnp.abs(x) < 0.05, series, jnp.exp(x) - 1.0)


def _iota(shape, dim):
    return lax.broadcasted_iota(jnp.int32, shape, dim)


def _group_matrix(n, value):
    r, c = _iota((n, n), 0), _iota((n, n), 1)
    return jnp.where((r // GRP) == (c // GRP), value, 0.0).astype(F32)


def _row(x, k):
    r = _iota(x.shape, 0)
    return jnp.sum(jnp.where(r == k, x, 0.0), axis=0, keepdims=True)


def _rms_bwd(dxn, hh, gain):
    rstd = lax.rsqrt(jnp.mean(hh * hh, axis=-1, keepdims=True) + EPS)
    xhat = hh * rstd
    dxh = dxn * gain
    dh = rstd * (dxh - xhat * jnp.mean(dxh * xhat, axis=-1, keepdims=True))
    return dh, jnp.sum(dxn * xhat, axis=0, keepdims=True)


def _params(sem):
    return pltpu.CompilerParams(dimension_semantics=sem, vmem_limit_bytes=VMEM_LIMIT)


def _all_gather(arrs, name):
    n = len(arrs)

    def body(*refs):
        ins, outs = refs[:n], refs[n:2 * n]
        send_sems, recv_sems, local_sems = refs[2 * n:]
        x, y, c = lax.axis_index("x"), lax.axis_index("y"), lax.axis_index("c")
        me, sibling = (x, y, c), (x, y, 1 - c)
        chips = [(1 - x, y), (x, 1 - y), (1 - x, 1 - y)]

        def slot(px, py, pc):
            return 4 * px + 2 * py + pc

        def copy(a, k, block, to, src=None):
            dst = outs[a].at[slot(*block)]
            return pltpu.make_async_remote_copy(
                src_ref=dst if src is None else src, dst_ref=dst,
                send_sem=send_sems.at[a * 7 + k], recv_sem=recv_sems.at[a * 7 + k],
                device_id=to, device_id_type=MESH)

        started = []
        for a in range(n):
            mine = pltpu.make_async_copy(ins[a], outs[a].at[slot(*me)], local_sems.at[a])
            mine.start()
            started.append(mine)
        first = []
        for a in range(n):
            first.append(copy(a, 0, me, sibling, src=ins[a]))
            first += [copy(a, 1 + j, me, (*chip, c), src=ins[a]) for j, chip in enumerate(chips)]
        for cp in first:
            cp.start()
        passed = []
        for a in range(n):
            for j, chip in enumerate(chips):
                copy(a, 1 + j, (*chip, c), me).wait_recv()
                fwd = copy(a, 4 + j, (*chip, c), sibling)
                fwd.start()
                passed.append(fwd)
        for a in range(n):
            copy(a, 0, sibling, me).wait_recv()
            for j, chip in enumerate(chips):
                copy(a, 4 + j, (*chip, 1 - c), me).wait_recv()
        for cp in first + passed:
            cp.wait_send()
        for mine in started:
            mine.wait()

    hbm = pl.BlockSpec(memory_space=pl.ANY)
    return pl.pallas_call(
        body, name=name,
        out_shape=[jax.ShapeDtypeStruct((NDEV,) + a.shape, a.dtype) for a in arrs],
        in_specs=[hbm] * n, out_specs=[hbm] * n,
        scratch_shapes=[pltpu.SemaphoreType.DMA((7 * n,)), pltpu.SemaphoreType.DMA((7 * n,)),
                        pltpu.SemaphoreType.DMA((n,))],
    )(*arrs)


def _peers():
    x, y, c = lax.axis_index("x"), lax.axis_index("y"), lax.axis_index("c")
    peers = [(x ^ ((k >> 2) & 1), y ^ ((k >> 1) & 1), c ^ (k & 1)) for k in range(1, NDEV)]
    return (x, y, c), 4 * x + 2 * y + c, peers


_HBM = pl.BlockSpec(memory_space=pltpu.HBM)
_SEM = pl.BlockSpec(memory_space=pltpu.SEMAPHORE)
_ANY = pl.BlockSpec(memory_space=pl.ANY)
_EFFECT = pltpu.SideEffectType.DATAFLOW_SIDE_EFFECTING


def _transfer_start(arrs, gather, name, deps=(), direct=False):
    n, nd = len(arrs), len(deps)
    shapes = [((NDEV,) + a.shape) if gather else a.shape for a in arrs]

    def body(*refs):
        ins, lands = refs[:n], refs[n:2 * n]
        send_sems, recv_sems, local_sems = refs[2 * n + nd:2 * n + nd + 3]
        token = refs[-1]
        (x, y, c), my, peers = _peers()
        if gather and not direct:
            peers = [(x, y, 1 - c), (1 - x, y, c), (x, 1 - y, c), (1 - x, 1 - y, c)]
        for a in range(n):
            own = ins[a] if gather else ins[a].at[my]
            pltpu.make_async_copy(own, lands[a].at[my], local_sems.at[a]).start()
        for a in range(n):
            for peer in peers:
                src = ins[a] if gather else ins[a].at[4 * peer[0] + 2 * peer[1] + peer[2]]
                pltpu.make_async_remote_copy(
                    src_ref=src, dst_ref=lands[a].at[my], send_sem=send_sems.at[a], recv_sem=recv_sems.at[a],
                    device_id=peer, device_id_type=MESH).start()
        token[...] = jnp.zeros_like(token)

    out_shape = [pltpu.SemaphoreType.DMA((n,))] * 3
    out_shape += [pltpu.HBM(a.shape, a.dtype) for a in arrs]
    out_shape += [pltpu.HBM(s, a.dtype) for s, a in zip(shapes, arrs)]
    out_shape += [jax.ShapeDtypeStruct((8, 128), F32)]
    operands = [pltpu.with_memory_space_constraint(a, pltpu.HBM) for a in arrs]
    operands += [pltpu.with_memory_space_constraint(lax.empty(s, a.dtype), pltpu.HBM) for s, a in zip(shapes, arrs)]
    res = pl.pallas_call(
        body, name=name, out_shape=out_shape,
        in_specs=[_HBM] * (2 * n) + [pl.BlockSpec(memory_space=pl.ANY)] * nd,
        out_specs=[_SEM] * 3 + [_HBM] * (2 * n) + [pl.BlockSpec(memory_space=pltpu.VMEM)],
        input_output_aliases={i: 3 + i for i in range(2 * n)},
        compiler_params=pltpu.CompilerParams(has_side_effects=_EFFECT),
    )(*operands, *deps)
    return dict(sems=res[:3], src=res[3:3 + n], lands=res[3 + n:3 + 2 * n], token=res[-1], n=n,
                count=4 if gather and not direct else NDEV - 1)


def _forward_start(lands, name, shards=()):
    n, m = len(lands), len(shards)
    zones = [(NDEV,) + a.shape for a in shards]

    def body(*refs):
        zone, ins, fresh = refs[:n], refs[n:n + m], refs[n + m:n + 2 * m]
        sems = refs[n + 2 * m:n + 2 * m + (5 if m else 2)]
        token = refs[-1]
        (x, y, c), my, _ = _peers()
        for a in range(n):
            for px, py in ((1 - x, y), (x, 1 - y), (1 - x, 1 - y)):
                block = zone[a].at[4 * px + 2 * py + c]
                pltpu.make_async_remote_copy(
                    src_ref=block, dst_ref=block, send_sem=sems[0].at[a], recv_sem=sems[1].at[a],
                    device_id=(x, y, 1 - c), device_id_type=MESH).start()
        for a in range(m):
            pltpu.make_async_copy(ins[a], fresh[a].at[my], sems[4].at[a]).start()
            for peer in ((x, y, 1 - c), (1 - x, y, c), (x, 1 - y, c), (1 - x, 1 - y, c)):
                pltpu.make_async_remote_copy(
                    src_ref=ins[a], dst_ref=fresh[a].at[my], send_sem=sems[2].at[a], recv_sem=sems[3].at[a],
                    device_id=peer, device_id_type=MESH).start()
        token[...] = jnp.zeros_like(token)

    sem_shapes = [pltpu.SemaphoreType.DMA((n,))] * 2 + ([pltpu.SemaphoreType.DMA((m,))] * 3 if m else [])
    ns = len(sem_shapes)
    thru = list(lands) + list(shards)
    operands = thru + [pltpu.with_memory_space_constraint(lax.empty(s, a.dtype), pltpu.HBM)
                       for s, a in zip(zones, shards)]
    res = pl.pallas_call(
        body, name=name,
        out_shape=sem_shapes + [pltpu.HBM(a.shape, a.dtype) for a in thru]
        + [pltpu.HBM(s, a.dtype) for s, a in zip(zones, shards)] + [jax.ShapeDtypeStruct((8, 128), F32)],
        in_specs=[_HBM] * (n + 2 * m),
        out_specs=[_SEM] * ns + [_HBM] * (n + 2 * m) + [pl.BlockSpec(memory_space=pltpu.VMEM)],
        input_output_aliases={i: ns + i for i in range(n + 2 * m)},
        compiler_params=pltpu.CompilerParams(has_side_effects=_EFFECT),
    )(*[pltpu.with_memory_space_constraint(a, pltpu.HBM) for a in thru], *operands[n + m:])
    forward = dict(sems=res[:2], src=[], lands=res[ns:ns + n], token=res[-1], n=n, count=3)
    nxt = None
    if m:
        nxt = dict(sems=res[2:5], src=res[ns + n:ns + n + m], lands=res[ns + n + m:ns + n + 2 * m],
                   token=res[-1], n=m, count=4)
    return forward, nxt


def _transfer_wait(handle, after, name):
    n, count = handle["n"], handle["count"]
    src, lands, sems = list(handle["src"]), list(handle["lands"]), list(handle["sems"])
    ns = len(src)

    def body(*refs):
        zone = refs[ns:ns + n]
        sem_refs = refs[ns + n:ns + n + len(sems)]
        me, _, _ = _peers()
        for a in range(n):
            moved = zone[a].at[pl.ds(0, count)]
            both = pltpu.make_async_remote_copy(
                src_ref=moved, dst_ref=moved, send_sem=sem_refs[0].at[a], recv_sem=sem_refs[1].at[a],
                device_id=me, device_id_type=MESH)
            both.wait_send()
            both.wait_recv()
            if len(sems) == 3:
                pltpu.make_async_copy(zone[a].at[0], zone[a].at[1], sem_refs[2].at[a]).wait()

    res = pl.pallas_call(
        body, name=name,
        out_shape=[pltpu.HBM(a.shape, a.dtype) for a in src + lands],
        in_specs=[_HBM] * (ns + n) + [_SEM] * len(sems) + [pl.BlockSpec(memory_space=pl.ANY)],
        out_specs=[_HBM] * (ns + n),
        input_output_aliases={i: i for i in range(ns + n)},
        compiler_params=pltpu.CompilerParams(has_side_effects=_EFFECT),
    )(*src, *lands, *sems, after)
    return list(res[ns:])


def _ffn_fwd(h, gain, wg, wu, wd, tm, after):
    t = h.shape[0]
    nj = FF // FBX

    def body(h_ref, g_ref, wg_ref, wu_ref, wd_ref, _, out_ref, xn_ref, a_ref, b_ref, acc_ref):
        j = pl.program_id(1)

        @pl.when(j == 0)
        def _():
            hh = h_ref[...]
            rstd = lax.rsqrt(jnp.mean(hh * hh, axis=-1, keepdims=True) + EPS)
            xn_ref[...] = (hh * rstd * g_ref[...]).astype(xn_ref.dtype)
            acc_ref[...] = jnp.zeros_like(acc_ref)

        sub = min(SUB, tm)
        for r in range(tm // sub):
            rows = slice(r * sub, (r + 1) * sub)
            xn = xn_ref[rows, :]
            y = None
            for c0 in range(0, FBX, FB):
                cols = slice(c0, min(c0 + FB, FBX))
                a = _mm_nt(xn, wg_ref[cols, :])
                b = _mm_nt(xn, wu_ref[cols, :])
                a_ref[rows, cols] = a.astype(a_ref.dtype)
                b_ref[rows, cols] = b.astype(b_ref.dtype)
                part = _mm(a * _sigmoid(a) * b, wd_ref[cols, :])
                y = part if y is None else y + part
            acc_ref[rows, :] += y

        @pl.when(j == nj - 1)
        def _():
            out_ref[...] = h_ref[...] + 0.5 * acc_ref[...]

    wspec = pl.BlockSpec((FBX, D), lambda i, j: (j, 0))
    return pl.pallas_call(
        body, name="ffn_fwd", grid=(t // tm, nj),
        in_specs=[pl.BlockSpec((tm, D), lambda i, j: (i, 0)),
                  pl.BlockSpec((1, D), lambda i, j: (0, 0)), wspec, wspec, wspec, _ANY],
        out_specs=[pl.BlockSpec((tm, D), lambda i, j: (i, 0)),
                   pl.BlockSpec((tm, D), lambda i, j: (i, 0)),
                   pl.BlockSpec((tm, FBX), lambda i, j: (i, j)),
                   pl.BlockSpec((tm, FBX), lambda i, j: (i, j))],
        out_shape=[jax.ShapeDtypeStruct((t, D), F32), jax.ShapeDtypeStruct((t, D), SAVE),
                   jax.ShapeDtypeStruct((t, FF), SAVE), jax.ShapeDtypeStruct((t, FF), SAVE)],
        scratch_shapes=[pltpu.VMEM((tm, D), F32)],
        compiler_params=_params(("parallel", "arbitrary")),
    )(h, gain, wg, wu, wd, after)


def _ffn_bwd_x(dout, h, gain, a_sv, b_sv, wg, wu, wd, tm, after):
    t = h.shape[0]
    nj = FF // FBX

    def body(dout_ref, h_ref, g_ref, a_ref, b_ref, wg_ref, wu_ref, wd_ref, _,
             dh_ref, dgain_ref, dy_ref, da_ref, db_ref, s_ref, acc_ref):
        i, j = pl.program_id(0), pl.program_id(1)

        @pl.when((i == 0) & (j == 0))
        def _():
            dgain_ref[...] = jnp.zeros_like(dgain_ref)

        @pl.when(j == 0)
        def _():
            dy_ref[...] = (0.5 * dout_ref[...]).astype(dy_ref.dtype)
            acc_ref[...] = jnp.zeros_like(acc_ref)

        sub = min(SUB_X, tm)
        for r in range(tm // sub):
            rows = slice(r * sub, (r + 1) * sub)
            dy = dy_ref[rows, :]
            dx = None
            for c0 in range(0, FBX, FB):
                cols = slice(c0, min(c0 + FB, FBX))
                ds = _mm_nt(dy, wd_ref[cols, :])
                a, b = a_ref[rows, cols].astype(F32), b_ref[rows, cols].astype(F32)
                sg = _sigmoid(a)
                sa = a * sg
                da = (ds * b * (sg * (1.0 + a * (1.0 - sg)))).astype(MXU)
                db = (ds * sa).astype(MXU)
                da_ref[rows, cols] = da.astype(da_ref.dtype)
                db_ref[rows, cols] = db.astype(db_ref.dtype)
                s_ref[rows, cols] = (sa * b).astype(s_ref.dtype)
                part = _mm(da, wg_ref[cols, :]) + _mm(db, wu_ref[cols, :])
                dx = part if dx is None else dx + part
            acc_ref[rows, :] += dx

        @pl.when(j == nj - 1)
        def _():
            dh, dg = _rms_bwd(acc_ref[...], h_ref[...], g_ref[...])
            dh_ref[...] = dout_ref[...] + dh
            dgain_ref[...] += dg

    tok = pl.BlockSpec((tm, D), lambda i, j: (i, 0))
    act = pl.BlockSpec((tm, FBX), lambda i, j: (i, j))
    wspec = pl.BlockSpec((FBX, D), lambda i, j: (j, 0))
    return pl.pallas_call(
        body, name="ffn_bwd_x", grid=(t // tm, nj),
        in_specs=[tok, tok, pl.BlockSpec((1, D), lambda i, j: (0, 0)), act, act, wspec, wspec, wspec, _ANY],
        out_specs=[tok, pl.BlockSpec((1, D), lambda i, j: (0, 0)), tok, act, act, act],
        out_shape=[jax.ShapeDtypeStruct((t, D), F32), jax.ShapeDtypeStruct((1, D), F32),
                   jax.ShapeDtypeStruct((t, D), SAVE)] + [jax.ShapeDtypeStruct((t, FF), SAVE)] * 3,
        scratch_shapes=[pltpu.VMEM((tm, D), F32)],
        compiler_params=_params(("arbitrary", "arbitrary")),
    )(dout, h, gain, a_sv, b_sv, wg, wu, wd, after)


def _ffn_bwd_w(xn, dy, da, db, s, tm):
    t = xn.shape[0]
    nt = t // tm
    nj = FF // FBX

    def body(xn_ref, dy_ref, da_ref, db_ref, s_ref, dwg_ref, dwu_ref, dwd_ref, ag_scr, au_scr, ad_scr):
        i = pl.program_id(1)

        @pl.when(i == 0)
        def _():
            for ref in (ag_scr, au_scr, ad_scr):
                ref[...] = jnp.zeros_like(ref)

        xn, dy = xn_ref[...], dy_ref[...]
        for c0 in range(0, FBX, FB):
            rows = slice(c0, min(c0 + FB, FBX))
            ag_scr[rows, :] += _mm_tn(da_ref[:, rows], xn)
            au_scr[rows, :] += _mm_tn(db_ref[:, rows], xn)
            ad_scr[rows, :] += _mm_tn(s_ref[:, rows], dy)

        @pl.when(i == nt - 1)
        def _():
            for out, ref in ((dwg_ref, ag_scr), (dwu_ref, au_scr), (dwd_ref, ad_scr)):
                out[...] = ref[...].astype(out.dtype)

    tok = pl.BlockSpec((tm, D), lambda j, i: (i, 0))
    act = pl.BlockSpec((tm, FBX), lambda j, i: (i, j))
    wspec = pl.BlockSpec((FBX, D), lambda j, i: (j, 0))
    return pl.pallas_call(
        body, name="ffn_bwd_w", grid=(nj, nt),
        in_specs=[tok, tok, act, act, act], out_specs=[wspec] * 3,
        out_shape=[jax.ShapeDtypeStruct((FF, D), WIRE)] * 3,
        scratch_shapes=[pltpu.VMEM((FBX, D), F32)] * 3,
        compiler_params=_params(("parallel", "arbitrary")),
    )(xn, dy, da, db, s)


def _inproj_fwd(h, gain, win, tm, after):
    t = h.shape[0]

    def body(h_ref, g_ref, w_ref, _, z_ref, xn_ref):
        hh = h_ref[...]
        rstd = lax.rsqrt(jnp.mean(hh * hh, axis=-1, keepdims=True) + EPS)
        xn = (hh * rstd * g_ref[...]).astype(MXU)
        xn_ref[...] = xn.astype(xn_ref.dtype)
        for j in range(DIN // ZB):
            z_ref[:, j * ZB:(j + 1) * ZB] = _mm_nt(xn, w_ref[j * ZB:(j + 1) * ZB, :])

    return pl.pallas_call(
        body, name="inproj_fwd", grid=(t // tm,),
        in_specs=[pl.BlockSpec((tm, D), lambda i: (i, 0)),
                  pl.BlockSpec((1, D), lambda i: (0, 0)),
                  pl.BlockSpec((DIN, D), lambda i: (0, 0)), _ANY],
        out_specs=[pl.BlockSpec((tm, DIN), lambda i: (i, 0)),
                   pl.BlockSpec((tm, D), lambda i: (i, 0))],
        out_shape=[jax.ShapeDtypeStruct((t, DIN), F32), jax.ShapeDtypeStruct((t, D), SAVE)],
        compiler_params=_params(("parallel",)),
    )(h, gain, win, after)


def _inproj_bwd_x(dres, dz, h, gain, win, tm, after):
    t = h.shape[0]

    def body(dres_ref, dz_ref, h_ref, g_ref, w_ref, _, dh_ref, dgain_ref):
        @pl.when(pl.program_id(0) == 0)
        def _():
            dgain_ref[...] = jnp.zeros_like(dgain_ref)

        dh, dg = _rms_bwd(_mm(dz_ref[...], w_ref[...]), h_ref[...], g_ref[...])
        dh_ref[...] = dres_ref[...] + dh
        dgain_ref[...] += dg

    return pl.pallas_call(
        body, name="inproj_bwd_x", grid=(t // tm,),
        in_specs=[pl.BlockSpec((tm, D), lambda i: (i, 0)),
                  pl.BlockSpec((tm, DIN), lambda i: (i, 0)),
                  pl.BlockSpec((tm, D), lambda i: (i, 0)),
                  pl.BlockSpec((1, D), lambda i: (0, 0)),
                  pl.BlockSpec((DIN, D), lambda i: (0, 0)), _ANY],
        out_specs=[pl.BlockSpec((tm, D), lambda i: (i, 0)),
                   pl.BlockSpec((1, D), lambda i: (0, 0))],
        out_shape=[jax.ShapeDtypeStruct((t, D), F32), jax.ShapeDtypeStruct((1, D), F32)],
        compiler_params=_params(("arbitrary",)),
    )(dres, dz, h, gain, win, after)


def _inproj_bwd_w(xn, dz, tm):
    t = xn.shape[0]
    nt = t // tm

    def body(xn_ref, dz_ref, dw_ref, acc_scr):
        i = pl.program_id(1)

        @pl.when(i == 0)
        def _():
            acc_scr[...] = jnp.zeros_like(acc_scr)

        acc_scr[...] += _mm_tn(dz_ref[...], xn_ref[...])

        @pl.when(i == nt - 1)
        def _():
            dw_ref[...] = acc_scr[...].astype(dw_ref.dtype)

    return pl.pallas_call(
        body, name="inproj_bwd_w", grid=(DIN // ZBW, nt),
        in_specs=[pl.BlockSpec((tm, D), lambda j, i: (i, 0)),
                  pl.BlockSpec((tm, ZBW), lambda j, i: (i, j))],
        out_specs=pl.BlockSpec((ZBW, D), lambda j, i: (j, 0)),
        out_shape=jax.ShapeDtypeStruct((DIN, D), WIRE),
        scratch_shapes=[pltpu.VMEM((ZBW, D), F32)],
        compiler_params=_params(("parallel", "arbitrary")),
    )(xn, dz)


def _outproj_fwd(h, oa, ob, oc, wout, tm):
    t = h.shape[0]

    def body(h_ref, oa_ref, ob_ref, oc_ref, w_ref, out_ref):
        ym = jnp.concatenate([oa_ref[...], ob_ref[...], oc_ref[...]], axis=1)
        out_ref[...] = h_ref[...] + _mm(ym, w_ref[...])

    return pl.pallas_call(
        body, name="outproj_fwd", grid=(t // tm,),
        in_specs=[pl.BlockSpec((tm, D), lambda i: (i, 0)),
                  pl.BlockSpec((tm, DA), lambda i: (i, 0)),
                  pl.BlockSpec((tm, DB), lambda i: (i, 0)),
                  pl.BlockSpec((tm, DC), lambda i: (i, 0)),
                  pl.BlockSpec((D, D), lambda i: (0, 0))],
        out_specs=pl.BlockSpec((tm, D), lambda i: (i, 0)),
        out_shape=jax.ShapeDtypeStruct((t, D), F32),
        compiler_params=_params(("parallel",)),
    )(h, oa, ob, oc, wout)


def _outproj_bwd(dh, oa, ob, oc, wout, tm):
    t = dh.shape[0]
    nt = t // tm

    def body(dh_ref, oa_ref, ob_ref, oc_ref, w_ref, da_ref, db_ref, dc_ref, dw_ref, acc_scr):
        i = pl.program_id(0)

        @pl.when(i == 0)
        def _():
            acc_scr[...] = jnp.zeros_like(acc_scr)

        d16 = dh_ref[...].astype(MXU)
        dym = _mm_nt(d16, w_ref[...])
        da_ref[...] = dym[:, :DA]
        db_ref[...] = dym[:, DA:DA + DB]
        dc_ref[...] = dym[:, DA + DB:]
        ym = jnp.concatenate([oa_ref[...], ob_ref[...], oc_ref[...]], axis=1)
        acc_scr[...] += _mm_tn(ym, d16)

        @pl.when(i == nt - 1)
        def _():
            dw_ref[...] = acc_scr[...].astype(dw_ref.dtype)

    return pl.pallas_call(
        body, name="outproj_bwd", grid=(nt,),
        in_specs=[pl.BlockSpec((tm, D), lambda i: (i, 0)),
                  pl.BlockSpec((tm, DA), lambda i: (i, 0)),
                  pl.BlockSpec((tm, DB), lambda i: (i, 0)),
                  pl.BlockSpec((tm, DC), lambda i: (i, 0)),
                  pl.BlockSpec((D, D), lambda i: (0, 0))],
        out_specs=[pl.BlockSpec((tm, DA), lambda i: (i, 0)),
                   pl.BlockSpec((tm, DB), lambda i: (i, 0)),
                   pl.BlockSpec((tm, DC), lambda i: (i, 0)),
                   pl.BlockSpec((D, D), lambda i: (0, 0))],
        out_shape=[jax.ShapeDtypeStruct((t, DA), F32), jax.ShapeDtypeStruct((t, DB), F32),
                   jax.ShapeDtypeStruct((t, DC), F32), jax.ShapeDtypeStruct((D, D), WIRE)],
        scratch_shapes=[pltpu.VMEM((D, D), F32)],
        compiler_params=_params(("arbitrary",)),
    )(dh, oa, ob, oc, wout)


def _lower_bounds(logits):
    depth, n = logits.shape

    def body(l_ref, lb_ref, p_ref):
        rows = [l_ref[l:l + 1, :] for l in range(depth)]
        mx = functools.reduce(jnp.maximum, rows)
        ex = [jnp.exp(r - mx) for r in rows]
        den = functools.reduce(lambda u, v: u + v, ex)
        acc = jnp.zeros_like(den)
        for l in range(depth):
            p = ex[l] / den
            p_ref[l:l + 1, :] = p
            if l > 0:
                acc = acc + p
            lb_ref[l:l + 1, :] = acc

    return pl.pallas_call(
        body, name="lower_bounds",
        out_shape=[jax.ShapeDtypeStruct((depth, n), F32), jax.ShapeDtypeStruct((depth, n), F32)],
    )(logits)


def _lower_bounds_bwd(p, dlb):
    depth, n = p.shape

    def body(p_ref, d_ref, out_ref):
        ps = [p_ref[l:l + 1, :] for l in range(depth)]
        ds = [d_ref[l:l + 1, :] for l in range(depth)]
        dp = [jnp.zeros_like(ps[0]) for _ in range(depth)]
        run = jnp.zeros_like(ps[0])
        for l in range(depth - 1, 0, -1):
            run = run + ds[l]
            dp[l] = run
        dot = functools.reduce(lambda u, v: u + v, [ps[l] * dp[l] for l in range(depth)])
        for l in range(depth):
            out_ref[l:l + 1, :] = ps[l] * (dp[l] - dot)

    return pl.pallas_call(body, name="lower_bounds_bwd", out_shape=jax.ShapeDtypeStruct((depth, n), F32))(p, dlb)


def _hgrn_block(z_ref, lb_ref, rb):
    q, fl = z_ref[:, 0:DA], z_ref[:, DA:2 * DA]
    lb = lb_ref[...]
    sq = _sigmoid(q)
    qs = q * sq
    sg = _sigmoid(fl)
    f = lb + (1.0 - lb) * sg
    k = 1.0 - f
    lf = jnp.log(f)
    row, col = _iota((rb, rb), 0), _iota((rb, rb), 1)
    same = (row // ACH) == (col // ACH)
    causal = same & (row >= col)
    b = _mm_exact_l(jnp.where(causal, 1.0, 0.0).astype(MXU), lf)
    bend = _mm_exact_l(jnp.where(same, 1.0, 0.0).astype(MXU), lf)
    r = 0.5 * bend
    eq, ek, eb, ed = jnp.exp(b - r), jnp.exp(r - b), jnp.exp(b), jnp.exp(bend - b)
    return dict(q=q, lb=lb, sq=sq, sg=sg, f=f, bend=bend, eq=eq, ek=ek, eb=eb, ed=ed,
                qt=qs * eq, kt=k * ek, qe=qs * eb, kd=k * ed, same=same, causal=causal)


def _hgrn_fwd(z, lb, gain):
    t = z.shape[0]
    nc = t // ACH
    cb = min(ACB, nc)
    rb = cb * ACH

    def body(z_ref, lb_ref, g_ref, o_ref, oa_ref, st_ref, st_scr):
        @pl.when(pl.program_id(0) == 0)
        def _():
            st_scr[...] = jnp.zeros_like(st_scr)

        c = _hgrn_block(z_ref, lb_ref, rb)
        for hd in range(NH):
            cols = slice(hd * HD, (hd + 1) * HD)
            v = z_ref[:, 2 * DA + hd * HD:2 * DA + (hd + 1) * HD]
            gg = z_ref[:, 3 * DA + hd * HD:3 * DA + (hd + 1) * HD]
            att = jnp.where(c["causal"], _mm_nt(c["qt"][:, cols], c["kt"][:, cols]), 0.0)
            o_in = _mm(att, v)
            qe, kd, bend = c["qe"][:, cols], c["kd"][:, cols], c["bend"][:, cols]
            st = st_scr[hd]
            outs = []
            for cc in range(cb):
                rows = slice(cc * ACH, (cc + 1) * ACH)
                st_ref[cc, hd] = st
                outs.append(o_in[rows] + _mm_nt(qe[rows], st))
                decay = jnp.exp(jnp.max(bend[rows], axis=0, keepdims=True))
                st = st * decay + _mm_tn(v[rows], kd[rows])
            st_scr[hd] = st
            o = jnp.concatenate(outs, axis=0)
            o_ref[:, cols] = o
            rstd = lax.rsqrt(jnp.mean(o * o, axis=-1, keepdims=True) + EPS)
            oa_ref[:, cols] = (o * rstd * g_ref[:, cols] * (gg * _sigmoid(gg))).astype(oa_ref.dtype)

    return pl.pallas_call(
        body, name="hgrn_fwd", grid=(nc // cb,),
        in_specs=[pl.BlockSpec((rb, 4 * DA), lambda c: (c, 0)),
                  pl.BlockSpec((1, DA), lambda c: (0, 0)),
                  pl.BlockSpec((1, DA), lambda c: (0, 0))],
        out_specs=[pl.BlockSpec((rb, DA), lambda c: (c, 0)),
                   pl.BlockSpec((rb, DA), lambda c: (c, 0)),
                   pl.BlockSpec((cb, NH, HD, HD), lambda c: (c, 0, 0, 0))],
        out_shape=[jax.ShapeDtypeStruct((t, DA), F32), jax.ShapeDtypeStruct((t, DA), SAVE),
                   jax.ShapeDtypeStruct((nc, NH, HD, HD), F32)],
        scratch_shapes=[pltpu.VMEM((NH, HD, HD), F32)],
        compiler_params=_params(("arbitrary",)),
    )(z, lb, gain)


def _hgrn_bwd(z, lb, gain, o, states, doa, after):
    t = z.shape[0]
    nc = t // ACH
    cb = min(ACB, nc)
    rb = cb * ACH
    nblk = nc // cb

    def body(z_ref, lb_ref, g_ref, o_ref, st_ref, doa_ref, _, dz_ref, dgain_ref, dlb_ref, dst_scr):
        @pl.when(pl.program_id(0) == 0)
        def _():
            dst_scr[...] = jnp.zeros_like(dst_scr)
            dgain_ref[...] = jnp.zeros_like(dgain_ref)
            dlb_ref[...] = jnp.zeros_like(dlb_ref)

        c = _hgrn_block(z_ref, lb_ref, rb)
        dbs, dqss, dks = [], [], []
        for hd in range(NH):
            cols = slice(hd * HD, (hd + 1) * HD)
            v = z_ref[:, 2 * DA + hd * HD:2 * DA + (hd + 1) * HD]
            gg = z_ref[:, 3 * DA + hd * HD:3 * DA + (hd + 1) * HD]
            qt, kt, qe, kd, bend = (c[n][:, cols] for n in ("qt", "kt", "qe", "kd", "bend"))
            o = o_ref[:, cols]
            do_a = doa_ref[:, cols]
            gain = g_ref[:, cols]
            sgg = _sigmoid(gg)
            silu_g = gg * sgg
            rstd = lax.rsqrt(jnp.mean(o * o, axis=-1, keepdims=True) + EPS)
            n = o * rstd
            dn = do_a * gain * silu_g
            dg = do_a * n * gain * (sgg * (1.0 + gg * (1.0 - sgg)))
            dgain_ref[:, cols] += jnp.sum(do_a * silu_g * n, axis=0, keepdims=True)
            d_o = rstd * (dn - n * jnp.mean(dn * n, axis=-1, keepdims=True))

            att = jnp.where(c["causal"], _mm_nt(qt, kt), 0.0)
            datt = jnp.where(c["causal"], _mm_nt(d_o, v), 0.0)
            dv_in = _mm_tn(att, d_o)
            dqt = _mm(datt, kt)
            dkt = _mm_tn(datt, qt)
            dsp = dst_scr[hd]
            dvs, dqes, dkds, dbends = [None] * cb, [None] * cb, [None] * cb, [None] * cb
            for cc in reversed(range(cb)):
                rows = slice(cc * ACH, (cc + 1) * ACH)
                st = st_ref[cc, hd]
                dvs[cc] = dv_in[rows] + _mm_nt(kd[rows], dsp)
                dqes[cc] = _mm(d_o[rows], st)
                dkds[cc] = _mm(v[rows], dsp)
                decay = jnp.exp(jnp.max(bend[rows], axis=0, keepdims=True))
                dbend = (decay * jnp.sum(st * dsp, axis=0, keepdims=True)
                         + jnp.sum(dkds[cc] * kd[rows], axis=0, keepdims=True))
                dbends[cc] = jnp.broadcast_to(dbend, (ACH, HD))
                dsp = dsp * decay + _mm_tn(d_o[rows], qe[rows])
            dst_scr[hd] = dsp
            dv, dqe, dkd, dbend = (jnp.concatenate(p, axis=0) for p in (dvs, dqes, dkds, dbends))
            dbs.append((dqt * qt + dqe * qe - dkt * kt - dkd * kd, dbend))
            dqss.append(dqt * c["eq"][:, cols] + dqe * c["eb"][:, cols])
            dks.append(dkt * c["ek"][:, cols] + dkd * c["ed"][:, cols])
            c0 = hd * HD
            dz_ref[:, 2 * DA + c0:2 * DA + c0 + HD] = dv.astype(dz_ref.dtype)
            dz_ref[:, 3 * DA + c0:3 * DA + c0 + HD] = dg.astype(dz_ref.dtype)

        db = jnp.concatenate([p[0] for p in dbs], axis=1)
        dbend = jnp.concatenate([p[1] for p in dbs], axis=1)
        dqs, dk = jnp.concatenate(dqss, axis=1), jnp.concatenate(dks, axis=1)
        row, col = _iota((rb, rb), 0), _iota((rb, rb), 1)
        upper = jnp.where(c["same"] & (row <= col), 1.0, 0.0).astype(MXU)
        dlf = _mm_exact_l(upper, db) + dbend
        df = dlf / c["f"] - dk
        sg, sq, q = c["sg"], c["sq"], c["q"]
        dlb_ref[...] += jnp.sum(df * (1.0 - sg), axis=0, keepdims=True)
        dz_ref[:, DA:2 * DA] = (df * (1.0 - c["lb"]) * sg * (1.0 - sg)).astype(dz_ref.dtype)
        dz_ref[:, 0:DA] = (dqs * (sq * (1.0 + q * (1.0 - sq)))).astype(dz_ref.dtype)

    rev = lambda c: (nblk - 1 - c, 0)
    return pl.pallas_call(
        body, name="hgrn_bwd", grid=(nblk,),
        in_specs=[pl.BlockSpec((rb, 4 * DA), rev),
                  pl.BlockSpec((1, DA), lambda c: (0, 0)),
                  pl.BlockSpec((1, DA), lambda c: (0, 0)),
                  pl.BlockSpec((rb, DA), rev),
                  pl.BlockSpec((cb, NH, HD, HD), lambda c: (nblk - 1 - c, 0, 0, 0)),
                  pl.BlockSpec((rb, DA), rev), _ANY],
        out_specs=[pl.BlockSpec((rb, 4 * DA), rev),
                   pl.BlockSpec((1, DA), lambda c: (0, 0)),
                   pl.BlockSpec((1, DA), lambda c: (0, 0))],
        out_shape=[jax.ShapeDtypeStruct((t, DIN), SAVE), jax.ShapeDtypeStruct((1, DA), F32),
                   jax.ShapeDtypeStruct((1, DA), F32)],
        scratch_shapes=[pltpu.VMEM((NH, HD, HD), F32)],
        compiler_params=_params(("arbitrary",)),
    )(z, lb, gain, o, states, doa, after)


def _shift_down(prev8, x, k):
    cat = jnp.concatenate([prev8, x], axis=0)
    return pltpu.roll(cat, k, axis=0)[8:, :]


def _shift_up(x, next8, k):
    n = x.shape[0]
    cat = jnp.concatenate([x, next8], axis=0)
    return pltpu.roll(cat, n + 8 - k, axis=0)[:n, :]


def _lru_gates(x, prev8, cw_ref, vec_ref, wa_ref, wx_ref):
    xs = [x, _shift_down(prev8, x, 1), _shift_down(prev8, x, 2), _shift_down(prev8, x, 3)]
    xc = vec_ref[0:1, :] + cw_ref[3:4, :] * xs[0] + cw_ref[2:3, :] * xs[1] + cw_ref[1:2, :] * xs[2] + cw_ref[0:1, :] * xs[3]
    r = _sigmoid(_mm(xc, wa_ref[...]) + vec_ref[1:2, :])
    gi = _sigmoid(_mm(xc, wx_ref[...]) + vec_ref[2:3, :])
    lam = vec_ref[3:4, :]
    sp = jnp.maximum(-lam, 0.0) + jnp.log(1.0 + jnp.exp(-jnp.abs(lam)))
    la = -LRU_C * r * sp
    a = jnp.exp(la)
    mult = jnp.sqrt(-_expm1(2.0 * la))
    return xs, xc, r, gi, sp, a, mult


def _scan_down(a, u):
    n = a.shape[0]
    row = _iota(a.shape, 0)
    s = 1
    while s < n:
        keep = row >= s
        ash = jnp.where(keep, pltpu.roll(a, s, axis=0), 1.0)
        ush = jnp.where(keep, pltpu.roll(u, s, axis=0), 0.0)
        u = a * ush + u
        a = a * ash
        s *= 2
    return a, u


def _scan_up(a, u):
    n = a.shape[0]
    row = _iota(a.shape, 0)
    s = 1
    while s < n:
        keep = row < n - s
        ash = jnp.where(keep, pltpu.roll(a, n - s, axis=0), 1.0)
        ush = jnp.where(keep, pltpu.roll(u, n - s, axis=0), 0.0)
        u = a * ush + u
        a = a * ash
        s *= 2
    return a, u


def _lru_fwd(z, cw, vec, wa, wx, tb):
    t = z.shape[0]
    xcol, gcol = (4 * DA) // DB, (4 * DA) // DB + 1

    def body(x_ref, gate_ref, cw_ref, vec_ref, wa_ref, wx_ref, ob_ref, h_ref, xprev_scr, hc_scr):
        @pl.when(pl.program_id(0) == 0)
        def _():
            xprev_scr[...] = jnp.zeros_like(xprev_scr)
            hc_scr[...] = jnp.zeros_like(hc_scr)

        x = x_ref[...]
        _, xc, _, gi, _, a, mult = _lru_gates(x, xprev_scr[...], cw_ref, vec_ref, wa_ref, wx_ref)
        acum, hloc = _scan_down(a, mult * gi * xc)
        h = hloc + acum * hc_scr[0:1, :]
        h_ref[...] = h
        hc_scr[...] = jnp.broadcast_to(_row(h, tb - 1), hc_scr.shape)
        xprev_scr[...] = x[tb - 8:, :]
        y = h * _gelu(gate_ref[...])
        ms = _mm_exact_r(y * y, _group_matrix(DB, 1.0 / GRP).astype(MXU))
        ob_ref[...] = (y * lax.rsqrt(ms + EPS) * vec_ref[4:5, :]).astype(ob_ref.dtype)

    return pl.pallas_call(
        body, name="lru_fwd", grid=(t // tb,),
        in_specs=[pl.BlockSpec((tb, DB), lambda i: (i, xcol)),
                  pl.BlockSpec((tb, DB), lambda i: (i, gcol)),
                  pl.BlockSpec((8, DB), lambda i: (0, 0)),
                  pl.BlockSpec((8, DB), lambda i: (0, 0)),
                  pl.BlockSpec((DB, DB), lambda i: (0, 0)),
                  pl.BlockSpec((DB, DB), lambda i: (0, 0))],
        out_specs=[pl.BlockSpec((tb, DB), lambda i: (i, 0)),
                   pl.BlockSpec((tb, DB), lambda i: (i, 0))],
        out_shape=[jax.ShapeDtypeStruct((t, DB), SAVE), jax.ShapeDtypeStruct((t, DB), F32)],
        scratch_shapes=[pltpu.VMEM((8, DB), F32), pltpu.VMEM((8, DB), F32)],
        compiler_params=_params(("arbitrary",)),
    )(z, z, cw, vec, wa, wx)


def _lru_bwd(z, hseq, dob, cw, vec, wa, wx, dz, tb):
    t = z.shape[0]
    nb = t // tb
    xcol, gcol = (4 * DA) // DB, (4 * DA) // DB + 1
    per = tb // 8

    def body(x_ref, xh_ref, gate_ref, h_ref, hh_ref, dob_ref, cw_ref, vec_ref, wa_ref, wx_ref, _,
             dz_ref, dcw_ref, dvec_ref, dwa_ref, dwx_ref, gc_scr, an_scr, dxc_scr):
        step = pl.program_id(0)
        blk = nb - 1 - step

        @pl.when(step == 0)
        def _():
            for ref in (gc_scr, an_scr, dxc_scr, dcw_ref, dvec_ref, dwa_ref, dwx_ref):
                ref[...] = jnp.zeros_like(ref)

        first = (blk > 0).astype(F32)
        x = x_ref[...]
        xs, xc, r, gi, sp, a, mult = _lru_gates(x, xh_ref[...] * first, cw_ref, vec_ref, wa_ref, wx_ref)
        h = h_ref[...]
        hprev = _shift_down(hh_ref[...] * first, h, 1)
        ge, dge = _gelu_and_grad(gate_ref[...])
        y = h * ge
        gmat = _group_matrix(DB, 1.0 / GRP).astype(MXU)
        rstd = lax.rsqrt(_mm_exact_r(y * y, gmat) + EPS)
        n = y * rstd
        d_ob = dob_ref[...]
        dn = d_ob * vec_ref[4:5, :]
        dvec_ref[4:5, :] += jnp.sum(d_ob * n, axis=0, keepdims=True)
        dy = rstd * (dn - n * _mm_exact_r(dn * n, gmat))
        dh = dy * ge
        dgate = dy * h * dge

        row = _iota(a.shape, 0)
        anext = jnp.where(row == tb - 1, an_scr[0:1, :], pltpu.roll(a, tb - 1, axis=0))
        acum, gloc = _scan_up(anext, dh)
        g = gloc + acum * gc_scr[0:1, :]
        gc_scr[...] = jnp.broadcast_to(_row(g, 0), gc_scr.shape)
        an_scr[...] = jnp.broadcast_to(_row(a, 0), an_scr.shape)

        da = g * hprev
        dmult = g * gi * xc
        dgi = g * mult * xc
        dxc = g * mult * gi
        dla = da * a - dmult * (a * a) / mult
        dr = dla * (-LRU_C * sp)
        dsp = jnp.sum(dla * (-LRU_C * r), axis=0, keepdims=True)
        lam = vec_ref[3:4, :]
        dvec_ref[3:4, :] += -dsp * _sigmoid(-lam)
        dpa = dr * r * (1.0 - r)
        dpx = dgi * gi * (1.0 - gi)
        dwa_ref[...] += _mm_tn(xc, dpa)
        dwx_ref[...] += _mm_tn(xc, dpx)
        dvec_ref[1:2, :] += jnp.sum(dpa, axis=0, keepdims=True)
        dvec_ref[2:3, :] += jnp.sum(dpx, axis=0, keepdims=True)
        dxc = dxc + _mm_nt(dpa, wa_ref[...]) + _mm_nt(dpx, wx_ref[...])
        dvec_ref[0:1, :] += jnp.sum(dxc, axis=0, keepdims=True)
        for tap in range(4):
            dcw_ref[tap:tap + 1, :] += jnp.sum(dxc * xs[3 - tap], axis=0, keepdims=True)
        nxt = dxc_scr[...]
        dx = (cw_ref[3:4, :] * dxc + cw_ref[2:3, :] * _shift_up(dxc, nxt, 1)
              + cw_ref[1:2, :] * _shift_up(dxc, nxt, 2) + cw_ref[0:1, :] * _shift_up(dxc, nxt, 3))
        dxc_scr[...] = dxc[:8, :]
        dz_ref[:, :DB] = dx.astype(dz_ref.dtype)
        dz_ref[:, DB:] = dgate.astype(dz_ref.dtype)

    def halo(col):
        return lambda s: (jnp.maximum((nb - 1 - s) * per - 1, 0), col)

    const = lambda s: (0, 0)
    return pl.pallas_call(
        body, name="lru_bwd", grid=(nb,),
        in_specs=[pl.BlockSpec((tb, DB), lambda s: (nb - 1 - s, xcol)),
                  pl.BlockSpec((8, DB), halo(xcol)),
                  pl.BlockSpec((tb, DB), lambda s: (nb - 1 - s, gcol)),
                  pl.BlockSpec((tb, DB), lambda s: (nb - 1 - s, 0)),
                  pl.BlockSpec((8, DB), halo(0)),
                  pl.BlockSpec((tb, DB), lambda s: (nb - 1 - s, 0)),
                  pl.BlockSpec((8, DB), const), pl.BlockSpec((8, DB), const),
                  pl.BlockSpec((DB, DB), const), pl.BlockSpec((DB, DB), const),
                  pl.BlockSpec(memory_space=pl.ANY)],
        out_specs=[pl.BlockSpec((tb, 2 * DB), lambda s: (nb - 1 - s, (4 * DA) // (2 * DB))),
                   pl.BlockSpec((8, DB), const), pl.BlockSpec((8, DB), const),
                   pl.BlockSpec((DB, DB), const), pl.BlockSpec((DB, DB), const)],
        out_shape=[jax.ShapeDtypeStruct((t, DIN), SAVE), jax.ShapeDtypeStruct((8, DB), F32),
                   jax.ShapeDtypeStruct((8, DB), F32), jax.ShapeDtypeStruct((DB, DB), F32),
                   jax.ShapeDtypeStruct((DB, DB), F32)],
        scratch_shapes=[pltpu.VMEM((8, DB), F32), pltpu.VMEM((8, DB), F32), pltpu.VMEM((8, DB), F32)],
        input_output_aliases={10: 0},
        compiler_params=_params(("arbitrary",)),
    )(z, z, z, hseq, hseq, dob, cw, vec, wa, wx, dz)


def _sgu_block(u_ref, v_ref, w_ref, b_ref, gmat, tb):
    uu, duu = _gelu_and_grad(u_ref[...])
    vv, dvv = _gelu_and_grad(v_ref[...])
    dlt = vv - _mm_exact_r(vv, gmat)
    rstd_v = lax.rsqrt(_mm_exact_r(dlt * dlt, gmat) + EPS)
    vn = dlt * rstd_v
    col = _iota((CCH, DC), 1) // GRP
    causal = _iota((CCH, CCH), 0) >= _iota((CCH, CCH), 1)
    ws = [jnp.where(causal, w_ref[g], 0.0) for g in range(DC // GRP)]
    zs = []
    for ch in range(tb // CCH):
        vn_c = vn[ch * CCH:(ch + 1) * CCH]
        zz = b_ref[...]
        for g, w in enumerate(ws):
            zz = zz + jnp.where(col == g, _mm(w, vn_c), 0.0)
        zs.append(zz)
    return uu, duu, dvv, rstd_v, vn, jnp.concatenate(zs, axis=0), ws, col, causal


def _sgu_fwd(z, w, bias, gain, tb):
    t = z.shape[0]
    ucol, vcol = (4 * DA + 2 * DB) // DC, (4 * DA + 2 * DB) // DC + 1

    def body(u_ref, v_ref, w_ref, b_ref, g_ref, oc_ref):
        gmat = _group_matrix(DC, 1.0 / GRP).astype(MXU)
        uu, _, _, _, _, zz, _, _, _ = _sgu_block(u_ref, v_ref, w_ref, b_ref, gmat, tb)
        y = uu * zz
        ms = _mm_exact_r(y * y, gmat)
        oc_ref[...] = (y * lax.rsqrt(ms + EPS) * g_ref[...]).astype(oc_ref.dtype)

    const = lambda i: (0, 0)
    return pl.pallas_call(
        body, name="sgu_fwd", grid=(t // tb,),
        in_specs=[pl.BlockSpec((tb, DC), lambda i: (i, ucol)),
                  pl.BlockSpec((tb, DC), lambda i: (i, vcol)),
                  pl.BlockSpec((DC // GRP, CCH, CCH), lambda i: (0, 0, 0)),
                  pl.BlockSpec((CCH, DC), const), pl.BlockSpec((1, DC), const)],
        out_specs=pl.BlockSpec((tb, DC), lambda i: (i, 0)),
        out_shape=jax.ShapeDtypeStruct((t, DC), SAVE),
        compiler_params=_params(("parallel",)),
    )(z, z, w, bias, gain)


def _sgu_bwd(z, doc, w, bias, gain, dz, tb):
    t = z.shape[0]
    nb = t // tb
    ucol, vcol = (4 * DA + 2 * DB) // DC, (4 * DA + 2 * DB) // DC + 1
    ng = DC // GRP

    def body(u_ref, v_ref, doc_ref, w_ref, b_ref, g_ref, _, dz_ref, dw_ref, dbias_ref, dgain_ref, dbsum_scr):
        i = pl.program_id(0)

        @pl.when(i == 0)
        def _():
            for ref in (dw_ref, dgain_ref, dbsum_scr):
                ref[...] = jnp.zeros_like(ref)

        gmat = _group_matrix(DC, 1.0 / GRP).astype(MXU)
        uu, duu, dvv, rstd_v, vn, zz, ws, col, causal = _sgu_block(u_ref, v_ref, w_ref, b_ref, gmat, tb)
        y = uu * zz
        rstd = lax.rsqrt(_mm_exact_r(y * y, gmat) + EPS)
        n = y * rstd
        d_oc = doc_ref[...]
        dn = d_oc * g_ref[...]
        dgain_ref[0:1, :] += jnp.sum(d_oc * n, axis=0, keepdims=True)
        dy = rstd * (dn - n * _mm_exact_r(dn * n, gmat))
        dzz = dy * uu
        dz_ref[:, :DC] = (dy * zz * duu).astype(dz_ref.dtype)
        dvns = []
        for ch in range(tb // CCH):
            rows = slice(ch * CCH, (ch + 1) * CCH)
            dzz_c, vn_c = dzz[rows], vn[rows]
            dbsum_scr[...] += dzz_c
            dvn = jnp.zeros_like(dzz_c)
            for g in range(ng):
                sel = col == g
                dvn = dvn + jnp.where(sel, _mm_tn(ws[g], dzz_c), 0.0)
                dw_ref[g] += jnp.where(causal, _mm_nt(jnp.where(sel, dzz_c, 0.0), vn_c), 0.0)
            dvns.append(dvn)
        dvn = jnp.concatenate(dvns, axis=0)
        dv = rstd_v * (dvn - _mm_exact_r(dvn, gmat) - vn * _mm_exact_r(dvn * vn, gmat))
        dz_ref[:, DC:] = (dv * dvv).astype(dz_ref.dtype)

        @pl.when(i == nb - 1)
        def _():
            dbias_ref[...] = _mm_exact_r(dbsum_scr[...], _group_matrix(DC, 1.0).astype(MXU))

    const = lambda i: (0, 0)
    return pl.pallas_call(
        body, name="sgu_bwd", grid=(nb,),
        in_specs=[pl.BlockSpec((tb, DC), lambda i: (i, ucol)),
                  pl.BlockSpec((tb, DC), lambda i: (i, vcol)),
                  pl.BlockSpec((tb, DC), lambda i: (i, 0)),
                  pl.BlockSpec((ng, CCH, CCH), lambda i: (0, 0, 0)),
                  pl.BlockSpec((CCH, DC), const), pl.BlockSpec((1, DC), const),
                  pl.BlockSpec(memory_space=pl.ANY)],
        out_specs=[pl.BlockSpec((tb, 2 * DC), lambda i: (i, (4 * DA + 2 * DB) // (2 * DC))),
                   pl.BlockSpec((ng, CCH, CCH), lambda i: (0, 0, 0)),
                   pl.BlockSpec((CCH, DC), const), pl.BlockSpec((8, DC), const)],
        out_shape=[jax.ShapeDtypeStruct((t, DIN), SAVE), jax.ShapeDtypeStruct((ng, CCH, CCH), F32),
                   jax.ShapeDtypeStruct((CCH, DC), F32), jax.ShapeDtypeStruct((8, DC), F32)],
        scratch_shapes=[pltpu.VMEM((CCH, DC), F32)],
        input_output_aliases={6: 0},
        compiler_params=_params(("arbitrary",)),
    )(z, z, doc, w, bias, gain, dz)


def _head(h, gain, target, tm):
    t = h.shape[0]

    def body(h_ref, g_ref, t_ref, dh_ref, loss_ref, dgain_ref):
        @pl.when(pl.program_id(0) == 0)
        def _():
            loss_ref[...] = jnp.zeros_like(loss_ref)
            dgain_ref[...] = jnp.zeros_like(dgain_ref)

        hh = h_ref[...]
        gain = g_ref[...]
        rstd = lax.rsqrt(jnp.mean(hh * hh, axis=-1, keepdims=True) + EPS)
        xhat = hh * rstd
        err = xhat * gain - t_ref[...]
        per_tok = jnp.mean(err * err, axis=-1, keepdims=True)
        loss_ref[...] += 0.5 * jnp.sum(per_tok, axis=0, keepdims=True)
        dy = err * (1.0 / D)
        dgain_ref[...] += jnp.sum(dy * xhat, axis=0, keepdims=True)
        dxh = dy * gain
        dh_ref[...] = rstd * (dxh - xhat * jnp.mean(dxh * xhat, axis=-1, keepdims=True))

    return pl.pallas_call(
        body, name="head", grid=(t // tm,),
        in_specs=[pl.BlockSpec((tm, D), lambda i: (i, 0)),
                  pl.BlockSpec((1, D), lambda i: (0, 0)),
                  pl.BlockSpec((tm, D), lambda i: (i, 0))],
        out_specs=[pl.BlockSpec((tm, D), lambda i: (i, 0)),
                   pl.BlockSpec((1, 128), lambda i: (0, 0)),
                   pl.BlockSpec((1, D), lambda i: (0, 0))],
        out_shape=[jax.ShapeDtypeStruct((t, D), F32), jax.ShapeDtypeStruct((1, 128), F32),
                   jax.ShapeDtypeStruct((1, D), F32)],
        compiler_params=_params(("arbitrary",)),
    )(h, gain, target)


def _adamw(w, g, m, v):
    m = ADAM_B1 * m + (1.0 - ADAM_B1) * g
    v = ADAM_B2 * v + (1.0 - ADAM_B2) * (g * g)
    m_hat = m / (1.0 - ADAM_B1 ** ADAM_STEP)
    v_hat = v / (1.0 - ADAM_B2 ** ADAM_STEP)
    delta = -ADAM_LR * (m_hat / (jnp.sqrt(v_hat) + ADAM_EPS) + ADAM_WD * w)
    return delta, m, v


def _adamw_big(recv, w, m, v, tr, name, after, transposed=False):
    depth, rows, cols = w.shape
    rspec = (pl.BlockSpec((NDEV, cols, tr), lambda i: (0, 0, i)) if transposed
             else pl.BlockSpec((NDEV, tr, cols), lambda i: (0, i, 0)))

    def body(*refs):
        r_refs = refs[:depth]
        w_ref, m_ref, v_ref, _, g_out, d_out, m_out, v_out = refs[depth:]
        for l in range(depth):
            g = r_refs[l][0].astype(F32)
            for k in range(1, NDEV):
                g = g + r_refs[l][k].astype(F32)
            if transposed:
                g = g.T
            delta, m_, v_ = _adamw(w_ref[l], g, m_ref[l], v_ref[l])
            g_out[l] = g
            d_out[l] = delta
            m_out[l] = m_
            v_out[l] = v_

    spec = pl.BlockSpec((depth, tr, cols), lambda i: (0, i, 0))
    return pl.pallas_call(
        body, name=name, grid=(rows // tr,),
        in_specs=[rspec] * depth + [spec] * 3
        + [pl.BlockSpec(memory_space=pl.ANY)],
        out_specs=[spec] * 4, out_shape=[jax.ShapeDtypeStruct((depth, rows, cols), F32)] * 4,
        compiler_params=_params(("parallel",)),
    )(*recv, w, m, v, after)


def _sum_devices(recv):
    _, r, _ = recv.shape

    def body(r_ref, out_ref):
        g = r_ref[0]
        for k in range(1, NDEV):
            g = g + r_ref[k]
        out_ref[...] = g

    return pl.pallas_call(body, name="sum_devices", out_shape=jax.ShapeDtypeStruct((r, 128), F32))(recv)


def _adamw_small(w, g, m, v):
    def body(w_ref, g_ref, m_ref, v_ref, d_out, m_out, v_out):
        delta, m_, v_ = _adamw(w_ref[...], g_ref[...], m_ref[...], v_ref[...])
        d_out[...] = delta
        m_out[...] = m_
        v_out[...] = v_

    return pl.pallas_call(body, name="adamw_small", out_shape=[jax.ShapeDtypeStruct(w.shape, F32)] * 3)(w, g, m, v)


def _pack(arrs):
    flat = jnp.concatenate([a.reshape(-1) for a in arrs])
    pad = (-flat.shape[0]) % 1024
    return jnp.pad(flat, (0, pad)).reshape(-1, 128)


def _unpack(buf, like):
    flat = buf.reshape(-1)
    out, off = [], 0
    for a in like:
        out.append(flat[off:off + a.size].reshape(a.shape))
        off += a.size
    return out


def _block_diag(w):
    nb, bd, _ = w.shape
    eye = jnp.eye(nb, dtype=w.dtype)
    return (eye[:, None, :, None] * w[:, :, None, :]).reshape(nb * bd, nb * bd)


def _diag_blocks(w):
    nb = w.shape[0] // GRP
    return jnp.stack([w[g * GRP:(g + 1) * GRP, g * GRP:(g + 1) * GRP] for g in range(nb)])


SMALL = ['ffn1_norm', 'mix_norm', 'hgrn_lb_logits', 'hgrn_norm', 'conv_b', 'lru_wa', 'lru_ba', 'lru_wx', 'lru_bx',
         'lru_lambda', 'lru_norm', 'sgu_w', 'sgu_b', 'sgu_norm', 'ffn2_norm', 'final_norm']
NAMES = ['ffn1_norm', 'ffn1_wg', 'ffn1_wu', 'ffn1_wd', 'mix_norm', 'w_in', 'hgrn_lb_logits', 'hgrn_norm', 'conv_w',
         'conv_b', 'lru_wa', 'lru_ba', 'lru_wx', 'lru_bx', 'lru_lambda', 'lru_norm', 'sgu_w', 'sgu_b', 'sgu_norm',
         'w_out', 'ffn2_norm', 'ffn2_wg', 'ffn2_wu', 'ffn2_wd', 'final_norm']


def _step(x, target, w, m, v):
    depth = w['ffn1_wg'].shape[0]
    t = x.shape[1]
    h = x.reshape(t, D)
    target = target.reshape(t, D)
    tm_f, tm_b, tb = min(TM_F, t), min(TM_B, t), min(TB, t)
    my = 4 * lax.axis_index("x") + 2 * lax.axis_index("y") + lax.axis_index("c")

    cw_tile = jnp.pad(w['conv_w'].reshape(-1, 128), ((0, 8 - depth), (0, 0)))
    lbs, lb_soft = _lower_bounds(w['hgrn_lb_logits'])

    def row(a):
        return a.reshape(1, -1)

    none = jnp.zeros((8, 128), F32)

    def tr(a):
        return jnp.swapaxes(a, -1, -2)

    def shards(l, unit, zero=None):
        def cast(a):
            return (a if zero is None else a + zero).astype(WIRE)

        if unit == 1:
            return [cast(tr(w['w_in'][l])), cast(w['w_out'][l])]
        f = 'ffn1' if unit == 0 else 'ffn2'
        return [cast(tr(w[f + '_wg'][l])), cast(tr(w[f + '_wu'][l])), cast(w[f + '_wd'][l])]

    units = [(l, u) for l in range(depth) for u in range(3)]

    def start_ici(idx, deps=()):
        return _transfer_start(ready[idx], True, "gather_ici_%d_%d" % units[idx], deps=deps)

    def relay(idx, handle, after, then=None):
        lands = _transfer_wait(handle, after, "gather_ici_wait_%d_%d" % units[idx])
        more = ready[then] if then is not None and then < len(units) else ()
        return _forward_start(lands, "gather_d2d_%d_%d" % units[idx], more)

    pipe = dict(idx=0)
    ready = [shards(0, 0)]
    first = start_ici(0)
    ready += [shards(l, u, first['token'][0, 0]) for l, u in units[1:]]
    pipe['ici'] = start_ici(1, deps=(first['token'],))
    cast_all = [a for unit in ready[2:] for a in unit]
    conv_flight = _transfer_start([cw_tile], True, "gather_conv_start", deps=(pipe['ici']['token'], *cast_all),
                                  direct=True)
    pipe['d2d'], _ = relay(0, first, conv_flight['token'])

    def next_weights(after):
        idx = pipe['idx']
        lands = _transfer_wait(pipe['d2d'], after, "gather_d2d_wait_%d_%d" % units[idx])
        pipe['idx'] = idx + 1
        tok = none
        if idx + 1 < len(units):
            pipe['d2d'], pipe['ici'] = relay(idx + 1, pipe['ici'], lands[-1], idx + 2)
            tok = pipe['d2d']['token']
        return lands, tok

    saved = []
    for l in range(depth):
        lands, tok = next_weights(h)
        s = dict(ffn1=[a.reshape(FF, D) for a in lands], h0=h)
        h, s['xn1'], s['a1'], s['b1'] = _ffn_fwd(h, row(w['ffn1_norm'][l]), *s['ffn1'], tm_f, tok)
        s['h1'] = h
        (win, wout), tok = next_weights(h)
        win, wout = win.reshape(DIN, D), wout.reshape(D, D)
        s['win'], s['wout'] = win, wout
        z, s['xnm'] = _inproj_fwd(h, row(w['mix_norm'][l]), win, tm_f, tok)
        s['z'] = z
        s['o'], oa, s['states'] = _hgrn_fwd(z, row(lbs[l]), row(w['hgrn_norm'][l]))
        if l == 0:
            cw_all = _transfer_wait(conv_flight, z, "gather_conv_wait")[0][:, :depth]
            conv_w = jnp.moveaxis(cw_all.reshape(NDEV, depth, 4, DB // NDEV), 0, 2).reshape(depth, 4, DB)
        s['cw'] = jnp.pad(conv_w[l], ((0, 4), (0, 0)))
        s['vec'] = jnp.concatenate([row(w['conv_b'][l]), row(w['lru_ba'][l]), row(w['lru_bx'][l]),
                                    row(w['lru_lambda'][l]), row(w['lru_norm'][l]), jnp.zeros((3, DB), F32)])
        s['wa'], s['wx'] = _block_diag(w['lru_wa'][l]), _block_diag(w['lru_wx'][l])
        ob, s['hseq'] = _lru_fwd(z, s['cw'], s['vec'], s['wa'], s['wx'], tb)
        s['bias'] = jnp.repeat(w['sgu_b'][l].T, GRP, axis=1)
        oc = _sgu_fwd(z, w['sgu_w'][l], s['bias'], row(w['sgu_norm'][l]), tb)
        s['oa'], s['ob'], s['oc'] = oa, ob, oc
        h = _outproj_fwd(h, oa, ob, oc, wout, tm_f)
        s['h2'] = h
        lands, tok = next_weights(h)
        s['ffn2'] = [a.reshape(FF, D) for a in lands]
        h, s['xn2'], s['a2'], s['b2'] = _ffn_fwd(h, row(w['ffn2_norm'][l]), *s['ffn2'], tm_f, tok)
        saved.append(s)

    dh, loss_part, g_final = _head(h, row(w['final_norm']), target, tm_f)
    loss = lax.psum(loss_part[0, 0], ("x", "y", "c"))

    recv = {k: [None] * depth for k in ('wg1', 'wu1', 'wd1', 'wg2', 'wu2', 'wd2', 'win', 'wout')}
    flight = []

    def land(after):
        handle, kinds, l = flight.pop()
        for k, a in zip(kinds, _transfer_wait(handle, after, f"exchange_wait_{kinds[0]}_{l}")):
            recv[k][l] = a

    def exchange(arrs, kinds, l, deps=()):
        handle = _transfer_start(arrs, False, f"exchange_start_{kinds[0]}_{l}", deps=deps)
        if flight:
            land(handle['token'])
        flight.append((handle, kinds, l))
        return handle['token']

    small = {k: [None] * depth for k in SMALL if k != 'final_norm'}
    dconv = [None] * depth
    dlb = [None] * depth
    tok = none
    for l in reversed(range(depth)):
        s = saved[l]
        dh, g, *cot = _ffn_bwd_x(dh, s['h2'], row(w['ffn2_norm'][l]), s['a2'], s['b2'], *s['ffn2'], tm_b, tok)
        dws = _ffn_bwd_w(s['xn2'], *cot, tm_b)
        tok = exchange([a.reshape(NDEV, FFS, D) for a in dws], ('wg2', 'wu2', 'wd2'), l)
        small['ffn2_norm'][l] = g
        doa, dob, doc, dwout = _outproj_bwd(dh, s['oa'], s['ob'], s['oc'], s['wout'], tm_f)
        dz, g_hn, dlb[l] = _hgrn_bwd(s['z'], row(lbs[l]), row(w['hgrn_norm'][l]), s['o'], s['states'], doa, tok)
        small['hgrn_norm'][l] = g_hn
        dz, dcw, dvec, dwa, dwx = _lru_bwd(s['z'], s['hseq'], dob, s['cw'], s['vec'], s['wa'], s['wx'], dz, tb)
        dconv[l] = dcw[:4]
        small['conv_b'][l], small['lru_ba'][l], small['lru_bx'][l] = dvec[0], dvec[1].reshape(4, GRP), dvec[2].reshape(4, GRP)
        small['lru_lambda'][l], small['lru_norm'][l] = dvec[3], dvec[4]
        small['lru_wa'][l], small['lru_wx'][l] = _diag_blocks(dwa), _diag_blocks(dwx)
        dz, dsw, dbias, dgc = _sgu_bwd(s['z'], doc, w['sgu_w'][l], s['bias'], row(w['sgu_norm'][l]), dz, tb)
        small['sgu_w'][l], small['sgu_b'][l], small['sgu_norm'][l] = dsw, dbias[:, ::GRP].T, dgc[0]
        dwin = _inproj_bwd_w(s['xnm'], dz, tm_f)
        tok = exchange([dwin.reshape(NDEV, DINS, D), dwout.reshape(NDEV, D // NDEV, D)], ('win', 'wout'), l)
        dh, g = _inproj_bwd_x(dh, dz, s['h1'], row(w['mix_norm'][l]), s['win'], tm_f, tok)
        small['mix_norm'][l] = g
        tok = none
        if l == 0:
            small['ffn1_norm'][0] = jnp.zeros((1, D), F32)
            small['hgrn_lb_logits'] = list(_lower_bounds_bwd(lb_soft, jnp.concatenate(dlb, axis=0)))
            parts = [jnp.stack([small[k][j].reshape(w[k].shape[1:]) for j in range(depth)])
                     for k in SMALL if k != 'final_norm']
            parts += [g_final.reshape(D), jnp.stack(dconv)]
            small_flight = _transfer_start([_pack(parts)], True, "gather_small_start", direct=True)
            tok = small_flight['token']
        dh, g, *cot = _ffn_bwd_x(dh, s['h0'], row(w['ffn1_norm'][l]), s['a1'], s['b1'], *s['ffn1'], tm_b, tok)
        dws = _ffn_bwd_w(s['xn1'], *cot, tm_b)
        before = ()
        if l == 0:
            g_last = _all_gather([g.reshape(8, 128)], "gather_last")[0]
            before = (g_last,)
        else:
            small['ffn1_norm'][l] = g
        tok = exchange([a.reshape(NDEV, FFS, D) for a in dws], ('wg1', 'wu1', 'wd1'), l, before)
    grad_x = dh.reshape(1, t, D)

    out = {}
    last = flight[0][0]['token']

    def ffn_update(f, n, after):
        for kind in ('wg', 'wu'):
            k = f + '_' + kind
            res = _adamw_big(recv[kind + n], tr(w[k]), tr(m[k]), tr(v[k]), 32, "adamw_ffn", after)
            out[k] = tuple(tr(a) for a in res)
        k = f + '_wd'
        out[k] = _adamw_big(recv['wd' + n], w[k], m[k], v[k], 32, "adamw_ffn", after)

    ffn_update('ffn2', '2', last)
    out['w_in'] = _adamw_big(recv['win'], w['w_in'], m['w_in'], v['w_in'], 128, "adamw_win", last, transposed=True)
    out['w_out'] = _adamw_big(recv['wout'], w['w_out'], m['w_out'], v['w_out'], 64, "adamw_wout", last)

    total = _sum_devices(_transfer_wait(small_flight, g_last, "gather_small_wait")[0])
    like = [w[k] for k in SMALL] + [jax.ShapeDtypeStruct((depth, 4, DB), F32)]
    grads = _unpack(total, like)
    gsmall = dict(zip(SMALL, grads[:-1]))
    gsmall['ffn1_norm'] = gsmall['ffn1_norm'].at[0].set(_sum_devices(g_last).reshape(D))
    gsmall['conv_w'] = lax.dynamic_slice_in_dim(grads[-1], my * (DB // NDEV), DB // NDEV, axis=2)
    keys = SMALL + ['conv_w']
    dl, mm, vv = _adamw_small(_pack([w[k] for k in keys]), _pack([gsmall[k] for k in keys]),
                              _pack([m[k] for k in keys]), _pack([v[k] for k in keys]))
    like = [w[k] for k in keys]
    for k, d_, m_, v_ in zip(keys, _unpack(dl, like), _unpack(mm, like), _unpack(vv, like)):
        out[k] = (gsmall[k], d_, m_, v_)
    done = [dl] + [out[k][1][0] for k in ('ffn2_wg', 'ffn2_wu', 'ffn2_wd', 'w_in', 'w_out')]
    land(functools.reduce(lambda p, q: p + q, [a[:1, :1] for a in done]))
    ffn_update('ffn1', '1', last)

    return (loss, grad_x, *[out[k][0] for k in NAMES], *[out[k][1] for k in NAMES],
            *[out[k][2] for k in NAMES], *[out[k][3] for k in NAMES])


def kernel(x, ffn1_norm, ffn1_wg, ffn1_wu, ffn1_wd, mix_norm, w_in, hgrn_lb_logits, hgrn_norm, conv_w, conv_b, lru_wa, lru_ba, lru_wx, lru_bx, lru_lambda, lru_norm, sgu_w, sgu_b, sgu_norm, w_out, ffn2_norm, ffn2_wg, ffn2_wu, ffn2_wd, final_norm, loss_target, m_ffn1_norm, m_ffn1_wg, m_ffn1_wu, m_ffn1_wd, m_mix_norm, m_w_in, m_hgrn_lb_logits, m_hgrn_norm, m_conv_w, m_conv_b, m_lru_wa, m_lru_ba, m_lru_wx, m_lru_bx, m_lru_lambda, m_lru_norm, m_sgu_w, m_sgu_b, m_sgu_norm, m_w_out, m_ffn2_norm, m_ffn2_wg, m_ffn2_wu, m_ffn2_wd, m_final_norm, v_ffn1_norm, v_ffn1_wg, v_ffn1_wu, v_ffn1_wd, v_mix_norm, v_w_in, v_hgrn_lb_logits, v_hgrn_norm, v_conv_w, v_conv_b, v_lru_wa, v_lru_ba, v_lru_wx, v_lru_bx, v_lru_lambda, v_lru_norm, v_sgu_w, v_sgu_b, v_sgu_norm, v_w_out, v_ffn2_norm, v_ffn2_wg, v_ffn2_wu, v_ffn2_wd, v_final_norm):
    args = locals()
    w = {k: args[k] for k in NAMES}
    m = {k: args['m_' + k] for k in NAMES}
    v = {k: args['v_' + k] for k in NAMES}
    return _step(x, loss_target, w, m, v)
```

```python
import functools

import jax
import jax.numpy as jnp
from jax import lax
from jax.experimental import pallas as pl
from jax.experimental.pallas import tpu as pltpu

F32 = jnp.float32
MXU = jnp.bfloat16
SAVE = jnp.bfloat16
WIRE = jnp.bfloat16

NDEV = 8
D = 1024
FF = 2816
FFS = FF // NDEV
FB = 256
FBX = FF // 2
DIN = 3072
DINS = DIN // NDEV
ZB = 512
ZBW = 1024
DA, DB, DC = 512, 256, 256
HD = 128
NH = DA // HD
ACH = 64
ACB = 4
CCH = 128
GRP = 64
EPS = 1e-6
LRU_C = 8.0
VMEM_LIMIT = 60 * 1024 * 1024
TM_F = 1024
TM_B = 512
TB = 1024
SUB = 512
SUB_X = 256

ADAM_LR, ADAM_B1, ADAM_B2, ADAM_EPS, ADAM_WD, ADAM_STEP = 0.001, 0.9, 0.999, 1e-08, 0.01, 10

MESH = pl.DeviceIdType.MESH


def _mm(a, b):
    return jnp.dot(a.astype(MXU), b.astype(MXU), preferred_element_type=F32)


def _mm_nt(a, b):
    return lax.dot_general(a.astype(MXU), b.astype(MXU), (((1,), (1,)), ((), ())), preferred_element_type=F32)


def _mm_tn(a, b):
    return lax.dot_general(a.astype(MXU), b.astype(MXU), (((0,), (0,)), ((), ())), preferred_element_type=F32)


def _split3(x):
    x1 = x.astype(MXU)
    r1 = x - x1.astype(F32)
    x2 = r1.astype(MXU)
    r2 = r1 - x2.astype(F32)
    return x1, x2, r2.astype(MXU)


def _mm_exact_l(c, x):
    x1, x2, x3 = _split3(x)
    return _mm(c, x1) + _mm(c, x2) + _mm(c, x3)


def _mm_exact_r(x, c):
    x1, x2, x3 = _split3(x)
    return _mm(x1, c) + _mm(x2, c) + _mm(x3, c)


def _sigmoid(x):
    return 1.0 / (1.0 + jnp.exp(-x))


def _gelu(x):
    c, k = 0.7978845608028654, 0.044715
    th = jnp.tanh(c * (x + k * x * x * x))
    return 0.5 * x * (1.0 + th)


def _gelu_and_grad(x):
    c, k = 0.7978845608028654, 0.044715
    th = jnp.tanh(c * (x + k * x * x * x))
    g = 0.5 * x * (1.0 + th)
    dg = 0.5 * (1.0 + th) + 0.5 * x * (1.0 - th * th) * c * (1.0 + 3.0 * k * x * x)
    return g, dg


def _expm1(x):
    series = x * (1.0 + x * (0.5 + x * (1.0 / 6.0 + x * (1.0 / 24.0 + x * (1.0 / 120.0)))))
    return jnp.where(jnp.abs(x) < 0.05, series, jnp.exp(x) - 1.0)


def _iota(shape, dim):
    return lax.broadcasted_iota(jnp.int32, shape, dim)


def _group_matrix(n, value):
    r, c = _iota((n, n), 0), _iota((n, n), 1)
    return jnp.where((r // GRP) == (c // GRP), value, 0.0).astype(F32)


def _row(x, k):
    r = _iota(x.shape, 0)
    return jnp.sum(jnp.where(r == k, x, 0.0), axis=0, keepdims=True)


def _rms_bwd(dxn, hh, gain):
    rstd = lax.rsqrt(jnp.mean(hh * hh, axis=-1, keepdims=True) + EPS)
    xhat = hh * rstd
    dxh = dxn * gain
    dh = rstd * (dxh - xhat * jnp.mean(dxh * xhat, axis=-1, keepdims=True))
    return dh, jnp.sum(dxn * xhat, axis=0, keepdims=True)


def _params(sem):
    return pltpu.CompilerParams(dimension_semantics=sem, vmem_limit_bytes=VMEM_LIMIT)


def _all_gather(arrs, name):
    n = len(arrs)

    def body(*refs):
        ins, outs = refs[:n], refs[n:2 * n]
        send_sems, recv_sems, local_sems = refs[2 * n:]
        x, y, c = lax.axis_index("x"), lax.axis_index("y"), lax.axis_index("c")
        me, sibling = (x, y, c), (x, y, 1 - c)
        chips = [(1 - x, y), (x, 1 - y), (1 - x, 1 - y)]

        def slot(px, py, pc):
            return 4 * px + 2 * py + pc

        def copy(a, k, block, to, src=None):
            dst = outs[a].at[slot(*block)]
            return pltpu.make_async_remote_copy(
                src_ref=dst if src is None else src, dst_ref=dst,
                send_sem=send_sems.at[a * 7 + k], recv_sem=recv_sems.at[a * 7 + k],
                device_id=to, device_id_type=MESH)

        started = []
        for a in range(n):
            mine = pltpu.make_async_copy(ins[a], outs[a].at[slot(*me)], local_sems.at[a])
            mine.start()
            started.append(mine)
        first = []
        for a in range(n):
            first.append(copy(a, 0, me, sibling, src=ins[a]))
            first += [copy(a, 1 + j, me, (*chip, c), src=ins[a]) for j, chip in enumerate(chips)]
        for cp in first:
            cp.start()
        passed = []
        for a in range(n):
            for j, chip in enumerate(chips):
                copy(a, 1 + j, (*chip, c), me).wait_recv()
                fwd = copy(a, 4 + j, (*chip, c), sibling)
                fwd.start()
                passed.append(fwd)
        for a in range(n):
            copy(a, 0, sibling, me).wait_recv()
            for j, chip in enumerate(chips):
                copy(a, 4 + j, (*chip, 1 - c), me).wait_recv()
        for cp in first + passed:
            cp.wait_send()
        for mine in started:
            mine.wait()

    hbm = pl.BlockSpec(memory_space=pl.ANY)
    return pl.pallas_call(
        body, name=name,
        out_shape=[jax.ShapeDtypeStruct((NDEV,) + a.shape, a.dtype) for a in arrs],
        in_specs=[hbm] * n, out_specs=[hbm] * n,
        scratch_shapes=[pltpu.SemaphoreType.DMA((7 * n,)), pltpu.SemaphoreType.DMA((7 * n,)),
                        pltpu.SemaphoreType.DMA((n,))],
    )(*arrs)


def _peers():
    x, y, c = lax.axis_index("x"), lax.axis_index("y"), lax.axis_index("c")
    peers = [(x ^ ((k >> 2) & 1), y ^ ((k >> 1) & 1), c ^ (k & 1)) for k in range(1, NDEV)]
    return (x, y, c), 4 * x + 2 * y + c, peers


_HBM = pl.BlockSpec(memory_space=pltpu.HBM)
_SEM = pl.BlockSpec(memory_space=pltpu.SEMAPHORE)
_ANY = pl.BlockSpec(memory_space=pl.ANY)
_EFFECT = pltpu.SideEffectType.DATAFLOW_SIDE_EFFECTING


def _transfer_start(arrs, gather, name, deps=(), direct=False):
    n, nd = len(arrs), len(deps)
    shapes = [((NDEV,) + a.shape) if gather else a.shape for a in arrs]

    def body(*refs):
        ins, lands = refs[:n], refs[n:2 * n]
        send_sems, recv_sems, local_sems = refs[2 * n + nd:2 * n + nd + 3]
        token = refs[-1]
        (x, y, c), my, peers = _peers()
        if gather and not direct:
            peers = [(x, y, 1 - c), (1 - x, y, c), (x, 1 - y, c), (1 - x, 1 - y, c)]
        for a in range(n):
            own = ins[a] if gather else ins[a].at[my]
            pltpu.make_async_copy(own, lands[a].at[my], local_sems.at[a]).start()
        for a in range(n):
            for peer in peers:
                src = ins[a] if gather else ins[a].at[4 * peer[0] + 2 * peer[1] + peer[2]]
                pltpu.make_async_remote_copy(
                    src_ref=src, dst_ref=lands[a].at[my], send_sem=send_sems.at[a], recv_sem=recv_sems.at[a],
                    device_id=peer, device_id_type=MESH).start()
        token[...] = jnp.zeros_like(token)

    out_shape = [pltpu.SemaphoreType.DMA((n,))] * 3
    out_shape += [pltpu.HBM(a.shape, a.dtype) for a in arrs]
    out_shape += [pltpu.HBM(s, a.dtype) for s, a in zip(shapes, arrs)]
    out_shape += [jax.ShapeDtypeStruct((8, 128), F32)]
    operands = [pltpu.with_memory_space_constraint(a, pltpu.HBM) for a in arrs]
    operands += [pltpu.with_memory_space_constraint(lax.empty(s, a.dtype), pltpu.HBM) for s, a in zip(shapes, arrs)]
    res = pl.pallas_call(
        body, name=name, out_shape=out_shape,
        in_specs=[_HBM] * (2 * n) + [pl.BlockSpec(memory_space=pl.ANY)] * nd,
        out_specs=[_SEM] * 3 + [_HBM] * (2 * n) + [pl.BlockSpec(memory_space=pltpu.VMEM)],
        input_output_aliases={i: 3 + i for i in range(2 * n)},
        compiler_params=pltpu.CompilerParams(has_side_effects=_EFFECT),
    )(*operands, *deps)
    return dict(sems=res[:3], src=res[3:3 + n], lands=res[3 + n:3 + 2 * n], token=res[-1], n=n,
                count=4 if gather and not direct else NDEV - 1)


def _forward_start(lands, name, shards=()):
    n, m = len(lands), len(shards)
    zones = [(NDEV,) + a.shape for a in shards]

    def body(*refs):
        zone, ins, fresh = refs[:n], refs[n:n + m], refs[n + m:n + 2 * m]
        sems = refs[n + 2 * m:n + 2 * m + (5 if m else 2)]
        token = refs[-1]
        (x, y, c), my, _ = _peers()
        for a in range(n):
            for px, py in ((1 - x, y), (x, 1 - y), (1 - x, 1 - y)):
                block = zone[a].at[4 * px + 2 * py + c]
                pltpu.make_async_remote_copy(
                    src_ref=block, dst_ref=block, send_sem=sems[0].at[a], recv_sem=sems[1].at[a],
                    device_id=(x, y, 1 - c), device_id_type=MESH).start()
        for a in range(m):
            pltpu.make_async_copy(ins[a], fresh[a].at[my], sems[4].at[a]).start()
            for peer in ((x, y, 1 - c), (1 - x, y, c), (x, 1 - y, c), (1 - x, 1 - y, c)):
                pltpu.make_async_remote_copy(
                    src_ref=ins[a], dst_ref=fresh[a].at[my], send_sem=sems[2].at[a], recv_sem=sems[3].at[a],
                    device_id=peer, device_id_type=MESH).start()
        token[...] = jnp.zeros_like(token)

    sem_shapes = [pltpu.SemaphoreType.DMA((n,))] * 2 + ([pltpu.SemaphoreType.DMA((m,))] * 3 if m else [])
    ns = len(sem_shapes)
    thru = list(lands) + list(shards)
    operands = thru + [pltpu.with_memory_space_constraint(lax.empty(s, a.dtype), pltpu.HBM)
                       for s, a in zip(zones, shards)]
    res = pl.pallas_call(
        body, name=name,
        out_shape=sem_shapes + [pltpu.HBM(a.shape, a.dtype) for a in thru]
        + [pltpu.HBM(s, a.dtype) for s, a in zip(zones, shards)] + [jax.ShapeDtypeStruct((8, 128), F32)],
        in_specs=[_HBM] * (n + 2 * m),
        out_specs=[_SEM] * ns + [_HBM] * (n + 2 * m) + [pl.BlockSpec(memory_space=pltpu.VMEM)],
        input_output_aliases={i: ns + i for i in range(n + 2 * m)},
        compiler_params=pltpu.CompilerParams(has_side_effects=_EFFECT),
    )(*[pltpu.with_memory_space_constraint(a, pltpu.HBM) for a in thru], *operands[n + m:])
    forward = dict(sems=res[:2], src=[], lands=res[ns:ns + n], token=res[-1], n=n, count=3)
    nxt = None
    if m:
        nxt = dict(sems=res[2:5], src=res[ns + n:ns + n + m], lands=res[ns + n + m:ns + n + 2 * m],
                   token=res[-1], n=m, count=4)
    return forward, nxt


def _transfer_wait(handle, after, name):
    n, count = handle["n"], handle["count"]
    src, lands, sems = list(handle["src"]), list(handle["lands"]), list(handle["sems"])
    ns = len(src)

    def body(*refs):
        zone = refs[ns:ns + n]
        sem_refs = refs[ns + n:ns + n + len(sems)]
        me, _, _ = _peers()
        for a in range(n):
            moved = zone[a].at[pl.ds(0, count)]
            both = pltpu.make_async_remote_copy(
                src_ref=moved, dst_ref=moved, send_sem=sem_refs[0].at[a], recv_sem=sem_refs[1].at[a],
                device_id=me, device_id_type=MESH)
            both.wait_send()
            both.wait_recv()
            if len(sems) == 3:
                pltpu.make_async_copy(zone[a].at[0], zone[a].at[1], sem_refs[2].at[a]).wait()

    res = pl.pallas_call(
        body, name=name,
        out_shape=[pltpu.HBM(a.shape, a.dtype) for a in src + lands],
        in_specs=[_HBM] * (ns + n) + [_SEM] * len(sems) + [pl.BlockSpec(memory_space=pl.ANY)],
        out_specs=[_HBM] * (ns + n),
        input_output_aliases={i: i for i in range(ns + n)},
        compiler_params=pltpu.CompilerParams(has_side_effects=_EFFECT),
    )(*src, *lands, *sems, after)
    return list(res[ns:])


def _ffn_fwd(h, gain, wg, wu, wd, tm, after):
    t = h.shape[0]
    nj = FF // FBX

    def body(h_ref, g_ref, wg_ref, wu_ref, wd_ref, _, out_ref, xn_ref, a_ref, b_ref, acc_ref):
        j = pl.program_id(1)

        @pl.when(j == 0)
        def _():
            hh = h_ref[...]
            rstd = lax.rsqrt(jnp.mean(hh * hh, axis=-1, keepdims=True) + EPS)
            xn_ref[...] = (hh * rstd * g_ref[...]).astype(xn_ref.dtype)
            acc_ref[...] = jnp.zeros_like(acc_ref)

        sub = min(SUB, tm)
        for r in range(tm // sub):
            rows = slice(r * sub, (r + 1) * sub)
            xn = xn_ref[rows, :]
            y = None
            for c0 in range(0, FBX, FB):
                cols = slice(c0, min(c0 + FB, FBX))
                a = _mm_nt(xn, wg_ref[cols, :])
                b = _mm_nt(xn, wu_ref[cols, :])
                a_ref[rows, cols] = a.astype(a_ref.dtype)
                b_ref[rows, cols] = b.astype(b_ref.dtype)
                part = _mm(a * _sigmoid(a) * b, wd_ref[cols, :])
                y = part if y is None else y + part
            acc_ref[rows, :] += y

        @pl.when(j == nj - 1)
        def _():
            out_ref[...] = h_ref[...] + 0.5 * acc_ref[...]

    wspec = pl.BlockSpec((FBX, D), lambda i, j: (j, 0))
    return pl.pallas_call(
        body, name="ffn_fwd", grid=(t // tm, nj),
        in_specs=[pl.BlockSpec((tm, D), lambda i, j: (i, 0)),
                  pl.BlockSpec((1, D), lambda i, j: (0, 0)), wspec, wspec, wspec, _ANY],
        out_specs=[pl.BlockSpec((tm, D), lambda i, j: (i, 0)),
                   pl.BlockSpec((tm, D), lambda i, j: (i, 0)),
                   pl.BlockSpec((tm, FBX), lambda i, j: (i, j)),
                   pl.BlockSpec((tm, FBX), lambda i, j: (i, j))],
        out_shape=[jax.ShapeDtypeStruct((t, D), F32), jax.ShapeDtypeStruct((t, D), SAVE),
                   jax.ShapeDtypeStruct((t, FF), SAVE), jax.ShapeDtypeStruct((t, FF), SAVE)],
        scratch_shapes=[pltpu.VMEM((tm, D), F32)],
        compiler_params=_params(("parallel", "arbitrary")),
    )(h, gain, wg, wu, wd, after)


def _ffn_bwd_x(dout, h, gain, a_sv, b_sv, wg, wu, wd, tm, after):
    t = h.shape[0]
    nj = FF // FBX

    def body(dout_ref, h_ref, g_ref, a_ref, b_ref, wg_ref, wu_ref, wd_ref, _,
             dh_ref, dgain_ref, dy_ref, da_ref, db_ref, s_ref, acc_ref):
        i, j = pl.program_id(0), pl.program_id(1)

        @pl.when((i == 0) & (j == 0))
        def _():
            dgain_ref[...] = jnp.zeros_like(dgain_ref)

        @pl.when(j == 0)
        def _():
            dy_ref[...] = (0.5 * dout_ref[...]).astype(dy_ref.dtype)
            acc_ref[...] = jnp.zeros_like(acc_ref)

        sub = min(SUB_X, tm)
        for r in range(tm // sub):
            rows = slice(r * sub, (r + 1) * sub)
            dy = dy_ref[rows, :]
            dx = None
            for c0 in range(0, FBX, FB):
                cols = slice(c0, min(c0 + FB, FBX))
                ds = _mm_nt(dy, wd_ref[cols, :])
                a, b = a_ref[rows, cols].astype(F32), b_ref[rows, cols].astype(F32)
                sg = _sigmoid(a)
                sa = a * sg
                da = (ds * b * (sg * (1.0 + a * (1.0 - sg)))).astype(MXU)
                db = (ds * sa).astype(MXU)
                da_ref[rows, cols] = da.astype(da_ref.dtype)
                db_ref[rows, cols] = db.astype(db_ref.dtype)
                s_ref[rows, cols] = (sa * b).astype(s_ref.dtype)
                part = _mm(da, wg_ref[cols, :]) + _mm(db, wu_ref[cols, :])
                dx = part if dx is None else dx + part
            acc_ref[rows, :] += dx

        @pl.when(j == nj - 1)
        def _():
            dh, dg = _rms_bwd(acc_ref[...], h_ref[...], g_ref[...])
            dh_ref[...] = dout_ref[...] + dh
            dgain_ref[...] += dg

    tok = pl.BlockSpec((tm, D), lambda i, j: (i, 0))
    act = pl.BlockSpec((tm, FBX), lambda i, j: (i, j))
    wspec = pl.BlockSpec((FBX, D), lambda i, j: (j, 0))
    return pl.pallas_call(
        body, name="ffn_bwd_x", grid=(t // tm, nj),
        in_specs=[tok, tok, pl.BlockSpec((1, D), lambda i, j: (0, 0)), act, act, wspec, wspec, wspec, _ANY],
        out_specs=[tok, pl.BlockSpec((1, D), lambda i, j: (0, 0)), tok, act, act, act],
        out_shape=[jax.ShapeDtypeStruct((t, D), F32), jax.ShapeDtypeStruct((1, D), F32),
                   jax.ShapeDtypeStruct((t, D), SAVE)] + [jax.ShapeDtypeStruct((t, FF), SAVE)] * 3,
        scratch_shapes=[pltpu.VMEM((tm, D), F32)],
        compiler_params=_params(("arbitrary", "arbitrary")),
    )(dout, h, gain, a_sv, b_sv, wg, wu, wd, after)


def _ffn_bwd_w(xn, dy, da, db, s, tm):
    t = xn.shape[0]
    nt = t // tm
    nj = FF // FBX

    def body(xn_ref, dy_ref, da_ref, db_ref, s_ref, dwg_ref, dwu_ref, dwd_ref, ag_scr, au_scr, ad_scr):
        i = pl.program_id(1)

        @pl.when(i == 0)
        def _():
            for ref in (ag_scr, au_scr, ad_scr):
                ref[...] = jnp.zeros_like(ref)

        xn, dy = xn_ref[...], dy_ref[...]
        for c0 in range(0, FBX, FB):
            rows = slice(c0, min(c0 + FB, FBX))
            ag_scr[rows, :] += _mm_tn(da_ref[:, rows], xn)
            au_scr[rows, :] += _mm_tn(db_ref[:, rows], xn)
            ad_scr[rows, :] += _mm_tn(s_ref[:, rows], dy)

        @pl.when(i == nt - 1)
        def _():
            for out, ref in ((dwg_ref, ag_scr), (dwu_ref, au_scr), (dwd_ref, ad_scr)):
                out[...] = ref[...].astype(out.dtype)

    tok = pl.BlockSpec((tm, D), lambda j, i: (i, 0))
    act = pl.BlockSpec((tm, FBX), lambda j, i: (i, j))
    wspec = pl.BlockSpec((FBX, D), lambda j, i: (j, 0))
    return pl.pallas_call(
        body, name="ffn_bwd_w", grid=(nj, nt),
        in_specs=[tok, tok, act, act, act], out_specs=[wspec] * 3,
        out_shape=[jax.ShapeDtypeStruct((FF, D), WIRE)] * 3,
        scratch_shapes=[pltpu.VMEM((FBX, D), F32)] * 3,
        compiler_params=_params(("parallel", "arbitrary")),
    )(xn, dy, da, db, s)


def _inproj_fwd(h, gain, win, tm, after):
    t = h.shape[0]

    def body(h_ref, g_ref, w_ref, _, z_ref, xn_ref):
        hh = h_ref[...]
        rstd = lax.rsqrt(jnp.mean(hh * hh, axis=-1, keepdims=True) + EPS)
        xn = (hh * rstd * g_ref[...]).astype(MXU)
        xn_ref[...] = xn.astype(xn_ref.dtype)
        for j in range(DIN // ZB):
            z_ref[:, j * ZB:(j + 1) * ZB] = _mm_nt(xn, w_ref[j * ZB:(j + 1) * ZB, :])

    return pl.pallas_call(
        body, name="inproj_fwd", grid=(t // tm,),
        in_specs=[pl.BlockSpec((tm, D), lambda i: (i, 0)),
                  pl.BlockSpec((1, D), lambda i: (0, 0)),
                  pl.BlockSpec((DIN, D), lambda i: (0, 0)), _ANY],
        out_specs=[pl.BlockSpec((tm, DIN), lambda i: (i, 0)),
                   pl.BlockSpec((tm, D), lambda i: (i, 0))],
        out_shape=[jax.ShapeDtypeStruct((t, DIN), F32), jax.ShapeDtypeStruct((t, D), SAVE)],
        compiler_params=_params(("parallel",)),
    )(h, gain, win, after)


def _inproj_bwd_x(dres, dz, h, gain, win, tm, after):
    t = h.shape[0]

    def body(dres_ref, dz_ref, h_ref, g_ref, w_ref, _, dh_ref, dgain_ref):
        @pl.when(pl.program_id(0) == 0)
        def _():
            dgain_ref[...] = jnp.zeros_like(dgain_ref)

        dh, dg = _rms_bwd(_mm(dz_ref[...], w_ref[...]), h_ref[...], g_ref[...])
        dh_ref[...] = dres_ref[...] + dh
        dgain_ref[...] += dg

    return pl.pallas_call(
        body, name="inproj_bwd_x", grid=(t // tm,),
        in_specs=[pl.BlockSpec((tm, D), lambda i: (i, 0)),
                  pl.BlockSpec((tm, DIN), lambda i: (i, 0)),
                  pl.BlockSpec((tm, D), lambda i: (i, 0)),
                  pl.BlockSpec((1, D), lambda i: (0, 0)),
                  pl.BlockSpec((DIN, D), lambda i: (0, 0)), _ANY],
        out_specs=[pl.BlockSpec((tm, D), lambda i: (i, 0)),
                   pl.BlockSpec((1, D), lambda i: (0, 0))],
        out_shape=[jax.ShapeDtypeStruct((t, D), F32), jax.ShapeDtypeStruct((1, D), F32)],
        compiler_params=_params(("arbitrary",)),
    )(dres, dz, h, gain, win, after)


def _inproj_bwd_w(xn, dz, tm):
    t = xn.shape[0]
    nt = t // tm

    def body(xn_ref, dz_ref, dw_ref, acc_scr):
        i = pl.program_id(1)

        @pl.when(i == 0)
        def _():
            acc_scr[...] = jnp.zeros_like(acc_scr)

        acc_scr[...] += _mm_tn(dz_ref[...], xn_ref[...])

        @pl.when(i == nt - 1)
        def _():
            dw_ref[...] = acc_scr[...].astype(dw_ref.dtype)

    return pl.pallas_call(
        body, name="inproj_bwd_w", grid=(DIN // ZBW, nt),
        in_specs=[pl.BlockSpec((tm, D), lambda j, i: (i, 0)),
                  pl.BlockSpec((tm, ZBW), lambda j, i: (i, j))],
        out_specs=pl.BlockSpec((ZBW, D), lambda j, i: (j, 0)),
        out_shape=jax.ShapeDtypeStruct((DIN, D), WIRE),
        scratch_shapes=[pltpu.VMEM((ZBW, D), F32)],
        compiler_params=_params(("parallel", "arbitrary")),
    )(xn, dz)


def _outproj_fwd(h, oa, ob, oc, wout, tm):
    t = h.shape[0]

    def body(h_ref, oa_ref, ob_ref, oc_ref, w_ref, out_ref):
        ym = jnp.concatenate([oa_ref[...], ob_ref[...], oc_ref[...]], axis=1)
        out_ref[...] = h_ref[...] + _mm(ym, w_ref[...])

    return pl.pallas_call(
        body, name="outproj_fwd", grid=(t // tm,),
        in_specs=[pl.BlockSpec((tm, D), lambda i: (i, 0)),
                  pl.BlockSpec((tm, DA), lambda i: (i, 0)),
                  pl.BlockSpec((tm, DB), lambda i: (i, 0)),
                  pl.BlockSpec((tm, DC), lambda i: (i, 0)),
                  pl.BlockSpec((D, D), lambda i: (0, 0))],
        out_specs=pl.BlockSpec((tm, D), lambda i: (i, 0)),
        out_shape=jax.ShapeDtypeStruct((t, D), F32),
        compiler_params=_params(("parallel",)),
    )(h, oa, ob, oc, wout)


def _outproj_bwd(dh, oa, ob, oc, wout, tm):
    t = dh.shape[0]
    nt = t // tm

    def body(dh_ref, oa_ref, ob_ref, oc_ref, w_ref, da_ref, db_ref, dc_ref, dw_ref, acc_scr):
        i = pl.program_id(0)

        @pl.when(i == 0)
        def _():
            acc_scr[...] = jnp.zeros_like(acc_scr)

        d16 = dh_ref[...].astype(MXU)
        dym = _mm_nt(d16, w_ref[...])
        da_ref[...] = dym[:, :DA]
        db_ref[...] = dym[:, DA:DA + DB]
        dc_ref[...] = dym[:, DA + DB:]
        ym = jnp.concatenate([oa_ref[...], ob_ref[...], oc_ref[...]], axis=1)
        acc_scr[...] += _mm_tn(ym, d16)

        @pl.when(i == nt - 1)
        def _():
            dw_ref[...] = acc_scr[...].astype(dw_ref.dtype)

    return pl.pallas_call(
        body, name="outproj_bwd", grid=(nt,),
        in_specs=[pl.BlockSpec((tm, D), lambda i: (i, 0)),
                  pl.BlockSpec((tm, DA), lambda i: (i, 0)),
                  pl.BlockSpec((tm, DB), lambda i: (i, 0)),
                  pl.BlockSpec((tm, DC), lambda i: (i, 0)),
                  pl.BlockSpec((D, D), lambda i: (0, 0))],
        out_specs=[pl.BlockSpec((tm, DA), lambda i: (i, 0)),
                   pl.BlockSpec((tm, DB), lambda i: (i, 0)),
                   pl.BlockSpec((tm, DC), lambda i: (i, 0)),
                   pl.BlockSpec((D, D), lambda i: (0, 0))],
        out_shape=[jax.ShapeDtypeStruct((t, DA), F32), jax.ShapeDtypeStruct((t, DB), F32),
                   jax.ShapeDtypeStruct((t, DC), F32), jax.ShapeDtypeStruct((D, D), WIRE)],
        scratch_shapes=[pltpu.VMEM((D, D), F32)],
        compiler_params=_params(("arbitrary",)),
    )(dh, oa, ob, oc, wout)


def _lower_bounds(logits):
    depth, n = logits.shape

    def body(l_ref, lb_ref, p_ref):
        rows = [l_ref[l:l + 1, :] for l in range(depth)]
        mx = functools.reduce(jnp.maximum, rows)
        ex = [jnp.exp(r - mx) for r in rows]
        den = functools.reduce(lambda u, v: u + v, ex)
        acc = jnp.zeros_like(den)
        for l in range(depth):
            p = ex[l] / den
            p_ref[l:l + 1, :] = p
            if l > 0:
                acc = acc + p
            lb_ref[l:l + 1, :] = acc

    return pl.pallas_call(
        body, name="lower_bounds",
        out_shape=[jax.ShapeDtypeStruct((depth, n), F32), jax.ShapeDtypeStruct((depth, n), F32)],
    )(logits)


def _lower_bounds_bwd(p, dlb):
    depth, n = p.shape

    def body(p_ref, d_ref, out_ref):
        ps = [p_ref[l:l + 1, :] for l in range(depth)]
        ds = [d_ref[l:l + 1, :] for l in range(depth)]
        dp = [jnp.zeros_like(ps[0]) for _ in range(depth)]
        run = jnp.zeros_like(ps[0])
        for l in range(depth - 1, 0, -1):
            run = run + ds[l]
            dp[l] = run
        dot = functools.reduce(lambda u, v: u + v, [ps[l] * dp[l] for l in range(depth)])
        for l in range(depth):
            out_ref[l:l + 1, :] = ps[l] * (dp[l] - dot)

    return pl.pallas_call(body, name="lower_bounds_bwd", out_shape=jax.ShapeDtypeStruct((depth, n), F32))(p, dlb)


def _hgrn_block(z_ref, lb_ref, rb):
    q, fl = z_ref[:, 0:DA], z_ref[:, DA:2 * DA]
    lb = lb_ref[...]
    sq = _sigmoid(q)
    qs = q * sq
    sg = _sigmoid(fl)
    f = lb + (1.0 - lb) * sg
    k = 1.0 - f
    lf = jnp.log(f)
    row, col = _iota((rb, rb), 0), _iota((rb, rb), 1)
    same = (row // ACH) == (col // ACH)
    causal = same & (row >= col)
    b = _mm_exact_l(jnp.where(causal, 1.0, 0.0).astype(MXU), lf)
    bend = _mm_exact_l(jnp.where(same, 1.0, 0.0).astype(MXU), lf)
    r = 0.5 * bend
    eq, ek, eb, ed = jnp.exp(b - r), jnp.exp(r - b), jnp.exp(b), jnp.exp(bend - b)
    return dict(q=q, lb=lb, sq=sq, sg=sg, f=f, bend=bend, eq=eq, ek=ek, eb=eb, ed=ed,
                qt=qs * eq, kt=k * ek, qe=qs * eb, kd=k * ed, same=same, causal=causal)


def _hgrn_fwd(z, lb, gain):
    t = z.shape[0]
    nc = t // ACH
    cb = min(ACB, nc)
    rb = cb * ACH

    def body(z_ref, lb_ref, g_ref, o_ref, oa_ref, st_ref, st_scr):
        @pl.when(pl.program_id(0) == 0)
        def _():
            st_scr[...] = jnp.zeros_like(st_scr)

        c = _hgrn_block(z_ref, lb_ref, rb)
        for hd in range(NH):
            cols = slice(hd * HD, (hd + 1) * HD)
            v = z_ref[:, 2 * DA + hd * HD:2 * DA + (hd + 1) * HD]
            gg = z_ref[:, 3 * DA + hd * HD:3 * DA + (hd + 1) * HD]
            att = jnp.where(c["causal"], _mm_nt(c["qt"][:, cols], c["kt"][:, cols]), 0.0)
            o_in = _mm(att, v)
            qe, kd, bend = c["qe"][:, cols], c["kd"][:, cols], c["bend"][:, cols]
            st = st_scr[hd]
            outs = []
            for cc in range(cb):
                rows = slice(cc * ACH, (cc + 1) * ACH)
                st_ref[cc, hd] = st
                outs.append(o_in[rows] + _mm_nt(qe[rows], st))
                decay = jnp.exp(jnp.max(bend[rows], axis=0, keepdims=True))
                st = st * decay + _mm_tn(v[rows], kd[rows])
            st_scr[hd] = st
            o = jnp.concatenate(outs, axis=0)
            o_ref[:, cols] = o
            rstd = lax.rsqrt(jnp.mean(o * o, axis=-1, keepdims=True) + EPS)
            oa_ref[:, cols] = (o * rstd * g_ref[:, cols] * (gg * _sigmoid(gg))).astype(oa_ref.dtype)

    return pl.pallas_call(
        body, name="hgrn_fwd", grid=(nc // cb,),
        in_specs=[pl.BlockSpec((rb, 4 * DA), lambda c: (c, 0)),
                  pl.BlockSpec((1, DA), lambda c: (0, 0)),
                  pl.BlockSpec((1, DA), lambda c: (0, 0))],
        out_specs=[pl.BlockSpec((rb, DA), lambda c: (c, 0)),
                   pl.BlockSpec((rb, DA), lambda c: (c, 0)),
                   pl.BlockSpec((cb, NH, HD, HD), lambda c: (c, 0, 0, 0))],
        out_shape=[jax.ShapeDtypeStruct((t, DA), F32), jax.ShapeDtypeStruct((t, DA), SAVE),
                   jax.ShapeDtypeStruct((nc, NH, HD, HD), F32)],
        scratch_shapes=[pltpu.VMEM((NH, HD, HD), F32)],
        compiler_params=_params(("arbitrary",)),
    )(z, lb, gain)


def _hgrn_bwd(z, lb, gain, o, states, doa, after):
    t = z.shape[0]
    nc = t // ACH
    cb = min(ACB, nc)
    rb = cb * ACH
    nblk = nc // cb

    def body(z_ref, lb_ref, g_ref, o_ref, st_ref, doa_ref, _, dz_ref, dgain_ref, dlb_ref, dst_scr):
        @pl.when(pl.program_id(0) == 0)
        def _():
            dst_scr[...] = jnp.zeros_like(dst_scr)
            dgain_ref[...] = jnp.zeros_like(dgain_ref)
            dlb_ref[...] = jnp.zeros_like(dlb_ref)

        c = _hgrn_block(z_ref, lb_ref, rb)
        dbs, dqss, dks = [], [], []
        for hd in range(NH):
            cols = slice(hd * HD, (hd + 1) * HD)
            v = z_ref[:, 2 * DA + hd * HD:2 * DA + (hd + 1) * HD]
            gg = z_ref[:, 3 * DA + hd * HD:3 * DA + (hd + 1) * HD]
            qt, kt, qe, kd, bend = (c[n][:, cols] for n in ("qt", "kt", "qe", "kd", "bend"))
            o = o_ref[:, cols]
            do_a = doa_ref[:, cols]
            gain = g_ref[:, cols]
            sgg = _sigmoid(gg)
            silu_g = gg * sgg
            rstd = lax.rsqrt(jnp.mean(o * o, axis=-1, keepdims=True) + EPS)
            n = o * rstd
            dn = do_a * gain * silu_g
            dg = do_a * n * gain * (sgg * (1.0 + gg * (1.0 - sgg)))
            dgain_ref[:, cols] += jnp.sum(do_a * silu_g * n, axis=0, keepdims=True)
            d_o = rstd * (dn - n * jnp.mean(dn * n, axis=-1, keepdims=True))

            att = jnp.where(c["causal"], _mm_nt(qt, kt), 0.0)
            datt = jnp.where(c["causal"], _mm_nt(d_o, v), 0.0)
            dv_in = _mm_tn(att, d_o)
            dqt = _mm(datt, kt)
            dkt = _mm_tn(datt, qt)
            dsp = dst_scr[hd]
            dvs, dqes, dkds, dbends = [None] * cb, [None] * cb, [None] * cb, [None] * cb
            for cc in reversed(range(cb)):
                rows = slice(cc * ACH, (cc + 1) * ACH)
                st = st_ref[cc, hd]
                dvs[cc] = dv_in[rows] + _mm_nt(kd[rows], dsp)
                dqes[cc] = _mm(d_o[rows], st)
                dkds[cc] = _mm(v[rows], dsp)
                decay = jnp.exp(jnp.max(bend[rows], axis=0, keepdims=True))
                dbend = (decay * jnp.sum(st * dsp, axis=0, keepdims=True)
                         + jnp.sum(dkds[cc] * kd[rows], axis=0, keepdims=True))
                dbends[cc] = jnp.broadcast_to(dbend, (ACH, HD))
                dsp = dsp * decay + _mm_tn(d_o[rows], qe[rows])
            dst_scr[hd] = dsp
            dv, dqe, dkd, dbend = (jnp.concatenate(p, axis=0) for p in (dvs, dqes, dkds, dbends))
            dbs.append((dqt * qt + dqe * qe - dkt * kt - dkd * kd, dbend))
            dqss.append(dqt * c["eq"][:, cols] + dqe * c["eb"][:, cols])
            dks.append(dkt * c["ek"][:, cols] + dkd * c["ed"][:, cols])
            c0 = hd * HD
            dz_ref[:, 2 * DA + c0:2 * DA + c0 + HD] = dv.astype(dz_ref.dtype)
            dz_ref[:, 3 * DA + c0:3 * DA + c0 + HD] = dg.astype(dz_ref.dtype)

        db = jnp.concatenate([p[0] for p in dbs], axis=1)
        dbend = jnp.concatenate([p[1] for p in dbs], axis=1)
        dqs, dk = jnp.concatenate(dqss, axis=1), jnp.concatenate(dks, axis=1)
        row, col = _iota((rb, rb), 0), _iota((rb, rb), 1)
        upper = jnp.where(c["same"] & (row <= col), 1.0, 0.0).astype(MXU)
        dlf = _mm_exact_l(upper, db) + dbend
        df = dlf / c["f"] - dk
        sg, sq, q = c["sg"], c["sq"], c["q"]
        dlb_ref[...] += jnp.sum(df * (1.0 - sg), axis=0, keepdims=True)
        dz_ref[:, DA:2 * DA] = (df * (1.0 - c["lb"]) * sg * (1.0 - sg)).astype(dz_ref.dtype)
        dz_ref[:, 0:DA] = (dqs * (sq * (1.0 + q * (1.0 - sq)))).astype(dz_ref.dtype)

    rev = lambda c: (nblk - 1 - c, 0)
    return pl.pallas_call(
        body, name="hgrn_bwd", grid=(nblk,),
        in_specs=[pl.BlockSpec((rb, 4 * DA), rev),
                  pl.BlockSpec((1, DA), lambda c: (0, 0)),
                  pl.BlockSpec((1, DA), lambda c: (0, 0)),
                  pl.BlockSpec((rb, DA), rev),
                  pl.BlockSpec((cb, NH, HD, HD), lambda c: (nblk - 1 - c, 0, 0, 0)),
                  pl.BlockSpec((rb, DA), rev), _ANY],
        out_specs=[pl.BlockSpec((rb, 4 * DA), rev),
                   pl.BlockSpec((1, DA), lambda c: (0, 0)),
                   pl.BlockSpec((1, DA), lambda c: (0, 0))],
        out_shape=[jax.ShapeDtypeStruct((t, DIN), SAVE), jax.ShapeDtypeStruct((1, DA), F32),
                   jax.ShapeDtypeStruct((1, DA), F32)],
        scratch_shapes=[pltpu.VMEM((NH, HD, HD), F32)],
        compiler_params=_params(("arbitrary",)),
    )(z, lb, gain, o, states, doa, after)


def _shift_down(prev8, x, k):
    cat = jnp.concatenate([prev8, x], axis=0)
    return pltpu.roll(cat, k, axis=0)[8:, :]


def _shift_up(x, next8, k):
    n = x.shape[0]
    cat = jnp.concatenate([x, next8], axis=0)
    return pltpu.roll(cat, n + 8 - k, axis=0)[:n, :]


def _lru_gates(x, prev8, cw_ref, vec_ref, wa_ref, wx_ref):
    xs = [x, _shift_down(prev8, x, 1), _shift_down(prev8, x, 2), _shift_down(prev8, x, 3)]
    xc = vec_ref[0:1, :] + cw_ref[3:4, :] * xs[0] + cw_ref[2:3, :] * xs[1] + cw_ref[1:2, :] * xs[2] + cw_ref[0:1, :] * xs[3]
    r = _sigmoid(_mm(xc, wa_ref[...]) + vec_ref[1:2, :])
    gi = _sigmoid(_mm(xc, wx_ref[...]) + vec_ref[2:3, :])
    lam = vec_ref[3:4, :]
    sp = jnp.maximum(-lam, 0.0) + jnp.log(1.0 + jnp.exp(-jnp.abs(lam)))
    la = -LRU_C * r * sp
    a = jnp.exp(la)
    mult = jnp.sqrt(-_expm1(2.0 * la))
    return xs, xc, r, gi, sp, a, mult


def _scan_down(a, u):
    n = a.shape[0]
    row = _iota(a.shape, 0)
    s = 1
    while s < n:
        keep = row >= s
        ash = jnp.where(keep, pltpu.roll(a, s, axis=0), 1.0)
        ush = jnp.where(keep, pltpu.roll(u, s, axis=0), 0.0)
        u = a * ush + u
        a = a * ash
        s *= 2
    return a, u


def _scan_up(a, u):
    n = a.shape[0]
    row = _iota(a.shape, 0)
    s = 1
    while s < n:
        keep = row < n - s
        ash = jnp.where(keep, pltpu.roll(a, n - s, axis=0), 1.0)
        ush = jnp.where(keep, pltpu.roll(u, n - s, axis=0), 0.0)
        u = a * ush + u
        a = a * ash
        s *= 2
    return a, u


def _lru_fwd(z, cw, vec, wa, wx, tb):
    t = z.shape[0]
    xcol, gcol = (4 * DA) // DB, (4 * DA) // DB + 1

    def body(x_ref, gate_ref, cw_ref, vec_ref, wa_ref, wx_ref, ob_ref, h_ref, xprev_scr, hc_scr):
        @pl.when(pl.program_id(0) == 0)
        def _():
            xprev_scr[...] = jnp.zeros_like(xprev_scr)
            hc_scr[...] = jnp.zeros_like(hc_scr)

        x = x_ref[...]
        _, xc, _, gi, _, a, mult = _lru_gates(x, xprev_scr[...], cw_ref, vec_ref, wa_ref, wx_ref)
        acum, hloc = _scan_down(a, mult * gi * xc)
        h = hloc + acum * hc_scr[0:1, :]
        h_ref[...] = h
        hc_scr[...] = jnp.broadcast_to(_row(h, tb - 1), hc_scr.shape)
        xprev_scr[...] = x[tb - 8:, :]
        y = h * _gelu(gate_ref[...])
        ms = _mm_exact_r(y * y, _group_matrix(DB, 1.0 / GRP).astype(MXU))
        ob_ref[...] = (y * lax.rsqrt(ms + EPS) * vec_ref[4:5, :]).astype(ob_ref.dtype)

    return pl.pallas_call(
        body, name="lru_fwd", grid=(t // tb,),
        in_specs=[pl.BlockSpec((tb, DB), lambda i: (i, xcol)),
                  pl.BlockSpec((tb, DB), lambda i: (i, gcol)),
                  pl.BlockSpec((8, DB), lambda i: (0, 0)),
                  pl.BlockSpec((8, DB), lambda i: (0, 0)),
                  pl.BlockSpec((DB, DB), lambda i: (0, 0)),
                  pl.BlockSpec((DB, DB), lambda i: (0, 0))],
        out_specs=[pl.BlockSpec((tb, DB), lambda i: (i, 0)),
                   pl.BlockSpec((tb, DB), lambda i: (i, 0))],
        out_shape=[jax.ShapeDtypeStruct((t, DB), SAVE), jax.ShapeDtypeStruct((t, DB), F32)],
        scratch_shapes=[pltpu.VMEM((8, DB), F32), pltpu.VMEM((8, DB), F32)],
        compiler_params=_params(("arbitrary",)),
    )(z, z, cw, vec, wa, wx)


def _lru_bwd(z, hseq, dob, cw, vec, wa, wx, dz, tb):
    t = z.shape[0]
    nb = t // tb
    xcol, gcol = (4 * DA) // DB, (4 * DA) // DB + 1
    per = tb // 8

    def body(x_ref, xh_ref, gate_ref, h_ref, hh_ref, dob_ref, cw_ref, vec_ref, wa_ref, wx_ref, _,
             dz_ref, dcw_ref, dvec_ref, dwa_ref, dwx_ref, gc_scr, an_scr, dxc_scr):
        step = pl.program_id(0)
        blk = nb - 1 - step

        @pl.when(step == 0)
        def _():
            for ref in (gc_scr, an_scr, dxc_scr, dcw_ref, dvec_ref, dwa_ref, dwx_ref):
                ref[...] = jnp.zeros_like(ref)

        first = (blk > 0).astype(F32)
        x = x_ref[...]
        xs, xc, r, gi, sp, a, mult = _lru_gates(x, xh_ref[...] * first, cw_ref, vec_ref, wa_ref, wx_ref)
        h = h_ref[...]
        hprev = _shift_down(hh_ref[...] * first, h, 1)
        ge, dge = _gelu_and_grad(gate_ref[...])
        y = h * ge
        gmat = _group_matrix(DB, 1.0 / GRP).astype(MXU)
        rstd = lax.rsqrt(_mm_exact_r(y * y, gmat) + EPS)
        n = y * rstd
        d_ob = dob_ref[...]
        dn = d_ob * vec_ref[4:5, :]
        dvec_ref[4:5, :] += jnp.sum(d_ob * n, axis=0, keepdims=True)
        dy = rstd * (dn - n * _mm_exact_r(dn * n, gmat))
        dh = dy * ge
        dgate = dy * h * dge

        row = _iota(a.shape, 0)
        anext = jnp.where(row == tb - 1, an_scr[0:1, :], pltpu.roll(a, tb - 1, axis=0))
        acum, gloc = _scan_up(anext, dh)
        g = gloc + acum * gc_scr[0:1, :]
        gc_scr[...] = jnp.broadcast_to(_row(g, 0), gc_scr.shape)
        an_scr[...] = jnp.broadcast_to(_row(a, 0), an_scr.shape)

        da = g * hprev
        dmult = g * gi * xc
        dgi = g * mult * xc
        dxc = g * mult * gi
        dla = da * a - dmult * (a * a) / mult
        dr = dla * (-LRU_C * sp)
        dsp = jnp.sum(dla * (-LRU_C * r), axis=0, keepdims=True)
        lam = vec_ref[3:4, :]
        dvec_ref[3:4, :] += -dsp * _sigmoid(-lam)
        dpa = dr * r * (1.0 - r)
        dpx = dgi * gi * (1.0 - gi)
        dwa_ref[...] += _mm_tn(xc, dpa)
        dwx_ref[...] += _mm_tn(xc, dpx)
        dvec_ref[1:2, :] += jnp.sum(dpa, axis=0, keepdims=True)
        dvec_ref[2:3, :] += jnp.sum(dpx, axis=0, keepdims=True)
        dxc = dxc + _mm_nt(dpa, wa_ref[...]) + _mm_nt(dpx, wx_ref[...])
        dvec_ref[0:1, :] += jnp.sum(dxc, axis=0, keepdims=True)
        for tap in range(4):
            dcw_ref[tap:tap + 1, :] += jnp.sum(dxc * xs[3 - tap], axis=0, keepdims=True)
        nxt = dxc_scr[...]
        dx = (cw_ref[3:4, :] * dxc + cw_ref[2:3, :] * _shift_up(dxc, nxt, 1)
              + cw_ref[1:2, :] * _shift_up(dxc, nxt, 2) + cw_ref[0:1, :] * _shift_up(dxc, nxt, 3))
        dxc_scr[...] = dxc[:8, :]
        dz_ref[:, :DB] = dx.astype(dz_ref.dtype)
        dz_ref[:, DB:] = dgate.astype(dz_ref.dtype)

    def halo(col):
        return lambda s: (jnp.maximum((nb - 1 - s) * per - 1, 0), col)

    const = lambda s: (0, 0)
    return pl.pallas_call(
        body, name="lru_bwd", grid=(nb,),
        in_specs=[pl.BlockSpec((tb, DB), lambda s: (nb - 1 - s, xcol)),
                  pl.BlockSpec((8, DB), halo(xcol)),
                  pl.BlockSpec((tb, DB), lambda s: (nb - 1 - s, gcol)),
                  pl.BlockSpec((tb, DB), lambda s: (nb - 1 - s, 0)),
                  pl.BlockSpec((8, DB), halo(0)),
                  pl.BlockSpec((tb, DB), lambda s: (nb - 1 - s, 0)),
                  pl.BlockSpec((8, DB), const), pl.BlockSpec((8, DB), const),
                  pl.BlockSpec((DB, DB), const), pl.BlockSpec((DB, DB), const),
                  pl.BlockSpec(memory_space=pl.ANY)],
        out_specs=[pl.BlockSpec((tb, 2 * DB), lambda s: (nb - 1 - s, (4 * DA) // (2 * DB))),
                   pl.BlockSpec((8, DB), const), pl.BlockSpec((8, DB), const),
                   pl.BlockSpec((DB, DB), const), pl.BlockSpec((DB, DB), const)],
        out_shape=[jax.ShapeDtypeStruct((t, DIN), SAVE), jax.ShapeDtypeStruct((8, DB), F32),
                   jax.ShapeDtypeStruct((8, DB), F32), jax.ShapeDtypeStruct((DB, DB), F32),
                   jax.ShapeDtypeStruct((DB, DB), F32)],
        scratch_shapes=[pltpu.VMEM((8, DB), F32), pltpu.VMEM((8, DB), F32), pltpu.VMEM((8, DB), F32)],
        input_output_aliases={10: 0},
        compiler_params=_params(("arbitrary",)),
    )(z, z, z, hseq, hseq, dob, cw, vec, wa, wx, dz)


def _sgu_block(u_ref, v_ref, w_ref, b_ref, gmat, tb):
    uu, duu = _gelu_and_grad(u_ref[...])
    vv, dvv = _gelu_and_grad(v_ref[...])
    dlt = vv - _mm_exact_r(vv, gmat)
    rstd_v = lax.rsqrt(_mm_exact_r(dlt * dlt, gmat) + EPS)
    vn = dlt * rstd_v
    col = _iota((CCH, DC), 1) // GRP
    causal = _iota((CCH, CCH), 0) >= _iota((CCH, CCH), 1)
    ws = [jnp.where(causal, w_ref[g], 0.0) for g in range(DC // GRP)]
    zs = []
    for ch in range(tb // CCH):
        vn_c = vn[ch * CCH:(ch + 1) * CCH]
        zz = b_ref[...]
        for g, w in enumerate(ws):
            zz = zz + jnp.where(col == g, _mm(w, vn_c), 0.0)
        zs.append(zz)
    return uu, duu, dvv, rstd_v, vn, jnp.concatenate(zs, axis=0), ws, col, causal


def _sgu_fwd(z, w, bias, gain, tb):
    t = z.shape[0]
    ucol, vcol = (4 * DA + 2 * DB) // DC, (4 * DA + 2 * DB) // DC + 1

    def body(u_ref, v_ref, w_ref, b_ref, g_ref, oc_ref):
        gmat = _group_matrix(DC, 1.0 / GRP).astype(MXU)
        uu, _, _, _, _, zz, _, _, _ = _sgu_block(u_ref, v_ref, w_ref, b_ref, gmat, tb)
        y = uu * zz
        ms = _mm_exact_r(y * y, gmat)
        oc_ref[...] = (y * lax.rsqrt(ms + EPS) * g_ref[...]).astype(oc_ref.dtype)

    const = lambda i: (0, 0)
    return pl.pallas_call(
        body, name="sgu_fwd", grid=(t // tb,),
        in_specs=[pl.BlockSpec((tb, DC), lambda i: (i, ucol)),
                  pl.BlockSpec((tb, DC), lambda i: (i, vcol)),
                  pl.BlockSpec((DC // GRP, CCH, CCH), lambda i: (0, 0, 0)),
                  pl.BlockSpec((CCH, DC), const), pl.BlockSpec((1, DC), const)],
        out_specs=pl.BlockSpec((tb, DC), lambda i: (i, 0)),
        out_shape=jax.ShapeDtypeStruct((t, DC), SAVE),
        compiler_params=_params(("parallel",)),
    )(z, z, w, bias, gain)


def _sgu_bwd(z, doc, w, bias, gain, dz, tb):
    t = z.shape[0]
    nb = t // tb
    ucol, vcol = (4 * DA + 2 * DB) // DC, (4 * DA + 2 * DB) // DC + 1
    ng = DC // GRP

    def body(u_ref, v_ref, doc_ref, w_ref, b_ref, g_ref, _, dz_ref, dw_ref, dbias_ref, dgain_ref, dbsum_scr):
        i = pl.program_id(0)

        @pl.when(i == 0)
        def _():
            for ref in (dw_ref, dgain_ref, dbsum_scr):
                ref[...] = jnp.zeros_like(ref)

        gmat = _group_matrix(DC, 1.0 / GRP).astype(MXU)
        uu, duu, dvv, rstd_v, vn, zz, ws, col, causal = _sgu_block(u_ref, v_ref, w_ref, b_ref, gmat, tb)
        y = uu * zz
        rstd = lax.rsqrt(_mm_exact_r(y * y, gmat) + EPS)
        n = y * rstd
        d_oc = doc_ref[...]
        dn = d_oc * g_ref[...]
        dgain_ref[0:1, :] += jnp.sum(d_oc * n, axis=0, keepdims=True)
        dy = rstd * (dn - n * _mm_exact_r(dn * n, gmat))
        dzz = dy * uu
        dz_ref[:, :DC] = (dy * zz * duu).astype(dz_ref.dtype)
        dvns = []
        for ch in range(tb // CCH):
            rows = slice(ch * CCH, (ch + 1) * CCH)
            dzz_c, vn_c = dzz[rows], vn[rows]
            dbsum_scr[...] += dzz_c
            dvn = jnp.zeros_like(dzz_c)
            for g in range(ng):
                sel = col == g
                dvn = dvn + jnp.where(sel, _mm_tn(ws[g], dzz_c), 0.0)
                dw_ref[g] += jnp.where(causal, _mm_nt(jnp.where(sel, dzz_c, 0.0), vn_c), 0.0)
            dvns.append(dvn)
        dvn = jnp.concatenate(dvns, axis=0)
        dv = rstd_v * (dvn - _mm_exact_r(dvn, gmat) - vn * _mm_exact_r(dvn * vn, gmat))
        dz_ref[:, DC:] = (dv * dvv).astype(dz_ref.dtype)

        @pl.when(i == nb - 1)
        def _():
            dbias_ref[...] = _mm_exact_r(dbsum_scr[...], _group_matrix(DC, 1.0).astype(MXU))

    const = lambda i: (0, 0)
    return pl.pallas_call(
        body, name="sgu_bwd", grid=(nb,),
        in_specs=[pl.BlockSpec((tb, DC), lambda i: (i, ucol)),
                  pl.BlockSpec((tb, DC), lambda i: (i, vcol)),
                  pl.BlockSpec((tb, DC), lambda i: (i, 0)),
                  pl.BlockSpec((ng, CCH, CCH), lambda i: (0, 0, 0)),
                  pl.BlockSpec((CCH, DC), const), pl.BlockSpec((1, DC), const),
                  pl.BlockSpec(memory_space=pl.ANY)],
        out_specs=[pl.BlockSpec((tb, 2 * DC), lambda i: (i, (4 * DA + 2 * DB) // (2 * DC))),
                   pl.BlockSpec((ng, CCH, CCH), lambda i: (0, 0, 0)),
                   pl.BlockSpec((CCH, DC), const), pl.BlockSpec((8, DC), const)],
        out_shape=[jax.ShapeDtypeStruct((t, DIN), SAVE), jax.ShapeDtypeStruct((ng, CCH, CCH), F32),
                   jax.ShapeDtypeStruct((CCH, DC), F32), jax.ShapeDtypeStruct((8, DC), F32)],
        scratch_shapes=[pltpu.VMEM((CCH, DC), F32)],
        input_output_aliases={6: 0},
        compiler_params=_params(("arbitrary",)),
    )(z, z, doc, w, bias, gain, dz)


def _head(h, gain, target, tm):
    t = h.shape[0]

    def body(h_ref, g_ref, t_ref, dh_ref, loss_ref, dgain_ref):
        @pl.when(pl.program_id(0) == 0)
        def _():
            loss_ref[...] = jnp.zeros_like(loss_ref)
            dgain_ref[...] = jnp.zeros_like(dgain_ref)

        hh = h_ref[...]
        gain = g_ref[...]
        rstd = lax.rsqrt(jnp.mean(hh * hh, axis=-1, keepdims=True) + EPS)
        xhat = hh * rstd
        err = xhat * gain - t_ref[...]
        per_tok = jnp.mean(err * err, axis=-1, keepdims=True)
        loss_ref[...] += 0.5 * jnp.sum(per_tok, axis=0, keepdims=True)
        dy = err * (1.0 / D)
        dgain_ref[...] += jnp.sum(dy * xhat, axis=0, keepdims=True)
        dxh = dy * gain
        dh_ref[...] = rstd * (dxh - xhat * jnp.mean(dxh * xhat, axis=-1, keepdims=True))

    return pl.pallas_call(
        body, name="head", grid=(t // tm,),
        in_specs=[pl.BlockSpec((tm, D), lambda i: (i, 0)),
                  pl.BlockSpec((1, D), lambda i: (0, 0)),
                  pl.BlockSpec((tm, D), lambda i: (i, 0))],
        out_specs=[pl.BlockSpec((tm, D), lambda i: (i, 0)),
                   pl.BlockSpec((1, 128), lambda i: (0, 0)),
                   pl.BlockSpec((1, D), lambda i: (0, 0))],
        out_shape=[jax.ShapeDtypeStruct((t, D), F32), jax.ShapeDtypeStruct((1, 128), F32),
                   jax.ShapeDtypeStruct((1, D), F32)],
        compiler_params=_params(("arbitrary",)),
    )(h, gain, target)


def _adamw(w, g, m, v):
    m = ADAM_B1 * m + (1.0 - ADAM_B1) * g
    v = ADAM_B2 * v + (1.0 - ADAM_B2) * (g * g)
    m_hat = m / (1.0 - ADAM_B1 ** ADAM_STEP)
    v_hat = v / (1.0 - ADAM_B2 ** ADAM_STEP)
    delta = -ADAM_LR * (m_hat / (jnp.sqrt(v_hat) + ADAM_EPS) + ADAM_WD * w)
    return delta, m, v


def _adamw_big(recv, w, m, v, tr, name, after, transposed=False):
    depth, rows, cols = w.shape
    rspec = (pl.BlockSpec((NDEV, cols, tr), lambda i: (0, 0, i)) if transposed
             else pl.BlockSpec((NDEV, tr, cols), lambda i: (0, i, 0)))

    def body(*refs):
        r_refs = refs[:depth]
        w_ref, m_ref, v_ref, _, g_out, d_out, m_out, v_out = refs[depth:]
        for l in range(depth):
            g = r_refs[l][0].astype(F32)
            for k in range(1, NDEV):
                g = g + r_refs[l][k].astype(F32)
            if transposed:
                g = g.T
            delta, m_, v_ = _adamw(w_ref[l], g, m_ref[l], v_ref[l])
            g_out[l] = g
            d_out[l] = delta
            m_out[l] = m_
            v_out[l] = v_

    spec = pl.BlockSpec((depth, tr, cols), lambda i: (0, i, 0))
    return pl.pallas_call(
        body, name=name, grid=(rows // tr,),
        in_specs=[rspec] * depth + [spec] * 3
        + [pl.BlockSpec(memory_space=pl.ANY)],
        out_specs=[spec] * 4, out_shape=[jax.ShapeDtypeStruct((depth, rows, cols), F32)] * 4,
        compiler_params=_params(("parallel",)),
    )(*recv, w, m, v, after)


def _sum_devices(recv):
    _, r, _ = recv.shape

    def body(r_ref, out_ref):
        g = r_ref[0]
        for k in range(1, NDEV):
            g = g + r_ref[k]
        out_ref[...] = g

    return pl.pallas_call(body, name="sum_devices", out_shape=jax.ShapeDtypeStruct((r, 128), F32))(recv)


def _adamw_small(w, g, m, v):
    def body(w_ref, g_ref, m_ref, v_ref, d_out, m_out, v_out):
        delta, m_, v_ = _adamw(w_ref[...], g_ref[...], m_ref[...], v_ref[...])
        d_out[...] = delta
        m_out[...] = m_
        v_out[...] = v_

    return pl.pallas_call(body, name="adamw_small", out_shape=[jax.ShapeDtypeStruct(w.shape, F32)] * 3)(w, g, m, v)


def _pack(arrs):
    flat = jnp.concatenate([a.reshape(-1) for a in arrs])
    pad = (-flat.shape[0]) % 1024
    return jnp.pad(flat, (0, pad)).reshape(-1, 128)


def _unpack(buf, like):
    flat = buf.reshape(-1)
    out, off = [], 0
    for a in like:
        out.append(flat[off:off + a.size].reshape(a.shape))
        off += a.size
    return out


def _block_diag(w):
    nb, bd, _ = w.shape
    eye = jnp.eye(nb, dtype=w.dtype)
    return (eye[:, None, :, None] * w[:, :, None, :]).reshape(nb * bd, nb * bd)


def _diag_blocks(w):
    nb = w.shape[0] // GRP
    return jnp.stack([w[g * GRP:(g + 1) * GRP, g * GRP:(g + 1) * GRP] for g in range(nb)])


SMALL = ['ffn1_norm', 'mix_norm', 'hgrn_lb_logits', 'hgrn_norm', 'conv_b', 'lru_wa', 'lru_ba', 'lru_wx', 'lru_bx',
         'lru_lambda', 'lru_norm', 'sgu_w', 'sgu_b', 'sgu_norm', 'ffn2_norm', 'final_norm']
NAMES = ['ffn1_norm', 'ffn1_wg', 'ffn1_wu', 'ffn1_wd', 'mix_norm', 'w_in', 'hgrn_lb_logits', 'hgrn_norm', 'conv_w',
         'conv_b', 'lru_wa', 'lru_ba', 'lru_wx', 'lru_bx', 'lru_lambda', 'lru_norm', 'sgu_w', 'sgu_b', 'sgu_norm',
         'w_out', 'ffn2_norm', 'ffn2_wg', 'ffn2_wu', 'ffn2_wd', 'final_norm']


def _step(x, target, w, m, v):
    depth = w['ffn1_wg'].shape[0]
    t = x.shape[1]
    h = x.reshape(t, D)
    target = target.reshape(t, D)
    tm_f, tm_b, tb = min(TM_F, t), min(TM_B, t), min(TB, t)
    my = 4 * lax.axis_index("x") + 2 * lax.axis_index("y") + lax.axis_index("c")

    cw_tile = jnp.pad(w['conv_w'].reshape(-1, 128), ((0, 8 - depth), (0, 0)))
    lbs, lb_soft = _lower_bounds(w['hgrn_lb_logits'])

    def row(a):
        return a.reshape(1, -1)

    none = jnp.zeros((8, 128), F32)

    def tr(a):
        return jnp.swapaxes(a, -1, -2)

    def shards(l, unit, zero=None):
        def cast(a):
            return (a if zero is None else a + zero).astype(WIRE)

        if unit == 1:
            return [cast(tr(w['w_in'][l])), cast(w['w_out'][l])]
        f = 'ffn1' if unit == 0 else 'ffn2'
        return [cast(tr(w[f + '_wg'][l])), cast(tr(w[f + '_wu'][l])), cast(w[f + '_wd'][l])]

    units = [(l, u) for l in range(depth) for u in range(3)]

    def start_ici(idx, deps=()):
        return _transfer_start(ready[idx], True, "gather_ici_%d_%d" % units[idx], deps=deps)

    def relay(idx, handle, after, then=None):
        lands = _transfer_wait(handle, after, "gather_ici_wait_%d_%d" % units[idx])
        more = ready[then] if then is not None and then < len(units) else ()
        return _forward_start(lands, "gather_d2d_%d_%d" % units[idx], more)

    pipe = dict(idx=0)
    ready = [shards(0, 0)]
    first = start_ici(0)
    ready += [shards(l, u, first['token'][0, 0]) for l, u in units[1:]]
    pipe['ici'] = start_ici(1, deps=(first['token'],))
    zero = first['token'][0, 0]
    mixer_consts = []
    for l in range(depth):
        vec = jnp.concatenate([row(w['conv_b'][l]), row(w['lru_ba'][l]), row(w['lru_bx'][l]),
                               row(w['lru_lambda'][l]), row(w['lru_norm'][l]), jnp.zeros((3, DB), F32)]) + zero
        mixer_consts.append((vec, _block_diag(w['lru_wa'][l] + zero), _block_diag(w['lru_wx'][l] + zero),
                             jnp.repeat((w['sgu_b'][l] + zero).T, GRP, axis=1)))
    cast_all = [a for unit in ready[2:] for a in unit] + [a for consts in mixer_consts for a in consts]
    conv_flight = _transfer_start([cw_tile], True, "gather_conv_start", deps=(pipe['ici']['token'], *cast_all),
                                  direct=True)
    pipe['d2d'], _ = relay(0, first, conv_flight['token'])

    def next_weights(after):
        idx = pipe['idx']
        lands = _transfer_wait(pipe['d2d'], after, "gather_d2d_wait_%d_%d" % units[idx])
        pipe['idx'] = idx + 1
        tok = none
        if idx + 1 < len(units):
            pipe['d2d'], pipe['ici'] = relay(idx + 1, pipe['ici'], lands[-1], idx + 2)
            tok = pipe['d2d']['token']
        return lands, tok

    saved = []
    for l in range(depth):
        lands, tok = next_weights(h)
        s = dict(ffn1=[a.reshape(FF, D) for a in lands], h0=h)
        h, s['xn1'], s['a1'], s['b1'] = _ffn_fwd(h, row(w['ffn1_norm'][l]), *s['ffn1'], tm_f, tok)
        s['h1'] = h
        (win, wout), tok = next_weights(h)
        win, wout = win.reshape(DIN, D), wout.reshape(D, D)
        s['win'], s['wout'] = win, wout
        z, s['xnm'] = _inproj_fwd(h, row(w['mix_norm'][l]), win, tm_f, tok)
        s['z'] = z
        s['o'], oa, s['states'] = _hgrn_fwd(z, row(lbs[l]), row(w['hgrn_norm'][l]))
        if l == 0:
            cw_all = _transfer_wait(conv_flight, z, "gather_conv_wait")[0][:, :depth]
            conv_w = jnp.moveaxis(cw_all.reshape(NDEV, depth, 4, DB // NDEV), 0, 2).reshape(depth, 4, DB)
        s['cw'] = jnp.pad(conv_w[l], ((0, 4), (0, 0)))
        s['vec'], s['wa'], s['wx'], s['bias'] = mixer_consts[l]
        ob, s['hseq'] = _lru_fwd(z, s['cw'], s['vec'], s['wa'], s['wx'], tb)
        oc = _sgu_fwd(z, w['sgu_w'][l], s['bias'], row(w['sgu_norm'][l]), tb)
        s['oa'], s['ob'], s['oc'] = oa, ob, oc
        h = _outproj_fwd(h, oa, ob, oc, wout, tm_f)
        s['h2'] = h
        lands, tok = next_weights(h)
        s['ffn2'] = [a.reshape(FF, D) for a in lands]
        h, s['xn2'], s['a2'], s['b2'] = _ffn_fwd(h, row(w['ffn2_norm'][l]), *s['ffn2'], tm_f, tok)
        saved.append(s)

    dh, loss_part, g_final = _head(h, row(w['final_norm']), target, tm_f)
    loss = lax.psum(loss_part[0, 0], ("x", "y", "c"))

    recv = {k: [None] * depth for k in ('wg1', 'wu1', 'wd1', 'wg2', 'wu2', 'wd2', 'win', 'wout')}
    flight = []

    def land(after):
        handle, kinds, l = flight.pop()
        for k, a in zip(kinds, _transfer_wait(handle, after, f"exchange_wait_{kinds[0]}_{l}")):
            recv[k][l] = a

    def exchange(arrs, kinds, l, deps=()):
        handle = _transfer_start(arrs, False, f"exchange_start_{kinds[0]}_{l}", deps=deps)
        if flight:
            land(handle['token'])
        flight.append((handle, kinds, l))
        return handle['token']

    small = {k: [None] * depth for k in SMALL if k != 'final_norm'}
    dconv = [None] * depth
    dlb = [None] * depth
    tok = none
    for l in reversed(range(depth)):
        s = saved[l]
        dh, g, *cot = _ffn_bwd_x(dh, s['h2'], row(w['ffn2_norm'][l]), s['a2'], s['b2'], *s['ffn2'], tm_b, tok)
        dws = _ffn_bwd_w(s['xn2'], *cot, tm_b)
        tok = exchange([a.reshape(NDEV, FFS, D) for a in dws], ('wg2', 'wu2', 'wd2'), l)
        small['ffn2_norm'][l] = g
        doa, dob, doc, dwout = _outproj_bwd(dh, s['oa'], s['ob'], s['oc'], s['wout'], tm_f)
        dz, g_hn, dlb[l] = _hgrn_bwd(s['z'], row(lbs[l]), row(w['hgrn_norm'][l]), s['o'], s['states'], doa, tok)
        small['hgrn_norm'][l] = g_hn
        dz, dcw, dvec, dwa, dwx = _lru_bwd(s['z'], s['hseq'], dob, s['cw'], s['vec'], s['wa'], s['wx'], dz, tb)
        dconv[l] = dcw[:4]
        small['conv_b'][l], small['lru_ba'][l], small['lru_bx'][l] = dvec[0], dvec[1].reshape(4, GRP), dvec[2].reshape(4, GRP)
        small['lru_lambda'][l], small['lru_norm'][l] = dvec[3], dvec[4]
        small['lru_wa'][l], small['lru_wx'][l] = _diag_blocks(dwa), _diag_blocks(dwx)
        dz, dsw, dbias, dgc = _sgu_bwd(s['z'], doc, w['sgu_w'][l], s['bias'], row(w['sgu_norm'][l]), dz, tb)
        small['sgu_w'][l], small['sgu_b'][l], small['sgu_norm'][l] = dsw, dbias[:, ::GRP].T, dgc[0]
        dwin = _inproj_bwd_w(s['xnm'], dz, tm_f)
        tok = exchange([dwin.reshape(NDEV, DINS, D), dwout.reshape(NDEV, D // NDEV, D)], ('win', 'wout'), l)
        dh, g = _inproj_bwd_x(dh, dz, s['h1'], row(w['mix_norm'][l]), s['win'], tm_f, tok)
        small['mix_norm'][l] = g
        tok = none
        if l == 0:
            small['ffn1_norm'][0] = jnp.zeros((1, D), F32)
            small['hgrn_lb_logits'] = list(_lower_bounds_bwd(lb_soft, jnp.concatenate(dlb, axis=0)))
            parts = [jnp.stack([small[k][j].reshape(w[k].shape[1:]) for j in range(depth)])
                     for k in SMALL if k != 'final_norm']
            parts += [g_final.reshape(D), jnp.stack(dconv)]
            small_flight = _transfer_start([_pack(parts)], True, "gather_small_start", direct=True)
            tok = small_flight['token']
        dh, g, *cot = _ffn_bwd_x(dh, s['h0'], row(w['ffn1_norm'][l]), s['a1'], s['b1'], *s['ffn1'], tm_b, tok)
        dws = _ffn_bwd_w(s['xn1'], *cot, tm_b)
        before = ()
        if l == 0:
            g_last = _all_gather([g.reshape(8, 128)], "gather_last")[0]
            before = (g_last,)
        else:
            small['ffn1_norm'][l] = g
        tok = exchange([a.reshape(NDEV, FFS, D) for a in dws], ('wg1', 'wu1', 'wd1'), l, before)
    grad_x = dh.reshape(1, t, D)

    out = {}
    last = flight[0][0]['token']

    def ffn_update(f, n, after):
        for kind in ('wg', 'wu'):
            k = f + '_' + kind
            res = _adamw_big(recv[kind + n], tr(w[k]), tr(m[k]), tr(v[k]), 32, "adamw_ffn", after)
            out[k] = tuple(tr(a) for a in res)
        k = f + '_wd'
        out[k] = _adamw_big(recv['wd' + n], w[k], m[k], v[k], 32, "adamw_ffn", after)

    ffn_update('ffn2', '2', last)
    out['w_in'] = _adamw_big(recv['win'], w['w_in'], m['w_in'], v['w_in'], 128, "adamw_win", last, transposed=True)
    out['w_out'] = _adamw_big(recv['wout'], w['w_out'], m['w_out'], v['w_out'], 64, "adamw_wout", last)

    total = _sum_devices(_transfer_wait(small_flight, g_last, "gather_small_wait")[0])
    like = [w[k] for k in SMALL] + [jax.ShapeDtypeStruct((depth, 4, DB), F32)]
    grads = _unpack(total, like)
    gsmall = dict(zip(SMALL, grads[:-1]))
    gsmall['ffn1_norm'] = gsmall['ffn1_norm'].at[0].set(_sum_devices(g_last).reshape(D))
    gsmall['conv_w'] = lax.dynamic_slice_in_dim(grads[-1], my * (DB // NDEV), DB // NDEV, axis=2)
    keys = SMALL + ['conv_w']
    dl, mm, vv = _adamw_small(_pack([w[k] for k in keys]), _pack([gsmall[k] for k in keys]),
                              _pack([m[k] for k in keys]), _pack([v[k] for k in keys]))
    like = [w[k] for k in keys]
    for k, d_, m_, v_ in zip(keys, _unpack(dl, like), _unpack(mm, like), _unpack(vv, like)):
        out[k] = (gsmall[k], d_, m_, v_)
    done = [dl] + [out[k][1][0] for k in ('ffn2_wg', 'ffn2_wu', 'ffn2_wd', 'w_in', 'w_out')]
    land(functools.reduce(lambda p, q: p + q, [a[:1, :1] for a in done]))
    ffn_update('ffn1', '1', last)

    return (loss, grad_x, *[out[k][0] for k in NAMES], *[out[k][1] for k in NAMES],
            *[out[k][2] for k in NAMES], *[out[k][3] for k in NAMES])


def kernel(x, ffn1_norm, ffn1_wg, ffn1_wu, ffn1_wd, mix_norm, w_in, hgrn_lb_logits, hgrn_norm, conv_w, conv_b, lru_wa, lru_ba, lru_wx, lru_bx, lru_lambda, lru_norm, sgu_w, sgu_b, sgu_norm, w_out, ffn2_norm, ffn2_wg, ffn2_wu, ffn2_wd, final_norm, loss_target, m_ffn1_norm, m_ffn1_wg, m_ffn1_wu, m_ffn1_wd, m_mix_norm, m_w_in, m_hgrn_lb_logits, m_hgrn_norm, m_conv_w, m_conv_b, m_lru_wa, m_lru_ba, m_lru_wx, m_lru_bx, m_lru_lambda, m_lru_norm, m_sgu_w, m_sgu_b, m_sgu_norm, m_w_out, m_ffn2_norm, m_ffn2_wg, m_ffn2_wu, m_ffn2_wd, m_final_norm, v_ffn1_norm, v_ffn1_wg, v_ffn1_wu, v_ffn1_wd, v_mix_norm, v_w_in, v_hgrn_lb_logits, v_hgrn_norm, v_conv_w, v_conv_b, v_lru_wa, v_lru_ba, v_lru_wx, v_lru_bx, v_lru_lambda, v_lru_norm, v_sgu_w, v_sgu_b, v_sgu_norm, v_w_out, v_ffn2_norm, v_ffn2_wg, v_ffn2_wu, v_ffn2_wd, v_final_norm):
    args = locals()
    w = {k: args[k] for k in NAMES}
    m = {k: args['m_' + k] for k in NAMES}
    v = {k: args['v_' + k] for k in NAMES}
    return _step(x, loss_target, w, m, v)
```
